```python
import math
import jax, jax.numpy as jnp
from jax import lax
import numpy as np

D_MODEL = 1024
BATCH = 32
SEQ = 2048
DEPTH = 2

MEM_LEN = 256
CONV_CH = 512
CONV_K = 31
SG_CH = 512
SG_GROUPS = 4
SG_CHUNK = 128
HEAD_DIM = 64
HEADS_PER_GROUP = 4
DIL_GROUPS = ((128, 1), (512, 4), (2048, 16))
ATT_HEADS = HEADS_PER_GROUP * len(DIL_GROUPS)
ATT_BLOCK = 128
REL_BUCKETS = 32
REL_MAX_DIST = 2048
N_BRANCH = 3
X_HEADS = 4
X_HEAD_DIM = D_MODEL // X_HEADS
D_FF = 2816
FFN_CONV_K = 3
NORM_EPS = 1e-6
LN_EPS = 1e-5

COL_A = 2 * CONV_CH
COL_B = 2 * SG_CH
COL_C = 3 * ATT_HEADS * HEAD_DIM
COL_G = N_BRANCH * D_MODEL
OFF_B = COL_A
OFF_C = COL_A + COL_B
OFF_G = COL_A + COL_B + COL_C
IN_COLS = COL_A + COL_B + COL_C + COL_G
ATT_OUT = HEADS_PER_GROUP * HEAD_DIM

kernel_name = "hybrid_conv_sgmlp_dilated_attn_block"


def rms_norm(x, g):
    xf = x.astype(jnp.float32)
    y = xf * lax.rsqrt(jnp.mean(xf * xf, axis=-1, keepdims=True) + NORM_EPS)
    return (y * g.astype(jnp.float32)).astype(x.dtype)


def layer_norm(x, g, b):
    xf = x.astype(jnp.float32)
    mu = jnp.mean(xf, axis=-1, keepdims=True)
    var = jnp.mean(jnp.square(xf - mu), axis=-1, keepdims=True)
    y = (xf - mu) * lax.rsqrt(var + LN_EPS)
    return (y * g.astype(jnp.float32) + b.astype(jnp.float32)).astype(x.dtype)


def causal_dwconv(x, w, b):
    k, c = w.shape
    y = lax.conv_general_dilated(x, w[:, None, :], window_strides=(1,), padding=[(k - 1, 0)],
                                 dimension_numbers=('NWC', 'WIO', 'NWC'), feature_group_count=c)
    return y + b


def t5_bucket(dist):
    n = jnp.maximum(dist, 0)
    max_exact = REL_BUCKETS // 2
    nf = jnp.maximum(n, 1).astype(jnp.float32)
    large = max_exact + (jnp.log(nf / max_exact) / math.log(REL_MAX_DIST / max_exact)
                         * (REL_BUCKETS - max_exact)).astype(jnp.int32)
    large = jnp.minimum(large, REL_BUCKETS - 1)
    return jnp.where(n < max_exact, n, large)


def dilated_window_attention(q, k, v, rel_table, dilation, window):
    B, S, H, E = q.shape
    span = window // dilation
    unit = dilation * ATT_BLOCK
    s_pad = -(-S // unit) * unit
    L = s_pad // dilation
    nb = L // ATT_BLOCK

    def to_blocks(t):
        t = jnp.pad(t, ((0, 0), (0, s_pad - S), (0, 0), (0, 0)))
        t = t.reshape(B, L, dilation, H, E).swapaxes(1, 2)
        return t.reshape(B, dilation, nb, ATT_BLOCK, H, E)

    def with_prev(t):
        prev = jnp.pad(t, ((0, 0), (0, 0), (1, 0), (0, 0), (0, 0), (0, 0)))[:, :, :-1]
        return jnp.concatenate([prev, t], axis=3)

    qb = to_blocks(q)
    kk = with_prev(to_blocks(k))
    vv = with_prev(to_blocks(v))

    qi = jnp.arange(ATT_BLOCK)[:, None]
    ki = jnp.arange(2 * ATT_BLOCK)[None, :]
    rel = qi + ATT_BLOCK - ki
    bias = rel_table[t5_bucket(rel * dilation)].transpose(2, 0, 1).astype(jnp.float32)
    blk_idx = jnp.arange(nb)[:, None, None]
    valid = (rel >= 0) & (rel <= span) & ((blk_idx > 0) | (ki >= ATT_BLOCK))

    scores = jnp.einsum('bcnqhe,bcnkhe->bcnhqk', qb, kk).astype(jnp.float32) * (E ** -0.5) + bias
    scores = jnp.where(valid[:, None], scores, -jnp.inf)
    m = jnp.max(scores, axis=-1, keepdims=True)
    e = jnp.exp(scores - m)
    s = jnp.sum(e, axis=-1, keepdims=True)
    o = jnp.einsum('bcnhqk,bcnkhe->bcnqhe', e, vv.astype(jnp.float32)) / s.swapaxes(3, 4)
    lse = (m + jnp.log(s))[..., 0].swapaxes(3, 4)

    def from_blocks(t):
        tail = t.shape[4:]
        t = t.reshape(B, dilation, L, *tail).swapaxes(1, 2).reshape(B, s_pad, *tail)
        return t[:, :S]

    return from_blocks(o), from_blocks(lse)


def parallel_mixer(h, w_in, b_gate, conv_a_w, conv_a_b, ln_a_g, ln_a_b, w_a_out,
                   ln_b_g, ln_b_b, w_s, b_s, w_b_out, rel_bias, w_c_out, w_mix_out):
    B, S, _ = h.shape
    za = h @ w_in[:, :COL_A]
    a_val, a_gate = jnp.split(za, 2, axis=-1)
    a = a_val * jax.nn.sigmoid(a_gate)
    a = causal_dwconv(a, conv_a_w, conv_a_b)
    a = jax.nn.silu(layer_norm(a, ln_a_g, ln_a_b))
    y_a = a @ w_a_out
    zb = jax.nn.gelu(h @ w_in[:, OFF_B:OFF_C], approximate=True)
    u, v = jnp.split(zb, 2, axis=-1)
    v = layer_norm(v, ln_b_g, ln_b_b)
    v = v.reshape(B, S // SG_CHUNK, SG_CHUNK, SG_GROUPS, SG_CH // SG_GROUPS)
    causal = jnp.tril(jnp.ones((SG_CHUNK, SG_CHUNK), dtype=bool))
    ws = jnp.where(causal, w_s, 0.0)
    v = jnp.einsum('gts,bnsgc->bntgc', ws, v) + b_s.T[:, :, None]
    y_b = (u * v.reshape(B, S, SG_CH)) @ w_b_out
    zc = h @ w_in[:, OFF_C:OFF_G]
    q, k, vc = [t.reshape(B, S, ATT_HEADS, HEAD_DIM) for t in jnp.split(zc, 3, axis=-1)]
    outs, lses = [], []
    for gi, (window, dil) in enumerate(DIL_GROUPS):
        sl = slice(gi * HEADS_PER_GROUP, (gi + 1) * HEADS_PER_GROUP)
        o, l = dilated_window_attention(q[:, :, sl], k[:, :, sl], vc[:, :, sl], rel_bias[:, sl], dil, window)
        outs.append(o)
        lses.append(l)
    wts = jax.nn.softmax(jnp.stack(lses, axis=0), axis=0)
    oc = jnp.sum(wts[..., None] * jnp.stack(outs, axis=0), axis=0)
    y_c = oc.reshape(B, S, ATT_OUT).astype(h.dtype) @ w_c_out
    zg = (h @ w_in[:, OFF_G:]).reshape(B, S, N_BRANCH, D_MODEL) + b_gate
    g = jax.nn.sigmoid(zg)
    merged = g[:, :, 0] * y_a + g[:, :, 1] * y_b + g[:, :, 2] * y_c
    return merged @ w_mix_out


def memory_cross_attention(h, mem_n, w_xq, w_xkv, w_xo):
    B, S, _ = h.shape
    M = mem_n.shape[1]
    q = (h @ w_xq).reshape(B, S, X_HEADS, X_HEAD_DIM)
    k, v = [t.reshape(B, M, X_HEADS, X_HEAD_DIM) for t in jnp.split(mem_n @ w_xkv, 2, axis=-1)]
    s = jnp.einsum('bshe,bmhe->bhsm', q, k).astype(jnp.float32) * (X_HEAD_DIM ** -0.5)
    p = jax.nn.softmax(s, axis=-1)
    o = jnp.einsum('bhsm,bmhe->bshe', p, v.astype(jnp.float32)).reshape(B, S, D_MODEL)
    return o.astype(h.dtype) @ w_xo


def conv_ffn(h, w_up, conv_f_w, conv_f_b, w_down):
    gate = causal_dwconv(h @ w_up[:, :D_FF], conv_f_w, conv_f_b)
    val = h @ w_up[:, D_FF:]
    return (jax.nn.gelu(gate, approximate=True) * val) @ w_down


def _fwd_setup_inputs(seed: int = 0) -> dict:
    key = jax.random.key(seed)
    ks = iter(jax.random.split(key, 40))

    def nrm(shape, scale):
        return jax.random.normal(next(ks), shape, jnp.float32) * scale

    def gain(shape):
        return 1.0 + nrm(shape, 0.05)

    L = DEPTH
    return {
        "x": nrm((BATCH, SEQ, D_MODEL), 1.0),
        "mem": nrm((BATCH, MEM_LEN, D_MODEL), 1.0),
        "rel_bias": nrm((REL_BUCKETS, ATT_HEADS), 0.5),
        "mix_pre_g": gain((L, D_MODEL)),
        "mix_post_g": gain((L, D_MODEL)),
        "w_in": nrm((L, D_MODEL, IN_COLS), D_MODEL ** -0.5),
        "b_gate": nrm((L, N_BRANCH, D_MODEL), 0.02),
        "conv_a_w": nrm((L, CONV_K, CONV_CH), CONV_K ** -0.5),
        "conv_a_b": nrm((L, CONV_CH), 0.02),
        "ln_a_g": gain((L, CONV_CH)),
        "ln_a_b": nrm((L, CONV_CH), 0.02),
        "w_a_out": nrm((L, CONV_CH, D_MODEL), CONV_CH ** -0.5),
        "ln_b_g": gain((L, SG_CH)),
        "ln_b_b": nrm((L, SG_CH), 0.02),
        "w_s": nrm((L, SG_GROUPS, SG_CHUNK, SG_CHUNK), SG_CHUNK ** -0.5),
        "b_s": 1.0 + nrm((L, SG_GROUPS, SG_CHUNK), 0.05),
        "w_b_out": nrm((L, SG_CH, D_MODEL), SG_CH ** -0.5),
        "w_c_out": nrm((L, ATT_OUT, D_MODEL), ATT_OUT ** -0.5),
        "w_mix_out": nrm((L, D_MODEL, D_MODEL), D_MODEL ** -0.5),
        "x_pre_g": gain((L, D_MODEL)),
        "x_post_g": gain((L, D_MODEL)),
        "mem_g": gain((L, D_MODEL)),
        "w_xq": nrm((L, D_MODEL, D_MODEL), D_MODEL ** -0.5),
        "w_xkv": nrm((L, D_MODEL, 2 * D_MODEL), D_MODEL ** -0.5),
        "w_xo": nrm((L, D_MODEL, D_MODEL), D_MODEL ** -0.5),
        "ffn_pre_g": gain((L, D_MODEL)),
        "ffn_post_g": gain((L, D_MODEL)),
        "w_up": nrm((L, D_MODEL, 2 * D_FF), D_MODEL ** -0.5),
        "conv_f_w": nrm((L, FFN_CONV_K, D_FF), FFN_CONV_K ** -0.5),
        "conv_f_b": nrm((L, D_FF), 0.02),
        "w_down": nrm((L, D_FF, D_MODEL), D_FF ** -0.5),
    }


def _fwd_reference(x, mem, rel_bias, mix_pre_g, mix_post_g, w_in, b_gate, conv_a_w, conv_a_b,
              ln_a_g, ln_a_b, w_a_out, ln_b_g, ln_b_b, w_s, b_s, w_b_out, w_c_out, w_mix_out,
              x_pre_g, x_post_g, mem_g, w_xq, w_xkv, w_xo,
              ffn_pre_g, ffn_post_g, w_up, conv_f_w, conv_f_b, w_down):
    for l in range(DEPTH):
        h = rms_norm(x, mix_pre_g[l])
        y = parallel_mixer(h, w_in[l], b_gate[l], conv_a_w[l], conv_a_b[l], ln_a_g[l], ln_a_b[l],
                           w_a_out[l], ln_b_g[l], ln_b_b[l], w_s[l], b_s[l], w_b_out[l],
                           rel_bias, w_c_out[l], w_mix_out[l])
        x = x + rms_norm(y, mix_post_g[l])

        h = rms_norm(x, x_pre_g[l])
        y = memory_cross_attention(h, rms_norm(mem, mem_g[l]), w_xq[l], w_xkv[l], w_xo[l])
        x = x + rms_norm(y, x_post_g[l])

        h = rms_norm(x, ffn_pre_g[l])
        y = conv_ffn(h, w_up[l], conv_f_w[l], conv_f_b[l], w_down[l])
        x = x + rms_norm(y, ffn_post_g[l])
    return x


import jax as _jax
import jax.numpy as _jnp

TWIN_FORMAT = 'train_step'
FWD_PARAMS = ['x', 'mem', 'rel_bias', 'mix_pre_g', 'mix_post_g', 'w_in', 'b_gate', 'conv_a_w', 'conv_a_b', 'ln_a_g', 'ln_a_b', 'w_a_out', 'ln_b_g', 'ln_b_b', 'w_s', 'b_s', 'w_b_out', 'w_c_out', 'w_mix_out', 'x_pre_g', 'x_post_g', 'mem_g', 'w_xq', 'w_xkv', 'w_xo', 'ffn_pre_g', 'ffn_post_g', 'w_up', 'conv_f_w', 'conv_f_b', 'w_down']
TWIN_WEIGHTS = ['rel_bias', 'mix_pre_g', 'mix_post_g', 'w_in', 'b_gate', 'conv_a_w', 'conv_a_b', 'ln_a_g', 'ln_a_b', 'w_a_out', 'ln_b_g', 'ln_b_b', 'w_s', 'b_s', 'w_b_out', 'w_c_out', 'w_mix_out', 'x_pre_g', 'x_post_g', 'mem_g', 'w_xq', 'w_xkv', 'w_xo', 'ffn_pre_g', 'ffn_post_g', 'w_up', 'conv_f_w', 'conv_f_b', 'w_down']
TWIN_DIFF_INPUT = 'x'
TWIN_INPUTS = ['x', 'mem', 'rel_bias', 'mix_pre_g', 'mix_post_g', 'w_in', 'b_gate', 'conv_a_w', 'conv_a_b', 'ln_a_g', 'ln_a_b', 'w_a_out', 'ln_b_g', 'ln_b_b', 'w_s', 'b_s', 'w_b_out', 'w_c_out', 'w_mix_out', 'x_pre_g', 'x_post_g', 'mem_g', 'w_xq', 'w_xkv', 'w_xo', 'ffn_pre_g', 'ffn_post_g', 'w_up', 'conv_f_w', 'conv_f_b', 'w_down', 'loss_target', 'm_rel_bias', 'm_mix_pre_g', 'm_mix_post_g', 'm_w_in', 'm_b_gate', 'm_conv_a_w', 'm_conv_a_b', 'm_ln_a_g', 'm_ln_a_b', 'm_w_a_out', 'm_ln_b_g', 'm_ln_b_b', 'm_w_s', 'm_b_s', 'm_w_b_out', 'm_w_c_out', 'm_w_mix_out', 'm_x_pre_g', 'm_x_post_g', 'm_mem_g', 'm_w_xq', 'm_w_xkv', 'm_w_xo', 'm_ffn_pre_g', 'm_ffn_post_g', 'm_w_up', 'm_conv_f_w', 'm_conv_f_b', 'm_w_down', 'v_rel_bias', 'v_mix_pre_g', 'v_mix_post_g', 'v_w_in', 'v_b_gate', 'v_conv_a_w', 'v_conv_a_b', 'v_ln_a_g', 'v_ln_a_b', 'v_w_a_out', 'v_ln_b_g', 'v_ln_b_b', 'v_w_s', 'v_b_s', 'v_w_b_out', 'v_w_c_out', 'v_w_mix_out', 'v_x_pre_g', 'v_x_post_g', 'v_mem_g', 'v_w_xq', 'v_w_xkv', 'v_w_xo', 'v_ffn_pre_g', 'v_ffn_post_g', 'v_w_up', 'v_conv_f_w', 'v_conv_f_b', 'v_w_down']
TWIN_OUTPUTS = ['loss', 'grad_x', 'grad_rel_bias', 'grad_mix_pre_g', 'grad_mix_post_g', 'grad_w_in', 'grad_b_gate', 'grad_conv_a_w', 'grad_conv_a_b', 'grad_ln_a_g', 'grad_ln_a_b', 'grad_w_a_out', 'grad_ln_b_g', 'grad_ln_b_b', 'grad_w_s', 'grad_b_s', 'grad_w_b_out', 'grad_w_c_out', 'grad_w_mix_out', 'grad_x_pre_g', 'grad_x_post_g', 'grad_mem_g', 'grad_w_xq', 'grad_w_xkv', 'grad_w_xo', 'grad_ffn_pre_g', 'grad_ffn_post_g', 'grad_w_up', 'grad_conv_f_w', 'grad_conv_f_b', 'grad_w_down', 'delta_rel_bias', 'delta_mix_pre_g', 'delta_mix_post_g', 'delta_w_in', 'delta_b_gate', 'delta_conv_a_w', 'delta_conv_a_b', 'delta_ln_a_g', 'delta_ln_a_b', 'delta_w_a_out', 'delta_ln_b_g', 'delta_ln_b_b', 'delta_w_s', 'delta_b_s', 'delta_w_b_out', 'delta_w_c_out', 'delta_w_mix_out', 'delta_x_pre_g', 'delta_x_post_g', 'delta_mem_g', 'delta_w_xq', 'delta_w_xkv', 'delta_w_xo', 'delta_ffn_pre_g', 'delta_ffn_post_g', 'delta_w_up', 'delta_conv_f_w', 'delta_conv_f_b', 'delta_w_down', 'new_m_rel_bias', 'new_m_mix_pre_g', 'new_m_mix_post_g', 'new_m_w_in', 'new_m_b_gate', 'new_m_conv_a_w', 'new_m_conv_a_b', 'new_m_ln_a_g', 'new_m_ln_a_b', 'new_m_w_a_out', 'new_m_ln_b_g', 'new_m_ln_b_b', 'new_m_w_s', 'new_m_b_s', 'new_m_w_b_out', 'new_m_w_c_out', 'new_m_w_mix_out', 'new_m_x_pre_g', 'new_m_x_post_g', 'new_m_mem_g', 'new_m_w_xq', 'new_m_w_xkv', 'new_m_w_xo', 'new_m_ffn_pre_g', 'new_m_ffn_post_g', 'new_m_w_up', 'new_m_conv_f_w', 'new_m_conv_f_b', 'new_m_w_down', 'new_v_rel_bias', 'new_v_mix_pre_g', 'new_v_mix_post_g', 'new_v_w_in', 'new_v_b_gate', 'new_v_conv_a_w', 'new_v_conv_a_b', 'new_v_ln_a_g', 'new_v_ln_a_b', 'new_v_w_a_out', 'new_v_ln_b_g', 'new_v_ln_b_b', 'new_v_w_s', 'new_v_b_s', 'new_v_w_b_out', 'new_v_w_c_out', 'new_v_w_mix_out', 'new_v_x_pre_g', 'new_v_x_post_g', 'new_v_mem_g', 'new_v_w_xq', 'new_v_w_xkv', 'new_v_w_xo', 'new_v_ffn_pre_g', 'new_v_ffn_post_g', 'new_v_w_up', 'new_v_conv_f_w', 'new_v_conv_f_b', 'new_v_w_down']
TWIN_LEAF_KINDS = {'loss': 'loss', 'grad_x': 'grad_x', 'grad_rel_bias': 'grad_w', 'grad_mix_pre_g': 'grad_w', 'grad_mix_post_g': 'grad_w', 'grad_w_in': 'grad_w', 'grad_b_gate': 'grad_w', 'grad_conv_a_w': 'grad_w', 'grad_conv_a_b': 'grad_w', 'grad_ln_a_g': 'grad_w', 'grad_ln_a_b': 'grad_w', 'grad_w_a_out': 'grad_w', 'grad_ln_b_g': 'grad_w', 'grad_ln_b_b': 'grad_w', 'grad_w_s': 'grad_w', 'grad_b_s': 'grad_w', 'grad_w_b_out': 'grad_w', 'grad_w_c_out': 'grad_w', 'grad_w_mix_out': 'grad_w', 'grad_x_pre_g': 'grad_w', 'grad_x_post_g': 'grad_w', 'grad_mem_g': 'grad_w', 'grad_w_xq': 'grad_w', 'grad_w_xkv': 'grad_w', 'grad_w_xo': 'grad_w', 'grad_ffn_pre_g': 'grad_w', 'grad_ffn_post_g': 'grad_w', 'grad_w_up': 'grad_w', 'grad_conv_f_w': 'grad_w', 'grad_conv_f_b': 'grad_w', 'grad_w_down': 'grad_w', 'delta_rel_bias': 'delta_w', 'delta_mix_pre_g': 'delta_w', 'delta_mix_post_g': 'delta_w', 'delta_w_in': 'delta_w', 'delta_b_gate': 'delta_w', 'delta_conv_a_w': 'delta_w', 'delta_conv_a_b': 'delta_w', 'delta_ln_a_g': 'delta_w', 'delta_ln_a_b': 'delta_w', 'delta_w_a_out': 'delta_w', 'delta_ln_b_g': 'delta_w', 'delta_ln_b_b': 'delta_w', 'delta_w_s': 'delta_w', 'delta_b_s': 'delta_w', 'delta_w_b_out': 'delta_w', 'delta_w_c_out': 'delta_w', 'delta_w_mix_out': 'delta_w', 'delta_x_pre_g': 'delta_w', 'delta_x_post_g': 'delta_w', 'delta_mem_g': 'delta_w', 'delta_w_xq': 'delta_w', 'delta_w_xkv': 'delta_w', 'delta_w_xo': 'delta_w', 'delta_ffn_pre_g': 'delta_w', 'delta_ffn_post_g': 'delta_w', 'delta_w_up': 'delta_w', 'delta_conv_f_w': 'delta_w', 'delta_conv_f_b': 'delta_w', 'delta_w_down': 'delta_w', 'new_m_rel_bias': 'new_m', 'new_m_mix_pre_g': 'new_m', 'new_m_mix_post_g': 'new_m', 'new_m_w_in': 'new_m', 'new_m_b_gate': 'new_m', 'new_m_conv_a_w': 'new_m', 'new_m_conv_a_b': 'new_m', 'new_m_ln_a_g': 'new_m', 'new_m_ln_a_b': 'new_m', 'new_m_w_a_out': 'new_m', 'new_m_ln_b_g': 'new_m', 'new_m_ln_b_b': 'new_m', 'new_m_w_s': 'new_m', 'new_m_b_s': 'new_m', 'new_m_w_b_out': 'new_m', 'new_m_w_c_out': 'new_m', 'new_m_w_mix_out': 'new_m', 'new_m_x_pre_g': 'new_m', 'new_m_x_post_g': 'new_m', 'new_m_mem_g': 'new_m', 'new_m_w_xq': 'new_m', 'new_m_w_xkv': 'new_m', 'new_m_w_xo': 'new_m', 'new_m_ffn_pre_g': 'new_m', 'new_m_ffn_post_g': 'new_m', 'new_m_w_up': 'new_m', 'new_m_conv_f_w': 'new_m', 'new_m_conv_f_b': 'new_m', 'new_m_w_down': 'new_m', 'new_v_rel_bias': 'new_v', 'new_v_mix_pre_g': 'new_v', 'new_v_mix_post_g': 'new_v', 'new_v_w_in': 'new_v', 'new_v_b_gate': 'new_v', 'new_v_conv_a_w': 'new_v', 'new_v_conv_a_b': 'new_v', 'new_v_ln_a_g': 'new_v', 'new_v_ln_a_b': 'new_v', 'new_v_w_a_out': 'new_v', 'new_v_ln_b_g': 'new_v', 'new_v_ln_b_b': 'new_v', 'new_v_w_s': 'new_v', 'new_v_b_s': 'new_v', 'new_v_w_b_out': 'new_v', 'new_v_w_c_out': 'new_v', 'new_v_w_mix_out': 'new_v', 'new_v_x_pre_g': 'new_v', 'new_v_x_post_g': 'new_v', 'new_v_mem_g': 'new_v', 'new_v_w_xq': 'new_v', 'new_v_w_xkv': 'new_v', 'new_v_w_xo': 'new_v', 'new_v_ffn_pre_g': 'new_v', 'new_v_ffn_post_g': 'new_v', 'new_v_w_up': 'new_v', 'new_v_conv_f_w': 'new_v', 'new_v_conv_f_b': 'new_v', 'new_v_w_down': 'new_v'}


def _forward(args):
    return _fwd_reference(*[args[k] for k in FWD_PARAMS])


def _output_shape():
    out = _jax.eval_shape(lambda: _forward(_fwd_setup_inputs(0)))
    return out.shape, out.dtype

N_MICROBATCH = 1
ADAM_LR = 0.001
ADAM_B1 = 0.9
ADAM_B2 = 0.999
ADAM_EPS = 1e-08
ADAM_WD = 0.01
ADAM_STEP = 10
PER_EXAMPLE_BATCH_AXIS = {'x': 0, 'mem': 0, 'loss_target': 0}
SHARED_INPUTS = []
_WEIGHT_DTYPES = {'rel_bias': _jnp.float32, 'mix_pre_g': _jnp.float32, 'mix_post_g': _jnp.float32, 'w_in': _jnp.float32, 'b_gate': _jnp.float32, 'conv_a_w': _jnp.float32, 'conv_a_b': _jnp.float32, 'ln_a_g': _jnp.float32, 'ln_a_b': _jnp.float32, 'w_a_out': _jnp.float32, 'ln_b_g': _jnp.float32, 'ln_b_b': _jnp.float32, 'w_s': _jnp.float32, 'b_s': _jnp.float32, 'w_b_out': _jnp.float32, 'w_c_out': _jnp.float32, 'w_mix_out': _jnp.float32, 'x_pre_g': _jnp.float32, 'x_post_g': _jnp.float32, 'mem_g': _jnp.float32, 'w_xq': _jnp.float32, 'w_xkv': _jnp.float32, 'w_xo': _jnp.float32, 'ffn_pre_g': _jnp.float32, 'ffn_post_g': _jnp.float32, 'w_up': _jnp.float32, 'conv_f_w': _jnp.float32, 'conv_f_b': _jnp.float32, 'w_down': _jnp.float32}
MOMENT_SCALE = {'rel_bias': 2.489183e+00, 'mix_pre_g': 5.027776e+00, 'mix_post_g': 6.702373e+01, 'w_in': 1.801339e+00, 'b_gate': 3.291929e+00, 'conv_a_w': 4.238741e+00, 'conv_a_b': 4.502657e+01, 'ln_a_g': 1.759453e+01, 'ln_a_b': 2.715073e+01, 'w_a_out': 8.382870e+00, 'ln_b_g': 9.270124e-01, 'ln_b_b': 7.680480e-01, 'w_s': 8.308486e-01, 'b_s': 1.222700e+00, 'w_b_out': 1.107354e+01, 'w_c_out': 3.498740e+00, 'w_mix_out': 1.428608e+01, 'x_pre_g': 5.326905e+00, 'x_post_g': 7.008301e+01, 'mem_g': 2.138187e+01, 'w_xq': 5.388189e+00, 'w_xkv': 1.454610e+01, 'w_xo': 1.995699e+01, 'ffn_pre_g': 7.945055e+00, 'ffn_post_g': 6.446646e+01, 'w_up': 3.430588e+00, 'conv_f_w': 3.475455e+00, 'conv_f_b': 7.146506e+00, 'w_down': 7.625490e+00}


def _to_microbatches(a, axis):
    t = _jnp.moveaxis(a, axis, 0)
    t = t.reshape((N_MICROBATCH, t.shape[0] // N_MICROBATCH) + t.shape[1:])
    return _jnp.moveaxis(t, 1, axis + 1)


def setup_inputs(seed: int = 0) -> dict:
    inp = _fwd_setup_inputs(seed)
    key = _jax.random.fold_in(_jax.random.key(seed), 7919)
    shape, _ = _output_shape()
    out = dict(inp)
    out["loss_target"] = _jax.random.normal(_jax.random.fold_in(key, 0), shape, _jnp.float32)
    for i, name in enumerate(TWIN_WEIGHTS):
        w = inp[name].astype(_jnp.float32)
        if MOMENT_SCALE is None:
            s = _jnp.sqrt(_jnp.mean(_jnp.square(w)) + 1e-30)
        else:
            s = MOMENT_SCALE[name]
        km, kv = _jax.random.split(_jax.random.fold_in(key, i + 1))
        out[name] = w
        out["m_" + name] = s * _jax.random.normal(km, w.shape, _jnp.float32)
        out["v_" + name] = (s * s) * _jax.random.uniform(kv, w.shape, _jnp.float32, 0.5, 1.5)
    if N_MICROBATCH > 1:
        for name, axis in PER_EXAMPLE_BATCH_AXIS.items():
            out[name] = _to_microbatches(out[name], axis)
    return {'x': out['x'], 'mem': out['mem'], 'rel_bias': out['rel_bias'], 'mix_pre_g': out['mix_pre_g'], 'mix_post_g': out['mix_post_g'], 'w_in': out['w_in'], 'b_gate': out['b_gate'], 'conv_a_w': out['conv_a_w'], 'conv_a_b': out['conv_a_b'], 'ln_a_g': out['ln_a_g'], 'ln_a_b': out['ln_a_b'], 'w_a_out': out['w_a_out'], 'ln_b_g': out['ln_b_g'], 'ln_b_b': out['ln_b_b'], 'w_s': out['w_s'], 'b_s': out['b_s'], 'w_b_out': out['w_b_out'], 'w_c_out': out['w_c_out'], 'w_mix_out': out['w_mix_out'], 'x_pre_g': out['x_pre_g'], 'x_post_g': out['x_post_g'], 'mem_g': out['mem_g'], 'w_xq': out['w_xq'], 'w_xkv': out['w_xkv'], 'w_xo': out['w_xo'], 'ffn_pre_g': out['ffn_pre_g'], 'ffn_post_g': out['ffn_post_g'], 'w_up': out['w_up'], 'conv_f_w': out['conv_f_w'], 'conv_f_b': out['conv_f_b'], 'w_down': out['w_down'], 'loss_target': out['loss_target'], 'm_rel_bias': out['m_rel_bias'], 'm_mix_pre_g': out['m_mix_pre_g'], 'm_mix_post_g': out['m_mix_post_g'], 'm_w_in': out['m_w_in'], 'm_b_gate': out['m_b_gate'], 'm_conv_a_w': out['m_conv_a_w'], 'm_conv_a_b': out['m_conv_a_b'], 'm_ln_a_g': out['m_ln_a_g'], 'm_ln_a_b': out['m_ln_a_b'], 'm_w_a_out': out['m_w_a_out'], 'm_ln_b_g': out['m_ln_b_g'], 'm_ln_b_b': out['m_ln_b_b'], 'm_w_s': out['m_w_s'], 'm_b_s': out['m_b_s'], 'm_w_b_out': out['m_w_b_out'], 'm_w_c_out': out['m_w_c_out'], 'm_w_mix_out': out['m_w_mix_out'], 'm_x_pre_g': out['m_x_pre_g'], 'm_x_post_g': out['m_x_post_g'], 'm_mem_g': out['m_mem_g'], 'm_w_xq': out['m_w_xq'], 'm_w_xkv': out['m_w_xkv'], 'm_w_xo': out['m_w_xo'], 'm_ffn_pre_g': out['m_ffn_pre_g'], 'm_ffn_post_g': out['m_ffn_post_g'], 'm_w_up': out['m_w_up'], 'm_conv_f_w': out['m_conv_f_w'], 'm_conv_f_b': out['m_conv_f_b'], 'm_w_down': out['m_w_down'], 'v_rel_bias': out['v_rel_bias'], 'v_mix_pre_g': out['v_mix_pre_g'], 'v_mix_post_g': out['v_mix_post_g'], 'v_w_in': out['v_w_in'], 'v_b_gate': out['v_b_gate'], 'v_conv_a_w': out['v_conv_a_w'], 'v_conv_a_b': out['v_conv_a_b'], 'v_ln_a_g': out['v_ln_a_g'], 'v_ln_a_b': out['v_ln_a_b'], 'v_w_a_out': out['v_w_a_out'], 'v_ln_b_g': out['v_ln_b_g'], 'v_ln_b_b': out['v_ln_b_b'], 'v_w_s': out['v_w_s'], 'v_b_s': out['v_b_s'], 'v_w_b_out': out['v_w_b_out'], 'v_w_c_out': out['v_w_c_out'], 'v_w_mix_out': out['v_w_mix_out'], 'v_x_pre_g': out['v_x_pre_g'], 'v_x_post_g': out['v_x_post_g'], 'v_mem_g': out['v_mem_g'], 'v_w_xq': out['v_w_xq'], 'v_w_xkv': out['v_w_xkv'], 'v_w_xo': out['v_w_xo'], 'v_ffn_pre_g': out['v_ffn_pre_g'], 'v_ffn_post_g': out['v_ffn_post_g'], 'v_w_up': out['v_w_up'], 'v_conv_f_w': out['v_conv_f_w'], 'v_conv_f_b': out['v_conv_f_b'], 'v_w_down': out['v_w_down']}


def _loss(weights, diff, rest, loss_target):
    with _jax.named_scope("forward"):
        args = {**rest, TWIN_DIFF_INPUT: diff, **{k: w.astype(_WEIGHT_DTYPES[k]) for k, w in weights.items()}}
        y = _forward(args)
    with _jax.named_scope("loss_head"):
        err = _jnp.square(y.astype(_jnp.float32) - loss_target)
        return 0.5 * _jnp.sum(_jnp.mean(err, axis=-1)) if err.ndim else 0.5 * err


def _adamw(w, g, m, v):
    m = ADAM_B1 * m + (1.0 - ADAM_B1) * g
    v = ADAM_B2 * v + (1.0 - ADAM_B2) * _jnp.square(g)
    m_hat = m / (1.0 - ADAM_B1 ** ADAM_STEP)
    v_hat = v / (1.0 - ADAM_B2 ** ADAM_STEP)
    delta = -ADAM_LR * (m_hat / (_jnp.sqrt(v_hat) + ADAM_EPS) + ADAM_WD * w)
    return delta, m, v


def reference(x, mem, rel_bias, mix_pre_g, mix_post_g, w_in, b_gate, conv_a_w, conv_a_b, ln_a_g, ln_a_b, w_a_out, ln_b_g, ln_b_b, w_s, b_s, w_b_out, w_c_out, w_mix_out, x_pre_g, x_post_g, mem_g, w_xq, w_xkv, w_xo, ffn_pre_g, ffn_post_g, w_up, conv_f_w, conv_f_b, w_down, loss_target, m_rel_bias, m_mix_pre_g, m_mix_post_g, m_w_in, m_b_gate, m_conv_a_w, m_conv_a_b, m_ln_a_g, m_ln_a_b, m_w_a_out, m_ln_b_g, m_ln_b_b, m_w_s, m_b_s, m_w_b_out, m_w_c_out, m_w_mix_out, m_x_pre_g, m_x_post_g, m_mem_g, m_w_xq, m_w_xkv, m_w_xo, m_ffn_pre_g, m_ffn_post_g, m_w_up, m_conv_f_w, m_conv_f_b, m_w_down, v_rel_bias, v_mix_pre_g, v_mix_post_g, v_w_in, v_b_gate, v_conv_a_w, v_conv_a_b, v_ln_a_g, v_ln_a_b, v_w_a_out, v_ln_b_g, v_ln_b_b, v_w_s, v_b_s, v_w_b_out, v_w_c_out, v_w_mix_out, v_x_pre_g, v_x_post_g, v_mem_g, v_w_xq, v_w_xkv, v_w_xo, v_ffn_pre_g, v_ffn_post_g, v_w_up, v_conv_f_w, v_conv_f_b, v_w_down):
    given = dict(x=x, mem=mem, rel_bias=rel_bias, mix_pre_g=mix_pre_g, mix_post_g=mix_post_g, w_in=w_in, b_gate=b_gate, conv_a_w=conv_a_w, conv_a_b=conv_a_b, ln_a_g=ln_a_g, ln_a_b=ln_a_b, w_a_out=w_a_out, ln_b_g=ln_b_g, ln_b_b=ln_b_b, w_s=w_s, b_s=b_s, w_b_out=w_b_out, w_c_out=w_c_out, w_mix_out=w_mix_out, x_pre_g=x_pre_g, x_post_g=x_post_g, mem_g=mem_g, w_xq=w_xq, w_xkv=w_xkv, w_xo=w_xo, ffn_pre_g=ffn_pre_g, ffn_post_g=ffn_post_g, w_up=w_up, conv_f_w=conv_f_w, conv_f_b=conv_f_b, w_down=w_down, loss_target=loss_target, m_rel_bias=m_rel_bias, m_mix_pre_g=m_mix_pre_g, m_mix_post_g=m_mix_post_g, m_w_in=m_w_in, m_b_gate=m_b_gate, m_conv_a_w=m_conv_a_w, m_conv_a_b=m_conv_a_b, m_ln_a_g=m_ln_a_g, m_ln_a_b=m_ln_a_b, m_w_a_out=m_w_a_out, m_ln_b_g=m_ln_b_g, m_ln_b_b=m_ln_b_b, m_w_s=m_w_s, m_b_s=m_b_s, m_w_b_out=m_w_b_out, m_w_c_out=m_w_c_out, m_w_mix_out=m_w_mix_out, m_x_pre_g=m_x_pre_g, m_x_post_g=m_x_post_g, m_mem_g=m_mem_g, m_w_xq=m_w_xq, m_w_xkv=m_w_xkv, m_w_xo=m_w_xo, m_ffn_pre_g=m_ffn_pre_g, m_ffn_post_g=m_ffn_post_g, m_w_up=m_w_up, m_conv_f_w=m_conv_f_w, m_conv_f_b=m_conv_f_b, m_w_down=m_w_down, v_rel_bias=v_rel_bias, v_mix_pre_g=v_mix_pre_g, v_mix_post_g=v_mix_post_g, v_w_in=v_w_in, v_b_gate=v_b_gate, v_conv_a_w=v_conv_a_w, v_conv_a_b=v_conv_a_b, v_ln_a_g=v_ln_a_g, v_ln_a_b=v_ln_a_b, v_w_a_out=v_w_a_out, v_ln_b_g=v_ln_b_g, v_ln_b_b=v_ln_b_b, v_w_s=v_w_s, v_b_s=v_b_s, v_w_b_out=v_w_b_out, v_w_c_out=v_w_c_out, v_w_mix_out=v_w_mix_out, v_x_pre_g=v_x_pre_g, v_x_post_g=v_x_post_g, v_mem_g=v_mem_g, v_w_xq=v_w_xq, v_w_xkv=v_w_xkv, v_w_xo=v_w_xo, v_ffn_pre_g=v_ffn_pre_g, v_ffn_post_g=v_ffn_post_g, v_w_up=v_w_up, v_conv_f_w=v_conv_f_w, v_conv_f_b=v_conv_f_b, v_w_down=v_w_down)
    weights = {n: given[n] for n in TWIN_WEIGHTS}
    shared = {n: given[n] for n in SHARED_INPUTS}
    per_example = {n: given[n] for n in ['x', 'mem']}
    grad_fn = _jax.value_and_grad(_loss, argnums=(0, 1))

    def one_microbatch(ex, loss_target):
        ex = dict(ex)
        diff = ex.pop(TWIN_DIFF_INPUT)
        return grad_fn(weights, diff, {**shared, **ex}, loss_target)

    if N_MICROBATCH == 1:
        loss, (grad_w, grad_x) = one_microbatch(per_example, given["loss_target"])
    else:
        def body(carry, xs):
            loss_sum, grad_sum = carry
            l_k, (gw_k, gx_k) = one_microbatch(xs[0], xs[1])
            with _jax.named_scope("update"):
                return (loss_sum + l_k, _jax.tree.map(_jnp.add, grad_sum, gw_k)), gx_k

        init = (_jnp.zeros((), _jnp.float32), _jax.tree.map(_jnp.zeros_like, weights))
        (loss, grad_w), grad_x = _jax.lax.scan(body, init, (per_example, given["loss_target"]))
    with _jax.named_scope("update"):
        delta_w, new_m, new_v = {}, {}, {}
        for n in TWIN_WEIGHTS:
            delta_w[n], new_m[n], new_v[n] = _adamw(weights[n], grad_w[n], given["m_" + n], given["v_" + n])
    return (loss, grad_x, *[grad_w[n] for n in TWIN_WEIGHTS], *[delta_w[n] for n in TWIN_WEIGHTS],
            *[new_m[n] for n in TWIN_WEIGHTS], *[new_v[n] for n in TWIN_WEIGHTS])
```

```python
import functools
import math

import jax
import jax.numpy as jnp
from jax import lax
from jax.experimental import pallas as pl
from jax.experimental.pallas import tpu as pltpu

F32 = jnp.float32
BF16 = jnp.bfloat16

N_DEV = 8
NORM_EPS = 1e-6
LN_EPS = 1e-5
CONV_K = 31
SG_CHUNK = 128
ATT_BLOCK = 128
HEAD_DIM = 64
HEADS_PER_GROUP = 4
GROUP_COLS = HEADS_PER_GROUP * HEAD_DIM
DIL_GROUPS = ((128, 1), (512, 4), (2048, 16))
REL_BUCKETS = 32
REL_MAX_DIST = 2048
X_HEADS = 4
ATT_SCALE = HEAD_DIM ** -0.5
MASK_VALUE = -1e30

ADAM_LR = 0.001
ADAM_B1 = 0.9
ADAM_B2 = 0.999
ADAM_EPS = 1e-08
ADAM_WD = 0.01
ADAM_STEP = 10

LANE = 128
ROW_TILE = 512
VMEM_LIMIT = 48 << 20

_NT = (((1,), (1,)), ((), ()))
_TN = (((0,), (0,)), ((), ()))


def _params(sem, vmem=VMEM_LIMIT):
    return pltpu.CompilerParams(dimension_semantics=sem, vmem_limit_bytes=vmem)


def _pick(n, cap):
    if n <= cap:
        return n
    best = None
    for t in range(LANE, cap + 1, LANE):
        if n % t == 0:
            best = t
    assert best is not None, (n, cap)
    return best


def _sigmoid(x):
    return 1.0 / (1.0 + jnp.exp(-x))


_GELU_C = math.sqrt(2.0 / math.pi)


def _gelu(x):
    return 0.5 * x * (1.0 + jnp.tanh(_GELU_C * (x + 0.044715 * x * x * x)))


def _gelu_and_grad(x):
    t = jnp.tanh(_GELU_C * (x + 0.044715 * x * x * x))
    g = 0.5 * x * (1.0 + t)
    dg = 0.5 * (1.0 + t) + 0.5 * x * (1.0 - t * t) * _GELU_C * (1.0 + 3 * 0.044715 * x * x)
    return g, dg


def _mm(a, b, *, out_dtype, name, add=None):
    M, K = a.shape
    K2, N = b.shape
    assert K == K2
    tm = _pick(M, 1024)
    tn = _pick(N, 1536)
    tk = _pick(K, 1536)
    nk = K // tk
    has_add = add is not None

    def body(*refs):
        if has_add:
            a_ref, b_ref, c_ref, o_ref = refs[:4]
            rest = refs[4:]
        else:
            a_ref, b_ref, o_ref = refs[:3]
            c_ref = None
            rest = refs[3:]
        part = jnp.dot(a_ref[...].astype(BF16), b_ref[...].astype(BF16), preferred_element_type=F32)
        if nk == 1:
            if has_add:
                part = part + c_ref[...]
            o_ref[...] = part.astype(o_ref.dtype)
            return
        acc_ref = rest[0]
        k = pl.program_id(2)

        @pl.when(k == 0)
        def _():
            acc_ref[...] = part

        @pl.when(k > 0)
        def _():
            acc_ref[...] += part

        @pl.when(k == nk - 1)
        def _():
            r = acc_ref[...]
            if has_add:
                r = r + c_ref[...]
            o_ref[...] = r.astype(o_ref.dtype)

    in_specs = [pl.BlockSpec((tm, tk), lambda i, j, k: (i, k)), pl.BlockSpec((tk, tn), lambda i, j, k: (k, j))]
    args = [a, b]
    if has_add:
        in_specs.append(pl.BlockSpec((tm, tn), lambda i, j, k: (i, j)))
        args.append(add)
    return pl.pallas_call(
        body, name=name, out_shape=jax.ShapeDtypeStruct((M, N), out_dtype),
        grid=(M // tm, N // tn, nk), in_specs=in_specs,
        out_specs=pl.BlockSpec((tm, tn), lambda i, j, k: (i, j)),
        scratch_shapes=[pltpu.VMEM((tm, tn), F32)] if nk > 1 else [],
        compiler_params=_params(("parallel", "parallel", "arbitrary")),
    )(*args)


def _mm_tn(a, b, *, name):
    T, M = a.shape
    T2, N = b.shape
    assert T == T2
    tm = _pick(M, 1536)
    tn = _pick(N, 1536)
    tt = _pick(T, 1024)
    nt = T // tt

    def body(a_ref, b_ref, o_ref):
        k = pl.program_id(2)
        part = lax.dot_general(a_ref[...].astype(BF16), b_ref[...].astype(BF16), _TN, preferred_element_type=F32)

        @pl.when(k == 0)
        def _():
            o_ref[...] = part

        @pl.when(k > 0)
        def _():
            o_ref[...] += part

    return pl.pallas_call(
        body, name=name, out_shape=jax.ShapeDtypeStruct((M, N), F32),
        grid=(M // tm, N // tn, nt),
        in_specs=[pl.BlockSpec((tt, tm), lambda i, j, k: (k, i)), pl.BlockSpec((tt, tn), lambda i, j, k: (k, j))],
        out_specs=pl.BlockSpec((tm, tn), lambda i, j, k: (i, j)),
        compiler_params=_params(("parallel", "parallel", "arbitrary")),
    )(a, b)


def _rms(x):
    r = lax.rsqrt(jnp.mean(x * x, axis=-1, keepdims=True) + NORM_EPS)
    return x * r, r


def _row_spec(cols, tr=ROW_TILE):
    return pl.BlockSpec((tr, cols), lambda i: (i, 0))


def _vec_spec(cols):
    return pl.BlockSpec((1, cols), lambda i: (0, 0))


def _norm_fwd(x, y, g_post, g_pre, *, name):
    T, D = x.shape
    has_post = y is not None
    has_pre = g_pre is not None

    def body(*refs):
        refs = list(refs)
        x_ref = refs.pop(0)
        xn = x_ref[...]
        if has_post:
            y_ref = refs.pop(0)
            gp_ref = refs.pop(0)
        if has_pre:
            gq_ref = refs.pop(0)
        if has_post:
            yh, _ = _rms(y_ref[...])
            xn = xn + yh * gp_ref[...]
            refs.pop(0)[...] = xn
        if has_pre:
            xh, _ = _rms(xn)
            refs.pop(0)[...] = (xh * gq_ref[...]).astype(BF16)

    args, in_specs, out_shape, out_specs = [x], [_row_spec(D)], [], []
    if has_post:
        args += [y, g_post.reshape(1, D)]
        in_specs += [_row_spec(D), _vec_spec(D)]
        out_shape.append(jax.ShapeDtypeStruct((T, D), F32))
        out_specs.append(_row_spec(D))
    if has_pre:
        args.append(g_pre.reshape(1, D))
        in_specs.append(_vec_spec(D))
        out_shape.append(jax.ShapeDtypeStruct((T, D), BF16))
        out_specs.append(_row_spec(D))
    return pl.pallas_call(body, name=name, out_shape=tuple(out_shape), grid=(T // ROW_TILE,), in_specs=in_specs,
                          out_specs=tuple(out_specs), compiler_params=_params(("parallel",)))(*args)


def _norm_bwd(dres, dh, x_new, g_pre, y, g_post, *, name):
    has_res = dres is not None
    has_pre = dh is not None
    has_post = y is not None
    out_d = has_pre and (has_res or has_post)
    T, D = (dres if has_res else dh).shape
    if not has_res:
        assert not has_post

    def body(*refs):
        refs = list(refs)
        i = pl.program_id(0)
        d = refs.pop(0)[...] if has_res else None
        if has_pre:
            dh_v = refs.pop(0)[...]
            xh, r = _rms(refs.pop(0)[...])
            gq = refs.pop(0)[...]
        if has_post:
            yh, ry = _rms(refs.pop(0)[...])
            gp = refs.pop(0)[...]
        if has_pre:
            dxh = dh_v * gq
            dpre = r * (dxh - xh * jnp.mean(dxh * xh, axis=-1, keepdims=True))
            d = dpre if d is None else d + dpre
            dgq = jnp.sum(dh_v * xh, axis=0, keepdims=True)
        if out_d:
            refs.pop(0)[...] = d
        if has_post:
            dyh = d * gp
            refs.pop(0)[...] = (ry * (dyh - yh * jnp.mean(dyh * yh, axis=-1, keepdims=True))).astype(BF16)
            dgp_ref = refs.pop(0)
            dgp = jnp.sum(d * yh, axis=0, keepdims=True)

            @pl.when(i == 0)
            def _():
                dgp_ref[...] = dgp

            @pl.when(i > 0)
            def _():
                dgp_ref[...] += dgp
        if has_pre:
            dgq_ref = refs.pop(0)

            @pl.when(i == 0)
            def _():
                dgq_ref[...] = dgq

            @pl.when(i > 0)
            def _():
                dgq_ref[...] += dgq

    args, in_specs, out_shape, out_specs = [], [], [], []
    if has_res:
        args.append(dres)
        in_specs.append(_row_spec(D))
    if has_pre:
        args += [dh, x_new, g_pre.reshape(1, D)]
        in_specs += [_row_spec(D), _row_spec(D), _vec_spec(D)]
    if has_post:
        args += [y, g_post.reshape(1, D)]
        in_specs += [_row_spec(D), _vec_spec(D)]
    if out_d:
        out_shape.append(jax.ShapeDtypeStruct((T, D), F32))
        out_specs.append(_row_spec(D))
    if has_post:
        out_shape += [jax.ShapeDtypeStruct((T, D), BF16), jax.ShapeDtypeStruct((1, D), F32)]
        out_specs += [_row_spec(D), _vec_spec(D)]
    if has_pre:
        out_shape.append(jax.ShapeDtypeStruct((1, D), F32))
        out_specs.append(_vec_spec(D))
    return pl.pallas_call(body, name=name, out_shape=tuple(out_shape), grid=(T // ROW_TILE,), in_specs=in_specs,
                          out_specs=tuple(out_specs), compiler_params=_params(("arbitrary",)))(*args)


def _loss_kernel(xf, target, *, name):
    T, D = xf.shape

    def body(x_ref, t_ref, loss_ref, d_ref):
        i = pl.program_id(0)
        e = x_ref[...] - t_ref[...]
        d_ref[...] = e * (1.0 / D)
        part = jnp.sum(jnp.sum(e * e, axis=0, keepdims=True), axis=1, keepdims=True) * (0.5 / D)
        part = jnp.broadcast_to(part, (1, LANE))

        @pl.when(i == 0)
        def _():
            loss_ref[...] = part

        @pl.when(i > 0)
        def _():
            loss_ref[...] += part

    return pl.pallas_call(body, name=name,
                          out_shape=(jax.ShapeDtypeStruct((1, LANE), F32), jax.ShapeDtypeStruct((T, D), F32)),
                          grid=(T // ROW_TILE,), in_specs=[_row_spec(D), _row_spec(D)],
                          out_specs=(_vec_spec(LANE), _row_spec(D)), compiler_params=_params(("arbitrary",)))(xf, target)


CONV_PAD = 32
CONV_CH = 64


def _ln_stats(x):
    mu = jnp.mean(x, axis=-1, keepdims=True)
    xc = x - mu
    rstd = lax.rsqrt(jnp.mean(xc * xc, axis=-1, keepdims=True) + LN_EPS)
    return xc * rstd, rstd


def _acc_out(ref, val, first):
    @pl.when(first)
    def _():
        ref[...] = val

    @pl.when(jnp.logical_not(first))
    def _():
        ref[...] += val


def _bra_fwd(za, conv_w, conv_b, ln_g, ln_b, *, B, S, name):
    C = conv_w.shape[1]

    def body(za_ref, w_ref, cb_ref, g_ref, b_ref, a4_ref, a2_ref, pad_ref):
        pad_ref[0:CONV_PAD, :] = jnp.zeros((CONV_PAD, C), F32)
        pad_ref[CONV_PAD:, :] = za_ref[:, 0:C] * _sigmoid(za_ref[:, C:2 * C])

        def chunk(i, _):
            base = pl.multiple_of(i * CONV_CH, CONV_CH)
            win = pad_ref[pl.ds(base, CONV_CH + CONV_PAD), :]
            acc = jnp.broadcast_to(cb_ref[...], (CONV_CH, C))
            for j in range(CONV_K):
                off = CONV_PAD - (CONV_K - 1) + j
                acc = acc + w_ref[j:j + 1, :] * win[off:off + CONV_CH, :]
            a2_ref[pl.ds(base, CONV_CH), :] = acc
            xh, _ = _ln_stats(acc)
            a3 = xh * g_ref[...] + b_ref[...]
            a4_ref[pl.ds(base, CONV_CH), :] = (a3 * _sigmoid(a3)).astype(BF16)
            return 0

        lax.fori_loop(0, S // CONV_CH, chunk, 0)

    vec = pl.BlockSpec((1, C), lambda b: (0, 0))
    return pl.pallas_call(
        body, name=name,
        out_shape=(jax.ShapeDtypeStruct((B * S, C), BF16), jax.ShapeDtypeStruct((B * S, C), F32)),
        grid=(B,),
        in_specs=[pl.BlockSpec((S, 2 * C), lambda b: (b, 0)), pl.BlockSpec((CONV_K, C), lambda b: (0, 0)), vec, vec, vec],
        out_specs=(pl.BlockSpec((S, C), lambda b: (b, 0)), pl.BlockSpec((S, C), lambda b: (b, 0))),
        scratch_shapes=[pltpu.VMEM((S + CONV_PAD, C), F32)],
        compiler_params=_params(("parallel",)),
    )(za, conv_w, conv_b.reshape(1, C), ln_g.reshape(1, C), ln_b.reshape(1, C))


def _bra_bwd_ln(da4, a2, ln_g, ln_b, *, name):
    T, C = a2.shape

    def body(da4_ref, a2_ref, g_ref, b_ref, da2_ref, dg_ref, db_ref, dcb_ref):
        first = pl.program_id(0) == 0
        xh, rstd = _ln_stats(a2_ref[...])
        a3 = xh * g_ref[...] + b_ref[...]
        sg = _sigmoid(a3)
        da3 = da4_ref[...] * (sg * (1.0 + a3 * (1.0 - sg)))
        dxh = da3 * g_ref[...]
        da2 = rstd * (dxh - jnp.mean(dxh, axis=-1, keepdims=True) - xh * jnp.mean(dxh * xh, axis=-1, keepdims=True))
        da2_ref[...] = da2
        _acc_out(dg_ref, jnp.sum(da3 * xh, axis=0, keepdims=True), first)
        _acc_out(db_ref, jnp.sum(da3, axis=0, keepdims=True), first)
        _acc_out(dcb_ref, jnp.sum(da2, axis=0, keepdims=True), first)

    vec = _vec_spec(C)
    vshape = jax.ShapeDtypeStruct((1, C), F32)
    return pl.pallas_call(body, name=name, out_shape=(jax.ShapeDtypeStruct((T, C), F32), vshape, vshape, vshape),
                          grid=(T // ROW_TILE,), in_specs=[_row_spec(C), _row_spec(C), vec, vec],
                          out_specs=(_row_spec(C), vec, vec, vec),
                          compiler_params=_params(("arbitrary",)))(da4, a2, ln_g.reshape(1, C), ln_b.reshape(1, C))


def _bra_bwd_conv(da2, za, conv_w, *, B, S, name):
    C = conv_w.shape[1]
    SUB = 8

    def body(da2_ref, za_ref, w_ref, dza_ref, dw_ref, pad1_ref, pad2_ref, dwacc_ref):
        first = pl.program_id(0) == 0
        pad1_ref[0:CONV_PAD, :] = jnp.zeros((CONV_PAD, C), F32)
        pad1_ref[CONV_PAD:, :] = za_ref[:, 0:C] * _sigmoid(za_ref[:, C:2 * C])
        pad2_ref[0:S, :] = da2_ref[...]
        pad2_ref[S:, :] = jnp.zeros((CONV_PAD, C), F32)
        dwacc_ref[...] = jnp.zeros_like(dwacc_ref)

        def chunk(i, _):
            base = pl.multiple_of(i * CONV_CH, CONV_CH)
            rows = pl.ds(base, CONV_CH)
            win1 = pad1_ref[pl.ds(base, CONV_CH + CONV_PAD), :]
            win2 = pad2_ref[pl.ds(base, CONV_CH + CONV_PAD), :]
            d = win2[0:CONV_CH, :]
            da1 = jnp.zeros((CONV_CH, C), F32)
            for j in range(CONV_K):
                off2 = CONV_K - 1 - j
                da1 = da1 + w_ref[j:j + 1, :] * win2[off2:off2 + CONV_CH, :]
                off1 = CONV_PAD - (CONV_K - 1) + j
                prod = d * win1[off1:off1 + CONV_CH, :]
                dwacc_ref[j * SUB:(j + 1) * SUB, :] += jnp.sum(prod.reshape(CONV_CH // SUB, SUB, C), axis=0)
            a_val = za_ref[rows, 0:C]
            sg = _sigmoid(za_ref[rows, C:2 * C])
            dza_ref[rows, 0:C] = (da1 * sg).astype(BF16)
            dza_ref[rows, C:2 * C] = (da1 * a_val * sg * (1.0 - sg)).astype(BF16)
            return 0

        lax.fori_loop(0, S // CONV_CH, chunk, 0)
        _acc_out(dw_ref, jnp.sum(dwacc_ref[...].reshape(CONV_K, SUB, C), axis=1), first)

    return pl.pallas_call(
        body, name=name,
        out_shape=(jax.ShapeDtypeStruct((B * S, 2 * C), BF16), jax.ShapeDtypeStruct((CONV_K, C), F32)),
        grid=(B,),
        in_specs=[pl.BlockSpec((S, C), lambda b: (b, 0)), pl.BlockSpec((S, 2 * C), lambda b: (b, 0)),
                  pl.BlockSpec((CONV_K, C), lambda b: (0, 0))],
        out_specs=(pl.BlockSpec((S, 2 * C), lambda b: (b, 0)), pl.BlockSpec((CONV_K, C), lambda b: (0, 0))),
        scratch_shapes=[pltpu.VMEM((S + CONV_PAD, C), F32), pltpu.VMEM((S + CONV_PAD, C), F32),
                        pltpu.VMEM((CONV_K * SUB, C), F32)],
        compiler_params=_params(("arbitrary",)),
    )(da2, za, conv_w)


def _tril_mask():
    r = lax.broadcasted_iota(jnp.int32, (SG_CHUNK, SG_CHUNK), 0)
    c = lax.broadcasted_iota(jnp.int32, (SG_CHUNK, SG_CHUNK), 1)
    return c <= r


def _brb_fwd(zb, ln_g, ln_b, w_s, b_s_t, *, name):
    T, C2 = zb.shape
    C = C2 // 2
    G = w_s.shape[0]
    GC = C // G

    def body(zb_ref, g_ref, b_ref, ws_ref, bs_ref, p_ref):
        z = _gelu(zb_ref[...])
        u = z[:, 0:C]
        xh, _ = _ln_stats(z[:, C:2 * C])
        v1 = (xh * g_ref[...] + b_ref[...]).astype(BF16)
        mask = _tril_mask()
        outs = []
        for gi in range(G):
            ws = jnp.where(mask, ws_ref[gi], 0.0).astype(BF16)
            v2 = jnp.dot(ws, v1[:, gi * GC:(gi + 1) * GC], preferred_element_type=F32) + bs_ref[:, gi:gi + 1]
            outs.append(v2)
        p_ref[...] = (u * jnp.concatenate(outs, axis=1)).astype(BF16)

    return pl.pallas_call(
        body, name=name, out_shape=jax.ShapeDtypeStruct((T, C), BF16), grid=(T // SG_CHUNK,),
        in_specs=[_row_spec(C2, SG_CHUNK), _vec_spec(C), _vec_spec(C),
                  pl.BlockSpec((G, SG_CHUNK, SG_CHUNK), lambda i: (0, 0, 0)), pl.BlockSpec((SG_CHUNK, G), lambda i: (0, 0))],
        out_specs=_row_spec(C, SG_CHUNK), compiler_params=_params(("parallel",)),
    )(zb, ln_g.reshape(1, C), ln_b.reshape(1, C), w_s, b_s_t)


def _brb_bwd(dp, zb, ln_g, ln_b, w_s, b_s_t, *, name):
    T, C2 = zb.shape
    C = C2 // 2
    G = w_s.shape[0]
    GC = C // G

    def body(dp_ref, zb_ref, g_ref, b_ref, ws_ref, bs_ref, dzb_ref, dws_ref, dbs_ref, dg_ref, db_ref):
        first = pl.program_id(0) == 0
        z, dz = _gelu_and_grad(zb_ref[...])
        u = z[:, 0:C]
        xh, rstd = _ln_stats(z[:, C:2 * C])
        v1 = (xh * g_ref[...] + b_ref[...]).astype(BF16)
        dp_v = dp_ref[...]
        mask = _tril_mask()
        v2s, dv1s, dwss, dbss = [], [], [], []
        for gi in range(G):
            cols = slice(gi * GC, (gi + 1) * GC)
            ws = jnp.where(mask, ws_ref[gi], 0.0).astype(BF16)
            v2s.append(jnp.dot(ws, v1[:, cols], preferred_element_type=F32) + bs_ref[:, gi:gi + 1])
            dv2 = dp_v[:, cols] * u[:, cols]
            dv2b = dv2.astype(BF16)
            dbss.append(jnp.sum(dv2, axis=1, keepdims=True))
            dwss.append(jnp.where(mask, lax.dot_general(dv2b, v1[:, cols], _NT, preferred_element_type=F32), 0.0))
            dv1s.append(lax.dot_general(ws, dv2b, _TN, preferred_element_type=F32))
        du = dp_v * jnp.concatenate(v2s, axis=1)
        dv1 = jnp.concatenate(dv1s, axis=1)
        dxh = dv1 * g_ref[...]
        dv0 = rstd * (dxh - jnp.mean(dxh, axis=-1, keepdims=True) - xh * jnp.mean(dxh * xh, axis=-1, keepdims=True))
        dzb_ref[:, 0:C] = (du * dz[:, 0:C]).astype(BF16)
        dzb_ref[:, C:2 * C] = (dv0 * dz[:, C:2 * C]).astype(BF16)
        _acc_out(dws_ref, jnp.stack(dwss, axis=0), first)
        _acc_out(dbs_ref, jnp.concatenate(dbss, axis=1), first)
        _acc_out(dg_ref, jnp.sum(dv1 * xh, axis=0, keepdims=True), first)
        _acc_out(db_ref, jnp.sum(dv1, axis=0, keepdims=True), first)

    wspec = pl.BlockSpec((G, SG_CHUNK, SG_CHUNK), lambda i: (0, 0, 0))
    bspec = pl.BlockSpec((SG_CHUNK, G), lambda i: (0, 0))
    return pl.pallas_call(
        body, name=name,
        out_shape=(jax.ShapeDtypeStruct((T, C2), BF16), jax.ShapeDtypeStruct((G, SG_CHUNK, SG_CHUNK), F32),
                   jax.ShapeDtypeStruct((SG_CHUNK, G), F32), jax.ShapeDtypeStruct((1, C), F32),
                   jax.ShapeDtypeStruct((1, C), F32)),
        grid=(T // SG_CHUNK,),
        in_specs=[_row_spec(C, SG_CHUNK), _row_spec(C2, SG_CHUNK), _vec_spec(C), _vec_spec(C), wspec, bspec],
        out_specs=(_row_spec(C2, SG_CHUNK), wspec, bspec, _vec_spec(C), _vec_spec(C)),
        compiler_params=_params(("arbitrary",)),
    )(dp, zb, ln_g.reshape(1, C), ln_b.reshape(1, C), w_s, b_s_t)


def _attn_fwd(qkv_g, bias, *, name):
    B, d, L, _ = qkv_g.shape
    nb = L // ATT_BLOCK
    GCOL = GROUP_COLS

    def body(qkv_ref, bias_ref, o_ref, lse_ref):
        def blk(r0, first):
            nk = ATT_BLOCK if first else 2 * ATT_BLOCK
            k0 = r0 if first else r0 - ATT_BLOCK
            qb = qkv_ref[pl.ds(r0, ATT_BLOCK), 0:GCOL]
            kb = qkv_ref[pl.ds(k0, nk), GCOL:2 * GCOL]
            vb = qkv_ref[pl.ds(k0, nk), 2 * GCOL:3 * GCOL]
            outs, lses = [], []
            for h in range(HEADS_PER_GROUP):
                sl = slice(h * HEAD_DIM, (h + 1) * HEAD_DIM)
                bh = bias_ref[h, :, ATT_BLOCK:2 * ATT_BLOCK] if first else bias_ref[h]
                s = lax.dot_general(qb[:, sl], kb[:, sl], _NT, preferred_element_type=F32) * ATT_SCALE + bh
                m = jnp.max(s, axis=1, keepdims=True)
                e = jnp.exp(s - m)
                ssum = jnp.sum(e, axis=1, keepdims=True)
                outs.append(jnp.dot(e.astype(BF16), vb[:, sl], preferred_element_type=F32) / ssum)
                lses.append(jnp.broadcast_to(m + jnp.log(ssum), (ATT_BLOCK, HEAD_DIM)))
            o_ref[pl.ds(r0, ATT_BLOCK), :] = jnp.concatenate(outs, axis=1)
            lse_ref[pl.ds(r0, ATT_BLOCK), :] = jnp.concatenate(lses, axis=1)

        blk(0, True)
        if nb > 1:
            def loop(n, _):
                blk(pl.multiple_of(n * ATT_BLOCK, ATT_BLOCK), False)
                return 0

            lax.fori_loop(1, nb, loop, 0)

    spec = lambda cols: pl.BlockSpec((None, None, L, cols), lambda b, c: (b, c, 0, 0))
    oshape = jax.ShapeDtypeStruct((B, d, L, GCOL), F32)
    return pl.pallas_call(
        body, name=name, out_shape=(oshape, oshape), grid=(B, d),
        in_specs=[spec(3 * GCOL), pl.BlockSpec((HEADS_PER_GROUP, ATT_BLOCK, 2 * ATT_BLOCK), lambda b, c: (0, 0, 0))],
        out_specs=(spec(GCOL), spec(GCOL)), compiler_params=_params(("parallel", "parallel")),
    )(qkv_g, bias)


def _attn_bwd(qkv_g, doc_g, lse_g, dd_g, bias, *, name):
    B, d, L, _ = qkv_g.shape
    nb = L // ATT_BLOCK
    GCOL = GROUP_COLS

    def body(qkv_ref, doc_ref, lse_ref, dd_ref, bias_ref, dqkv_ref, dbias_ref, dk_ref, dv_ref):
        @pl.when(jnp.logical_and(pl.program_id(0) == 0, pl.program_id(1) == 0))
        def _():
            dbias_ref[...] = jnp.zeros_like(dbias_ref)

        dk_ref[...] = jnp.zeros_like(dk_ref)
        dv_ref[...] = jnp.zeros_like(dv_ref)

        def blk(r0, first):
            nk = ATT_BLOCK if first else 2 * ATT_BLOCK
            k0 = r0 if first else r0 - ATT_BLOCK
            rows = pl.ds(r0, ATT_BLOCK)
            krows = pl.ds(k0, nk)
            qb = qkv_ref[rows, 0:GCOL]
            kb = qkv_ref[krows, GCOL:2 * GCOL]
            vb = qkv_ref[krows, 2 * GCOL:3 * GCOL]
            dob = doc_ref[rows, :].astype(BF16)
            lse = lse_ref[rows, :]
            dd = dd_ref[rows, :]
            dqs, dks, dvs = [], [], []
            for h in range(HEADS_PER_GROUP):
                sl = slice(h * HEAD_DIM, (h + 1) * HEAD_DIM)
                c0 = h * HEAD_DIM
                bh = bias_ref[h, :, ATT_BLOCK:2 * ATT_BLOCK] if first else bias_ref[h]
                s = lax.dot_general(qb[:, sl], kb[:, sl], _NT, preferred_element_type=F32) * ATT_SCALE + bh
                p = jnp.exp(s - lse[:, c0:c0 + 1])
                dp = lax.dot_general(dob[:, sl], vb[:, sl], _NT, preferred_element_type=F32)
                ds = p * (dp - dd[:, c0:c0 + 1])
                if first:
                    dbias_ref[h, :, ATT_BLOCK:2 * ATT_BLOCK] += ds
                else:
                    dbias_ref[h] += ds
                dsb = ds.astype(BF16)
                dqs.append(jnp.dot(dsb, kb[:, sl], preferred_element_type=F32) * ATT_SCALE)
                dks.append(lax.dot_general(dsb, qb[:, sl], _TN, preferred_element_type=F32) * ATT_SCALE)
                dvs.append(lax.dot_general(p.astype(BF16), dob[:, sl], _TN, preferred_element_type=F32))
            dqkv_ref[rows, 0:GCOL] = jnp.concatenate(dqs, axis=1).astype(BF16)
            dk_ref[krows, :] += jnp.concatenate(dks, axis=1)
            dv_ref[krows, :] += jnp.concatenate(dvs, axis=1)

        blk(0, True)
        if nb > 1:
            def loop(n, _):
                blk(pl.multiple_of(n * ATT_BLOCK, ATT_BLOCK), False)
                return 0

            lax.fori_loop(1, nb, loop, 0)
        dqkv_ref[:, GCOL:2 * GCOL] = dk_ref[...].astype(BF16)
        dqkv_ref[:, 2 * GCOL:3 * GCOL] = dv_ref[...].astype(BF16)

    spec = lambda cols: pl.BlockSpec((None, None, L, cols), lambda b, c: (b, c, 0, 0))
    bspec = pl.BlockSpec((HEADS_PER_GROUP, ATT_BLOCK, 2 * ATT_BLOCK), lambda b, c: (0, 0, 0))
    return pl.pallas_call(
        body, name=name,
        out_shape=(jax.ShapeDtypeStruct((B, d, L, 3 * GCOL), BF16),
                   jax.ShapeDtypeStruct((HEADS_PER_GROUP, ATT_BLOCK, 2 * ATT_BLOCK), F32)),
        grid=(B, d),
        in_specs=[spec(3 * GCOL), spec(GCOL), spec(GCOL), spec(GCOL), bspec],
        out_specs=(spec(3 * GCOL), bspec),
        scratch_shapes=[pltpu.VMEM((L, GCOL), F32), pltpu.VMEM((L, GCOL), F32)],
        compiler_params=_params(("arbitrary", "arbitrary")),
    )(qkv_g, doc_g, lse_g, dd_g, bias)


def _attn_combine(os_, lses, *, name):
    T, GC = os_[0].shape
    n = len(os_)

    def body(*refs):
        o_refs, l_refs, oc_ref, lt_ref = refs[:n], refs[n:2 * n], refs[2 * n], refs[2 * n + 1]
        ls = [r[...] for r in l_refs]
        m = functools.reduce(jnp.maximum, ls)
        ws = [jnp.exp(l - m) for l in ls]
        tot = functools.reduce(jnp.add, ws)
        acc = functools.reduce(jnp.add, [w * r[...] for w, r in zip(ws, o_refs)])
        oc_ref[...] = acc / tot
        lt_ref[...] = m + jnp.log(tot)

    shp = jax.ShapeDtypeStruct((T, GC), F32)
    return pl.pallas_call(body, name=name, out_shape=(shp, shp), grid=(T // ROW_TILE,),
                          in_specs=[_row_spec(GC)] * (2 * n), out_specs=(_row_spec(GC), _row_spec(GC)),
                          compiler_params=_params(("parallel",)))(*os_, *lses)


def _attn_rowdot(doc, oc, *, name):
    T, GC = oc.shape

    def body(doc_ref, oc_ref, dd_ref):
        prod = doc_ref[...] * oc_ref[...]
        parts = []
        for h in range(GC // HEAD_DIM):
            s = jnp.sum(prod[:, h * HEAD_DIM:(h + 1) * HEAD_DIM], axis=1, keepdims=True)
            parts.append(jnp.broadcast_to(s, (ROW_TILE, HEAD_DIM)))
        dd_ref[...] = jnp.concatenate(parts, axis=1)

    return pl.pallas_call(body, name=name, out_shape=jax.ShapeDtypeStruct((T, GC), F32), grid=(T // ROW_TILE,),
                          in_specs=[_row_spec(GC), _row_spec(GC)], out_specs=_row_spec(GC),
                          compiler_params=_params(("parallel",)))(doc, oc)


def _bucket_sum(dbias, buckets, *, name):
    depth, NH = dbias.shape[:2]

    def body(db_ref, bk_ref, out_ref):
        rows = lax.broadcasted_iota(jnp.int32, (REL_BUCKETS, LANE), 0)
        cols = lax.broadcasted_iota(jnp.int32, (REL_BUCKETS, LANE), 1)

        def per_bucket(b, acc):
            for h in range(NH):
                sel = bk_ref[h // HEADS_PER_GROUP] == b
                tot = functools.reduce(jnp.add, [db_ref[l, h] for l in range(depth)])
                s = jnp.sum(jnp.where(sel, tot, 0.0))
                acc = acc + jnp.where(jnp.logical_and(rows == b, cols == h), s, 0.0)
            return acc

        out_ref[...] = lax.fori_loop(0, REL_BUCKETS, per_bucket, jnp.zeros((REL_BUCKETS, LANE), F32))

    return pl.pallas_call(body, name=name, out_shape=jax.ShapeDtypeStruct((REL_BUCKETS, LANE), F32),
                          compiler_params=pltpu.CompilerParams(vmem_limit_bytes=VMEM_LIMIT))(dbias, buckets)


def _merge_fwd(zg, bg, ya, yb, yc, *, name):
    T, D3 = zg.shape
    D = D3 // 3

    def body(zg_ref, bg_ref, ya_ref, yb_ref, yc_ref, out_ref):
        acc = None
        for i, y_ref in enumerate((ya_ref, yb_ref, yc_ref)):
            g = _sigmoid(zg_ref[:, i * D:(i + 1) * D] + bg_ref[:, i * D:(i + 1) * D])
            t = g * y_ref[...]
            acc = t if acc is None else acc + t
        out_ref[...] = acc.astype(BF16)

    return pl.pallas_call(body, name=name, out_shape=jax.ShapeDtypeStruct((T, D), BF16), grid=(T // ROW_TILE,),
                          in_specs=[_row_spec(D3), _vec_spec(D3), _row_spec(D), _row_spec(D), _row_spec(D)],
                          out_specs=_row_spec(D), compiler_params=_params(("parallel",)))(zg, bg.reshape(1, D3), ya, yb, yc)


def _merge_bwd(dm, zg, bg, ya, yb, yc, *, name):
    T, D3 = zg.shape
    D = D3 // 3

    def body(dm_ref, zg_ref, bg_ref, ya_ref, yb_ref, yc_ref, dya_ref, dyb_ref, dyc_ref, dzg_ref, dbg_ref):
        first = pl.program_id(0) == 0
        dm_v = dm_ref[...]
        dbs = []
        for i, (y_ref, dy_ref) in enumerate(((ya_ref, dya_ref), (yb_ref, dyb_ref), (yc_ref, dyc_ref))):
            g = _sigmoid(zg_ref[:, i * D:(i + 1) * D] + bg_ref[:, i * D:(i + 1) * D])
            dy_ref[...] = (dm_v * g).astype(BF16)
            dz = dm_v * y_ref[...] * g * (1.0 - g)
            dzg_ref[:, i * D:(i + 1) * D] = dz.astype(BF16)
            dbs.append(jnp.sum(dz, axis=0, keepdims=True))
        _acc_out(dbg_ref, jnp.concatenate(dbs, axis=1), first)

    bshape = jax.ShapeDtypeStruct((T, D), BF16)
    return pl.pallas_call(
        body, name=name,
        out_shape=(bshape, bshape, bshape, jax.ShapeDtypeStruct((T, D3), BF16), jax.ShapeDtypeStruct((1, D3), F32)),
        grid=(T // ROW_TILE,),
        in_specs=[_row_spec(D), _row_spec(D3), _vec_spec(D3), _row_spec(D), _row_spec(D), _row_spec(D)],
        out_specs=(_row_spec(D), _row_spec(D), _row_spec(D), _row_spec(D3), _vec_spec(D3)),
        compiler_params=_params(("arbitrary",)))(dm, zg, bg.reshape(1, D3), ya, yb, yc)


XQ_TILE = 512


def _xattn_fwd(q, kv, *, B, S, name):
    D = q.shape[1]
    M = kv.shape[0] // B
    E = D // X_HEADS
    scale = E ** -0.5

    def body(q_ref, kv_ref, o_ref):
        outs = []
        for h in range(X_HEADS):
            s = lax.dot_general(q_ref[:, h * E:(h + 1) * E], kv_ref[:, h * E:(h + 1) * E], _NT,
                                preferred_element_type=F32) * scale
            e = jnp.exp(s - jnp.max(s, axis=1, keepdims=True))
            p = e / jnp.sum(e, axis=1, keepdims=True)
            outs.append(jnp.dot(p.astype(BF16), kv_ref[:, D + h * E:D + (h + 1) * E], preferred_element_type=F32))
        o_ref[...] = jnp.concatenate(outs, axis=1).astype(BF16)

    nq = S // XQ_TILE
    return pl.pallas_call(
        body, name=name, out_shape=jax.ShapeDtypeStruct((B * S, D), BF16), grid=(B, nq),
        in_specs=[pl.BlockSpec((XQ_TILE, D), lambda b, i: (b * nq + i, 0)), pl.BlockSpec((M, 2 * D), lambda b, i: (b, 0))],
        out_specs=pl.BlockSpec((XQ_TILE, D), lambda b, i: (b * nq + i, 0)),
        compiler_params=_params(("parallel", "parallel")))(q, kv)


def _xattn_bwd(q, kv, do, *, B, S, name):
    D = q.shape[1]
    M = kv.shape[0] // B
    E = D // X_HEADS
    scale = E ** -0.5

    def body(q_ref, kv_ref, do_ref, dq_ref, dkv_ref):
        first = pl.program_id(1) == 0
        dqs, dks, dvs = [], [], []
        for h in range(X_HEADS):
            qh = q_ref[:, h * E:(h + 1) * E]
            kh = kv_ref[:, h * E:(h + 1) * E]
            vh = kv_ref[:, D + h * E:D + (h + 1) * E]
            doh = do_ref[:, h * E:(h + 1) * E]
            s = lax.dot_general(qh, kh, _NT, preferred_element_type=F32) * scale
            e = jnp.exp(s - jnp.max(s, axis=1, keepdims=True))
            p = e / jnp.sum(e, axis=1, keepdims=True)
            dp = lax.dot_general(doh, vh, _NT, preferred_element_type=F32)
            ds = (p * (dp - jnp.sum(p * dp, axis=1, keepdims=True))).astype(BF16)
            dqs.append(jnp.dot(ds, kh, preferred_element_type=F32) * scale)
            dks.append(lax.dot_general(ds, qh, _TN, preferred_element_type=F32) * scale)
            dvs.append(lax.dot_general(p.astype(BF16), doh, _TN, preferred_element_type=F32))
        dq_ref[...] = jnp.concatenate(dqs, axis=1).astype(BF16)
        _acc_out(dkv_ref, jnp.concatenate(dks + dvs, axis=1), first)

    nq = S // XQ_TILE
    qspec = pl.BlockSpec((XQ_TILE, D), lambda b, i: (b * nq + i, 0))
    kvspec = pl.BlockSpec((M, 2 * D), lambda b, i: (b, 0))
    return pl.pallas_call(
        body, name=name,
        out_shape=(jax.ShapeDtypeStruct((B * S, D), BF16), jax.ShapeDtypeStruct((B * M, 2 * D), F32)),
        grid=(B, nq), in_specs=[qspec, kvspec, qspec], out_specs=(qspec, kvspec),
        compiler_params=_params(("arbitrary", "arbitrary")))(q, kv, do)


FFN_COLS = 256
FFN_PAD = 8
FFN_CH = 256
FFN_K = 3


def _ffn_gate(win, w_ref, cb_ref, n):
    g = jnp.broadcast_to(cb_ref[...], (n, win.shape[1]))
    for j in range(FFN_K):
        off = FFN_PAD - (FFN_K - 1) + j
        g = g + w_ref[j:j + 1, :] * win[off:off + n, :]
    return g


def _ffn_fwd(up, conv_w, conv_b, *, B, S, name):
    F = conv_w.shape[1]
    nc = F // FFN_COLS

    def body(gp_ref, val_ref, w_ref, cb_ref, act_ref, pad_ref):
        pad_ref[0:FFN_PAD, :] = jnp.zeros((FFN_PAD, FFN_COLS), F32)
        pad_ref[FFN_PAD:, :] = gp_ref[...]

        def chunk(i, _):
            base = pl.multiple_of(i * FFN_CH, FFN_CH)
            gate = _ffn_gate(pad_ref[pl.ds(base, FFN_CH + FFN_PAD), :], w_ref, cb_ref, FFN_CH)
            act_ref[pl.ds(base, FFN_CH), :] = (_gelu(gate) * val_ref[pl.ds(base, FFN_CH), :]).astype(BF16)
            return 0

        lax.fori_loop(0, S // FFN_CH, chunk, 0)

    return pl.pallas_call(
        body, name=name, out_shape=jax.ShapeDtypeStruct((B * S, F), BF16), grid=(B, nc),
        in_specs=[pl.BlockSpec((S, FFN_COLS), lambda b, j: (b, j)), pl.BlockSpec((S, FFN_COLS), lambda b, j: (b, nc + j)),
                  pl.BlockSpec((FFN_K, FFN_COLS), lambda b, j: (0, j)), pl.BlockSpec((1, FFN_COLS), lambda b, j: (0, j))],
        out_specs=pl.BlockSpec((S, FFN_COLS), lambda b, j: (b, j)),
        scratch_shapes=[pltpu.VMEM((S + FFN_PAD, FFN_COLS), F32)],
        compiler_params=_params(("parallel", "parallel")))(up, up, conv_w, conv_b.reshape(1, F))


def _ffn_bwd(dact, up, conv_w, conv_b, *, B, S, name):
    F = conv_w.shape[1]
    nc = F // FFN_COLS
    SUB = 8

    def body(dact_ref, gp_ref, val_ref, w_ref, cb_ref, dgp_ref, dval_ref, dw_ref, dcb_ref, pad_ref, pad2_ref, acc_ref):
        first = pl.program_id(1) == 0
        pad_ref[0:FFN_PAD, :] = jnp.zeros((FFN_PAD, FFN_COLS), F32)
        pad_ref[FFN_PAD:, :] = gp_ref[...]
        pad2_ref[S:, :] = jnp.zeros((FFN_PAD, FFN_COLS), F32)
        acc_ref[...] = jnp.zeros_like(acc_ref)

        def chunk1(i, _):
            base = pl.multiple_of(i * FFN_CH, FFN_CH)
            rows = pl.ds(base, FFN_CH)
            gate = _ffn_gate(pad_ref[pl.ds(base, FFN_CH + FFN_PAD), :], w_ref, cb_ref, FFN_CH)
            gl, dgl = _gelu_and_grad(gate)
            da = dact_ref[rows, :]
            dval_ref[rows, :] = (da * gl).astype(BF16)
            pad2_ref[rows, :] = da * val_ref[rows, :] * dgl
            return 0

        lax.fori_loop(0, S // FFN_CH, chunk1, 0)

        def chunk2(i, _):
            base = pl.multiple_of(i * FFN_CH, FFN_CH)
            rows = pl.ds(base, FFN_CH)
            win2 = pad2_ref[pl.ds(base, FFN_CH + FFN_PAD), :]
            win1 = pad_ref[pl.ds(base, FFN_CH + FFN_PAD), :]
            dg = win2[0:FFN_CH, :]
            dgp = jnp.zeros((FFN_CH, FFN_COLS), F32)
            for j in range(FFN_K):
                off2 = FFN_K - 1 - j
                dgp = dgp + w_ref[j:j + 1, :] * win2[off2:off2 + FFN_CH, :]
                off1 = FFN_PAD - (FFN_K - 1) + j
                prod = dg * win1[off1:off1 + FFN_CH, :]
                acc_ref[j * SUB:(j + 1) * SUB, :] += jnp.sum(prod.reshape(FFN_CH // SUB, SUB, FFN_COLS), axis=0)
            acc_ref[FFN_K * SUB:(FFN_K + 1) * SUB, :] += jnp.sum(dg.reshape(FFN_CH // SUB, SUB, FFN_COLS), axis=0)
            dgp_ref[rows, :] = dgp.astype(BF16)
            return 0

        lax.fori_loop(0, S // FFN_CH, chunk2, 0)
        sums = jnp.sum(acc_ref[...].reshape(FFN_K + 1, SUB, FFN_COLS), axis=1)
        _acc_out(dw_ref, sums[0:FFN_K, :], first)
        _acc_out(dcb_ref, sums[FFN_K:FFN_K + 1, :], first)

    blk = lambda off: pl.BlockSpec((S, FFN_COLS), lambda j, b: (b, off + j))
    return pl.pallas_call(
        body, name=name,
        out_shape=(jax.ShapeDtypeStruct((B * S, F), BF16), jax.ShapeDtypeStruct((B * S, F), BF16),
                   jax.ShapeDtypeStruct((FFN_K, F), F32), jax.ShapeDtypeStruct((1, F), F32)),
        grid=(nc, B),
        in_specs=[blk(0), blk(0), blk(nc), pl.BlockSpec((FFN_K, FFN_COLS), lambda j, b: (0, j)),
                  pl.BlockSpec((1, FFN_COLS), lambda j, b: (0, j))],
        out_specs=(blk(0), blk(0), pl.BlockSpec((FFN_K, FFN_COLS), lambda j, b: (0, j)),
                   pl.BlockSpec((1, FFN_COLS), lambda j, b: (0, j))),
        scratch_shapes=[pltpu.VMEM((S + FFN_PAD, FFN_COLS), F32), pltpu.VMEM((S + FFN_PAD, FFN_COLS), F32),
                        pltpu.VMEM(((FFN_K + 1) * SUB, FFN_COLS), F32)],
        compiler_params=_params(("arbitrary", "arbitrary")))(dact, up, up, conv_w, conv_b.reshape(1, F))


def _adamw(w, g, m, v, *, name):
    shape = w.shape
    cols = shape[-1]
    rows = w.size // cols
    w2, g2, m2, v2 = (t.reshape(rows, cols) for t in (w, g, m, v))
    tr = rows
    if rows * cols * 4 > (1 << 20):
        for t in range(8, rows, 8):
            if rows % t == 0 and t * cols * 4 <= (1 << 20):
                tr = t
    c1 = 1.0 - ADAM_B1 ** ADAM_STEP
    c2 = 1.0 - ADAM_B2 ** ADAM_STEP

    def body(w_ref, g_ref, m_ref, v_ref, d_ref, mo_ref, vo_ref):
        gv = g_ref[...]
        mn = ADAM_B1 * m_ref[...] + (1.0 - ADAM_B1) * gv
        vn = ADAM_B2 * v_ref[...] + (1.0 - ADAM_B2) * (gv * gv)
        d_ref[...] = -ADAM_LR * ((mn / c1) / (jnp.sqrt(vn / c2) + ADAM_EPS) + ADAM_WD * w_ref[...])
        mo_ref[...] = mn
        vo_ref[...] = vn

    spec = pl.BlockSpec((tr, cols), lambda i: (i, 0))
    oshape = jax.ShapeDtypeStruct((rows, cols), F32)
    d, mn, vn = pl.pallas_call(body, name=name, out_shape=(oshape, oshape, oshape), grid=(rows // tr,),
                               in_specs=[spec] * 4, out_specs=(spec, spec, spec),
                               compiler_params=_params(("parallel",)))(w2, g2, m2, v2)
    return d.reshape(shape), mn.reshape(shape), vn.reshape(shape)


_HBM = pl.BlockSpec(memory_space=pltpu.HBM)
_MESH = pl.DeviceIdType.MESH


def _my_pos():
    return lax.axis_index("x"), lax.axis_index("y"), lax.axis_index("c")


def _flip(pos, k):
    x, y, c = pos
    fx, fy, fc = (k >> 2) & 1, (k >> 1) & 1, k & 1
    return (x ^ fx if fx else x, y ^ fy if fy else y, c ^ fc if fc else c)


def _index_of(pos):
    return 4 * pos[0] + 2 * pos[1] + pos[2]


def _all_gather(x, *, name):
    R, C = x.shape

    def body(x_ref, out_ref, send_sems, recv_sems, local_sem):
        me = _my_pos()
        sibling = _flip(me, 1)
        chips = [2, 4, 6]

        def copy(k, block_pos, to, src=None):
            blk = out_ref.at[_index_of(block_pos)]
            return pltpu.make_async_remote_copy(src_ref=blk if src is None else src, dst_ref=blk,
                                                send_sem=send_sems.at[k], recv_sem=recv_sems.at[k],
                                                device_id=to, device_id_type=_MESH)

        mine = pltpu.make_async_copy(x_ref, out_ref.at[_index_of(me)], local_sem)
        mine.start()
        first = [copy(0, me, sibling, src=x_ref)]
        first += [copy(1 + j, me, _flip(me, f), src=x_ref) for j, f in enumerate(chips)]
        for cp in first:
            cp.start()
        passed = [copy(4 + j, _flip(me, f), sibling) for j, f in enumerate(chips)]
        for j, f in enumerate(chips):
            copy(1 + j, _flip(me, f), me).wait_recv()
            passed[j].start()
        copy(0, sibling, me).wait_recv()
        for j, f in enumerate(chips):
            copy(4 + j, _flip(sibling, f), me).wait_recv()
        for cp in first + passed:
            cp.wait_send()
        mine.wait()

    return pl.pallas_call(
        body, name=name, out_shape=jax.ShapeDtypeStruct((N_DEV, R, C), x.dtype),
        in_specs=[_HBM], out_specs=_HBM,
        scratch_shapes=[pltpu.SemaphoreType.DMA((7,)), pltpu.SemaphoreType.DMA((7,)), pltpu.SemaphoreType.DMA],
    )(x)


def _all_to_all(x, *, name):
    _, R, C = x.shape

    def body(x_ref, out_ref, send_sems, recv_sems, local_sem):
        me = _my_pos()
        my_i = _index_of(me)
        mine = pltpu.make_async_copy(x_ref.at[my_i], out_ref.at[my_i], local_sem)
        mine.start()
        copies = []
        for k in range(1, N_DEV):
            peer = _flip(me, k)
            copies.append(pltpu.make_async_remote_copy(
                src_ref=x_ref.at[_index_of(peer)], dst_ref=out_ref.at[my_i],
                send_sem=send_sems.at[k - 1], recv_sem=recv_sems.at[k - 1], device_id=peer, device_id_type=_MESH))
        for cp in copies:
            cp.start()
        for k in range(1, N_DEV):
            peer = _flip(me, k)
            pltpu.make_async_remote_copy(
                src_ref=x_ref.at[my_i], dst_ref=out_ref.at[_index_of(peer)],
                send_sem=send_sems.at[k - 1], recv_sem=recv_sems.at[k - 1], device_id=peer,
                device_id_type=_MESH).wait_recv()
        for cp in copies:
            cp.wait_send()
        mine.wait()

    return pl.pallas_call(
        body, name=name, out_shape=jax.ShapeDtypeStruct(x.shape, x.dtype),
        in_specs=[_HBM], out_specs=_HBM,
        scratch_shapes=[pltpu.SemaphoreType.DMA((7,)), pltpu.SemaphoreType.DMA((7,)), pltpu.SemaphoreType.DMA],
    )(x)


def _sum8(x, *, name):
    n, R, C = x.shape
    tr = R
    for t in range(8, R, 8):
        if R % t == 0 and t * C * 4 <= (1 << 19):
            tr = t

    def body(x_ref, o_ref):
        acc = x_ref[0].astype(F32)
        for i in range(1, n):
            acc = acc + x_ref[i].astype(F32)
        o_ref[...] = acc

    return pl.pallas_call(body, name=name, out_shape=jax.ShapeDtypeStruct((R, C), F32), grid=(R // tr,),
                          in_specs=[pl.BlockSpec((n, tr, C), lambda i: (0, i, 0))],
                          out_specs=pl.BlockSpec((tr, C), lambda i: (i, 0)),
                          compiler_params=_params(("parallel",)))(x)


def _t5_bucket(dist):
    n = jnp.maximum(dist, 0)
    max_exact = REL_BUCKETS // 2
    nf = jnp.maximum(n, 1).astype(F32)
    large = max_exact + (jnp.log(nf / max_exact) / math.log(REL_MAX_DIST / max_exact)
                         * (REL_BUCKETS - max_exact)).astype(jnp.int32)
    large = jnp.minimum(large, REL_BUCKETS - 1)
    return jnp.where(n < max_exact, n, large)


def _bias_tables(rel_bias):
    qi = jnp.arange(ATT_BLOCK)[:, None]
    ki = jnp.arange(2 * ATT_BLOCK)[None, :]
    rel = qi + ATT_BLOCK - ki
    out = []
    for gi, (window, dil) in enumerate(DIL_GROUPS):
        span = window // dil
        bucket = _t5_bucket(rel * dil)
        valid = (rel >= 0) & (rel <= span)
        tab = rel_bias[:, gi * HEADS_PER_GROUP:(gi + 1) * HEADS_PER_GROUP][bucket]
        tab = jnp.where(valid[:, :, None], tab, MASK_VALUE).transpose(2, 0, 1)
        out.append((tab.astype(F32), bucket.astype(jnp.int32)))
    return out


def _regroup(t, B, S, d):
    C = t.shape[-1]
    return t.reshape(B, S // d, d, C).swapaxes(1, 2)


def _ungroup(t):
    B, d, L, C = t.shape
    return t.swapaxes(1, 2).reshape(B * L * d, C)


def _group_qkv(qkv, gi):
    n = len(DIL_GROUPS) * GROUP_COLS
    return jnp.concatenate([qkv[:, j * n + gi * GROUP_COLS: j * n + (gi + 1) * GROUP_COLS] for j in range(3)], axis=1)


def _layer_fwd(l, x0, h1, mem2, W, P, tabs, next_pre_g, *, B, S):
    tag = f"l{l}_"
    sv = {"x0": x0, "h1": h1}
    za = _mm(h1, W["in_a"], out_dtype=F32, name=tag + "mm_in_a")
    zb = _mm(h1, W["in_b"], out_dtype=F32, name=tag + "mm_in_b")
    qkv = _mm(h1, W["in_c"], out_dtype=BF16, name=tag + "mm_in_c")
    zg = _mm(h1, W["in_g"], out_dtype=F32, name=tag + "mm_in_g")
    a4, a2 = _bra_fwd(za, P["conv_a_w"], P["conv_a_b"], P["ln_a_g"], P["ln_a_b"], B=B, S=S, name=tag + "bra_fwd")
    ya = _mm(a4, W["a_out"], out_dtype=F32, name=tag + "mm_a_out")
    b_s_t = P["b_s"].T
    p = _brb_fwd(zb, P["ln_b_g"], P["ln_b_b"], P["w_s"], b_s_t, name=tag + "brb_fwd")
    yb = _mm(p, W["b_out"], out_dtype=F32, name=tag + "mm_b_out")
    os_, lses, qkv_gs = [], [], []
    for gi, (_, dil) in enumerate(DIL_GROUPS):
        qkv_g = _regroup(_group_qkv(qkv, gi), B, S, dil)
        o_g, lse_g = _attn_fwd(qkv_g, tabs[gi][0], name=tag + f"attn_fwd{gi}")
        qkv_gs.append(qkv_g)
        os_.append(_ungroup(o_g))
        lses.append(_ungroup(lse_g))
    oc, lse_tot = _attn_combine(os_, lses, name=tag + "attn_combine")
    yc = _mm(oc, W["c_out"], out_dtype=F32, name=tag + "mm_c_out")
    merged = _merge_fwd(zg, P["b_gate"], ya, yb, yc, name=tag + "merge_fwd")
    y1 = _mm(merged, W["mix_out"], out_dtype=F32, name=tag + "mm_mix")
    x1, h2 = _norm_fwd(x0, y1, P["mix_post_g"], P["x_pre_g"], name=tag + "norm1")
    q = _mm(h2, W["xq"], out_dtype=BF16, name=tag + "mm_xq")
    (memn,) = _norm_fwd(mem2, None, None, P["mem_g"], name=tag + "norm_mem")
    kv = _mm(memn, W["xkv"], out_dtype=BF16, name=tag + "mm_xkv")
    ox = _xattn_fwd(q, kv, B=B, S=S, name=tag + "xattn_fwd")
    y2 = _mm(ox, W["xo"], out_dtype=F32, name=tag + "mm_xo")
    x2, h3 = _norm_fwd(x1, y2, P["x_post_g"], P["ffn_pre_g"], name=tag + "norm2")
    up = _mm(h3, W["up"], out_dtype=F32, name=tag + "mm_up")
    act = _ffn_fwd(up, P["conv_f_w"], P["conv_f_b"], B=B, S=S, name=tag + "ffn_fwd")
    y3 = _mm(act, W["down"], out_dtype=F32, name=tag + "mm_down")
    outs = _norm_fwd(x2, y3, P["ffn_post_g"], next_pre_g, name=tag + "norm3")
    x3 = outs[0]
    h_next = outs[1] if next_pre_g is not None else None
    sv.update(za=za, zb=zb, zg=zg, a4=a4, a2=a2, ya=ya, p=p, yb=yb, qkv_gs=qkv_gs, oc=oc, lse_tot=lse_tot, yc=yc,
              merged=merged, y1=y1, x1=x1, h2=h2, q=q, memn=memn, kv=kv, ox=ox, y2=y2, x2=x2, h3=h3, up=up, act=act,
              y3=y3, x3=x3, b_s_t=b_s_t)
    return x3, h_next, sv


def _layer_bwd(l, d, dh_next, next_pre_g, sv, mem2, W, WT, P, tabs, *, B, S):
    tag = f"l{l}_"
    G = {}
    if dh_next is not None:
        d, dy3, G["ffn_post_g"], dg_next = _norm_bwd(d, dh_next, sv["x3"], next_pre_g, sv["y3"], P["ffn_post_g"],
                                                     name=tag + "norm3_bwd")
    else:
        dy3, G["ffn_post_g"] = _norm_bwd(d, None, None, None, sv["y3"], P["ffn_post_g"], name=tag + "norm3_bwd")
        dg_next = None
    dact = _mm(dy3, WT["down"], out_dtype=F32, name=tag + "mm_down_dx")
    G["w_down"] = _mm_tn(sv["act"], dy3, name=tag + "mm_down_dw")
    dgp, dval, G["conv_f_w"], G["conv_f_b"] = _ffn_bwd(dact, sv["up"], P["conv_f_w"], P["conv_f_b"], B=B, S=S,
                                                      name=tag + "ffn_bwd")
    dup = jnp.concatenate([dgp, dval], axis=1)
    dh3 = _mm(dup, WT["up"], out_dtype=F32, name=tag + "mm_up_dx")
    G["w_up"] = _mm_tn(sv["h3"], dup, name=tag + "mm_up_dw")
    d, dy2, G["x_post_g"], G["ffn_pre_g"] = _norm_bwd(d, dh3, sv["x2"], P["ffn_pre_g"], sv["y2"], P["x_post_g"],
                                                      name=tag + "norm2_bwd")
    dox = _mm(dy2, WT["xo"], out_dtype=BF16, name=tag + "mm_xo_dx")
    G["w_xo"] = _mm_tn(sv["ox"], dy2, name=tag + "mm_xo_dw")
    dq, dkv = _xattn_bwd(sv["q"], sv["kv"], dox, B=B, S=S, name=tag + "xattn_bwd")
    dh2 = _mm(dq, WT["xq"], out_dtype=F32, name=tag + "mm_xq_dx")
    G["w_xq"] = _mm_tn(sv["h2"], dq, name=tag + "mm_xq_dw")
    G["w_xkv"] = _mm_tn(sv["memn"], dkv, name=tag + "mm_xkv_dw")
    dmemn = _mm(dkv, WT["xkv"], out_dtype=F32, name=tag + "mm_xkv_dx")
    (G["mem_g"],) = _norm_bwd(None, dmemn, mem2, P["mem_g"], None, None, name=tag + "norm_mem_bwd")
    d, dy1, G["mix_post_g"], G["x_pre_g"] = _norm_bwd(d, dh2, sv["x1"], P["x_pre_g"], sv["y1"], P["mix_post_g"],
                                                      name=tag + "norm1_bwd")
    dm = _mm(dy1, WT["mix_out"], out_dtype=F32, name=tag + "mm_mix_dx")
    G["w_mix_out"] = _mm_tn(sv["merged"], dy1, name=tag + "mm_mix_dw")
    dya, dyb, dyc, dzg, dbg = _merge_bwd(dm, sv["zg"], P["b_gate"], sv["ya"], sv["yb"], sv["yc"], name=tag + "merge_bwd")
    G["b_gate"] = dbg.reshape(P["b_gate"].shape)
    da4 = _mm(dya, WT["a_out"], out_dtype=F32, name=tag + "mm_a_out_dx")
    G["w_a_out"] = _mm_tn(sv["a4"], dya, name=tag + "mm_a_out_dw")
    da2, G["ln_a_g"], G["ln_a_b"], G["conv_a_b"] = _bra_bwd_ln(da4, sv["a2"], P["ln_a_g"], P["ln_a_b"], name=tag + "bra_bwd_ln")
    dza, G["conv_a_w"] = _bra_bwd_conv(da2, sv["za"], P["conv_a_w"], B=B, S=S, name=tag + "bra_bwd_conv")
    dp = _mm(dyb, WT["b_out"], out_dtype=F32, name=tag + "mm_b_out_dx")
    G["w_b_out"] = _mm_tn(sv["p"], dyb, name=tag + "mm_b_out_dw")
    dzb, G["w_s"], dbs_t, G["ln_b_g"], G["ln_b_b"] = _brb_bwd(dp, sv["zb"], P["ln_b_g"], P["ln_b_b"], P["w_s"],
                                                             sv["b_s_t"], name=tag + "brb_bwd")
    G["b_s"] = dbs_t.T
    doc = _mm(dyc, WT["c_out"], out_dtype=F32, name=tag + "mm_c_out_dx")
    G["w_c_out"] = _mm_tn(sv["oc"], dyc, name=tag + "mm_c_out_dw")
    dd = _attn_rowdot(doc, sv["oc"], name=tag + "attn_rowdot")
    dq_parts, dk_parts, dv_parts, dbiases = [], [], [], []
    for gi, (_, dil) in enumerate(DIL_GROUPS):
        dqkv_g, dbias = _attn_bwd(sv["qkv_gs"][gi], _regroup(doc, B, S, dil), _regroup(sv["lse_tot"], B, S, dil),
                                  _regroup(dd, B, S, dil), tabs[gi][0], name=tag + f"attn_bwd{gi}")
        t = _ungroup(dqkv_g)
        dq_parts.append(t[:, 0:GROUP_COLS])
        dk_parts.append(t[:, GROUP_COLS:2 * GROUP_COLS])
        dv_parts.append(t[:, 2 * GROUP_COLS:3 * GROUP_COLS])
        dbiases.append(dbias)
    dqkv = jnp.concatenate(dq_parts + dk_parts + dv_parts, axis=1)
    G["dbias"] = jnp.concatenate(dbiases, axis=0)
    dh1 = _mm(dza, WT["in_a"], out_dtype=F32, name=tag + "mm_in_a_dx")
    dh1 = _mm(dzb, WT["in_b"], out_dtype=F32, name=tag + "mm_in_b_dx", add=dh1)
    dh1 = _mm(dqkv, WT["in_c"], out_dtype=F32, name=tag + "mm_in_c_dx", add=dh1)
    dh1 = _mm(dzg, WT["in_g"], out_dtype=F32, name=tag + "mm_in_g_dx", add=dh1)
    h1 = sv["h1"]
    G["w_in"] = jnp.concatenate([_mm_tn(h1, dza, name=tag + "mm_in_a_dw"), _mm_tn(h1, dzb, name=tag + "mm_in_b_dw"),
                                 _mm_tn(h1, dqkv, name=tag + "mm_in_c_dw"), _mm_tn(h1, dzg, name=tag + "mm_in_g_dw")],
                                axis=1)
    return d, dh1, G, dg_next


_COL_SHARDED = ("w_in", "b_gate", "conv_a_w", "w_a_out", "w_b_out", "w_c_out", "w_xkv", "w_up", "conv_f_w")
_ROW_SHARDED = ("w_mix_out", "w_xq", "w_xo", "w_down")
_SHARDED_BIG = ("w_in", "w_a_out", "w_b_out", "w_c_out", "w_mix_out", "w_xq", "w_xkv", "w_xo", "w_up", "w_down")
_SHARDED_SMALL = ("b_gate", "conv_a_w", "conv_f_w")
_REPLICATED = ("mix_pre_g", "mix_post_g", "conv_a_b", "ln_a_g", "ln_a_b", "ln_b_g", "ln_b_b", "w_s", "b_s",
               "x_pre_g", "x_post_g", "mem_g", "ffn_pre_g", "ffn_post_g", "conv_f_b")
_WEIGHTS = ('rel_bias', 'mix_pre_g', 'mix_post_g', 'w_in', 'b_gate', 'conv_a_w', 'conv_a_b', 'ln_a_g', 'ln_a_b',
            'w_a_out', 'ln_b_g', 'ln_b_b', 'w_s', 'b_s', 'w_b_out', 'w_c_out', 'w_mix_out', 'x_pre_g', 'x_post_g',
            'mem_g', 'w_xq', 'w_xkv', 'w_xo', 'ffn_pre_g', 'ffn_post_g', 'w_up', 'conv_f_w', 'conv_f_b', 'w_down')
_PACK_COLS = 1024


def _shard_axis(name):
    return 1 if name in _ROW_SHARDED else 2


def _pack(parts, dtype, row_mult):
    flat = jnp.concatenate([p.astype(dtype).reshape(-1) for p in parts])
    n = flat.shape[0]
    unit = _PACK_COLS * row_mult
    padded = -(-n // unit) * unit
    flat = jnp.pad(flat, (0, padded - n))
    return flat.reshape(padded // _PACK_COLS, _PACK_COLS)


def _unpack(flat, shapes):
    out, off = [], 0
    for shp in shapes:
        n = math.prod(shp)
        out.append(flat[off:off + n].reshape(shp))
        off += n
    return out


def _gather_full(names, shards, dtype, row_mult, *, name):
    packed = _pack([shards[n] for n in names], dtype, row_mult)
    g = _all_gather(packed, name=name).reshape(N_DEV, -1)
    pieces = _unpack_rows(g, [shards[n].shape for n in names])
    full = {}
    for n, piece in zip(names, pieces):
        ax = _shard_axis(n)
        full[n] = jnp.concatenate([piece[i] for i in range(N_DEV)], axis=ax)
    return full


def _unpack_rows(g, shapes):
    out, off = [], 0
    for shp in shapes:
        n = math.prod(shp)
        out.append(g[:, off:off + n].reshape((N_DEV,) + tuple(shp)))
        off += n
    return out


def _split8(full, name):
    ax = _shard_axis(name)
    shp = full.shape
    t = full.reshape(shp[:ax] + (N_DEV, shp[ax] // N_DEV) + shp[ax + 1:])
    return jnp.moveaxis(t, ax, 0)


def kernel(x, mem, rel_bias, mix_pre_g, mix_post_g, w_in, b_gate, conv_a_w, conv_a_b, ln_a_g, ln_a_b, w_a_out, ln_b_g, ln_b_b, w_s, b_s, w_b_out, w_c_out, w_mix_out, x_pre_g, x_post_g, mem_g, w_xq, w_xkv, w_xo, ffn_pre_g, ffn_post_g, w_up, conv_f_w, conv_f_b, w_down, loss_target, m_rel_bias, m_mix_pre_g, m_mix_post_g, m_w_in, m_b_gate, m_conv_a_w, m_conv_a_b, m_ln_a_g, m_ln_a_b, m_w_a_out, m_ln_b_g, m_ln_b_b, m_w_s, m_b_s, m_w_b_out, m_w_c_out, m_w_mix_out, m_x_pre_g, m_x_post_g, m_mem_g, m_w_xq, m_w_xkv, m_w_xo, m_ffn_pre_g, m_ffn_post_g, m_w_up, m_conv_f_w, m_conv_f_b, m_w_down, v_rel_bias, v_mix_pre_g, v_mix_post_g, v_w_in, v_b_gate, v_conv_a_w, v_conv_a_b, v_ln_a_g, v_ln_a_b, v_w_a_out, v_ln_b_g, v_ln_b_b, v_w_s, v_b_s, v_w_b_out, v_w_c_out, v_w_mix_out, v_x_pre_g, v_x_post_g, v_mem_g, v_w_xq, v_w_xkv, v_w_xo, v_ffn_pre_g, v_ffn_post_g, v_w_up, v_conv_f_w, v_conv_f_b, v_w_down):
    args = locals()
    w_loc = {n: args[n] for n in _WEIGHTS}
    m_loc = {n: args["m_" + n] for n in _WEIGHTS}
    v_loc = {n: args["v_" + n] for n in _WEIGHTS}

    big = _gather_full(_SHARDED_BIG, w_loc, BF16, 16, name="gather_weights")
    small = _gather_full(_SHARDED_SMALL, w_loc, F32, 8, name="gather_small")
    full = dict(w_loc)
    full.update(big)
    full.update(small)

    loss_part, grad_x, grads = _local_step(x, mem, loss_target, full)

    sharded = _SHARDED_BIG + _SHARDED_SMALL
    send = jnp.concatenate([_split8(grads[n], n).reshape(N_DEV, -1) for n in sharded], axis=1)
    n_send = send.shape[1]
    unit = _PACK_COLS * 8
    n_pad = -(-n_send // unit) * unit
    send = jnp.pad(send, ((0, 0), (0, n_pad - n_send))).reshape(N_DEV, n_pad // _PACK_COLS, _PACK_COLS)
    recv = _all_to_all(send, name="scatter_grads")
    gsum = _sum8(recv, name="sum_grads").reshape(-1)
    g_loc = dict(zip(sharded, _unpack(gsum, [w_loc[n].shape for n in sharded])))

    rep = ("rel_bias",) + _REPLICATED
    packed = _pack([grads[n] for n in rep], F32, 8)
    allp = _all_gather(packed, name="gather_rep_grads")
    rsum = _sum8(allp, name="sum_rep_grads").reshape(-1)
    g_loc.update(zip(rep, _unpack(rsum, [w_loc[n].shape for n in rep])))

    loss = lax.psum(loss_part, ("x", "y", "c"))

    deltas, new_m, new_v = {}, {}, {}
    for n in _WEIGHTS:
        deltas[n], new_m[n], new_v[n] = _adamw(w_loc[n], g_loc[n], m_loc[n], v_loc[n], name="adamw_" + n)
    return (loss, grad_x, *[g_loc[n] for n in _WEIGHTS], *[deltas[n] for n in _WEIGHTS],
            *[new_m[n] for n in _WEIGHTS], *[new_v[n] for n in _WEIGHTS])


def _local_step(x, mem, loss_target, full):
    B, S, D = x.shape
    depth = full["w_in"].shape[0]
    x2d = x.reshape(B * S, D)
    mem2 = mem.reshape(-1, D)
    tabs = _bias_tables(full["rel_bias"])
    n_a, n_b, n_c = 1024, 1024, 3 * len(DIL_GROUPS) * GROUP_COLS
    Ws, WTs, Ps = [], [], []
    for l in range(depth):
        P = {n: full[n][l] for n in _WEIGHTS if n != "rel_bias"}
        w_in_l = full["w_in"][l].astype(BF16)
        W = {"in_a": w_in_l[:, :n_a], "in_b": w_in_l[:, n_a:n_a + n_b], "in_c": w_in_l[:, n_a + n_b:n_a + n_b + n_c],
             "in_g": w_in_l[:, n_a + n_b + n_c:]}
        for key, nm in (("a_out", "w_a_out"), ("b_out", "w_b_out"), ("c_out", "w_c_out"), ("mix_out", "w_mix_out"),
                        ("xq", "w_xq"), ("xkv", "w_xkv"), ("xo", "w_xo"), ("up", "w_up"), ("down", "w_down")):
            W[key] = full[nm][l].astype(BF16)
        Ws.append(W)
        WTs.append({k: v.T for k, v in W.items()})
        Ps.append(P)

    (h1,) = _norm_fwd(x2d, None, None, Ps[0]["mix_pre_g"], name="norm0")
    xc = x2d
    saved = []
    for l in range(depth):
        nxt = Ps[l + 1]["mix_pre_g"] if l + 1 < depth else None
        xc, h1, sv = _layer_fwd(l, xc, h1, mem2, Ws[l], Ps[l], tabs, nxt, B=B, S=S)
        saved.append(sv)
    loss_vec, d = _loss_kernel(xc, loss_target.reshape(B * S, D), name="loss")

    layer_grads = [None] * depth
    dh_next = None
    for l in reversed(range(depth)):
        nxt = Ps[l + 1]["mix_pre_g"] if l + 1 < depth else None
        d, dh1, G, dg_next = _layer_bwd(l, d, dh_next, nxt, saved[l], mem2, Ws[l], WTs[l], Ps[l], tabs, B=B, S=S)
        if dg_next is not None:
            layer_grads[l + 1]["mix_pre_g"] = dg_next
        layer_grads[l] = G
        dh_next = dh1
    grad_x2d, dg0 = _norm_bwd(d, dh_next, x2d, Ps[0]["mix_pre_g"], None, None, name="norm0_bwd")
    layer_grads[0]["mix_pre_g"] = dg0

    grads = {}
    for n in _WEIGHTS:
        if n == "rel_bias":
            continue
        grads[n] = jnp.stack([layer_grads[l][n].reshape(full[n].shape[1:]) for l in range(depth)], axis=0)
    dbias = jnp.stack([layer_grads[l]["dbias"] for l in range(depth)], axis=0)
    buckets = jnp.stack([t[1] for t in tabs], axis=0)
    rb = _bucket_sum(dbias, buckets, name="rel_bias_grad")
    grads["rel_bias"] = rb[:, :full["rel_bias"].shape[1]]
    return loss_vec[0, 0], grad_x2d.reshape(B, S, D), grads
```

```python
import functools
import math

import jax
import jax.numpy as jnp
from jax import lax
from jax.experimental import pallas as pl
from jax.experimental.pallas import tpu as pltpu

F32 = jnp.float32
BF16 = jnp.bfloat16

N_DEV = 8
NORM_EPS = 1e-6
LN_EPS = 1e-5
CONV_K = 31
SG_CHUNK = 128
ATT_BLOCK = 128
HEAD_DIM = 64
HEADS_PER_GROUP = 4
GROUP_COLS = HEADS_PER_GROUP * HEAD_DIM
DIL_GROUPS = ((128, 1), (512, 4), (2048, 16))
REL_BUCKETS = 32
REL_MAX_DIST = 2048
X_HEADS = 4
ATT_SCALE = HEAD_DIM ** -0.5
MASK_VALUE = -1e30

ADAM_LR = 0.001
ADAM_B1 = 0.9
ADAM_B2 = 0.999
ADAM_EPS = 1e-08
ADAM_WD = 0.01
ADAM_STEP = 10

LANE = 128
ROW_TILE = 512
VMEM_LIMIT = 48 << 20

_NT = (((1,), (1,)), ((), ()))
_TN = (((0,), (0,)), ((), ()))


def _params(sem, vmem=VMEM_LIMIT):
    return pltpu.CompilerParams(dimension_semantics=sem, vmem_limit_bytes=vmem)


def _pick(n, cap):
    if n <= cap:
        return n
    best = None
    for t in range(LANE, cap + 1, LANE):
        if n % t == 0:
            best = t
    assert best is not None, (n, cap)
    return best


def _sigmoid(x):
    return 1.0 / (1.0 + jnp.exp(-x))


_GELU_C = math.sqrt(2.0 / math.pi)


def _gelu(x):
    return 0.5 * x * (1.0 + jnp.tanh(_GELU_C * (x + 0.044715 * x * x * x)))


def _gelu_and_grad(x):
    t = jnp.tanh(_GELU_C * (x + 0.044715 * x * x * x))
    g = 0.5 * x * (1.0 + t)
    dg = 0.5 * (1.0 + t) + 0.5 * x * (1.0 - t * t) * _GELU_C * (1.0 + 3 * 0.044715 * x * x)
    return g, dg


def _mm(a, b, *, out_dtype, name, add=None):
    M, K = a.shape
    K2, N = b.shape
    assert K == K2
    tm = _pick(M, 1024)
    tn = _pick(N, 1536)
    tk = _pick(K, 1536)
    nk = K // tk
    has_add = add is not None

    def body(*refs):
        if has_add:
            a_ref, b_ref, c_ref, o_ref = refs[:4]
            rest = refs[4:]
        else:
            a_ref, b_ref, o_ref = refs[:3]
            c_ref = None
            rest = refs[3:]
        part = jnp.dot(a_ref[...].astype(BF16), b_ref[...].astype(BF16), preferred_element_type=F32)
        if nk == 1:
            if has_add:
                part = part + c_ref[...]
            o_ref[...] = part.astype(o_ref.dtype)
            return
        acc_ref = rest[0]
        k = pl.program_id(2)

        @pl.when(k == 0)
        def _():
            acc_ref[...] = part

        @pl.when(k > 0)
        def _():
            acc_ref[...] += part

        @pl.when(k == nk - 1)
        def _():
            r = acc_ref[...]
            if has_add:
                r = r + c_ref[...]
            o_ref[...] = r.astype(o_ref.dtype)

    in_specs = [pl.BlockSpec((tm, tk), lambda i, j, k: (i, k)), pl.BlockSpec((tk, tn), lambda i, j, k: (k, j))]
    args = [a, b]
    if has_add:
        in_specs.append(pl.BlockSpec((tm, tn), lambda i, j, k: (i, j)))
        args.append(add)
    return pl.pallas_call(
        body, name=name, out_shape=jax.ShapeDtypeStruct((M, N), out_dtype),
        grid=(M // tm, N // tn, nk), in_specs=in_specs,
        out_specs=pl.BlockSpec((tm, tn), lambda i, j, k: (i, j)),
        scratch_shapes=[pltpu.VMEM((tm, tn), F32)] if nk > 1 else [],
        compiler_params=_params(("parallel", "parallel", "arbitrary")),
    )(*args)


def _mm_tn(a, b, *, name):
    T, M = a.shape
    T2, N = b.shape
    assert T == T2
    tm = _pick(M, 1536)
    tn = _pick(N, 1536)
    tt = _pick(T, 1024)
    nt = T // tt

    def body(a_ref, b_ref, o_ref):
        k = pl.program_id(2)
        part = lax.dot_general(a_ref[...].astype(BF16), b_ref[...].astype(BF16), _TN, preferred_element_type=F32)

        @pl.when(k == 0)
        def _():
            o_ref[...] = part

        @pl.when(k > 0)
        def _():
            o_ref[...] += part

    return pl.pallas_call(
        body, name=name, out_shape=jax.ShapeDtypeStruct((M, N), F32),
        grid=(M // tm, N // tn, nt),
        in_specs=[pl.BlockSpec((tt, tm), lambda i, j, k: (k, i)), pl.BlockSpec((tt, tn), lambda i, j, k: (k, j))],
        out_specs=pl.BlockSpec((tm, tn), lambda i, j, k: (i, j)),
        compiler_params=_params(("parallel", "parallel", "arbitrary")),
    )(a, b)


def _rms(x):
    r = lax.rsqrt(jnp.mean(x * x, axis=-1, keepdims=True) + NORM_EPS)
    return x * r, r


def _row_spec(cols, tr=ROW_TILE):
    return pl.BlockSpec((tr, cols), lambda i: (i, 0))


def _vec_spec(cols):
    return pl.BlockSpec((1, cols), lambda i: (0, 0))


def _norm_fwd(x, y, g_post, g_pre, *, name):
    T, D = x.shape
    has_post = y is not None
    has_pre = g_pre is not None

    def body(*refs):
        refs = list(refs)
        x_ref = refs.pop(0)
        xn = x_ref[...]
        if has_post:
            y_ref = refs.pop(0)
            gp_ref = refs.pop(0)
        if has_pre:
            gq_ref = refs.pop(0)
        if has_post:
            yh, _ = _rms(y_ref[...])
            xn = xn + yh * gp_ref[...]
            refs.pop(0)[...] = xn
        if has_pre:
            xh, _ = _rms(xn)
            refs.pop(0)[...] = (xh * gq_ref[...]).astype(BF16)

    args, in_specs, out_shape, out_specs = [x], [_row_spec(D)], [], []
    if has_post:
        args += [y, g_post.reshape(1, D)]
        in_specs += [_row_spec(D), _vec_spec(D)]
        out_shape.append(jax.ShapeDtypeStruct((T, D), F32))
        out_specs.append(_row_spec(D))
    if has_pre:
        args.append(g_pre.reshape(1, D))
        in_specs.append(_vec_spec(D))
        out_shape.append(jax.ShapeDtypeStruct((T, D), BF16))
        out_specs.append(_row_spec(D))
    return pl.pallas_call(body, name=name, out_shape=tuple(out_shape), grid=(T // ROW_TILE,), in_specs=in_specs,
                          out_specs=tuple(out_specs), compiler_params=_params(("parallel",)))(*args)


def _norm_bwd(dres, dh, x_new, g_pre, y, g_post, *, name):
    has_res = dres is not None
    has_pre = dh is not None
    has_post = y is not None
    out_d = has_pre and (has_res or has_post)
    T, D = (dres if has_res else dh).shape
    if not has_res:
        assert not has_post

    def body(*refs):
        refs = list(refs)
        i = pl.program_id(0)
        d = refs.pop(0)[...] if has_res else None
        if has_pre:
            dh_v = refs.pop(0)[...]
            xh, r = _rms(refs.pop(0)[...])
            gq = refs.pop(0)[...]
        if has_post:
            yh, ry = _rms(refs.pop(0)[...])
            gp = refs.pop(0)[...]
        if has_pre:
            dxh = dh_v * gq
            dpre = r * (dxh - xh * jnp.mean(dxh * xh, axis=-1, keepdims=True))
            d = dpre if d is None else d + dpre
            dgq = jnp.sum(dh_v * xh, axis=0, keepdims=True)
        if out_d:
            refs.pop(0)[...] = d
        if has_post:
            dyh = d * gp
            refs.pop(0)[...] = (ry * (dyh - yh * jnp.mean(dyh * yh, axis=-1, keepdims=True))).astype(BF16)
            dgp_ref = refs.pop(0)
            dgp = jnp.sum(d * yh, axis=0, keepdims=True)

            @pl.when(i == 0)
            def _():
                dgp_ref[...] = dgp

            @pl.when(i > 0)
            def _():
                dgp_ref[...] += dgp
        if has_pre:
            dgq_ref = refs.pop(0)

            @pl.when(i == 0)
            def _():
                dgq_ref[...] = dgq

            @pl.when(i > 0)
            def _():
                dgq_ref[...] += dgq

    args, in_specs, out_shape, out_specs = [], [], [], []
    if has_res:
        args.append(dres)
        in_specs.append(_row_spec(D))
    if has_pre:
        args += [dh, x_new, g_pre.reshape(1, D)]
        in_specs += [_row_spec(D), _row_spec(D), _vec_spec(D)]
    if has_post:
        args += [y, g_post.reshape(1, D)]
        in_specs += [_row_spec(D), _vec_spec(D)]
    if out_d:
        out_shape.append(jax.ShapeDtypeStruct((T, D), F32))
        out_specs.append(_row_spec(D))
    if has_post:
        out_shape += [jax.ShapeDtypeStruct((T, D), BF16), jax.ShapeDtypeStruct((1, D), F32)]
        out_specs += [_row_spec(D), _vec_spec(D)]
    if has_pre:
        out_shape.append(jax.ShapeDtypeStruct((1, D), F32))
        out_specs.append(_vec_spec(D))
    return pl.pallas_call(body, name=name, out_shape=tuple(out_shape), grid=(T // ROW_TILE,), in_specs=in_specs,
                          out_specs=tuple(out_specs), compiler_params=_params(("arbitrary",)))(*args)


def _loss_kernel(xf, target, *, name):
    T, D = xf.shape

    def body(x_ref, t_ref, loss_ref, d_ref):
        i = pl.program_id(0)
        e = x_ref[...] - t_ref[...]
        d_ref[...] = e * (1.0 / D)
        part = jnp.sum(jnp.sum(e * e, axis=0, keepdims=True), axis=1, keepdims=True) * (0.5 / D)
        part = jnp.broadcast_to(part, (1, LANE))

        @pl.when(i == 0)
        def _():
            loss_ref[...] = part

        @pl.when(i > 0)
        def _():
            loss_ref[...] += part

    return pl.pallas_call(body, name=name,
                          out_shape=(jax.ShapeDtypeStruct((1, LANE), F32), jax.ShapeDtypeStruct((T, D), F32)),
                          grid=(T // ROW_TILE,), in_specs=[_row_spec(D), _row_spec(D)],
                          out_specs=(_vec_spec(LANE), _row_spec(D)), compiler_params=_params(("arbitrary",)))(xf, target)


CONV_PAD = 32
CONV_CH = 64


def _ln_stats(x):
    mu = jnp.mean(x, axis=-1, keepdims=True)
    xc = x - mu
    rstd = lax.rsqrt(jnp.mean(xc * xc, axis=-1, keepdims=True) + LN_EPS)
    return xc * rstd, rstd


def _acc_out(ref, val, first):
    @pl.when(first)
    def _():
        ref[...] = val

    @pl.when(jnp.logical_not(first))
    def _():
        ref[...] += val


def _bra_fwd(za, conv_w, conv_b, ln_g, ln_b, *, B, S, name):
    C = conv_w.shape[1]

    def body(za_ref, w_ref, cb_ref, g_ref, b_ref, a4_ref, a2_ref, pad_ref):
        pad_ref[0:CONV_PAD, :] = jnp.zeros((CONV_PAD, C), F32)
        pad_ref[CONV_PAD:, :] = za_ref[:, 0:C] * _sigmoid(za_ref[:, C:2 * C])

        def chunk(i, _):
            base = pl.multiple_of(i * CONV_CH, CONV_CH)
            win = pad_ref[pl.ds(base, CONV_CH + CONV_PAD), :]
            acc = jnp.broadcast_to(cb_ref[...], (CONV_CH, C))
            for j in range(CONV_K):
                off = CONV_PAD - (CONV_K - 1) + j
                acc = acc + w_ref[j:j + 1, :] * win[off:off + CONV_CH, :]
            a2_ref[pl.ds(base, CONV_CH), :] = acc
            xh, _ = _ln_stats(acc)
            a3 = xh * g_ref[...] + b_ref[...]
            a4_ref[pl.ds(base, CONV_CH), :] = (a3 * _sigmoid(a3)).astype(BF16)
            return 0

        lax.fori_loop(0, S // CONV_CH, chunk, 0)

    vec = pl.BlockSpec((1, C), lambda b: (0, 0))
    return pl.pallas_call(
        body, name=name,
        out_shape=(jax.ShapeDtypeStruct((B * S, C), BF16), jax.ShapeDtypeStruct((B * S, C), F32)),
        grid=(B,),
        in_specs=[pl.BlockSpec((S, 2 * C), lambda b: (b, 0)), pl.BlockSpec((CONV_K, C), lambda b: (0, 0)), vec, vec, vec],
        out_specs=(pl.BlockSpec((S, C), lambda b: (b, 0)), pl.BlockSpec((S, C), lambda b: (b, 0))),
        scratch_shapes=[pltpu.VMEM((S + CONV_PAD, C), F32)],
        compiler_params=_params(("parallel",)),
    )(za, conv_w, conv_b.reshape(1, C), ln_g.reshape(1, C), ln_b.reshape(1, C))


def _bra_bwd_ln(da4, a2, ln_g, ln_b, *, name):
    T, C = a2.shape

    def body(da4_ref, a2_ref, g_ref, b_ref, da2_ref, dg_ref, db_ref, dcb_ref):
        first = pl.program_id(0) == 0
        xh, rstd = _ln_stats(a2_ref[...])
        a3 = xh * g_ref[...] + b_ref[...]
        sg = _sigmoid(a3)
        da3 = da4_ref[...] * (sg * (1.0 + a3 * (1.0 - sg)))
        dxh = da3 * g_ref[...]
        da2 = rstd * (dxh - jnp.mean(dxh, axis=-1, keepdims=True) - xh * jnp.mean(dxh * xh, axis=-1, keepdims=True))
        da2_ref[...] = da2
        _acc_out(dg_ref, jnp.sum(da3 * xh, axis=0, keepdims=True), first)
        _acc_out(db_ref, jnp.sum(da3, axis=0, keepdims=True), first)
        _acc_out(dcb_ref, jnp.sum(da2, axis=0, keepdims=True), first)

    vec = _vec_spec(C)
    vshape = jax.ShapeDtypeStruct((1, C), F32)
    return pl.pallas_call(body, name=name, out_shape=(jax.ShapeDtypeStruct((T, C), F32), vshape, vshape, vshape),
                          grid=(T // ROW_TILE,), in_specs=[_row_spec(C), _row_spec(C), vec, vec],
                          out_specs=(_row_spec(C), vec, vec, vec),
                          compiler_params=_params(("arbitrary",)))(da4, a2, ln_g.reshape(1, C), ln_b.reshape(1, C))


def _bra_bwd_conv(da2, za, conv_w, *, B, S, name):
    C = conv_w.shape[1]
    SUB = 8

    def body(da2_ref, za_ref, w_ref, dza_ref, dw_ref, pad1_ref, pad2_ref, dwacc_ref):
        first = pl.program_id(0) == 0
        pad1_ref[0:CONV_PAD, :] = jnp.zeros((CONV_PAD, C), F32)
        pad1_ref[CONV_PAD:, :] = za_ref[:, 0:C] * _sigmoid(za_ref[:, C:2 * C])
        pad2_ref[0:S, :] = da2_ref[...]
        pad2_ref[S:, :] = jnp.zeros((CONV_PAD, C), F32)
        dwacc_ref[...] = jnp.zeros_like(dwacc_ref)

        def chunk(i, _):
            base = pl.multiple_of(i * CONV_CH, CONV_CH)
            rows = pl.ds(base, CONV_CH)
            win1 = pad1_ref[pl.ds(base, CONV_CH + CONV_PAD), :]
            win2 = pad2_ref[pl.ds(base, CONV_CH + CONV_PAD), :]
            d = win2[0:CONV_CH, :]
            da1 = jnp.zeros((CONV_CH, C), F32)
            for j in range(CONV_K):
                off2 = CONV_K - 1 - j
                da1 = da1 + w_ref[j:j + 1, :] * win2[off2:off2 + CONV_CH, :]
                off1 = CONV_PAD - (CONV_K - 1) + j
                prod = d * win1[off1:off1 + CONV_CH, :]
                dwacc_ref[j * SUB:(j + 1) * SUB, :] += jnp.sum(prod.reshape(CONV_CH // SUB, SUB, C), axis=0)
            a_val = za_ref[rows, 0:C]
            sg = _sigmoid(za_ref[rows, C:2 * C])
            dza_ref[rows, 0:C] = (da1 * sg).astype(BF16)
            dza_ref[rows, C:2 * C] = (da1 * a_val * sg * (1.0 - sg)).astype(BF16)
            return 0

        lax.fori_loop(0, S // CONV_CH, chunk, 0)
        _acc_out(dw_ref, jnp.sum(dwacc_ref[...].reshape(CONV_K, SUB, C), axis=1), first)

    return pl.pallas_call(
        body, name=name,
        out_shape=(jax.ShapeDtypeStruct((B * S, 2 * C), BF16), jax.ShapeDtypeStruct((CONV_K, C), F32)),
        grid=(B,),
        in_specs=[pl.BlockSpec((S, C), lambda b: (b, 0)), pl.BlockSpec((S, 2 * C), lambda b: (b, 0)),
                  pl.BlockSpec((CONV_K, C), lambda b: (0, 0))],
        out_specs=(pl.BlockSpec((S, 2 * C), lambda b: (b, 0)), pl.BlockSpec((CONV_K, C), lambda b: (0, 0))),
        scratch_shapes=[pltpu.VMEM((S + CONV_PAD, C), F32), pltpu.VMEM((S + CONV_PAD, C), F32),
                        pltpu.VMEM((CONV_K * SUB, C), F32)],
        compiler_params=_params(("arbitrary",)),
    )(da2, za, conv_w)


def _tril_mask():
    r = lax.broadcasted_iota(jnp.int32, (SG_CHUNK, SG_CHUNK), 0)
    c = lax.broadcasted_iota(jnp.int32, (SG_CHUNK, SG_CHUNK), 1)
    return c <= r


def _brb_fwd(zb, ln_g, ln_b, w_s, b_s_t, *, name):
    T, C2 = zb.shape
    C = C2 // 2
    G = w_s.shape[0]
    GC = C // G

    def body(zb_ref, g_ref, b_ref, ws_ref, bs_ref, p_ref):
        z = _gelu(zb_ref[...])
        u = z[:, 0:C]
        xh, _ = _ln_stats(z[:, C:2 * C])
        v1 = (xh * g_ref[...] + b_ref[...]).astype(BF16)
        mask = _tril_mask()
        outs = []
        for gi in range(G):
            ws = jnp.where(mask, ws_ref[gi], 0.0).astype(BF16)
            v2 = jnp.dot(ws, v1[:, gi * GC:(gi + 1) * GC], preferred_element_type=F32) + bs_ref[:, gi:gi + 1]
            outs.append(v2)
        p_ref[...] = (u * jnp.concatenate(outs, axis=1)).astype(BF16)

    return pl.pallas_call(
        body, name=name, out_shape=jax.ShapeDtypeStruct((T, C), BF16), grid=(T // SG_CHUNK,),
        in_specs=[_row_spec(C2, SG_CHUNK), _vec_spec(C), _vec_spec(C),
                  pl.BlockSpec((G, SG_CHUNK, SG_CHUNK), lambda i: (0, 0, 0)), pl.BlockSpec((SG_CHUNK, G), lambda i: (0, 0))],
        out_specs=_row_spec(C, SG_CHUNK), compiler_params=_params(("parallel",)),
    )(zb, ln_g.reshape(1, C), ln_b.reshape(1, C), w_s, b_s_t)


def _brb_bwd(dp, zb, ln_g, ln_b, w_s, b_s_t, *, name):
    T, C2 = zb.shape
    C = C2 // 2
    G = w_s.shape[0]
    GC = C // G

    def body(dp_ref, zb_ref, g_ref, b_ref, ws_ref, bs_ref, dzb_ref, dws_ref, dbs_ref, dg_ref, db_ref):
        first = pl.program_id(0) == 0
        z, dz = _gelu_and_grad(zb_ref[...])
        u = z[:, 0:C]
        xh, rstd = _ln_stats(z[:, C:2 * C])
        v1 = (xh * g_ref[...] + b_ref[...]).astype(BF16)
        dp_v = dp_ref[...]
        mask = _tril_mask()
        v2s, dv1s, dwss, dbss = [], [], [], []
        for gi in range(G):
            cols = slice(gi * GC, (gi + 1) * GC)
            ws = jnp.where(mask, ws_ref[gi], 0.0).astype(BF16)
            v2s.append(jnp.dot(ws, v1[:, cols], preferred_element_type=F32) + bs_ref[:, gi:gi + 1])
            dv2 = dp_v[:, cols] * u[:, cols]
            dv2b = dv2.astype(BF16)
            dbss.append(jnp.sum(dv2, axis=1, keepdims=True))
            dwss.append(jnp.where(mask, lax.dot_general(dv2b, v1[:, cols], _NT, preferred_element_type=F32), 0.0))
            dv1s.append(lax.dot_general(ws, dv2b, _TN, preferred_element_type=F32))
        du = dp_v * jnp.concatenate(v2s, axis=1)
        dv1 = jnp.concatenate(dv1s, axis=1)
        dxh = dv1 * g_ref[...]
        dv0 = rstd * (dxh - jnp.mean(dxh, axis=-1, keepdims=True) - xh * jnp.mean(dxh * xh, axis=-1, keepdims=True))
        dzb_ref[:, 0:C] = (du * dz[:, 0:C]).astype(BF16)
        dzb_ref[:, C:2 * C] = (dv0 * dz[:, C:2 * C]).astype(BF16)
        _acc_out(dws_ref, jnp.stack(dwss, axis=0), first)
        _acc_out(dbs_ref, jnp.concatenate(dbss, axis=1), first)
        _acc_out(dg_ref, jnp.sum(dv1 * xh, axis=0, keepdims=True), first)
        _acc_out(db_ref, jnp.sum(dv1, axis=0, keepdims=True), first)

    wspec = pl.BlockSpec((G, SG_CHUNK, SG_CHUNK), lambda i: (0, 0, 0))
    bspec = pl.BlockSpec((SG_CHUNK, G), lambda i: (0, 0))
    return pl.pallas_call(
        body, name=name,
        out_shape=(jax.ShapeDtypeStruct((T, C2), BF16), jax.ShapeDtypeStruct((G, SG_CHUNK, SG_CHUNK), F32),
                   jax.ShapeDtypeStruct((SG_CHUNK, G), F32), jax.ShapeDtypeStruct((1, C), F32),
                   jax.ShapeDtypeStruct((1, C), F32)),
        grid=(T // SG_CHUNK,),
        in_specs=[_row_spec(C, SG_CHUNK), _row_spec(C2, SG_CHUNK), _vec_spec(C), _vec_spec(C), wspec, bspec],
        out_specs=(_row_spec(C2, SG_CHUNK), wspec, bspec, _vec_spec(C), _vec_spec(C)),
        compiler_params=_params(("arbitrary",)),
    )(dp, zb, ln_g.reshape(1, C), ln_b.reshape(1, C), w_s, b_s_t)


def _attn_fwd(qkv_g, bias, *, name):
    B, d, L, _ = qkv_g.shape
    nb = L // ATT_BLOCK
    GCOL = GROUP_COLS

    def body(qkv_ref, bias_ref, o_ref, lse_ref):
        def blk(r0, first):
            nk = ATT_BLOCK if first else 2 * ATT_BLOCK
            k0 = r0 if first else r0 - ATT_BLOCK
            qb = qkv_ref[pl.ds(r0, ATT_BLOCK), 0:GCOL]
            kb = qkv_ref[pl.ds(k0, nk), GCOL:2 * GCOL]
            vb = qkv_ref[pl.ds(k0, nk), 2 * GCOL:3 * GCOL]
            outs, lses = [], []
            for h in range(HEADS_PER_GROUP):
                sl = slice(h * HEAD_DIM, (h + 1) * HEAD_DIM)
                bh = bias_ref[h, :, ATT_BLOCK:2 * ATT_BLOCK] if first else bias_ref[h]
                s = lax.dot_general(qb[:, sl], kb[:, sl], _NT, preferred_element_type=F32) * ATT_SCALE + bh
                m = jnp.max(s, axis=1, keepdims=True)
                e = jnp.exp(s - m)
                ssum = jnp.sum(e, axis=1, keepdims=True)
                outs.append(jnp.dot(e.astype(BF16), vb[:, sl], preferred_element_type=F32) / ssum)
                lses.append(jnp.broadcast_to(m + jnp.log(ssum), (ATT_BLOCK, HEAD_DIM)))
            o_ref[pl.ds(r0, ATT_BLOCK), :] = jnp.concatenate(outs, axis=1)
            lse_ref[pl.ds(r0, ATT_BLOCK), :] = jnp.concatenate(lses, axis=1)

        blk(0, True)
        if nb > 1:
            def loop(n, _):
                blk(pl.multiple_of(n * ATT_BLOCK, ATT_BLOCK), False)
                return 0

            lax.fori_loop(1, nb, loop, 0)

    spec = lambda cols: pl.BlockSpec((None, None, L, cols), lambda b, c: (b, c, 0, 0))
    oshape = jax.ShapeDtypeStruct((B, d, L, GCOL), F32)
    return pl.pallas_call(
        body, name=name, out_shape=(oshape, oshape), grid=(B, d),
        in_specs=[spec(3 * GCOL), pl.BlockSpec((HEADS_PER_GROUP, ATT_BLOCK, 2 * ATT_BLOCK), lambda b, c: (0, 0, 0))],
        out_specs=(spec(GCOL), spec(GCOL)), compiler_params=_params(("parallel", "parallel")),
    )(qkv_g, bias)


def _attn_bwd(qkv_g, doc_g, lse_g, dd_g, bias, *, name):
    B, d, L, _ = qkv_g.shape
    nb = L // ATT_BLOCK
    GCOL = GROUP_COLS

    def body(qkv_ref, doc_ref, lse_ref, dd_ref, bias_ref, dqkv_ref, dbias_ref, dk_ref, dv_ref):
        @pl.when(jnp.logical_and(pl.program_id(0) == 0, pl.program_id(1) == 0))
        def _():
            dbias_ref[...] = jnp.zeros_like(dbias_ref)

        dk_ref[...] = jnp.zeros_like(dk_ref)
        dv_ref[...] = jnp.zeros_like(dv_ref)

        def blk(r0, first):
            nk = ATT_BLOCK if first else 2 * ATT_BLOCK
            k0 = r0 if first else r0 - ATT_BLOCK
            rows = pl.ds(r0, ATT_BLOCK)
            krows = pl.ds(k0, nk)
            qb = qkv_ref[rows, 0:GCOL]
            kb = qkv_ref[krows, GCOL:2 * GCOL]
            vb = qkv_ref[krows, 2 * GCOL:3 * GCOL]
            dob = doc_ref[rows, :].astype(BF16)
            lse = lse_ref[rows, :]
            dd = dd_ref[rows, :]
            dqs, dks, dvs = [], [], []
            for h in range(HEADS_PER_GROUP):
                sl = slice(h * HEAD_DIM, (h + 1) * HEAD_DIM)
                c0 = h * HEAD_DIM
                bh = bias_ref[h, :, ATT_BLOCK:2 * ATT_BLOCK] if first else bias_ref[h]
                s = lax.dot_general(qb[:, sl], kb[:, sl], _NT, preferred_element_type=F32) * ATT_SCALE + bh
                p = jnp.exp(s - lse[:, c0:c0 + 1])
                dp = lax.dot_general(dob[:, sl], vb[:, sl], _NT, preferred_element_type=F32)
                ds = p * (dp - dd[:, c0:c0 + 1])
                if first:
                    dbias_ref[h, :, ATT_BLOCK:2 * ATT_BLOCK] += ds
                else:
                    dbias_ref[h] += ds
                dsb = ds.astype(BF16)
                dqs.append(jnp.dot(dsb, kb[:, sl], preferred_element_type=F32) * ATT_SCALE)
                dks.append(lax.dot_general(dsb, qb[:, sl], _TN, preferred_element_type=F32) * ATT_SCALE)
                dvs.append(lax.dot_general(p.astype(BF16), dob[:, sl], _TN, preferred_element_type=F32))
            dqkv_ref[rows, 0:GCOL] = jnp.concatenate(dqs, axis=1).astype(BF16)
            dk_ref[krows, :] += jnp.concatenate(dks, axis=1)
            dv_ref[krows, :] += jnp.concatenate(dvs, axis=1)

        blk(0, True)
        if nb > 1:
            def loop(n, _):
                blk(pl.multiple_of(n * ATT_BLOCK, ATT_BLOCK), False)
                return 0

            lax.fori_loop(1, nb, loop, 0)
        dqkv_ref[:, GCOL:2 * GCOL] = dk_ref[...].astype(BF16)
        dqkv_ref[:, 2 * GCOL:3 * GCOL] = dv_ref[...].astype(BF16)

    spec = lambda cols: pl.BlockSpec((None, None, L, cols), lambda b, c: (b, c, 0, 0))
    bspec = pl.BlockSpec((HEADS_PER_GROUP, ATT_BLOCK, 2 * ATT_BLOCK), lambda b, c: (0, 0, 0))
    return pl.pallas_call(
        body, name=name,
        out_shape=(jax.ShapeDtypeStruct((B, d, L, 3 * GCOL), BF16),
                   jax.ShapeDtypeStruct((HEADS_PER_GROUP, ATT_BLOCK, 2 * ATT_BLOCK), F32)),
        grid=(B, d),
        in_specs=[spec(3 * GCOL), spec(GCOL), spec(GCOL), spec(GCOL), bspec],
        out_specs=(spec(3 * GCOL), bspec),
        scratch_shapes=[pltpu.VMEM((L, GCOL), F32), pltpu.VMEM((L, GCOL), F32)],
        compiler_params=_params(("arbitrary", "arbitrary")),
    )(qkv_g, doc_g, lse_g, dd_g, bias)


def _attn_combine(os_, lses, *, name):
    T, GC = os_[0].shape
    n = len(os_)

    def body(*refs):
        o_refs, l_refs, oc_ref, lt_ref = refs[:n], refs[n:2 * n], refs[2 * n], refs[2 * n + 1]
        ls = [r[...] for r in l_refs]
        m = functools.reduce(jnp.maximum, ls)
        ws = [jnp.exp(l - m) for l in ls]
        tot = functools.reduce(jnp.add, ws)
        acc = functools.reduce(jnp.add, [w * r[...] for w, r in zip(ws, o_refs)])
        oc_ref[...] = acc / tot
        lt_ref[...] = m + jnp.log(tot)

    shp = jax.ShapeDtypeStruct((T, GC), F32)
    return pl.pallas_call(body, name=name, out_shape=(shp, shp), grid=(T // ROW_TILE,),
                          in_specs=[_row_spec(GC)] * (2 * n), out_specs=(_row_spec(GC), _row_spec(GC)),
                          compiler_params=_params(("parallel",)))(*os_, *lses)


def _attn_rowdot(doc, oc, *, name):
    T, GC = oc.shape

    def body(doc_ref, oc_ref, dd_ref):
        prod = doc_ref[...] * oc_ref[...]
        parts = []
        for h in range(GC // HEAD_DIM):
            s = jnp.sum(prod[:, h * HEAD_DIM:(h + 1) * HEAD_DIM], axis=1, keepdims=True)
            parts.append(jnp.broadcast_to(s, (ROW_TILE, HEAD_DIM)))
        dd_ref[...] = jnp.concatenate(parts, axis=1)

    return pl.pallas_call(body, name=name, out_shape=jax.ShapeDtypeStruct((T, GC), F32), grid=(T // ROW_TILE,),
                          in_specs=[_row_spec(GC), _row_spec(GC)], out_specs=_row_spec(GC),
                          compiler_params=_params(("parallel",)))(doc, oc)


def _bucket_sum(dbias, buckets, *, name):
    depth, NH = dbias.shape[:2]

    def body(db_ref, bk_ref, out_ref):
        rows = lax.broadcasted_iota(jnp.int32, (REL_BUCKETS, LANE), 0)
        cols = lax.broadcasted_iota(jnp.int32, (REL_BUCKETS, LANE), 1)

        def per_bucket(b, acc):
            for h in range(NH):
                sel = bk_ref[h // HEADS_PER_GROUP] == b
                tot = functools.reduce(jnp.add, [db_ref[l, h] for l in range(depth)])
                s = jnp.sum(jnp.where(sel, tot, 0.0))
                acc = acc + jnp.where(jnp.logical_and(rows == b, cols == h), s, 0.0)
            return acc

        out_ref[...] = lax.fori_loop(0, REL_BUCKETS, per_bucket, jnp.zeros((REL_BUCKETS, LANE), F32))

    return pl.pallas_call(body, name=name, out_shape=jax.ShapeDtypeStruct((REL_BUCKETS, LANE), F32),
                          compiler_params=pltpu.CompilerParams(vmem_limit_bytes=VMEM_LIMIT))(dbias, buckets)


def _merge_fwd(zg, bg, ya, yb, yc, *, name):
    T, D3 = zg.shape
    D = D3 // 3

    def body(zg_ref, bg_ref, ya_ref, yb_ref, yc_ref, out_ref):
        acc = None
        for i, y_ref in enumerate((ya_ref, yb_ref, yc_ref)):
            g = _sigmoid(zg_ref[:, i * D:(i + 1) * D] + bg_ref[:, i * D:(i + 1) * D])
            t = g * y_ref[...]
            acc = t if acc is None else acc + t
        out_ref[...] = acc.astype(BF16)

    return pl.pallas_call(body, name=name, out_shape=jax.ShapeDtypeStruct((T, D), BF16), grid=(T // ROW_TILE,),
                          in_specs=[_row_spec(D3), _vec_spec(D3), _row_spec(D), _row_spec(D), _row_spec(D)],
                          out_specs=_row_spec(D), compiler_params=_params(("parallel",)))(zg, bg.reshape(1, D3), ya, yb, yc)


def _merge_bwd(dm, zg, bg, ya, yb, yc, *, name):
    T, D3 = zg.shape
    D = D3 // 3

    def body(dm_ref, zg_ref, bg_ref, ya_ref, yb_ref, yc_ref, dya_ref, dyb_ref, dyc_ref, dzg_ref, dbg_ref):
        first = pl.program_id(0) == 0
        dm_v = dm_ref[...]
        dbs = []
        for i, (y_ref, dy_ref) in enumerate(((ya_ref, dya_ref), (yb_ref, dyb_ref), (yc_ref, dyc_ref))):
            g = _sigmoid(zg_ref[:, i * D:(i + 1) * D] + bg_ref[:, i * D:(i + 1) * D])
            dy_ref[...] = (dm_v * g).astype(BF16)
            dz = dm_v * y_ref[...] * g * (1.0 - g)
            dzg_ref[:, i * D:(i + 1) * D] = dz.astype(BF16)
            dbs.append(jnp.sum(dz, axis=0, keepdims=True))
        _acc_out(dbg_ref, jnp.concatenate(dbs, axis=1), first)

    bshape = jax.ShapeDtypeStruct((T, D), BF16)
    return pl.pallas_call(
        body, name=name,
        out_shape=(bshape, bshape, bshape, jax.ShapeDtypeStruct((T, D3), BF16), jax.ShapeDtypeStruct((1, D3), F32)),
        grid=(T // ROW_TILE,),
        in_specs=[_row_spec(D), _row_spec(D3), _vec_spec(D3), _row_spec(D), _row_spec(D), _row_spec(D)],
        out_specs=(_row_spec(D), _row_spec(D), _row_spec(D), _row_spec(D3), _vec_spec(D3)),
        compiler_params=_params(("arbitrary",)))(dm, zg, bg.reshape(1, D3), ya, yb, yc)


XQ_TILE = 512


def _xattn_fwd(q, kv, *, B, S, name):
    D = q.shape[1]
    M = kv.shape[0] // B
    E = D // X_HEADS
    scale = E ** -0.5

    def body(q_ref, kv_ref, o_ref):
        outs = []
        for h in range(X_HEADS):
            s = lax.dot_general(q_ref[:, h * E:(h + 1) * E], kv_ref[:, h * E:(h + 1) * E], _NT,
                                preferred_element_type=F32) * scale
            e = jnp.exp(s - jnp.max(s, axis=1, keepdims=True))
            p = e / jnp.sum(e, axis=1, keepdims=True)
            outs.append(jnp.dot(p.astype(BF16), kv_ref[:, D + h * E:D + (h + 1) * E], preferred_element_type=F32))
        o_ref[...] = jnp.concatenate(outs, axis=1).astype(BF16)

    nq = S // XQ_TILE
    return pl.pallas_call(
        body, name=name, out_shape=jax.ShapeDtypeStruct((B * S, D), BF16), grid=(B, nq),
        in_specs=[pl.BlockSpec((XQ_TILE, D), lambda b, i: (b * nq + i, 0)), pl.BlockSpec((M, 2 * D), lambda b, i: (b, 0))],
        out_specs=pl.BlockSpec((XQ_TILE, D), lambda b, i: (b * nq + i, 0)),
        compiler_params=_params(("parallel", "parallel")))(q, kv)


def _xattn_bwd(q, kv, do, *, B, S, name):
    D = q.shape[1]
    M = kv.shape[0] // B
    E = D // X_HEADS
    scale = E ** -0.5

    def body(q_ref, kv_ref, do_ref, dq_ref, dkv_ref):
        first = pl.program_id(1) == 0
        dqs, dks, dvs = [], [], []
        for h in range(X_HEADS):
            qh = q_ref[:, h * E:(h + 1) * E]
            kh = kv_ref[:, h * E:(h + 1) * E]
            vh = kv_ref[:, D + h * E:D + (h + 1) * E]
            doh = do_ref[:, h * E:(h + 1) * E]
            s = lax.dot_general(qh, kh, _NT, preferred_element_type=F32) * scale
            e = jnp.exp(s - jnp.max(s, axis=1, keepdims=True))
            p = e / jnp.sum(e, axis=1, keepdims=True)
            dp = lax.dot_general(doh, vh, _NT, preferred_element_type=F32)
            ds = (p * (dp - jnp.sum(p * dp, axis=1, keepdims=True))).astype(BF16)
            dqs.append(jnp.dot(ds, kh, preferred_element_type=F32) * scale)
            dks.append(lax.dot_general(ds, qh, _TN, preferred_element_type=F32) * scale)
            dvs.append(lax.dot_general(p.astype(BF16), doh, _TN, preferred_element_type=F32))
        dq_ref[...] = jnp.concatenate(dqs, axis=1).astype(BF16)
        _acc_out(dkv_ref, jnp.concatenate(dks + dvs, axis=1), first)

    nq = S // XQ_TILE
    qspec = pl.BlockSpec((XQ_TILE, D), lambda b, i: (b * nq + i, 0))
    kvspec = pl.BlockSpec((M, 2 * D), lambda b, i: (b, 0))
    return pl.pallas_call(
        body, name=name,
        out_shape=(jax.ShapeDtypeStruct((B * S, D), BF16), jax.ShapeDtypeStruct((B * M, 2 * D), F32)),
        grid=(B, nq), in_specs=[qspec, kvspec, qspec], out_specs=(qspec, kvspec),
        compiler_params=_params(("arbitrary", "arbitrary")))(q, kv, do)


FFN_COLS = 256
FFN_PAD = 8
FFN_CH = 256
FFN_K = 3


def _ffn_gate(win, w_ref, cb_ref, n):
    g = jnp.broadcast_to(cb_ref[...], (n, win.shape[1]))
    for j in range(FFN_K):
        off = FFN_PAD - (FFN_K - 1) + j
        g = g + w_ref[j:j + 1, :] * win[off:off + n, :]
    return g


def _ffn_fwd(up, conv_w, conv_b, *, B, S, name):
    F = conv_w.shape[1]
    nc = F // FFN_COLS

    def body(gp_ref, val_ref, w_ref, cb_ref, act_ref, pad_ref):
        pad_ref[0:FFN_PAD, :] = jnp.zeros((FFN_PAD, FFN_COLS), F32)
        pad_ref[FFN_PAD:, :] = gp_ref[...]

        def chunk(i, _):
            base = pl.multiple_of(i * FFN_CH, FFN_CH)
            gate = _ffn_gate(pad_ref[pl.ds(base, FFN_CH + FFN_PAD), :], w_ref, cb_ref, FFN_CH)
            act_ref[pl.ds(base, FFN_CH), :] = (_gelu(gate) * val_ref[pl.ds(base, FFN_CH), :]).astype(BF16)
            return 0

        lax.fori_loop(0, S // FFN_CH, chunk, 0)

    return pl.pallas_call(
        body, name=name, out_shape=jax.ShapeDtypeStruct((B * S, F), BF16), grid=(B, nc),
        in_specs=[pl.BlockSpec((S, FFN_COLS), lambda b, j: (b, j)), pl.BlockSpec((S, FFN_COLS), lambda b, j: (b, nc + j)),
                  pl.BlockSpec((FFN_K, FFN_COLS), lambda b, j: (0, j)), pl.BlockSpec((1, FFN_COLS), lambda b, j: (0, j))],
        out_specs=pl.BlockSpec((S, FFN_COLS), lambda b, j: (b, j)),
        scratch_shapes=[pltpu.VMEM((S + FFN_PAD, FFN_COLS), F32)],
        compiler_params=_params(("parallel", "parallel")))(up, up, conv_w, conv_b.reshape(1, F))


def _ffn_bwd(dact, up, conv_w, conv_b, *, B, S, name):
    F = conv_w.shape[1]
    nc = F // FFN_COLS
    SUB = 8

    def body(dact_ref, gp_ref, val_ref, w_ref, cb_ref, dgp_ref, dval_ref, dw_ref, dcb_ref, pad_ref, pad2_ref, acc_ref):
        first = pl.program_id(1) == 0
        pad_ref[0:FFN_PAD, :] = jnp.zeros((FFN_PAD, FFN_COLS), F32)
        pad_ref[FFN_PAD:, :] = gp_ref[...]
        pad2_ref[S:, :] = jnp.zeros((FFN_PAD, FFN_COLS), F32)
        acc_ref[...] = jnp.zeros_like(acc_ref)

        def chunk1(i, _):
            base = pl.multiple_of(i * FFN_CH, FFN_CH)
            rows = pl.ds(base, FFN_CH)
            gate = _ffn_gate(pad_ref[pl.ds(base, FFN_CH + FFN_PAD), :], w_ref, cb_ref, FFN_CH)
            gl, dgl = _gelu_and_grad(gate)
            da = dact_ref[rows, :]
            dval_ref[rows, :] = (da * gl).astype(BF16)
            pad2_ref[rows, :] = da * val_ref[rows, :] * dgl
            return 0

        lax.fori_loop(0, S // FFN_CH, chunk1, 0)

        def chunk2(i, _):
            base = pl.multiple_of(i * FFN_CH, FFN_CH)
            rows = pl.ds(base, FFN_CH)
            win2 = pad2_ref[pl.ds(base, FFN_CH + FFN_PAD), :]
            win1 = pad_ref[pl.ds(base, FFN_CH + FFN_PAD), :]
            dg = win2[0:FFN_CH, :]
            dgp = jnp.zeros((FFN_CH, FFN_COLS), F32)
            for j in range(FFN_K):
                off2 = FFN_K - 1 - j
                dgp = dgp + w_ref[j:j + 1, :] * win2[off2:off2 + FFN_CH, :]
                off1 = FFN_PAD - (FFN_K - 1) + j
                prod = dg * win1[off1:off1 + FFN_CH, :]
                acc_ref[j * SUB:(j + 1) * SUB, :] += jnp.sum(prod.reshape(FFN_CH // SUB, SUB, FFN_COLS), axis=0)
            acc_ref[FFN_K * SUB:(FFN_K + 1) * SUB, :] += jnp.sum(dg.reshape(FFN_CH // SUB, SUB, FFN_COLS), axis=0)
            dgp_ref[rows, :] = dgp.astype(BF16)
            return 0

        lax.fori_loop(0, S // FFN_CH, chunk2, 0)
        sums = jnp.sum(acc_ref[...].reshape(FFN_K + 1, SUB, FFN_COLS), axis=1)
        _acc_out(dw_ref, sums[0:FFN_K, :], first)
        _acc_out(dcb_ref, sums[FFN_K:FFN_K + 1, :], first)

    blk = lambda off: pl.BlockSpec((S, FFN_COLS), lambda j, b: (b, off + j))
    return pl.pallas_call(
        body, name=name,
        out_shape=(jax.ShapeDtypeStruct((B * S, F), BF16), jax.ShapeDtypeStruct((B * S, F), BF16),
                   jax.ShapeDtypeStruct((FFN_K, F), F32), jax.ShapeDtypeStruct((1, F), F32)),
        grid=(nc, B),
        in_specs=[blk(0), blk(0), blk(nc), pl.BlockSpec((FFN_K, FFN_COLS), lambda j, b: (0, j)),
                  pl.BlockSpec((1, FFN_COLS), lambda j, b: (0, j))],
        out_specs=(blk(0), blk(0), pl.BlockSpec((FFN_K, FFN_COLS), lambda j, b: (0, j)),
                   pl.BlockSpec((1, FFN_COLS), lambda j, b: (0, j))),
        scratch_shapes=[pltpu.VMEM((S + FFN_PAD, FFN_COLS), F32), pltpu.VMEM((S + FFN_PAD, FFN_COLS), F32),
                        pltpu.VMEM(((FFN_K + 1) * SUB, FFN_COLS), F32)],
        compiler_params=_params(("arbitrary", "arbitrary")))(dact, up, up, conv_w, conv_b.reshape(1, F))


def _row_tile(rows, row_bytes, budget):
    tr = rows
    if rows * row_bytes > budget:
        for t in range(16, rows, 16):
            if rows % t == 0 and t * row_bytes <= budget:
                tr = t
    return tr


def _adamw(w, parts, m, v, *, name):
    shape = w.shape
    n = parts.shape[0]
    cols = shape[-1]
    rows = w.size // cols
    w2, m2, v2 = (t.reshape(rows, cols) for t in (w, m, v))
    p3 = parts.reshape(n, rows, cols)
    tr = _row_tile(rows, cols * 4, 1 << 19)
    c1 = 1.0 - ADAM_B1 ** ADAM_STEP
    c2 = 1.0 - ADAM_B2 ** ADAM_STEP

    def body(w_ref, p_ref, m_ref, v_ref, g_ref, d_ref, mo_ref, vo_ref):
        gv = p_ref[0].astype(F32)
        for i in range(1, n):
            gv = gv + p_ref[i].astype(F32)
        g_ref[...] = gv
        mn = ADAM_B1 * m_ref[...] + (1.0 - ADAM_B1) * gv
        vn = ADAM_B2 * v_ref[...] + (1.0 - ADAM_B2) * (gv * gv)
        d_ref[...] = -ADAM_LR * ((mn / c1) / (jnp.sqrt(vn / c2) + ADAM_EPS) + ADAM_WD * w_ref[...])
        mo_ref[...] = mn
        vo_ref[...] = vn

    spec = pl.BlockSpec((tr, cols), lambda i: (i, 0))
    pspec = pl.BlockSpec((n, tr, cols), lambda i: (0, i, 0))
    oshape = jax.ShapeDtypeStruct((rows, cols), F32)
    outs = pl.pallas_call(body, name=name, out_shape=(oshape,) * 4, grid=(rows // tr,),
                          in_specs=[spec, pspec, spec, spec], out_specs=(spec,) * 4,
                          compiler_params=_params(("parallel",)))(w2, p3, m2, v2)
    return tuple(t.reshape(shape) for t in outs)


def _add2(a, b, *, name):
    shape = a.shape
    cols = shape[-1]
    rows = a.size // cols
    tr = _row_tile(rows, cols * 4, 1 << 20)

    def body(a_ref, b_ref, o_ref):
        o_ref[...] = (a_ref[...].astype(F32) + b_ref[...].astype(F32)).astype(o_ref.dtype)

    spec = pl.BlockSpec((tr, cols), lambda i: (i, 0))
    out = pl.pallas_call(body, name=name, out_shape=jax.ShapeDtypeStruct((rows, cols), a.dtype), grid=(rows // tr,),
                         in_specs=[spec, spec], out_specs=spec,
                         compiler_params=_params(("parallel",)))(a.reshape(rows, cols), b.reshape(rows, cols))
    return out.reshape(shape)


_HBM = pl.BlockSpec(memory_space=pltpu.HBM)
_MESH = pl.DeviceIdType.MESH


def _my_pos():
    return lax.axis_index("x"), lax.axis_index("y"), lax.axis_index("c")


def _flip(pos, k):
    x, y, c = pos
    fx, fy, fc = (k >> 2) & 1, (k >> 1) & 1, k & 1
    return (x ^ fx if fx else x, y ^ fy if fy else y, c ^ fc if fc else c)


def _index_of(pos):
    return 4 * pos[0] + 2 * pos[1] + pos[2]


def _all_gather(xs, *, name):
    n = len(xs)

    def body(*refs):
        x_refs, out_refs = refs[:n], refs[n:2 * n]
        send_sems, recv_sems, local_sems = refs[2 * n:]
        me = _my_pos()
        sibling = _flip(me, 1)
        chips = [2, 4, 6]

        def copy(i, k, block_pos, to, from_x=False):
            blk = out_refs[i].at[_index_of(block_pos)]
            return pltpu.make_async_remote_copy(src_ref=x_refs[i] if from_x else blk, dst_ref=blk,
                                                send_sem=send_sems.at[k, i], recv_sem=recv_sems.at[k, i],
                                                device_id=to, device_id_type=_MESH)

        mine = [pltpu.make_async_copy(x_refs[i], out_refs[i].at[_index_of(me)], local_sems.at[i]) for i in range(n)]
        first = [copy(i, 1 + j, me, _flip(me, f), from_x=True) for j, f in enumerate(chips) for i in range(n)]
        first += [copy(i, 0, me, sibling, from_x=True) for i in range(n)]
        for cp in first + mine:
            cp.start()
        passed = []
        for j, f in enumerate(chips):
            for i in range(n):
                copy(i, 1 + j, _flip(me, f), me).wait_recv()
                cp = copy(i, 4 + j, _flip(me, f), sibling)
                cp.start()
                passed.append(cp)
        for i in range(n):
            copy(i, 0, sibling, me).wait_recv()
        for j, f in enumerate(chips):
            for i in range(n):
                copy(i, 4 + j, _flip(sibling, f), me).wait_recv()
        for cp in first + passed:
            cp.wait_send()
        for cp in mine:
            cp.wait()

    return pl.pallas_call(
        body, name=name, out_shape=tuple(jax.ShapeDtypeStruct((N_DEV,) + x.shape, x.dtype) for x in xs),
        in_specs=[_HBM] * n, out_specs=(_HBM,) * n,
        scratch_shapes=[pltpu.SemaphoreType.DMA((7, n)), pltpu.SemaphoreType.DMA((7, n)), pltpu.SemaphoreType.DMA((n,))],
    )(*xs)


N_CHIP = 4


def _scatter_d2d(gs, *, name):
    n = len(gs)

    def body(*refs):
        g_refs, own_refs, recv_refs = refs[:n], refs[n:2 * n], refs[2 * n:3 * n]
        send_sems, recv_sems, local_sems = refs[3 * n:]
        me = _my_pos()
        sibling = _flip(me, 1)
        c = me[2]
        sends, locals_ = [], []
        for i in range(n):
            for s in range(N_CHIP):
                sends.append(pltpu.make_async_remote_copy(
                    src_ref=g_refs[i].at[2 * s + 1 - c], dst_ref=recv_refs[i].at[s], send_sem=send_sems.at[s, i],
                    recv_sem=recv_sems.at[s, i], device_id=sibling, device_id_type=_MESH))
                locals_.append(pltpu.make_async_copy(g_refs[i].at[2 * s + c], own_refs[i].at[s], local_sems.at[s, i]))
        for cp in sends + locals_:
            cp.start()
        for cp in sends:
            cp.wait_recv()
        for cp in sends:
            cp.wait_send()
        for cp in locals_:
            cp.wait()

    shapes = tuple(jax.ShapeDtypeStruct((N_CHIP,) + g.shape[1:], g.dtype) for g in gs)
    outs = pl.pallas_call(
        body, name=name, out_shape=shapes + shapes, in_specs=[_HBM] * n, out_specs=(_HBM,) * (2 * n),
        scratch_shapes=[pltpu.SemaphoreType.DMA((N_CHIP, n)), pltpu.SemaphoreType.DMA((N_CHIP, n)),
                        pltpu.SemaphoreType.DMA((N_CHIP, n))],
    )(*gs)
    return outs[:n], outs[n:]


def _scatter_ici(ss, *, name):
    n = len(ss)

    def body(*refs):
        s_refs, r_refs = refs[:n], refs[n:2 * n]
        send_sems, recv_sems, local_sems = refs[2 * n:]
        me = _my_pos()
        my_chip = 2 * me[0] + me[1]
        locals_ = [pltpu.make_async_copy(s_refs[i].at[my_chip], r_refs[i].at[my_chip], local_sems.at[i])
                   for i in range(n)]
        sends, recvs = [], []
        for k in (1, 2, 3):
            peer = _flip(me, 2 * k)
            peer_chip = 2 * peer[0] + peer[1]
            for i in range(n):
                sends.append(pltpu.make_async_remote_copy(
                    src_ref=s_refs[i].at[peer_chip], dst_ref=r_refs[i].at[my_chip], send_sem=send_sems.at[k - 1, i],
                    recv_sem=recv_sems.at[k - 1, i], device_id=peer, device_id_type=_MESH))
                recvs.append(pltpu.make_async_remote_copy(
                    src_ref=s_refs[i].at[my_chip], dst_ref=r_refs[i].at[peer_chip], send_sem=send_sems.at[k - 1, i],
                    recv_sem=recv_sems.at[k - 1, i], device_id=peer, device_id_type=_MESH))
        for cp in sends + locals_:
            cp.start()
        for cp in recvs:
            cp.wait_recv()
        for cp in sends:
            cp.wait_send()
        for cp in locals_:
            cp.wait()

    return pl.pallas_call(
        body, name=name, out_shape=tuple(jax.ShapeDtypeStruct(s.shape, s.dtype) for s in ss),
        in_specs=[_HBM] * n, out_specs=(_HBM,) * n,
        scratch_shapes=[pltpu.SemaphoreType.DMA((3, n)), pltpu.SemaphoreType.DMA((3, n)), pltpu.SemaphoreType.DMA((n,))],
    )(*ss)


def _t5_bucket(dist):
    n = jnp.maximum(dist, 0)
    max_exact = REL_BUCKETS // 2
    nf = jnp.maximum(n, 1).astype(F32)
    large = max_exact + (jnp.log(nf / max_exact) / math.log(REL_MAX_DIST / max_exact)
                         * (REL_BUCKETS - max_exact)).astype(jnp.int32)
    large = jnp.minimum(large, REL_BUCKETS - 1)
    return jnp.where(n < max_exact, n, large)


def _bias_tables(rel_bias):
    qi = jnp.arange(ATT_BLOCK)[:, None]
    ki = jnp.arange(2 * ATT_BLOCK)[None, :]
    rel = qi + ATT_BLOCK - ki
    out = []
    for gi, (window, dil) in enumerate(DIL_GROUPS):
        span = window // dil
        bucket = _t5_bucket(rel * dil)
        valid = (rel >= 0) & (rel <= span)
        rb = rel_bias[:, gi * HEADS_PER_GROUP:(gi + 1) * HEADS_PER_GROUP]
        tab = functools.reduce(jnp.add, [jnp.where((bucket == b)[:, :, None], rb[b], 0.0)
                                         for b in range(REL_BUCKETS)])
        tab = jnp.where(valid[:, :, None], tab, MASK_VALUE).transpose(2, 0, 1)
        out.append((tab.astype(F32), bucket.astype(jnp.int32)))
    return out


def _regroup(t, B, S, d):
    C = t.shape[-1]
    return t.reshape(B, S // d, d, C).swapaxes(1, 2)


def _ungroup(t):
    B, d, L, C = t.shape
    return t.swapaxes(1, 2).reshape(B * L * d, C)


def _group_qkv(qkv, gi):
    n = len(DIL_GROUPS) * GROUP_COLS
    return jnp.concatenate([qkv[:, j * n + gi * GROUP_COLS: j * n + (gi + 1) * GROUP_COLS] for j in range(3)], axis=1)


def _layer_fwd(l, x0, h1, mem2, W, P, tabs, next_pre_g, *, B, S):
    tag = f"l{l}_"
    sv = {"x0": x0, "h1": h1}
    za = _mm(h1, W["in_a"], out_dtype=F32, name=tag + "mm_in_a")
    zb = _mm(h1, W["in_b"], out_dtype=F32, name=tag + "mm_in_b")
    qkv = _mm(h1, W["in_c"], out_dtype=BF16, name=tag + "mm_in_c")
    zg = _mm(h1, W["in_g"], out_dtype=F32, name=tag + "mm_in_g")
    a4, a2 = _bra_fwd(za, P["conv_a_w"], P["conv_a_b"], P["ln_a_g"], P["ln_a_b"], B=B, S=S, name=tag + "bra_fwd")
    ya = _mm(a4, W["a_out"], out_dtype=F32, name=tag + "mm_a_out")
    b_s_t = P["b_s"].T
    p = _brb_fwd(zb, P["ln_b_g"], P["ln_b_b"], P["w_s"], b_s_t, name=tag + "brb_fwd")
    yb = _mm(p, W["b_out"], out_dtype=F32, name=tag + "mm_b_out")
    os_, lses, qkv_gs = [], [], []
    for gi, (_, dil) in enumerate(DIL_GROUPS):
        qkv_g = _regroup(_group_qkv(qkv, gi), B, S, dil)
        o_g, lse_g = _attn_fwd(qkv_g, tabs[gi][0], name=tag + f"attn_fwd{gi}")
        qkv_gs.append(qkv_g)
        os_.append(_ungroup(o_g))
        lses.append(_ungroup(lse_g))
    oc, lse_tot = _attn_combine(os_, lses, name=tag + "attn_combine")
    yc = _mm(oc, W["c_out"], out_dtype=F32, name=tag + "mm_c_out")
    merged = _merge_fwd(zg, P["b_gate"], ya, yb, yc, name=tag + "merge_fwd")
    y1 = _mm(merged, W["mix_out"], out_dtype=F32, name=tag + "mm_mix")
    x1, h2 = _norm_fwd(x0, y1, P["mix_post_g"], P["x_pre_g"], name=tag + "norm1")
    q = _mm(h2, W["xq"], out_dtype=BF16, name=tag + "mm_xq")
    (memn,) = _norm_fwd(mem2, None, None, P["mem_g"], name=tag + "norm_mem")
    kv = _mm(memn, W["xkv"], out_dtype=BF16, name=tag + "mm_xkv")
    ox = _xattn_fwd(q, kv, B=B, S=S, name=tag + "xattn_fwd")
    y2 = _mm(ox, W["xo"], out_dtype=F32, name=tag + "mm_xo")
    x2, h3 = _norm_fwd(x1, y2, P["x_post_g"], P["ffn_pre_g"], name=tag + "norm2")
    up = _mm(h3, W["up"], out_dtype=F32, name=tag + "mm_up")
    act = _ffn_fwd(up, P["conv_f_w"], P["conv_f_b"], B=B, S=S, name=tag + "ffn_fwd")
    y3 = _mm(act, W["down"], out_dtype=F32, name=tag + "mm_down")
    outs = _norm_fwd(x2, y3, P["ffn_post_g"], next_pre_g, name=tag + "norm3")
    x3 = outs[0]
    h_next = outs[1] if next_pre_g is not None else None
    sv.update(za=za, zb=zb, zg=zg, a4=a4, a2=a2, ya=ya, p=p, yb=yb, qkv_gs=qkv_gs, oc=oc, lse_tot=lse_tot, yc=yc,
              merged=merged, y1=y1, x1=x1, h2=h2, q=q, memn=memn, kv=kv, ox=ox, y2=y2, x2=x2, h3=h3, up=up, act=act,
              y3=y3, x3=x3, b_s_t=b_s_t)
    return x3, h_next, sv


def _layer_bwd(l, d, dh_next, next_pre_g, sv, mem2, W, WT, P, tabs, *, B, S):
    tag = f"l{l}_"
    G = {}
    if dh_next is not None:
        d, dy3, G["ffn_post_g"], dg_next = _norm_bwd(d, dh_next, sv["x3"], next_pre_g, sv["y3"], P["ffn_post_g"],
                                                     name=tag + "norm3_bwd")
    else:
        dy3, G["ffn_post_g"] = _norm_bwd(d, None, None, None, sv["y3"], P["ffn_post_g"], name=tag + "norm3_bwd")
        dg_next = None
    dact = _mm(dy3, WT["down"], out_dtype=F32, name=tag + "mm_down_dx")
    G["w_down"] = _mm_tn(sv["act"], dy3, name=tag + "mm_down_dw")
    dgp, dval, G["conv_f_w"], G["conv_f_b"] = _ffn_bwd(dact, sv["up"], P["conv_f_w"], P["conv_f_b"], B=B, S=S,
                                                      name=tag + "ffn_bwd")
    dup = jnp.concatenate([dgp, dval], axis=1)
    dh3 = _mm(dup, WT["up"], out_dtype=F32, name=tag + "mm_up_dx")
    G["w_up"] = _mm_tn(sv["h3"], dup, name=tag + "mm_up_dw")
    d, dy2, G["x_post_g"], G["ffn_pre_g"] = _norm_bwd(d, dh3, sv["x2"], P["ffn_pre_g"], sv["y2"], P["x_post_g"],
                                                      name=tag + "norm2_bwd")
    dox = _mm(dy2, WT["xo"], out_dtype=BF16, name=tag + "mm_xo_dx")
    G["w_xo"] = _mm_tn(sv["ox"], dy2, name=tag + "mm_xo_dw")
    dq, dkv = _xattn_bwd(sv["q"], sv["kv"], dox, B=B, S=S, name=tag + "xattn_bwd")
    dh2 = _mm(dq, WT["xq"], out_dtype=F32, name=tag + "mm_xq_dx")
    G["w_xq"] = _mm_tn(sv["h2"], dq, name=tag + "mm_xq_dw")
    G["w_xkv"] = _mm_tn(sv["memn"], dkv, name=tag + "mm_xkv_dw")
    dmemn = _mm(dkv, WT["xkv"], out_dtype=F32, name=tag + "mm_xkv_dx")
    (G["mem_g"],) = _norm_bwd(None, dmemn, mem2, P["mem_g"], None, None, name=tag + "norm_mem_bwd")
    d, dy1, G["mix_post_g"], G["x_pre_g"] = _norm_bwd(d, dh2, sv["x1"], P["x_pre_g"], sv["y1"], P["mix_post_g"],
                                                      name=tag + "norm1_bwd")
    dm = _mm(dy1, WT["mix_out"], out_dtype=F32, name=tag + "mm_mix_dx")
    G["w_mix_out"] = _mm_tn(sv["merged"], dy1, name=tag + "mm_mix_dw")
    dya, dyb, dyc, dzg, dbg = _merge_bwd(dm, sv["zg"], P["b_gate"], sv["ya"], sv["yb"], sv["yc"], name=tag + "merge_bwd")
    G["b_gate"] = dbg.reshape(P["b_gate"].shape)
    da4 = _mm(dya, WT["a_out"], out_dtype=F32, name=tag + "mm_a_out_dx")
    G["w_a_out"] = _mm_tn(sv["a4"], dya, name=tag + "mm_a_out_dw")
    da2, G["ln_a_g"], G["ln_a_b"], G["conv_a_b"] = _bra_bwd_ln(da4, sv["a2"], P["ln_a_g"], P["ln_a_b"], name=tag + "bra_bwd_ln")
    dza, G["conv_a_w"] = _bra_bwd_conv(da2, sv["za"], P["conv_a_w"], B=B, S=S, name=tag + "bra_bwd_conv")
    dp = _mm(dyb, WT["b_out"], out_dtype=F32, name=tag + "mm_b_out_dx")
    G["w_b_out"] = _mm_tn(sv["p"], dyb, name=tag + "mm_b_out_dw")
    dzb, G["w_s"], dbs_t, G["ln_b_g"], G["ln_b_b"] = _brb_bwd(dp, sv["zb"], P["ln_b_g"], P["ln_b_b"], P["w_s"],
                                                             sv["b_s_t"], name=tag + "brb_bwd")
    G["b_s"] = dbs_t.T
    doc = _mm(dyc, WT["c_out"], out_dtype=F32, name=tag + "mm_c_out_dx")
    G["w_c_out"] = _mm_tn(sv["oc"], dyc, name=tag + "mm_c_out_dw")
    dd = _attn_rowdot(doc, sv["oc"], name=tag + "attn_rowdot")
    dq_parts, dk_parts, dv_parts, dbiases = [], [], [], []
    for gi, (_, dil) in enumerate(DIL_GROUPS):
        dqkv_g, dbias = _attn_bwd(sv["qkv_gs"][gi], _regroup(doc, B, S, dil), _regroup(sv["lse_tot"], B, S, dil),
                                  _regroup(dd, B, S, dil), tabs[gi][0], name=tag + f"attn_bwd{gi}")
        t = _ungroup(dqkv_g)
        dq_parts.append(t[:, 0:GROUP_COLS])
        dk_parts.append(t[:, GROUP_COLS:2 * GROUP_COLS])
        dv_parts.append(t[:, 2 * GROUP_COLS:3 * GROUP_COLS])
        dbiases.append(dbias)
    dqkv = jnp.concatenate(dq_parts + dk_parts + dv_parts, axis=1)
    G["dbias"] = jnp.concatenate(dbiases, axis=0)
    dh1 = _mm(dza, WT["in_a"], out_dtype=F32, name=tag + "mm_in_a_dx")
    dh1 = _mm(dzb, WT["in_b"], out_dtype=F32, name=tag + "mm_in_b_dx", add=dh1)
    dh1 = _mm(dqkv, WT["in_c"], out_dtype=F32, name=tag + "mm_in_c_dx", add=dh1)
    dh1 = _mm(dzg, WT["in_g"], out_dtype=F32, name=tag + "mm_in_g_dx", add=dh1)
    h1 = sv["h1"]
    G["w_in"] = jnp.concatenate([_mm_tn(h1, dza, name=tag + "mm_in_a_dw"), _mm_tn(h1, dzb, name=tag + "mm_in_b_dw"),
                                 _mm_tn(h1, dqkv, name=tag + "mm_in_c_dw"), _mm_tn(h1, dzg, name=tag + "mm_in_g_dw")],
                                axis=1)
    return d, dh1, G, dg_next


_COL_SHARDED = ("w_in", "b_gate", "conv_a_w", "w_a_out", "w_b_out", "w_c_out", "w_xkv", "w_up", "conv_f_w")
_ROW_SHARDED = ("w_mix_out", "w_xq", "w_xo", "w_down")
_SHARDED_BIG = ("w_in", "w_a_out", "w_b_out", "w_c_out", "w_mix_out", "w_xq", "w_xkv", "w_xo", "w_up", "w_down")
_SHARDED_SMALL = ("b_gate", "conv_a_w", "conv_f_w")
_REPLICATED = ("mix_pre_g", "mix_post_g", "conv_a_b", "ln_a_g", "ln_a_b", "ln_b_g", "ln_b_b", "w_s", "b_s",
               "x_pre_g", "x_post_g", "mem_g", "ffn_pre_g", "ffn_post_g", "conv_f_b")
_WEIGHTS = ('rel_bias', 'mix_pre_g', 'mix_post_g', 'w_in', 'b_gate', 'conv_a_w', 'conv_a_b', 'ln_a_g', 'ln_a_b',
            'w_a_out', 'ln_b_g', 'ln_b_b', 'w_s', 'b_s', 'w_b_out', 'w_c_out', 'w_mix_out', 'x_pre_g', 'x_post_g',
            'mem_g', 'w_xq', 'w_xkv', 'w_xo', 'ffn_pre_g', 'ffn_post_g', 'w_up', 'conv_f_w', 'conv_f_b', 'w_down')
_PACK_COLS = 1024


def _shard_axis(name):
    return 1 if name in _ROW_SHARDED else 2


def _pack(parts, dtype, row_mult):
    flat = jnp.concatenate([p.astype(dtype).reshape(-1) for p in parts])
    n = flat.shape[0]
    unit = _PACK_COLS * row_mult
    padded = -(-n // unit) * unit
    flat = jnp.pad(flat, (0, padded - n))
    return flat.reshape(padded // _PACK_COLS, _PACK_COLS)


def _unpack(flat, shapes):
    out, off = [], 0
    for shp in shapes:
        n = math.prod(shp)
        out.append(flat[off:off + n].reshape(shp))
        off += n
    return out


def _join8(blocks, name):
    ax = _shard_axis(name)
    t = jnp.moveaxis(blocks, 0, ax)
    shp = t.shape
    return t.reshape(shp[:ax] + (shp[ax] * shp[ax + 1],) + shp[ax + 2:])


def _split8(full, name):
    ax = _shard_axis(name)
    shp = full.shape
    t = full.reshape(shp[:ax] + (N_DEV, shp[ax] // N_DEV) + shp[ax + 1:])
    return jnp.moveaxis(t, ax, 0)


def kernel(x, mem, rel_bias, mix_pre_g, mix_post_g, w_in, b_gate, conv_a_w, conv_a_b, ln_a_g, ln_a_b, w_a_out, ln_b_g, ln_b_b, w_s, b_s, w_b_out, w_c_out, w_mix_out, x_pre_g, x_post_g, mem_g, w_xq, w_xkv, w_xo, ffn_pre_g, ffn_post_g, w_up, conv_f_w, conv_f_b, w_down, loss_target, m_rel_bias, m_mix_pre_g, m_mix_post_g, m_w_in, m_b_gate, m_conv_a_w, m_conv_a_b, m_ln_a_g, m_ln_a_b, m_w_a_out, m_ln_b_g, m_ln_b_b, m_w_s, m_b_s, m_w_b_out, m_w_c_out, m_w_mix_out, m_x_pre_g, m_x_post_g, m_mem_g, m_w_xq, m_w_xkv, m_w_xo, m_ffn_pre_g, m_ffn_post_g, m_w_up, m_conv_f_w, m_conv_f_b, m_w_down, v_rel_bias, v_mix_pre_g, v_mix_post_g, v_w_in, v_b_gate, v_conv_a_w, v_conv_a_b, v_ln_a_g, v_ln_a_b, v_w_a_out, v_ln_b_g, v_ln_b_b, v_w_s, v_b_s, v_w_b_out, v_w_c_out, v_w_mix_out, v_x_pre_g, v_x_post_g, v_mem_g, v_w_xq, v_w_xkv, v_w_xo, v_ffn_pre_g, v_ffn_post_g, v_w_up, v_conv_f_w, v_conv_f_b, v_w_down):
    args = locals()
    w_loc = {n: args[n] for n in _WEIGHTS}
    m_loc = {n: args["m_" + n] for n in _WEIGHTS}
    v_loc = {n: args["v_" + n] for n in _WEIGHTS}

    sharded = _SHARDED_BIG + _SHARDED_SMALL
    wire = {n: (BF16 if n in _SHARDED_BIG else F32) for n in sharded}
    gathered = _all_gather([w_loc[n].astype(wire[n]) for n in sharded], name="gather_weights")
    full = dict(w_loc)
    full.update({n: _join8(g, n) for n, g in zip(sharded, gathered)})

    loss_part, grad_x, grads = _local_step(x, mem, loss_target, full)

    send = [_split8(grads[n], n).astype(wire[n]) for n in sharded]
    own, got = _scatter_d2d(send, name="scatter_grads_d2d")
    chip_sums = [_add2(a, b, name="chip_sum_" + n) for n, a, b in zip(sharded, own, got)]
    g_parts = dict(zip(sharded, _scatter_ici(chip_sums, name="scatter_grads_ici")))

    rep = ("rel_bias",) + _REPLICATED
    packed = _pack([grads[n] for n in rep], F32, 8)
    (allp,) = _all_gather([packed], name="gather_rep_grads")
    allp = allp.reshape(N_DEV, -1)
    off = 0
    for n in rep:
        size = math.prod(w_loc[n].shape)
        g_parts[n] = allp[:, off:off + size].reshape((N_DEV,) + w_loc[n].shape)
        off += size

    loss = lax.psum(loss_part, ("x", "y", "c"))

    g_loc, deltas, new_m, new_v = {}, {}, {}, {}
    for n in _WEIGHTS:
        g_loc[n], deltas[n], new_m[n], new_v[n] = _adamw(w_loc[n], g_parts[n], m_loc[n], v_loc[n], name="adamw_" + n)
    return (loss, grad_x, *[g_loc[n] for n in _WEIGHTS], *[deltas[n] for n in _WEIGHTS],
            *[new_m[n] for n in _WEIGHTS], *[new_v[n] for n in _WEIGHTS])


def _local_step(x, mem, loss_target, full):
    B, S, D = x.shape
    depth = full["w_in"].shape[0]
    x2d = x.reshape(B * S, D)
    mem2 = mem.reshape(-1, D)
    tabs = _bias_tables(full["rel_bias"])
    n_a, n_b, n_c = 1024, 1024, 3 * len(DIL_GROUPS) * GROUP_COLS
    Ws, WTs, Ps = [], [], []
    for l in range(depth):
        P = {n: full[n][l] for n in _WEIGHTS if n != "rel_bias"}
        w_in_l = full["w_in"][l].astype(BF16)
        W = {"in_a": w_in_l[:, :n_a], "in_b": w_in_l[:, n_a:n_a + n_b], "in_c": w_in_l[:, n_a + n_b:n_a + n_b + n_c],
             "in_g": w_in_l[:, n_a + n_b + n_c:]}
        for key, nm in (("a_out", "w_a_out"), ("b_out", "w_b_out"), ("c_out", "w_c_out"), ("mix_out", "w_mix_out"),
                        ("xq", "w_xq"), ("xkv", "w_xkv"), ("xo", "w_xo"), ("up", "w_up"), ("down", "w_down")):
            W[key] = full[nm][l].astype(BF16)
        Ws.append(W)
        WTs.append({k: v.T for k, v in W.items()})
        Ps.append(P)

    (h1,) = _norm_fwd(x2d, None, None, Ps[0]["mix_pre_g"], name="norm0")
    xc = x2d
    saved = []
    for l in range(depth):
        nxt = Ps[l + 1]["mix_pre_g"] if l + 1 < depth else None
        xc, h1, sv = _layer_fwd(l, xc, h1, mem2, Ws[l], Ps[l], tabs, nxt, B=B, S=S)
        saved.append(sv)
    loss_vec, d = _loss_kernel(xc, loss_target.reshape(B * S, D), name="loss")

    layer_grads = [None] * depth
    dh_next = None
    for l in reversed(range(depth)):
        nxt = Ps[l + 1]["mix_pre_g"] if l + 1 < depth else None
        d, dh1, G, dg_next = _layer_bwd(l, d, dh_next, nxt, saved[l], mem2, Ws[l], WTs[l], Ps[l], tabs, B=B, S=S)
        if dg_next is not None:
            layer_grads[l + 1]["mix_pre_g"] = dg_next
        layer_grads[l] = G
        dh_next = dh1
    grad_x2d, dg0 = _norm_bwd(d, dh_next, x2d, Ps[0]["mix_pre_g"], None, None, name="norm0_bwd")
    layer_grads[0]["mix_pre_g"] = dg0

    grads = {}
    for n in _WEIGHTS:
        if n == "rel_bias":
            continue
        grads[n] = jnp.stack([layer_grads[l][n].reshape(full[n].shape[1:]) for l in range(depth)], axis=0)
    dbias = jnp.stack([layer_grads[l]["dbias"] for l in range(depth)], axis=0)
    buckets = jnp.stack([t[1] for t in tabs], axis=0)
    rb = _bucket_sum(dbias, buckets, name="rel_bias_grad")
    grads["rel_bias"] = rb[:, :full["rel_bias"].shape[1]]
    return loss_vec[0, 0], grad_x2d.reshape(B, S, D), grads
```

```python
import functools
import math

import jax
import jax.numpy as jnp
from jax import lax
from jax.experimental import pallas as pl
from jax.experimental.pallas import tpu as pltpu

F32 = jnp.float32
BF16 = jnp.bfloat16

N_DEV = 8
NORM_EPS = 1e-6
LN_EPS = 1e-5
CONV_K = 31
SG_CHUNK = 128
ATT_BLOCK = 128
HEAD_DIM = 64
HEADS_PER_GROUP = 4
GROUP_COLS = HEADS_PER_GROUP * HEAD_DIM
DIL_GROUPS = ((128, 1), (512, 4), (2048, 16))
REL_BUCKETS = 32
REL_MAX_DIST = 2048
X_HEADS = 4
ATT_SCALE = HEAD_DIM ** -0.5
MASK_VALUE = -1e30

ADAM_LR = 0.001
ADAM_B1 = 0.9
ADAM_B2 = 0.999
ADAM_EPS = 1e-08
ADAM_WD = 0.01
ADAM_STEP = 10

LANE = 128
ACT = BF16
ROW_TILE = 512
VMEM_LIMIT = 48 << 20

_NT = (((1,), (1,)), ((), ()))
_TN = (((0,), (0,)), ((), ()))


def _params(sem, vmem=VMEM_LIMIT):
    return pltpu.CompilerParams(dimension_semantics=sem, vmem_limit_bytes=vmem)


def _pick(n, cap):
    if n <= cap:
        return n
    best = None
    for t in range(LANE, cap + 1, LANE):
        if n % t == 0:
            best = t
    assert best is not None, (n, cap)
    return best


def _sigmoid(x):
    return 1.0 / (1.0 + jnp.exp(-x))


_GELU_C = math.sqrt(2.0 / math.pi)


def _gelu(x):
    return 0.5 * x * (1.0 + jnp.tanh(_GELU_C * (x + 0.044715 * x * x * x)))


def _gelu_and_grad(x):
    t = jnp.tanh(_GELU_C * (x + 0.044715 * x * x * x))
    g = 0.5 * x * (1.0 + t)
    dg = 0.5 * (1.0 + t) + 0.5 * x * (1.0 - t * t) * _GELU_C * (1.0 + 3 * 0.044715 * x * x)
    return g, dg


def _mm(a, b, *, out_dtype, name, add=None):
    M, K = a.shape
    K2, N = b.shape
    assert K == K2
    tm = _pick(M, 1024)
    tn = _pick(N, 1536)
    tk = _pick(K, 1536)
    nk = K // tk
    has_add = add is not None

    def body(*refs):
        if has_add:
            a_ref, b_ref, c_ref, o_ref = refs[:4]
            rest = refs[4:]
        else:
            a_ref, b_ref, o_ref = refs[:3]
            c_ref = None
            rest = refs[3:]
        part = jnp.dot(a_ref[...].astype(BF16), b_ref[...].astype(BF16), preferred_element_type=F32)
        if nk == 1:
            if has_add:
                part = part + c_ref[...]
            o_ref[...] = part.astype(o_ref.dtype)
            return
        acc_ref = rest[0]
        k = pl.program_id(2)

        @pl.when(k == 0)
        def _():
            acc_ref[...] = part

        @pl.when(k > 0)
        def _():
            acc_ref[...] += part

        @pl.when(k == nk - 1)
        def _():
            r = acc_ref[...]
            if has_add:
                r = r + c_ref[...]
            o_ref[...] = r.astype(o_ref.dtype)

    in_specs = [pl.BlockSpec((tm, tk), lambda i, j, k: (i, k)), pl.BlockSpec((tk, tn), lambda i, j, k: (k, j))]
    args = [a, b]
    if has_add:
        in_specs.append(pl.BlockSpec((tm, tn), lambda i, j, k: (i, j)))
        args.append(add)
    return pl.pallas_call(
        body, name=name, out_shape=jax.ShapeDtypeStruct((M, N), out_dtype),
        grid=(M // tm, N // tn, nk), in_specs=in_specs,
        out_specs=pl.BlockSpec((tm, tn), lambda i, j, k: (i, j)),
        scratch_shapes=[pltpu.VMEM((tm, tn), F32)] if nk > 1 else [],
        compiler_params=_params(("parallel", "parallel", "arbitrary")),
    )(*args)


def _mm_tn(a, b, *, name):
    T, M = a.shape
    T2, N = b.shape
    assert T == T2
    tm = _pick(M, 1536)
    tn = _pick(N, 1536)
    tt = _pick(T, 1024)
    nt = T // tt

    def body(a_ref, b_ref, o_ref):
        k = pl.program_id(2)
        part = lax.dot_general(a_ref[...].astype(BF16), b_ref[...].astype(BF16), _TN, preferred_element_type=F32)

        @pl.when(k == 0)
        def _():
            o_ref[...] = part

        @pl.when(k > 0)
        def _():
            o_ref[...] += part

    return pl.pallas_call(
        body, name=name, out_shape=jax.ShapeDtypeStruct((M, N), F32),
        grid=(M // tm, N // tn, nt),
        in_specs=[pl.BlockSpec((tt, tm), lambda i, j, k: (k, i)), pl.BlockSpec((tt, tn), lambda i, j, k: (k, j))],
        out_specs=pl.BlockSpec((tm, tn), lambda i, j, k: (i, j)),
        compiler_params=_params(("parallel", "parallel", "arbitrary")),
    )(a, b)


def _rms(x):
    r = lax.rsqrt(jnp.mean(x * x, axis=-1, keepdims=True) + NORM_EPS)
    return x * r, r


def _row_spec(cols, tr=ROW_TILE):
    return pl.BlockSpec((tr, cols), lambda i: (i, 0))


def _vec_spec(cols):
    return pl.BlockSpec((1, cols), lambda i: (0, 0))


def _norm_fwd(x, y, g_post, g_pre, *, name):
    T, D = x.shape
    has_post = y is not None
    has_pre = g_pre is not None

    def body(*refs):
        refs = list(refs)
        x_ref = refs.pop(0)
        xn = x_ref[...]
        if has_post:
            y_ref = refs.pop(0)
            gp_ref = refs.pop(0)
        if has_pre:
            gq_ref = refs.pop(0)
        if has_post:
            yh, _ = _rms(y_ref[...].astype(F32))
            xn = xn + yh * gp_ref[...]
            refs.pop(0)[...] = xn
        if has_pre:
            xh, _ = _rms(xn)
            refs.pop(0)[...] = (xh * gq_ref[...]).astype(BF16)

    args, in_specs, out_shape, out_specs = [x], [_row_spec(D)], [], []
    if has_post:
        args += [y, g_post.reshape(1, D)]
        in_specs += [_row_spec(D), _vec_spec(D)]
        out_shape.append(jax.ShapeDtypeStruct((T, D), F32))
        out_specs.append(_row_spec(D))
    if has_pre:
        args.append(g_pre.reshape(1, D))
        in_specs.append(_vec_spec(D))
        out_shape.append(jax.ShapeDtypeStruct((T, D), BF16))
        out_specs.append(_row_spec(D))
    return pl.pallas_call(body, name=name, out_shape=tuple(out_shape), grid=(T // ROW_TILE,), in_specs=in_specs,
                          out_specs=tuple(out_specs), compiler_params=_params(("parallel",)))(*args)


def _norm_bwd(dres, dh, x_new, g_pre, y, g_post, *, name):
    has_res = dres is not None
    has_pre = dh is not None
    has_post = y is not None
    out_d = has_pre and (has_res or has_post)
    T, D = (dres if has_res else dh).shape
    if not has_res:
        assert not has_post

    def body(*refs):
        refs = list(refs)
        i = pl.program_id(0)
        d = refs.pop(0)[...] if has_res else None
        if has_pre:
            dh_v = refs.pop(0)[...].astype(F32)
            xh, r = _rms(refs.pop(0)[...])
            gq = refs.pop(0)[...]
        if has_post:
            yh, ry = _rms(refs.pop(0)[...].astype(F32))
            gp = refs.pop(0)[...]
        if has_pre:
            dxh = dh_v * gq
            dpre = r * (dxh - xh * jnp.mean(dxh * xh, axis=-1, keepdims=True))
            d = dpre if d is None else d + dpre
            dgq = jnp.sum(dh_v * xh, axis=0, keepdims=True)
        if out_d:
            refs.pop(0)[...] = d
        if has_post:
            dyh = d * gp
            refs.pop(0)[...] = (ry * (dyh - yh * jnp.mean(dyh * yh, axis=-1, keepdims=True))).astype(BF16)
            dgp_ref = refs.pop(0)
            dgp = jnp.sum(d * yh, axis=0, keepdims=True)

            @pl.when(i == 0)
            def _():
                dgp_ref[...] = dgp

            @pl.when(i > 0)
            def _():
                dgp_ref[...] += dgp
        if has_pre:
            dgq_ref = refs.pop(0)

            @pl.when(i == 0)
            def _():
                dgq_ref[...] = dgq

            @pl.when(i > 0)
            def _():
                dgq_ref[...] += dgq

    args, in_specs, out_shape, out_specs = [], [], [], []
    if has_res:
        args.append(dres)
        in_specs.append(_row_spec(D))
    if has_pre:
        args += [dh, x_new, g_pre.reshape(1, D)]
        in_specs += [_row_spec(D), _row_spec(D), _vec_spec(D)]
    if has_post:
        args += [y, g_post.reshape(1, D)]
        in_specs += [_row_spec(D), _vec_spec(D)]
    if out_d:
        out_shape.append(jax.ShapeDtypeStruct((T, D), F32))
        out_specs.append(_row_spec(D))
    if has_post:
        out_shape += [jax.ShapeDtypeStruct((T, D), BF16), jax.ShapeDtypeStruct((1, D), F32)]
        out_specs += [_row_spec(D), _vec_spec(D)]
    if has_pre:
        out_shape.append(jax.ShapeDtypeStruct((1, D), F32))
        out_specs.append(_vec_spec(D))
    return pl.pallas_call(body, name=name, out_shape=tuple(out_shape), grid=(T // ROW_TILE,), in_specs=in_specs,
                          out_specs=tuple(out_specs), compiler_params=_params(("arbitrary",)))(*args)


def _loss_kernel(xf, target, *, name):
    T, D = xf.shape

    def body(x_ref, t_ref, loss_ref, d_ref):
        i = pl.program_id(0)
        e = x_ref[...] - t_ref[...]
        d_ref[...] = e * (1.0 / D)
        part = jnp.sum(jnp.sum(e * e, axis=0, keepdims=True), axis=1, keepdims=True) * (0.5 / D)
        part = jnp.broadcast_to(part, (1, LANE))

        @pl.when(i == 0)
        def _():
            loss_ref[...] = part

        @pl.when(i > 0)
        def _():
            loss_ref[...] += part

    return pl.pallas_call(body, name=name,
                          out_shape=(jax.ShapeDtypeStruct((1, LANE), F32), jax.ShapeDtypeStruct((T, D), F32)),
                          grid=(T // ROW_TILE,), in_specs=[_row_spec(D), _row_spec(D)],
                          out_specs=(_vec_spec(LANE), _row_spec(D)), compiler_params=_params(("arbitrary",)))(xf, target)


CONV_PAD = 32
CONV_CH = 64


def _ln_stats(x):
    mu = jnp.mean(x, axis=-1, keepdims=True)
    xc = x - mu
    rstd = lax.rsqrt(jnp.mean(xc * xc, axis=-1, keepdims=True) + LN_EPS)
    return xc * rstd, rstd


def _acc_out(ref, val, first):
    @pl.when(first)
    def _():
        ref[...] = val

    @pl.when(jnp.logical_not(first))
    def _():
        ref[...] += val


def _bra_fwd(za, conv_w, conv_b, ln_g, ln_b, *, B, S, name):
    C = conv_w.shape[1]

    def body(za_ref, w_ref, cb_ref, g_ref, b_ref, a4_ref, a2_ref, pad_ref):
        pad_ref[0:CONV_PAD, :] = jnp.zeros((CONV_PAD, C), F32)
        pad_ref[CONV_PAD:, :] = za_ref[:, 0:C].astype(F32) * _sigmoid(za_ref[:, C:2 * C].astype(F32))

        def chunk(i, _):
            base = pl.multiple_of(i * CONV_CH, CONV_CH)
            win = pad_ref[pl.ds(base, CONV_CH + CONV_PAD), :]
            acc = jnp.broadcast_to(cb_ref[...], (CONV_CH, C))
            for j in range(CONV_K):
                off = CONV_PAD - (CONV_K - 1) + j
                acc = acc + w_ref[j:j + 1, :] * win[off:off + CONV_CH, :]
            a2_ref[pl.ds(base, CONV_CH), :] = acc
            xh, _ = _ln_stats(acc)
            a3 = xh * g_ref[...] + b_ref[...]
            a4_ref[pl.ds(base, CONV_CH), :] = (a3 * _sigmoid(a3)).astype(BF16)
            return 0

        lax.fori_loop(0, S // CONV_CH, chunk, 0)

    vec = pl.BlockSpec((1, C), lambda b: (0, 0))
    return pl.pallas_call(
        body, name=name,
        out_shape=(jax.ShapeDtypeStruct((B * S, C), BF16), jax.ShapeDtypeStruct((B * S, C), F32)),
        grid=(B,),
        in_specs=[pl.BlockSpec((S, 2 * C), lambda b: (b, 0)), pl.BlockSpec((CONV_K, C), lambda b: (0, 0)), vec, vec, vec],
        out_specs=(pl.BlockSpec((S, C), lambda b: (b, 0)), pl.BlockSpec((S, C), lambda b: (b, 0))),
        scratch_shapes=[pltpu.VMEM((S + CONV_PAD, C), F32)],
        compiler_params=_params(("parallel",)),
    )(za, conv_w, conv_b.reshape(1, C), ln_g.reshape(1, C), ln_b.reshape(1, C))


def _bra_bwd_ln(da4, a2, ln_g, ln_b, *, name):
    T, C = a2.shape

    def body(da4_ref, a2_ref, g_ref, b_ref, da2_ref, dg_ref, db_ref, dcb_ref):
        first = pl.program_id(0) == 0
        xh, rstd = _ln_stats(a2_ref[...])
        a3 = xh * g_ref[...] + b_ref[...]
        sg = _sigmoid(a3)
        da3 = da4_ref[...].astype(F32) * (sg * (1.0 + a3 * (1.0 - sg)))
        dxh = da3 * g_ref[...]
        da2 = rstd * (dxh - jnp.mean(dxh, axis=-1, keepdims=True) - xh * jnp.mean(dxh * xh, axis=-1, keepdims=True))
        da2_ref[...] = da2
        _acc_out(dg_ref, jnp.sum(da3 * xh, axis=0, keepdims=True), first)
        _acc_out(db_ref, jnp.sum(da3, axis=0, keepdims=True), first)
        _acc_out(dcb_ref, jnp.sum(da2, axis=0, keepdims=True), first)

    vec = _vec_spec(C)
    vshape = jax.ShapeDtypeStruct((1, C), F32)
    return pl.pallas_call(body, name=name, out_shape=(jax.ShapeDtypeStruct((T, C), F32), vshape, vshape, vshape),
                          grid=(T // ROW_TILE,), in_specs=[_row_spec(C), _row_spec(C), vec, vec],
                          out_specs=(_row_spec(C), vec, vec, vec),
                          compiler_params=_params(("arbitrary",)))(da4, a2, ln_g.reshape(1, C), ln_b.reshape(1, C))


def _bra_bwd_conv(da2, za, conv_w, *, B, S, name):
    C = conv_w.shape[1]
    SUB = 8

    def body(da2_ref, za_ref, w_ref, dza_ref, dw_ref, pad1_ref, pad2_ref, dwacc_ref):
        first = pl.program_id(0) == 0
        pad1_ref[0:CONV_PAD, :] = jnp.zeros((CONV_PAD, C), F32)
        pad1_ref[CONV_PAD:, :] = za_ref[:, 0:C].astype(F32) * _sigmoid(za_ref[:, C:2 * C].astype(F32))
        pad2_ref[0:S, :] = da2_ref[...]
        pad2_ref[S:, :] = jnp.zeros((CONV_PAD, C), F32)
        dwacc_ref[...] = jnp.zeros_like(dwacc_ref)

        def chunk(i, _):
            base = pl.multiple_of(i * CONV_CH, CONV_CH)
            rows = pl.ds(base, CONV_CH)
            win1 = pad1_ref[pl.ds(base, CONV_CH + CONV_PAD), :]
            win2 = pad2_ref[pl.ds(base, CONV_CH + CONV_PAD), :]
            d = win2[0:CONV_CH, :]
            da1 = jnp.zeros((CONV_CH, C), F32)
            for j in range(CONV_K):
                off2 = CONV_K - 1 - j
                da1 = da1 + w_ref[j:j + 1, :] * win2[off2:off2 + CONV_CH, :]
                off1 = CONV_PAD - (CONV_K - 1) + j
                prod = d * win1[off1:off1 + CONV_CH, :]
                dwacc_ref[j * SUB:(j + 1) * SUB, :] += jnp.sum(prod.reshape(CONV_CH // SUB, SUB, C), axis=0)
            a_val = za_ref[rows, 0:C].astype(F32)
            sg = _sigmoid(za_ref[rows, C:2 * C].astype(F32))
            dza_ref[rows, 0:C] = (da1 * sg).astype(BF16)
            dza_ref[rows, C:2 * C] = (da1 * a_val * sg * (1.0 - sg)).astype(BF16)
            return 0

        lax.fori_loop(0, S // CONV_CH, chunk, 0)
        _acc_out(dw_ref, jnp.sum(dwacc_ref[...].reshape(CONV_K, SUB, C), axis=1), first)

    return pl.pallas_call(
        body, name=name,
        out_shape=(jax.ShapeDtypeStruct((B * S, 2 * C), BF16), jax.ShapeDtypeStruct((CONV_K, C), F32)),
        grid=(B,),
        in_specs=[pl.BlockSpec((S, C), lambda b: (b, 0)), pl.BlockSpec((S, 2 * C), lambda b: (b, 0)),
                  pl.BlockSpec((CONV_K, C), lambda b: (0, 0))],
        out_specs=(pl.BlockSpec((S, 2 * C), lambda b: (b, 0)), pl.BlockSpec((CONV_K, C), lambda b: (0, 0))),
        scratch_shapes=[pltpu.VMEM((S + CONV_PAD, C), F32), pltpu.VMEM((S + CONV_PAD, C), F32),
                        pltpu.VMEM((CONV_K * SUB, C), F32)],
        compiler_params=_params(("arbitrary",)),
    )(da2, za, conv_w)


def _tril_mask():
    r = lax.broadcasted_iota(jnp.int32, (SG_CHUNK, SG_CHUNK), 0)
    c = lax.broadcasted_iota(jnp.int32, (SG_CHUNK, SG_CHUNK), 1)
    return c <= r


def _brb_fwd(zb, ln_g, ln_b, w_s, b_s_t, *, name):
    T, C2 = zb.shape
    C = C2 // 2
    G = w_s.shape[0]
    GC = C // G

    def body(zb_ref, g_ref, b_ref, ws_ref, bs_ref, p_ref):
        z = _gelu(zb_ref[...].astype(F32))
        u = z[:, 0:C]
        xh, _ = _ln_stats(z[:, C:2 * C])
        v1 = (xh * g_ref[...] + b_ref[...]).astype(BF16)
        mask = _tril_mask()
        outs = []
        for gi in range(G):
            ws = jnp.where(mask, ws_ref[gi], 0.0).astype(BF16)
            v2 = jnp.dot(ws, v1[:, gi * GC:(gi + 1) * GC], preferred_element_type=F32) + bs_ref[:, gi:gi + 1]
            outs.append(v2)
        p_ref[...] = (u * jnp.concatenate(outs, axis=1)).astype(BF16)

    return pl.pallas_call(
        body, name=name, out_shape=jax.ShapeDtypeStruct((T, C), BF16), grid=(T // SG_CHUNK,),
        in_specs=[_row_spec(C2, SG_CHUNK), _vec_spec(C), _vec_spec(C),
                  pl.BlockSpec((G, SG_CHUNK, SG_CHUNK), lambda i: (0, 0, 0)), pl.BlockSpec((SG_CHUNK, G), lambda i: (0, 0))],
        out_specs=_row_spec(C, SG_CHUNK), compiler_params=_params(("parallel",)),
    )(zb, ln_g.reshape(1, C), ln_b.reshape(1, C), w_s, b_s_t)


def _brb_bwd(dp, zb, ln_g, ln_b, w_s, b_s_t, *, name):
    T, C2 = zb.shape
    C = C2 // 2
    G = w_s.shape[0]
    GC = C // G

    def body(dp_ref, zb_ref, g_ref, b_ref, ws_ref, bs_ref, dzb_ref, dws_ref, dbs_ref, dg_ref, db_ref):
        first = pl.program_id(0) == 0
        z, dz = _gelu_and_grad(zb_ref[...].astype(F32))
        u = z[:, 0:C]
        xh, rstd = _ln_stats(z[:, C:2 * C])
        v1 = (xh * g_ref[...] + b_ref[...]).astype(BF16)
        dp_v = dp_ref[...].astype(F32)
        mask = _tril_mask()
        v2s, dv1s, dwss, dbss = [], [], [], []
        for gi in range(G):
            cols = slice(gi * GC, (gi + 1) * GC)
            ws = jnp.where(mask, ws_ref[gi], 0.0).astype(BF16)
            v2s.append(jnp.dot(ws, v1[:, cols], preferred_element_type=F32) + bs_ref[:, gi:gi + 1])
            dv2 = dp_v[:, cols] * u[:, cols]
            dv2b = dv2.astype(BF16)
            dbss.append(jnp.sum(dv2, axis=1, keepdims=True))
            dwss.append(jnp.where(mask, lax.dot_general(dv2b, v1[:, cols], _NT, preferred_element_type=F32), 0.0))
            dv1s.append(lax.dot_general(ws, dv2b, _TN, preferred_element_type=F32))
        du = dp_v * jnp.concatenate(v2s, axis=1)
        dv1 = jnp.concatenate(dv1s, axis=1)
        dxh = dv1 * g_ref[...]
        dv0 = rstd * (dxh - jnp.mean(dxh, axis=-1, keepdims=True) - xh * jnp.mean(dxh * xh, axis=-1, keepdims=True))
        dzb_ref[:, 0:C] = (du * dz[:, 0:C]).astype(BF16)
        dzb_ref[:, C:2 * C] = (dv0 * dz[:, C:2 * C]).astype(BF16)
        _acc_out(dws_ref, jnp.stack(dwss, axis=0), first)
        _acc_out(dbs_ref, jnp.concatenate(dbss, axis=1), first)
        _acc_out(dg_ref, jnp.sum(dv1 * xh, axis=0, keepdims=True), first)
        _acc_out(db_ref, jnp.sum(dv1, axis=0, keepdims=True), first)

    wspec = pl.BlockSpec((G, SG_CHUNK, SG_CHUNK), lambda i: (0, 0, 0))
    bspec = pl.BlockSpec((SG_CHUNK, G), lambda i: (0, 0))
    return pl.pallas_call(
        body, name=name,
        out_shape=(jax.ShapeDtypeStruct((T, C2), BF16), jax.ShapeDtypeStruct((G, SG_CHUNK, SG_CHUNK), F32),
                   jax.ShapeDtypeStruct((SG_CHUNK, G), F32), jax.ShapeDtypeStruct((1, C), F32),
                   jax.ShapeDtypeStruct((1, C), F32)),
        grid=(T // SG_CHUNK,),
        in_specs=[_row_spec(C, SG_CHUNK), _row_spec(C2, SG_CHUNK), _vec_spec(C), _vec_spec(C), wspec, bspec],
        out_specs=(_row_spec(C2, SG_CHUNK), wspec, bspec, _vec_spec(C), _vec_spec(C)),
        compiler_params=_params(("arbitrary",)),
    )(dp, zb, ln_g.reshape(1, C), ln_b.reshape(1, C), w_s, b_s_t)


def _attn_fwd(qkv_g, bias, *, name):
    B, d, L, _ = qkv_g.shape
    nb = L // ATT_BLOCK
    GCOL = GROUP_COLS

    def body(qkv_ref, bias_ref, o_ref, lse_ref):
        def blk(r0, first):
            nk = ATT_BLOCK if first else 2 * ATT_BLOCK
            k0 = r0 if first else r0 - ATT_BLOCK
            qb = qkv_ref[pl.ds(r0, ATT_BLOCK), 0:GCOL]
            kb = qkv_ref[pl.ds(k0, nk), GCOL:2 * GCOL]
            vb = qkv_ref[pl.ds(k0, nk), 2 * GCOL:3 * GCOL]
            outs, lses = [], []
            for h in range(HEADS_PER_GROUP):
                sl = slice(h * HEAD_DIM, (h + 1) * HEAD_DIM)
                bh = bias_ref[h, :, ATT_BLOCK:2 * ATT_BLOCK] if first else bias_ref[h]
                s = lax.dot_general(qb[:, sl], kb[:, sl], _NT, preferred_element_type=F32) * ATT_SCALE + bh
                m = jnp.max(s, axis=1, keepdims=True)
                e = jnp.exp(s - m)
                ssum = jnp.sum(e, axis=1, keepdims=True)
                outs.append(jnp.dot(e.astype(BF16), vb[:, sl], preferred_element_type=F32) / ssum)
                lses.append(jnp.broadcast_to(m + jnp.log(ssum), (ATT_BLOCK, HEAD_DIM)))
            o_ref[pl.ds(r0, ATT_BLOCK), :] = jnp.concatenate(outs, axis=1)
            lse_ref[pl.ds(r0, ATT_BLOCK), :] = jnp.concatenate(lses, axis=1)

        blk(0, True)
        if nb > 1:
            def loop(n, _):
                blk(pl.multiple_of(n * ATT_BLOCK, ATT_BLOCK), False)
                return 0

            lax.fori_loop(1, nb, loop, 0)

    spec = lambda cols: pl.BlockSpec((None, None, L, cols), lambda b, c: (b, c, 0, 0))
    oshape = jax.ShapeDtypeStruct((B, d, L, GCOL), F32)
    return pl.pallas_call(
        body, name=name, out_shape=(oshape, oshape), grid=(B, d),
        in_specs=[spec(3 * GCOL), pl.BlockSpec((HEADS_PER_GROUP, ATT_BLOCK, 2 * ATT_BLOCK), lambda b, c: (0, 0, 0))],
        out_specs=(spec(GCOL), spec(GCOL)), compiler_params=_params(("parallel", "parallel")),
    )(qkv_g, bias)


def _attn_bwd(qkv_g, doc_g, lse_g, dd_g, bias, *, name):
    B, d, L, _ = qkv_g.shape
    nb = L // ATT_BLOCK
    GCOL = GROUP_COLS

    def body(qkv_ref, doc_ref, lse_ref, dd_ref, bias_ref, dqkv_ref, dbias_ref, dk_ref, dv_ref):
        @pl.when(jnp.logical_and(pl.program_id(0) == 0, pl.program_id(1) == 0))
        def _():
            dbias_ref[...] = jnp.zeros_like(dbias_ref)

        dk_ref[...] = jnp.zeros_like(dk_ref)
        dv_ref[...] = jnp.zeros_like(dv_ref)

        def blk(r0, first):
            nk = ATT_BLOCK if first else 2 * ATT_BLOCK
            k0 = r0 if first else r0 - ATT_BLOCK
            rows = pl.ds(r0, ATT_BLOCK)
            krows = pl.ds(k0, nk)
            qb = qkv_ref[rows, 0:GCOL]
            kb = qkv_ref[krows, GCOL:2 * GCOL]
            vb = qkv_ref[krows, 2 * GCOL:3 * GCOL]
            dob = doc_ref[rows, :].astype(BF16)
            lse = lse_ref[rows, :]
            dd = dd_ref[rows, :]
            dqs, dks, dvs = [], [], []
            for h in range(HEADS_PER_GROUP):
                sl = slice(h * HEAD_DIM, (h + 1) * HEAD_DIM)
                c0 = h * HEAD_DIM
                bh = bias_ref[h, :, ATT_BLOCK:2 * ATT_BLOCK] if first else bias_ref[h]
                s = lax.dot_general(qb[:, sl], kb[:, sl], _NT, preferred_element_type=F32) * ATT_SCALE + bh
                p = jnp.exp(s - lse[:, c0:c0 + 1])
                dp = lax.dot_general(dob[:, sl], vb[:, sl], _NT, preferred_element_type=F32)
                ds = p * (dp - dd[:, c0:c0 + 1])
                if first:
                    dbias_ref[h, :, ATT_BLOCK:2 * ATT_BLOCK] += ds
                else:
                    dbias_ref[h] += ds
                dsb = ds.astype(BF16)
                dqs.append(jnp.dot(dsb, kb[:, sl], preferred_element_type=F32) * ATT_SCALE)
                dks.append(lax.dot_general(dsb, qb[:, sl], _TN, preferred_element_type=F32) * ATT_SCALE)
                dvs.append(lax.dot_general(p.astype(BF16), dob[:, sl], _TN, preferred_element_type=F32))
            dqkv_ref[rows, 0:GCOL] = jnp.concatenate(dqs, axis=1).astype(BF16)
            dk_ref[krows, :] += jnp.concatenate(dks, axis=1)
            dv_ref[krows, :] += jnp.concatenate(dvs, axis=1)

        blk(0, True)
        if nb > 1:
            def loop(n, _):
                blk(pl.multiple_of(n * ATT_BLOCK, ATT_BLOCK), False)
                return 0

            lax.fori_loop(1, nb, loop, 0)
        dqkv_ref[:, GCOL:2 * GCOL] = dk_ref[...].astype(BF16)
        dqkv_ref[:, 2 * GCOL:3 * GCOL] = dv_ref[...].astype(BF16)

    spec = lambda cols: pl.BlockSpec((None, None, L, cols), lambda b, c: (b, c, 0, 0))
    bspec = pl.BlockSpec((HEADS_PER_GROUP, ATT_BLOCK, 2 * ATT_BLOCK), lambda b, c: (0, 0, 0))
    return pl.pallas_call(
        body, name=name,
        out_shape=(jax.ShapeDtypeStruct((B, d, L, 3 * GCOL), BF16),
                   jax.ShapeDtypeStruct((HEADS_PER_GROUP, ATT_BLOCK, 2 * ATT_BLOCK), F32)),
        grid=(B, d),
        in_specs=[spec(3 * GCOL), spec(GCOL), spec(GCOL), spec(GCOL), bspec],
        out_specs=(spec(3 * GCOL), bspec),
        scratch_shapes=[pltpu.VMEM((L, GCOL), F32), pltpu.VMEM((L, GCOL), F32)],
        compiler_params=_params(("arbitrary", "arbitrary")),
    )(qkv_g, doc_g, lse_g, dd_g, bias)


def _attn_combine(os_, lses, *, name):
    T, GC = os_[0].shape
    n = len(os_)

    def body(*refs):
        o_refs, l_refs, oc_ref, lt_ref = refs[:n], refs[n:2 * n], refs[2 * n], refs[2 * n + 1]
        ls = [r[...] for r in l_refs]
        m = functools.reduce(jnp.maximum, ls)
        ws = [jnp.exp(l - m) for l in ls]
        tot = functools.reduce(jnp.add, ws)
        acc = functools.reduce(jnp.add, [w * r[...] for w, r in zip(ws, o_refs)])
        oc_ref[...] = acc / tot
        lt_ref[...] = m + jnp.log(tot)

    shp = jax.ShapeDtypeStruct((T, GC), F32)
    return pl.pallas_call(body, name=name, out_shape=(shp, shp), grid=(T // ROW_TILE,),
                          in_specs=[_row_spec(GC)] * (2 * n), out_specs=(_row_spec(GC), _row_spec(GC)),
                          compiler_params=_params(("parallel",)))(*os_, *lses)


def _attn_rowdot(doc, oc, *, name):
    T, GC = oc.shape

    def body(doc_ref, oc_ref, dd_ref):
        prod = doc_ref[...].astype(F32) * oc_ref[...]
        parts = []
        for h in range(GC // HEAD_DIM):
            s = jnp.sum(prod[:, h * HEAD_DIM:(h + 1) * HEAD_DIM], axis=1, keepdims=True)
            parts.append(jnp.broadcast_to(s, (ROW_TILE, HEAD_DIM)))
        dd_ref[...] = jnp.concatenate(parts, axis=1)

    return pl.pallas_call(body, name=name, out_shape=jax.ShapeDtypeStruct((T, GC), F32), grid=(T // ROW_TILE,),
                          in_specs=[_row_spec(GC), _row_spec(GC)], out_specs=_row_spec(GC),
                          compiler_params=_params(("parallel",)))(doc, oc)


def _bucket_sum(dbias, buckets, *, name):
    depth, NH = dbias.shape[:2]

    def body(db_ref, bk_ref, out_ref):
        rows = lax.broadcasted_iota(jnp.int32, (REL_BUCKETS, LANE), 0)
        cols = lax.broadcasted_iota(jnp.int32, (REL_BUCKETS, LANE), 1)

        def per_bucket(b, acc):
            for h in range(NH):
                sel = bk_ref[h // HEADS_PER_GROUP] == b
                tot = functools.reduce(jnp.add, [db_ref[l, h] for l in range(depth)])
                s = jnp.sum(jnp.where(sel, tot, 0.0))
                acc = acc + jnp.where(jnp.logical_and(rows == b, cols == h), s, 0.0)
            return acc

        out_ref[...] = lax.fori_loop(0, REL_BUCKETS, per_bucket, jnp.zeros((REL_BUCKETS, LANE), F32))

    return pl.pallas_call(body, name=name, out_shape=jax.ShapeDtypeStruct((REL_BUCKETS, LANE), F32),
                          compiler_params=pltpu.CompilerParams(vmem_limit_bytes=VMEM_LIMIT))(dbias, buckets)


def _merge_fwd(zg, bg, ya, yb, yc, *, name):
    T, D3 = zg.shape
    D = D3 // 3

    def body(zg_ref, bg_ref, ya_ref, yb_ref, yc_ref, out_ref):
        acc = None
        for i, y_ref in enumerate((ya_ref, yb_ref, yc_ref)):
            g = _sigmoid(zg_ref[:, i * D:(i + 1) * D].astype(F32) + bg_ref[:, i * D:(i + 1) * D])
            t = g * y_ref[...].astype(F32)
            acc = t if acc is None else acc + t
        out_ref[...] = acc.astype(BF16)

    return pl.pallas_call(body, name=name, out_shape=jax.ShapeDtypeStruct((T, D), BF16), grid=(T // ROW_TILE,),
                          in_specs=[_row_spec(D3), _vec_spec(D3), _row_spec(D), _row_spec(D), _row_spec(D)],
                          out_specs=_row_spec(D), compiler_params=_params(("parallel",)))(zg, bg.reshape(1, D3), ya, yb, yc)


def _merge_bwd(dm, zg, bg, ya, yb, yc, *, name):
    T, D3 = zg.shape
    D = D3 // 3

    def body(dm_ref, zg_ref, bg_ref, ya_ref, yb_ref, yc_ref, dya_ref, dyb_ref, dyc_ref, dzg_ref, dbg_ref):
        first = pl.program_id(0) == 0
        dm_v = dm_ref[...].astype(F32)
        dbs = []
        for i, (y_ref, dy_ref) in enumerate(((ya_ref, dya_ref), (yb_ref, dyb_ref), (yc_ref, dyc_ref))):
            g = _sigmoid(zg_ref[:, i * D:(i + 1) * D].astype(F32) + bg_ref[:, i * D:(i + 1) * D])
            dy_ref[...] = (dm_v * g).astype(BF16)
            dz = dm_v * y_ref[...].astype(F32) * g * (1.0 - g)
            dzg_ref[:, i * D:(i + 1) * D] = dz.astype(BF16)
            dbs.append(jnp.sum(dz, axis=0, keepdims=True))
        _acc_out(dbg_ref, jnp.concatenate(dbs, axis=1), first)

    bshape = jax.ShapeDtypeStruct((T, D), BF16)
    return pl.pallas_call(
        body, name=name,
        out_shape=(bshape, bshape, bshape, jax.ShapeDtypeStruct((T, D3), BF16), jax.ShapeDtypeStruct((1, D3), F32)),
        grid=(T // ROW_TILE,),
        in_specs=[_row_spec(D), _row_spec(D3), _vec_spec(D3), _row_spec(D), _row_spec(D), _row_spec(D)],
        out_specs=(_row_spec(D), _row_spec(D), _row_spec(D), _row_spec(D3), _vec_spec(D3)),
        compiler_params=_params(("arbitrary",)))(dm, zg, bg.reshape(1, D3), ya, yb, yc)


XQ_TILE = 512


def _xattn_fwd(q, kv, *, B, S, name):
    D = q.shape[1]
    M = kv.shape[0] // B
    E = D // X_HEADS
    scale = E ** -0.5

    def body(q_ref, kv_ref, o_ref):
        outs = []
        for h in range(X_HEADS):
            s = lax.dot_general(q_ref[:, h * E:(h + 1) * E], kv_ref[:, h * E:(h + 1) * E], _NT,
                                preferred_element_type=F32) * scale
            e = jnp.exp(s - jnp.max(s, axis=1, keepdims=True))
            p = e / jnp.sum(e, axis=1, keepdims=True)
            outs.append(jnp.dot(p.astype(BF16), kv_ref[:, D + h * E:D + (h + 1) * E], preferred_element_type=F32))
        o_ref[...] = jnp.concatenate(outs, axis=1).astype(BF16)

    nq = S // XQ_TILE
    return pl.pallas_call(
        body, name=name, out_shape=jax.ShapeDtypeStruct((B * S, D), BF16), grid=(B, nq),
        in_specs=[pl.BlockSpec((XQ_TILE, D), lambda b, i: (b * nq + i, 0)), pl.BlockSpec((M, 2 * D), lambda b, i: (b, 0))],
        out_specs=pl.BlockSpec((XQ_TILE, D), lambda b, i: (b * nq + i, 0)),
        compiler_params=_params(("parallel", "parallel")))(q, kv)


def _xattn_bwd(q, kv, do, *, B, S, name):
    D = q.shape[1]
    M = kv.shape[0] // B
    E = D // X_HEADS
    scale = E ** -0.5

    def body(q_ref, kv_ref, do_ref, dq_ref, dkv_ref):
        first = pl.program_id(1) == 0
        dqs, dks, dvs = [], [], []
        for h in range(X_HEADS):
            qh = q_ref[:, h * E:(h + 1) * E]
            kh = kv_ref[:, h * E:(h + 1) * E]
            vh = kv_ref[:, D + h * E:D + (h + 1) * E]
            doh = do_ref[:, h * E:(h + 1) * E]
            s = lax.dot_general(qh, kh, _NT, preferred_element_type=F32) * scale
            e = jnp.exp(s - jnp.max(s, axis=1, keepdims=True))
            p = e / jnp.sum(e, axis=1, keepdims=True)
            dp = lax.dot_general(doh, vh, _NT, preferred_element_type=F32)
            ds = (p * (dp - jnp.sum(p * dp, axis=1, keepdims=True))).astype(BF16)
            dqs.append(jnp.dot(ds, kh, preferred_element_type=F32) * scale)
            dks.append(lax.dot_general(ds, qh, _TN, preferred_element_type=F32) * scale)
            dvs.append(lax.dot_general(p.astype(BF16), doh, _TN, preferred_element_type=F32))
        dq_ref[...] = jnp.concatenate(dqs, axis=1).astype(BF16)
        _acc_out(dkv_ref, jnp.concatenate(dks + dvs, axis=1), first)

    nq = S // XQ_TILE
    qspec = pl.BlockSpec((XQ_TILE, D), lambda b, i: (b * nq + i, 0))
    kvspec = pl.BlockSpec((M, 2 * D), lambda b, i: (b, 0))
    return pl.pallas_call(
        body, name=name,
        out_shape=(jax.ShapeDtypeStruct((B * S, D), BF16), jax.ShapeDtypeStruct((B * M, 2 * D), F32)),
        grid=(B, nq), in_specs=[qspec, kvspec, qspec], out_specs=(qspec, kvspec),
        compiler_params=_params(("arbitrary", "arbitrary")))(q, kv, do)


FFN_COLS = 256
FFN_PAD = 8
FFN_CH = 256
FFN_K = 3


def _ffn_gate(win, w_ref, cb_ref, n):
    g = jnp.broadcast_to(cb_ref[...], (n, win.shape[1]))
    for j in range(FFN_K):
        off = FFN_PAD - (FFN_K - 1) + j
        g = g + w_ref[j:j + 1, :] * win[off:off + n, :]
    return g


def _ffn_fwd(up, conv_w, conv_b, *, B, S, name):
    F = conv_w.shape[1]
    nc = F // FFN_COLS

    def body(gp_ref, val_ref, w_ref, cb_ref, act_ref, pad_ref):
        pad_ref[0:FFN_PAD, :] = jnp.zeros((FFN_PAD, FFN_COLS), F32)
        pad_ref[FFN_PAD:, :] = gp_ref[...].astype(F32)

        def chunk(i, _):
            base = pl.multiple_of(i * FFN_CH, FFN_CH)
            gate = _ffn_gate(pad_ref[pl.ds(base, FFN_CH + FFN_PAD), :], w_ref, cb_ref, FFN_CH)
            act_ref[pl.ds(base, FFN_CH), :] = (_gelu(gate) * val_ref[pl.ds(base, FFN_CH), :].astype(F32)).astype(BF16)
            return 0

        lax.fori_loop(0, S // FFN_CH, chunk, 0)

    return pl.pallas_call(
        body, name=name, out_shape=jax.ShapeDtypeStruct((B * S, F), BF16), grid=(B, nc),
        in_specs=[pl.BlockSpec((S, FFN_COLS), lambda b, j: (b, j)), pl.BlockSpec((S, FFN_COLS), lambda b, j: (b, nc + j)),
                  pl.BlockSpec((FFN_K, FFN_COLS), lambda b, j: (0, j)), pl.BlockSpec((1, FFN_COLS), lambda b, j: (0, j))],
        out_specs=pl.BlockSpec((S, FFN_COLS), lambda b, j: (b, j)),
        scratch_shapes=[pltpu.VMEM((S + FFN_PAD, FFN_COLS), F32)],
        compiler_params=_params(("parallel", "parallel")))(up, up, conv_w, conv_b.reshape(1, F))


def _ffn_bwd(dact, up, conv_w, conv_b, *, B, S, name):
    F = conv_w.shape[1]
    nc = F // FFN_COLS
    SUB = 8

    def body(dact_ref, gp_ref, val_ref, w_ref, cb_ref, dgp_ref, dval_ref, dw_ref, dcb_ref, pad_ref, pad2_ref, acc_ref):
        first = pl.program_id(1) == 0
        pad_ref[0:FFN_PAD, :] = jnp.zeros((FFN_PAD, FFN_COLS), F32)
        pad_ref[FFN_PAD:, :] = gp_ref[...].astype(F32)
        pad2_ref[S:, :] = jnp.zeros((FFN_PAD, FFN_COLS), F32)
        acc_ref[...] = jnp.zeros_like(acc_ref)

        def chunk1(i, _):
            base = pl.multiple_of(i * FFN_CH, FFN_CH)
            rows = pl.ds(base, FFN_CH)
            gate = _ffn_gate(pad_ref[pl.ds(base, FFN_CH + FFN_PAD), :], w_ref, cb_ref, FFN_CH)
            gl, dgl = _gelu_and_grad(gate)
            da = dact_ref[rows, :].astype(F32)
            dval_ref[rows, :] = (da * gl).astype(BF16)
            pad2_ref[rows, :] = da * val_ref[rows, :].astype(F32) * dgl
            return 0

        lax.fori_loop(0, S // FFN_CH, chunk1, 0)

        def chunk2(i, _):
            base = pl.multiple_of(i * FFN_CH, FFN_CH)
            rows = pl.ds(base, FFN_CH)
            win2 = pad2_ref[pl.ds(base, FFN_CH + FFN_PAD), :]
            win1 = pad_ref[pl.ds(base, FFN_CH + FFN_PAD), :]
            dg = win2[0:FFN_CH, :]
            dgp = jnp.zeros((FFN_CH, FFN_COLS), F32)
            for j in range(FFN_K):
                off2 = FFN_K - 1 - j
                dgp = dgp + w_ref[j:j + 1, :] * win2[off2:off2 + FFN_CH, :]
                off1 = FFN_PAD - (FFN_K - 1) + j
                prod = dg * win1[off1:off1 + FFN_CH, :]
                acc_ref[j * SUB:(j + 1) * SUB, :] += jnp.sum(prod.reshape(FFN_CH // SUB, SUB, FFN_COLS), axis=0)
            acc_ref[FFN_K * SUB:(FFN_K + 1) * SUB, :] += jnp.sum(dg.reshape(FFN_CH // SUB, SUB, FFN_COLS), axis=0)
            dgp_ref[rows, :] = dgp.astype(BF16)
            return 0

        lax.fori_loop(0, S // FFN_CH, chunk2, 0)
        sums = jnp.sum(acc_ref[...].reshape(FFN_K + 1, SUB, FFN_COLS), axis=1)
        _acc_out(dw_ref, sums[0:FFN_K, :], first)
        _acc_out(dcb_ref, sums[FFN_K:FFN_K + 1, :], first)

    blk = lambda off: pl.BlockSpec((S, FFN_COLS), lambda j, b: (b, off + j))
    return pl.pallas_call(
        body, name=name,
        out_shape=(jax.ShapeDtypeStruct((B * S, F), BF16), jax.ShapeDtypeStruct((B * S, F), BF16),
                   jax.ShapeDtypeStruct((FFN_K, F), F32), jax.ShapeDtypeStruct((1, F), F32)),
        grid=(nc, B),
        in_specs=[blk(0), blk(0), blk(nc), pl.BlockSpec((FFN_K, FFN_COLS), lambda j, b: (0, j)),
                  pl.BlockSpec((1, FFN_COLS), lambda j, b: (0, j))],
        out_specs=(blk(0), blk(0), pl.BlockSpec((FFN_K, FFN_COLS), lambda j, b: (0, j)),
                   pl.BlockSpec((1, FFN_COLS), lambda j, b: (0, j))),
        scratch_shapes=[pltpu.VMEM((S + FFN_PAD, FFN_COLS), F32), pltpu.VMEM((S + FFN_PAD, FFN_COLS), F32),
                        pltpu.VMEM(((FFN_K + 1) * SUB, FFN_COLS), F32)],
        compiler_params=_params(("arbitrary", "arbitrary")))(dact, up, up, conv_w, conv_b.reshape(1, F))


def _row_tile(rows, row_bytes, budget):
    tr = rows
    if rows * row_bytes > budget:
        for t in range(16, rows, 16):
            if rows % t == 0 and t * row_bytes <= budget:
                tr = t
    return tr


def _adamw(w, parts, m, v, *, name, own=None, own_slot=None):
    shape = w.shape
    n = parts.shape[0]
    cols = shape[-1]
    rows = w.size // cols
    w2, m2, v2 = (t.reshape(rows, cols) for t in (w, m, v))
    p3 = parts.reshape(n, rows, cols)
    tr = _row_tile(rows, cols * 4, 1 << 19)
    c1 = 1.0 - ADAM_B1 ** ADAM_STEP
    c2 = 1.0 - ADAM_B2 ** ADAM_STEP
    has_own = own is not None

    def body(*refs):
        if has_own:
            slot_ref, w_ref, p_ref, o_ref, m_ref, v_ref, g_ref, d_ref, mo_ref, vo_ref = refs
        else:
            w_ref, p_ref, m_ref, v_ref, g_ref, d_ref, mo_ref, vo_ref = refs
        gv = None
        for i in range(n):
            t = p_ref[i].astype(F32)
            if has_own:
                t = jnp.where(slot_ref[0] == i, o_ref[i].astype(F32), t)
            gv = t if gv is None else gv + t
        g_ref[...] = gv
        mn = ADAM_B1 * m_ref[...] + (1.0 - ADAM_B1) * gv
        vn = ADAM_B2 * v_ref[...] + (1.0 - ADAM_B2) * (gv * gv)
        d_ref[...] = -ADAM_LR * ((mn / c1) / (jnp.sqrt(vn / c2) + ADAM_EPS) + ADAM_WD * w_ref[...])
        mo_ref[...] = mn
        vo_ref[...] = vn

    spec = pl.BlockSpec((tr, cols), lambda i, *_: (i, 0))
    pspec = pl.BlockSpec((n, tr, cols), lambda i, *_: (0, i, 0))
    oshape = jax.ShapeDtypeStruct((rows, cols), F32)
    in_specs = [spec, pspec] + ([pspec] if has_own else []) + [spec, spec]
    args = [w2, p3] + ([own.reshape(n, rows, cols)] if has_own else []) + [m2, v2]
    if has_own:
        grid_spec = pltpu.PrefetchScalarGridSpec(num_scalar_prefetch=1, grid=(rows // tr,), in_specs=in_specs,
                                                 out_specs=(spec,) * 4)
        outs = pl.pallas_call(body, name=name, out_shape=(oshape,) * 4, grid_spec=grid_spec,
                              compiler_params=_params(("parallel",)))(own_slot, *args)
    else:
        outs = pl.pallas_call(body, name=name, out_shape=(oshape,) * 4, grid=(rows // tr,), in_specs=in_specs,
                              out_specs=(spec,) * 4, compiler_params=_params(("parallel",)))(*args)
    return tuple(t.reshape(shape) for t in outs)


def _chip_sum(g, got, core, *, name):
    shard = got.shape[1:]
    cols = shard[-1]
    rows = math.prod(shard) // cols
    tr = _row_tile(rows, cols * 4, 1 << 20)

    def body(c_ref, g_ref, r_ref, o_ref):
        o_ref[...] = (g_ref[...].astype(F32) + r_ref[...].astype(F32)).astype(o_ref.dtype)

    blk = (None, tr, cols)
    grid_spec = pltpu.PrefetchScalarGridSpec(
        num_scalar_prefetch=1, grid=(N_CHIP, rows // tr),
        in_specs=[pl.BlockSpec(blk, lambda s, i, c: (2 * s + c[0], i, 0)), pl.BlockSpec(blk, lambda s, i, c: (s, i, 0))],
        out_specs=pl.BlockSpec(blk, lambda s, i, c: (s, i, 0)))
    out = pl.pallas_call(body, name=name, out_shape=jax.ShapeDtypeStruct((N_CHIP, rows, cols), g.dtype),
                         grid_spec=grid_spec, compiler_params=_params(("parallel", "parallel")),
                         )(core, g.reshape(N_DEV, rows, cols), got.reshape(N_CHIP, rows, cols))
    return out.reshape((N_CHIP,) + shard)


_HBM = pl.BlockSpec(memory_space=pltpu.HBM)
_MESH = pl.DeviceIdType.MESH


def _my_pos():
    return lax.axis_index("x"), lax.axis_index("y"), lax.axis_index("c")


def _flip(pos, k):
    x, y, c = pos
    fx, fy, fc = (k >> 2) & 1, (k >> 1) & 1, k & 1
    return (x ^ fx if fx else x, y ^ fy if fy else y, c ^ fc if fc else c)


def _index_of(pos):
    return 4 * pos[0] + 2 * pos[1] + pos[2]


def _all_gather(xs, *, name):
    n = len(xs)

    def body(*refs):
        x_refs, out_refs = refs[:n], refs[n:2 * n]
        send_sems, recv_sems = refs[2 * n:]
        me = _my_pos()
        sibling = _flip(me, 1)
        chips = [2, 4, 6]

        def copy(i, k, block_pos, to, from_x=False):
            blk = out_refs[i].at[_index_of(block_pos)]
            return pltpu.make_async_remote_copy(src_ref=x_refs[i] if from_x else blk, dst_ref=blk,
                                                send_sem=send_sems.at[k, i], recv_sem=recv_sems.at[k, i],
                                                device_id=to, device_id_type=_MESH)

        first = [copy(i, 1 + j, me, _flip(me, f), from_x=True) for j, f in enumerate(chips) for i in range(n)]
        first += [copy(i, 0, me, sibling, from_x=True) for i in range(n)]
        for cp in first:
            cp.start()
        passed = []
        for j, f in enumerate(chips):
            for i in range(n):
                copy(i, 1 + j, _flip(me, f), me).wait_recv()
                cp = copy(i, 4 + j, _flip(me, f), sibling)
                cp.start()
                passed.append(cp)
        for i in range(n):
            copy(i, 0, sibling, me).wait_recv()
        for j, f in enumerate(chips):
            for i in range(n):
                copy(i, 4 + j, _flip(sibling, f), me).wait_recv()
        for cp in first + passed:
            cp.wait_send()

    outs = pl.pallas_call(
        body, name=name, out_shape=tuple(jax.ShapeDtypeStruct((N_DEV,) + x.shape, x.dtype) for x in xs),
        in_specs=[_HBM] * n, out_specs=(_HBM,) * n,
        scratch_shapes=[pltpu.SemaphoreType.DMA((7, n)), pltpu.SemaphoreType.DMA((7, n))],
    )(*xs)
    mine = _index_of(_my_pos())
    res = []
    for x, o in zip(xs, outs):
        sel = (lax.broadcasted_iota(jnp.int32, (N_DEV,) + (1,) * x.ndim, 0) == mine)
        res.append(jnp.where(sel, x[None], o))
    return res


N_CHIP = 4


def _scatter_d2d(gs, *, name):
    n = len(gs)

    def body(*refs):
        g_refs, recv_refs = refs[:n], refs[n:2 * n]
        send_sems, recv_sems = refs[2 * n:]
        me = _my_pos()
        sibling = _flip(me, 1)
        c = me[2]
        sends = []
        for i in range(n):
            for s in range(N_CHIP):
                sends.append(pltpu.make_async_remote_copy(
                    src_ref=g_refs[i].at[2 * s + 1 - c], dst_ref=recv_refs[i].at[s], send_sem=send_sems.at[s, i],
                    recv_sem=recv_sems.at[s, i], device_id=sibling, device_id_type=_MESH))
        for cp in sends:
            cp.start()
        for cp in sends:
            cp.wait_recv()
        for cp in sends:
            cp.wait_send()

    shapes = tuple(jax.ShapeDtypeStruct((N_CHIP,) + g.shape[1:], g.dtype) for g in gs)
    return pl.pallas_call(
        body, name=name, out_shape=shapes, in_specs=[_HBM] * n, out_specs=(_HBM,) * n,
        scratch_shapes=[pltpu.SemaphoreType.DMA((N_CHIP, n)), pltpu.SemaphoreType.DMA((N_CHIP, n))],
    )(*gs)


def _scatter_ici(ss, *, name):
    n = len(ss)

    def body(*refs):
        s_refs, r_refs = refs[:n], refs[n:2 * n]
        send_sems, recv_sems = refs[2 * n:]
        me = _my_pos()
        my_chip = 2 * me[0] + me[1]
        sends, recvs = [], []
        for k in (1, 2, 3):
            peer = _flip(me, 2 * k)
            peer_chip = 2 * peer[0] + peer[1]
            for i in range(n):
                sends.append(pltpu.make_async_remote_copy(
                    src_ref=s_refs[i].at[peer_chip], dst_ref=r_refs[i].at[my_chip], send_sem=send_sems.at[k - 1, i],
                    recv_sem=recv_sems.at[k - 1, i], device_id=peer, device_id_type=_MESH))
                recvs.append(pltpu.make_async_remote_copy(
                    src_ref=s_refs[i].at[my_chip], dst_ref=r_refs[i].at[peer_chip], send_sem=send_sems.at[k - 1, i],
                    recv_sem=recv_sems.at[k - 1, i], device_id=peer, device_id_type=_MESH))
        for cp in sends:
            cp.start()
        for cp in recvs:
            cp.wait_recv()
        for cp in sends:
            cp.wait_send()

    return pl.pallas_call(
        body, name=name, out_shape=tuple(jax.ShapeDtypeStruct(s.shape, s.dtype) for s in ss),
        in_specs=[_HBM] * n, out_specs=(_HBM,) * n,
        scratch_shapes=[pltpu.SemaphoreType.DMA((3, n)), pltpu.SemaphoreType.DMA((3, n))],
    )(*ss)


def _t5_bucket(dist):
    n = jnp.maximum(dist, 0)
    max_exact = REL_BUCKETS // 2
    nf = jnp.maximum(n, 1).astype(F32)
    large = max_exact + (jnp.log(nf / max_exact) / math.log(REL_MAX_DIST / max_exact)
                         * (REL_BUCKETS - max_exact)).astype(jnp.int32)
    large = jnp.minimum(large, REL_BUCKETS - 1)
    return jnp.where(n < max_exact, n, large)


def _bias_tables(rel_bias):
    qi = jnp.arange(ATT_BLOCK)[:, None]
    ki = jnp.arange(2 * ATT_BLOCK)[None, :]
    rel = qi + ATT_BLOCK - ki
    out = []
    for gi, (window, dil) in enumerate(DIL_GROUPS):
        span = window // dil
        bucket = _t5_bucket(rel * dil)
        valid = (rel >= 0) & (rel <= span)
        rb = rel_bias[:, gi * HEADS_PER_GROUP:(gi + 1) * HEADS_PER_GROUP]
        tab = functools.reduce(jnp.add, [jnp.where((bucket == b)[:, :, None], rb[b], 0.0)
                                         for b in range(REL_BUCKETS)])
        tab = jnp.where(valid[:, :, None], tab, MASK_VALUE).transpose(2, 0, 1)
        out.append((tab.astype(F32), bucket.astype(jnp.int32)))
    return out


def _regroup(t, B, S, d):
    C = t.shape[-1]
    return t.reshape(B, S // d, d, C).swapaxes(1, 2)


def _ungroup(t):
    B, d, L, C = t.shape
    return t.swapaxes(1, 2).reshape(B * L * d, C)


def _group_qkv(qkv, gi):
    n = len(DIL_GROUPS) * GROUP_COLS
    return jnp.concatenate([qkv[:, j * n + gi * GROUP_COLS: j * n + (gi + 1) * GROUP_COLS] for j in range(3)], axis=1)


def _layer_fwd(l, x0, h1, mem2, W, P, tabs, next_pre_g, *, B, S):
    tag = f"l{l}_"
    sv = {"x0": x0, "h1": h1}
    za = _mm(h1, W["in_a"], out_dtype=ACT, name=tag + "mm_in_a")
    zb = _mm(h1, W["in_b"], out_dtype=ACT, name=tag + "mm_in_b")
    qkv = _mm(h1, W["in_c"], out_dtype=BF16, name=tag + "mm_in_c")
    zg = _mm(h1, W["in_g"], out_dtype=ACT, name=tag + "mm_in_g")
    a4, a2 = _bra_fwd(za, P["conv_a_w"], P["conv_a_b"], P["ln_a_g"], P["ln_a_b"], B=B, S=S, name=tag + "bra_fwd")
    ya = _mm(a4, W["a_out"], out_dtype=ACT, name=tag + "mm_a_out")
    b_s_t = P["b_s"].T
    p = _brb_fwd(zb, P["ln_b_g"], P["ln_b_b"], P["w_s"], b_s_t, name=tag + "brb_fwd")
    yb = _mm(p, W["b_out"], out_dtype=ACT, name=tag + "mm_b_out")
    os_, lses, qkv_gs = [], [], []
    for gi, (_, dil) in enumerate(DIL_GROUPS):
        qkv_g = _regroup(_group_qkv(qkv, gi), B, S, dil)
        o_g, lse_g = _attn_fwd(qkv_g, tabs[gi][0], name=tag + f"attn_fwd{gi}")
        qkv_gs.append(qkv_g)
        os_.append(_ungroup(o_g))
        lses.append(_ungroup(lse_g))
    oc, lse_tot = _attn_combine(os_, lses, name=tag + "attn_combine")
    yc = _mm(oc, W["c_out"], out_dtype=ACT, name=tag + "mm_c_out")
    merged = _merge_fwd(zg, P["b_gate"], ya, yb, yc, name=tag + "merge_fwd")
    y1 = _mm(merged, W["mix_out"], out_dtype=ACT, name=tag + "mm_mix")
    x1, h2 = _norm_fwd(x0, y1, P["mix_post_g"], P["x_pre_g"], name=tag + "norm1")
    q = _mm(h2, W["xq"], out_dtype=BF16, name=tag + "mm_xq")
    (memn,) = _norm_fwd(mem2, None, None, P["mem_g"], name=tag + "norm_mem")
    kv = _mm(memn, W["xkv"], out_dtype=BF16, name=tag + "mm_xkv")
    ox = _xattn_fwd(q, kv, B=B, S=S, name=tag + "xattn_fwd")
    y2 = _mm(ox, W["xo"], out_dtype=ACT, name=tag + "mm_xo")
    x2, h3 = _norm_fwd(x1, y2, P["x_post_g"], P["ffn_pre_g"], name=tag + "norm2")
    up = _mm(h3, W["up"], out_dtype=ACT, name=tag + "mm_up")
    act = _ffn_fwd(up, P["conv_f_w"], P["conv_f_b"], B=B, S=S, name=tag + "ffn_fwd")
    y3 = _mm(act, W["down"], out_dtype=ACT, name=tag + "mm_down")
    outs = _norm_fwd(x2, y3, P["ffn_post_g"], next_pre_g, name=tag + "norm3")
    x3 = outs[0]
    h_next = outs[1] if next_pre_g is not None else None
    sv.update(za=za, zb=zb, zg=zg, a4=a4, a2=a2, ya=ya, p=p, yb=yb, qkv_gs=qkv_gs, oc=oc, lse_tot=lse_tot, yc=yc,
              merged=merged, y1=y1, x1=x1, h2=h2, q=q, memn=memn, kv=kv, ox=ox, y2=y2, x2=x2, h3=h3, up=up, act=act,
              y3=y3, x3=x3, b_s_t=b_s_t)
    return x3, h_next, sv


def _layer_bwd(l, d, dh_next, next_pre_g, sv, mem2, W, WT, P, tabs, *, B, S):
    tag = f"l{l}_"
    G = {}
    if dh_next is not None:
        d, dy3, G["ffn_post_g"], dg_next = _norm_bwd(d, dh_next, sv["x3"], next_pre_g, sv["y3"], P["ffn_post_g"],
                                                     name=tag + "norm3_bwd")
    else:
        dy3, G["ffn_post_g"] = _norm_bwd(d, None, None, None, sv["y3"], P["ffn_post_g"], name=tag + "norm3_bwd")
        dg_next = None
    dact = _mm(dy3, WT["down"], out_dtype=ACT, name=tag + "mm_down_dx")
    G["w_down"] = _mm_tn(sv["act"], dy3, name=tag + "mm_down_dw")
    dgp, dval, G["conv_f_w"], G["conv_f_b"] = _ffn_bwd(dact, sv["up"], P["conv_f_w"], P["conv_f_b"], B=B, S=S,
                                                      name=tag + "ffn_bwd")
    dup = jnp.concatenate([dgp, dval], axis=1)
    dh3 = _mm(dup, WT["up"], out_dtype=ACT, name=tag + "mm_up_dx")
    G["w_up"] = _mm_tn(sv["h3"], dup, name=tag + "mm_up_dw")
    d, dy2, G["x_post_g"], G["ffn_pre_g"] = _norm_bwd(d, dh3, sv["x2"], P["ffn_pre_g"], sv["y2"], P["x_post_g"],
                                                      name=tag + "norm2_bwd")
    dox = _mm(dy2, WT["xo"], out_dtype=BF16, name=tag + "mm_xo_dx")
    G["w_xo"] = _mm_tn(sv["ox"], dy2, name=tag + "mm_xo_dw")
    dq, dkv = _xattn_bwd(sv["q"], sv["kv"], dox, B=B, S=S, name=tag + "xattn_bwd")
    dh2 = _mm(dq, WT["xq"], out_dtype=ACT, name=tag + "mm_xq_dx")
    G["w_xq"] = _mm_tn(sv["h2"], dq, name=tag + "mm_xq_dw")
    G["w_xkv"] = _mm_tn(sv["memn"], dkv, name=tag + "mm_xkv_dw")
    dmemn = _mm(dkv, WT["xkv"], out_dtype=F32, name=tag + "mm_xkv_dx")
    (G["mem_g"],) = _norm_bwd(None, dmemn, mem2, P["mem_g"], None, None, name=tag + "norm_mem_bwd")
    d, dy1, G["mix_post_g"], G["x_pre_g"] = _norm_bwd(d, dh2, sv["x1"], P["x_pre_g"], sv["y1"], P["mix_post_g"],
                                                      name=tag + "norm1_bwd")
    dm = _mm(dy1, WT["mix_out"], out_dtype=ACT, name=tag + "mm_mix_dx")
    G["w_mix_out"] = _mm_tn(sv["merged"], dy1, name=tag + "mm_mix_dw")
    dya, dyb, dyc, dzg, dbg = _merge_bwd(dm, sv["zg"], P["b_gate"], sv["ya"], sv["yb"], sv["yc"], name=tag + "merge_bwd")
    G["b_gate"] = dbg.reshape(P["b_gate"].shape)
    da4 = _mm(dya, WT["a_out"], out_dtype=ACT, name=tag + "mm_a_out_dx")
    G["w_a_out"] = _mm_tn(sv["a4"], dya, name=tag + "mm_a_out_dw")
    da2, G["ln_a_g"], G["ln_a_b"], G["conv_a_b"] = _bra_bwd_ln(da4, sv["a2"], P["ln_a_g"], P["ln_a_b"], name=tag + "bra_bwd_ln")
    dza, G["conv_a_w"] = _bra_bwd_conv(da2, sv["za"], P["conv_a_w"], B=B, S=S, name=tag + "bra_bwd_conv")
    dp = _mm(dyb, WT["b_out"], out_dtype=ACT, name=tag + "mm_b_out_dx")
    G["w_b_out"] = _mm_tn(sv["p"], dyb, name=tag + "mm_b_out_dw")
    dzb, G["w_s"], dbs_t, G["ln_b_g"], G["ln_b_b"] = _brb_bwd(dp, sv["zb"], P["ln_b_g"], P["ln_b_b"], P["w_s"],
                                                             sv["b_s_t"], name=tag + "brb_bwd")
    G["b_s"] = dbs_t.T
    doc = _mm(dyc, WT["c_out"], out_dtype=ACT, name=tag + "mm_c_out_dx")
    G["w_c_out"] = _mm_tn(sv["oc"], dyc, name=tag + "mm_c_out_dw")
    dd = _attn_rowdot(doc, sv["oc"], name=tag + "attn_rowdot")
    dq_parts, dk_parts, dv_parts, dbiases = [], [], [], []
    for gi, (_, dil) in enumerate(DIL_GROUPS):
        dqkv_g, dbias = _attn_bwd(sv["qkv_gs"][gi], _regroup(doc, B, S, dil), _regroup(sv["lse_tot"], B, S, dil),
                                  _regroup(dd, B, S, dil), tabs[gi][0], name=tag + f"attn_bwd{gi}")
        t = _ungroup(dqkv_g)
        dq_parts.append(t[:, 0:GROUP_COLS])
        dk_parts.append(t[:, GROUP_COLS:2 * GROUP_COLS])
        dv_parts.append(t[:, 2 * GROUP_COLS:3 * GROUP_COLS])
        dbiases.append(dbias)
    dqkv = jnp.concatenate(dq_parts + dk_parts + dv_parts, axis=1)
    G["dbias"] = jnp.concatenate(dbiases, axis=0)
    dh1 = _mm(dza, WT["in_a"], out_dtype=F32, name=tag + "mm_in_a_dx")
    dh1 = _mm(dzb, WT["in_b"], out_dtype=F32, name=tag + "mm_in_b_dx", add=dh1)
    dh1 = _mm(dqkv, WT["in_c"], out_dtype=F32, name=tag + "mm_in_c_dx", add=dh1)
    dh1 = _mm(dzg, WT["in_g"], out_dtype=ACT, name=tag + "mm_in_g_dx", add=dh1)
    h1 = sv["h1"]
    G["w_in"] = jnp.concatenate([_mm_tn(h1, dza, name=tag + "mm_in_a_dw"), _mm_tn(h1, dzb, name=tag + "mm_in_b_dw"),
                                 _mm_tn(h1, dqkv, name=tag + "mm_in_c_dw"), _mm_tn(h1, dzg, name=tag + "mm_in_g_dw")],
                                axis=1)
    return d, dh1, G, dg_next


_COL_SHARDED = ("w_in", "b_gate", "conv_a_w", "w_a_out", "w_b_out", "w_c_out", "w_xkv", "w_up", "conv_f_w")
_ROW_SHARDED = ("w_mix_out", "w_xq", "w_xo", "w_down")
_SHARDED_BIG = ("w_in", "w_a_out", "w_b_out", "w_c_out", "w_mix_out", "w_xq", "w_xkv", "w_xo", "w_up", "w_down")
_SHARDED_SMALL = ("b_gate", "conv_a_w", "conv_f_w")
_REPLICATED = ("mix_pre_g", "mix_post_g", "conv_a_b", "ln_a_g", "ln_a_b", "ln_b_g", "ln_b_b", "w_s", "b_s",
               "x_pre_g", "x_post_g", "mem_g", "ffn_pre_g", "ffn_post_g", "conv_f_b")
_WEIGHTS = ('rel_bias', 'mix_pre_g', 'mix_post_g', 'w_in', 'b_gate', 'conv_a_w', 'conv_a_b', 'ln_a_g', 'ln_a_b',
            'w_a_out', 'ln_b_g', 'ln_b_b', 'w_s', 'b_s', 'w_b_out', 'w_c_out', 'w_mix_out', 'x_pre_g', 'x_post_g',
            'mem_g', 'w_xq', 'w_xkv', 'w_xo', 'ffn_pre_g', 'ffn_post_g', 'w_up', 'conv_f_w', 'conv_f_b', 'w_down')
_PACK_COLS = 1024


def _shard_axis(name):
    return 1 if name in _ROW_SHARDED else 2


def _pack(parts, dtype, row_mult):
    flat = jnp.concatenate([p.astype(dtype).reshape(-1) for p in parts])
    n = flat.shape[0]
    unit = _PACK_COLS * row_mult
    padded = -(-n // unit) * unit
    flat = jnp.pad(flat, (0, padded - n))
    return flat.reshape(padded // _PACK_COLS, _PACK_COLS)


def _unpack(flat, shapes):
    out, off = [], 0
    for shp in shapes:
        n = math.prod(shp)
        out.append(flat[off:off + n].reshape(shp))
        off += n
    return out


def _join8(blocks, name):
    ax = _shard_axis(name)
    t = jnp.moveaxis(blocks, 0, ax)
    shp = t.shape
    return t.reshape(shp[:ax] + (shp[ax] * shp[ax + 1],) + shp[ax + 2:])


def _split8(full, name):
    ax = _shard_axis(name)
    shp = full.shape
    t = full.reshape(shp[:ax] + (N_DEV, shp[ax] // N_DEV) + shp[ax + 1:])
    return jnp.moveaxis(t, ax, 0)


def kernel(x, mem, rel_bias, mix_pre_g, mix_post_g, w_in, b_gate, conv_a_w, conv_a_b, ln_a_g, ln_a_b, w_a_out, ln_b_g, ln_b_b, w_s, b_s, w_b_out, w_c_out, w_mix_out, x_pre_g, x_post_g, mem_g, w_xq, w_xkv, w_xo, ffn_pre_g, ffn_post_g, w_up, conv_f_w, conv_f_b, w_down, loss_target, m_rel_bias, m_mix_pre_g, m_mix_post_g, m_w_in, m_b_gate, m_conv_a_w, m_conv_a_b, m_ln_a_g, m_ln_a_b, m_w_a_out, m_ln_b_g, m_ln_b_b, m_w_s, m_b_s, m_w_b_out, m_w_c_out, m_w_mix_out, m_x_pre_g, m_x_post_g, m_mem_g, m_w_xq, m_w_xkv, m_w_xo, m_ffn_pre_g, m_ffn_post_g, m_w_up, m_conv_f_w, m_conv_f_b, m_w_down, v_rel_bias, v_mix_pre_g, v_mix_post_g, v_w_in, v_b_gate, v_conv_a_w, v_conv_a_b, v_ln_a_g, v_ln_a_b, v_w_a_out, v_ln_b_g, v_ln_b_b, v_w_s, v_b_s, v_w_b_out, v_w_c_out, v_w_mix_out, v_x_pre_g, v_x_post_g, v_mem_g, v_w_xq, v_w_xkv, v_w_xo, v_ffn_pre_g, v_ffn_post_g, v_w_up, v_conv_f_w, v_conv_f_b, v_w_down):
    args = locals()
    w_loc = {n: args[n] for n in _WEIGHTS}
    m_loc = {n: args["m_" + n] for n in _WEIGHTS}
    v_loc = {n: args["v_" + n] for n in _WEIGHTS}

    sharded = _SHARDED_BIG + _SHARDED_SMALL
    wire = {n: (BF16 if n in _SHARDED_BIG else F32) for n in sharded}
    gathered = _all_gather([w_loc[n].astype(wire[n]) for n in sharded], name="gather_weights")
    full = dict(w_loc)
    full.update({n: _join8(g, n) for n, g in zip(sharded, gathered)})

    loss_part, grad_x, grads = _local_step(x, mem, loss_target, full)

    send = [_split8(grads[n], n).astype(wire[n]) for n in sharded]
    core = lax.axis_index("c").astype(jnp.int32).reshape(1)
    my_chip = (2 * lax.axis_index("x") + lax.axis_index("y")).astype(jnp.int32).reshape(1)
    got = _scatter_d2d(send, name="scatter_grads_d2d")
    chip_sums = dict(zip(sharded, [_chip_sum(g, r, core, name="chip_sum_" + n) for n, g, r in zip(sharded, send, got)]))
    g_parts = dict(zip(sharded, _scatter_ici([chip_sums[n] for n in sharded], name="scatter_grads_ici")))

    rep = ("rel_bias",) + _REPLICATED
    packed = _pack([grads[n] for n in rep], F32, 8)
    (allp,) = _all_gather([packed], name="gather_rep_grads")
    allp = allp.reshape(N_DEV, -1)
    off = 0
    for n in rep:
        size = math.prod(w_loc[n].shape)
        g_parts[n] = allp[:, off:off + size].reshape((N_DEV,) + w_loc[n].shape)
        off += size

    loss = lax.psum(loss_part, ("x", "y", "c"))

    g_loc, deltas, new_m, new_v = {}, {}, {}, {}
    for n in _WEIGHTS:
        own = {"own": chip_sums[n], "own_slot": my_chip} if n in chip_sums else {}
        g_loc[n], deltas[n], new_m[n], new_v[n] = _adamw(w_loc[n], g_parts[n], m_loc[n], v_loc[n], name="adamw_" + n,
                                                         **own)
    return (loss, grad_x, *[g_loc[n] for n in _WEIGHTS], *[deltas[n] for n in _WEIGHTS],
            *[new_m[n] for n in _WEIGHTS], *[new_v[n] for n in _WEIGHTS])


def _local_step(x, mem, loss_target, full):
    B, S, D = x.shape
    depth = full["w_in"].shape[0]
    x2d = x.reshape(B * S, D)
    mem2 = mem.reshape(-1, D)
    tabs = _bias_tables(full["rel_bias"])
    n_a, n_b, n_c = 1024, 1024, 3 * len(DIL_GROUPS) * GROUP_COLS
    Ws, WTs, Ps = [], [], []
    for l in range(depth):
        P = {n: full[n][l] for n in _WEIGHTS if n != "rel_bias"}
        w_in_l = full["w_in"][l].astype(BF16)
        W = {"in_a": w_in_l[:, :n_a], "in_b": w_in_l[:, n_a:n_a + n_b], "in_c": w_in_l[:, n_a + n_b:n_a + n_b + n_c],
             "in_g": w_in_l[:, n_a + n_b + n_c:]}
        for key, nm in (("a_out", "w_a_out"), ("b_out", "w_b_out"), ("c_out", "w_c_out"), ("mix_out", "w_mix_out"),
                        ("xq", "w_xq"), ("xkv", "w_xkv"), ("xo", "w_xo"), ("up", "w_up"), ("down", "w_down")):
            W[key] = full[nm][l].astype(BF16)
        Ws.append(W)
        WTs.append({k: v.T for k, v in W.items()})
        Ps.append(P)

    (h1,) = _norm_fwd(x2d, None, None, Ps[0]["mix_pre_g"], name="norm0")
    xc = x2d
    saved = []
    for l in range(depth):
        nxt = Ps[l + 1]["mix_pre_g"] if l + 1 < depth else None
        xc, h1, sv = _layer_fwd(l, xc, h1, mem2, Ws[l], Ps[l], tabs, nxt, B=B, S=S)
        saved.append(sv)
    loss_vec, d = _loss_kernel(xc, loss_target.reshape(B * S, D), name="loss")

    layer_grads = [None] * depth
    dh_next = None
    for l in reversed(range(depth)):
        nxt = Ps[l + 1]["mix_pre_g"] if l + 1 < depth else None
        d, dh1, G, dg_next = _layer_bwd(l, d, dh_next, nxt, saved[l], mem2, Ws[l], WTs[l], Ps[l], tabs, B=B, S=S)
        if dg_next is not None:
            layer_grads[l + 1]["mix_pre_g"] = dg_next
        layer_grads[l] = G
        dh_next = dh1
    grad_x2d, dg0 = _norm_bwd(d, dh_next, x2d, Ps[0]["mix_pre_g"], None, None, name="norm0_bwd")
    layer_grads[0]["mix_pre_g"] = dg0

    grads = {}
    for n in _WEIGHTS:
        if n == "rel_bias":
            continue
        grads[n] = jnp.stack([layer_grads[l][n].reshape(full[n].shape[1:]) for l in range(depth)], axis=0)
    dbias = jnp.stack([layer_grads[l]["dbias"] for l in range(depth)], axis=0)
    buckets = jnp.stack([t[1] for t in tabs], axis=0)
    rb = _bucket_sum(dbias, buckets, name="rel_bias_grad")
    grads["rel_bias"] = rb[:, :full["rel_bias"].shape[1]]
    return loss_vec[0, 0], grad_x2d.reshape(B, S, D), grads
```

```python
import functools
import math

import jax
import jax.numpy as jnp
from jax import lax
from jax.experimental import pallas as pl
from jax.experimental.pallas import tpu as pltpu

F32 = jnp.float32
BF16 = jnp.bfloat16

N_DEV = 8
NORM_EPS = 1e-6
LN_EPS = 1e-5
CONV_K = 31
SG_CHUNK = 128
ATT_BLOCK = 128
HEAD_DIM = 64
HEADS_PER_GROUP = 4
GROUP_COLS = HEADS_PER_GROUP * HEAD_DIM
DIL_GROUPS = ((128, 1), (512, 4), (2048, 16))
REL_BUCKETS = 32
REL_MAX_DIST = 2048
X_HEADS = 4
ATT_SCALE = HEAD_DIM ** -0.5
MASK_VALUE = -1e30

ADAM_LR = 0.001
ADAM_B1 = 0.9
ADAM_B2 = 0.999
ADAM_EPS = 1e-08
ADAM_WD = 0.01
ADAM_STEP = 10

LANE = 128
ACT = BF16
ROW_TILE = 512
VMEM_LIMIT = 48 << 20

_NT = (((1,), (1,)), ((), ()))
_TN = (((0,), (0,)), ((), ()))


def _params(sem, vmem=VMEM_LIMIT):
    return pltpu.CompilerParams(dimension_semantics=sem, vmem_limit_bytes=vmem)


def _pick(n, cap):
    if n <= cap:
        return n
    best = None
    for t in range(LANE, cap + 1, LANE):
        if n % t == 0:
            best = t
    assert best is not None, (n, cap)
    return best


def _sigmoid(x):
    return 1.0 / (1.0 + jnp.exp(-x))


_GELU_C = math.sqrt(2.0 / math.pi)


def _gelu(x):
    return 0.5 * x * (1.0 + jnp.tanh(_GELU_C * (x + 0.044715 * x * x * x)))


def _gelu_and_grad(x):
    t = jnp.tanh(_GELU_C * (x + 0.044715 * x * x * x))
    g = 0.5 * x * (1.0 + t)
    dg = 0.5 * (1.0 + t) + 0.5 * x * (1.0 - t * t) * _GELU_C * (1.0 + 3 * 0.044715 * x * x)
    return g, dg


MM_VMEM_BUDGET = 36 << 20


def _divisors128(n):
    return [n] + [t for t in range(n - n % LANE, 0, -LANE) if n % t == 0 and t != n]


class _Hosted:
    def __init__(self, hosts, name):
        self.ex = hosts.get(name) if hosts is not None else None
        self.hosts, self.name = hosts, name

    @property
    def on(self):
        return self.ex is not None

    def specs(self):
        if not self.on:
            return [], [], [], []
        n = self.ex.n_copies
        return (list(self.ex.ins), [_HBM] * len(self.ex.ins), list(self.ex.out_shapes),
                [pltpu.SemaphoreType.DMA((n,)), pltpu.SemaphoreType.DMA((n,))])

    def split(self, refs, n_in, n_out, n_scratch):
        if not self.on:
            return list(refs), None
        ni, no = len(self.ex.ins), len(self.ex.out_shapes)
        refs = list(refs)
        ins, rest = refs[:n_in], refs[n_in:]
        ex_in, rest = rest[:ni], rest[ni:]
        outs, rest = rest[:n_out], rest[n_out:]
        ex_out, rest = rest[:no], rest[no:]
        scratch, sems = rest[:n_scratch], rest[n_scratch:]
        return ins + outs + scratch, (ex_in, ex_out, sems[0], sems[1])

    def run(self, ex_refs, grid):
        if not self.on:
            return
        ids = [pl.program_id(ax) for ax in range(len(grid))]
        first = functools.reduce(jnp.logical_and, [i == 0 for i in ids])
        last = functools.reduce(jnp.logical_and, [i == g - 1 for i, g in zip(ids, grid)])

        @pl.when(first)
        def _():
            sends, _ = self.ex.make(*ex_refs)
            for cp in sends:
                cp.start()

        return last

    def finish(self, ex_refs, last):
        if not self.on:
            return

        @pl.when(last)
        def _():
            sends, recvs = self.ex.make(*ex_refs)
            for cp in recvs:
                cp.wait_recv()
            for cp in sends:
                cp.wait_send()

    def done(self, outs):
        if self.on:
            self.hosts.put(self.name, list(outs))


def _mm(a, b, *, out_dtype, name, add=None, hosts=None):
    M, K = a.shape
    K2, N = b.shape
    assert K == K2
    has_add = add is not None
    ob = jnp.dtype(out_dtype).itemsize
    tm = _pick(M, 1024)
    tk = _pick(K, 3072)
    nk = K // tk
    for tn in _divisors128(N):
        use = 2 * (tm * tk * a.dtype.itemsize + tk * tn * b.dtype.itemsize + tm * tn * (ob + (4 if has_add else 0)))
        if use + (tm * tn * 4 if nk > 1 else 0) <= MM_VMEM_BUDGET:
            break
    grid = (M // tm, N // tn, nk)
    hosted = _Hosted(hosts, name)
    n_in = 3 if has_add else 2
    n_scr = 1 if nk > 1 else 0

    def body(*refs):
        own, ex_refs = hosted.split(refs, n_in, 1, n_scr)
        last = hosted.run(ex_refs, grid)
        a_ref, b_ref = own[0], own[1]
        c_ref = own[2] if has_add else None
        o_ref = own[n_in]
        part = jnp.dot(a_ref[...].astype(BF16), b_ref[...].astype(BF16), preferred_element_type=F32)
        if nk == 1:
            if has_add:
                part = part + c_ref[...]
            o_ref[...] = part.astype(o_ref.dtype)
        else:
            acc_ref = own[n_in + 1]
            k = pl.program_id(2)

            @pl.when(k == 0)
            def _():
                acc_ref[...] = part

            @pl.when(k > 0)
            def _():
                acc_ref[...] += part

            @pl.when(k == nk - 1)
            def _():
                r = acc_ref[...]
                if has_add:
                    r = r + c_ref[...]
                o_ref[...] = r.astype(o_ref.dtype)
        hosted.finish(ex_refs, last)

    in_specs = [pl.BlockSpec((tm, tk), lambda i, j, k: (i, k)), pl.BlockSpec((tk, tn), lambda i, j, k: (k, j))]
    args = [a, b]
    if has_add:
        in_specs.append(pl.BlockSpec((tm, tn), lambda i, j, k: (i, j)))
        args.append(add)
    ex_args, ex_in_specs, ex_out_shapes, ex_scratch = hosted.specs()
    outs = pl.pallas_call(
        body, name=name, out_shape=(jax.ShapeDtypeStruct((M, N), out_dtype), *ex_out_shapes),
        grid=grid, in_specs=in_specs + ex_in_specs,
        out_specs=(pl.BlockSpec((tm, tn), lambda i, j, k: (i, j)), *([_HBM] * len(ex_out_shapes))),
        scratch_shapes=([pltpu.VMEM((tm, tn), F32)] if nk > 1 else []) + ex_scratch,
        compiler_params=_params(("arbitrary",) * 3 if hosted.on else ("parallel", "parallel", "arbitrary")),
    )(*args, *ex_args)
    hosted.done(outs[1:])
    return outs[0]


def _mm_tn(a, b, *, name, hosts=None):
    T, M = a.shape
    T2, N = b.shape
    assert T == T2
    tm = _pick(M, 1536)
    tt = _pick(T, 2048)
    nt = T // tt
    for tn in _divisors128(N):
        if 2 * (tt * tm * a.dtype.itemsize + tt * tn * b.dtype.itemsize + tm * tn * 4) <= MM_VMEM_BUDGET:
            break
    grid = (M // tm, N // tn, nt)
    hosted = _Hosted(hosts, name)

    def body(*refs):
        (a_ref, b_ref, o_ref), ex_refs = hosted.split(refs, 2, 1, 0)
        last = hosted.run(ex_refs, grid)
        k = pl.program_id(2)
        part = lax.dot_general(a_ref[...].astype(BF16), b_ref[...].astype(BF16), _TN, preferred_element_type=F32)

        @pl.when(k == 0)
        def _():
            o_ref[...] = part

        @pl.when(k > 0)
        def _():
            o_ref[...] += part

        hosted.finish(ex_refs, last)

    ex_args, ex_in_specs, ex_out_shapes, ex_scratch = hosted.specs()
    outs = pl.pallas_call(
        body, name=name, out_shape=(jax.ShapeDtypeStruct((M, N), F32), *ex_out_shapes),
        grid=grid,
        in_specs=[pl.BlockSpec((tt, tm), lambda i, j, k: (k, i)), pl.BlockSpec((tt, tn), lambda i, j, k: (k, j))]
        + ex_in_specs,
        out_specs=(pl.BlockSpec((tm, tn), lambda i, j, k: (i, j)), *([_HBM] * len(ex_out_shapes))),
        scratch_shapes=ex_scratch,
        compiler_params=_params(("arbitrary",) * 3 if hosted.on else ("parallel", "parallel", "arbitrary")),
    )(a, b, *ex_args)
    hosted.done(outs[1:])
    return outs[0]


def _rms(x):
    r = lax.rsqrt(jnp.mean(x * x, axis=-1, keepdims=True) + NORM_EPS)
    return x * r, r


def _row_spec(cols, tr=ROW_TILE):
    return pl.BlockSpec((tr, cols), lambda i: (i, 0))


def _vec_spec(cols):
    return pl.BlockSpec((1, cols), lambda i: (0, 0))


def _norm_fwd(x, y, g_post, g_pre, *, name):
    T, D = x.shape
    has_post = y is not None
    has_pre = g_pre is not None

    def body(*refs):
        refs = list(refs)
        x_ref = refs.pop(0)
        xn = x_ref[...]
        if has_post:
            y_ref = refs.pop(0)
            gp_ref = refs.pop(0)
        if has_pre:
            gq_ref = refs.pop(0)
        if has_post:
            yh, _ = _rms(y_ref[...].astype(F32))
            xn = xn + yh * gp_ref[...]
            refs.pop(0)[...] = xn
        if has_pre:
            xh, _ = _rms(xn)
            refs.pop(0)[...] = (xh * gq_ref[...]).astype(BF16)

    args, in_specs, out_shape, out_specs = [x], [_row_spec(D)], [], []
    if has_post:
        args += [y, g_post.reshape(1, D)]
        in_specs += [_row_spec(D), _vec_spec(D)]
        out_shape.append(jax.ShapeDtypeStruct((T, D), F32))
        out_specs.append(_row_spec(D))
    if has_pre:
        args.append(g_pre.reshape(1, D))
        in_specs.append(_vec_spec(D))
        out_shape.append(jax.ShapeDtypeStruct((T, D), BF16))
        out_specs.append(_row_spec(D))
    return pl.pallas_call(body, name=name, out_shape=tuple(out_shape), grid=(T // ROW_TILE,), in_specs=in_specs,
                          out_specs=tuple(out_specs), compiler_params=_params(("parallel",)))(*args)


def _norm_bwd(dres, dh, x_new, g_pre, y, g_post, *, name):
    has_res = dres is not None
    has_pre = dh is not None
    has_post = y is not None
    out_d = has_pre and (has_res or has_post)
    T, D = (dres if has_res else dh).shape
    if not has_res:
        assert not has_post

    def body(*refs):
        refs = list(refs)
        i = pl.program_id(0)
        d = refs.pop(0)[...] if has_res else None
        if has_pre:
            dh_v = refs.pop(0)[...].astype(F32)
            xh, r = _rms(refs.pop(0)[...])
            gq = refs.pop(0)[...]
        if has_post:
            yh, ry = _rms(refs.pop(0)[...].astype(F32))
            gp = refs.pop(0)[...]
        if has_pre:
            dxh = dh_v * gq
            dpre = r * (dxh - xh * jnp.mean(dxh * xh, axis=-1, keepdims=True))
            d = dpre if d is None else d + dpre
            dgq = jnp.sum(dh_v * xh, axis=0, keepdims=True)
        if out_d:
            refs.pop(0)[...] = d
        if has_post:
            dyh = d * gp
            refs.pop(0)[...] = (ry * (dyh - yh * jnp.mean(dyh * yh, axis=-1, keepdims=True))).astype(BF16)
            dgp_ref = refs.pop(0)
            dgp = jnp.sum(d * yh, axis=0, keepdims=True)

            @pl.when(i == 0)
            def _():
                dgp_ref[...] = dgp

            @pl.when(i > 0)
            def _():
                dgp_ref[...] += dgp
        if has_pre:
            dgq_ref = refs.pop(0)

            @pl.when(i == 0)
            def _():
                dgq_ref[...] = dgq

            @pl.when(i > 0)
            def _():
                dgq_ref[...] += dgq

    args, in_specs, out_shape, out_specs = [], [], [], []
    if has_res:
        args.append(dres)
        in_specs.append(_row_spec(D))
    if has_pre:
        args += [dh, x_new, g_pre.reshape(1, D)]
        in_specs += [_row_spec(D), _row_spec(D), _vec_spec(D)]
    if has_post:
        args += [y, g_post.reshape(1, D)]
        in_specs += [_row_spec(D), _vec_spec(D)]
    if out_d:
        out_shape.append(jax.ShapeDtypeStruct((T, D), F32))
        out_specs.append(_row_spec(D))
    if has_post:
        out_shape += [jax.ShapeDtypeStruct((T, D), BF16), jax.ShapeDtypeStruct((1, D), F32)]
        out_specs += [_row_spec(D), _vec_spec(D)]
    if has_pre:
        out_shape.append(jax.ShapeDtypeStruct((1, D), F32))
        out_specs.append(_vec_spec(D))
    return pl.pallas_call(body, name=name, out_shape=tuple(out_shape), grid=(T // ROW_TILE,), in_specs=in_specs,
                          out_specs=tuple(out_specs), compiler_params=_params(("arbitrary",)))(*args)


def _loss_kernel(xf, target, *, name):
    T, D = xf.shape

    def body(x_ref, t_ref, loss_ref, d_ref):
        i = pl.program_id(0)
        e = x_ref[...] - t_ref[...]
        d_ref[...] = e * (1.0 / D)
        part = jnp.sum(jnp.sum(e * e, axis=0, keepdims=True), axis=1, keepdims=True) * (0.5 / D)
        part = jnp.broadcast_to(part, (1, LANE))

        @pl.when(i == 0)
        def _():
            loss_ref[...] = part

        @pl.when(i > 0)
        def _():
            loss_ref[...] += part

    return pl.pallas_call(body, name=name,
                          out_shape=(jax.ShapeDtypeStruct((1, LANE), F32), jax.ShapeDtypeStruct((T, D), F32)),
                          grid=(T // ROW_TILE,), in_specs=[_row_spec(D), _row_spec(D)],
                          out_specs=(_vec_spec(LANE), _row_spec(D)), compiler_params=_params(("arbitrary",)))(xf, target)


CONV_PAD = 32
CONV_CH = 64


def _ln_stats(x):
    mu = jnp.mean(x, axis=-1, keepdims=True)
    xc = x - mu
    rstd = lax.rsqrt(jnp.mean(xc * xc, axis=-1, keepdims=True) + LN_EPS)
    return xc * rstd, rstd


def _acc_out(ref, val, first):
    @pl.when(first)
    def _():
        ref[...] = val

    @pl.when(jnp.logical_not(first))
    def _():
        ref[...] += val


def _bra_fwd(za, conv_w, conv_b, ln_g, ln_b, *, B, S, name):
    C = conv_w.shape[1]

    def body(za_ref, w_ref, cb_ref, g_ref, b_ref, a4_ref, a2_ref, pad_ref):
        pad_ref[0:CONV_PAD, :] = jnp.zeros((CONV_PAD, C), F32)
        pad_ref[CONV_PAD:, :] = za_ref[:, 0:C].astype(F32) * _sigmoid(za_ref[:, C:2 * C].astype(F32))

        def chunk(i, _):
            base = pl.multiple_of(i * CONV_CH, CONV_CH)
            win = pad_ref[pl.ds(base, CONV_CH + CONV_PAD), :]
            acc = jnp.broadcast_to(cb_ref[...], (CONV_CH, C))
            for j in range(CONV_K):
                off = CONV_PAD - (CONV_K - 1) + j
                acc = acc + w_ref[j:j + 1, :] * win[off:off + CONV_CH, :]
            a2_ref[pl.ds(base, CONV_CH), :] = acc
            xh, _ = _ln_stats(acc)
            a3 = xh * g_ref[...] + b_ref[...]
            a4_ref[pl.ds(base, CONV_CH), :] = (a3 * _sigmoid(a3)).astype(BF16)
            return 0

        lax.fori_loop(0, S // CONV_CH, chunk, 0)

    vec = pl.BlockSpec((1, C), lambda b: (0, 0))
    return pl.pallas_call(
        body, name=name,
        out_shape=(jax.ShapeDtypeStruct((B * S, C), BF16), jax.ShapeDtypeStruct((B * S, C), F32)),
        grid=(B,),
        in_specs=[pl.BlockSpec((S, 2 * C), lambda b: (b, 0)), pl.BlockSpec((CONV_K, C), lambda b: (0, 0)), vec, vec, vec],
        out_specs=(pl.BlockSpec((S, C), lambda b: (b, 0)), pl.BlockSpec((S, C), lambda b: (b, 0))),
        scratch_shapes=[pltpu.VMEM((S + CONV_PAD, C), F32)],
        compiler_params=_params(("parallel",)),
    )(za, conv_w, conv_b.reshape(1, C), ln_g.reshape(1, C), ln_b.reshape(1, C))


def _bra_bwd_ln(da4, a2, ln_g, ln_b, *, name):
    T, C = a2.shape

    def body(da4_ref, a2_ref, g_ref, b_ref, da2_ref, dg_ref, db_ref, dcb_ref):
        first = pl.program_id(0) == 0
        xh, rstd = _ln_stats(a2_ref[...])
        a3 = xh * g_ref[...] + b_ref[...]
        sg = _sigmoid(a3)
        da3 = da4_ref[...].astype(F32) * (sg * (1.0 + a3 * (1.0 - sg)))
        dxh = da3 * g_ref[...]
        da2 = rstd * (dxh - jnp.mean(dxh, axis=-1, keepdims=True) - xh * jnp.mean(dxh * xh, axis=-1, keepdims=True))
        da2_ref[...] = da2
        _acc_out(dg_ref, jnp.sum(da3 * xh, axis=0, keepdims=True), first)
        _acc_out(db_ref, jnp.sum(da3, axis=0, keepdims=True), first)
        _acc_out(dcb_ref, jnp.sum(da2, axis=0, keepdims=True), first)

    vec = _vec_spec(C)
    vshape = jax.ShapeDtypeStruct((1, C), F32)
    return pl.pallas_call(body, name=name, out_shape=(jax.ShapeDtypeStruct((T, C), F32), vshape, vshape, vshape),
                          grid=(T // ROW_TILE,), in_specs=[_row_spec(C), _row_spec(C), vec, vec],
                          out_specs=(_row_spec(C), vec, vec, vec),
                          compiler_params=_params(("arbitrary",)))(da4, a2, ln_g.reshape(1, C), ln_b.reshape(1, C))


def _bra_bwd_conv(da2, za, conv_w, *, B, S, name):
    C = conv_w.shape[1]
    SUB = 8

    def body(da2_ref, za_ref, w_ref, dza_ref, dw_ref, pad1_ref, pad2_ref, dwacc_ref):
        first = pl.program_id(0) == 0
        pad1_ref[0:CONV_PAD, :] = jnp.zeros((CONV_PAD, C), F32)
        pad1_ref[CONV_PAD:, :] = za_ref[:, 0:C].astype(F32) * _sigmoid(za_ref[:, C:2 * C].astype(F32))
        pad2_ref[0:S, :] = da2_ref[...]
        pad2_ref[S:, :] = jnp.zeros((CONV_PAD, C), F32)
        dwacc_ref[...] = jnp.zeros_like(dwacc_ref)

        def chunk(i, _):
            base = pl.multiple_of(i * CONV_CH, CONV_CH)
            rows = pl.ds(base, CONV_CH)
            win1 = pad1_ref[pl.ds(base, CONV_CH + CONV_PAD), :]
            win2 = pad2_ref[pl.ds(base, CONV_CH + CONV_PAD), :]
            d = win2[0:CONV_CH, :]
            da1 = jnp.zeros((CONV_CH, C), F32)
            for j in range(CONV_K):
                off2 = CONV_K - 1 - j
                da1 = da1 + w_ref[j:j + 1, :] * win2[off2:off2 + CONV_CH, :]
                off1 = CONV_PAD - (CONV_K - 1) + j
                prod = d * win1[off1:off1 + CONV_CH, :]
                dwacc_ref[j * SUB:(j + 1) * SUB, :] += jnp.sum(prod.reshape(CONV_CH // SUB, SUB, C), axis=0)
            a_val = za_ref[rows, 0:C].astype(F32)
            sg = _sigmoid(za_ref[rows, C:2 * C].astype(F32))
            dza_ref[rows, 0:C] = (da1 * sg).astype(BF16)
            dza_ref[rows, C:2 * C] = (da1 * a_val * sg * (1.0 - sg)).astype(BF16)
            return 0

        lax.fori_loop(0, S // CONV_CH, chunk, 0)
        _acc_out(dw_ref, jnp.sum(dwacc_ref[...].reshape(CONV_K, SUB, C), axis=1), first)

    return pl.pallas_call(
        body, name=name,
        out_shape=(jax.ShapeDtypeStruct((B * S, 2 * C), BF16), jax.ShapeDtypeStruct((CONV_K, C), F32)),
        grid=(B,),
        in_specs=[pl.BlockSpec((S, C), lambda b: (b, 0)), pl.BlockSpec((S, 2 * C), lambda b: (b, 0)),
                  pl.BlockSpec((CONV_K, C), lambda b: (0, 0))],
        out_specs=(pl.BlockSpec((S, 2 * C), lambda b: (b, 0)), pl.BlockSpec((CONV_K, C), lambda b: (0, 0))),
        scratch_shapes=[pltpu.VMEM((S + CONV_PAD, C), F32), pltpu.VMEM((S + CONV_PAD, C), F32),
                        pltpu.VMEM((CONV_K * SUB, C), F32)],
        compiler_params=_params(("arbitrary",)),
    )(da2, za, conv_w)


def _tril_mask():
    r = lax.broadcasted_iota(jnp.int32, (SG_CHUNK, SG_CHUNK), 0)
    c = lax.broadcasted_iota(jnp.int32, (SG_CHUNK, SG_CHUNK), 1)
    return c <= r


def _brb_fwd(zb, ln_g, ln_b, w_s, b_s_t, *, name):
    T, C2 = zb.shape
    C = C2 // 2
    G = w_s.shape[0]
    GC = C // G

    def body(zb_ref, g_ref, b_ref, ws_ref, bs_ref, p_ref):
        z = _gelu(zb_ref[...].astype(F32))
        u = z[:, 0:C]
        xh, _ = _ln_stats(z[:, C:2 * C])
        v1 = (xh * g_ref[...] + b_ref[...]).astype(BF16)
        mask = _tril_mask()
        outs = []
        for gi in range(G):
            ws = jnp.where(mask, ws_ref[gi], 0.0).astype(BF16)
            v2 = jnp.dot(ws, v1[:, gi * GC:(gi + 1) * GC], preferred_element_type=F32) + bs_ref[:, gi:gi + 1]
            outs.append(v2)
        p_ref[...] = (u * jnp.concatenate(outs, axis=1)).astype(BF16)

    return pl.pallas_call(
        body, name=name, out_shape=jax.ShapeDtypeStruct((T, C), BF16), grid=(T // SG_CHUNK,),
        in_specs=[_row_spec(C2, SG_CHUNK), _vec_spec(C), _vec_spec(C),
                  pl.BlockSpec((G, SG_CHUNK, SG_CHUNK), lambda i: (0, 0, 0)), pl.BlockSpec((SG_CHUNK, G), lambda i: (0, 0))],
        out_specs=_row_spec(C, SG_CHUNK), compiler_params=_params(("parallel",)),
    )(zb, ln_g.reshape(1, C), ln_b.reshape(1, C), w_s, b_s_t)


def _brb_bwd(dp, zb, ln_g, ln_b, w_s, b_s_t, *, name):
    T, C2 = zb.shape
    C = C2 // 2
    G = w_s.shape[0]
    GC = C // G

    def body(dp_ref, zb_ref, g_ref, b_ref, ws_ref, bs_ref, dzb_ref, dws_ref, dbs_ref, dg_ref, db_ref):
        first = pl.program_id(0) == 0
        z, dz = _gelu_and_grad(zb_ref[...].astype(F32))
        u = z[:, 0:C]
        xh, rstd = _ln_stats(z[:, C:2 * C])
        v1 = (xh * g_ref[...] + b_ref[...]).astype(BF16)
        dp_v = dp_ref[...].astype(F32)
        mask = _tril_mask()
        v2s, dv1s, dwss, dbss = [], [], [], []
        for gi in range(G):
            cols = slice(gi * GC, (gi + 1) * GC)
            ws = jnp.where(mask, ws_ref[gi], 0.0).astype(BF16)
            v2s.append(jnp.dot(ws, v1[:, cols], preferred_element_type=F32) + bs_ref[:, gi:gi + 1])
            dv2 = dp_v[:, cols] * u[:, cols]
            dv2b = dv2.astype(BF16)
            dbss.append(jnp.sum(dv2, axis=1, keepdims=True))
            dwss.append(jnp.where(mask, lax.dot_general(dv2b, v1[:, cols], _NT, preferred_element_type=F32), 0.0))
            dv1s.append(lax.dot_general(ws, dv2b, _TN, preferred_element_type=F32))
        du = dp_v * jnp.concatenate(v2s, axis=1)
        dv1 = jnp.concatenate(dv1s, axis=1)
        dxh = dv1 * g_ref[...]
        dv0 = rstd * (dxh - jnp.mean(dxh, axis=-1, keepdims=True) - xh * jnp.mean(dxh * xh, axis=-1, keepdims=True))
        dzb_ref[:, 0:C] = (du * dz[:, 0:C]).astype(BF16)
        dzb_ref[:, C:2 * C] = (dv0 * dz[:, C:2 * C]).astype(BF16)
        _acc_out(dws_ref, jnp.stack(dwss, axis=0), first)
        _acc_out(dbs_ref, jnp.concatenate(dbss, axis=1), first)
        _acc_out(dg_ref, jnp.sum(dv1 * xh, axis=0, keepdims=True), first)
        _acc_out(db_ref, jnp.sum(dv1, axis=0, keepdims=True), first)

    wspec = pl.BlockSpec((G, SG_CHUNK, SG_CHUNK), lambda i: (0, 0, 0))
    bspec = pl.BlockSpec((SG_CHUNK, G), lambda i: (0, 0))
    return pl.pallas_call(
        body, name=name,
        out_shape=(jax.ShapeDtypeStruct((T, C2), BF16), jax.ShapeDtypeStruct((G, SG_CHUNK, SG_CHUNK), F32),
                   jax.ShapeDtypeStruct((SG_CHUNK, G), F32), jax.ShapeDtypeStruct((1, C), F32),
                   jax.ShapeDtypeStruct((1, C), F32)),
        grid=(T // SG_CHUNK,),
        in_specs=[_row_spec(C, SG_CHUNK), _row_spec(C2, SG_CHUNK), _vec_spec(C), _vec_spec(C), wspec, bspec],
        out_specs=(_row_spec(C2, SG_CHUNK), wspec, bspec, _vec_spec(C), _vec_spec(C)),
        compiler_params=_params(("arbitrary",)),
    )(dp, zb, ln_g.reshape(1, C), ln_b.reshape(1, C), w_s, b_s_t)


def _attn_fwd(qkv_g, bias, *, name):
    B, d, L, _ = qkv_g.shape
    nb = L // ATT_BLOCK
    GCOL = GROUP_COLS

    def body(qkv_ref, bias_ref, o_ref, lse_ref):
        def blk(r0, first):
            nk = ATT_BLOCK if first else 2 * ATT_BLOCK
            k0 = r0 if first else r0 - ATT_BLOCK
            qb = qkv_ref[pl.ds(r0, ATT_BLOCK), 0:GCOL]
            kb = qkv_ref[pl.ds(k0, nk), GCOL:2 * GCOL]
            vb = qkv_ref[pl.ds(k0, nk), 2 * GCOL:3 * GCOL]
            outs, lses = [], []
            for h in range(HEADS_PER_GROUP):
                sl = slice(h * HEAD_DIM, (h + 1) * HEAD_DIM)
                bh = bias_ref[h, :, ATT_BLOCK:2 * ATT_BLOCK] if first else bias_ref[h]
                s = lax.dot_general(qb[:, sl], kb[:, sl], _NT, preferred_element_type=F32) * ATT_SCALE + bh
                m = jnp.max(s, axis=1, keepdims=True)
                e = jnp.exp(s - m)
                ssum = jnp.sum(e, axis=1, keepdims=True)
                outs.append(jnp.dot(e.astype(BF16), vb[:, sl], preferred_element_type=F32) / ssum)
                lses.append(jnp.broadcast_to(m + jnp.log(ssum), (ATT_BLOCK, HEAD_DIM)))
            o_ref[pl.ds(r0, ATT_BLOCK), :] = jnp.concatenate(outs, axis=1)
            lse_ref[pl.ds(r0, ATT_BLOCK), :] = jnp.concatenate(lses, axis=1)

        blk(0, True)
        if nb > 1:
            def loop(n, _):
                blk(pl.multiple_of(n * ATT_BLOCK, ATT_BLOCK), False)
                return 0

            lax.fori_loop(1, nb, loop, 0)

    spec = lambda cols: pl.BlockSpec((None, None, L, cols), lambda b, c: (b, c, 0, 0))
    oshape = jax.ShapeDtypeStruct((B, d, L, GCOL), F32)
    return pl.pallas_call(
        body, name=name, out_shape=(oshape, oshape), grid=(B, d),
        in_specs=[spec(3 * GCOL), pl.BlockSpec((HEADS_PER_GROUP, ATT_BLOCK, 2 * ATT_BLOCK), lambda b, c: (0, 0, 0))],
        out_specs=(spec(GCOL), spec(GCOL)), compiler_params=_params(("parallel", "parallel")),
    )(qkv_g, bias)


def _attn_bwd(qkv_g, doc_g, lse_g, dd_g, bias, *, name):
    B, d, L, _ = qkv_g.shape
    nb = L // ATT_BLOCK
    GCOL = GROUP_COLS

    def body(qkv_ref, doc_ref, lse_ref, dd_ref, bias_ref, dqkv_ref, dbias_ref, dk_ref, dv_ref):
        @pl.when(jnp.logical_and(pl.program_id(0) == 0, pl.program_id(1) == 0))
        def _():
            dbias_ref[...] = jnp.zeros_like(dbias_ref)

        dk_ref[...] = jnp.zeros_like(dk_ref)
        dv_ref[...] = jnp.zeros_like(dv_ref)

        def blk(r0, first):
            nk = ATT_BLOCK if first else 2 * ATT_BLOCK
            k0 = r0 if first else r0 - ATT_BLOCK
            rows = pl.ds(r0, ATT_BLOCK)
            krows = pl.ds(k0, nk)
            qb = qkv_ref[rows, 0:GCOL]
            kb = qkv_ref[krows, GCOL:2 * GCOL]
            vb = qkv_ref[krows, 2 * GCOL:3 * GCOL]
            dob = doc_ref[rows, :].astype(BF16)
            lse = lse_ref[rows, :]
            dd = dd_ref[rows, :]
            dqs, dks, dvs = [], [], []
            for h in range(HEADS_PER_GROUP):
                sl = slice(h * HEAD_DIM, (h + 1) * HEAD_DIM)
                c0 = h * HEAD_DIM
                bh = bias_ref[h, :, ATT_BLOCK:2 * ATT_BLOCK] if first else bias_ref[h]
                s = lax.dot_general(qb[:, sl], kb[:, sl], _NT, preferred_element_type=F32) * ATT_SCALE + bh
                p = jnp.exp(s - lse[:, c0:c0 + 1])
                dp = lax.dot_general(dob[:, sl], vb[:, sl], _NT, preferred_element_type=F32)
                ds = p * (dp - dd[:, c0:c0 + 1])
                if first:
                    dbias_ref[h, :, ATT_BLOCK:2 * ATT_BLOCK] += ds
                else:
                    dbias_ref[h] += ds
                dsb = ds.astype(BF16)
                dqs.append(jnp.dot(dsb, kb[:, sl], preferred_element_type=F32) * ATT_SCALE)
                dks.append(lax.dot_general(dsb, qb[:, sl], _TN, preferred_element_type=F32) * ATT_SCALE)
                dvs.append(lax.dot_general(p.astype(BF16), dob[:, sl], _TN, preferred_element_type=F32))
            dqkv_ref[rows, 0:GCOL] = jnp.concatenate(dqs, axis=1).astype(BF16)
            dk_ref[krows, :] += jnp.concatenate(dks, axis=1)
            dv_ref[krows, :] += jnp.concatenate(dvs, axis=1)

        blk(0, True)
        if nb > 1:
            def loop(n, _):
                blk(pl.multiple_of(n * ATT_BLOCK, ATT_BLOCK), False)
                return 0

            lax.fori_loop(1, nb, loop, 0)
        dqkv_ref[:, GCOL:2 * GCOL] = dk_ref[...].astype(BF16)
        dqkv_ref[:, 2 * GCOL:3 * GCOL] = dv_ref[...].astype(BF16)

    spec = lambda cols: pl.BlockSpec((None, None, L, cols), lambda b, c: (b, c, 0, 0))
    bspec = pl.BlockSpec((HEADS_PER_GROUP, ATT_BLOCK, 2 * ATT_BLOCK), lambda b, c: (0, 0, 0))
    return pl.pallas_call(
        body, name=name,
        out_shape=(jax.ShapeDtypeStruct((B, d, L, 3 * GCOL), BF16),
                   jax.ShapeDtypeStruct((HEADS_PER_GROUP, ATT_BLOCK, 2 * ATT_BLOCK), F32)),
        grid=(B, d),
        in_specs=[spec(3 * GCOL), spec(GCOL), spec(GCOL), spec(GCOL), bspec],
        out_specs=(spec(3 * GCOL), bspec),
        scratch_shapes=[pltpu.VMEM((L, GCOL), F32), pltpu.VMEM((L, GCOL), F32)],
        compiler_params=_params(("arbitrary", "arbitrary")),
    )(qkv_g, doc_g, lse_g, dd_g, bias)


def _attn_combine(os_, lses, *, name):
    T, GC = os_[0].shape
    n = len(os_)

    def body(*refs):
        o_refs, l_refs, oc_ref, lt_ref = refs[:n], refs[n:2 * n], refs[2 * n], refs[2 * n + 1]
        ls = [r[...] for r in l_refs]
        m = functools.reduce(jnp.maximum, ls)
        ws = [jnp.exp(l - m) for l in ls]
        tot = functools.reduce(jnp.add, ws)
        acc = functools.reduce(jnp.add, [w * r[...] for w, r in zip(ws, o_refs)])
        oc_ref[...] = acc / tot
        lt_ref[...] = m + jnp.log(tot)

    shp = jax.ShapeDtypeStruct((T, GC), F32)
    return pl.pallas_call(body, name=name, out_shape=(shp, shp), grid=(T // ROW_TILE,),
                          in_specs=[_row_spec(GC)] * (2 * n), out_specs=(_row_spec(GC), _row_spec(GC)),
                          compiler_params=_params(("parallel",)))(*os_, *lses)


def _attn_rowdot(doc, oc, *, name):
    T, GC = oc.shape

    def body(doc_ref, oc_ref, dd_ref):
        prod = doc_ref[...].astype(F32) * oc_ref[...]
        parts = []
        for h in range(GC // HEAD_DIM):
            s = jnp.sum(prod[:, h * HEAD_DIM:(h + 1) * HEAD_DIM], axis=1, keepdims=True)
            parts.append(jnp.broadcast_to(s, (ROW_TILE, HEAD_DIM)))
        dd_ref[...] = jnp.concatenate(parts, axis=1)

    return pl.pallas_call(body, name=name, out_shape=jax.ShapeDtypeStruct((T, GC), F32), grid=(T // ROW_TILE,),
                          in_specs=[_row_spec(GC), _row_spec(GC)], out_specs=_row_spec(GC),
                          compiler_params=_params(("parallel",)))(doc, oc)


def _bucket_sum(dbias, buckets, *, name):
    depth, NH = dbias.shape[:2]

    def body(db_ref, bk_ref, out_ref):
        rows = lax.broadcasted_iota(jnp.int32, (REL_BUCKETS, LANE), 0)
        cols = lax.broadcasted_iota(jnp.int32, (REL_BUCKETS, LANE), 1)

        def per_bucket(b, acc):
            for h in range(NH):
                sel = bk_ref[h // HEADS_PER_GROUP] == b
                tot = functools.reduce(jnp.add, [db_ref[l, h] for l in range(depth)])
                s = jnp.sum(jnp.where(sel, tot, 0.0))
                acc = acc + jnp.where(jnp.logical_and(rows == b, cols == h), s, 0.0)
            return acc

        out_ref[...] = lax.fori_loop(0, REL_BUCKETS, per_bucket, jnp.zeros((REL_BUCKETS, LANE), F32))

    return pl.pallas_call(body, name=name, out_shape=jax.ShapeDtypeStruct((REL_BUCKETS, LANE), F32),
                          compiler_params=pltpu.CompilerParams(vmem_limit_bytes=VMEM_LIMIT))(dbias, buckets)


def _merge_fwd(zg, bg, ya, yb, yc, *, name):
    T, D3 = zg.shape
    D = D3 // 3

    def body(zg_ref, bg_ref, ya_ref, yb_ref, yc_ref, out_ref):
        acc = None
        for i, y_ref in enumerate((ya_ref, yb_ref, yc_ref)):
            g = _sigmoid(zg_ref[:, i * D:(i + 1) * D].astype(F32) + bg_ref[:, i * D:(i + 1) * D])
            t = g * y_ref[...].astype(F32)
            acc = t if acc is None else acc + t
        out_ref[...] = acc.astype(BF16)

    return pl.pallas_call(body, name=name, out_shape=jax.ShapeDtypeStruct((T, D), BF16), grid=(T // ROW_TILE,),
                          in_specs=[_row_spec(D3), _vec_spec(D3), _row_spec(D), _row_spec(D), _row_spec(D)],
                          out_specs=_row_spec(D), compiler_params=_params(("parallel",)))(zg, bg.reshape(1, D3), ya, yb, yc)


def _merge_bwd(dm, zg, bg, ya, yb, yc, *, name):
    T, D3 = zg.shape
    D = D3 // 3

    def body(dm_ref, zg_ref, bg_ref, ya_ref, yb_ref, yc_ref, dya_ref, dyb_ref, dyc_ref, dzg_ref, dbg_ref):
        first = pl.program_id(0) == 0
        dm_v = dm_ref[...].astype(F32)
        dbs = []
        for i, (y_ref, dy_ref) in enumerate(((ya_ref, dya_ref), (yb_ref, dyb_ref), (yc_ref, dyc_ref))):
            g = _sigmoid(zg_ref[:, i * D:(i + 1) * D].astype(F32) + bg_ref[:, i * D:(i + 1) * D])
            dy_ref[...] = (dm_v * g).astype(BF16)
            dz = dm_v * y_ref[...].astype(F32) * g * (1.0 - g)
            dzg_ref[:, i * D:(i + 1) * D] = dz.astype(BF16)
            dbs.append(jnp.sum(dz, axis=0, keepdims=True))
        _acc_out(dbg_ref, jnp.concatenate(dbs, axis=1), first)

    bshape = jax.ShapeDtypeStruct((T, D), BF16)
    return pl.pallas_call(
        body, name=name,
        out_shape=(bshape, bshape, bshape, jax.ShapeDtypeStruct((T, D3), BF16), jax.ShapeDtypeStruct((1, D3), F32)),
        grid=(T // ROW_TILE,),
        in_specs=[_row_spec(D), _row_spec(D3), _vec_spec(D3), _row_spec(D), _row_spec(D), _row_spec(D)],
        out_specs=(_row_spec(D), _row_spec(D), _row_spec(D), _row_spec(D3), _vec_spec(D3)),
        compiler_params=_params(("arbitrary",)))(dm, zg, bg.reshape(1, D3), ya, yb, yc)


XQ_TILE = 512


def _xattn_fwd(q, kv, *, B, S, name):
    D = q.shape[1]
    M = kv.shape[0] // B
    E = D // X_HEADS
    scale = E ** -0.5

    def body(q_ref, kv_ref, o_ref):
        outs = []
        for h in range(X_HEADS):
            s = lax.dot_general(q_ref[:, h * E:(h + 1) * E], kv_ref[:, h * E:(h + 1) * E], _NT,
                                preferred_element_type=F32) * scale
            e = jnp.exp(s - jnp.max(s, axis=1, keepdims=True))
            p = e / jnp.sum(e, axis=1, keepdims=True)
            outs.append(jnp.dot(p.astype(BF16), kv_ref[:, D + h * E:D + (h + 1) * E], preferred_element_type=F32))
        o_ref[...] = jnp.concatenate(outs, axis=1).astype(BF16)

    nq = S // XQ_TILE
    return pl.pallas_call(
        body, name=name, out_shape=jax.ShapeDtypeStruct((B * S, D), BF16), grid=(B, nq),
        in_specs=[pl.BlockSpec((XQ_TILE, D), lambda b, i: (b * nq + i, 0)), pl.BlockSpec((M, 2 * D), lambda b, i: (b, 0))],
        out_specs=pl.BlockSpec((XQ_TILE, D), lambda b, i: (b * nq + i, 0)),
        compiler_params=_params(("parallel", "parallel")))(q, kv)


def _xattn_bwd(q, kv, do, *, B, S, name):
    D = q.shape[1]
    M = kv.shape[0] // B
    E = D // X_HEADS
    scale = E ** -0.5

    def body(q_ref, kv_ref, do_ref, dq_ref, dkv_ref):
        first = pl.program_id(1) == 0
        dqs, dks, dvs = [], [], []
        for h in range(X_HEADS):
            qh = q_ref[:, h * E:(h + 1) * E]
            kh = kv_ref[:, h * E:(h + 1) * E]
            vh = kv_ref[:, D + h * E:D + (h + 1) * E]
            doh = do_ref[:, h * E:(h + 1) * E]
            s = lax.dot_general(qh, kh, _NT, preferred_element_type=F32) * scale
            e = jnp.exp(s - jnp.max(s, axis=1, keepdims=True))
            p = e / jnp.sum(e, axis=1, keepdims=True)
            dp = lax.dot_general(doh, vh, _NT, preferred_element_type=F32)
            ds = (p * (dp - jnp.sum(p * dp, axis=1, keepdims=True))).astype(BF16)
            dqs.append(jnp.dot(ds, kh, preferred_element_type=F32) * scale)
            dks.append(lax.dot_general(ds, qh, _TN, preferred_element_type=F32) * scale)
            dvs.append(lax.dot_general(p.astype(BF16), doh, _TN, preferred_element_type=F32))
        dq_ref[...] = jnp.concatenate(dqs, axis=1).astype(BF16)
        _acc_out(dkv_ref, jnp.concatenate(dks + dvs, axis=1), first)

    nq = S // XQ_TILE
    qspec = pl.BlockSpec((XQ_TILE, D), lambda b, i: (b * nq + i, 0))
    kvspec = pl.BlockSpec((M, 2 * D), lambda b, i: (b, 0))
    return pl.pallas_call(
        body, name=name,
        out_shape=(jax.ShapeDtypeStruct((B * S, D), BF16), jax.ShapeDtypeStruct((B * M, 2 * D), F32)),
        grid=(B, nq), in_specs=[qspec, kvspec, qspec], out_specs=(qspec, kvspec),
        compiler_params=_params(("arbitrary", "arbitrary")))(q, kv, do)


FFN_COLS = 256
FFN_PAD = 8
FFN_CH = 256
FFN_K = 3


def _ffn_gate(win, w_ref, cb_ref, n):
    g = jnp.broadcast_to(cb_ref[...], (n, win.shape[1]))
    for j in range(FFN_K):
        off = FFN_PAD - (FFN_K - 1) + j
        g = g + w_ref[j:j + 1, :] * win[off:off + n, :]
    return g


def _ffn_fwd(up, conv_w, conv_b, *, B, S, name):
    F = conv_w.shape[1]
    nc = F // FFN_COLS

    def body(gp_ref, val_ref, w_ref, cb_ref, act_ref, pad_ref):
        pad_ref[0:FFN_PAD, :] = jnp.zeros((FFN_PAD, FFN_COLS), F32)
        pad_ref[FFN_PAD:, :] = gp_ref[...].astype(F32)

        def chunk(i, _):
            base = pl.multiple_of(i * FFN_CH, FFN_CH)
            gate = _ffn_gate(pad_ref[pl.ds(base, FFN_CH + FFN_PAD), :], w_ref, cb_ref, FFN_CH)
            act_ref[pl.ds(base, FFN_CH), :] = (_gelu(gate) * val_ref[pl.ds(base, FFN_CH), :].astype(F32)).astype(BF16)
            return 0

        lax.fori_loop(0, S // FFN_CH, chunk, 0)

    return pl.pallas_call(
        body, name=name, out_shape=jax.ShapeDtypeStruct((B * S, F), BF16), grid=(B, nc),
        in_specs=[pl.BlockSpec((S, FFN_COLS), lambda b, j: (b, j)), pl.BlockSpec((S, FFN_COLS), lambda b, j: (b, nc + j)),
                  pl.BlockSpec((FFN_K, FFN_COLS), lambda b, j: (0, j)), pl.BlockSpec((1, FFN_COLS), lambda b, j: (0, j))],
        out_specs=pl.BlockSpec((S, FFN_COLS), lambda b, j: (b, j)),
        scratch_shapes=[pltpu.VMEM((S + FFN_PAD, FFN_COLS), F32)],
        compiler_params=_params(("parallel", "parallel")))(up, up, conv_w, conv_b.reshape(1, F))


def _ffn_bwd(dact, up, conv_w, conv_b, *, B, S, name):
    F = conv_w.shape[1]
    nc = F // FFN_COLS
    SUB = 8

    def body(dact_ref, gp_ref, val_ref, w_ref, cb_ref, dgp_ref, dval_ref, dw_ref, dcb_ref, pad_ref, pad2_ref, acc_ref):
        first = pl.program_id(1) == 0
        pad_ref[0:FFN_PAD, :] = jnp.zeros((FFN_PAD, FFN_COLS), F32)
        pad_ref[FFN_PAD:, :] = gp_ref[...].astype(F32)
        pad2_ref[S:, :] = jnp.zeros((FFN_PAD, FFN_COLS), F32)
        acc_ref[...] = jnp.zeros_like(acc_ref)

        def chunk1(i, _):
            base = pl.multiple_of(i * FFN_CH, FFN_CH)
            rows = pl.ds(base, FFN_CH)
            gate = _ffn_gate(pad_ref[pl.ds(base, FFN_CH + FFN_PAD), :], w_ref, cb_ref, FFN_CH)
            gl, dgl = _gelu_and_grad(gate)
            da = dact_ref[rows, :].astype(F32)
            dval_ref[rows, :] = (da * gl).astype(BF16)
            pad2_ref[rows, :] = da * val_ref[rows, :].astype(F32) * dgl
            return 0

        lax.fori_loop(0, S // FFN_CH, chunk1, 0)

        def chunk2(i, _):
            base = pl.multiple_of(i * FFN_CH, FFN_CH)
            rows = pl.ds(base, FFN_CH)
            win2 = pad2_ref[pl.ds(base, FFN_CH + FFN_PAD), :]
            win1 = pad_ref[pl.ds(base, FFN_CH + FFN_PAD), :]
            dg = win2[0:FFN_CH, :]
            dgp = jnp.zeros((FFN_CH, FFN_COLS), F32)
            for j in range(FFN_K):
                off2 = FFN_K - 1 - j
                dgp = dgp + w_ref[j:j + 1, :] * win2[off2:off2 + FFN_CH, :]
                off1 = FFN_PAD - (FFN_K - 1) + j
                prod = dg * win1[off1:off1 + FFN_CH, :]
                acc_ref[j * SUB:(j + 1) * SUB, :] += jnp.sum(prod.reshape(FFN_CH // SUB, SUB, FFN_COLS), axis=0)
            acc_ref[FFN_K * SUB:(FFN_K + 1) * SUB, :] += jnp.sum(dg.reshape(FFN_CH // SUB, SUB, FFN_COLS), axis=0)
            dgp_ref[rows, :] = dgp.astype(BF16)
            return 0

        lax.fori_loop(0, S // FFN_CH, chunk2, 0)
        sums = jnp.sum(acc_ref[...].reshape(FFN_K + 1, SUB, FFN_COLS), axis=1)
        _acc_out(dw_ref, sums[0:FFN_K, :], first)
        _acc_out(dcb_ref, sums[FFN_K:FFN_K + 1, :], first)

    blk = lambda off: pl.BlockSpec((S, FFN_COLS), lambda j, b: (b, off + j))
    return pl.pallas_call(
        body, name=name,
        out_shape=(jax.ShapeDtypeStruct((B * S, F), BF16), jax.ShapeDtypeStruct((B * S, F), BF16),
                   jax.ShapeDtypeStruct((FFN_K, F), F32), jax.ShapeDtypeStruct((1, F), F32)),
        grid=(nc, B),
        in_specs=[blk(0), blk(0), blk(nc), pl.BlockSpec((FFN_K, FFN_COLS), lambda j, b: (0, j)),
                  pl.BlockSpec((1, FFN_COLS), lambda j, b: (0, j))],
        out_specs=(blk(0), blk(0), pl.BlockSpec((FFN_K, FFN_COLS), lambda j, b: (0, j)),
                   pl.BlockSpec((1, FFN_COLS), lambda j, b: (0, j))),
        scratch_shapes=[pltpu.VMEM((S + FFN_PAD, FFN_COLS), F32), pltpu.VMEM((S + FFN_PAD, FFN_COLS), F32),
                        pltpu.VMEM(((FFN_K + 1) * SUB, FFN_COLS), F32)],
        compiler_params=_params(("arbitrary", "arbitrary")))(dact, up, up, conv_w, conv_b.reshape(1, F))


def _row_tile(rows, row_bytes, budget):
    tr = rows
    if rows * row_bytes > budget:
        for t in range(16, rows, 16):
            if rows % t == 0 and t * row_bytes <= budget:
                tr = t
    return tr


def _adamw(w, parts, m, v, *, name, own=None, own_slot=None):
    shape = w.shape
    L = len(parts)
    n = parts[0].shape[0]
    cols = shape[-1]
    rows = w.size // (cols * L)
    w3, m3, v3 = (t.reshape(L, rows, cols) for t in (w, m, v))
    tr = _row_tile(rows, cols * 4, 1 << 19)
    c1 = 1.0 - ADAM_B1 ** ADAM_STEP
    c2 = 1.0 - ADAM_B2 ** ADAM_STEP
    has_own = own is not None

    def body(*refs):
        refs = list(refs)
        slot_ref = refs.pop(0) if has_own else None
        w_ref = refs.pop(0)
        p_refs = [refs.pop(0) for _ in range(L)]
        o_refs = [refs.pop(0) for _ in range(L)] if has_own else None
        m_ref, v_ref, g_ref, d_ref, mo_ref, vo_ref = refs
        layer = pl.program_id(0)
        gv = None
        for j in range(L):
            gj = None
            for i in range(n):
                t = p_refs[j][i].astype(F32)
                if has_own:
                    t = jnp.where(slot_ref[0] == i, o_refs[j][i].astype(F32), t)
                gj = t if gj is None else gj + t
            gv = gj if gv is None else jnp.where(layer == j, gj, gv)
        g_ref[...] = gv
        mn = ADAM_B1 * m_ref[...] + (1.0 - ADAM_B1) * gv
        vn = ADAM_B2 * v_ref[...] + (1.0 - ADAM_B2) * (gv * gv)
        d_ref[...] = -ADAM_LR * ((mn / c1) / (jnp.sqrt(vn / c2) + ADAM_EPS) + ADAM_WD * w_ref[...])
        mo_ref[...] = mn
        vo_ref[...] = vn

    spec = pl.BlockSpec((None, tr, cols), lambda l, i, *_: (l, i, 0))

    def pspec(j):
        return pl.BlockSpec((n, tr, cols), lambda l, i, *_: (0, jnp.where(l == j, i, 0), 0))

    oshape = jax.ShapeDtypeStruct((L, rows, cols), F32)
    in_specs = [spec] + [pspec(j) for j in range(L)] * (2 if has_own else 1) + [spec, spec]
    args = [w3] + [p.reshape(n, rows, cols) for p in parts]
    if has_own:
        args += [o.reshape(n, rows, cols) for o in own]
    args += [m3, v3]
    grid = (L, rows // tr)
    if has_own:
        grid_spec = pltpu.PrefetchScalarGridSpec(num_scalar_prefetch=1, grid=grid, in_specs=in_specs,
                                                 out_specs=(spec,) * 4)
        outs = pl.pallas_call(body, name=name, out_shape=(oshape,) * 4, grid_spec=grid_spec,
                              compiler_params=_params(("parallel", "parallel")))(own_slot, *args)
    else:
        outs = pl.pallas_call(body, name=name, out_shape=(oshape,) * 4, grid=grid, in_specs=in_specs,
                              out_specs=(spec,) * 4, compiler_params=_params(("parallel", "parallel")))(*args)
    return tuple(t.reshape(shape) for t in outs)


def _chip_sum(g, got, core, *, name):
    shard = got.shape[1:]
    cols = shard[-1]
    rows = math.prod(shard) // cols
    tr = _row_tile(rows, cols * 4, 1 << 20)

    def body(c_ref, g_ref, r_ref, o_ref):
        o_ref[...] = (g_ref[...].astype(F32) + r_ref[...].astype(F32)).astype(o_ref.dtype)

    blk = (None, tr, cols)
    grid_spec = pltpu.PrefetchScalarGridSpec(
        num_scalar_prefetch=1, grid=(N_CHIP, rows // tr),
        in_specs=[pl.BlockSpec(blk, lambda s, i, c: (2 * s + c[0], i, 0)), pl.BlockSpec(blk, lambda s, i, c: (s, i, 0))],
        out_specs=pl.BlockSpec(blk, lambda s, i, c: (s, i, 0)))
    out = pl.pallas_call(body, name=name, out_shape=jax.ShapeDtypeStruct((N_CHIP, rows, cols), g.dtype),
                         grid_spec=grid_spec, compiler_params=_params(("parallel", "parallel")),
                         )(core, g.reshape(N_DEV, rows, cols), got.reshape(N_CHIP, rows, cols))
    return out.reshape((N_CHIP,) + shard)


_HBM = pl.BlockSpec(memory_space=pltpu.HBM)
_MESH = pl.DeviceIdType.MESH


def _my_pos():
    return lax.axis_index("x"), lax.axis_index("y"), lax.axis_index("c")


def _flip(pos, k):
    x, y, c = pos
    fx, fy, fc = (k >> 2) & 1, (k >> 1) & 1, k & 1
    return (x ^ fx if fx else x, y ^ fy if fy else y, c ^ fc if fc else c)


def _index_of(pos):
    return 4 * pos[0] + 2 * pos[1] + pos[2]


def _all_gather(xs, *, name):
    n = len(xs)

    def body(*refs):
        x_refs, out_refs = refs[:n], refs[n:2 * n]
        send_sems, recv_sems = refs[2 * n:]
        me = _my_pos()
        sibling = _flip(me, 1)
        chips = [2, 4, 6]

        def copy(i, k, block_pos, to, from_x=False):
            blk = out_refs[i].at[_index_of(block_pos)]
            return pltpu.make_async_remote_copy(src_ref=x_refs[i] if from_x else blk, dst_ref=blk,
                                                send_sem=send_sems.at[k, i], recv_sem=recv_sems.at[k, i],
                                                device_id=to, device_id_type=_MESH)

        first = [copy(i, 1 + j, me, _flip(me, f), from_x=True) for j, f in enumerate(chips) for i in range(n)]
        first += [copy(i, 0, me, sibling, from_x=True) for i in range(n)]
        for cp in first:
            cp.start()
        passed = []
        for j, f in enumerate(chips):
            for i in range(n):
                copy(i, 1 + j, _flip(me, f), me).wait_recv()
                cp = copy(i, 4 + j, _flip(me, f), sibling)
                cp.start()
                passed.append(cp)
        for i in range(n):
            copy(i, 0, sibling, me).wait_recv()
        for j, f in enumerate(chips):
            for i in range(n):
                copy(i, 4 + j, _flip(sibling, f), me).wait_recv()
        for cp in first + passed:
            cp.wait_send()

    return pl.pallas_call(
        body, name=name, out_shape=tuple(jax.ShapeDtypeStruct((N_DEV,) + x.shape, x.dtype) for x in xs),
        in_specs=[_HBM] * n, out_specs=(_HBM,) * n,
        scratch_shapes=[pltpu.SemaphoreType.DMA((7, n)), pltpu.SemaphoreType.DMA((7, n))],
    )(*xs)


N_CHIP = 4


def _scatter_d2d(gs):
    n = len(gs)

    def make(g_refs, recv_refs, send_sems, recv_sems):
        me = _my_pos()
        sibling = _flip(me, 1)
        c = me[2]
        sends = []
        for i in range(n):
            for s in range(N_CHIP):
                sends.append(pltpu.make_async_remote_copy(
                    src_ref=g_refs[i].at[2 * s + 1 - c], dst_ref=recv_refs[i].at[s],
                    send_sem=send_sems.at[N_CHIP * i + s], recv_sem=recv_sems.at[N_CHIP * i + s],
                    device_id=sibling, device_id_type=_MESH))
        return sends, sends

    shapes = [jax.ShapeDtypeStruct((N_CHIP,) + g.shape[1:], g.dtype) for g in gs]
    return _Exchange(list(gs), shapes, N_CHIP * n, make)


def _scatter_ici(ss):
    n = len(ss)

    def make(s_refs, r_refs, send_sems, recv_sems):
        me = _my_pos()
        my_chip = 2 * me[0] + me[1]
        sends, recvs = [], []
        for k in (1, 2, 3):
            peer = _flip(me, 2 * k)
            peer_chip = 2 * peer[0] + peer[1]
            for i in range(n):
                j = 3 * i + k - 1
                sends.append(pltpu.make_async_remote_copy(
                    src_ref=s_refs[i].at[peer_chip], dst_ref=r_refs[i].at[my_chip], send_sem=send_sems.at[j],
                    recv_sem=recv_sems.at[j], device_id=peer, device_id_type=_MESH))
                recvs.append(pltpu.make_async_remote_copy(
                    src_ref=s_refs[i].at[my_chip], dst_ref=r_refs[i].at[peer_chip], send_sem=send_sems.at[j],
                    recv_sem=recv_sems.at[j], device_id=peer, device_id_type=_MESH))
        return sends, recvs

    return _Exchange(list(ss), [jax.ShapeDtypeStruct(s.shape, s.dtype) for s in ss], 3 * n, make)


def _gather_ici(shards, layers):
    n = len(shards)
    FLIPS = (2, 4, 6, 1)

    def make(x_refs, out_refs, send_sems, recv_sems):
        me = _my_pos()
        sends, recvs = [], []
        for k, f in enumerate(FLIPS):
            peer = _flip(me, f)
            for i in range(n):
                src = x_refs[i] if layers[i] is None else x_refs[i].at[layers[i]]
                j = len(FLIPS) * i + k
                sends.append(pltpu.make_async_remote_copy(
                    src_ref=src, dst_ref=out_refs[i].at[_index_of(me)], send_sem=send_sems.at[j],
                    recv_sem=recv_sems.at[j], device_id=peer, device_id_type=_MESH))
                recvs.append(pltpu.make_async_remote_copy(
                    src_ref=src, dst_ref=out_refs[i].at[_index_of(peer)], send_sem=send_sems.at[j],
                    recv_sem=recv_sems.at[j], device_id=peer, device_id_type=_MESH))
        return sends, recvs

    shapes = [jax.ShapeDtypeStruct((N_DEV,) + (x.shape if l is None else x.shape[1:]), x.dtype)
              for x, l in zip(shards, layers)]
    return _Exchange(list(shards), shapes, len(FLIPS) * n, make)


def _gather_d2d(blocks):
    n = len(blocks)
    FLIPS = (2, 4, 6)

    def make(in_refs, out_refs, send_sems, recv_sems):
        me = _my_pos()
        sibling = _flip(me, 1)
        sends, recvs = [], []
        for k, f in enumerate(FLIPS):
            for i in range(n):
                j = len(FLIPS) * i + k
                mine = out_refs[i].at[_index_of(_flip(me, f))]
                sends.append(pltpu.make_async_remote_copy(
                    src_ref=mine, dst_ref=mine, send_sem=send_sems.at[j], recv_sem=recv_sems.at[j],
                    device_id=sibling, device_id_type=_MESH))
                recvs.append(pltpu.make_async_remote_copy(
                    src_ref=mine, dst_ref=out_refs[i].at[_index_of(_flip(sibling, f))], send_sem=send_sems.at[j],
                    recv_sem=recv_sems.at[j], device_id=sibling, device_id_type=_MESH))
        return sends, recvs

    shapes = [jax.ShapeDtypeStruct(b.shape, b.dtype) for b in blocks]
    return _Exchange(list(blocks), shapes, len(FLIPS) * n, make, aliases={i: i for i in range(n)})


class _Exchange:
    def __init__(self, ins, out_shapes, n_copies, make, aliases=None):
        self.ins, self.out_shapes, self.n_copies, self.make, self.aliases = ins, out_shapes, n_copies, make, aliases or {}


def _exchange_call(ex, *, name):
    n_in, n_out = len(ex.ins), len(ex.out_shapes)

    def body(*refs):
        sends, recvs = ex.make(refs[:n_in], refs[n_in:n_in + n_out], refs[n_in + n_out], refs[n_in + n_out + 1])
        for cp in sends:
            cp.start()
        for cp in recvs:
            cp.wait_recv()
        for cp in sends:
            cp.wait_send()

    return pl.pallas_call(
        body, name=name, out_shape=tuple(ex.out_shapes), in_specs=[_HBM] * n_in, out_specs=(_HBM,) * n_out,
        scratch_shapes=[pltpu.SemaphoreType.DMA((ex.n_copies,)), pltpu.SemaphoreType.DMA((ex.n_copies,))],
        input_output_aliases=ex.aliases,
    )(*ex.ins)


class _Hosts:
    def __init__(self):
        self.plan = {}

    def add(self, name, build, done):
        assert name not in self.plan
        self.plan[name] = (build, done)

    def get(self, name):
        return self.plan[name][0]() if name in self.plan else None

    def put(self, name, outs):
        self.plan.pop(name)[1](outs)


def _t5_bucket(dist):
    n = jnp.maximum(dist, 0)
    max_exact = REL_BUCKETS // 2
    nf = jnp.maximum(n, 1).astype(F32)
    large = max_exact + (jnp.log(nf / max_exact) / math.log(REL_MAX_DIST / max_exact)
                         * (REL_BUCKETS - max_exact)).astype(jnp.int32)
    large = jnp.minimum(large, REL_BUCKETS - 1)
    return jnp.where(n < max_exact, n, large)


def _bias_tables(rel_bias):
    qi = jnp.arange(ATT_BLOCK)[:, None]
    ki = jnp.arange(2 * ATT_BLOCK)[None, :]
    rel = qi + ATT_BLOCK - ki
    out = []
    for gi, (window, dil) in enumerate(DIL_GROUPS):
        span = window // dil
        bucket = _t5_bucket(rel * dil)
        valid = (rel >= 0) & (rel <= span)
        rb = rel_bias[:, gi * HEADS_PER_GROUP:(gi + 1) * HEADS_PER_GROUP]
        tab = functools.reduce(jnp.add, [jnp.where((bucket == b)[:, :, None], rb[b], 0.0)
                                         for b in range(REL_BUCKETS)])
        tab = jnp.where(valid[:, :, None], tab, MASK_VALUE).transpose(2, 0, 1)
        out.append((tab.astype(F32), bucket.astype(jnp.int32)))
    return out


def _regroup(t, B, S, d):
    C = t.shape[-1]
    return t.reshape(B, S // d, d, C).swapaxes(1, 2)


def _ungroup(t):
    B, d, L, C = t.shape
    return t.swapaxes(1, 2).reshape(B * L * d, C)


def _group_qkv(qkv, gi):
    n = len(DIL_GROUPS) * GROUP_COLS
    return jnp.concatenate([qkv[:, j * n + gi * GROUP_COLS: j * n + (gi + 1) * GROUP_COLS] for j in range(3)], axis=1)


def _layer_fwd(l, x0, h1, mem2, W, P, tabs, next_pre_g, *, B, S, hosts=None):
    tag = f"l{l}_"
    mm = functools.partial(_mm, hosts=hosts)
    sv = {"x0": x0, "h1": h1}
    za = mm(h1, W["in_a"], out_dtype=ACT, name=tag + "mm_in_a")
    zb = mm(h1, W["in_b"], out_dtype=ACT, name=tag + "mm_in_b")
    qkv = mm(h1, W["in_c"], out_dtype=BF16, name=tag + "mm_in_c")
    zg = mm(h1, W["in_g"], out_dtype=ACT, name=tag + "mm_in_g")
    a4, a2 = _bra_fwd(za, P["conv_a_w"], P["conv_a_b"], P["ln_a_g"], P["ln_a_b"], B=B, S=S, name=tag + "bra_fwd")
    ya = mm(a4, W["a_out"], out_dtype=ACT, name=tag + "mm_a_out")
    b_s_t = P["b_s"].T
    p = _brb_fwd(zb, P["ln_b_g"], P["ln_b_b"], P["w_s"], b_s_t, name=tag + "brb_fwd")
    yb = mm(p, W["b_out"], out_dtype=ACT, name=tag + "mm_b_out")
    os_, lses, qkv_gs = [], [], []
    for gi, (_, dil) in enumerate(DIL_GROUPS):
        qkv_g = _regroup(_group_qkv(qkv, gi), B, S, dil)
        o_g, lse_g = _attn_fwd(qkv_g, tabs[gi][0], name=tag + f"attn_fwd{gi}")
        qkv_gs.append(qkv_g)
        os_.append(_ungroup(o_g))
        lses.append(_ungroup(lse_g))
    oc, lse_tot = _attn_combine(os_, lses, name=tag + "attn_combine")
    yc = mm(oc, W["c_out"], out_dtype=ACT, name=tag + "mm_c_out")
    merged = _merge_fwd(zg, P["b_gate"], ya, yb, yc, name=tag + "merge_fwd")
    y1 = mm(merged, W["mix_out"], out_dtype=ACT, name=tag + "mm_mix")
    x1, h2 = _norm_fwd(x0, y1, P["mix_post_g"], P["x_pre_g"], name=tag + "norm1")
    q = mm(h2, W["xq"], out_dtype=BF16, name=tag + "mm_xq")
    (memn,) = _norm_fwd(mem2, None, None, P["mem_g"], name=tag + "norm_mem")
    kv = mm(memn, W["xkv"], out_dtype=BF16, name=tag + "mm_xkv")
    ox = _xattn_fwd(q, kv, B=B, S=S, name=tag + "xattn_fwd")
    y2 = mm(ox, W["xo"], out_dtype=ACT, name=tag + "mm_xo")
    x2, h3 = _norm_fwd(x1, y2, P["x_post_g"], P["ffn_pre_g"], name=tag + "norm2")
    up = mm(h3, W["up"], out_dtype=ACT, name=tag + "mm_up")
    act = _ffn_fwd(up, P["conv_f_w"], P["conv_f_b"], B=B, S=S, name=tag + "ffn_fwd")
    y3 = mm(act, W["down"], out_dtype=ACT, name=tag + "mm_down")
    outs = _norm_fwd(x2, y3, P["ffn_post_g"], next_pre_g, name=tag + "norm3")
    x3 = outs[0]
    h_next = outs[1] if next_pre_g is not None else None
    sv.update(za=za, zb=zb, zg=zg, a4=a4, a2=a2, ya=ya, p=p, yb=yb, qkv_gs=qkv_gs, oc=oc, lse_tot=lse_tot, yc=yc,
              merged=merged, y1=y1, x1=x1, h2=h2, q=q, memn=memn, kv=kv, ox=ox, y2=y2, x2=x2, h3=h3, up=up, act=act,
              y3=y3, x3=x3, b_s_t=b_s_t)
    return x3, h_next, sv


def _layer_bwd(l, d, dh_next, next_pre_g, sv, mem2, W, WT, P, tabs, G, *, B, S, hosts=None):
    tag = f"l{l}_"
    mm = functools.partial(_mm, hosts=hosts)
    mm_tn = functools.partial(_mm_tn, hosts=hosts)
    if dh_next is not None:
        d, dy3, G["ffn_post_g"], dg_next = _norm_bwd(d, dh_next, sv["x3"], next_pre_g, sv["y3"], P["ffn_post_g"],
                                                     name=tag + "norm3_bwd")
    else:
        dy3, G["ffn_post_g"] = _norm_bwd(d, None, None, None, sv["y3"], P["ffn_post_g"], name=tag + "norm3_bwd")
        dg_next = None
    dact = mm(dy3, WT["down"], out_dtype=ACT, name=tag + "mm_down_dx")
    G["w_down"] = mm_tn(sv["act"], dy3, name=tag + "mm_down_dw")
    dgp, dval, G["conv_f_w"], G["conv_f_b"] = _ffn_bwd(dact, sv["up"], P["conv_f_w"], P["conv_f_b"], B=B, S=S,
                                                      name=tag + "ffn_bwd")
    dup = jnp.concatenate([dgp, dval], axis=1)
    dh3 = mm(dup, WT["up"], out_dtype=ACT, name=tag + "mm_up_dx")
    G["w_up"] = mm_tn(sv["h3"], dup, name=tag + "mm_up_dw")
    d, dy2, G["x_post_g"], G["ffn_pre_g"] = _norm_bwd(d, dh3, sv["x2"], P["ffn_pre_g"], sv["y2"], P["x_post_g"],
                                                      name=tag + "norm2_bwd")
    dox = mm(dy2, WT["xo"], out_dtype=BF16, name=tag + "mm_xo_dx")
    G["w_xo"] = mm_tn(sv["ox"], dy2, name=tag + "mm_xo_dw")
    dq, dkv = _xattn_bwd(sv["q"], sv["kv"], dox, B=B, S=S, name=tag + "xattn_bwd")
    dh2 = mm(dq, WT["xq"], out_dtype=ACT, name=tag + "mm_xq_dx")
    G["w_xq"] = mm_tn(sv["h2"], dq, name=tag + "mm_xq_dw")
    G["w_xkv"] = mm_tn(sv["memn"], dkv, name=tag + "mm_xkv_dw")
    dmemn = mm(dkv, WT["xkv"], out_dtype=F32, name=tag + "mm_xkv_dx")
    (G["mem_g"],) = _norm_bwd(None, dmemn, mem2, P["mem_g"], None, None, name=tag + "norm_mem_bwd")
    d, dy1, G["mix_post_g"], G["x_pre_g"] = _norm_bwd(d, dh2, sv["x1"], P["x_pre_g"], sv["y1"], P["mix_post_g"],
                                                      name=tag + "norm1_bwd")
    dm = mm(dy1, WT["mix_out"], out_dtype=ACT, name=tag + "mm_mix_dx")
    G["w_mix_out"] = mm_tn(sv["merged"], dy1, name=tag + "mm_mix_dw")
    dya, dyb, dyc, dzg, dbg = _merge_bwd(dm, sv["zg"], P["b_gate"], sv["ya"], sv["yb"], sv["yc"], name=tag + "merge_bwd")
    G["b_gate"] = dbg.reshape(P["b_gate"].shape)
    da4 = mm(dya, WT["a_out"], out_dtype=ACT, name=tag + "mm_a_out_dx")
    G["w_a_out"] = mm_tn(sv["a4"], dya, name=tag + "mm_a_out_dw")
    da2, G["ln_a_g"], G["ln_a_b"], G["conv_a_b"] = _bra_bwd_ln(da4, sv["a2"], P["ln_a_g"], P["ln_a_b"], name=tag + "bra_bwd_ln")
    dza, G["conv_a_w"] = _bra_bwd_conv(da2, sv["za"], P["conv_a_w"], B=B, S=S, name=tag + "bra_bwd_conv")
    dp = mm(dyb, WT["b_out"], out_dtype=ACT, name=tag + "mm_b_out_dx")
    G["w_b_out"] = mm_tn(sv["p"], dyb, name=tag + "mm_b_out_dw")
    dzb, G["w_s"], dbs_t, G["ln_b_g"], G["ln_b_b"] = _brb_bwd(dp, sv["zb"], P["ln_b_g"], P["ln_b_b"], P["w_s"],
                                                             sv["b_s_t"], name=tag + "brb_bwd")
    G["b_s"] = dbs_t.T
    doc = mm(dyc, WT["c_out"], out_dtype=ACT, name=tag + "mm_c_out_dx")
    G["w_c_out"] = mm_tn(sv["oc"], dyc, name=tag + "mm_c_out_dw")
    dd = _attn_rowdot(doc, sv["oc"], name=tag + "attn_rowdot")
    dq_parts, dk_parts, dv_parts, dbiases = [], [], [], []
    for gi, (_, dil) in enumerate(DIL_GROUPS):
        dqkv_g, dbias = _attn_bwd(sv["qkv_gs"][gi], _regroup(doc, B, S, dil), _regroup(sv["lse_tot"], B, S, dil),
                                  _regroup(dd, B, S, dil), tabs[gi][0], name=tag + f"attn_bwd{gi}")
        t = _ungroup(dqkv_g)
        dq_parts.append(t[:, 0:GROUP_COLS])
        dk_parts.append(t[:, GROUP_COLS:2 * GROUP_COLS])
        dv_parts.append(t[:, 2 * GROUP_COLS:3 * GROUP_COLS])
        dbiases.append(dbias)
    dqkv = jnp.concatenate(dq_parts + dk_parts + dv_parts, axis=1)
    G["dbias"] = jnp.concatenate(dbiases, axis=0)
    dh1 = mm(dza, WT["in_a"], out_dtype=F32, name=tag + "mm_in_a_dx")
    dh1 = mm(dzb, WT["in_b"], out_dtype=F32, name=tag + "mm_in_b_dx", add=dh1)
    dh1 = mm(dqkv, WT["in_c"], out_dtype=F32, name=tag + "mm_in_c_dx", add=dh1)
    dh1 = mm(dzg, WT["in_g"], out_dtype=ACT, name=tag + "mm_in_g_dx", add=dh1)
    h1 = sv["h1"]
    G["w_in"] = jnp.concatenate([mm_tn(h1, dza, name=tag + "mm_in_a_dw"), mm_tn(h1, dzb, name=tag + "mm_in_b_dw"),
                                 mm_tn(h1, dqkv, name=tag + "mm_in_c_dw"), mm_tn(h1, dzg, name=tag + "mm_in_g_dw")],
                                axis=1)
    return d, dh1, dg_next


_COL_SHARDED = ("w_in", "b_gate", "conv_a_w", "w_a_out", "w_b_out", "w_c_out", "w_xkv", "w_up", "conv_f_w")
_ROW_SHARDED = ("w_mix_out", "w_xq", "w_xo", "w_down")
_SHARDED_BIG = ("w_in", "w_a_out", "w_b_out", "w_c_out", "w_mix_out", "w_xq", "w_xkv", "w_xo", "w_up", "w_down")
_SHARDED_SMALL = ("b_gate", "conv_a_w", "conv_f_w")
_REPLICATED = ("mix_pre_g", "mix_post_g", "conv_a_b", "ln_a_g", "ln_a_b", "ln_b_g", "ln_b_b", "w_s", "b_s",
               "x_pre_g", "x_post_g", "mem_g", "ffn_pre_g", "ffn_post_g", "conv_f_b")
_WEIGHTS = ('rel_bias', 'mix_pre_g', 'mix_post_g', 'w_in', 'b_gate', 'conv_a_w', 'conv_a_b', 'ln_a_g', 'ln_a_b',
            'w_a_out', 'ln_b_g', 'ln_b_b', 'w_s', 'b_s', 'w_b_out', 'w_c_out', 'w_mix_out', 'x_pre_g', 'x_post_g',
            'mem_g', 'w_xq', 'w_xkv', 'w_xo', 'ffn_pre_g', 'ffn_post_g', 'w_up', 'conv_f_w', 'conv_f_b', 'w_down')
_PACK_COLS = 1024


def _shard_axis(name):
    return 1 if name in _ROW_SHARDED else 2


def _pack(parts, dtype, row_mult):
    flat = jnp.concatenate([p.astype(dtype).reshape(-1) for p in parts])
    n = flat.shape[0]
    unit = _PACK_COLS * row_mult
    padded = -(-n // unit) * unit
    flat = jnp.pad(flat, (0, padded - n))
    return flat.reshape(padded // _PACK_COLS, _PACK_COLS)


def _join8(blocks, ax):
    t = jnp.moveaxis(blocks, 0, ax)
    shp = t.shape
    return t.reshape(shp[:ax] + (shp[ax] * shp[ax + 1],) + shp[ax + 2:])


def _split8(full, ax):
    shp = full.shape
    t = full.reshape(shp[:ax] + (N_DEV, shp[ax] // N_DEV) + shp[ax + 1:])
    return jnp.moveaxis(t, ax, 0)


_W_KEYS = {"w_a_out": "a_out", "w_b_out": "b_out", "w_c_out": "c_out", "w_mix_out": "mix_out", "w_xq": "xq",
           "w_xkv": "xkv", "w_xo": "xo", "w_up": "up", "w_down": "down"}
_IN_SPLITS = (("in_a", 0, 1024), ("in_b", 1024, 2048), ("in_c", 2048, 4352), ("in_g", 4352, 7424))


def _layer_weights(name, full_l):
    if name == "w_in":
        assert full_l.shape[1] == _IN_SPLITS[-1][2]
        W = {k: full_l[:, a:b] for k, a, b in _IN_SPLITS}
    else:
        W = {_W_KEYS[name]: full_l}
    return W, {k: v.T for k, v in W.items()}


def _sum_rows(x, *, name):
    n = x.shape[0]

    def body(x_ref, o_ref):
        acc = x_ref[0:1, :]
        for i in range(1, n):
            acc = acc + x_ref[i:i + 1, :]
        o_ref[...] = acc

    return pl.pallas_call(body, name=name, out_shape=jax.ShapeDtypeStruct((1, x.shape[1]), x.dtype))(x)


def kernel(x, mem, rel_bias, mix_pre_g, mix_post_g, w_in, b_gate, conv_a_w, conv_a_b, ln_a_g, ln_a_b, w_a_out, ln_b_g, ln_b_b, w_s, b_s, w_b_out, w_c_out, w_mix_out, x_pre_g, x_post_g, mem_g, w_xq, w_xkv, w_xo, ffn_pre_g, ffn_post_g, w_up, conv_f_w, conv_f_b, w_down, loss_target, m_rel_bias, m_mix_pre_g, m_mix_post_g, m_w_in, m_b_gate, m_conv_a_w, m_conv_a_b, m_ln_a_g, m_ln_a_b, m_w_a_out, m_ln_b_g, m_ln_b_b, m_w_s, m_b_s, m_w_b_out, m_w_c_out, m_w_mix_out, m_x_pre_g, m_x_post_g, m_mem_g, m_w_xq, m_w_xkv, m_w_xo, m_ffn_pre_g, m_ffn_post_g, m_w_up, m_conv_f_w, m_conv_f_b, m_w_down, v_rel_bias, v_mix_pre_g, v_mix_post_g, v_w_in, v_b_gate, v_conv_a_w, v_conv_a_b, v_ln_a_g, v_ln_a_b, v_w_a_out, v_ln_b_g, v_ln_b_b, v_w_s, v_b_s, v_w_b_out, v_w_c_out, v_w_mix_out, v_x_pre_g, v_x_post_g, v_mem_g, v_w_xq, v_w_xkv, v_w_xo, v_ffn_pre_g, v_ffn_post_g, v_w_up, v_conv_f_w, v_conv_f_b, v_w_down):
    args = locals()
    w_loc = {n: args[n] for n in _WEIGHTS}
    m_loc = {n: args["m_" + n] for n in _WEIGHTS}
    v_loc = {n: args["v_" + n] for n in _WEIGHTS}

    depth = w_in.shape[0]
    assert depth == 2
    B, S, D = x.shape
    mine = _index_of(_my_pos())
    core = lax.axis_index("c").astype(jnp.int32).reshape(1)
    my_chip = (2 * lax.axis_index("x") + lax.axis_index("y")).astype(jnp.int32).reshape(1)
    shard = {n: w_loc[n].astype(BF16) for n in _SHARDED_BIG}
    shard.update({n: w_loc[n] for n in _SHARDED_SMALL})
    Ws, WTs = [{} for _ in range(depth)], [{} for _ in range(depth)]
    small_full = {}
    hosts = _Hosts()

    def install(items, blocks):
        for (n, l), blk in zip(items, blocks):
            own = shard[n] if l is None else shard[n][l]
            sel = lax.broadcasted_iota(jnp.int32, (N_DEV,) + (1,) * own.ndim, 0) == mine
            blk = jnp.where(sel, own[None], blk)
            if l is None:
                small_full[n] = _join8(blk, _shard_axis(n))
            else:
                W, WT = _layer_weights(n, _join8(blk, _shard_axis(n) - 1))
                Ws[l].update(W)
                WTs[l].update(WT)

    def carry_gather(name, items, group, then=None):
        def done(outs):
            group.extend(zip(items, outs))
            if then is not None:
                then()
        hosts.add(name, lambda: _gather_ici([shard[n] for n, _ in items], [l for _, l in items]), done)

    def finish_gather(group, tag):
        install([it for it, _ in group], _exchange_call(_gather_d2d([b for _, b in group]), name="gather_d2d_" + tag))

    first = [(n, 0) for n in ("w_in", "w_a_out", "w_b_out", "w_c_out", "w_mix_out")] + [(n, None) for n in _SHARDED_SMALL]
    install(first, _all_gather([shard[n] if l is None else shard[n][l] for n, l in first], name="gather_first"))
    g0, g1 = [], []
    carry_gather("l0_mm_in_a", [("w_xq", 0)], g0)
    carry_gather("l0_mm_in_b", [("w_xo", 0)], g0)
    carry_gather("l0_mm_in_c", [("w_xkv", 0), ("w_down", 0)], g0)
    carry_gather("l0_mm_in_g", [("w_up", 0)], g0, then=lambda: finish_gather(g0, "l0"))
    carry_gather("l0_mm_mix", [("w_xq", 1), ("w_mix_out", 1)], g1)
    carry_gather("l0_mm_xq", [("w_xo", 1), ("w_a_out", 1), ("w_b_out", 1), ("w_c_out", 1)], g1)
    carry_gather("l0_mm_xo", [("w_xkv", 1)], g1)
    carry_gather("l0_mm_up", [("w_in", 1), ("w_down", 1)], g1)
    carry_gather("l0_mm_down", [("w_up", 1)], g1, then=lambda: finish_gather(g1, "l1"))

    Gs = [{} for _ in range(depth)]
    sendbuf, chip_sums, parts = {}, {}, {}

    def to_send(item):
        n, l = item
        if l is None:
            g = jnp.stack([Gs[k][n].reshape(small_full[n].shape[1:]) for k in range(depth)], axis=0)
            sendbuf[item] = _split8(g, _shard_axis(n))
        else:
            sendbuf[item] = _split8(Gs[l][n], _shard_axis(n) - 1).astype(BF16)
        return sendbuf[item]

    def swapped(items, got):
        for item, r in zip(items, got):
            chip_sums[item] = _chip_sum(sendbuf[item], r, core, name=f"chip_sum_{item[0]}_{item[1]}")

    def carry_swap(name, items):
        hosts.add(name, lambda: _scatter_d2d([to_send(it) for it in items]), lambda outs: swapped(items, outs))

    def carry_scatter(name, items):
        hosts.add(name, lambda: _scatter_ici([chip_sums[it] for it in items]), lambda outs: parts.update(zip(items, outs)))

    carry_swap("l0_mm_down_dx", [(n, 1) for n in _SHARDED_BIG])
    carry_scatter("l0_mm_down_dw", [(n, 1) for n in ("w_xq", "w_xkv", "w_xo", "w_mix_out", "w_a_out", "w_b_out", "w_c_out")])
    carry_scatter("l0_mm_up_dx", [("w_in", 1)])
    carry_scatter("l0_mm_up_dw", [("w_up", 1), ("w_down", 1)])
    carry_swap("l0_mm_xo_dx", [("w_down", 0), ("w_up", 0)])
    carry_scatter("l0_mm_xq_dx", [("w_down", 0)])
    carry_scatter("l0_mm_in_g_dx", [("w_up", 0)])

    rep_w = {n: w_loc[n] for n in ("rel_bias",) + _REPLICATED}
    Ps = lambda: [dict({n: rep_w[n][l] for n in _REPLICATED}, **{n: small_full[n][l] for n in _SHARDED_SMALL})
                  for l in range(depth)]
    loss_vec, grad_x, g_rel = _run_step(x, mem, loss_target, Ws, WTs, Ps(), rep_w["rel_bias"], Gs, hosts)
    assert not hosts.plan, list(hosts.plan)

    last = [(n, 0) for n in ("w_in", "w_a_out", "w_b_out", "w_c_out", "w_mix_out", "w_xq", "w_xkv", "w_xo")]
    last += [(n, None) for n in _SHARDED_SMALL]
    swapped(last, _exchange_call(_scatter_d2d([to_send(it) for it in last]), name="swap_grads_last"))
    parts.update(zip(last, _exchange_call(_scatter_ici([chip_sums[it] for it in last]), name="scatter_grads_last")))

    rep = ("rel_bias",) + _REPLICATED
    rep_g = {n: jnp.stack([Gs[l][n].reshape(w_loc[n].shape[1:]) for l in range(depth)], axis=0) for n in _REPLICATED}
    rep_g["rel_bias"] = g_rel
    packed = _pack([loss_vec] + [rep_g[n] for n in rep], F32, 8)
    (allp,) = _all_gather([packed], name="gather_rep_grads")
    sel = lax.broadcasted_iota(jnp.int32, (N_DEV, 1, 1), 0) == mine
    allp = jnp.where(sel, packed[None], allp).reshape(N_DEV, -1)
    loss = _sum_rows(allp[:, :LANE], name="loss_sum")[0, 0]
    off = LANE
    rep_parts = {}
    for n in rep:
        size = math.prod(w_loc[n].shape)
        rep_parts[n] = allp[:, off:off + size].reshape((N_DEV,) + w_loc[n].shape)
        off += size

    g_loc, deltas, new_m, new_v = {}, {}, {}, {}
    for n in _WEIGHTS:
        if n in rep_parts:
            p, own = [rep_parts[n]], {}
        else:
            its = [(n, None)] if n in _SHARDED_SMALL else [(n, l) for l in range(depth)]
            p, own = [parts[it] for it in its], {"own": [chip_sums[it] for it in its], "own_slot": my_chip}
        g_loc[n], deltas[n], new_m[n], new_v[n] = _adamw(w_loc[n], p, m_loc[n], v_loc[n], name="adamw_" + n, **own)
    return (loss, grad_x, *[g_loc[n] for n in _WEIGHTS], *[deltas[n] for n in _WEIGHTS],
            *[new_m[n] for n in _WEIGHTS], *[new_v[n] for n in _WEIGHTS])


def _local_step(x, mem, loss_target, full):
    depth = full["w_in"].shape[0]
    Ws, WTs, Ps = [{} for _ in range(depth)], [{} for _ in range(depth)], []
    for l in range(depth):
        Ps.append({n: full[n][l] for n in _WEIGHTS if n != "rel_bias"})
        for n in _SHARDED_BIG:
            W, WT = _layer_weights(n, full[n][l].astype(BF16))
            Ws[l].update(W)
            WTs[l].update(WT)
    Gs = [{} for _ in range(depth)]
    loss_vec, grad_x, g_rel = _run_step(x, mem, loss_target, Ws, WTs, Ps, full["rel_bias"], Gs, None)
    grads = {"rel_bias": g_rel}
    for n in _WEIGHTS:
        if n != "rel_bias":
            grads[n] = jnp.stack([Gs[l][n].reshape(full[n].shape[1:]) for l in range(depth)], axis=0)
    return loss_vec[0, 0], grad_x, grads


def _run_step(x, mem, loss_target, Ws, WTs, Ps, rel_bias, Gs, hosts):
    B, S, D = x.shape
    depth = len(Ws)
    x2d = x.reshape(B * S, D)
    mem2 = mem.reshape(-1, D)
    tabs = _bias_tables(rel_bias)
    (h1,) = _norm_fwd(x2d, None, None, Ps[0]["mix_pre_g"], name="norm0")
    xc = x2d
    saved = []
    for l in range(depth):
        nxt = Ps[l + 1]["mix_pre_g"] if l + 1 < depth else None
        xc, h1, sv = _layer_fwd(l, xc, h1, mem2, Ws[l], Ps[l], tabs, nxt, B=B, S=S, hosts=hosts)
        saved.append(sv)
    loss_vec, d = _loss_kernel(xc, loss_target.reshape(B * S, D), name="loss")

    dh_next = None
    for l in reversed(range(depth)):
        nxt = Ps[l + 1]["mix_pre_g"] if l + 1 < depth else None
        d, dh_next, dg_next = _layer_bwd(l, d, dh_next, nxt, saved[l], mem2, Ws[l], WTs[l], Ps[l], tabs, Gs[l], B=B, S=S,
                                         hosts=hosts)
        if dg_next is not None:
            Gs[l + 1]["mix_pre_g"] = dg_next
    grad_x2d, Gs[0]["mix_pre_g"] = _norm_bwd(d, dh_next, x2d, Ps[0]["mix_pre_g"], None, None, name="norm0_bwd")

    dbias = jnp.stack([Gs[l]["dbias"] for l in range(depth)], axis=0)
    buckets = jnp.stack([t[1] for t in tabs], axis=0)
    rb = _bucket_sum(dbias, buckets, name="rel_bias_grad")
    return loss_vec, grad_x2d.reshape(B, S, D), rb[:, :rel_bias.shape[1]]
```

```python
import functools
import math

import jax
import jax.numpy as jnp
from jax import lax
from jax.experimental import pallas as pl
from jax.experimental.pallas import tpu as pltpu

F32 = jnp.float32
BF16 = jnp.bfloat16

N_DEV = 8
NORM_EPS = 1e-6
LN_EPS = 1e-5
CONV_K = 31
SG_CHUNK = 128
ATT_BLOCK = 128
HEAD_DIM = 64
HEADS_PER_GROUP = 4
GROUP_COLS = HEADS_PER_GROUP * HEAD_DIM
DIL_GROUPS = ((128, 1), (512, 4), (2048, 16))
REL_BUCKETS = 32
REL_MAX_DIST = 2048
X_HEADS = 4
ATT_SCALE = HEAD_DIM ** -0.5
MASK_VALUE = -1e30

ADAM_LR = 0.001
ADAM_B1 = 0.9
ADAM_B2 = 0.999
ADAM_EPS = 1e-08
ADAM_WD = 0.01
ADAM_STEP = 10

LANE = 128
ACT = BF16
ROW_TILE = 512
VMEM_LIMIT = 48 << 20

_NT = (((1,), (1,)), ((), ()))
_TN = (((0,), (0,)), ((), ()))


def _params(sem, vmem=VMEM_LIMIT):
    return pltpu.CompilerParams(dimension_semantics=sem, vmem_limit_bytes=vmem)


def _pick(n, cap):
    if n <= cap:
        return n
    best = None
    for t in range(LANE, cap + 1, LANE):
        if n % t == 0:
            best = t
    assert best is not None, (n, cap)
    return best


def _sigmoid(x):
    return 1.0 / (1.0 + jnp.exp(-x))


_GELU_C = math.sqrt(2.0 / math.pi)


def _gelu(x):
    return 0.5 * x * (1.0 + jnp.tanh(_GELU_C * (x + 0.044715 * x * x * x)))


def _gelu_and_grad(x):
    t = jnp.tanh(_GELU_C * (x + 0.044715 * x * x * x))
    g = 0.5 * x * (1.0 + t)
    dg = 0.5 * (1.0 + t) + 0.5 * x * (1.0 - t * t) * _GELU_C * (1.0 + 3 * 0.044715 * x * x)
    return g, dg


MM_VMEM_BUDGET = 36 << 20


def _divisors128(n):
    return [n] + [t for t in range(n - n % LANE, 0, -LANE) if n % t == 0 and t != n]


class _Hosted:
    def __init__(self, hosts, name):
        self.ex = hosts.get(name) if hosts is not None else None
        self.hosts, self.name = hosts, name

    @property
    def on(self):
        return self.ex is not None

    def specs(self):
        if not self.on:
            return [], [], [], []
        n = self.ex.n_copies
        return (list(self.ex.ins), [_HBM] * len(self.ex.ins), list(self.ex.out_shapes),
                [pltpu.SemaphoreType.DMA((n,)), pltpu.SemaphoreType.DMA((n,))])

    def split(self, refs, n_in, n_out, n_scratch):
        if not self.on:
            return list(refs), None
        ni, no = len(self.ex.ins), len(self.ex.out_shapes)
        refs = list(refs)
        ins, rest = refs[:n_in], refs[n_in:]
        ex_in, rest = rest[:ni], rest[ni:]
        outs, rest = rest[:n_out], rest[n_out:]
        ex_out, rest = rest[:no], rest[no:]
        scratch, sems = rest[:n_scratch], rest[n_scratch:]
        return ins + outs + scratch, (ex_in, ex_out, sems[0], sems[1])

    def run(self, ex_refs, grid):
        if not self.on:
            return
        ids = [pl.program_id(ax) for ax in range(len(grid))]
        first = functools.reduce(jnp.logical_and, [i == 0 for i in ids])
        last = functools.reduce(jnp.logical_and, [i == g - 1 for i, g in zip(ids, grid)])

        @pl.when(first)
        def _():
            sends, _ = self.ex.make(*ex_refs)
            for cp in sends:
                cp.start()

        return last

    def finish(self, ex_refs, last):
        if not self.on:
            return

        @pl.when(last)
        def _():
            sends, recvs = self.ex.make(*ex_refs)
            for cp in recvs:
                cp.wait_recv()
            for cp in sends:
                cp.wait_send()

    def done(self, outs):
        if self.on:
            self.hosts.put(self.name, list(outs))


def _mm(a, b, *, out_dtype, name, add=None, hosts=None):
    M, K = a.shape
    K2, N = b.shape
    assert K == K2
    has_add = add is not None
    ob = jnp.dtype(out_dtype).itemsize
    tm = _pick(M, 1024)
    tk = _pick(K, 3072)
    nk = K // tk
    for tn in _divisors128(N):
        use = 2 * (tm * tk * a.dtype.itemsize + tk * tn * b.dtype.itemsize + tm * tn * (ob + (4 if has_add else 0)))
        if use + (tm * tn * 4 if nk > 1 else 0) <= MM_VMEM_BUDGET:
            break
    grid = (M // tm, N // tn, nk)
    hosted = _Hosted(hosts, name)
    n_in = 3 if has_add else 2
    n_scr = 1 if nk > 1 else 0

    def body(*refs):
        own, ex_refs = hosted.split(refs, n_in, 1, n_scr)
        last = hosted.run(ex_refs, grid)
        a_ref, b_ref = own[0], own[1]
        c_ref = own[2] if has_add else None
        o_ref = own[n_in]
        part = jnp.dot(a_ref[...].astype(BF16), b_ref[...].astype(BF16), preferred_element_type=F32)
        if nk == 1:
            if has_add:
                part = part + c_ref[...]
            o_ref[...] = part.astype(o_ref.dtype)
        else:
            acc_ref = own[n_in + 1]
            k = pl.program_id(2)

            @pl.when(k == 0)
            def _():
                acc_ref[...] = part

            @pl.when(k > 0)
            def _():
                acc_ref[...] += part

            @pl.when(k == nk - 1)
            def _():
                r = acc_ref[...]
                if has_add:
                    r = r + c_ref[...]
                o_ref[...] = r.astype(o_ref.dtype)
        hosted.finish(ex_refs, last)

    in_specs = [pl.BlockSpec((tm, tk), lambda i, j, k: (i, k)), pl.BlockSpec((tk, tn), lambda i, j, k: (k, j))]
    args = [a, b]
    if has_add:
        in_specs.append(pl.BlockSpec((tm, tn), lambda i, j, k: (i, j)))
        args.append(add)
    ex_args, ex_in_specs, ex_out_shapes, ex_scratch = hosted.specs()
    outs = pl.pallas_call(
        body, name=name, out_shape=(jax.ShapeDtypeStruct((M, N), out_dtype), *ex_out_shapes),
        grid=grid, in_specs=in_specs + ex_in_specs,
        out_specs=(pl.BlockSpec((tm, tn), lambda i, j, k: (i, j)), *([_HBM] * len(ex_out_shapes))),
        scratch_shapes=([pltpu.VMEM((tm, tn), F32)] if nk > 1 else []) + ex_scratch,
        compiler_params=_params(("arbitrary",) * 3 if hosted.on else ("parallel", "parallel", "arbitrary")),
    )(*args, *ex_args)
    hosted.done(outs[1:])
    return outs[0]


def _mm_tn(a, b, *, name, hosts=None):
    T, M = a.shape
    T2, N = b.shape
    assert T == T2
    tm = _pick(M, 1536)
    tt = _pick(T, 2048)
    nt = T // tt
    for tn in _divisors128(N):
        if 2 * (tt * tm * a.dtype.itemsize + tt * tn * b.dtype.itemsize + tm * tn * 4) <= MM_VMEM_BUDGET:
            break
    grid = (M // tm, N // tn, nt)
    hosted = _Hosted(hosts, name)

    def body(*refs):
        (a_ref, b_ref, o_ref), ex_refs = hosted.split(refs, 2, 1, 0)
        last = hosted.run(ex_refs, grid)
        k = pl.program_id(2)
        part = lax.dot_general(a_ref[...].astype(BF16), b_ref[...].astype(BF16), _TN, preferred_element_type=F32)

        @pl.when(k == 0)
        def _():
            o_ref[...] = part

        @pl.when(k > 0)
        def _():
            o_ref[...] += part

        hosted.finish(ex_refs, last)

    ex_args, ex_in_specs, ex_out_shapes, ex_scratch = hosted.specs()
    outs = pl.pallas_call(
        body, name=name, out_shape=(jax.ShapeDtypeStruct((M, N), F32), *ex_out_shapes),
        grid=grid,
        in_specs=[pl.BlockSpec((tt, tm), lambda i, j, k: (k, i)), pl.BlockSpec((tt, tn), lambda i, j, k: (k, j))]
        + ex_in_specs,
        out_specs=(pl.BlockSpec((tm, tn), lambda i, j, k: (i, j)), *([_HBM] * len(ex_out_shapes))),
        scratch_shapes=ex_scratch,
        compiler_params=_params(("arbitrary",) * 3 if hosted.on else ("parallel", "parallel", "arbitrary")),
    )(a, b, *ex_args)
    hosted.done(outs[1:])
    return outs[0]


def _rms(x):
    r = lax.rsqrt(jnp.mean(x * x, axis=-1, keepdims=True) + NORM_EPS)
    return x * r, r


def _row_spec(cols, tr=ROW_TILE):
    return pl.BlockSpec((tr, cols), lambda i: (i, 0))


def _vec_spec(cols):
    return pl.BlockSpec((1, cols), lambda i: (0, 0))


def _norm_fwd(x, y, g_post, g_pre, *, name):
    T, D = x.shape
    has_post = y is not None
    has_pre = g_pre is not None

    def body(*refs):
        refs = list(refs)
        x_ref = refs.pop(0)
        xn = x_ref[...]
        if has_post:
            y_ref = refs.pop(0)
            gp_ref = refs.pop(0)
        if has_pre:
            gq_ref = refs.pop(0)
        if has_post:
            yh, _ = _rms(y_ref[...].astype(F32))
            xn = xn + yh * gp_ref[...]
            refs.pop(0)[...] = xn
        if has_pre:
            xh, _ = _rms(xn)
            refs.pop(0)[...] = (xh * gq_ref[...]).astype(BF16)

    args, in_specs, out_shape, out_specs = [x], [_row_spec(D)], [], []
    if has_post:
        args += [y, g_post.reshape(1, D)]
        in_specs += [_row_spec(D), _vec_spec(D)]
        out_shape.append(jax.ShapeDtypeStruct((T, D), F32))
        out_specs.append(_row_spec(D))
    if has_pre:
        args.append(g_pre.reshape(1, D))
        in_specs.append(_vec_spec(D))
        out_shape.append(jax.ShapeDtypeStruct((T, D), BF16))
        out_specs.append(_row_spec(D))
    return pl.pallas_call(body, name=name, out_shape=tuple(out_shape), grid=(T // ROW_TILE,), in_specs=in_specs,
                          out_specs=tuple(out_specs), compiler_params=_params(("parallel",)))(*args)


def _norm_bwd(dres, dh, x_new, g_pre, y, g_post, *, name):
    has_res = dres is not None
    has_pre = dh is not None
    has_post = y is not None
    out_d = has_pre and (has_res or has_post)
    T, D = (dres if has_res else dh).shape
    if not has_res:
        assert not has_post

    def body(*refs):
        refs = list(refs)
        i = pl.program_id(0)
        d = refs.pop(0)[...] if has_res else None
        if has_pre:
            dh_v = refs.pop(0)[...].astype(F32)
            xh, r = _rms(refs.pop(0)[...])
            gq = refs.pop(0)[...]
        if has_post:
            yh, ry = _rms(refs.pop(0)[...].astype(F32))
            gp = refs.pop(0)[...]
        if has_pre:
            dxh = dh_v * gq
            dpre = r * (dxh - xh * jnp.mean(dxh * xh, axis=-1, keepdims=True))
            d = dpre if d is None else d + dpre
            dgq = jnp.sum(dh_v * xh, axis=0, keepdims=True)
        if out_d:
            refs.pop(0)[...] = d
        if has_post:
            dyh = d * gp
            refs.pop(0)[...] = (ry * (dyh - yh * jnp.mean(dyh * yh, axis=-1, keepdims=True))).astype(BF16)
            dgp_ref = refs.pop(0)
            dgp = jnp.sum(d * yh, axis=0, keepdims=True)

            @pl.when(i == 0)
            def _():
                dgp_ref[...] = dgp

            @pl.when(i > 0)
            def _():
                dgp_ref[...] += dgp
        if has_pre:
            dgq_ref = refs.pop(0)

            @pl.when(i == 0)
            def _():
                dgq_ref[...] = dgq

            @pl.when(i > 0)
            def _():
                dgq_ref[...] += dgq

    args, in_specs, out_shape, out_specs = [], [], [], []
    if has_res:
        args.append(dres)
        in_specs.append(_row_spec(D))
    if has_pre:
        args += [dh, x_new, g_pre.reshape(1, D)]
        in_specs += [_row_spec(D), _row_spec(D), _vec_spec(D)]
    if has_post:
        args += [y, g_post.reshape(1, D)]
        in_specs += [_row_spec(D), _vec_spec(D)]
    if out_d:
        out_shape.append(jax.ShapeDtypeStruct((T, D), F32))
        out_specs.append(_row_spec(D))
    if has_post:
        out_shape += [jax.ShapeDtypeStruct((T, D), BF16), jax.ShapeDtypeStruct((1, D), F32)]
        out_specs += [_row_spec(D), _vec_spec(D)]
    if has_pre:
        out_shape.append(jax.ShapeDtypeStruct((1, D), F32))
        out_specs.append(_vec_spec(D))
    return pl.pallas_call(body, name=name, out_shape=tuple(out_shape), grid=(T // ROW_TILE,), in_specs=in_specs,
                          out_specs=tuple(out_specs), compiler_params=_params(("arbitrary",)))(*args)


def _loss_kernel(xf, target, *, name):
    T, D = xf.shape

    def body(x_ref, t_ref, loss_ref, d_ref):
        i = pl.program_id(0)
        e = x_ref[...] - t_ref[...]
        d_ref[...] = e * (1.0 / D)
        part = jnp.sum(jnp.sum(e * e, axis=0, keepdims=True), axis=1, keepdims=True) * (0.5 / D)
        part = jnp.broadcast_to(part, (1, LANE))

        @pl.when(i == 0)
        def _():
            loss_ref[...] = part

        @pl.when(i > 0)
        def _():
            loss_ref[...] += part

    return pl.pallas_call(body, name=name,
                          out_shape=(jax.ShapeDtypeStruct((1, LANE), F32), jax.ShapeDtypeStruct((T, D), F32)),
                          grid=(T // ROW_TILE,), in_specs=[_row_spec(D), _row_spec(D)],
                          out_specs=(_vec_spec(LANE), _row_spec(D)), compiler_params=_params(("arbitrary",)))(xf, target)


CONV_PAD = 32
CONV_CH = 64
SUBLANES = 8


def _shift_rows(win, r):
    return win if r == 0 else win[r:r + win.shape[0] - SUBLANES, :]


def _ln_stats(x):
    mu = jnp.mean(x, axis=-1, keepdims=True)
    xc = x - mu
    rstd = lax.rsqrt(jnp.mean(xc * xc, axis=-1, keepdims=True) + LN_EPS)
    return xc * rstd, rstd


def _acc_out(ref, val, first):
    @pl.when(first)
    def _():
        ref[...] = val

    @pl.when(jnp.logical_not(first))
    def _():
        ref[...] += val


def _bra_fwd(za, conv_w, conv_b, ln_g, ln_b, *, B, S, name):
    C = conv_w.shape[1]

    def body(za_ref, w_ref, cb_ref, g_ref, b_ref, a4_ref, a2_ref, pad_ref):
        pad_ref[0:CONV_PAD, :] = jnp.zeros((CONV_PAD, C), F32)
        pad_ref[CONV_PAD:, :] = za_ref[:, 0:C].astype(F32) * _sigmoid(za_ref[:, C:2 * C].astype(F32))

        def chunk(i, _):
            base = pl.multiple_of(i * CONV_CH, CONV_CH)
            win = pad_ref[pl.ds(base, CONV_CH + CONV_PAD), :]
            acc = jnp.broadcast_to(cb_ref[...], (CONV_CH, C))
            for r in range(SUBLANES):
                sh = _shift_rows(win, r)
                for j in range(CONV_K):
                    off = CONV_PAD - (CONV_K - 1) + j
                    if off % SUBLANES == r:
                        acc = acc + w_ref[j:j + 1, :] * sh[off - r:off - r + CONV_CH, :]
            a2_ref[pl.ds(base, CONV_CH), :] = acc
            xh, _ = _ln_stats(acc)
            a3 = xh * g_ref[...] + b_ref[...]
            a4_ref[pl.ds(base, CONV_CH), :] = (a3 * _sigmoid(a3)).astype(BF16)
            return 0

        lax.fori_loop(0, S // CONV_CH, chunk, 0)

    vec = pl.BlockSpec((1, C), lambda b: (0, 0))
    return pl.pallas_call(
        body, name=name,
        out_shape=(jax.ShapeDtypeStruct((B * S, C), BF16), jax.ShapeDtypeStruct((B * S, C), F32)),
        grid=(B,),
        in_specs=[pl.BlockSpec((S, 2 * C), lambda b: (b, 0)), pl.BlockSpec((CONV_K, C), lambda b: (0, 0)), vec, vec, vec],
        out_specs=(pl.BlockSpec((S, C), lambda b: (b, 0)), pl.BlockSpec((S, C), lambda b: (b, 0))),
        scratch_shapes=[pltpu.VMEM((S + CONV_PAD, C), F32)],
        compiler_params=_params(("parallel",)),
    )(za, conv_w, conv_b.reshape(1, C), ln_g.reshape(1, C), ln_b.reshape(1, C))


def _bra_bwd_ln(da4, a2, ln_g, ln_b, *, name):
    T, C = a2.shape

    def body(da4_ref, a2_ref, g_ref, b_ref, da2_ref, dg_ref, db_ref, dcb_ref):
        first = pl.program_id(0) == 0
        xh, rstd = _ln_stats(a2_ref[...])
        a3 = xh * g_ref[...] + b_ref[...]
        sg = _sigmoid(a3)
        da3 = da4_ref[...].astype(F32) * (sg * (1.0 + a3 * (1.0 - sg)))
        dxh = da3 * g_ref[...]
        da2 = rstd * (dxh - jnp.mean(dxh, axis=-1, keepdims=True) - xh * jnp.mean(dxh * xh, axis=-1, keepdims=True))
        da2_ref[...] = da2
        _acc_out(dg_ref, jnp.sum(da3 * xh, axis=0, keepdims=True), first)
        _acc_out(db_ref, jnp.sum(da3, axis=0, keepdims=True), first)
        _acc_out(dcb_ref, jnp.sum(da2, axis=0, keepdims=True), first)

    vec = _vec_spec(C)
    vshape = jax.ShapeDtypeStruct((1, C), F32)
    return pl.pallas_call(body, name=name, out_shape=(jax.ShapeDtypeStruct((T, C), F32), vshape, vshape, vshape),
                          grid=(T // ROW_TILE,), in_specs=[_row_spec(C), _row_spec(C), vec, vec],
                          out_specs=(_row_spec(C), vec, vec, vec),
                          compiler_params=_params(("arbitrary",)))(da4, a2, ln_g.reshape(1, C), ln_b.reshape(1, C))


def _bra_bwd_conv(da2, za, conv_w, *, B, S, name):
    C = conv_w.shape[1]
    SUB = 8

    def body(da2_ref, za_ref, w_ref, dza_ref, dw_ref, pad1_ref, pad2_ref, dwacc_ref):
        first = pl.program_id(0) == 0
        pad1_ref[0:CONV_PAD, :] = jnp.zeros((CONV_PAD, C), F32)
        pad1_ref[CONV_PAD:, :] = za_ref[:, 0:C].astype(F32) * _sigmoid(za_ref[:, C:2 * C].astype(F32))
        pad2_ref[0:S, :] = da2_ref[...]
        pad2_ref[S:, :] = jnp.zeros((CONV_PAD, C), F32)
        dwacc_ref[...] = jnp.zeros_like(dwacc_ref)

        def chunk(i, _):
            base = pl.multiple_of(i * CONV_CH, CONV_CH)
            rows = pl.ds(base, CONV_CH)
            win1 = pad1_ref[pl.ds(base, CONV_CH + CONV_PAD), :]
            win2 = pad2_ref[pl.ds(base, CONV_CH + CONV_PAD), :]
            d = win2[0:CONV_CH, :]
            da1 = jnp.zeros((CONV_CH, C), F32)
            for r in range(SUBLANES):
                sh1 = _shift_rows(win1, r)
                sh2 = _shift_rows(win2, r)
                for j in range(CONV_K):
                    off2 = CONV_K - 1 - j
                    if off2 % SUBLANES == r:
                        da1 = da1 + w_ref[j:j + 1, :] * sh2[off2 - r:off2 - r + CONV_CH, :]
                    off1 = CONV_PAD - (CONV_K - 1) + j
                    if off1 % SUBLANES == r:
                        prod = d * sh1[off1 - r:off1 - r + CONV_CH, :]
                        dwacc_ref[j * SUB:(j + 1) * SUB, :] += jnp.sum(prod.reshape(CONV_CH // SUB, SUB, C), axis=0)
            a_val = za_ref[rows, 0:C].astype(F32)
            sg = _sigmoid(za_ref[rows, C:2 * C].astype(F32))
            dza_ref[rows, 0:C] = (da1 * sg).astype(BF16)
            dza_ref[rows, C:2 * C] = (da1 * a_val * sg * (1.0 - sg)).astype(BF16)
            return 0

        lax.fori_loop(0, S // CONV_CH, chunk, 0)
        _acc_out(dw_ref, jnp.sum(dwacc_ref[...].reshape(CONV_K, SUB, C), axis=1), first)

    return pl.pallas_call(
        body, name=name,
        out_shape=(jax.ShapeDtypeStruct((B * S, 2 * C), BF16), jax.ShapeDtypeStruct((CONV_K, C), F32)),
        grid=(B,),
        in_specs=[pl.BlockSpec((S, C), lambda b: (b, 0)), pl.BlockSpec((S, 2 * C), lambda b: (b, 0)),
                  pl.BlockSpec((CONV_K, C), lambda b: (0, 0))],
        out_specs=(pl.BlockSpec((S, 2 * C), lambda b: (b, 0)), pl.BlockSpec((CONV_K, C), lambda b: (0, 0))),
        scratch_shapes=[pltpu.VMEM((S + CONV_PAD, C), F32), pltpu.VMEM((S + CONV_PAD, C), F32),
                        pltpu.VMEM((CONV_K * SUB, C), F32)],
        compiler_params=_params(("arbitrary",)),
    )(da2, za, conv_w)


def _tril_mask():
    r = lax.broadcasted_iota(jnp.int32, (SG_CHUNK, SG_CHUNK), 0)
    c = lax.broadcasted_iota(jnp.int32, (SG_CHUNK, SG_CHUNK), 1)
    return c <= r


def _brb_fwd(zb, ln_g, ln_b, w_s, b_s_t, *, name):
    T, C2 = zb.shape
    C = C2 // 2
    G = w_s.shape[0]
    GC = C // G

    def body(zb_ref, g_ref, b_ref, ws_ref, bs_ref, p_ref):
        z = _gelu(zb_ref[...].astype(F32))
        u = z[:, 0:C]
        xh, _ = _ln_stats(z[:, C:2 * C])
        v1 = (xh * g_ref[...] + b_ref[...]).astype(BF16)
        mask = _tril_mask()
        outs = []
        for gi in range(G):
            ws = jnp.where(mask, ws_ref[gi], 0.0).astype(BF16)
            v2 = jnp.dot(ws, v1[:, gi * GC:(gi + 1) * GC], preferred_element_type=F32) + bs_ref[:, gi:gi + 1]
            outs.append(v2)
        p_ref[...] = (u * jnp.concatenate(outs, axis=1)).astype(BF16)

    return pl.pallas_call(
        body, name=name, out_shape=jax.ShapeDtypeStruct((T, C), BF16), grid=(T // SG_CHUNK,),
        in_specs=[_row_spec(C2, SG_CHUNK), _vec_spec(C), _vec_spec(C),
                  pl.BlockSpec((G, SG_CHUNK, SG_CHUNK), lambda i: (0, 0, 0)), pl.BlockSpec((SG_CHUNK, G), lambda i: (0, 0))],
        out_specs=_row_spec(C, SG_CHUNK), compiler_params=_params(("parallel",)),
    )(zb, ln_g.reshape(1, C), ln_b.reshape(1, C), w_s, b_s_t)


def _brb_bwd(dp, zb, ln_g, ln_b, w_s, b_s_t, *, name):
    T, C2 = zb.shape
    C = C2 // 2
    G = w_s.shape[0]
    GC = C // G

    def body(dp_ref, zb_ref, g_ref, b_ref, ws_ref, bs_ref, dzb_ref, dws_ref, dbs_ref, dg_ref, db_ref):
        first = pl.program_id(0) == 0
        z, dz = _gelu_and_grad(zb_ref[...].astype(F32))
        u = z[:, 0:C]
        xh, rstd = _ln_stats(z[:, C:2 * C])
        v1 = (xh * g_ref[...] + b_ref[...]).astype(BF16)
        dp_v = dp_ref[...].astype(F32)
        mask = _tril_mask()
        v2s, dv1s, dwss, dbss = [], [], [], []
        for gi in range(G):
            cols = slice(gi * GC, (gi + 1) * GC)
            ws = jnp.where(mask, ws_ref[gi], 0.0).astype(BF16)
            v2s.append(jnp.dot(ws, v1[:, cols], preferred_element_type=F32) + bs_ref[:, gi:gi + 1])
            dv2 = dp_v[:, cols] * u[:, cols]
            dv2b = dv2.astype(BF16)
            dbss.append(jnp.sum(dv2, axis=1, keepdims=True))
            dwss.append(jnp.where(mask, lax.dot_general(dv2b, v1[:, cols], _NT, preferred_element_type=F32), 0.0))
            dv1s.append(lax.dot_general(ws, dv2b, _TN, preferred_element_type=F32))
        du = dp_v * jnp.concatenate(v2s, axis=1)
        dv1 = jnp.concatenate(dv1s, axis=1)
        dxh = dv1 * g_ref[...]
        dv0 = rstd * (dxh - jnp.mean(dxh, axis=-1, keepdims=True) - xh * jnp.mean(dxh * xh, axis=-1, keepdims=True))
        dzb_ref[:, 0:C] = (du * dz[:, 0:C]).astype(BF16)
        dzb_ref[:, C:2 * C] = (dv0 * dz[:, C:2 * C]).astype(BF16)
        _acc_out(dws_ref, jnp.stack(dwss, axis=0), first)
        _acc_out(dbs_ref, jnp.concatenate(dbss, axis=1), first)
        _acc_out(dg_ref, jnp.sum(dv1 * xh, axis=0, keepdims=True), first)
        _acc_out(db_ref, jnp.sum(dv1, axis=0, keepdims=True), first)

    wspec = pl.BlockSpec((G, SG_CHUNK, SG_CHUNK), lambda i: (0, 0, 0))
    bspec = pl.BlockSpec((SG_CHUNK, G), lambda i: (0, 0))
    return pl.pallas_call(
        body, name=name,
        out_shape=(jax.ShapeDtypeStruct((T, C2), BF16), jax.ShapeDtypeStruct((G, SG_CHUNK, SG_CHUNK), F32),
                   jax.ShapeDtypeStruct((SG_CHUNK, G), F32), jax.ShapeDtypeStruct((1, C), F32),
                   jax.ShapeDtypeStruct((1, C), F32)),
        grid=(T // SG_CHUNK,),
        in_specs=[_row_spec(C, SG_CHUNK), _row_spec(C2, SG_CHUNK), _vec_spec(C), _vec_spec(C), wspec, bspec],
        out_specs=(_row_spec(C2, SG_CHUNK), wspec, bspec, _vec_spec(C), _vec_spec(C)),
        compiler_params=_params(("arbitrary",)),
    )(dp, zb, ln_g.reshape(1, C), ln_b.reshape(1, C), w_s, b_s_t)


def _attn_fwd(qkv_g, bias, *, name):
    B, d, L, _ = qkv_g.shape
    nb = L // ATT_BLOCK
    GCOL = GROUP_COLS

    def body(qkv_ref, bias_ref, o_ref, lse_ref):
        def blk(r0, first):
            nk = ATT_BLOCK if first else 2 * ATT_BLOCK
            k0 = r0 if first else r0 - ATT_BLOCK
            qb = qkv_ref[pl.ds(r0, ATT_BLOCK), 0:GCOL]
            kb = qkv_ref[pl.ds(k0, nk), GCOL:2 * GCOL]
            vb = qkv_ref[pl.ds(k0, nk), 2 * GCOL:3 * GCOL]
            outs, lses = [], []
            for h in range(HEADS_PER_GROUP):
                sl = slice(h * HEAD_DIM, (h + 1) * HEAD_DIM)
                bh = bias_ref[h, :, ATT_BLOCK:2 * ATT_BLOCK] if first else bias_ref[h]
                s = lax.dot_general(qb[:, sl], kb[:, sl], _NT, preferred_element_type=F32) * ATT_SCALE + bh
                m = jnp.max(s, axis=1, keepdims=True)
                e = jnp.exp(s - m)
                ssum = jnp.sum(e, axis=1, keepdims=True)
                outs.append(jnp.dot(e.astype(BF16), vb[:, sl], preferred_element_type=F32) / ssum)
                lses.append(jnp.broadcast_to(m + jnp.log(ssum), (ATT_BLOCK, HEAD_DIM)))
            o_ref[pl.ds(r0, ATT_BLOCK), :] = jnp.concatenate(outs, axis=1)
            lse_ref[pl.ds(r0, ATT_BLOCK), :] = jnp.concatenate(lses, axis=1)

        blk(0, True)
        if nb > 1:
            def loop(n, _):
                blk(pl.multiple_of(n * ATT_BLOCK, ATT_BLOCK), False)
                return 0

            lax.fori_loop(1, nb, loop, 0)

    spec = lambda cols: pl.BlockSpec((None, None, L, cols), lambda b, c: (b, c, 0, 0))
    oshape = jax.ShapeDtypeStruct((B, d, L, GCOL), F32)
    return pl.pallas_call(
        body, name=name, out_shape=(oshape, oshape), grid=(B, d),
        in_specs=[spec(3 * GCOL), pl.BlockSpec((HEADS_PER_GROUP, ATT_BLOCK, 2 * ATT_BLOCK), lambda b, c: (0, 0, 0))],
        out_specs=(spec(GCOL), spec(GCOL)), compiler_params=_params(("parallel", "parallel")),
    )(qkv_g, bias)


def _attn_bwd(qkv_g, doc_g, lse_g, dd_g, bias, *, name):
    B, d, L, _ = qkv_g.shape
    nb = L // ATT_BLOCK
    GCOL = GROUP_COLS

    def body(qkv_ref, doc_ref, lse_ref, dd_ref, bias_ref, dqkv_ref, dbias_ref, dk_ref, dv_ref):
        @pl.when(jnp.logical_and(pl.program_id(0) == 0, pl.program_id(1) == 0))
        def _():
            dbias_ref[...] = jnp.zeros_like(dbias_ref)

        dk_ref[...] = jnp.zeros_like(dk_ref)
        dv_ref[...] = jnp.zeros_like(dv_ref)

        def blk(r0, first):
            nk = ATT_BLOCK if first else 2 * ATT_BLOCK
            k0 = r0 if first else r0 - ATT_BLOCK
            rows = pl.ds(r0, ATT_BLOCK)
            krows = pl.ds(k0, nk)
            qb = qkv_ref[rows, 0:GCOL]
            kb = qkv_ref[krows, GCOL:2 * GCOL]
            vb = qkv_ref[krows, 2 * GCOL:3 * GCOL]
            dob = doc_ref[rows, :].astype(BF16)
            lse = lse_ref[rows, :]
            dd = dd_ref[rows, :]
            dqs, dks, dvs = [], [], []
            for h in range(HEADS_PER_GROUP):
                sl = slice(h * HEAD_DIM, (h + 1) * HEAD_DIM)
                c0 = h * HEAD_DIM
                bh = bias_ref[h, :, ATT_BLOCK:2 * ATT_BLOCK] if first else bias_ref[h]
                s = lax.dot_general(qb[:, sl], kb[:, sl], _NT, preferred_element_type=F32) * ATT_SCALE + bh
                p = jnp.exp(s - lse[:, c0:c0 + 1])
                dp = lax.dot_general(dob[:, sl], vb[:, sl], _NT, preferred_element_type=F32)
                ds = p * (dp - dd[:, c0:c0 + 1])
                if first:
                    dbias_ref[h, :, ATT_BLOCK:2 * ATT_BLOCK] += ds
                else:
                    dbias_ref[h] += ds
                dsb = ds.astype(BF16)
                dqs.append(jnp.dot(dsb, kb[:, sl], preferred_element_type=F32) * ATT_SCALE)
                dks.append(lax.dot_general(dsb, qb[:, sl], _TN, preferred_element_type=F32) * ATT_SCALE)
                dvs.append(lax.dot_general(p.astype(BF16), dob[:, sl], _TN, preferred_element_type=F32))
            dqkv_ref[rows, 0:GCOL] = jnp.concatenate(dqs, axis=1).astype(BF16)
            dk_ref[krows, :] += jnp.concatenate(dks, axis=1)
            dv_ref[krows, :] += jnp.concatenate(dvs, axis=1)

        blk(0, True)
        if nb > 1:
            def loop(n, _):
                blk(pl.multiple_of(n * ATT_BLOCK, ATT_BLOCK), False)
                return 0

            lax.fori_loop(1, nb, loop, 0)
        dqkv_ref[:, GCOL:2 * GCOL] = dk_ref[...].astype(BF16)
        dqkv_ref[:, 2 * GCOL:3 * GCOL] = dv_ref[...].astype(BF16)

    spec = lambda cols: pl.BlockSpec((None, None, L, cols), lambda b, c: (b, c, 0, 0))
    bspec = pl.BlockSpec((HEADS_PER_GROUP, ATT_BLOCK, 2 * ATT_BLOCK), lambda b, c: (0, 0, 0))
    return pl.pallas_call(
        body, name=name,
        out_shape=(jax.ShapeDtypeStruct((B, d, L, 3 * GCOL), BF16),
                   jax.ShapeDtypeStruct((HEADS_PER_GROUP, ATT_BLOCK, 2 * ATT_BLOCK), F32)),
        grid=(B, d),
        in_specs=[spec(3 * GCOL), spec(GCOL), spec(GCOL), spec(GCOL), bspec],
        out_specs=(spec(3 * GCOL), bspec),
        scratch_shapes=[pltpu.VMEM((L, GCOL), F32), pltpu.VMEM((L, GCOL), F32)],
        compiler_params=_params(("arbitrary", "arbitrary")),
    )(qkv_g, doc_g, lse_g, dd_g, bias)


def _attn_combine(os_, lses, *, name):
    T, GC = os_[0].shape
    n = len(os_)

    def body(*refs):
        o_refs, l_refs, oc_ref, lt_ref = refs[:n], refs[n:2 * n], refs[2 * n], refs[2 * n + 1]
        ls = [r[...] for r in l_refs]
        m = functools.reduce(jnp.maximum, ls)
        ws = [jnp.exp(l - m) for l in ls]
        tot = functools.reduce(jnp.add, ws)
        acc = functools.reduce(jnp.add, [w * r[...] for w, r in zip(ws, o_refs)])
        oc_ref[...] = acc / tot
        lt_ref[...] = m + jnp.log(tot)

    shp = jax.ShapeDtypeStruct((T, GC), F32)
    return pl.pallas_call(body, name=name, out_shape=(shp, shp), grid=(T // ROW_TILE,),
                          in_specs=[_row_spec(GC)] * (2 * n), out_specs=(_row_spec(GC), _row_spec(GC)),
                          compiler_params=_params(("parallel",)))(*os_, *lses)


def _attn_rowdot(doc, oc, *, name):
    T, GC = oc.shape

    def body(doc_ref, oc_ref, dd_ref):
        prod = doc_ref[...].astype(F32) * oc_ref[...]
        parts = []
        for h in range(GC // HEAD_DIM):
            s = jnp.sum(prod[:, h * HEAD_DIM:(h + 1) * HEAD_DIM], axis=1, keepdims=True)
            parts.append(jnp.broadcast_to(s, (ROW_TILE, HEAD_DIM)))
        dd_ref[...] = jnp.concatenate(parts, axis=1)

    return pl.pallas_call(body, name=name, out_shape=jax.ShapeDtypeStruct((T, GC), F32), grid=(T // ROW_TILE,),
                          in_specs=[_row_spec(GC), _row_spec(GC)], out_specs=_row_spec(GC),
                          compiler_params=_params(("parallel",)))(doc, oc)


def _bucket_sum(dbias, buckets, *, name):
    depth, NH = dbias.shape[:2]

    def body(db_ref, bk_ref, out_ref):
        rows = lax.broadcasted_iota(jnp.int32, (REL_BUCKETS, LANE), 0)
        cols = lax.broadcasted_iota(jnp.int32, (REL_BUCKETS, LANE), 1)

        def per_bucket(b, acc):
            for h in range(NH):
                sel = bk_ref[h // HEADS_PER_GROUP] == b
                tot = functools.reduce(jnp.add, [db_ref[l, h] for l in range(depth)])
                s = jnp.sum(jnp.where(sel, tot, 0.0))
                acc = acc + jnp.where(jnp.logical_and(rows == b, cols == h), s, 0.0)
            return acc

        out_ref[...] = lax.fori_loop(0, REL_BUCKETS, per_bucket, jnp.zeros((REL_BUCKETS, LANE), F32))

    return pl.pallas_call(body, name=name, out_shape=jax.ShapeDtypeStruct((REL_BUCKETS, LANE), F32),
                          compiler_params=pltpu.CompilerParams(vmem_limit_bytes=VMEM_LIMIT))(dbias, buckets)


def _merge_fwd(zg, bg, ya, yb, yc, *, name):
    T, D3 = zg.shape
    D = D3 // 3

    def body(zg_ref, bg_ref, ya_ref, yb_ref, yc_ref, out_ref):
        acc = None
        for i, y_ref in enumerate((ya_ref, yb_ref, yc_ref)):
            g = _sigmoid(zg_ref[:, i * D:(i + 1) * D].astype(F32) + bg_ref[:, i * D:(i + 1) * D])
            t = g * y_ref[...].astype(F32)
            acc = t if acc is None else acc + t
        out_ref[...] = acc.astype(BF16)

    return pl.pallas_call(body, name=name, out_shape=jax.ShapeDtypeStruct((T, D), BF16), grid=(T // ROW_TILE,),
                          in_specs=[_row_spec(D3), _vec_spec(D3), _row_spec(D), _row_spec(D), _row_spec(D)],
                          out_specs=_row_spec(D), compiler_params=_params(("parallel",)))(zg, bg.reshape(1, D3), ya, yb, yc)


def _merge_bwd(dm, zg, bg, ya, yb, yc, *, name):
    T, D3 = zg.shape
    D = D3 // 3

    def body(dm_ref, zg_ref, bg_ref, ya_ref, yb_ref, yc_ref, dya_ref, dyb_ref, dyc_ref, dzg_ref, dbg_ref):
        first = pl.program_id(0) == 0
        dm_v = dm_ref[...].astype(F32)
        dbs = []
        for i, (y_ref, dy_ref) in enumerate(((ya_ref, dya_ref), (yb_ref, dyb_ref), (yc_ref, dyc_ref))):
            g = _sigmoid(zg_ref[:, i * D:(i + 1) * D].astype(F32) + bg_ref[:, i * D:(i + 1) * D])
            dy_ref[...] = (dm_v * g).astype(BF16)
            dz = dm_v * y_ref[...].astype(F32) * g * (1.0 - g)
            dzg_ref[:, i * D:(i + 1) * D] = dz.astype(BF16)
            dbs.append(jnp.sum(dz, axis=0, keepdims=True))
        _acc_out(dbg_ref, jnp.concatenate(dbs, axis=1), first)

    bshape = jax.ShapeDtypeStruct((T, D), BF16)
    return pl.pallas_call(
        body, name=name,
        out_shape=(bshape, bshape, bshape, jax.ShapeDtypeStruct((T, D3), BF16), jax.ShapeDtypeStruct((1, D3), F32)),
        grid=(T // ROW_TILE,),
        in_specs=[_row_spec(D), _row_spec(D3), _vec_spec(D3), _row_spec(D), _row_spec(D), _row_spec(D)],
        out_specs=(_row_spec(D), _row_spec(D), _row_spec(D), _row_spec(D3), _vec_spec(D3)),
        compiler_params=_params(("arbitrary",)))(dm, zg, bg.reshape(1, D3), ya, yb, yc)


XQ_TILE = 512


def _xattn_fwd(q, kv, *, B, S, name):
    D = q.shape[1]
    M = kv.shape[0] // B
    E = D // X_HEADS
    scale = E ** -0.5

    def body(q_ref, kv_ref, o_ref):
        outs = []
        for h in range(X_HEADS):
            s = lax.dot_general(q_ref[:, h * E:(h + 1) * E], kv_ref[:, h * E:(h + 1) * E], _NT,
                                preferred_element_type=F32) * scale
            e = jnp.exp(s - jnp.max(s, axis=1, keepdims=True))
            p = e / jnp.sum(e, axis=1, keepdims=True)
            outs.append(jnp.dot(p.astype(BF16), kv_ref[:, D + h * E:D + (h + 1) * E], preferred_element_type=F32))
        o_ref[...] = jnp.concatenate(outs, axis=1).astype(BF16)

    nq = S // XQ_TILE
    return pl.pallas_call(
        body, name=name, out_shape=jax.ShapeDtypeStruct((B * S, D), BF16), grid=(B, nq),
        in_specs=[pl.BlockSpec((XQ_TILE, D), lambda b, i: (b * nq + i, 0)), pl.BlockSpec((M, 2 * D), lambda b, i: (b, 0))],
        out_specs=pl.BlockSpec((XQ_TILE, D), lambda b, i: (b * nq + i, 0)),
        compiler_params=_params(("parallel", "parallel")))(q, kv)


def _xattn_bwd(q, kv, do, *, B, S, name):
    D = q.shape[1]
    M = kv.shape[0] // B
    E = D // X_HEADS
    scale = E ** -0.5

    def body(q_ref, kv_ref, do_ref, dq_ref, dkv_ref):
        first = pl.program_id(1) == 0
        dqs, dks, dvs = [], [], []
        for h in range(X_HEADS):
            qh = q_ref[:, h * E:(h + 1) * E]
            kh = kv_ref[:, h * E:(h + 1) * E]
            vh = kv_ref[:, D + h * E:D + (h + 1) * E]
            doh = do_ref[:, h * E:(h + 1) * E]
            s = lax.dot_general(qh, kh, _NT, preferred_element_type=F32) * scale
            e = jnp.exp(s - jnp.max(s, axis=1, keepdims=True))
            p = e / jnp.sum(e, axis=1, keepdims=True)
            dp = lax.dot_general(doh, vh, _NT, preferred_element_type=F32)
            ds = (p * (dp - jnp.sum(p * dp, axis=1, keepdims=True))).astype(BF16)
            dqs.append(jnp.dot(ds, kh, preferred_element_type=F32) * scale)
            dks.append(lax.dot_general(ds, qh, _TN, preferred_element_type=F32) * scale)
            dvs.append(lax.dot_general(p.astype(BF16), doh, _TN, preferred_element_type=F32))
        dq_ref[...] = jnp.concatenate(dqs, axis=1).astype(BF16)
        _acc_out(dkv_ref, jnp.concatenate(dks + dvs, axis=1), first)

    nq = S // XQ_TILE
    qspec = pl.BlockSpec((XQ_TILE, D), lambda b, i: (b * nq + i, 0))
    kvspec = pl.BlockSpec((M, 2 * D), lambda b, i: (b, 0))
    return pl.pallas_call(
        body, name=name,
        out_shape=(jax.ShapeDtypeStruct((B * S, D), BF16), jax.ShapeDtypeStruct((B * M, 2 * D), F32)),
        grid=(B, nq), in_specs=[qspec, kvspec, qspec], out_specs=(qspec, kvspec),
        compiler_params=_params(("arbitrary", "arbitrary")))(q, kv, do)


FFN_COLS = 256
FFN_PAD = 8
FFN_CH = 256
FFN_K = 3


def _ffn_gate(win, w_ref, cb_ref, n):
    g = jnp.broadcast_to(cb_ref[...], (n, win.shape[1]))
    for j in range(FFN_K):
        off = FFN_PAD - (FFN_K - 1) + j
        g = g + w_ref[j:j + 1, :] * win[off:off + n, :]
    return g


def _ffn_interleave(w):
    lead, n2 = w.shape[:-1], w.shape[-1]
    nc = n2 // (2 * FFN_COLS)
    return jnp.swapaxes(w.reshape(lead + (2, nc, FFN_COLS)), -3, -2).reshape(lead + (n2,))


def _ffn_deinterleave(w):
    lead, n2 = w.shape[:-1], w.shape[-1]
    nc = n2 // (2 * FFN_COLS)
    return jnp.swapaxes(w.reshape(lead + (nc, 2, FFN_COLS)), -3, -2).reshape(lead + (n2,))


def _ffn_fwd(up, conv_w, conv_b, *, B, S, name):
    F = conv_w.shape[1]
    nc = F // FFN_COLS

    def body(gp_ref, val_ref, w_ref, cb_ref, act_ref, pad_ref):
        pad_ref[0:FFN_PAD, :] = jnp.zeros((FFN_PAD, FFN_COLS), F32)
        pad_ref[FFN_PAD:, :] = gp_ref[...].astype(F32)

        def chunk(i, _):
            base = pl.multiple_of(i * FFN_CH, FFN_CH)
            gate = _ffn_gate(pad_ref[pl.ds(base, FFN_CH + FFN_PAD), :], w_ref, cb_ref, FFN_CH)
            act_ref[pl.ds(base, FFN_CH), :] = (_gelu(gate) * val_ref[pl.ds(base, FFN_CH), :].astype(F32)).astype(BF16)
            return 0

        lax.fori_loop(0, S // FFN_CH, chunk, 0)

    return pl.pallas_call(
        body, name=name, out_shape=jax.ShapeDtypeStruct((B * S, F), BF16), grid=(B, nc),
        in_specs=[pl.BlockSpec((S, FFN_COLS), lambda b, j: (b, 2 * j)), pl.BlockSpec((S, FFN_COLS), lambda b, j: (b, 2 * j + 1)),
                  pl.BlockSpec((FFN_K, FFN_COLS), lambda b, j: (0, j)), pl.BlockSpec((1, FFN_COLS), lambda b, j: (0, j))],
        out_specs=pl.BlockSpec((S, FFN_COLS), lambda b, j: (b, j)),
        scratch_shapes=[pltpu.VMEM((S + FFN_PAD, FFN_COLS), F32)],
        compiler_params=_params(("parallel", "parallel")))(up, up, conv_w, conv_b.reshape(1, F))


def _ffn_bwd(dact, up, conv_w, conv_b, *, B, S, name):
    F = conv_w.shape[1]
    nc = F // FFN_COLS
    SUB = 8

    def body(dact_ref, gp_ref, val_ref, w_ref, cb_ref, dup_ref, dw_ref, dcb_ref, pad_ref, pad2_ref, acc_ref):
        first = pl.program_id(1) == 0
        pad_ref[0:FFN_PAD, :] = jnp.zeros((FFN_PAD, FFN_COLS), F32)
        pad_ref[FFN_PAD:, :] = gp_ref[...].astype(F32)
        pad2_ref[S:, :] = jnp.zeros((FFN_PAD, FFN_COLS), F32)
        acc_ref[...] = jnp.zeros_like(acc_ref)

        def chunk1(i, _):
            base = pl.multiple_of(i * FFN_CH, FFN_CH)
            rows = pl.ds(base, FFN_CH)
            gate = _ffn_gate(pad_ref[pl.ds(base, FFN_CH + FFN_PAD), :], w_ref, cb_ref, FFN_CH)
            gl, dgl = _gelu_and_grad(gate)
            da = dact_ref[rows, :].astype(F32)
            dup_ref[rows, FFN_COLS:2 * FFN_COLS] = (da * gl).astype(BF16)
            pad2_ref[rows, :] = da * val_ref[rows, :].astype(F32) * dgl
            return 0

        lax.fori_loop(0, S // FFN_CH, chunk1, 0)

        def chunk2(i, _):
            base = pl.multiple_of(i * FFN_CH, FFN_CH)
            rows = pl.ds(base, FFN_CH)
            win2 = pad2_ref[pl.ds(base, FFN_CH + FFN_PAD), :]
            win1 = pad_ref[pl.ds(base, FFN_CH + FFN_PAD), :]
            dg = win2[0:FFN_CH, :]
            dgp = jnp.zeros((FFN_CH, FFN_COLS), F32)
            for j in range(FFN_K):
                off2 = FFN_K - 1 - j
                dgp = dgp + w_ref[j:j + 1, :] * win2[off2:off2 + FFN_CH, :]
                off1 = FFN_PAD - (FFN_K - 1) + j
                prod = dg * win1[off1:off1 + FFN_CH, :]
                acc_ref[j * SUB:(j + 1) * SUB, :] += jnp.sum(prod.reshape(FFN_CH // SUB, SUB, FFN_COLS), axis=0)
            acc_ref[FFN_K * SUB:(FFN_K + 1) * SUB, :] += jnp.sum(dg.reshape(FFN_CH // SUB, SUB, FFN_COLS), axis=0)
            dup_ref[rows, 0:FFN_COLS] = dgp.astype(BF16)
            return 0

        lax.fori_loop(0, S // FFN_CH, chunk2, 0)
        sums = jnp.sum(acc_ref[...].reshape(FFN_K + 1, SUB, FFN_COLS), axis=1)
        _acc_out(dw_ref, sums[0:FFN_K, :], first)
        _acc_out(dcb_ref, sums[FFN_K:FFN_K + 1, :], first)

    return pl.pallas_call(
        body, name=name,
        out_shape=(jax.ShapeDtypeStruct((B * S, 2 * F), BF16),
                   jax.ShapeDtypeStruct((FFN_K, F), F32), jax.ShapeDtypeStruct((1, F), F32)),
        grid=(nc, B),
        in_specs=[pl.BlockSpec((S, FFN_COLS), lambda j, b: (b, j)), pl.BlockSpec((S, FFN_COLS), lambda j, b: (b, 2 * j)),
                  pl.BlockSpec((S, FFN_COLS), lambda j, b: (b, 2 * j + 1)),
                  pl.BlockSpec((FFN_K, FFN_COLS), lambda j, b: (0, j)), pl.BlockSpec((1, FFN_COLS), lambda j, b: (0, j))],
        out_specs=(pl.BlockSpec((S, 2 * FFN_COLS), lambda j, b: (b, j)),
                   pl.BlockSpec((FFN_K, FFN_COLS), lambda j, b: (0, j)), pl.BlockSpec((1, FFN_COLS), lambda j, b: (0, j))),
        scratch_shapes=[pltpu.VMEM((S + FFN_PAD, FFN_COLS), F32), pltpu.VMEM((S + FFN_PAD, FFN_COLS), F32),
                        pltpu.VMEM(((FFN_K + 1) * SUB, FFN_COLS), F32)],
        compiler_params=_params(("arbitrary", "arbitrary")))(dact, up, up, conv_w, conv_b.reshape(1, F))


def _row_tile(rows, row_bytes, budget):
    tr = rows
    if rows * row_bytes > budget:
        for t in range(16, rows, 16):
            if rows % t == 0 and t * row_bytes <= budget:
                tr = t
    return tr


def _adamw(w, parts, m, v, *, name, own=None, own_slot=None):
    shape = w.shape
    L = len(parts)
    n = parts[0].shape[0]
    cols = shape[-1]
    rows = w.size // (cols * L)
    w3, m3, v3 = (t.reshape(L, rows, cols) for t in (w, m, v))
    tr = _row_tile(rows, cols * 4, 1 << 19)
    c1 = 1.0 - ADAM_B1 ** ADAM_STEP
    c2 = 1.0 - ADAM_B2 ** ADAM_STEP
    has_own = own is not None

    def body(*refs):
        refs = list(refs)
        slot_ref = refs.pop(0) if has_own else None
        w_ref = refs.pop(0)
        p_refs = [refs.pop(0) for _ in range(L)]
        o_refs = [refs.pop(0) for _ in range(L)] if has_own else None
        m_ref, v_ref, g_ref, d_ref, mo_ref, vo_ref = refs
        layer = pl.program_id(0)
        gv = None
        for j in range(L):
            gj = None
            for i in range(n):
                t = p_refs[j][i].astype(F32)
                if has_own:
                    t = jnp.where(slot_ref[0] == i, o_refs[j][i].astype(F32), t)
                gj = t if gj is None else gj + t
            gv = gj if gv is None else jnp.where(layer == j, gj, gv)
        g_ref[...] = gv
        mn = ADAM_B1 * m_ref[...] + (1.0 - ADAM_B1) * gv
        vn = ADAM_B2 * v_ref[...] + (1.0 - ADAM_B2) * (gv * gv)
        d_ref[...] = -ADAM_LR * ((mn / c1) / (jnp.sqrt(vn / c2) + ADAM_EPS) + ADAM_WD * w_ref[...])
        mo_ref[...] = mn
        vo_ref[...] = vn

    spec = pl.BlockSpec((None, tr, cols), lambda l, i, *_: (l, i, 0))

    def pspec(j):
        return pl.BlockSpec((n, tr, cols), lambda l, i, *_: (0, jnp.where(l == j, i, 0), 0))

    oshape = jax.ShapeDtypeStruct((L, rows, cols), F32)
    in_specs = [spec] + [pspec(j) for j in range(L)] * (2 if has_own else 1) + [spec, spec]
    args = [w3] + [p.reshape(n, rows, cols) for p in parts]
    if has_own:
        args += [o.reshape(n, rows, cols) for o in own]
    args += [m3, v3]
    grid = (L, rows // tr)
    if has_own:
        grid_spec = pltpu.PrefetchScalarGridSpec(num_scalar_prefetch=1, grid=grid, in_specs=in_specs,
                                                 out_specs=(spec,) * 4)
        outs = pl.pallas_call(body, name=name, out_shape=(oshape,) * 4, grid_spec=grid_spec,
                              compiler_params=_params(("parallel", "parallel")))(own_slot, *args)
    else:
        outs = pl.pallas_call(body, name=name, out_shape=(oshape,) * 4, grid=grid, in_specs=in_specs,
                              out_specs=(spec,) * 4, compiler_params=_params(("parallel", "parallel")))(*args)
    return tuple(t.reshape(shape) for t in outs)


def _chip_sum(g, got, core, *, name):
    shard = got.shape[1:]
    cols = shard[-1]
    rows = math.prod(shard) // cols
    tr = _row_tile(rows, cols * 4, 1 << 20)

    def body(c_ref, g_ref, r_ref, o_ref):
        o_ref[...] = (g_ref[...].astype(F32) + r_ref[...].astype(F32)).astype(o_ref.dtype)

    blk = (None, tr, cols)
    grid_spec = pltpu.PrefetchScalarGridSpec(
        num_scalar_prefetch=1, grid=(N_CHIP, rows // tr),
        in_specs=[pl.BlockSpec(blk, lambda s, i, c: (2 * s + c[0], i, 0)), pl.BlockSpec(blk, lambda s, i, c: (s, i, 0))],
        out_specs=pl.BlockSpec(blk, lambda s, i, c: (s, i, 0)))
    out = pl.pallas_call(body, name=name, out_shape=jax.ShapeDtypeStruct((N_CHIP, rows, cols), g.dtype),
                         grid_spec=grid_spec, compiler_params=_params(("parallel", "parallel")),
                         )(core, g.reshape(N_DEV, rows, cols), got.reshape(N_CHIP, rows, cols))
    return out.reshape((N_CHIP,) + shard)


_HBM = pl.BlockSpec(memory_space=pltpu.HBM)
_MESH = pl.DeviceIdType.MESH


def _my_pos():
    return lax.axis_index("x"), lax.axis_index("y"), lax.axis_index("c")


def _flip(pos, k):
    x, y, c = pos
    fx, fy, fc = (k >> 2) & 1, (k >> 1) & 1, k & 1
    return (x ^ fx if fx else x, y ^ fy if fy else y, c ^ fc if fc else c)


def _index_of(pos):
    return 4 * pos[0] + 2 * pos[1] + pos[2]


def _all_gather(xs, *, name):
    n = len(xs)

    def body(*refs):
        x_refs, out_refs = refs[:n], refs[n:2 * n]
        send_sems, recv_sems = refs[2 * n:]
        me = _my_pos()
        sibling = _flip(me, 1)
        chips = [2, 4, 6]

        def copy(i, k, block_pos, to, from_x=False):
            blk = out_refs[i].at[_index_of(block_pos)]
            return pltpu.make_async_remote_copy(src_ref=x_refs[i] if from_x else blk, dst_ref=blk,
                                                send_sem=send_sems.at[k, i], recv_sem=recv_sems.at[k, i],
                                                device_id=to, device_id_type=_MESH)

        first = [copy(i, 1 + j, me, _flip(me, f), from_x=True) for j, f in enumerate(chips) for i in range(n)]
        first += [copy(i, 0, me, sibling, from_x=True) for i in range(n)]
        for cp in first:
            cp.start()
        passed = []
        for j, f in enumerate(chips):
            for i in range(n):
                copy(i, 1 + j, _flip(me, f), me).wait_recv()
                cp = copy(i, 4 + j, _flip(me, f), sibling)
                cp.start()
                passed.append(cp)
        for i in range(n):
            copy(i, 0, sibling, me).wait_recv()
        for j, f in enumerate(chips):
            for i in range(n):
                copy(i, 4 + j, _flip(sibling, f), me).wait_recv()
        for cp in first + passed:
            cp.wait_send()

    return pl.pallas_call(
        body, name=name, out_shape=tuple(jax.ShapeDtypeStruct((N_DEV,) + x.shape, x.dtype) for x in xs),
        in_specs=[_HBM] * n, out_specs=(_HBM,) * n,
        scratch_shapes=[pltpu.SemaphoreType.DMA((7, n)), pltpu.SemaphoreType.DMA((7, n))],
    )(*xs)


N_CHIP = 4


def _scatter_d2d(gs):
    n = len(gs)

    def make(g_refs, recv_refs, send_sems, recv_sems):
        me = _my_pos()
        sibling = _flip(me, 1)
        c = me[2]
        sends = []
        for i in range(n):
            for s in range(N_CHIP):
                sends.append(pltpu.make_async_remote_copy(
                    src_ref=g_refs[i].at[2 * s + 1 - c], dst_ref=recv_refs[i].at[s],
                    send_sem=send_sems.at[N_CHIP * i + s], recv_sem=recv_sems.at[N_CHIP * i + s],
                    device_id=sibling, device_id_type=_MESH))
        return sends, sends

    shapes = [jax.ShapeDtypeStruct((N_CHIP,) + g.shape[1:], g.dtype) for g in gs]
    return _Exchange(list(gs), shapes, N_CHIP * n, make)


def _scatter_ici(ss):
    n = len(ss)

    def make(s_refs, r_refs, send_sems, recv_sems):
        me = _my_pos()
        my_chip = 2 * me[0] + me[1]
        sends, recvs = [], []
        for k in (1, 2, 3):
            peer = _flip(me, 2 * k)
            peer_chip = 2 * peer[0] + peer[1]
            for i in range(n):
                j = 3 * i + k - 1
                sends.append(pltpu.make_async_remote_copy(
                    src_ref=s_refs[i].at[peer_chip], dst_ref=r_refs[i].at[my_chip], send_sem=send_sems.at[j],
                    recv_sem=recv_sems.at[j], device_id=peer, device_id_type=_MESH))
                recvs.append(pltpu.make_async_remote_copy(
                    src_ref=s_refs[i].at[my_chip], dst_ref=r_refs[i].at[peer_chip], send_sem=send_sems.at[j],
                    recv_sem=recv_sems.at[j], device_id=peer, device_id_type=_MESH))
        return sends, recvs

    return _Exchange(list(ss), [jax.ShapeDtypeStruct(s.shape, s.dtype) for s in ss], 3 * n, make)


def _gather_ici(shards, layers):
    n = len(shards)
    FLIPS = (2, 4, 6, 1)

    def make(x_refs, out_refs, send_sems, recv_sems):
        me = _my_pos()
        sends, recvs = [], []
        for k, f in enumerate(FLIPS):
            peer = _flip(me, f)
            for i in range(n):
                src = x_refs[i] if layers[i] is None else x_refs[i].at[layers[i]]
                j = len(FLIPS) * i + k
                sends.append(pltpu.make_async_remote_copy(
                    src_ref=src, dst_ref=out_refs[i].at[_index_of(me)], send_sem=send_sems.at[j],
                    recv_sem=recv_sems.at[j], device_id=peer, device_id_type=_MESH))
                recvs.append(pltpu.make_async_remote_copy(
                    src_ref=src, dst_ref=out_refs[i].at[_index_of(peer)], send_sem=send_sems.at[j],
                    recv_sem=recv_sems.at[j], device_id=peer, device_id_type=_MESH))
        return sends, recvs

    shapes = [jax.ShapeDtypeStruct((N_DEV,) + (x.shape if l is None else x.shape[1:]), x.dtype)
              for x, l in zip(shards, layers)]
    return _Exchange(list(shards), shapes, len(FLIPS) * n, make)


def _gather_d2d(blocks):
    n = len(blocks)
    FLIPS = (2, 4, 6)

    def make(in_refs, out_refs, send_sems, recv_sems):
        me = _my_pos()
        sibling = _flip(me, 1)
        sends, recvs = [], []
        for k, f in enumerate(FLIPS):
            for i in range(n):
                j = len(FLIPS) * i + k
                mine = out_refs[i].at[_index_of(_flip(me, f))]
                sends.append(pltpu.make_async_remote_copy(
                    src_ref=mine, dst_ref=mine, send_sem=send_sems.at[j], recv_sem=recv_sems.at[j],
                    device_id=sibling, device_id_type=_MESH))
                recvs.append(pltpu.make_async_remote_copy(
                    src_ref=mine, dst_ref=out_refs[i].at[_index_of(_flip(sibling, f))], send_sem=send_sems.at[j],
                    recv_sem=recv_sems.at[j], device_id=sibling, device_id_type=_MESH))
        return sends, recvs

    shapes = [jax.ShapeDtypeStruct(b.shape, b.dtype) for b in blocks]
    return _Exchange(list(blocks), shapes, len(FLIPS) * n, make, aliases={i: i for i in range(n)})


class _Exchange:
    def __init__(self, ins, out_shapes, n_copies, make, aliases=None):
        self.ins, self.out_shapes, self.n_copies, self.make, self.aliases = ins, out_shapes, n_copies, make, aliases or {}


def _exchange_call(ex, *, name):
    n_in, n_out = len(ex.ins), len(ex.out_shapes)

    def body(*refs):
        sends, recvs = ex.make(refs[:n_in], refs[n_in:n_in + n_out], refs[n_in + n_out], refs[n_in + n_out + 1])
        for cp in sends:
            cp.start()
        for cp in recvs:
            cp.wait_recv()
        for cp in sends:
            cp.wait_send()

    return pl.pallas_call(
        body, name=name, out_shape=tuple(ex.out_shapes), in_specs=[_HBM] * n_in, out_specs=(_HBM,) * n_out,
        scratch_shapes=[pltpu.SemaphoreType.DMA((ex.n_copies,)), pltpu.SemaphoreType.DMA((ex.n_copies,))],
        input_output_aliases=ex.aliases,
    )(*ex.ins)


class _Hosts:
    def __init__(self):
        self.plan = {}

    def add(self, name, build, done):
        assert name not in self.plan
        self.plan[name] = (build, done)

    def get(self, name):
        return self.plan[name][0]() if name in self.plan else None

    def put(self, name, outs):
        self.plan.pop(name)[1](outs)


def _t5_bucket(dist):
    n = jnp.maximum(dist, 0)
    max_exact = REL_BUCKETS // 2
    nf = jnp.maximum(n, 1).astype(F32)
    large = max_exact + (jnp.log(nf / max_exact) / math.log(REL_MAX_DIST / max_exact)
                         * (REL_BUCKETS - max_exact)).astype(jnp.int32)
    large = jnp.minimum(large, REL_BUCKETS - 1)
    return jnp.where(n < max_exact, n, large)


def _bias_tables(rel_bias):
    qi = jnp.arange(ATT_BLOCK)[:, None]
    ki = jnp.arange(2 * ATT_BLOCK)[None, :]
    rel = qi + ATT_BLOCK - ki
    out = []
    for gi, (window, dil) in enumerate(DIL_GROUPS):
        span = window // dil
        bucket = _t5_bucket(rel * dil)
        valid = (rel >= 0) & (rel <= span)
        rb = rel_bias[:, gi * HEADS_PER_GROUP:(gi + 1) * HEADS_PER_GROUP]
        tab = functools.reduce(jnp.add, [jnp.where((bucket == b)[:, :, None], rb[b], 0.0)
                                         for b in range(REL_BUCKETS)])
        tab = jnp.where(valid[:, :, None], tab, MASK_VALUE).transpose(2, 0, 1)
        out.append((tab.astype(F32), bucket.astype(jnp.int32)))
    return out


def _regroup(t, B, S, d):
    C = t.shape[-1]
    return t.reshape(B, S // d, d, C).swapaxes(1, 2)


def _ungroup(t):
    B, d, L, C = t.shape
    return t.swapaxes(1, 2).reshape(B * L * d, C)


def _group_qkv(qkv, gi):
    n = len(DIL_GROUPS) * GROUP_COLS
    return jnp.concatenate([qkv[:, j * n + gi * GROUP_COLS: j * n + (gi + 1) * GROUP_COLS] for j in range(3)], axis=1)


def _layer_fwd(l, x0, h1, mem2, W, P, tabs, next_pre_g, *, B, S, hosts=None):
    tag = f"l{l}_"
    mm = functools.partial(_mm, hosts=hosts)
    sv = {"x0": x0, "h1": h1}
    za = mm(h1, W["in_a"], out_dtype=ACT, name=tag + "mm_in_a")
    zb = mm(h1, W["in_b"], out_dtype=ACT, name=tag + "mm_in_b")
    qkv = mm(h1, W["in_c"], out_dtype=BF16, name=tag + "mm_in_c")
    zg = mm(h1, W["in_g"], out_dtype=ACT, name=tag + "mm_in_g")
    a4, a2 = _bra_fwd(za, P["conv_a_w"], P["conv_a_b"], P["ln_a_g"], P["ln_a_b"], B=B, S=S, name=tag + "bra_fwd")
    ya = mm(a4, W["a_out"], out_dtype=ACT, name=tag + "mm_a_out")
    b_s_t = P["b_s"].T
    p = _brb_fwd(zb, P["ln_b_g"], P["ln_b_b"], P["w_s"], b_s_t, name=tag + "brb_fwd")
    yb = mm(p, W["b_out"], out_dtype=ACT, name=tag + "mm_b_out")
    os_, lses, qkv_gs = [], [], []
    for gi, (_, dil) in enumerate(DIL_GROUPS):
        qkv_g = _regroup(_group_qkv(qkv, gi), B, S, dil)
        o_g, lse_g = _attn_fwd(qkv_g, tabs[gi][0], name=tag + f"attn_fwd{gi}")
        qkv_gs.append(qkv_g)
        os_.append(_ungroup(o_g))
        lses.append(_ungroup(lse_g))
    oc, lse_tot = _attn_combine(os_, lses, name=tag + "attn_combine")
    yc = mm(oc, W["c_out"], out_dtype=ACT, name=tag + "mm_c_out")
    merged = _merge_fwd(zg, P["b_gate"], ya, yb, yc, name=tag + "merge_fwd")
    y1 = mm(merged, W["mix_out"], out_dtype=ACT, name=tag + "mm_mix")
    x1, h2 = _norm_fwd(x0, y1, P["mix_post_g"], P["x_pre_g"], name=tag + "norm1")
    q = mm(h2, W["xq"], out_dtype=BF16, name=tag + "mm_xq")
    (memn,) = _norm_fwd(mem2, None, None, P["mem_g"], name=tag + "norm_mem")
    kv = mm(memn, W["xkv"], out_dtype=BF16, name=tag + "mm_xkv")
    ox = _xattn_fwd(q, kv, B=B, S=S, name=tag + "xattn_fwd")
    y2 = mm(ox, W["xo"], out_dtype=ACT, name=tag + "mm_xo")
    x2, h3 = _norm_fwd(x1, y2, P["x_post_g"], P["ffn_pre_g"], name=tag + "norm2")
    up = mm(h3, W["up"], out_dtype=ACT, name=tag + "mm_up")
    act = _ffn_fwd(up, P["conv_f_w"], P["conv_f_b"], B=B, S=S, name=tag + "ffn_fwd")
    y3 = mm(act, W["down"], out_dtype=ACT, name=tag + "mm_down")
    outs = _norm_fwd(x2, y3, P["ffn_post_g"], next_pre_g, name=tag + "norm3")
    x3 = outs[0]
    h_next = outs[1] if next_pre_g is not None else None
    sv.update(za=za, zb=zb, zg=zg, a4=a4, a2=a2, ya=ya, p=p, yb=yb, qkv_gs=qkv_gs, oc=oc, lse_tot=lse_tot, yc=yc,
              merged=merged, y1=y1, x1=x1, h2=h2, q=q, memn=memn, kv=kv, ox=ox, y2=y2, x2=x2, h3=h3, up=up, act=act,
              y3=y3, x3=x3, b_s_t=b_s_t)
    return x3, h_next, sv


def _layer_bwd(l, d, dh_next, next_pre_g, sv, mem2, W, WT, P, tabs, G, *, B, S, hosts=None):
    tag = f"l{l}_"
    mm = functools.partial(_mm, hosts=hosts)
    mm_tn = functools.partial(_mm_tn, hosts=hosts)
    if dh_next is not None:
        d, dy3, G["ffn_post_g"], dg_next = _norm_bwd(d, dh_next, sv["x3"], next_pre_g, sv["y3"], P["ffn_post_g"],
                                                     name=tag + "norm3_bwd")
    else:
        dy3, G["ffn_post_g"] = _norm_bwd(d, None, None, None, sv["y3"], P["ffn_post_g"], name=tag + "norm3_bwd")
        dg_next = None
    dact = mm(dy3, WT["down"], out_dtype=ACT, name=tag + "mm_down_dx")
    G["w_down"] = mm_tn(sv["act"], dy3, name=tag + "mm_down_dw")
    dup, G["conv_f_w"], G["conv_f_b"] = _ffn_bwd(dact, sv["up"], P["conv_f_w"], P["conv_f_b"], B=B, S=S,
                                                name=tag + "ffn_bwd")
    dh3 = mm(dup, WT["up"], out_dtype=ACT, name=tag + "mm_up_dx")
    G["w_up"] = _ffn_deinterleave(mm_tn(sv["h3"], dup, name=tag + "mm_up_dw"))
    d, dy2, G["x_post_g"], G["ffn_pre_g"] = _norm_bwd(d, dh3, sv["x2"], P["ffn_pre_g"], sv["y2"], P["x_post_g"],
                                                      name=tag + "norm2_bwd")
    dox = mm(dy2, WT["xo"], out_dtype=BF16, name=tag + "mm_xo_dx")
    G["w_xo"] = mm_tn(sv["ox"], dy2, name=tag + "mm_xo_dw")
    dq, dkv = _xattn_bwd(sv["q"], sv["kv"], dox, B=B, S=S, name=tag + "xattn_bwd")
    dh2 = mm(dq, WT["xq"], out_dtype=ACT, name=tag + "mm_xq_dx")
    G["w_xq"] = mm_tn(sv["h2"], dq, name=tag + "mm_xq_dw")
    G["w_xkv"] = mm_tn(sv["memn"], dkv, name=tag + "mm_xkv_dw")
    dmemn = mm(dkv, WT["xkv"], out_dtype=F32, name=tag + "mm_xkv_dx")
    (G["mem_g"],) = _norm_bwd(None, dmemn, mem2, P["mem_g"], None, None, name=tag + "norm_mem_bwd")
    d, dy1, G["mix_post_g"], G["x_pre_g"] = _norm_bwd(d, dh2, sv["x1"], P["x_pre_g"], sv["y1"], P["mix_post_g"],
                                                      name=tag + "norm1_bwd")
    dm = mm(dy1, WT["mix_out"], out_dtype=ACT, name=tag + "mm_mix_dx")
    G["w_mix_out"] = mm_tn(sv["merged"], dy1, name=tag + "mm_mix_dw")
    dya, dyb, dyc, dzg, dbg = _merge_bwd(dm, sv["zg"], P["b_gate"], sv["ya"], sv["yb"], sv["yc"], name=tag + "merge_bwd")
    G["b_gate"] = dbg.reshape(P["b_gate"].shape)
    da4 = mm(dya, WT["a_out"], out_dtype=ACT, name=tag + "mm_a_out_dx")
    G["w_a_out"] = mm_tn(sv["a4"], dya, name=tag + "mm_a_out_dw")
    da2, G["ln_a_g"], G["ln_a_b"], G["conv_a_b"] = _bra_bwd_ln(da4, sv["a2"], P["ln_a_g"], P["ln_a_b"], name=tag + "bra_bwd_ln")
    dza, G["conv_a_w"] = _bra_bwd_conv(da2, sv["za"], P["conv_a_w"], B=B, S=S, name=tag + "bra_bwd_conv")
    dp = mm(dyb, WT["b_out"], out_dtype=ACT, name=tag + "mm_b_out_dx")
    G["w_b_out"] = mm_tn(sv["p"], dyb, name=tag + "mm_b_out_dw")
    dzb, G["w_s"], dbs_t, G["ln_b_g"], G["ln_b_b"] = _brb_bwd(dp, sv["zb"], P["ln_b_g"], P["ln_b_b"], P["w_s"],
                                                             sv["b_s_t"], name=tag + "brb_bwd")
    G["b_s"] = dbs_t.T
    doc = mm(dyc, WT["c_out"], out_dtype=ACT, name=tag + "mm_c_out_dx")
    G["w_c_out"] = mm_tn(sv["oc"], dyc, name=tag + "mm_c_out_dw")
    dd = _attn_rowdot(doc, sv["oc"], name=tag + "attn_rowdot")
    dq_parts, dk_parts, dv_parts, dbiases = [], [], [], []
    for gi, (_, dil) in enumerate(DIL_GROUPS):
        dqkv_g, dbias = _attn_bwd(sv["qkv_gs"][gi], _regroup(doc, B, S, dil), _regroup(sv["lse_tot"], B, S, dil),
                                  _regroup(dd, B, S, dil), tabs[gi][0], name=tag + f"attn_bwd{gi}")
        t = _ungroup(dqkv_g)
        dq_parts.append(t[:, 0:GROUP_COLS])
        dk_parts.append(t[:, GROUP_COLS:2 * GROUP_COLS])
        dv_parts.append(t[:, 2 * GROUP_COLS:3 * GROUP_COLS])
        dbiases.append(dbias)
    dqkv = jnp.concatenate(dq_parts + dk_parts + dv_parts, axis=1)
    G["dbias"] = jnp.concatenate(dbiases, axis=0)
    dh1 = mm(dza, WT["in_a"], out_dtype=F32, name=tag + "mm_in_a_dx")
    dh1 = mm(dzb, WT["in_b"], out_dtype=F32, name=tag + "mm_in_b_dx", add=dh1)
    dh1 = mm(dqkv, WT["in_c"], out_dtype=F32, name=tag + "mm_in_c_dx", add=dh1)
    dh1 = mm(dzg, WT["in_g"], out_dtype=ACT, name=tag + "mm_in_g_dx", add=dh1)
    h1 = sv["h1"]
    G["w_in"] = jnp.concatenate([mm_tn(h1, dza, name=tag + "mm_in_a_dw"), mm_tn(h1, dzb, name=tag + "mm_in_b_dw"),
                                 mm_tn(h1, dqkv, name=tag + "mm_in_c_dw"), mm_tn(h1, dzg, name=tag + "mm_in_g_dw")],
                                axis=1)
    return d, dh1, dg_next


_COL_SHARDED = ("w_in", "b_gate", "conv_a_w", "w_a_out", "w_b_out", "w_c_out", "w_xkv", "w_up", "conv_f_w")
_ROW_SHARDED = ("w_mix_out", "w_xq", "w_xo", "w_down")
_SHARDED_BIG = ("w_in", "w_a_out", "w_b_out", "w_c_out", "w_mix_out", "w_xq", "w_xkv", "w_xo", "w_up", "w_down")
_SHARDED_SMALL = ("b_gate", "conv_a_w", "conv_f_w")
_REPLICATED = ("mix_pre_g", "mix_post_g", "conv_a_b", "ln_a_g", "ln_a_b", "ln_b_g", "ln_b_b", "w_s", "b_s",
               "x_pre_g", "x_post_g", "mem_g", "ffn_pre_g", "ffn_post_g", "conv_f_b")
_WEIGHTS = ('rel_bias', 'mix_pre_g', 'mix_post_g', 'w_in', 'b_gate', 'conv_a_w', 'conv_a_b', 'ln_a_g', 'ln_a_b',
            'w_a_out', 'ln_b_g', 'ln_b_b', 'w_s', 'b_s', 'w_b_out', 'w_c_out', 'w_mix_out', 'x_pre_g', 'x_post_g',
            'mem_g', 'w_xq', 'w_xkv', 'w_xo', 'ffn_pre_g', 'ffn_post_g', 'w_up', 'conv_f_w', 'conv_f_b', 'w_down')
_PACK_COLS = 1024


def _shard_axis(name):
    return 1 if name in _ROW_SHARDED else 2


def _pack(parts, dtype, row_mult):
    flat = jnp.concatenate([p.astype(dtype).reshape(-1) for p in parts])
    n = flat.shape[0]
    unit = _PACK_COLS * row_mult
    padded = -(-n // unit) * unit
    flat = jnp.pad(flat, (0, padded - n))
    return flat.reshape(padded // _PACK_COLS, _PACK_COLS)


def _join8(blocks, ax):
    t = jnp.moveaxis(blocks, 0, ax)
    shp = t.shape
    return t.reshape(shp[:ax] + (shp[ax] * shp[ax + 1],) + shp[ax + 2:])


def _split8(full, ax):
    shp = full.shape
    t = full.reshape(shp[:ax] + (N_DEV, shp[ax] // N_DEV) + shp[ax + 1:])
    return jnp.moveaxis(t, ax, 0)


_W_KEYS = {"w_a_out": "a_out", "w_b_out": "b_out", "w_c_out": "c_out", "w_mix_out": "mix_out", "w_xq": "xq",
           "w_xkv": "xkv", "w_xo": "xo", "w_up": "up", "w_down": "down"}
_IN_SPLITS = (("in_a", 0, 1024), ("in_b", 1024, 2048), ("in_c", 2048, 4352), ("in_g", 4352, 7424))


def _layer_weights(name, full_l):
    if name == "w_in":
        assert full_l.shape[1] == _IN_SPLITS[-1][2]
        W = {k: full_l[:, a:b] for k, a, b in _IN_SPLITS}
    else:
        W = {_W_KEYS[name]: _ffn_interleave(full_l) if name == "w_up" else full_l}
    return W, {k: v.T for k, v in W.items()}


def _sum_rows(x, *, name):
    n = x.shape[0]

    def body(x_ref, o_ref):
        acc = x_ref[0:1, :]
        for i in range(1, n):
            acc = acc + x_ref[i:i + 1, :]
        o_ref[...] = acc

    return pl.pallas_call(body, name=name, out_shape=jax.ShapeDtypeStruct((1, x.shape[1]), x.dtype))(x)


def kernel(x, mem, rel_bias, mix_pre_g, mix_post_g, w_in, b_gate, conv_a_w, conv_a_b, ln_a_g, ln_a_b, w_a_out, ln_b_g, ln_b_b, w_s, b_s, w_b_out, w_c_out, w_mix_out, x_pre_g, x_post_g, mem_g, w_xq, w_xkv, w_xo, ffn_pre_g, ffn_post_g, w_up, conv_f_w, conv_f_b, w_down, loss_target, m_rel_bias, m_mix_pre_g, m_mix_post_g, m_w_in, m_b_gate, m_conv_a_w, m_conv_a_b, m_ln_a_g, m_ln_a_b, m_w_a_out, m_ln_b_g, m_ln_b_b, m_w_s, m_b_s, m_w_b_out, m_w_c_out, m_w_mix_out, m_x_pre_g, m_x_post_g, m_mem_g, m_w_xq, m_w_xkv, m_w_xo, m_ffn_pre_g, m_ffn_post_g, m_w_up, m_conv_f_w, m_conv_f_b, m_w_down, v_rel_bias, v_mix_pre_g, v_mix_post_g, v_w_in, v_b_gate, v_conv_a_w, v_conv_a_b, v_ln_a_g, v_ln_a_b, v_w_a_out, v_ln_b_g, v_ln_b_b, v_w_s, v_b_s, v_w_b_out, v_w_c_out, v_w_mix_out, v_x_pre_g, v_x_post_g, v_mem_g, v_w_xq, v_w_xkv, v_w_xo, v_ffn_pre_g, v_ffn_post_g, v_w_up, v_conv_f_w, v_conv_f_b, v_w_down):
    args = locals()
    w_loc = {n: args[n] for n in _WEIGHTS}
    m_loc = {n: args["m_" + n] for n in _WEIGHTS}
    v_loc = {n: args["v_" + n] for n in _WEIGHTS}

    depth = w_in.shape[0]
    assert depth == 2
    B, S, D = x.shape
    mine = _index_of(_my_pos())
    core = lax.axis_index("c").astype(jnp.int32).reshape(1)
    my_chip = (2 * lax.axis_index("x") + lax.axis_index("y")).astype(jnp.int32).reshape(1)
    shard = {n: w_loc[n].astype(BF16) for n in _SHARDED_BIG}
    shard.update({n: w_loc[n] for n in _SHARDED_SMALL})
    Ws, WTs = [{} for _ in range(depth)], [{} for _ in range(depth)]
    small_full = {}
    hosts = _Hosts()

    def install(items, blocks):
        for (n, l), blk in zip(items, blocks):
            own = shard[n] if l is None else shard[n][l]
            sel = lax.broadcasted_iota(jnp.int32, (N_DEV,) + (1,) * own.ndim, 0) == mine
            blk = jnp.where(sel, own[None], blk)
            if l is None:
                small_full[n] = _join8(blk, _shard_axis(n))
            else:
                W, WT = _layer_weights(n, _join8(blk, _shard_axis(n) - 1))
                Ws[l].update(W)
                WTs[l].update(WT)

    def carry_gather(name, items, group, then=None):
        def done(outs):
            group.extend(zip(items, outs))
            if then is not None:
                then()
        hosts.add(name, lambda: _gather_ici([shard[n] for n, _ in items], [l for _, l in items]), done)

    def finish_gather(group, tag):
        install([it for it, _ in group], _exchange_call(_gather_d2d([b for _, b in group]), name="gather_d2d_" + tag))

    first = [("w_in", 0)] + [(n, None) for n in _SHARDED_SMALL]
    install(first, _all_gather([shard[n] if l is None else shard[n][l] for n, l in first], name="gather_first"))
    g0, g1 = [], []
    carry_gather("l0_mm_in_a", [("w_xq", 0), ("w_a_out", 0), ("w_b_out", 0), ("w_c_out", 0)], g0)
    carry_gather("l0_mm_in_b", [("w_xo", 0), ("w_mix_out", 0)], g0)
    carry_gather("l0_mm_in_c", [("w_xkv", 0), ("w_down", 0)], g0)
    carry_gather("l0_mm_in_g", [("w_up", 0)], g0, then=lambda: finish_gather(g0, "l0"))
    carry_gather("l0_mm_mix", [("w_xq", 1), ("w_mix_out", 1)], g1)
    carry_gather("l0_mm_xq", [("w_xo", 1), ("w_a_out", 1), ("w_b_out", 1), ("w_c_out", 1)], g1)
    carry_gather("l0_mm_xo", [("w_xkv", 1)], g1)
    carry_gather("l0_mm_up", [("w_in", 1), ("w_down", 1)], g1)
    carry_gather("l0_mm_down", [("w_up", 1)], g1, then=lambda: finish_gather(g1, "l1"))

    Gs = [{} for _ in range(depth)]
    sendbuf, chip_sums, parts = {}, {}, {}

    def to_send(item):
        n, l = item
        if l is None:
            g = jnp.stack([Gs[k][n].reshape(small_full[n].shape[1:]) for k in range(depth)], axis=0)
            sendbuf[item] = _split8(g, _shard_axis(n))
        else:
            sendbuf[item] = _split8(Gs[l][n], _shard_axis(n) - 1).astype(BF16)
        return sendbuf[item]

    def swapped(items, got):
        for item, r in zip(items, got):
            chip_sums[item] = _chip_sum(sendbuf[item], r, core, name=f"chip_sum_{item[0]}_{item[1]}")

    def carry_swap(name, items):
        hosts.add(name, lambda: _scatter_d2d([to_send(it) for it in items]), lambda outs: swapped(items, outs))

    def carry_scatter(name, items):
        hosts.add(name, lambda: _scatter_ici([chip_sums[it] for it in items]), lambda outs: parts.update(zip(items, outs)))

    carry_swap("l0_mm_down_dx", [(n, 1) for n in _SHARDED_BIG])
    carry_scatter("l0_mm_down_dw", [(n, 1) for n in ("w_xq", "w_xkv", "w_xo", "w_mix_out", "w_a_out", "w_b_out", "w_c_out")])
    carry_scatter("l0_mm_up_dx", [("w_in", 1)])
    carry_scatter("l0_mm_up_dw", [("w_up", 1), ("w_down", 1)])
    carry_swap("l0_mm_xo_dx", [("w_down", 0), ("w_up", 0)])
    carry_scatter("l0_mm_xq_dx", [("w_down", 0)])
    carry_swap("l0_mm_mix_dx", [("w_xo", 0), ("w_xq", 0), ("w_xkv", 0)])
    carry_swap("l0_mm_in_a_dx", [("w_mix_out", 0), ("w_a_out", 0), ("w_b_out", 0), ("w_c_out", 0)])
    carry_scatter("l0_mm_in_b_dx", [("w_mix_out", 0), ("w_a_out", 0), ("w_b_out", 0), ("w_c_out", 0)])
    carry_scatter("l0_mm_in_c_dx", [("w_xo", 0), ("w_xq", 0), ("w_xkv", 0)])
    carry_scatter("l0_mm_in_g_dx", [("w_up", 0)])

    rep_w = {n: w_loc[n] for n in ("rel_bias",) + _REPLICATED}
    Ps = lambda: [dict({n: rep_w[n][l] for n in _REPLICATED}, **{n: small_full[n][l] for n in _SHARDED_SMALL})
                  for l in range(depth)]
    loss_vec, grad_x, g_rel = _run_step(x, mem, loss_target, Ws, WTs, Ps(), rep_w["rel_bias"], Gs, hosts)
    assert not hosts.plan, list(hosts.plan)

    last = [("w_in", 0)] + [(n, None) for n in _SHARDED_SMALL]
    swapped(last, _exchange_call(_scatter_d2d([to_send(it) for it in last]), name="swap_grads_last"))
    parts.update(zip(last, _exchange_call(_scatter_ici([chip_sums[it] for it in last]), name="scatter_grads_last")))

    rep = ("rel_bias",) + _REPLICATED
    rep_g = {n: jnp.stack([Gs[l][n].reshape(w_loc[n].shape[1:]) for l in range(depth)], axis=0) for n in _REPLICATED}
    rep_g["rel_bias"] = g_rel
    packed = _pack([loss_vec] + [rep_g[n] for n in rep], F32, 8)
    (allp,) = _all_gather([packed], name="gather_rep_grads")
    sel = lax.broadcasted_iota(jnp.int32, (N_DEV, 1, 1), 0) == mine
    allp = jnp.where(sel, packed[None], allp).reshape(N_DEV, -1)
    loss = _sum_rows(allp[:, :LANE], name="loss_sum")[0, 0]
    off = LANE
    rep_parts = {}
    for n in rep:
        size = math.prod(w_loc[n].shape)
        rep_parts[n] = allp[:, off:off + size].reshape((N_DEV,) + w_loc[n].shape)
        off += size

    g_loc, deltas, new_m, new_v = {}, {}, {}, {}
    for n in _WEIGHTS:
        if n in rep_parts:
            p, own = [rep_parts[n]], {}
        else:
            its = [(n, None)] if n in _SHARDED_SMALL else [(n, l) for l in range(depth)]
            p, own = [parts[it] for it in its], {"own": [chip_sums[it] for it in its], "own_slot": my_chip}
        g_loc[n], deltas[n], new_m[n], new_v[n] = _adamw(w_loc[n], p, m_loc[n], v_loc[n], name="adamw_" + n, **own)
    return (loss, grad_x, *[g_loc[n] for n in _WEIGHTS], *[deltas[n] for n in _WEIGHTS],
            *[new_m[n] for n in _WEIGHTS], *[new_v[n] for n in _WEIGHTS])


def _local_step(x, mem, loss_target, full):
    depth = full["w_in"].shape[0]
    Ws, WTs, Ps = [{} for _ in range(depth)], [{} for _ in range(depth)], []
    for l in range(depth):
        Ps.append({n: full[n][l] for n in _WEIGHTS if n != "rel_bias"})
        for n in _SHARDED_BIG:
            W, WT = _layer_weights(n, full[n][l].astype(BF16))
            Ws[l].update(W)
            WTs[l].update(WT)
    Gs = [{} for _ in range(depth)]
    loss_vec, grad_x, g_rel = _run_step(x, mem, loss_target, Ws, WTs, Ps, full["rel_bias"], Gs, None)
    grads = {"rel_bias": g_rel}
    for n in _WEIGHTS:
        if n != "rel_bias":
            grads[n] = jnp.stack([Gs[l][n].reshape(full[n].shape[1:]) for l in range(depth)], axis=0)
    return loss_vec[0, 0], grad_x, grads


def _run_step(x, mem, loss_target, Ws, WTs, Ps, rel_bias, Gs, hosts):
    B, S, D = x.shape
    depth = len(Ws)
    x2d = x.reshape(B * S, D)
    mem2 = mem.reshape(-1, D)
    tabs = _bias_tables(rel_bias)
    (h1,) = _norm_fwd(x2d, None, None, Ps[0]["mix_pre_g"], name="norm0")
    xc = x2d
    saved = []
    for l in range(depth):
        nxt = Ps[l + 1]["mix_pre_g"] if l + 1 < depth else None
        xc, h1, sv = _layer_fwd(l, xc, h1, mem2, Ws[l], Ps[l], tabs, nxt, B=B, S=S, hosts=hosts)
        saved.append(sv)
    loss_vec, d = _loss_kernel(xc, loss_target.reshape(B * S, D), name="loss")

    dh_next = None
    for l in reversed(range(depth)):
        nxt = Ps[l + 1]["mix_pre_g"] if l + 1 < depth else None
        d, dh_next, dg_next = _layer_bwd(l, d, dh_next, nxt, saved[l], mem2, Ws[l], WTs[l], Ps[l], tabs, Gs[l], B=B, S=S,
                                         hosts=hosts)
        if dg_next is not None:
            Gs[l + 1]["mix_pre_g"] = dg_next
    grad_x2d, Gs[0]["mix_pre_g"] = _norm_bwd(d, dh_next, x2d, Ps[0]["mix_pre_g"], None, None, name="norm0_bwd")

    dbias = jnp.stack([Gs[l]["dbias"] for l in range(depth)], axis=0)
    buckets = jnp.stack([t[1] for t in tabs], axis=0)
    rb = _bucket_sum(dbias, buckets, name="rel_bias_grad")
    return loss_vec, grad_x2d.reshape(B, S, D), rb[:, :rel_bias.shape[1]]
```

```python
import functools
import math

import jax
import jax.numpy as jnp
from jax import lax
from jax.experimental import pallas as pl
from jax.experimental.pallas import tpu as pltpu

F32 = jnp.float32
BF16 = jnp.bfloat16

N_DEV = 8
NORM_EPS = 1e-6
LN_EPS = 1e-5
CONV_K = 31
SG_CHUNK = 128
ATT_BLOCK = 128
HEAD_DIM = 64
HEADS_PER_GROUP = 4
GROUP_COLS = HEADS_PER_GROUP * HEAD_DIM
DIL_GROUPS = ((128, 1), (512, 4), (2048, 16))
REL_BUCKETS = 32
REL_MAX_DIST = 2048
X_HEADS = 4
ATT_SCALE = HEAD_DIM ** -0.5
MASK_VALUE = -1e30

ADAM_LR = 0.001
ADAM_B1 = 0.9
ADAM_B2 = 0.999
ADAM_EPS = 1e-08
ADAM_WD = 0.01
ADAM_STEP = 10

LANE = 128
ACT = BF16
ROW_TILE = 512
VMEM_LIMIT = 48 << 20

_NT = (((1,), (1,)), ((), ()))
_TN = (((0,), (0,)), ((), ()))


def _params(sem, vmem=VMEM_LIMIT):
    return pltpu.CompilerParams(dimension_semantics=sem, vmem_limit_bytes=vmem)


def _pick(n, cap):
    if n <= cap:
        return n
    best = None
    for t in range(LANE, cap + 1, LANE):
        if n % t == 0:
            best = t
    assert best is not None, (n, cap)
    return best


def _sigmoid(x):
    return 1.0 / (1.0 + jnp.exp(-x))


_GELU_C = math.sqrt(2.0 / math.pi)


def _gelu(x):
    return 0.5 * x * (1.0 + jnp.tanh(_GELU_C * (x + 0.044715 * x * x * x)))


def _gelu_and_grad(x):
    t = jnp.tanh(_GELU_C * (x + 0.044715 * x * x * x))
    g = 0.5 * x * (1.0 + t)
    dg = 0.5 * (1.0 + t) + 0.5 * x * (1.0 - t * t) * _GELU_C * (1.0 + 3 * 0.044715 * x * x)
    return g, dg


MM_VMEM_BUDGET = 36 << 20


def _divisors128(n):
    return [n] + [t for t in range(n - n % LANE, 0, -LANE) if n % t == 0 and t != n]


class _Hosted:
    def __init__(self, hosts, name):
        self.ex = hosts.get(name) if hosts is not None else None
        self.hosts, self.name = hosts, name

    @property
    def on(self):
        return self.ex is not None

    def specs(self):
        if not self.on:
            return [], [], [], []
        n = self.ex.n_copies
        return (list(self.ex.ins), [_HBM] * len(self.ex.ins), list(self.ex.out_shapes),
                [pltpu.SemaphoreType.DMA((n,)), pltpu.SemaphoreType.DMA((n,))])

    def split(self, refs, n_in, n_out, n_scratch):
        if not self.on:
            return list(refs), None
        ni, no = len(self.ex.ins), len(self.ex.out_shapes)
        refs = list(refs)
        ins, rest = refs[:n_in], refs[n_in:]
        ex_in, rest = rest[:ni], rest[ni:]
        outs, rest = rest[:n_out], rest[n_out:]
        ex_out, rest = rest[:no], rest[no:]
        scratch, sems = rest[:n_scratch], rest[n_scratch:]
        return ins + outs + scratch, (ex_in, ex_out, sems[0], sems[1])

    def run(self, ex_refs, grid):
        if not self.on:
            return
        ids = [pl.program_id(ax) for ax in range(len(grid))]
        first = functools.reduce(jnp.logical_and, [i == 0 for i in ids])
        last = functools.reduce(jnp.logical_and, [i == g - 1 for i, g in zip(ids, grid)])

        @pl.when(first)
        def _():
            sends, _ = self.ex.make(*ex_refs)
            for cp in sends:
                cp.start()

        return last

    def finish(self, ex_refs, last):
        if not self.on:
            return

        @pl.when(last)
        def _():
            sends, recvs = self.ex.make(*ex_refs)
            for cp in recvs:
                cp.wait_recv()
            for cp in sends:
                cp.wait_send()

    def done(self, outs):
        if self.on:
            self.hosts.put(self.name, list(outs))


def _mm(a, b, *, out_dtype, name, add=None, hosts=None):
    M, K = a.shape
    K2, N = b.shape
    assert K == K2
    has_add = add is not None
    ob = jnp.dtype(out_dtype).itemsize
    tm = _pick(M, 1024)
    tk = _pick(K, 3072)
    nk = K // tk
    for tn in _divisors128(N):
        use = 2 * (tm * tk * a.dtype.itemsize + tk * tn * b.dtype.itemsize + tm * tn * (ob + (4 if has_add else 0)))
        if use + (tm * tn * 4 if nk > 1 else 0) <= MM_VMEM_BUDGET:
            break
    grid = (M // tm, N // tn, nk)
    hosted = _Hosted(hosts, name)
    n_in = 3 if has_add else 2
    n_scr = 1 if nk > 1 else 0

    def body(*refs):
        own, ex_refs = hosted.split(refs, n_in, 1, n_scr)
        last = hosted.run(ex_refs, grid)
        a_ref, b_ref = own[0], own[1]
        c_ref = own[2] if has_add else None
        o_ref = own[n_in]
        part = jnp.dot(a_ref[...].astype(BF16), b_ref[...].astype(BF16), preferred_element_type=F32)
        if nk == 1:
            if has_add:
                part = part + c_ref[...]
            o_ref[...] = part.astype(o_ref.dtype)
        else:
            acc_ref = own[n_in + 1]
            k = pl.program_id(2)

            @pl.when(k == 0)
            def _():
                acc_ref[...] = part

            @pl.when(k > 0)
            def _():
                acc_ref[...] += part

            @pl.when(k == nk - 1)
            def _():
                r = acc_ref[...]
                if has_add:
                    r = r + c_ref[...]
                o_ref[...] = r.astype(o_ref.dtype)
        hosted.finish(ex_refs, last)

    in_specs = [pl.BlockSpec((tm, tk), lambda i, j, k: (i, k)), pl.BlockSpec((tk, tn), lambda i, j, k: (k, j))]
    args = [a, b]
    if has_add:
        in_specs.append(pl.BlockSpec((tm, tn), lambda i, j, k: (i, j)))
        args.append(add)
    ex_args, ex_in_specs, ex_out_shapes, ex_scratch = hosted.specs()
    outs = pl.pallas_call(
        body, name=name, out_shape=(jax.ShapeDtypeStruct((M, N), out_dtype), *ex_out_shapes),
        grid=grid, in_specs=in_specs + ex_in_specs,
        out_specs=(pl.BlockSpec((tm, tn), lambda i, j, k: (i, j)), *([_HBM] * len(ex_out_shapes))),
        scratch_shapes=([pltpu.VMEM((tm, tn), F32)] if nk > 1 else []) + ex_scratch,
        compiler_params=_params(("arbitrary",) * 3 if hosted.on else ("parallel", "parallel", "arbitrary")),
    )(*args, *ex_args)
    hosted.done(outs[1:])
    return outs[0]


def _mm_tn(a, b, *, name, hosts=None):
    T, M = a.shape
    T2, N = b.shape
    assert T == T2
    tm = _pick(M, 1536)
    tt = _pick(T, 2048)
    nt = T // tt
    for tn in _divisors128(N):
        if 2 * (tt * tm * a.dtype.itemsize + tt * tn * b.dtype.itemsize + tm * tn * 4) <= MM_VMEM_BUDGET:
            break
    grid = (M // tm, N // tn, nt)
    hosted = _Hosted(hosts, name)

    def body(*refs):
        (a_ref, b_ref, o_ref), ex_refs = hosted.split(refs, 2, 1, 0)
        last = hosted.run(ex_refs, grid)
        k = pl.program_id(2)
        part = lax.dot_general(a_ref[...].astype(BF16), b_ref[...].astype(BF16), _TN, preferred_element_type=F32)

        @pl.when(k == 0)
        def _():
            o_ref[...] = part

        @pl.when(k > 0)
        def _():
            o_ref[...] += part

        hosted.finish(ex_refs, last)

    ex_args, ex_in_specs, ex_out_shapes, ex_scratch = hosted.specs()
    outs = pl.pallas_call(
        body, name=name, out_shape=(jax.ShapeDtypeStruct((M, N), F32), *ex_out_shapes),
        grid=grid,
        in_specs=[pl.BlockSpec((tt, tm), lambda i, j, k: (k, i)), pl.BlockSpec((tt, tn), lambda i, j, k: (k, j))]
        + ex_in_specs,
        out_specs=(pl.BlockSpec((tm, tn), lambda i, j, k: (i, j)), *([_HBM] * len(ex_out_shapes))),
        scratch_shapes=ex_scratch,
        compiler_params=_params(("arbitrary",) * 3 if hosted.on else ("parallel", "parallel", "arbitrary")),
    )(a, b, *ex_args)
    hosted.done(outs[1:])
    return outs[0]


def _rms(x):
    r = lax.rsqrt(jnp.mean(x * x, axis=-1, keepdims=True) + NORM_EPS)
    return x * r, r


def _row_spec(cols, tr=ROW_TILE):
    return pl.BlockSpec((tr, cols), lambda i: (i, 0))


def _vec_spec(cols):
    return pl.BlockSpec((1, cols), lambda i: (0, 0))


def _norm_fwd(x, y, g_post, g_pre, *, name):
    T, D = x.shape
    has_post = y is not None
    has_pre = g_pre is not None

    def body(*refs):
        refs = list(refs)
        x_ref = refs.pop(0)
        xn = x_ref[...]
        if has_post:
            y_ref = refs.pop(0)
            gp_ref = refs.pop(0)
        if has_pre:
            gq_ref = refs.pop(0)
        if has_post:
            yh, _ = _rms(y_ref[...].astype(F32))
            xn = xn + yh * gp_ref[...]
            refs.pop(0)[...] = xn
        if has_pre:
            xh, _ = _rms(xn)
            refs.pop(0)[...] = (xh * gq_ref[...]).astype(BF16)

    args, in_specs, out_shape, out_specs = [x], [_row_spec(D)], [], []
    if has_post:
        args += [y, g_post.reshape(1, D)]
        in_specs += [_row_spec(D), _vec_spec(D)]
        out_shape.append(jax.ShapeDtypeStruct((T, D), F32))
        out_specs.append(_row_spec(D))
    if has_pre:
        args.append(g_pre.reshape(1, D))
        in_specs.append(_vec_spec(D))
        out_shape.append(jax.ShapeDtypeStruct((T, D), BF16))
        out_specs.append(_row_spec(D))
    return pl.pallas_call(body, name=name, out_shape=tuple(out_shape), grid=(T // ROW_TILE,), in_specs=in_specs,
                          out_specs=tuple(out_specs), compiler_params=_params(("parallel",)))(*args)


def _norm_bwd(dres, dh, x_new, g_pre, y, g_post, *, name):
    has_res = dres is not None
    has_pre = dh is not None
    has_post = y is not None
    out_d = has_pre and (has_res or has_post)
    T, D = (dres if has_res else dh).shape
    if not has_res:
        assert not has_post

    def body(*refs):
        refs = list(refs)
        i = pl.program_id(0)
        d = refs.pop(0)[...] if has_res else None
        if has_pre:
            dh_v = refs.pop(0)[...].astype(F32)
            xh, r = _rms(refs.pop(0)[...])
            gq = refs.pop(0)[...]
        if has_post:
            yh, ry = _rms(refs.pop(0)[...].astype(F32))
            gp = refs.pop(0)[...]
        if has_pre:
            dxh = dh_v * gq
            dpre = r * (dxh - xh * jnp.mean(dxh * xh, axis=-1, keepdims=True))
            d = dpre if d is None else d + dpre
            dgq = jnp.sum(dh_v * xh, axis=0, keepdims=True)
        if out_d:
            refs.pop(0)[...] = d
        if has_post:
            dyh = d * gp
            refs.pop(0)[...] = (ry * (dyh - yh * jnp.mean(dyh * yh, axis=-1, keepdims=True))).astype(BF16)
            dgp_ref = refs.pop(0)
            dgp = jnp.sum(d * yh, axis=0, keepdims=True)

            @pl.when(i == 0)
            def _():
                dgp_ref[...] = dgp

            @pl.when(i > 0)
            def _():
                dgp_ref[...] += dgp
        if has_pre:
            dgq_ref = refs.pop(0)

            @pl.when(i == 0)
            def _():
                dgq_ref[...] = dgq

            @pl.when(i > 0)
            def _():
                dgq_ref[...] += dgq

    args, in_specs, out_shape, out_specs = [], [], [], []
    if has_res:
        args.append(dres)
        in_specs.append(_row_spec(D))
    if has_pre:
        args += [dh, x_new, g_pre.reshape(1, D)]
        in_specs += [_row_spec(D), _row_spec(D), _vec_spec(D)]
    if has_post:
        args += [y, g_post.reshape(1, D)]
        in_specs += [_row_spec(D), _vec_spec(D)]
    if out_d:
        out_shape.append(jax.ShapeDtypeStruct((T, D), F32))
        out_specs.append(_row_spec(D))
    if has_post:
        out_shape += [jax.ShapeDtypeStruct((T, D), BF16), jax.ShapeDtypeStruct((1, D), F32)]
        out_specs += [_row_spec(D), _vec_spec(D)]
    if has_pre:
        out_shape.append(jax.ShapeDtypeStruct((1, D), F32))
        out_specs.append(_vec_spec(D))
    return pl.pallas_call(body, name=name, out_shape=tuple(out_shape), grid=(T // ROW_TILE,), in_specs=in_specs,
                          out_specs=tuple(out_specs), compiler_params=_params(("arbitrary",)))(*args)


def _loss_kernel(xf, target, *, name):
    T, D = xf.shape

    def body(x_ref, t_ref, loss_ref, d_ref):
        i = pl.program_id(0)
        e = x_ref[...] - t_ref[...]
        d_ref[...] = e * (1.0 / D)
        part = jnp.sum(jnp.sum(e * e, axis=0, keepdims=True), axis=1, keepdims=True) * (0.5 / D)
        part = jnp.broadcast_to(part, (1, LANE))

        @pl.when(i == 0)
        def _():
            loss_ref[...] = part

        @pl.when(i > 0)
        def _():
            loss_ref[...] += part

    return pl.pallas_call(body, name=name,
                          out_shape=(jax.ShapeDtypeStruct((1, LANE), F32), jax.ShapeDtypeStruct((T, D), F32)),
                          grid=(T // ROW_TILE,), in_specs=[_row_spec(D), _row_spec(D)],
                          out_specs=(_vec_spec(LANE), _row_spec(D)), compiler_params=_params(("arbitrary",)))(xf, target)


CONV_PAD = 32
CONV_CH = 64
SUBLANES = 8


def _shift_rows(win, r):
    return win if r == 0 else win[r:r + win.shape[0] - SUBLANES, :]


def _ln_stats(x):
    mu = jnp.mean(x, axis=-1, keepdims=True)
    xc = x - mu
    rstd = lax.rsqrt(jnp.mean(xc * xc, axis=-1, keepdims=True) + LN_EPS)
    return xc * rstd, rstd


def _acc_out(ref, val, first):
    @pl.when(first)
    def _():
        ref[...] = val

    @pl.when(jnp.logical_not(first))
    def _():
        ref[...] += val


def _bra_fwd(za, conv_w, conv_b, ln_g, ln_b, *, B, S, name):
    C = conv_w.shape[1]

    def body(za_ref, w_ref, cb_ref, g_ref, b_ref, a4_ref, a2_ref, pad_ref):
        pad_ref[0:CONV_PAD, :] = jnp.zeros((CONV_PAD, C), F32)
        pad_ref[CONV_PAD:, :] = za_ref[:, 0:C].astype(F32) * _sigmoid(za_ref[:, C:2 * C].astype(F32))

        def chunk(i, _):
            base = pl.multiple_of(i * CONV_CH, CONV_CH)
            win = pad_ref[pl.ds(base, CONV_CH + CONV_PAD), :]
            acc = jnp.broadcast_to(cb_ref[...], (CONV_CH, C))
            for r in range(SUBLANES):
                sh = _shift_rows(win, r)
                for j in range(CONV_K):
                    off = CONV_PAD - (CONV_K - 1) + j
                    if off % SUBLANES == r:
                        acc = acc + w_ref[j:j + 1, :] * sh[off - r:off - r + CONV_CH, :]
            a2_ref[pl.ds(base, CONV_CH), :] = acc
            xh, _ = _ln_stats(acc)
            a3 = xh * g_ref[...] + b_ref[...]
            a4_ref[pl.ds(base, CONV_CH), :] = (a3 * _sigmoid(a3)).astype(BF16)
            return 0

        lax.fori_loop(0, S // CONV_CH, chunk, 0)

    vec = pl.BlockSpec((1, C), lambda b: (0, 0))
    return pl.pallas_call(
        body, name=name,
        out_shape=(jax.ShapeDtypeStruct((B * S, C), BF16), jax.ShapeDtypeStruct((B * S, C), F32)),
        grid=(B,),
        in_specs=[pl.BlockSpec((S, 2 * C), lambda b: (b, 0)), pl.BlockSpec((CONV_K, C), lambda b: (0, 0)), vec, vec, vec],
        out_specs=(pl.BlockSpec((S, C), lambda b: (b, 0)), pl.BlockSpec((S, C), lambda b: (b, 0))),
        scratch_shapes=[pltpu.VMEM((S + CONV_PAD, C), F32)],
        compiler_params=_params(("parallel",)),
    )(za, conv_w, conv_b.reshape(1, C), ln_g.reshape(1, C), ln_b.reshape(1, C))


def _bra_bwd_ln(da4, a2, ln_g, ln_b, *, name):
    T, C = a2.shape

    def body(da4_ref, a2_ref, g_ref, b_ref, da2_ref, dg_ref, db_ref, dcb_ref):
        first = pl.program_id(0) == 0
        xh, rstd = _ln_stats(a2_ref[...])
        a3 = xh * g_ref[...] + b_ref[...]
        sg = _sigmoid(a3)
        da3 = da4_ref[...].astype(F32) * (sg * (1.0 + a3 * (1.0 - sg)))
        dxh = da3 * g_ref[...]
        da2 = rstd * (dxh - jnp.mean(dxh, axis=-1, keepdims=True) - xh * jnp.mean(dxh * xh, axis=-1, keepdims=True))
        da2_ref[...] = da2
        _acc_out(dg_ref, jnp.sum(da3 * xh, axis=0, keepdims=True), first)
        _acc_out(db_ref, jnp.sum(da3, axis=0, keepdims=True), first)
        _acc_out(dcb_ref, jnp.sum(da2, axis=0, keepdims=True), first)

    vec = _vec_spec(C)
    vshape = jax.ShapeDtypeStruct((1, C), F32)
    return pl.pallas_call(body, name=name, out_shape=(jax.ShapeDtypeStruct((T, C), F32), vshape, vshape, vshape),
                          grid=(T // ROW_TILE,), in_specs=[_row_spec(C), _row_spec(C), vec, vec],
                          out_specs=(_row_spec(C), vec, vec, vec),
                          compiler_params=_params(("arbitrary",)))(da4, a2, ln_g.reshape(1, C), ln_b.reshape(1, C))


def _bra_bwd_conv(da2, za, conv_w, *, B, S, name):
    C = conv_w.shape[1]
    SUB = 8

    def body(da2_ref, za_ref, w_ref, dza_ref, dw_ref, pad1_ref, pad2_ref, dwacc_ref):
        first = pl.program_id(0) == 0
        pad1_ref[0:CONV_PAD, :] = jnp.zeros((CONV_PAD, C), F32)
        pad1_ref[CONV_PAD:, :] = za_ref[:, 0:C].astype(F32) * _sigmoid(za_ref[:, C:2 * C].astype(F32))
        pad2_ref[0:S, :] = da2_ref[...]
        pad2_ref[S:, :] = jnp.zeros((CONV_PAD, C), F32)
        dwacc_ref[...] = jnp.zeros_like(dwacc_ref)

        def chunk(i, _):
            base = pl.multiple_of(i * CONV_CH, CONV_CH)
            rows = pl.ds(base, CONV_CH)
            win1 = pad1_ref[pl.ds(base, CONV_CH + CONV_PAD), :]
            win2 = pad2_ref[pl.ds(base, CONV_CH + CONV_PAD), :]
            d = win2[0:CONV_CH, :]
            da1 = jnp.zeros((CONV_CH, C), F32)
            for r in range(SUBLANES):
                sh1 = _shift_rows(win1, r)
                sh2 = _shift_rows(win2, r)
                for j in range(CONV_K):
                    off2 = CONV_K - 1 - j
                    if off2 % SUBLANES == r:
                        da1 = da1 + w_ref[j:j + 1, :] * sh2[off2 - r:off2 - r + CONV_CH, :]
                    off1 = CONV_PAD - (CONV_K - 1) + j
                    if off1 % SUBLANES == r:
                        prod = d * sh1[off1 - r:off1 - r + CONV_CH, :]
                        dwacc_ref[j * SUB:(j + 1) * SUB, :] += jnp.sum(prod.reshape(CONV_CH // SUB, SUB, C), axis=0)
            a_val = za_ref[rows, 0:C].astype(F32)
            sg = _sigmoid(za_ref[rows, C:2 * C].astype(F32))
            dza_ref[rows, 0:C] = (da1 * sg).astype(BF16)
            dza_ref[rows, C:2 * C] = (da1 * a_val * sg * (1.0 - sg)).astype(BF16)
            return 0

        lax.fori_loop(0, S // CONV_CH, chunk, 0)
        _acc_out(dw_ref, jnp.sum(dwacc_ref[...].reshape(CONV_K, SUB, C), axis=1), first)

    return pl.pallas_call(
        body, name=name,
        out_shape=(jax.ShapeDtypeStruct((B * S, 2 * C), BF16), jax.ShapeDtypeStruct((CONV_K, C), F32)),
        grid=(B,),
        in_specs=[pl.BlockSpec((S, C), lambda b: (b, 0)), pl.BlockSpec((S, 2 * C), lambda b: (b, 0)),
                  pl.BlockSpec((CONV_K, C), lambda b: (0, 0))],
        out_specs=(pl.BlockSpec((S, 2 * C), lambda b: (b, 0)), pl.BlockSpec((CONV_K, C), lambda b: (0, 0))),
        scratch_shapes=[pltpu.VMEM((S + CONV_PAD, C), F32), pltpu.VMEM((S + CONV_PAD, C), F32),
                        pltpu.VMEM((CONV_K * SUB, C), F32)],
        compiler_params=_params(("arbitrary",)),
    )(da2, za, conv_w)


SG_STEP = 4


def _tril_mask():
    r = lax.broadcasted_iota(jnp.int32, (SG_CHUNK, SG_CHUNK), 0)
    c = lax.broadcasted_iota(jnp.int32, (SG_CHUNK, SG_CHUNK), 1)
    return c <= r


def _brb_fwd(zb, ln_g, ln_b, w_s, b_s_t, *, name):
    T, C2 = zb.shape
    C = C2 // 2
    G = w_s.shape[0]
    GC = C // G

    def body(zb_ref, g_ref, b_ref, ws_ref, bs_ref, p_ref):
        z = _gelu(zb_ref[...].astype(F32))
        u = z[:, 0:C]
        xh, _ = _ln_stats(z[:, C:2 * C])
        v1 = (xh * g_ref[...] + b_ref[...]).astype(BF16)
        mask = _tril_mask()
        wss = [jnp.where(mask, ws_ref[gi], 0.0).astype(BF16) for gi in range(G)]
        for ci in range(SG_STEP):
            rows = slice(ci * SG_CHUNK, (ci + 1) * SG_CHUNK)
            outs = [jnp.dot(wss[gi], v1[rows, gi * GC:(gi + 1) * GC], preferred_element_type=F32) + bs_ref[:, gi:gi + 1]
                    for gi in range(G)]
            p_ref[rows, :] = (u[rows, :] * jnp.concatenate(outs, axis=1)).astype(BF16)

    tr = SG_STEP * SG_CHUNK
    return pl.pallas_call(
        body, name=name, out_shape=jax.ShapeDtypeStruct((T, C), BF16), grid=(T // tr,),
        in_specs=[_row_spec(C2, tr), _vec_spec(C), _vec_spec(C),
                  pl.BlockSpec((G, SG_CHUNK, SG_CHUNK), lambda i: (0, 0, 0)), pl.BlockSpec((SG_CHUNK, G), lambda i: (0, 0))],
        out_specs=_row_spec(C, tr), compiler_params=_params(("parallel",)),
    )(zb, ln_g.reshape(1, C), ln_b.reshape(1, C), w_s, b_s_t)


def _brb_bwd(dp, zb, ln_g, ln_b, w_s, b_s_t, *, name):
    T, C2 = zb.shape
    C = C2 // 2
    G = w_s.shape[0]
    GC = C // G

    def body(dp_ref, zb_ref, g_ref, b_ref, ws_ref, bs_ref, dzb_ref, dws_ref, dbs_ref, dg_ref, db_ref):
        first = pl.program_id(0) == 0
        z, dz = _gelu_and_grad(zb_ref[...].astype(F32))
        u = z[:, 0:C]
        xh, rstd = _ln_stats(z[:, C:2 * C])
        v1 = (xh * g_ref[...] + b_ref[...]).astype(BF16)
        dp_v = dp_ref[...].astype(F32)
        mask = _tril_mask()
        wss = [jnp.where(mask, ws_ref[gi], 0.0).astype(BF16) for gi in range(G)]
        dv2_all = dp_v * u
        dv2b_all = dv2_all.astype(BF16)
        v2_rows, dv1_rows = [], []
        dwss, dbss = [None] * G, [None] * G
        for ci in range(SG_STEP):
            rows = slice(ci * SG_CHUNK, (ci + 1) * SG_CHUNK)
            v2s, dv1s = [], []
            for gi in range(G):
                cols = slice(gi * GC, (gi + 1) * GC)
                v2s.append(jnp.dot(wss[gi], v1[rows, cols], preferred_element_type=F32) + bs_ref[:, gi:gi + 1])
                dv2b = dv2b_all[rows, cols]
                dbs = jnp.sum(dv2_all[rows, cols], axis=1, keepdims=True)
                dws = lax.dot_general(dv2b, v1[rows, cols], _NT, preferred_element_type=F32)
                dbss[gi] = dbs if dbss[gi] is None else dbss[gi] + dbs
                dwss[gi] = dws if dwss[gi] is None else dwss[gi] + dws
                dv1s.append(lax.dot_general(wss[gi], dv2b, _TN, preferred_element_type=F32))
            v2_rows.append(jnp.concatenate(v2s, axis=1))
            dv1_rows.append(jnp.concatenate(dv1s, axis=1))
        dwss = [jnp.where(mask, t, 0.0) for t in dwss]
        du = dp_v * jnp.concatenate(v2_rows, axis=0)
        dv1 = jnp.concatenate(dv1_rows, axis=0)
        dxh = dv1 * g_ref[...]
        dv0 = rstd * (dxh - jnp.mean(dxh, axis=-1, keepdims=True) - xh * jnp.mean(dxh * xh, axis=-1, keepdims=True))
        dzb_ref[:, 0:C] = (du * dz[:, 0:C]).astype(BF16)
        dzb_ref[:, C:2 * C] = (dv0 * dz[:, C:2 * C]).astype(BF16)
        _acc_out(dws_ref, jnp.stack(dwss, axis=0), first)
        _acc_out(dbs_ref, jnp.concatenate(dbss, axis=1), first)
        _acc_out(dg_ref, jnp.sum(dv1 * xh, axis=0, keepdims=True), first)
        _acc_out(db_ref, jnp.sum(dv1, axis=0, keepdims=True), first)

    wspec = pl.BlockSpec((G, SG_CHUNK, SG_CHUNK), lambda i: (0, 0, 0))
    bspec = pl.BlockSpec((SG_CHUNK, G), lambda i: (0, 0))
    return pl.pallas_call(
        body, name=name,
        out_shape=(jax.ShapeDtypeStruct((T, C2), BF16), jax.ShapeDtypeStruct((G, SG_CHUNK, SG_CHUNK), F32),
                   jax.ShapeDtypeStruct((SG_CHUNK, G), F32), jax.ShapeDtypeStruct((1, C), F32),
                   jax.ShapeDtypeStruct((1, C), F32)),
        grid=(T // (SG_STEP * SG_CHUNK),),
        in_specs=[_row_spec(C, SG_STEP * SG_CHUNK), _row_spec(C2, SG_STEP * SG_CHUNK), _vec_spec(C), _vec_spec(C), wspec,
                  bspec],
        out_specs=(_row_spec(C2, SG_STEP * SG_CHUNK), wspec, bspec, _vec_spec(C), _vec_spec(C)),
        compiler_params=_params(("arbitrary",)),
    )(dp, zb, ln_g.reshape(1, C), ln_b.reshape(1, C), w_s, b_s_t)


ATT_UNROLL = 3


def _each_class(d, fn):
    if d == 1:
        fn(0)
    else:
        def step(c, _):
            fn(c)
            return 0

        lax.fori_loop(0, d, step, 0, unroll=ATT_UNROLL + 1 if d % (ATT_UNROLL + 1) == 0 and d > ATT_UNROLL + 1 else 1)


def _attn_fwd(qkv_g, bias, *, name):
    B, d, L, _ = qkv_g.shape
    nb = L // ATT_BLOCK
    GCOL = GROUP_COLS

    def body(qkv_all, bias_ref, o_all, lse_all):
        _each_class(d, lambda c: one_class(qkv_all.at[c], bias_ref, o_all.at[c], lse_all.at[c]))

    def one_class(qkv_ref, bias_ref, o_ref, lse_ref):
        def blk(r0, first):
            nk = ATT_BLOCK if first else 2 * ATT_BLOCK
            k0 = r0 if first else r0 - ATT_BLOCK
            qb = qkv_ref[pl.ds(r0, ATT_BLOCK), 0:GCOL]
            kb = qkv_ref[pl.ds(k0, nk), GCOL:2 * GCOL]
            vb = qkv_ref[pl.ds(k0, nk), 2 * GCOL:3 * GCOL]
            outs, lses = [], []
            for h in range(HEADS_PER_GROUP):
                sl = slice(h * HEAD_DIM, (h + 1) * HEAD_DIM)
                bh = bias_ref[h, :, ATT_BLOCK:2 * ATT_BLOCK] if first else bias_ref[h]
                s = lax.dot_general(qb[:, sl], kb[:, sl], _NT, preferred_element_type=F32) * ATT_SCALE + bh
                m = jnp.max(s, axis=1, keepdims=True)
                e = jnp.exp(s - m)
                ssum = jnp.sum(e, axis=1, keepdims=True)
                outs.append(jnp.dot(e.astype(BF16), vb[:, sl], preferred_element_type=F32) / ssum)
                lses.append(jnp.broadcast_to(m + jnp.log(ssum), (ATT_BLOCK, HEAD_DIM)))
            o_ref[pl.ds(r0, ATT_BLOCK), :] = jnp.concatenate(outs, axis=1)
            lse_ref[pl.ds(r0, ATT_BLOCK), :] = jnp.concatenate(lses, axis=1)

        blk(0, True)
        if nb > 1:
            def loop(n, _):
                blk(pl.multiple_of(n * ATT_BLOCK, ATT_BLOCK), False)
                return 0

            lax.fori_loop(1, nb, loop, 0, unroll=ATT_UNROLL)

    spec = lambda cols: pl.BlockSpec((None, d, L, cols), lambda b: (b, 0, 0, 0))
    oshape = jax.ShapeDtypeStruct((B, d, L, GCOL), F32)
    return pl.pallas_call(
        body, name=name, out_shape=(oshape, oshape), grid=(B,),
        in_specs=[spec(3 * GCOL), pl.BlockSpec((HEADS_PER_GROUP, ATT_BLOCK, 2 * ATT_BLOCK), lambda b: (0, 0, 0))],
        out_specs=(spec(GCOL), spec(GCOL)), compiler_params=_params(("parallel",)),
    )(qkv_g, bias)


def _attn_bwd(qkv_g, doc_g, lse_g, dd_g, bias, *, name):
    B, d, L, _ = qkv_g.shape
    nb = L // ATT_BLOCK
    GCOL = GROUP_COLS

    def body(qkv_all, doc_all, lse_all, dd_all, bias_ref, dqkv_all, dbias_ref, dk_ref, dv_ref):
        @pl.when(pl.program_id(0) == 0)
        def _():
            dbias_ref[...] = jnp.zeros_like(dbias_ref)

        _each_class(d, lambda c: one_class(qkv_all.at[c], doc_all.at[c], lse_all.at[c], dd_all.at[c], bias_ref,
                                           dqkv_all.at[c], dbias_ref, dk_ref, dv_ref))

    def one_class(qkv_ref, doc_ref, lse_ref, dd_ref, bias_ref, dqkv_ref, dbias_ref, dk_ref, dv_ref):
        dk_ref[...] = jnp.zeros_like(dk_ref)
        dv_ref[...] = jnp.zeros_like(dv_ref)

        def blk(r0, first):
            nk = ATT_BLOCK if first else 2 * ATT_BLOCK
            k0 = r0 if first else r0 - ATT_BLOCK
            rows = pl.ds(r0, ATT_BLOCK)
            krows = pl.ds(k0, nk)
            qb = qkv_ref[rows, 0:GCOL]
            kb = qkv_ref[krows, GCOL:2 * GCOL]
            vb = qkv_ref[krows, 2 * GCOL:3 * GCOL]
            dob = doc_ref[rows, :].astype(BF16)
            lse = lse_ref[rows, :]
            dd = dd_ref[rows, :]
            dqs, dks, dvs = [], [], []
            for h in range(HEADS_PER_GROUP):
                sl = slice(h * HEAD_DIM, (h + 1) * HEAD_DIM)
                c0 = h * HEAD_DIM
                bh = bias_ref[h, :, ATT_BLOCK:2 * ATT_BLOCK] if first else bias_ref[h]
                s = lax.dot_general(qb[:, sl], kb[:, sl], _NT, preferred_element_type=F32) * ATT_SCALE + bh
                p = jnp.exp(s - lse[:, c0:c0 + 1])
                dp = lax.dot_general(dob[:, sl], vb[:, sl], _NT, preferred_element_type=F32)
                ds = p * (dp - dd[:, c0:c0 + 1])
                if first:
                    dbias_ref[h, :, ATT_BLOCK:2 * ATT_BLOCK] += ds
                else:
                    dbias_ref[h] += ds
                dsb = ds.astype(BF16)
                dqs.append(jnp.dot(dsb, kb[:, sl], preferred_element_type=F32) * ATT_SCALE)
                dks.append(lax.dot_general(dsb, qb[:, sl], _TN, preferred_element_type=F32) * ATT_SCALE)
                dvs.append(lax.dot_general(p.astype(BF16), dob[:, sl], _TN, preferred_element_type=F32))
            dqkv_ref[rows, 0:GCOL] = jnp.concatenate(dqs, axis=1).astype(BF16)
            dk_ref[krows, :] += jnp.concatenate(dks, axis=1)
            dv_ref[krows, :] += jnp.concatenate(dvs, axis=1)

        blk(0, True)
        if nb > 1:
            def loop(n, _):
                blk(pl.multiple_of(n * ATT_BLOCK, ATT_BLOCK), False)
                return 0

            lax.fori_loop(1, nb, loop, 0, unroll=ATT_UNROLL)
        dqkv_ref[:, GCOL:2 * GCOL] = dk_ref[...].astype(BF16)
        dqkv_ref[:, 2 * GCOL:3 * GCOL] = dv_ref[...].astype(BF16)

    spec = lambda cols: pl.BlockSpec((None, d, L, cols), lambda b: (b, 0, 0, 0))
    bspec = pl.BlockSpec((HEADS_PER_GROUP, ATT_BLOCK, 2 * ATT_BLOCK), lambda b: (0, 0, 0))
    return pl.pallas_call(
        body, name=name,
        out_shape=(jax.ShapeDtypeStruct((B, d, L, 3 * GCOL), BF16),
                   jax.ShapeDtypeStruct((HEADS_PER_GROUP, ATT_BLOCK, 2 * ATT_BLOCK), F32)),
        grid=(B,),
        in_specs=[spec(3 * GCOL), spec(GCOL), spec(GCOL), spec(GCOL), bspec],
        out_specs=(spec(3 * GCOL), bspec),
        scratch_shapes=[pltpu.VMEM((L, GCOL), F32), pltpu.VMEM((L, GCOL), F32)],
        compiler_params=_params(("arbitrary",)),
    )(qkv_g, doc_g, lse_g, dd_g, bias)


def _attn_combine(os_, lses, *, name):
    T, GC = os_[0].shape
    n = len(os_)

    def body(*refs):
        o_refs, l_refs, oc_ref, lt_ref = refs[:n], refs[n:2 * n], refs[2 * n], refs[2 * n + 1]
        ls = [r[...] for r in l_refs]
        m = functools.reduce(jnp.maximum, ls)
        ws = [jnp.exp(l - m) for l in ls]
        tot = functools.reduce(jnp.add, ws)
        acc = functools.reduce(jnp.add, [w * r[...] for w, r in zip(ws, o_refs)])
        oc_ref[...] = acc / tot
        lt_ref[...] = m + jnp.log(tot)

    shp = jax.ShapeDtypeStruct((T, GC), F32)
    return pl.pallas_call(body, name=name, out_shape=(shp, shp), grid=(T // ROW_TILE,),
                          in_specs=[_row_spec(GC)] * (2 * n), out_specs=(_row_spec(GC), _row_spec(GC)),
                          compiler_params=_params(("parallel",)))(*os_, *lses)


def _attn_rowdot(doc, oc, *, name):
    T, GC = oc.shape

    def body(doc_ref, oc_ref, dd_ref):
        prod = doc_ref[...].astype(F32) * oc_ref[...]
        parts = []
        for h in range(GC // HEAD_DIM):
            s = jnp.sum(prod[:, h * HEAD_DIM:(h + 1) * HEAD_DIM], axis=1, keepdims=True)
            parts.append(jnp.broadcast_to(s, (ROW_TILE, HEAD_DIM)))
        dd_ref[...] = jnp.concatenate(parts, axis=1)

    return pl.pallas_call(body, name=name, out_shape=jax.ShapeDtypeStruct((T, GC), F32), grid=(T // ROW_TILE,),
                          in_specs=[_row_spec(GC), _row_spec(GC)], out_specs=_row_spec(GC),
                          compiler_params=_params(("parallel",)))(doc, oc)


def _bucket_sum(dbias, buckets, *, name):
    depth, NH = dbias.shape[:2]

    def body(db_ref, bk_ref, out_ref):
        rows = lax.broadcasted_iota(jnp.int32, (REL_BUCKETS, LANE), 0)
        cols = lax.broadcasted_iota(jnp.int32, (REL_BUCKETS, LANE), 1)

        def per_bucket(b, acc):
            for h in range(NH):
                sel = bk_ref[h // HEADS_PER_GROUP] == b
                tot = functools.reduce(jnp.add, [db_ref[l, h] for l in range(depth)])
                s = jnp.sum(jnp.where(sel, tot, 0.0))
                acc = acc + jnp.where(jnp.logical_and(rows == b, cols == h), s, 0.0)
            return acc

        out_ref[...] = lax.fori_loop(0, REL_BUCKETS, per_bucket, jnp.zeros((REL_BUCKETS, LANE), F32))

    return pl.pallas_call(body, name=name, out_shape=jax.ShapeDtypeStruct((REL_BUCKETS, LANE), F32),
                          compiler_params=pltpu.CompilerParams(vmem_limit_bytes=VMEM_LIMIT))(dbias, buckets)


def _merge_fwd(zg, bg, ya, yb, yc, *, name):
    T, D3 = zg.shape
    D = D3 // 3

    def body(zg_ref, bg_ref, ya_ref, yb_ref, yc_ref, out_ref):
        acc = None
        for i, y_ref in enumerate((ya_ref, yb_ref, yc_ref)):
            g = _sigmoid(zg_ref[:, i * D:(i + 1) * D].astype(F32) + bg_ref[:, i * D:(i + 1) * D])
            t = g * y_ref[...].astype(F32)
            acc = t if acc is None else acc + t
        out_ref[...] = acc.astype(BF16)

    return pl.pallas_call(body, name=name, out_shape=jax.ShapeDtypeStruct((T, D), BF16), grid=(T // ROW_TILE,),
                          in_specs=[_row_spec(D3), _vec_spec(D3), _row_spec(D), _row_spec(D), _row_spec(D)],
                          out_specs=_row_spec(D), compiler_params=_params(("parallel",)))(zg, bg.reshape(1, D3), ya, yb, yc)


def _merge_bwd(dm, zg, bg, ya, yb, yc, *, name):
    T, D3 = zg.shape
    D = D3 // 3

    def body(dm_ref, zg_ref, bg_ref, ya_ref, yb_ref, yc_ref, dya_ref, dyb_ref, dyc_ref, dzg_ref, dbg_ref):
        first = pl.program_id(0) == 0
        dm_v = dm_ref[...].astype(F32)
        dbs = []
        for i, (y_ref, dy_ref) in enumerate(((ya_ref, dya_ref), (yb_ref, dyb_ref), (yc_ref, dyc_ref))):
            g = _sigmoid(zg_ref[:, i * D:(i + 1) * D].astype(F32) + bg_ref[:, i * D:(i + 1) * D])
            dy_ref[...] = (dm_v * g).astype(BF16)
            dz = dm_v * y_ref[...].astype(F32) * g * (1.0 - g)
            dzg_ref[:, i * D:(i + 1) * D] = dz.astype(BF16)
            dbs.append(jnp.sum(dz, axis=0, keepdims=True))
        _acc_out(dbg_ref, jnp.concatenate(dbs, axis=1), first)

    bshape = jax.ShapeDtypeStruct((T, D), BF16)
    return pl.pallas_call(
        body, name=name,
        out_shape=(bshape, bshape, bshape, jax.ShapeDtypeStruct((T, D3), BF16), jax.ShapeDtypeStruct((1, D3), F32)),
        grid=(T // ROW_TILE,),
        in_specs=[_row_spec(D), _row_spec(D3), _vec_spec(D3), _row_spec(D), _row_spec(D), _row_spec(D)],
        out_specs=(_row_spec(D), _row_spec(D), _row_spec(D), _row_spec(D3), _vec_spec(D3)),
        compiler_params=_params(("arbitrary",)))(dm, zg, bg.reshape(1, D3), ya, yb, yc)


XQ_TILE = 512


def _xattn_fwd(q, kv, *, B, S, name):
    D = q.shape[1]
    M = kv.shape[0] // B
    E = D // X_HEADS
    scale = E ** -0.5

    def body(q_ref, kv_ref, o_ref):
        outs = []
        for h in range(X_HEADS):
            s = lax.dot_general(q_ref[:, h * E:(h + 1) * E], kv_ref[:, h * E:(h + 1) * E], _NT,
                                preferred_element_type=F32) * scale
            e = jnp.exp(s - jnp.max(s, axis=1, keepdims=True))
            p = e / jnp.sum(e, axis=1, keepdims=True)
            outs.append(jnp.dot(p.astype(BF16), kv_ref[:, D + h * E:D + (h + 1) * E], preferred_element_type=F32))
        o_ref[...] = jnp.concatenate(outs, axis=1).astype(BF16)

    nq = S // XQ_TILE
    return pl.pallas_call(
        body, name=name, out_shape=jax.ShapeDtypeStruct((B * S, D), BF16), grid=(B, nq),
        in_specs=[pl.BlockSpec((XQ_TILE, D), lambda b, i: (b * nq + i, 0)), pl.BlockSpec((M, 2 * D), lambda b, i: (b, 0))],
        out_specs=pl.BlockSpec((XQ_TILE, D), lambda b, i: (b * nq + i, 0)),
        compiler_params=_params(("parallel", "parallel")))(q, kv)


def _xattn_bwd(q, kv, do, *, B, S, name):
    D = q.shape[1]
    M = kv.shape[0] // B
    E = D // X_HEADS
    scale = E ** -0.5

    def body(q_ref, kv_ref, do_ref, dq_ref, dkv_ref):
        first = pl.program_id(1) == 0
        dqs, dks, dvs = [], [], []
        for h in range(X_HEADS):
            qh = q_ref[:, h * E:(h + 1) * E]
            kh = kv_ref[:, h * E:(h + 1) * E]
            vh = kv_ref[:, D + h * E:D + (h + 1) * E]
            doh = do_ref[:, h * E:(h + 1) * E]
            s = lax.dot_general(qh, kh, _NT, preferred_element_type=F32) * scale
            e = jnp.exp(s - jnp.max(s, axis=1, keepdims=True))
            p = e / jnp.sum(e, axis=1, keepdims=True)
            dp = lax.dot_general(doh, vh, _NT, preferred_element_type=F32)
            ds = (p * (dp - jnp.sum(p * dp, axis=1, keepdims=True))).astype(BF16)
            dqs.append(jnp.dot(ds, kh, preferred_element_type=F32) * scale)
            dks.append(lax.dot_general(ds, qh, _TN, preferred_element_type=F32) * scale)
            dvs.append(lax.dot_general(p.astype(BF16), doh, _TN, preferred_element_type=F32))
        dq_ref[...] = jnp.concatenate(dqs, axis=1).astype(BF16)
        _acc_out(dkv_ref, jnp.concatenate(dks + dvs, axis=1), first)

    nq = S // XQ_TILE
    qspec = pl.BlockSpec((XQ_TILE, D), lambda b, i: (b * nq + i, 0))
    kvspec = pl.BlockSpec((M, 2 * D), lambda b, i: (b, 0))
    return pl.pallas_call(
        body, name=name,
        out_shape=(jax.ShapeDtypeStruct((B * S, D), BF16), jax.ShapeDtypeStruct((B * M, 2 * D), F32)),
        grid=(B, nq), in_specs=[qspec, kvspec, qspec], out_specs=(qspec, kvspec),
        compiler_params=_params(("arbitrary", "arbitrary")))(q, kv, do)


FFN_COLS = 256
FFN_PAD = 8
FFN_CH = 256
FFN_K = 3


def _ffn_gate(win, w_ref, cb_ref, n):
    g = jnp.broadcast_to(cb_ref[...], (n, win.shape[1]))
    for j in range(FFN_K):
        off = FFN_PAD - (FFN_K - 1) + j
        g = g + w_ref[j:j + 1, :] * win[off:off + n, :]
    return g


def _ffn_interleave(w):
    lead, n2 = w.shape[:-1], w.shape[-1]
    nc = n2 // (2 * FFN_COLS)
    return jnp.swapaxes(w.reshape(lead + (2, nc, FFN_COLS)), -3, -2).reshape(lead + (n2,))


def _ffn_deinterleave(w):
    lead, n2 = w.shape[:-1], w.shape[-1]
    nc = n2 // (2 * FFN_COLS)
    return jnp.swapaxes(w.reshape(lead + (nc, 2, FFN_COLS)), -3, -2).reshape(lead + (n2,))


def _ffn_fwd(up, conv_w, conv_b, *, B, S, name):
    F = conv_w.shape[1]
    nc = F // FFN_COLS

    def body(gp_ref, val_ref, w_ref, cb_ref, act_ref, pad_ref):
        pad_ref[0:FFN_PAD, :] = jnp.zeros((FFN_PAD, FFN_COLS), F32)
        pad_ref[FFN_PAD:, :] = gp_ref[...].astype(F32)

        def chunk(i, _):
            base = pl.multiple_of(i * FFN_CH, FFN_CH)
            gate = _ffn_gate(pad_ref[pl.ds(base, FFN_CH + FFN_PAD), :], w_ref, cb_ref, FFN_CH)
            act_ref[pl.ds(base, FFN_CH), :] = (_gelu(gate) * val_ref[pl.ds(base, FFN_CH), :].astype(F32)).astype(BF16)
            return 0

        lax.fori_loop(0, S // FFN_CH, chunk, 0)

    return pl.pallas_call(
        body, name=name, out_shape=jax.ShapeDtypeStruct((B * S, F), BF16), grid=(B, nc),
        in_specs=[pl.BlockSpec((S, FFN_COLS), lambda b, j: (b, 2 * j)), pl.BlockSpec((S, FFN_COLS), lambda b, j: (b, 2 * j + 1)),
                  pl.BlockSpec((FFN_K, FFN_COLS), lambda b, j: (0, j)), pl.BlockSpec((1, FFN_COLS), lambda b, j: (0, j))],
        out_specs=pl.BlockSpec((S, FFN_COLS), lambda b, j: (b, j)),
        scratch_shapes=[pltpu.VMEM((S + FFN_PAD, FFN_COLS), F32)],
        compiler_params=_params(("parallel", "parallel")))(up, up, conv_w, conv_b.reshape(1, F))


def _ffn_bwd(dact, up, conv_w, conv_b, *, B, S, name):
    F = conv_w.shape[1]
    nc = F // FFN_COLS
    SUB = 8

    def body(dact_ref, gp_ref, val_ref, w_ref, cb_ref, dup_ref, dw_ref, dcb_ref, pad_ref, pad2_ref, acc_ref):
        first = pl.program_id(1) == 0
        pad_ref[0:FFN_PAD, :] = jnp.zeros((FFN_PAD, FFN_COLS), F32)
        pad_ref[FFN_PAD:, :] = gp_ref[...].astype(F32)
        pad2_ref[S:, :] = jnp.zeros((FFN_PAD, FFN_COLS), F32)
        acc_ref[...] = jnp.zeros_like(acc_ref)

        def chunk1(i, _):
            base = pl.multiple_of(i * FFN_CH, FFN_CH)
            rows = pl.ds(base, FFN_CH)
            gate = _ffn_gate(pad_ref[pl.ds(base, FFN_CH + FFN_PAD), :], w_ref, cb_ref, FFN_CH)
            gl, dgl = _gelu_and_grad(gate)
            da = dact_ref[rows, :].astype(F32)
            dup_ref[rows, FFN_COLS:2 * FFN_COLS] = (da * gl).astype(BF16)
            pad2_ref[rows, :] = da * val_ref[rows, :].astype(F32) * dgl
            return 0

        lax.fori_loop(0, S // FFN_CH, chunk1, 0)

        def chunk2(i, _):
            base = pl.multiple_of(i * FFN_CH, FFN_CH)
            rows = pl.ds(base, FFN_CH)
            win2 = pad2_ref[pl.ds(base, FFN_CH + FFN_PAD), :]
            win1 = pad_ref[pl.ds(base, FFN_CH + FFN_PAD), :]
            dg = win2[0:FFN_CH, :]
            dgp = jnp.zeros((FFN_CH, FFN_COLS), F32)
            for j in range(FFN_K):
                off2 = FFN_K - 1 - j
                dgp = dgp + w_ref[j:j + 1, :] * win2[off2:off2 + FFN_CH, :]
                off1 = FFN_PAD - (FFN_K - 1) + j
                prod = dg * win1[off1:off1 + FFN_CH, :]
                acc_ref[j * SUB:(j + 1) * SUB, :] += jnp.sum(prod.reshape(FFN_CH // SUB, SUB, FFN_COLS), axis=0)
            acc_ref[FFN_K * SUB:(FFN_K + 1) * SUB, :] += jnp.sum(dg.reshape(FFN_CH // SUB, SUB, FFN_COLS), axis=0)
            dup_ref[rows, 0:FFN_COLS] = dgp.astype(BF16)
            return 0

        lax.fori_loop(0, S // FFN_CH, chunk2, 0)
        sums = jnp.sum(acc_ref[...].reshape(FFN_K + 1, SUB, FFN_COLS), axis=1)
        _acc_out(dw_ref, sums[0:FFN_K, :], first)
        _acc_out(dcb_ref, sums[FFN_K:FFN_K + 1, :], first)

    return pl.pallas_call(
        body, name=name,
        out_shape=(jax.ShapeDtypeStruct((B * S, 2 * F), BF16),
                   jax.ShapeDtypeStruct((FFN_K, F), F32), jax.ShapeDtypeStruct((1, F), F32)),
        grid=(nc, B),
        in_specs=[pl.BlockSpec((S, FFN_COLS), lambda j, b: (b, j)), pl.BlockSpec((S, FFN_COLS), lambda j, b: (b, 2 * j)),
                  pl.BlockSpec((S, FFN_COLS), lambda j, b: (b, 2 * j + 1)),
                  pl.BlockSpec((FFN_K, FFN_COLS), lambda j, b: (0, j)), pl.BlockSpec((1, FFN_COLS), lambda j, b: (0, j))],
        out_specs=(pl.BlockSpec((S, 2 * FFN_COLS), lambda j, b: (b, j)),
                   pl.BlockSpec((FFN_K, FFN_COLS), lambda j, b: (0, j)), pl.BlockSpec((1, FFN_COLS), lambda j, b: (0, j))),
        scratch_shapes=[pltpu.VMEM((S + FFN_PAD, FFN_COLS), F32), pltpu.VMEM((S + FFN_PAD, FFN_COLS), F32),
                        pltpu.VMEM(((FFN_K + 1) * SUB, FFN_COLS), F32)],
        compiler_params=_params(("arbitrary", "arbitrary")))(dact, up, up, conv_w, conv_b.reshape(1, F))


def _row_tile(rows, row_bytes, budget):
    tr = rows
    if rows * row_bytes > budget:
        for t in range(16, rows, 16):
            if rows % t == 0 and t * row_bytes <= budget:
                tr = t
    return tr


def _adamw(w, parts, m, v, *, name, own=None, own_slot=None):
    shape = w.shape
    L = len(parts)
    n = parts[0].shape[0]
    cols = shape[-1]
    rows = w.size // (cols * L)
    w3, m3, v3 = (t.reshape(L, rows, cols) for t in (w, m, v))
    tr = _row_tile(rows, cols * 4, 1 << 19)
    c1 = 1.0 - ADAM_B1 ** ADAM_STEP
    c2 = 1.0 - ADAM_B2 ** ADAM_STEP
    has_own = own is not None

    def body(*refs):
        refs = list(refs)
        slot_ref = refs.pop(0) if has_own else None
        w_ref = refs.pop(0)
        p_refs = [refs.pop(0) for _ in range(L)]
        o_refs = [refs.pop(0) for _ in range(L)] if has_own else None
        m_ref, v_ref, g_ref, d_ref, mo_ref, vo_ref = refs
        layer = pl.program_id(0)
        gv = None
        for j in range(L):
            gj = None
            for i in range(n):
                t = p_refs[j][i].astype(F32)
                if has_own:
                    t = jnp.where(slot_ref[0] == i, o_refs[j][i].astype(F32), t)
                gj = t if gj is None else gj + t
            gv = gj if gv is None else jnp.where(layer == j, gj, gv)
        g_ref[...] = gv
        mn = ADAM_B1 * m_ref[...] + (1.0 - ADAM_B1) * gv
        vn = ADAM_B2 * v_ref[...] + (1.0 - ADAM_B2) * (gv * gv)
        d_ref[...] = -ADAM_LR * ((mn / c1) / (jnp.sqrt(vn / c2) + ADAM_EPS) + ADAM_WD * w_ref[...])
        mo_ref[...] = mn
        vo_ref[...] = vn

    spec = pl.BlockSpec((None, tr, cols), lambda l, i, *_: (l, i, 0))

    def pspec(j):
        return pl.BlockSpec((n, tr, cols), lambda l, i, *_: (0, jnp.where(l == j, i, 0), 0))

    oshape = jax.ShapeDtypeStruct((L, rows, cols), F32)
    in_specs = [spec] + [pspec(j) for j in range(L)] * (2 if has_own else 1) + [spec, spec]
    args = [w3] + [p.reshape(n, rows, cols) for p in parts]
    if has_own:
        args += [o.reshape(n, rows, cols) for o in own]
    args += [m3, v3]
    grid = (L, rows // tr)
    if has_own:
        grid_spec = pltpu.PrefetchScalarGridSpec(num_scalar_prefetch=1, grid=grid, in_specs=in_specs,
                                                 out_specs=(spec,) * 4)
        outs = pl.pallas_call(body, name=name, out_shape=(oshape,) * 4, grid_spec=grid_spec,
                              compiler_params=_params(("parallel", "parallel")))(own_slot, *args)
    else:
        outs = pl.pallas_call(body, name=name, out_shape=(oshape,) * 4, grid=grid, in_specs=in_specs,
                              out_specs=(spec,) * 4, compiler_params=_params(("parallel", "parallel")))(*args)
    return tuple(t.reshape(shape) for t in outs)


def _chip_sum(g, got, core, *, name):
    shard = got.shape[1:]
    cols = shard[-1]
    rows = math.prod(shard) // cols
    tr = _row_tile(rows, cols * 4, 1 << 20)

    def body(c_ref, g_ref, r_ref, o_ref):
        o_ref[...] = (g_ref[...].astype(F32) + r_ref[...].astype(F32)).astype(o_ref.dtype)

    blk = (None, tr, cols)
    grid_spec = pltpu.PrefetchScalarGridSpec(
        num_scalar_prefetch=1, grid=(N_CHIP, rows // tr),
        in_specs=[pl.BlockSpec(blk, lambda s, i, c: (2 * s + c[0], i, 0)), pl.BlockSpec(blk, lambda s, i, c: (s, i, 0))],
        out_specs=pl.BlockSpec(blk, lambda s, i, c: (s, i, 0)))
    out = pl.pallas_call(body, name=name, out_shape=jax.ShapeDtypeStruct((N_CHIP, rows, cols), g.dtype),
                         grid_spec=grid_spec, compiler_params=_params(("parallel", "parallel")),
                         )(core, g.reshape(N_DEV, rows, cols), got.reshape(N_CHIP, rows, cols))
    return out.reshape((N_CHIP,) + shard)


_HBM = pl.BlockSpec(memory_space=pltpu.HBM)
_MESH = pl.DeviceIdType.MESH


def _my_pos():
    return lax.axis_index("x"), lax.axis_index("y"), lax.axis_index("c")


def _flip(pos, k):
    x, y, c = pos
    fx, fy, fc = (k >> 2) & 1, (k >> 1) & 1, k & 1
    return (x ^ fx if fx else x, y ^ fy if fy else y, c ^ fc if fc else c)


def _index_of(pos):
    return 4 * pos[0] + 2 * pos[1] + pos[2]


def _all_gather(xs, *, name):
    n = len(xs)

    def body(*refs):
        x_refs, out_refs = refs[:n], refs[n:2 * n]
        send_sems, recv_sems = refs[2 * n:]
        me = _my_pos()
        sibling = _flip(me, 1)
        chips = [2, 4, 6]

        def copy(i, k, block_pos, to, from_x=False):
            blk = out_refs[i].at[_index_of(block_pos)]
            return pltpu.make_async_remote_copy(src_ref=x_refs[i] if from_x else blk, dst_ref=blk,
                                                send_sem=send_sems.at[k, i], recv_sem=recv_sems.at[k, i],
                                                device_id=to, device_id_type=_MESH)

        first = [copy(i, 1 + j, me, _flip(me, f), from_x=True) for j, f in enumerate(chips) for i in range(n)]
        first += [copy(i, 0, me, sibling, from_x=True) for i in range(n)]
        for cp in first:
            cp.start()
        passed = []
        for j, f in enumerate(chips):
            for i in range(n):
                copy(i, 1 + j, _flip(me, f), me).wait_recv()
                cp = copy(i, 4 + j, _flip(me, f), sibling)
                cp.start()
                passed.append(cp)
        for i in range(n):
            copy(i, 0, sibling, me).wait_recv()
        for j, f in enumerate(chips):
            for i in range(n):
                copy(i, 4 + j, _flip(sibling, f), me).wait_recv()
        for cp in first + passed:
            cp.wait_send()

    return pl.pallas_call(
        body, name=name, out_shape=tuple(jax.ShapeDtypeStruct((N_DEV,) + x.shape, x.dtype) for x in xs),
        in_specs=[_HBM] * n, out_specs=(_HBM,) * n,
        scratch_shapes=[pltpu.SemaphoreType.DMA((7, n)), pltpu.SemaphoreType.DMA((7, n))],
    )(*xs)


N_CHIP = 4


def _scatter_d2d(gs):
    n = len(gs)

    def make(g_refs, recv_refs, send_sems, recv_sems):
        me = _my_pos()
        sibling = _flip(me, 1)
        c = me[2]
        sends = []
        for i in range(n):
            for s in range(N_CHIP):
                sends.append(pltpu.make_async_remote_copy(
                    src_ref=g_refs[i].at[2 * s + 1 - c], dst_ref=recv_refs[i].at[s],
                    send_sem=send_sems.at[N_CHIP * i + s], recv_sem=recv_sems.at[N_CHIP * i + s],
                    device_id=sibling, device_id_type=_MESH))
        return sends, sends

    shapes = [jax.ShapeDtypeStruct((N_CHIP,) + g.shape[1:], g.dtype) for g in gs]
    return _Exchange(list(gs), shapes, N_CHIP * n, make)


def _scatter_ici(ss):
    n = len(ss)

    def make(s_refs, r_refs, send_sems, recv_sems):
        me = _my_pos()
        my_chip = 2 * me[0] + me[1]
        sends, recvs = [], []
        for k in (1, 2, 3):
            peer = _flip(me, 2 * k)
            peer_chip = 2 * peer[0] + peer[1]
            for i in range(n):
                j = 3 * i + k - 1
                sends.append(pltpu.make_async_remote_copy(
                    src_ref=s_refs[i].at[peer_chip], dst_ref=r_refs[i].at[my_chip], send_sem=send_sems.at[j],
                    recv_sem=recv_sems.at[j], device_id=peer, device_id_type=_MESH))
                recvs.append(pltpu.make_async_remote_copy(
                    src_ref=s_refs[i].at[my_chip], dst_ref=r_refs[i].at[peer_chip], send_sem=send_sems.at[j],
                    recv_sem=recv_sems.at[j], device_id=peer, device_id_type=_MESH))
        return sends, recvs

    return _Exchange(list(ss), [jax.ShapeDtypeStruct(s.shape, s.dtype) for s in ss], 3 * n, make)


def _gather_ici(shards, layers):
    n = len(shards)
    FLIPS = (2, 4, 6, 1)

    def make(x_refs, out_refs, send_sems, recv_sems):
        me = _my_pos()
        sends, recvs = [], []
        for k, f in enumerate(FLIPS):
            peer = _flip(me, f)
            for i in range(n):
                src = x_refs[i] if layers[i] is None else x_refs[i].at[layers[i]]
                j = len(FLIPS) * i + k
                sends.append(pltpu.make_async_remote_copy(
                    src_ref=src, dst_ref=out_refs[i].at[_index_of(me)], send_sem=send_sems.at[j],
                    recv_sem=recv_sems.at[j], device_id=peer, device_id_type=_MESH))
                recvs.append(pltpu.make_async_remote_copy(
                    src_ref=src, dst_ref=out_refs[i].at[_index_of(peer)], send_sem=send_sems.at[j],
                    recv_sem=recv_sems.at[j], device_id=peer, device_id_type=_MESH))
        return sends, recvs

    shapes = [jax.ShapeDtypeStruct((N_DEV,) + (x.shape if l is None else x.shape[1:]), x.dtype)
              for x, l in zip(shards, layers)]
    return _Exchange(list(shards), shapes, len(FLIPS) * n, make)


def _gather_d2d(blocks):
    n = len(blocks)
    FLIPS = (2, 4, 6)

    def make(in_refs, out_refs, send_sems, recv_sems):
        me = _my_pos()
        sibling = _flip(me, 1)
        sends, recvs = [], []
        for k, f in enumerate(FLIPS):
            for i in range(n):
                j = len(FLIPS) * i + k
                mine = out_refs[i].at[_index_of(_flip(me, f))]
                sends.append(pltpu.make_async_remote_copy(
                    src_ref=mine, dst_ref=mine, send_sem=send_sems.at[j], recv_sem=recv_sems.at[j],
                    device_id=sibling, device_id_type=_MESH))
                recvs.append(pltpu.make_async_remote_copy(
                    src_ref=mine, dst_ref=out_refs[i].at[_index_of(_flip(sibling, f))], send_sem=send_sems.at[j],
                    recv_sem=recv_sems.at[j], device_id=sibling, device_id_type=_MESH))
        return sends, recvs

    shapes = [jax.ShapeDtypeStruct(b.shape, b.dtype) for b in blocks]
    return _Exchange(list(blocks), shapes, len(FLIPS) * n, make, aliases={i: i for i in range(n)})


class _Exchange:
    def __init__(self, ins, out_shapes, n_copies, make, aliases=None):
        self.ins, self.out_shapes, self.n_copies, self.make, self.aliases = ins, out_shapes, n_copies, make, aliases or {}


def _exchange_call(ex, *, name):
    n_in, n_out = len(ex.ins), len(ex.out_shapes)

    def body(*refs):
        sends, recvs = ex.make(refs[:n_in], refs[n_in:n_in + n_out], refs[n_in + n_out], refs[n_in + n_out + 1])
        for cp in sends:
            cp.start()
        for cp in recvs:
            cp.wait_recv()
        for cp in sends:
            cp.wait_send()

    return pl.pallas_call(
        body, name=name, out_shape=tuple(ex.out_shapes), in_specs=[_HBM] * n_in, out_specs=(_HBM,) * n_out,
        scratch_shapes=[pltpu.SemaphoreType.DMA((ex.n_copies,)), pltpu.SemaphoreType.DMA((ex.n_copies,))],
        input_output_aliases=ex.aliases,
    )(*ex.ins)


class _Hosts:
    def __init__(self):
        self.plan = {}

    def add(self, name, build, done):
        assert name not in self.plan
        self.plan[name] = (build, done)

    def get(self, name):
        return self.plan[name][0]() if name in self.plan else None

    def put(self, name, outs):
        self.plan.pop(name)[1](outs)


def _t5_bucket(dist):
    n = jnp.maximum(dist, 0)
    max_exact = REL_BUCKETS // 2
    nf = jnp.maximum(n, 1).astype(F32)
    large = max_exact + (jnp.log(nf / max_exact) / math.log(REL_MAX_DIST / max_exact)
                         * (REL_BUCKETS - max_exact)).astype(jnp.int32)
    large = jnp.minimum(large, REL_BUCKETS - 1)
    return jnp.where(n < max_exact, n, large)


def _bias_tables(rel_bias):
    qi = jnp.arange(ATT_BLOCK)[:, None]
    ki = jnp.arange(2 * ATT_BLOCK)[None, :]
    rel = qi + ATT_BLOCK - ki
    out = []
    for gi, (window, dil) in enumerate(DIL_GROUPS):
        span = window // dil
        bucket = _t5_bucket(rel * dil)
        valid = (rel >= 0) & (rel <= span)
        rb = rel_bias[:, gi * HEADS_PER_GROUP:(gi + 1) * HEADS_PER_GROUP]
        tab = functools.reduce(jnp.add, [jnp.where((bucket == b)[:, :, None], rb[b], 0.0)
                                         for b in range(REL_BUCKETS)])
        tab = jnp.where(valid[:, :, None], tab, MASK_VALUE).transpose(2, 0, 1)
        out.append((tab.astype(F32), bucket.astype(jnp.int32)))
    return out


def _regroup(t, B, S, d):
    C = t.shape[-1]
    return t.reshape(B, S // d, d, C).swapaxes(1, 2)


def _ungroup(t):
    B, d, L, C = t.shape
    return t.swapaxes(1, 2).reshape(B * L * d, C)


def _group_qkv(qkv, gi):
    n = len(DIL_GROUPS) * GROUP_COLS
    return jnp.concatenate([qkv[:, j * n + gi * GROUP_COLS: j * n + (gi + 1) * GROUP_COLS] for j in range(3)], axis=1)


def _layer_fwd(l, x0, h1, mem2, W, P, tabs, next_pre_g, *, B, S, hosts=None):
    tag = f"l{l}_"
    mm = functools.partial(_mm, hosts=hosts)
    sv = {"x0": x0, "h1": h1}
    za = mm(h1, W["in_a"], out_dtype=ACT, name=tag + "mm_in_a")
    zb = mm(h1, W["in_b"], out_dtype=ACT, name=tag + "mm_in_b")
    qkv = mm(h1, W["in_c"], out_dtype=BF16, name=tag + "mm_in_c")
    zg = mm(h1, W["in_g"], out_dtype=ACT, name=tag + "mm_in_g")
    a4, a2 = _bra_fwd(za, P["conv_a_w"], P["conv_a_b"], P["ln_a_g"], P["ln_a_b"], B=B, S=S, name=tag + "bra_fwd")
    ya = mm(a4, W["a_out"], out_dtype=ACT, name=tag + "mm_a_out")
    b_s_t = P["b_s"].T
    p = _brb_fwd(zb, P["ln_b_g"], P["ln_b_b"], P["w_s"], b_s_t, name=tag + "brb_fwd")
    yb = mm(p, W["b_out"], out_dtype=ACT, name=tag + "mm_b_out")
    os_, lses, qkv_gs = [], [], []
    for gi, (_, dil) in enumerate(DIL_GROUPS):
        qkv_g = _regroup(_group_qkv(qkv, gi), B, S, dil)
        o_g, lse_g = _attn_fwd(qkv_g, tabs[gi][0], name=tag + f"attn_fwd{gi}")
        qkv_gs.append(qkv_g)
        os_.append(_ungroup(o_g))
        lses.append(_ungroup(lse_g))
    oc, lse_tot = _attn_combine(os_, lses, name=tag + "attn_combine")
    yc = mm(oc, W["c_out"], out_dtype=ACT, name=tag + "mm_c_out")
    merged = _merge_fwd(zg, P["b_gate"], ya, yb, yc, name=tag + "merge_fwd")
    y1 = mm(merged, W["mix_out"], out_dtype=ACT, name=tag + "mm_mix")
    x1, h2 = _norm_fwd(x0, y1, P["mix_post_g"], P["x_pre_g"], name=tag + "norm1")
    q = mm(h2, W["xq"], out_dtype=BF16, name=tag + "mm_xq")
    (memn,) = _norm_fwd(mem2, None, None, P["mem_g"], name=tag + "norm_mem")
    kv = mm(memn, W["xkv"], out_dtype=BF16, name=tag + "mm_xkv")
    ox = _xattn_fwd(q, kv, B=B, S=S, name=tag + "xattn_fwd")
    y2 = mm(ox, W["xo"], out_dtype=ACT, name=tag + "mm_xo")
    x2, h3 = _norm_fwd(x1, y2, P["x_post_g"], P["ffn_pre_g"], name=tag + "norm2")
    up = mm(h3, W["up"], out_dtype=ACT, name=tag + "mm_up")
    act = _ffn_fwd(up, P["conv_f_w"], P["conv_f_b"], B=B, S=S, name=tag + "ffn_fwd")
    y3 = mm(act, W["down"], out_dtype=ACT, name=tag + "mm_down")
    outs = _norm_fwd(x2, y3, P["ffn_post_g"], next_pre_g, name=tag + "norm3")
    x3 = outs[0]
    h_next = outs[1] if next_pre_g is not None else None
    sv.update(za=za, zb=zb, zg=zg, a4=a4, a2=a2, ya=ya, p=p, yb=yb, qkv_gs=qkv_gs, oc=oc, lse_tot=lse_tot, yc=yc,
              merged=merged, y1=y1, x1=x1, h2=h2, q=q, memn=memn, kv=kv, ox=ox, y2=y2, x2=x2, h3=h3, up=up, act=act,
              y3=y3, x3=x3, b_s_t=b_s_t)
    return x3, h_next, sv


def _layer_bwd(l, d, dh_next, next_pre_g, sv, mem2, W, WT, P, tabs, G, *, B, S, hosts=None):
    tag = f"l{l}_"
    mm = functools.partial(_mm, hosts=hosts)
    mm_tn = functools.partial(_mm_tn, hosts=hosts)
    if dh_next is not None:
        d, dy3, G["ffn_post_g"], dg_next = _norm_bwd(d, dh_next, sv["x3"], next_pre_g, sv["y3"], P["ffn_post_g"],
                                                     name=tag + "norm3_bwd")
    else:
        dy3, G["ffn_post_g"] = _norm_bwd(d, None, None, None, sv["y3"], P["ffn_post_g"], name=tag + "norm3_bwd")
        dg_next = None
    dact = mm(dy3, WT["down"], out_dtype=ACT, name=tag + "mm_down_dx")
    G["w_down"] = mm_tn(sv["act"], dy3, name=tag + "mm_down_dw")
    dup, G["conv_f_w"], G["conv_f_b"] = _ffn_bwd(dact, sv["up"], P["conv_f_w"], P["conv_f_b"], B=B, S=S,
                                                name=tag + "ffn_bwd")
    dh3 = mm(dup, WT["up"], out_dtype=ACT, name=tag + "mm_up_dx")
    G["w_up"] = _ffn_deinterleave(mm_tn(sv["h3"], dup, name=tag + "mm_up_dw"))
    d, dy2, G["x_post_g"], G["ffn_pre_g"] = _norm_bwd(d, dh3, sv["x2"], P["ffn_pre_g"], sv["y2"], P["x_post_g"],
                                                      name=tag + "norm2_bwd")
    dox = mm(dy2, WT["xo"], out_dtype=BF16, name=tag + "mm_xo_dx")
    G["w_xo"] = mm_tn(sv["ox"], dy2, name=tag + "mm_xo_dw")
    dq, dkv = _xattn_bwd(sv["q"], sv["kv"], dox, B=B, S=S, name=tag + "xattn_bwd")
    dh2 = mm(dq, WT["xq"], out_dtype=ACT, name=tag + "mm_xq_dx")
    G["w_xq"] = mm_tn(sv["h2"], dq, name=tag + "mm_xq_dw")
    G["w_xkv"] = mm_tn(sv["memn"], dkv, name=tag + "mm_xkv_dw")
    dmemn = mm(dkv, WT["xkv"], out_dtype=F32, name=tag + "mm_xkv_dx")
    (G["mem_g"],) = _norm_bwd(None, dmemn, mem2, P["mem_g"], None, None, name=tag + "norm_mem_bwd")
    d, dy1, G["mix_post_g"], G["x_pre_g"] = _norm_bwd(d, dh2, sv["x1"], P["x_pre_g"], sv["y1"], P["mix_post_g"],
                                                      name=tag + "norm1_bwd")
    dm = mm(dy1, WT["mix_out"], out_dtype=ACT, name=tag + "mm_mix_dx")
    G["w_mix_out"] = mm_tn(sv["merged"], dy1, name=tag + "mm_mix_dw")
    dya, dyb, dyc, dzg, dbg = _merge_bwd(dm, sv["zg"], P["b_gate"], sv["ya"], sv["yb"], sv["yc"], name=tag + "merge_bwd")
    G["b_gate"] = dbg.reshape(P["b_gate"].shape)
    da4 = mm(dya, WT["a_out"], out_dtype=ACT, name=tag + "mm_a_out_dx")
    G["w_a_out"] = mm_tn(sv["a4"], dya, name=tag + "mm_a_out_dw")
    da2, G["ln_a_g"], G["ln_a_b"], G["conv_a_b"] = _bra_bwd_ln(da4, sv["a2"], P["ln_a_g"], P["ln_a_b"], name=tag + "bra_bwd_ln")
    dza, G["conv_a_w"] = _bra_bwd_conv(da2, sv["za"], P["conv_a_w"], B=B, S=S, name=tag + "bra_bwd_conv")
    dp = mm(dyb, WT["b_out"], out_dtype=ACT, name=tag + "mm_b_out_dx")
    G["w_b_out"] = mm_tn(sv["p"], dyb, name=tag + "mm_b_out_dw")
    dzb, G["w_s"], dbs_t, G["ln_b_g"], G["ln_b_b"] = _brb_bwd(dp, sv["zb"], P["ln_b_g"], P["ln_b_b"], P["w_s"],
                                                             sv["b_s_t"], name=tag + "brb_bwd")
    G["b_s"] = dbs_t.T
    doc = mm(dyc, WT["c_out"], out_dtype=ACT, name=tag + "mm_c_out_dx")
    G["w_c_out"] = mm_tn(sv["oc"], dyc, name=tag + "mm_c_out_dw")
    dd = _attn_rowdot(doc, sv["oc"], name=tag + "attn_rowdot")
    dq_parts, dk_parts, dv_parts, dbiases = [], [], [], []
    for gi, (_, dil) in enumerate(DIL_GROUPS):
        dqkv_g, dbias = _attn_bwd(sv["qkv_gs"][gi], _regroup(doc, B, S, dil), _regroup(sv["lse_tot"], B, S, dil),
                                  _regroup(dd, B, S, dil), tabs[gi][0], name=tag + f"attn_bwd{gi}")
        t = _ungroup(dqkv_g)
        dq_parts.append(t[:, 0:GROUP_COLS])
        dk_parts.append(t[:, GROUP_COLS:2 * GROUP_COLS])
        dv_parts.append(t[:, 2 * GROUP_COLS:3 * GROUP_COLS])
        dbiases.append(dbias)
    dqkv = jnp.concatenate(dq_parts + dk_parts + dv_parts, axis=1)
    G["dbias"] = jnp.concatenate(dbiases, axis=0)
    dh1 = mm(dza, WT["in_a"], out_dtype=F32, name=tag + "mm_in_a_dx")
    dh1 = mm(dzb, WT["in_b"], out_dtype=F32, name=tag + "mm_in_b_dx", add=dh1)
    dh1 = mm(dqkv, WT["in_c"], out_dtype=F32, name=tag + "mm_in_c_dx", add=dh1)
    dh1 = mm(dzg, WT["in_g"], out_dtype=ACT, name=tag + "mm_in_g_dx", add=dh1)
    h1 = sv["h1"]
    G["w_in"] = jnp.concatenate([mm_tn(h1, dza, name=tag + "mm_in_a_dw"), mm_tn(h1, dzb, name=tag + "mm_in_b_dw"),
                                 mm_tn(h1, dqkv, name=tag + "mm_in_c_dw"), mm_tn(h1, dzg, name=tag + "mm_in_g_dw")],
                                axis=1)
    return d, dh1, dg_next


_COL_SHARDED = ("w_in", "b_gate", "conv_a_w", "w_a_out", "w_b_out", "w_c_out", "w_xkv", "w_up", "conv_f_w")
_ROW_SHARDED = ("w_mix_out", "w_xq", "w_xo", "w_down")
_SHARDED_BIG = ("w_in", "w_a_out", "w_b_out", "w_c_out", "w_mix_out", "w_xq", "w_xkv", "w_xo", "w_up", "w_down")
_SHARDED_SMALL = ("b_gate", "conv_a_w", "conv_f_w")
_REPLICATED = ("mix_pre_g", "mix_post_g", "conv_a_b", "ln_a_g", "ln_a_b", "ln_b_g", "ln_b_b", "w_s", "b_s",
               "x_pre_g", "x_post_g", "mem_g", "ffn_pre_g", "ffn_post_g", "conv_f_b")
_WEIGHTS = ('rel_bias', 'mix_pre_g', 'mix_post_g', 'w_in', 'b_gate', 'conv_a_w', 'conv_a_b', 'ln_a_g', 'ln_a_b',
            'w_a_out', 'ln_b_g', 'ln_b_b', 'w_s', 'b_s', 'w_b_out', 'w_c_out', 'w_mix_out', 'x_pre_g', 'x_post_g',
            'mem_g', 'w_xq', 'w_xkv', 'w_xo', 'ffn_pre_g', 'ffn_post_g', 'w_up', 'conv_f_w', 'conv_f_b', 'w_down')
_PACK_COLS = 1024


def _shard_axis(name):
    return 1 if name in _ROW_SHARDED else 2


def _pack(parts, dtype, row_mult):
    flat = jnp.concatenate([p.astype(dtype).reshape(-1) for p in parts])
    n = flat.shape[0]
    unit = _PACK_COLS * row_mult
    padded = -(-n // unit) * unit
    flat = jnp.pad(flat, (0, padded - n))
    return flat.reshape(padded // _PACK_COLS, _PACK_COLS)


def _join8(blocks, ax):
    t = jnp.moveaxis(blocks, 0, ax)
    shp = t.shape
    return t.reshape(shp[:ax] + (shp[ax] * shp[ax + 1],) + shp[ax + 2:])


def _split8(full, ax):
    shp = full.shape
    t = full.reshape(shp[:ax] + (N_DEV, shp[ax] // N_DEV) + shp[ax + 1:])
    return jnp.moveaxis(t, ax, 0)


_W_KEYS = {"w_a_out": "a_out", "w_b_out": "b_out", "w_c_out": "c_out", "w_mix_out": "mix_out", "w_xq": "xq",
           "w_xkv": "xkv", "w_xo": "xo", "w_up": "up", "w_down": "down"}
_IN_SPLITS = (("in_a", 0, 1024), ("in_b", 1024, 2048), ("in_c", 2048, 4352), ("in_g", 4352, 7424))


def _layer_weights(name, full_l):
    if name == "w_in":
        assert full_l.shape[1] == _IN_SPLITS[-1][2]
        W = {k: full_l[:, a:b] for k, a, b in _IN_SPLITS}
    else:
        W = {_W_KEYS[name]: _ffn_interleave(full_l) if name == "w_up" else full_l}
    return W, {k: v.T for k, v in W.items()}


def _sum_rows(x, *, name):
    n = x.shape[0]

    def body(x_ref, o_ref):
        acc = x_ref[0:1, :]
        for i in range(1, n):
            acc = acc + x_ref[i:i + 1, :]
        o_ref[...] = acc

    return pl.pallas_call(body, name=name, out_shape=jax.ShapeDtypeStruct((1, x.shape[1]), x.dtype))(x)


def kernel(x, mem, rel_bias, mix_pre_g, mix_post_g, w_in, b_gate, conv_a_w, conv_a_b, ln_a_g, ln_a_b, w_a_out, ln_b_g, ln_b_b, w_s, b_s, w_b_out, w_c_out, w_mix_out, x_pre_g, x_post_g, mem_g, w_xq, w_xkv, w_xo, ffn_pre_g, ffn_post_g, w_up, conv_f_w, conv_f_b, w_down, loss_target, m_rel_bias, m_mix_pre_g, m_mix_post_g, m_w_in, m_b_gate, m_conv_a_w, m_conv_a_b, m_ln_a_g, m_ln_a_b, m_w_a_out, m_ln_b_g, m_ln_b_b, m_w_s, m_b_s, m_w_b_out, m_w_c_out, m_w_mix_out, m_x_pre_g, m_x_post_g, m_mem_g, m_w_xq, m_w_xkv, m_w_xo, m_ffn_pre_g, m_ffn_post_g, m_w_up, m_conv_f_w, m_conv_f_b, m_w_down, v_rel_bias, v_mix_pre_g, v_mix_post_g, v_w_in, v_b_gate, v_conv_a_w, v_conv_a_b, v_ln_a_g, v_ln_a_b, v_w_a_out, v_ln_b_g, v_ln_b_b, v_w_s, v_b_s, v_w_b_out, v_w_c_out, v_w_mix_out, v_x_pre_g, v_x_post_g, v_mem_g, v_w_xq, v_w_xkv, v_w_xo, v_ffn_pre_g, v_ffn_post_g, v_w_up, v_conv_f_w, v_conv_f_b, v_w_down):
    args = locals()
    w_loc = {n: args[n] for n in _WEIGHTS}
    m_loc = {n: args["m_" + n] for n in _WEIGHTS}
    v_loc = {n: args["v_" + n] for n in _WEIGHTS}

    depth = w_in.shape[0]
    assert depth == 2
    B, S, D = x.shape
    mine = _index_of(_my_pos())
    core = lax.axis_index("c").astype(jnp.int32).reshape(1)
    my_chip = (2 * lax.axis_index("x") + lax.axis_index("y")).astype(jnp.int32).reshape(1)
    shard = {n: w_loc[n].astype(BF16) for n in _SHARDED_BIG}
    shard.update({n: w_loc[n] for n in _SHARDED_SMALL})
    Ws, WTs = [{} for _ in range(depth)], [{} for _ in range(depth)]
    small_full = {}
    hosts = _Hosts()

    def install(items, blocks):
        for (n, l), blk in zip(items, blocks):
            own = shard[n] if l is None else shard[n][l]
            sel = lax.broadcasted_iota(jnp.int32, (N_DEV,) + (1,) * own.ndim, 0) == mine
            blk = jnp.where(sel, own[None], blk)
            if l is None:
                small_full[n] = _join8(blk, _shard_axis(n))
            else:
                W, WT = _layer_weights(n, _join8(blk, _shard_axis(n) - 1))
                Ws[l].update(W)
                WTs[l].update(WT)

    def carry_gather(name, items, group, then=None):
        def done(outs):
            group.extend(zip(items, outs))
            if then is not None:
                then()
        hosts.add(name, lambda: _gather_ici([shard[n] for n, _ in items], [l for _, l in items]), done)

    def finish_gather(group, tag):
        install([it for it, _ in group], _exchange_call(_gather_d2d([b for _, b in group]), name="gather_d2d_" + tag))

    first = [("w_in", 0)] + [(n, None) for n in _SHARDED_SMALL]
    install(first, _all_gather([shard[n] if l is None else shard[n][l] for n, l in first], name="gather_first"))
    g0, g1 = [], []
    carry_gather("l0_mm_in_a", [("w_xq", 0), ("w_a_out", 0), ("w_b_out", 0), ("w_c_out", 0)], g0)
    carry_gather("l0_mm_in_b", [("w_xo", 0), ("w_mix_out", 0)], g0)
    carry_gather("l0_mm_in_c", [("w_xkv", 0), ("w_down", 0)], g0)
    carry_gather("l0_mm_in_g", [("w_up", 0)], g0, then=lambda: finish_gather(g0, "l0"))
    carry_gather("l0_mm_mix", [("w_xq", 1), ("w_mix_out", 1)], g1)
    carry_gather("l0_mm_xq", [("w_xo", 1), ("w_a_out", 1), ("w_b_out", 1), ("w_c_out", 1)], g1)
    carry_gather("l0_mm_xo", [("w_xkv", 1)], g1)
    carry_gather("l0_mm_up", [("w_in", 1), ("w_down", 1)], g1)
    carry_gather("l0_mm_down", [("w_up", 1)], g1, then=lambda: finish_gather(g1, "l1"))

    Gs = [{} for _ in range(depth)]
    sendbuf, chip_sums, parts = {}, {}, {}

    def to_send(item):
        n, l = item
        if l is None:
            g = jnp.stack([Gs[k][n].reshape(small_full[n].shape[1:]) for k in range(depth)], axis=0)
            sendbuf[item] = _split8(g, _shard_axis(n))
        else:
            sendbuf[item] = _split8(Gs[l][n], _shard_axis(n) - 1).astype(BF16)
        return sendbuf[item]

    def swapped(items, got):
        for item, r in zip(items, got):
            chip_sums[item] = _chip_sum(sendbuf[item], r, core, name=f"chip_sum_{item[0]}_{item[1]}")

    def carry_swap(name, items):
        hosts.add(name, lambda: _scatter_d2d([to_send(it) for it in items]), lambda outs: swapped(items, outs))

    def carry_scatter(name, items):
        hosts.add(name, lambda: _scatter_ici([chip_sums[it] for it in items]), lambda outs: parts.update(zip(items, outs)))

    carry_swap("l0_mm_down_dx", [(n, 1) for n in _SHARDED_BIG])
    carry_scatter("l0_mm_down_dw", [(n, 1) for n in ("w_xq", "w_xkv", "w_xo", "w_mix_out", "w_a_out", "w_b_out", "w_c_out")])
    carry_scatter("l0_mm_up_dx", [("w_in", 1)])
    carry_scatter("l0_mm_up_dw", [("w_up", 1), ("w_down", 1)])
    carry_swap("l0_mm_xo_dx", [("w_down", 0), ("w_up", 0)])
    carry_scatter("l0_mm_xq_dx", [("w_down", 0)])
    carry_swap("l0_mm_mix_dx", [("w_xo", 0), ("w_xq", 0), ("w_xkv", 0)])
    carry_swap("l0_mm_in_a_dx", [("w_mix_out", 0), ("w_a_out", 0), ("w_b_out", 0), ("w_c_out", 0)])
    carry_scatter("l0_mm_in_b_dx", [("w_mix_out", 0), ("w_a_out", 0), ("w_b_out", 0), ("w_c_out", 0)])
    carry_scatter("l0_mm_in_c_dx", [("w_xo", 0), ("w_xq", 0), ("w_xkv", 0)])
    carry_scatter("l0_mm_in_g_dx", [("w_up", 0)])

    rep_w = {n: w_loc[n] for n in ("rel_bias",) + _REPLICATED}
    Ps = lambda: [dict({n: rep_w[n][l] for n in _REPLICATED}, **{n: small_full[n][l] for n in _SHARDED_SMALL})
                  for l in range(depth)]
    loss_vec, grad_x, g_rel = _run_step(x, mem, loss_target, Ws, WTs, Ps(), rep_w["rel_bias"], Gs, hosts)
    assert not hosts.plan, list(hosts.plan)

    last = [("w_in", 0)] + [(n, None) for n in _SHARDED_SMALL]
    swapped(last, _exchange_call(_scatter_d2d([to_send(it) for it in last]), name="swap_grads_last"))
    parts.update(zip(last, _exchange_call(_scatter_ici([chip_sums[it] for it in last]), name="scatter_grads_last")))

    rep = ("rel_bias",) + _REPLICATED
    rep_g = {n: jnp.stack([Gs[l][n].reshape(w_loc[n].shape[1:]) for l in range(depth)], axis=0) for n in _REPLICATED}
    rep_g["rel_bias"] = g_rel
    packed = _pack([loss_vec] + [rep_g[n] for n in rep], F32, 8)
    (allp,) = _all_gather([packed], name="gather_rep_grads")
    sel = lax.broadcasted_iota(jnp.int32, (N_DEV, 1, 1), 0) == mine
    allp = jnp.where(sel, packed[None], allp).reshape(N_DEV, -1)
    loss = _sum_rows(allp[:, :LANE], name="loss_sum")[0, 0]
    off = LANE
    rep_parts = {}
    for n in rep:
        size = math.prod(w_loc[n].shape)
        rep_parts[n] = allp[:, off:off + size].reshape((N_DEV,) + w_loc[n].shape)
        off += size

    g_loc, deltas, new_m, new_v = {}, {}, {}, {}
    for n in _WEIGHTS:
        if n in rep_parts:
            p, own = [rep_parts[n]], {}
        else:
            its = [(n, None)] if n in _SHARDED_SMALL else [(n, l) for l in range(depth)]
            p, own = [parts[it] for it in its], {"own": [chip_sums[it] for it in its], "own_slot": my_chip}
        g_loc[n], deltas[n], new_m[n], new_v[n] = _adamw(w_loc[n], p, m_loc[n], v_loc[n], name="adamw_" + n, **own)
    return (loss, grad_x, *[g_loc[n] for n in _WEIGHTS], *[deltas[n] for n in _WEIGHTS],
            *[new_m[n] for n in _WEIGHTS], *[new_v[n] for n in _WEIGHTS])


def _local_step(x, mem, loss_target, full):
    depth = full["w_in"].shape[0]
    Ws, WTs, Ps = [{} for _ in range(depth)], [{} for _ in range(depth)], []
    for l in range(depth):
        Ps.append({n: full[n][l] for n in _WEIGHTS if n != "rel_bias"})
        for n in _SHARDED_BIG:
            W, WT = _layer_weights(n, full[n][l].astype(BF16))
            Ws[l].update(W)
            WTs[l].update(WT)
    Gs = [{} for _ in range(depth)]
    loss_vec, grad_x, g_rel = _run_step(x, mem, loss_target, Ws, WTs, Ps, full["rel_bias"], Gs, None)
    grads = {"rel_bias": g_rel}
    for n in _WEIGHTS:
        if n != "rel_bias":
            grads[n] = jnp.stack([Gs[l][n].reshape(full[n].shape[1:]) for l in range(depth)], axis=0)
    return loss_vec[0, 0], grad_x, grads


def _run_step(x, mem, loss_target, Ws, WTs, Ps, rel_bias, Gs, hosts):
    B, S, D = x.shape
    depth = len(Ws)
    x2d = x.reshape(B * S, D)
    mem2 = mem.reshape(-1, D)
    tabs = _bias_tables(rel_bias)
    (h1,) = _norm_fwd(x2d, None, None, Ps[0]["mix_pre_g"], name="norm0")
    xc = x2d
    saved = []
    for l in range(depth):
        nxt = Ps[l + 1]["mix_pre_g"] if l + 1 < depth else None
        xc, h1, sv = _layer_fwd(l, xc, h1, mem2, Ws[l], Ps[l], tabs, nxt, B=B, S=S, hosts=hosts)
        saved.append(sv)
    loss_vec, d = _loss_kernel(xc, loss_target.reshape(B * S, D), name="loss")

    dh_next = None
    for l in reversed(range(depth)):
        nxt = Ps[l + 1]["mix_pre_g"] if l + 1 < depth else None
        d, dh_next, dg_next = _layer_bwd(l, d, dh_next, nxt, saved[l], mem2, Ws[l], WTs[l], Ps[l], tabs, Gs[l], B=B, S=S,
                                         hosts=hosts)
        if dg_next is not None:
            Gs[l + 1]["mix_pre_g"] = dg_next
    grad_x2d, Gs[0]["mix_pre_g"] = _norm_bwd(d, dh_next, x2d, Ps[0]["mix_pre_g"], None, None, name="norm0_bwd")

    dbias = jnp.stack([Gs[l]["dbias"] for l in range(depth)], axis=0)
    buckets = jnp.stack([t[1] for t in tabs], axis=0)
    rb = _bucket_sum(dbias, buckets, name="rel_bias_grad")
    return loss_vec, grad_x2d.reshape(B, S, D), rb[:, :rel_bias.shape[1]]
```

```python
import functools
import math

import jax
import jax.numpy as jnp
from jax import lax
from jax.experimental import pallas as pl
from jax.experimental.pallas import tpu as pltpu

F32 = jnp.float32
BF16 = jnp.bfloat16

N_DEV = 8
NORM_EPS = 1e-6
LN_EPS = 1e-5
CONV_K = 31
SG_CHUNK = 128
ATT_BLOCK = 128
HEAD_DIM = 64
HEADS_PER_GROUP = 4
GROUP_COLS = HEADS_PER_GROUP * HEAD_DIM
DIL_GROUPS = ((128, 1), (512, 4), (2048, 16))
REL_BUCKETS = 32
REL_MAX_DIST = 2048
X_HEADS = 4
ATT_SCALE = HEAD_DIM ** -0.5
MASK_VALUE = -1e30

ADAM_LR = 0.001
ADAM_B1 = 0.9
ADAM_B2 = 0.999
ADAM_EPS = 1e-08
ADAM_WD = 0.01
ADAM_STEP = 10

LANE = 128
ACT = BF16
ROW_TILE = 512
VMEM_LIMIT = 48 << 20

_NT = (((1,), (1,)), ((), ()))
_TN = (((0,), (0,)), ((), ()))


def _params(sem, vmem=VMEM_LIMIT):
    return pltpu.CompilerParams(dimension_semantics=sem, vmem_limit_bytes=vmem)


def _pick(n, cap):
    if n <= cap:
        return n
    best = None
    for t in range(LANE, cap + 1, LANE):
        if n % t == 0:
            best = t
    assert best is not None, (n, cap)
    return best


def _sigmoid(x):
    return 1.0 / (1.0 + jnp.exp(-x))


_GELU_C = math.sqrt(2.0 / math.pi)


def _gelu(x):
    return 0.5 * x * (1.0 + jnp.tanh(_GELU_C * (x + 0.044715 * x * x * x)))


def _gelu_and_grad(x):
    t = jnp.tanh(_GELU_C * (x + 0.044715 * x * x * x))
    g = 0.5 * x * (1.0 + t)
    dg = 0.5 * (1.0 + t) + 0.5 * x * (1.0 - t * t) * _GELU_C * (1.0 + 3 * 0.044715 * x * x)
    return g, dg


MM_VMEM_BUDGET = 36 << 20


def _divisors128(n):
    return [n] + [t for t in range(n - n % LANE, 0, -LANE) if n % t == 0 and t != n]


class _Hosted:
    def __init__(self, hosts, name):
        self.ex = hosts.get(name) if hosts is not None else None
        self.hosts, self.name = hosts, name

    @property
    def on(self):
        return self.ex is not None

    def specs(self):
        if not self.on:
            return [], [], [], []
        n = self.ex.n_copies
        return (list(self.ex.ins), [_HBM] * len(self.ex.ins), list(self.ex.out_shapes),
                [pltpu.SemaphoreType.DMA((n,)), pltpu.SemaphoreType.DMA((n,))])

    def split(self, refs, n_in, n_out, n_scratch):
        if not self.on:
            return list(refs), None
        ni, no = len(self.ex.ins), len(self.ex.out_shapes)
        refs = list(refs)
        ins, rest = refs[:n_in], refs[n_in:]
        ex_in, rest = rest[:ni], rest[ni:]
        outs, rest = rest[:n_out], rest[n_out:]
        ex_out, rest = rest[:no], rest[no:]
        scratch, sems = rest[:n_scratch], rest[n_scratch:]
        return ins + outs + scratch, (ex_in, ex_out, sems[0], sems[1])

    def run(self, ex_refs, grid):
        if not self.on:
            return
        ids = [pl.program_id(ax) for ax in range(len(grid))]
        first = functools.reduce(jnp.logical_and, [i == 0 for i in ids])
        last = functools.reduce(jnp.logical_and, [i == g - 1 for i, g in zip(ids, grid)])

        @pl.when(first)
        def _():
            sends, _ = self.ex.make(*ex_refs)
            for cp in sends:
                cp.start()

        return last

    def finish(self, ex_refs, last):
        if not self.on:
            return

        @pl.when(last)
        def _():
            sends, recvs = self.ex.make(*ex_refs)
            for cp in recvs:
                cp.wait_recv()
            for cp in sends:
                cp.wait_send()

    def done(self, outs):
        if self.on:
            self.hosts.put(self.name, list(outs))


def _mm(a, b, *, out_dtype, name, add=None, hosts=None):
    M, K = a.shape
    K2, N = b.shape
    assert K == K2
    has_add = add is not None
    ob = jnp.dtype(out_dtype).itemsize
    tm = _pick(M, 1024)
    tk = _pick(K, 3072)
    nk = K // tk
    for tn in _divisors128(N):
        use = 2 * (tm * tk * a.dtype.itemsize + tk * tn * b.dtype.itemsize + tm * tn * (ob + (4 if has_add else 0)))
        if use + (tm * tn * 4 if nk > 1 else 0) <= MM_VMEM_BUDGET:
            break
    grid = (M // tm, N // tn, nk)
    hosted = _Hosted(hosts, name)
    n_in = 3 if has_add else 2
    n_scr = 1 if nk > 1 else 0

    def body(*refs):
        own, ex_refs = hosted.split(refs, n_in, 1, n_scr)
        last = hosted.run(ex_refs, grid)
        a_ref, b_ref = own[0], own[1]
        c_ref = own[2] if has_add else None
        o_ref = own[n_in]
        part = jnp.dot(a_ref[...].astype(BF16), b_ref[...].astype(BF16), preferred_element_type=F32)
        if nk == 1:
            if has_add:
                part = part + c_ref[...]
            o_ref[...] = part.astype(o_ref.dtype)
        else:
            acc_ref = own[n_in + 1]
            k = pl.program_id(2)

            @pl.when(k == 0)
            def _():
                acc_ref[...] = part

            @pl.when(k > 0)
            def _():
                acc_ref[...] += part

            @pl.when(k == nk - 1)
            def _():
                r = acc_ref[...]
                if has_add:
                    r = r + c_ref[...]
                o_ref[...] = r.astype(o_ref.dtype)
        hosted.finish(ex_refs, last)

    in_specs = [pl.BlockSpec((tm, tk), lambda i, j, k: (i, k)), pl.BlockSpec((tk, tn), lambda i, j, k: (k, j))]
    args = [a, b]
    if has_add:
        in_specs.append(pl.BlockSpec((tm, tn), lambda i, j, k: (i, j)))
        args.append(add)
    ex_args, ex_in_specs, ex_out_shapes, ex_scratch = hosted.specs()
    outs = pl.pallas_call(
        body, name=name, out_shape=(jax.ShapeDtypeStruct((M, N), out_dtype), *ex_out_shapes),
        grid=grid, in_specs=in_specs + ex_in_specs,
        out_specs=(pl.BlockSpec((tm, tn), lambda i, j, k: (i, j)), *([_HBM] * len(ex_out_shapes))),
        scratch_shapes=([pltpu.VMEM((tm, tn), F32)] if nk > 1 else []) + ex_scratch,
        compiler_params=_params(("arbitrary",) * 3 if hosted.on else ("parallel", "parallel", "arbitrary")),
    )(*args, *ex_args)
    hosted.done(outs[1:])
    return outs[0]


def _mm_tn(a, b, *, name, hosts=None):
    T, M = a.shape
    T2, N = b.shape
    assert T == T2
    tm = _pick(M, 1536)
    tt = _pick(T, 2048)
    nt = T // tt
    for tn in _divisors128(N):
        if 2 * (tt * tm * a.dtype.itemsize + tt * tn * b.dtype.itemsize + tm * tn * 4) <= MM_VMEM_BUDGET:
            break
    grid = (M // tm, N // tn, nt)
    hosted = _Hosted(hosts, name)

    def body(*refs):
        (a_ref, b_ref, o_ref), ex_refs = hosted.split(refs, 2, 1, 0)
        last = hosted.run(ex_refs, grid)
        k = pl.program_id(2)
        part = lax.dot_general(a_ref[...].astype(BF16), b_ref[...].astype(BF16), _TN, preferred_element_type=F32)

        @pl.when(k == 0)
        def _():
            o_ref[...] = part

        @pl.when(k > 0)
        def _():
            o_ref[...] += part

        hosted.finish(ex_refs, last)

    ex_args, ex_in_specs, ex_out_shapes, ex_scratch = hosted.specs()
    outs = pl.pallas_call(
        body, name=name, out_shape=(jax.ShapeDtypeStruct((M, N), F32), *ex_out_shapes),
        grid=grid,
        in_specs=[pl.BlockSpec((tt, tm), lambda i, j, k: (k, i)), pl.BlockSpec((tt, tn), lambda i, j, k: (k, j))]
        + ex_in_specs,
        out_specs=(pl.BlockSpec((tm, tn), lambda i, j, k: (i, j)), *([_HBM] * len(ex_out_shapes))),
        scratch_shapes=ex_scratch,
        compiler_params=_params(("arbitrary",) * 3 if hosted.on else ("parallel", "parallel", "arbitrary")),
    )(a, b, *ex_args)
    hosted.done(outs[1:])
    return outs[0]


def _rms(x):
    r = lax.rsqrt(jnp.mean(x * x, axis=-1, keepdims=True) + NORM_EPS)
    return x * r, r


def _row_spec(cols, tr=ROW_TILE):
    return pl.BlockSpec((tr, cols), lambda i: (i, 0))


def _vec_spec(cols):
    return pl.BlockSpec((1, cols), lambda i: (0, 0))


def _norm_fwd(x, y, g_post, g_pre, *, name):
    T, D = x.shape
    has_post = y is not None
    has_pre = g_pre is not None

    def body(*refs):
        refs = list(refs)
        x_ref = refs.pop(0)
        xn = x_ref[...]
        if has_post:
            y_ref = refs.pop(0)
            gp_ref = refs.pop(0)
        if has_pre:
            gq_ref = refs.pop(0)
        if has_post:
            yh, _ = _rms(y_ref[...].astype(F32))
            xn = xn + yh * gp_ref[...]
            refs.pop(0)[...] = xn
        if has_pre:
            xh, _ = _rms(xn)
            refs.pop(0)[...] = (xh * gq_ref[...]).astype(BF16)

    args, in_specs, out_shape, out_specs = [x], [_row_spec(D)], [], []
    if has_post:
        args += [y, g_post.reshape(1, D)]
        in_specs += [_row_spec(D), _vec_spec(D)]
        out_shape.append(jax.ShapeDtypeStruct((T, D), F32))
        out_specs.append(_row_spec(D))
    if has_pre:
        args.append(g_pre.reshape(1, D))
        in_specs.append(_vec_spec(D))
        out_shape.append(jax.ShapeDtypeStruct((T, D), BF16))
        out_specs.append(_row_spec(D))
    return pl.pallas_call(body, name=name, out_shape=tuple(out_shape), grid=(T // ROW_TILE,), in_specs=in_specs,
                          out_specs=tuple(out_specs), compiler_params=_params(("parallel",)))(*args)


def _norm_bwd(dres, dh, x_new, g_pre, y, g_post, *, name):
    has_res = dres is not None
    has_pre = dh is not None
    has_post = y is not None
    out_d = has_pre and (has_res or has_post)
    T, D = (dres if has_res else dh).shape
    if not has_res:
        assert not has_post

    def body(*refs):
        refs = list(refs)
        i = pl.program_id(0)
        d = refs.pop(0)[...] if has_res else None
        if has_pre:
            dh_v = refs.pop(0)[...].astype(F32)
            xh, r = _rms(refs.pop(0)[...])
            gq = refs.pop(0)[...]
        if has_post:
            yh, ry = _rms(refs.pop(0)[...].astype(F32))
            gp = refs.pop(0)[...]
        if has_pre:
            dxh = dh_v * gq
            dpre = r * (dxh - xh * jnp.mean(dxh * xh, axis=-1, keepdims=True))
            d = dpre if d is None else d + dpre
            dgq = jnp.sum(dh_v * xh, axis=0, keepdims=True)
        if out_d:
            refs.pop(0)[...] = d
        if has_post:
            dyh = d * gp
            refs.pop(0)[...] = (ry * (dyh - yh * jnp.mean(dyh * yh, axis=-1, keepdims=True))).astype(BF16)
            dgp_ref = refs.pop(0)
            dgp = jnp.sum(d * yh, axis=0, keepdims=True)

            @pl.when(i == 0)
            def _():
                dgp_ref[...] = dgp

            @pl.when(i > 0)
            def _():
                dgp_ref[...] += dgp
        if has_pre:
            dgq_ref = refs.pop(0)

            @pl.when(i == 0)
            def _():
                dgq_ref[...] = dgq

            @pl.when(i > 0)
            def _():
                dgq_ref[...] += dgq

    args, in_specs, out_shape, out_specs = [], [], [], []
    if has_res:
        args.append(dres)
        in_specs.append(_row_spec(D))
    if has_pre:
        args += [dh, x_new, g_pre.reshape(1, D)]
        in_specs += [_row_spec(D), _row_spec(D), _vec_spec(D)]
    if has_post:
        args += [y, g_post.reshape(1, D)]
        in_specs += [_row_spec(D), _vec_spec(D)]
    if out_d:
        out_shape.append(jax.ShapeDtypeStruct((T, D), F32))
        out_specs.append(_row_spec(D))
    if has_post:
        out_shape += [jax.ShapeDtypeStruct((T, D), BF16), jax.ShapeDtypeStruct((1, D), F32)]
        out_specs += [_row_spec(D), _vec_spec(D)]
    if has_pre:
        out_shape.append(jax.ShapeDtypeStruct((1, D), F32))
        out_specs.append(_vec_spec(D))
    return pl.pallas_call(body, name=name, out_shape=tuple(out_shape), grid=(T // ROW_TILE,), in_specs=in_specs,
                          out_specs=tuple(out_specs), compiler_params=_params(("arbitrary",)))(*args)


def _loss_kernel(xf, target, *, name):
    T, D = xf.shape

    def body(x_ref, t_ref, loss_ref, d_ref):
        i = pl.program_id(0)
        e = x_ref[...] - t_ref[...]
        d_ref[...] = e * (1.0 / D)
        part = jnp.sum(jnp.sum(e * e, axis=0, keepdims=True), axis=1, keepdims=True) * (0.5 / D)
        part = jnp.broadcast_to(part, (1, LANE))

        @pl.when(i == 0)
        def _():
            loss_ref[...] = part

        @pl.when(i > 0)
        def _():
            loss_ref[...] += part

    return pl.pallas_call(body, name=name,
                          out_shape=(jax.ShapeDtypeStruct((1, LANE), F32), jax.ShapeDtypeStruct((T, D), F32)),
                          grid=(T // ROW_TILE,), in_specs=[_row_spec(D), _row_spec(D)],
                          out_specs=(_vec_spec(LANE), _row_spec(D)), compiler_params=_params(("arbitrary",)))(xf, target)


CONV_PAD = 32
CONV_CH = 64
SUBLANES = 8


def _shift_rows(win, r):
    return win if r == 0 else win[r:r + win.shape[0] - SUBLANES, :]


def _ln_stats(x):
    mu = jnp.mean(x, axis=-1, keepdims=True)
    xc = x - mu
    rstd = lax.rsqrt(jnp.mean(xc * xc, axis=-1, keepdims=True) + LN_EPS)
    return xc * rstd, rstd


def _acc_out(ref, val, first):
    @pl.when(first)
    def _():
        ref[...] = val

    @pl.when(jnp.logical_not(first))
    def _():
        ref[...] += val


def _bra_fwd(za, conv_w, conv_b, ln_g, ln_b, *, B, S, name):
    C = conv_w.shape[1]

    def body(za_ref, w_ref, cb_ref, g_ref, b_ref, a4_ref, a2_ref, pad_ref):
        pad_ref[0:CONV_PAD, :] = jnp.zeros((CONV_PAD, C), F32)
        pad_ref[CONV_PAD:, :] = za_ref[:, 0:C].astype(F32) * _sigmoid(za_ref[:, C:2 * C].astype(F32))

        def chunk(i, _):
            base = pl.multiple_of(i * CONV_CH, CONV_CH)
            win = pad_ref[pl.ds(base, CONV_CH + CONV_PAD), :]
            acc = jnp.broadcast_to(cb_ref[...], (CONV_CH, C))
            for r in range(SUBLANES):
                sh = _shift_rows(win, r)
                for j in range(CONV_K):
                    off = CONV_PAD - (CONV_K - 1) + j
                    if off % SUBLANES == r:
                        acc = acc + w_ref[j:j + 1, :] * sh[off - r:off - r + CONV_CH, :]
            a2_ref[pl.ds(base, CONV_CH), :] = acc
            xh, _ = _ln_stats(acc)
            a3 = xh * g_ref[...] + b_ref[...]
            a4_ref[pl.ds(base, CONV_CH), :] = (a3 * _sigmoid(a3)).astype(BF16)
            return 0

        lax.fori_loop(0, S // CONV_CH, chunk, 0)

    vec = pl.BlockSpec((1, C), lambda b: (0, 0))
    return pl.pallas_call(
        body, name=name,
        out_shape=(jax.ShapeDtypeStruct((B * S, C), BF16), jax.ShapeDtypeStruct((B * S, C), F32)),
        grid=(B,),
        in_specs=[pl.BlockSpec((S, 2 * C), lambda b: (b, 0)), pl.BlockSpec((CONV_K, C), lambda b: (0, 0)), vec, vec, vec],
        out_specs=(pl.BlockSpec((S, C), lambda b: (b, 0)), pl.BlockSpec((S, C), lambda b: (b, 0))),
        scratch_shapes=[pltpu.VMEM((S + CONV_PAD, C), F32)],
        compiler_params=_params(("parallel",)),
    )(za, conv_w, conv_b.reshape(1, C), ln_g.reshape(1, C), ln_b.reshape(1, C))


def _bra_bwd_ln(da4, a2, ln_g, ln_b, *, name):
    T, C = a2.shape

    def body(da4_ref, a2_ref, g_ref, b_ref, da2_ref, dg_ref, db_ref, dcb_ref):
        first = pl.program_id(0) == 0
        xh, rstd = _ln_stats(a2_ref[...])
        a3 = xh * g_ref[...] + b_ref[...]
        sg = _sigmoid(a3)
        da3 = da4_ref[...].astype(F32) * (sg * (1.0 + a3 * (1.0 - sg)))
        dxh = da3 * g_ref[...]
        da2 = rstd * (dxh - jnp.mean(dxh, axis=-1, keepdims=True) - xh * jnp.mean(dxh * xh, axis=-1, keepdims=True))
        da2_ref[...] = da2
        _acc_out(dg_ref, jnp.sum(da3 * xh, axis=0, keepdims=True), first)
        _acc_out(db_ref, jnp.sum(da3, axis=0, keepdims=True), first)
        _acc_out(dcb_ref, jnp.sum(da2, axis=0, keepdims=True), first)

    vec = _vec_spec(C)
    vshape = jax.ShapeDtypeStruct((1, C), F32)
    return pl.pallas_call(body, name=name, out_shape=(jax.ShapeDtypeStruct((T, C), F32), vshape, vshape, vshape),
                          grid=(T // ROW_TILE,), in_specs=[_row_spec(C), _row_spec(C), vec, vec],
                          out_specs=(_row_spec(C), vec, vec, vec),
                          compiler_params=_params(("arbitrary",)))(da4, a2, ln_g.reshape(1, C), ln_b.reshape(1, C))


def _bra_bwd_conv(da2, za, conv_w, *, B, S, name):
    C = conv_w.shape[1]
    SUB = 8

    def body(da2_ref, za_ref, w_ref, dza_ref, dw_ref, pad1_ref, pad2_ref, dwacc_ref):
        first = pl.program_id(0) == 0
        pad1_ref[0:CONV_PAD, :] = jnp.zeros((CONV_PAD, C), F32)
        pad1_ref[CONV_PAD:, :] = za_ref[:, 0:C].astype(F32) * _sigmoid(za_ref[:, C:2 * C].astype(F32))
        pad2_ref[0:S, :] = da2_ref[...]
        pad2_ref[S:, :] = jnp.zeros((CONV_PAD, C), F32)
        dwacc_ref[...] = jnp.zeros_like(dwacc_ref)

        def chunk(i, _):
            base = pl.multiple_of(i * CONV_CH, CONV_CH)
            rows = pl.ds(base, CONV_CH)
            win1 = pad1_ref[pl.ds(base, CONV_CH + CONV_PAD), :]
            win2 = pad2_ref[pl.ds(base, CONV_CH + CONV_PAD), :]
            d = win2[0:CONV_CH, :]
            da1 = jnp.zeros((CONV_CH, C), F32)
            for r in range(SUBLANES):
                sh1 = _shift_rows(win1, r)
                sh2 = _shift_rows(win2, r)
                for j in range(CONV_K):
                    off2 = CONV_K - 1 - j
                    if off2 % SUBLANES == r:
                        da1 = da1 + w_ref[j:j + 1, :] * sh2[off2 - r:off2 - r + CONV_CH, :]
                    off1 = CONV_PAD - (CONV_K - 1) + j
                    if off1 % SUBLANES == r:
                        prod = d * sh1[off1 - r:off1 - r + CONV_CH, :]
                        dwacc_ref[j * SUB:(j + 1) * SUB, :] += jnp.sum(prod.reshape(CONV_CH // SUB, SUB, C), axis=0)
            a_val = za_ref[rows, 0:C].astype(F32)
            sg = _sigmoid(za_ref[rows, C:2 * C].astype(F32))
            dza_ref[rows, 0:C] = (da1 * sg).astype(BF16)
            dza_ref[rows, C:2 * C] = (da1 * a_val * sg * (1.0 - sg)).astype(BF16)
            return 0

        lax.fori_loop(0, S // CONV_CH, chunk, 0)
        _acc_out(dw_ref, jnp.sum(dwacc_ref[...].reshape(CONV_K, SUB, C), axis=1), first)

    return pl.pallas_call(
        body, name=name,
        out_shape=(jax.ShapeDtypeStruct((B * S, 2 * C), BF16), jax.ShapeDtypeStruct((CONV_K, C), F32)),
        grid=(B,),
        in_specs=[pl.BlockSpec((S, C), lambda b: (b, 0)), pl.BlockSpec((S, 2 * C), lambda b: (b, 0)),
                  pl.BlockSpec((CONV_K, C), lambda b: (0, 0))],
        out_specs=(pl.BlockSpec((S, 2 * C), lambda b: (b, 0)), pl.BlockSpec((CONV_K, C), lambda b: (0, 0))),
        scratch_shapes=[pltpu.VMEM((S + CONV_PAD, C), F32), pltpu.VMEM((S + CONV_PAD, C), F32),
                        pltpu.VMEM((CONV_K * SUB, C), F32)],
        compiler_params=_params(("arbitrary",)),
    )(da2, za, conv_w)


SG_STEP = 4


def _tril_mask():
    r = lax.broadcasted_iota(jnp.int32, (SG_CHUNK, SG_CHUNK), 0)
    c = lax.broadcasted_iota(jnp.int32, (SG_CHUNK, SG_CHUNK), 1)
    return c <= r


def _brb_fwd(zb, ln_g, ln_b, w_s, b_s_t, *, name):
    T, C2 = zb.shape
    C = C2 // 2
    G = w_s.shape[0]
    GC = C // G

    def body(zb_ref, g_ref, b_ref, ws_ref, bs_ref, p_ref):
        z = _gelu(zb_ref[...].astype(F32))
        u = z[:, 0:C]
        xh, _ = _ln_stats(z[:, C:2 * C])
        v1 = (xh * g_ref[...] + b_ref[...]).astype(BF16)
        mask = _tril_mask()
        wss = [jnp.where(mask, ws_ref[gi], 0.0).astype(BF16) for gi in range(G)]
        for ci in range(SG_STEP):
            rows = slice(ci * SG_CHUNK, (ci + 1) * SG_CHUNK)
            outs = [jnp.dot(wss[gi], v1[rows, gi * GC:(gi + 1) * GC], preferred_element_type=F32) + bs_ref[:, gi:gi + 1]
                    for gi in range(G)]
            p_ref[rows, :] = (u[rows, :] * jnp.concatenate(outs, axis=1)).astype(BF16)

    tr = SG_STEP * SG_CHUNK
    return pl.pallas_call(
        body, name=name, out_shape=jax.ShapeDtypeStruct((T, C), BF16), grid=(T // tr,),
        in_specs=[_row_spec(C2, tr), _vec_spec(C), _vec_spec(C),
                  pl.BlockSpec((G, SG_CHUNK, SG_CHUNK), lambda i: (0, 0, 0)), pl.BlockSpec((SG_CHUNK, G), lambda i: (0, 0))],
        out_specs=_row_spec(C, tr), compiler_params=_params(("parallel",)),
    )(zb, ln_g.reshape(1, C), ln_b.reshape(1, C), w_s, b_s_t)


def _brb_bwd(dp, zb, ln_g, ln_b, w_s, b_s_t, *, name):
    T, C2 = zb.shape
    C = C2 // 2
    G = w_s.shape[0]
    GC = C // G

    def body(dp_ref, zb_ref, g_ref, b_ref, ws_ref, bs_ref, dzb_ref, dws_ref, dbs_ref, dg_ref, db_ref):
        first = pl.program_id(0) == 0
        z, dz = _gelu_and_grad(zb_ref[...].astype(F32))
        u = z[:, 0:C]
        xh, rstd = _ln_stats(z[:, C:2 * C])
        v1 = (xh * g_ref[...] + b_ref[...]).astype(BF16)
        dp_v = dp_ref[...].astype(F32)
        mask = _tril_mask()
        wss = [jnp.where(mask, ws_ref[gi], 0.0).astype(BF16) for gi in range(G)]
        dv2_all = dp_v * u
        dv2b_all = dv2_all.astype(BF16)
        v2_rows, dv1_rows = [], []
        dwss, dbss = [None] * G, [None] * G
        for ci in range(SG_STEP):
            rows = slice(ci * SG_CHUNK, (ci + 1) * SG_CHUNK)
            v2s, dv1s = [], []
            for gi in range(G):
                cols = slice(gi * GC, (gi + 1) * GC)
                v2s.append(jnp.dot(wss[gi], v1[rows, cols], preferred_element_type=F32) + bs_ref[:, gi:gi + 1])
                dv2b = dv2b_all[rows, cols]
                dbs = jnp.sum(dv2_all[rows, cols], axis=1, keepdims=True)
                dws = lax.dot_general(dv2b, v1[rows, cols], _NT, preferred_element_type=F32)
                dbss[gi] = dbs if dbss[gi] is None else dbss[gi] + dbs
                dwss[gi] = dws if dwss[gi] is None else dwss[gi] + dws
                dv1s.append(lax.dot_general(wss[gi], dv2b, _TN, preferred_element_type=F32))
            v2_rows.append(jnp.concatenate(v2s, axis=1))
            dv1_rows.append(jnp.concatenate(dv1s, axis=1))
        dwss = [jnp.where(mask, t, 0.0) for t in dwss]
        du = dp_v * jnp.concatenate(v2_rows, axis=0)
        dv1 = jnp.concatenate(dv1_rows, axis=0)
        dxh = dv1 * g_ref[...]
        dv0 = rstd * (dxh - jnp.mean(dxh, axis=-1, keepdims=True) - xh * jnp.mean(dxh * xh, axis=-1, keepdims=True))
        dzb_ref[:, 0:C] = (du * dz[:, 0:C]).astype(BF16)
        dzb_ref[:, C:2 * C] = (dv0 * dz[:, C:2 * C]).astype(BF16)
        _acc_out(dws_ref, jnp.stack(dwss, axis=0), first)
        _acc_out(dbs_ref, jnp.concatenate(dbss, axis=1), first)
        _acc_out(dg_ref, jnp.sum(dv1 * xh, axis=0, keepdims=True), first)
        _acc_out(db_ref, jnp.sum(dv1, axis=0, keepdims=True), first)

    wspec = pl.BlockSpec((G, SG_CHUNK, SG_CHUNK), lambda i: (0, 0, 0))
    bspec = pl.BlockSpec((SG_CHUNK, G), lambda i: (0, 0))
    return pl.pallas_call(
        body, name=name,
        out_shape=(jax.ShapeDtypeStruct((T, C2), BF16), jax.ShapeDtypeStruct((G, SG_CHUNK, SG_CHUNK), F32),
                   jax.ShapeDtypeStruct((SG_CHUNK, G), F32), jax.ShapeDtypeStruct((1, C), F32),
                   jax.ShapeDtypeStruct((1, C), F32)),
        grid=(T // (SG_STEP * SG_CHUNK),),
        in_specs=[_row_spec(C, SG_STEP * SG_CHUNK), _row_spec(C2, SG_STEP * SG_CHUNK), _vec_spec(C), _vec_spec(C), wspec,
                  bspec],
        out_specs=(_row_spec(C2, SG_STEP * SG_CHUNK), wspec, bspec, _vec_spec(C), _vec_spec(C)),
        compiler_params=_params(("arbitrary",)),
    )(dp, zb, ln_g.reshape(1, C), ln_b.reshape(1, C), w_s, b_s_t)


ATT_UNROLL = 3


def _each_class(d, fn):
    if d == 1:
        fn(0)
    else:
        def step(c, _):
            fn(c)
            return 0

        lax.fori_loop(0, d, step, 0, unroll=ATT_UNROLL + 1 if d % (ATT_UNROLL + 1) == 0 and d > ATT_UNROLL + 1 else 1)


def _pair_bias(b):
    off = jnp.full((ATT_BLOCK, ATT_BLOCK), MASK_VALUE, F32)
    return jnp.concatenate([jnp.concatenate([b, off], axis=1), jnp.concatenate([off, b], axis=1)], axis=0)


def _attn_fwd(qkv_g, bias, *, name):
    B, d, L, _ = qkv_g.shape
    nb = L // ATT_BLOCK
    GCOL = GROUP_COLS

    def body(qkv_all, bias_ref, o_all, lse_all):
        if nb == 1 and d % 2 == 0:
            two_classes(qkv_all, bias_ref, o_all, lse_all)
        else:
            _each_class(d, lambda c: one_class(qkv_all.at[c], bias_ref, o_all.at[c], lse_all.at[c]))

    def two_classes(qkv_all, bias_ref, o_all, lse_all):
        bias2 = [_pair_bias(bias_ref[h, :, ATT_BLOCK:2 * ATT_BLOCK]) for h in range(HEADS_PER_GROUP)]

        def pair(p, _):
            two = pl.ds(2 * p, 2)
            x = qkv_all[two].reshape(2 * ATT_BLOCK, 3 * GCOL)
            qb, kb, vb = x[:, 0:GCOL], x[:, GCOL:2 * GCOL], x[:, 2 * GCOL:3 * GCOL]
            outs, lses = [], []
            for h in range(HEADS_PER_GROUP):
                sl = slice(h * HEAD_DIM, (h + 1) * HEAD_DIM)
                s = lax.dot_general(qb[:, sl], kb[:, sl], _NT, preferred_element_type=F32) * ATT_SCALE + bias2[h]
                m = jnp.max(s, axis=1, keepdims=True)
                e = jnp.exp(s - m)
                ssum = jnp.sum(e, axis=1, keepdims=True)
                outs.append(jnp.dot(e.astype(BF16), vb[:, sl], preferred_element_type=F32) / ssum)
                lses.append(jnp.broadcast_to(m + jnp.log(ssum), (2 * ATT_BLOCK, HEAD_DIM)))
            o_all[two] = jnp.concatenate(outs, axis=1).reshape(2, ATT_BLOCK, GCOL)
            lse_all[two] = jnp.concatenate(lses, axis=1).reshape(2, ATT_BLOCK, GCOL)
            return 0

        lax.fori_loop(0, d // 2, pair, 0, unroll=2)

    def one_class(qkv_ref, bias_ref, o_ref, lse_ref):
        def blk(r0, first):
            nk = ATT_BLOCK if first else 2 * ATT_BLOCK
            k0 = r0 if first else r0 - ATT_BLOCK
            qb = qkv_ref[pl.ds(r0, ATT_BLOCK), 0:GCOL]
            kb = qkv_ref[pl.ds(k0, nk), GCOL:2 * GCOL]
            vb = qkv_ref[pl.ds(k0, nk), 2 * GCOL:3 * GCOL]
            outs, lses = [], []
            for h in range(HEADS_PER_GROUP):
                sl = slice(h * HEAD_DIM, (h + 1) * HEAD_DIM)
                bh = bias_ref[h, :, ATT_BLOCK:2 * ATT_BLOCK] if first else bias_ref[h]
                s = lax.dot_general(qb[:, sl], kb[:, sl], _NT, preferred_element_type=F32) * ATT_SCALE + bh
                m = jnp.max(s, axis=1, keepdims=True)
                e = jnp.exp(s - m)
                ssum = jnp.sum(e, axis=1, keepdims=True)
                outs.append(jnp.dot(e.astype(BF16), vb[:, sl], preferred_element_type=F32) / ssum)
                lses.append(jnp.broadcast_to(m + jnp.log(ssum), (ATT_BLOCK, HEAD_DIM)))
            o_ref[pl.ds(r0, ATT_BLOCK), :] = jnp.concatenate(outs, axis=1)
            lse_ref[pl.ds(r0, ATT_BLOCK), :] = jnp.concatenate(lses, axis=1)

        blk(0, True)
        if nb > 1:
            def loop(n, _):
                blk(pl.multiple_of(n * ATT_BLOCK, ATT_BLOCK), False)
                return 0

            lax.fori_loop(1, nb, loop, 0, unroll=ATT_UNROLL)

    spec = lambda cols: pl.BlockSpec((None, d, L, cols), lambda b: (b, 0, 0, 0))
    oshape = jax.ShapeDtypeStruct((B, d, L, GCOL), F32)
    return pl.pallas_call(
        body, name=name, out_shape=(oshape, oshape), grid=(B,),
        in_specs=[spec(3 * GCOL), pl.BlockSpec((HEADS_PER_GROUP, ATT_BLOCK, 2 * ATT_BLOCK), lambda b: (0, 0, 0))],
        out_specs=(spec(GCOL), spec(GCOL)), compiler_params=_params(("parallel",)),
    )(qkv_g, bias)


def _attn_bwd(qkv_g, doc_g, lse_g, dd_g, bias, *, name):
    B, d, L, _ = qkv_g.shape
    nb = L // ATT_BLOCK
    GCOL = GROUP_COLS

    def body(qkv_all, doc_all, lse_all, dd_all, bias_ref, dqkv_all, dbias_ref, dk_ref, dv_ref):
        @pl.when(pl.program_id(0) == 0)
        def _():
            dbias_ref[...] = jnp.zeros_like(dbias_ref)

        if nb == 1 and d % 2 == 0:
            two_classes(qkv_all, doc_all, lse_all, dd_all, bias_ref, dqkv_all, dbias_ref)
        else:
            _each_class(d, lambda c: one_class(qkv_all.at[c], doc_all.at[c], lse_all.at[c], dd_all.at[c], bias_ref,
                                               dqkv_all.at[c], dbias_ref, dk_ref, dv_ref))

    def two_classes(qkv_all, doc_all, lse_all, dd_all, bias_ref, dqkv_all, dbias_ref):
        bias2 = [_pair_bias(bias_ref[h, :, ATT_BLOCK:2 * ATT_BLOCK]) for h in range(HEADS_PER_GROUP)]
        A = ATT_BLOCK

        def pair(p, _):
            two = pl.ds(2 * p, 2)
            x = qkv_all[two].reshape(2 * A, 3 * GCOL)
            qb, kb, vb = x[:, 0:GCOL], x[:, GCOL:2 * GCOL], x[:, 2 * GCOL:3 * GCOL]
            dob = doc_all[two].reshape(2 * A, GCOL).astype(BF16)
            lse = lse_all[two].reshape(2 * A, GCOL)
            dd = dd_all[two].reshape(2 * A, GCOL)
            dqs, dks, dvs = [], [], []
            for h in range(HEADS_PER_GROUP):
                sl = slice(h * HEAD_DIM, (h + 1) * HEAD_DIM)
                c0 = h * HEAD_DIM
                s = lax.dot_general(qb[:, sl], kb[:, sl], _NT, preferred_element_type=F32) * ATT_SCALE + bias2[h]
                pr = jnp.exp(s - lse[:, c0:c0 + 1])
                dp = lax.dot_general(dob[:, sl], vb[:, sl], _NT, preferred_element_type=F32)
                ds = pr * (dp - dd[:, c0:c0 + 1])
                dbias_ref[h, :, A:2 * A] += ds[0:A, 0:A] + ds[A:2 * A, A:2 * A]
                dsb = ds.astype(BF16)
                dqs.append(jnp.dot(dsb, kb[:, sl], preferred_element_type=F32) * ATT_SCALE)
                dks.append(lax.dot_general(dsb, qb[:, sl], _TN, preferred_element_type=F32) * ATT_SCALE)
                dvs.append(lax.dot_general(pr.astype(BF16), dob[:, sl], _TN, preferred_element_type=F32))
            dqkv_all[two] = jnp.concatenate(dqs + dks + dvs, axis=1).astype(BF16).reshape(2, A, 3 * GCOL)
            return 0

        lax.fori_loop(0, d // 2, pair, 0, unroll=2)

    def one_class(qkv_ref, doc_ref, lse_ref, dd_ref, bias_ref, dqkv_ref, dbias_ref, dk_ref, dv_ref):
        dk_ref[...] = jnp.zeros_like(dk_ref)
        dv_ref[...] = jnp.zeros_like(dv_ref)

        def blk(r0, first):
            nk = ATT_BLOCK if first else 2 * ATT_BLOCK
            k0 = r0 if first else r0 - ATT_BLOCK
            rows = pl.ds(r0, ATT_BLOCK)
            krows = pl.ds(k0, nk)
            qb = qkv_ref[rows, 0:GCOL]
            kb = qkv_ref[krows, GCOL:2 * GCOL]
            vb = qkv_ref[krows, 2 * GCOL:3 * GCOL]
            dob = doc_ref[rows, :].astype(BF16)
            lse = lse_ref[rows, :]
            dd = dd_ref[rows, :]
            dqs, dks, dvs = [], [], []
            for h in range(HEADS_PER_GROUP):
                sl = slice(h * HEAD_DIM, (h + 1) * HEAD_DIM)
                c0 = h * HEAD_DIM
                bh = bias_ref[h, :, ATT_BLOCK:2 * ATT_BLOCK] if first else bias_ref[h]
                s = lax.dot_general(qb[:, sl], kb[:, sl], _NT, preferred_element_type=F32) * ATT_SCALE + bh
                p = jnp.exp(s - lse[:, c0:c0 + 1])
                dp = lax.dot_general(dob[:, sl], vb[:, sl], _NT, preferred_element_type=F32)
                ds = p * (dp - dd[:, c0:c0 + 1])
                if first:
                    dbias_ref[h, :, ATT_BLOCK:2 * ATT_BLOCK] += ds
                else:
                    dbias_ref[h] += ds
                dsb = ds.astype(BF16)
                dqs.append(jnp.dot(dsb, kb[:, sl], preferred_element_type=F32) * ATT_SCALE)
                dks.append(lax.dot_general(dsb, qb[:, sl], _TN, preferred_element_type=F32) * ATT_SCALE)
                dvs.append(lax.dot_general(p.astype(BF16), dob[:, sl], _TN, preferred_element_type=F32))
            dqkv_ref[rows, 0:GCOL] = jnp.concatenate(dqs, axis=1).astype(BF16)
            dk_ref[krows, :] += jnp.concatenate(dks, axis=1)
            dv_ref[krows, :] += jnp.concatenate(dvs, axis=1)

        blk(0, True)
        if nb > 1:
            def loop(n, _):
                blk(pl.multiple_of(n * ATT_BLOCK, ATT_BLOCK), False)
                return 0

            lax.fori_loop(1, nb, loop, 0, unroll=ATT_UNROLL)
        dqkv_ref[:, GCOL:2 * GCOL] = dk_ref[...].astype(BF16)
        dqkv_ref[:, 2 * GCOL:3 * GCOL] = dv_ref[...].astype(BF16)

    spec = lambda cols: pl.BlockSpec((None, d, L, cols), lambda b: (b, 0, 0, 0))
    bspec = pl.BlockSpec((HEADS_PER_GROUP, ATT_BLOCK, 2 * ATT_BLOCK), lambda b: (0, 0, 0))
    return pl.pallas_call(
        body, name=name,
        out_shape=(jax.ShapeDtypeStruct((B, d, L, 3 * GCOL), BF16),
                   jax.ShapeDtypeStruct((HEADS_PER_GROUP, ATT_BLOCK, 2 * ATT_BLOCK), F32)),
        grid=(B,),
        in_specs=[spec(3 * GCOL), spec(GCOL), spec(GCOL), spec(GCOL), bspec],
        out_specs=(spec(3 * GCOL), bspec),
        scratch_shapes=[pltpu.VMEM((L, GCOL), F32), pltpu.VMEM((L, GCOL), F32)],
        compiler_params=_params(("arbitrary",)),
    )(qkv_g, doc_g, lse_g, dd_g, bias)


def _attn_combine(os_, lses, *, name):
    T, GC = os_[0].shape
    n = len(os_)

    def body(*refs):
        o_refs, l_refs, oc_ref, lt_ref = refs[:n], refs[n:2 * n], refs[2 * n], refs[2 * n + 1]
        ls = [r[...] for r in l_refs]
        m = functools.reduce(jnp.maximum, ls)
        ws = [jnp.exp(l - m) for l in ls]
        tot = functools.reduce(jnp.add, ws)
        acc = functools.reduce(jnp.add, [w * r[...] for w, r in zip(ws, o_refs)])
        oc_ref[...] = acc / tot
        lt_ref[...] = m + jnp.log(tot)

    shp = jax.ShapeDtypeStruct((T, GC), F32)
    return pl.pallas_call(body, name=name, out_shape=(shp, shp), grid=(T // ROW_TILE,),
                          in_specs=[_row_spec(GC)] * (2 * n), out_specs=(_row_spec(GC), _row_spec(GC)),
                          compiler_params=_params(("parallel",)))(*os_, *lses)


def _attn_rowdot(doc, oc, *, name):
    T, GC = oc.shape

    def body(doc_ref, oc_ref, dd_ref):
        prod = doc_ref[...].astype(F32) * oc_ref[...]
        parts = []
        for h in range(GC // HEAD_DIM):
            s = jnp.sum(prod[:, h * HEAD_DIM:(h + 1) * HEAD_DIM], axis=1, keepdims=True)
            parts.append(jnp.broadcast_to(s, (ROW_TILE, HEAD_DIM)))
        dd_ref[...] = jnp.concatenate(parts, axis=1)

    return pl.pallas_call(body, name=name, out_shape=jax.ShapeDtypeStruct((T, GC), F32), grid=(T // ROW_TILE,),
                          in_specs=[_row_spec(GC), _row_spec(GC)], out_specs=_row_spec(GC),
                          compiler_params=_params(("parallel",)))(doc, oc)


def _bucket_sum(dbias, buckets, *, name):
    depth, NH = dbias.shape[:2]

    def body(db_ref, bk_ref, out_ref):
        rows = lax.broadcasted_iota(jnp.int32, (REL_BUCKETS, LANE), 0)
        cols = lax.broadcasted_iota(jnp.int32, (REL_BUCKETS, LANE), 1)

        def per_bucket(b, acc):
            for h in range(NH):
                sel = bk_ref[h // HEADS_PER_GROUP] == b
                tot = functools.reduce(jnp.add, [db_ref[l, h] for l in range(depth)])
                s = jnp.sum(jnp.where(sel, tot, 0.0))
                acc = acc + jnp.where(jnp.logical_and(rows == b, cols == h), s, 0.0)
            return acc

        out_ref[...] = lax.fori_loop(0, REL_BUCKETS, per_bucket, jnp.zeros((REL_BUCKETS, LANE), F32))

    return pl.pallas_call(body, name=name, out_shape=jax.ShapeDtypeStruct((REL_BUCKETS, LANE), F32),
                          compiler_params=pltpu.CompilerParams(vmem_limit_bytes=VMEM_LIMIT))(dbias, buckets)


def _merge_fwd(zg, bg, ya, yb, yc, *, name):
    T, D3 = zg.shape
    D = D3 // 3

    def body(zg_ref, bg_ref, ya_ref, yb_ref, yc_ref, out_ref):
        acc = None
        for i, y_ref in enumerate((ya_ref, yb_ref, yc_ref)):
            g = _sigmoid(zg_ref[:, i * D:(i + 1) * D].astype(F32) + bg_ref[:, i * D:(i + 1) * D])
            t = g * y_ref[...].astype(F32)
            acc = t if acc is None else acc + t
        out_ref[...] = acc.astype(BF16)

    return pl.pallas_call(body, name=name, out_shape=jax.ShapeDtypeStruct((T, D), BF16), grid=(T // ROW_TILE,),
                          in_specs=[_row_spec(D3), _vec_spec(D3), _row_spec(D), _row_spec(D), _row_spec(D)],
                          out_specs=_row_spec(D), compiler_params=_params(("parallel",)))(zg, bg.reshape(1, D3), ya, yb, yc)


def _merge_bwd(dm, zg, bg, ya, yb, yc, *, name):
    T, D3 = zg.shape
    D = D3 // 3

    def body(dm_ref, zg_ref, bg_ref, ya_ref, yb_ref, yc_ref, dya_ref, dyb_ref, dyc_ref, dzg_ref, dbg_ref):
        first = pl.program_id(0) == 0
        dm_v = dm_ref[...].astype(F32)
        dbs = []
        for i, (y_ref, dy_ref) in enumerate(((ya_ref, dya_ref), (yb_ref, dyb_ref), (yc_ref, dyc_ref))):
            g = _sigmoid(zg_ref[:, i * D:(i + 1) * D].astype(F32) + bg_ref[:, i * D:(i + 1) * D])
            dy_ref[...] = (dm_v * g).astype(BF16)
            dz = dm_v * y_ref[...].astype(F32) * g * (1.0 - g)
            dzg_ref[:, i * D:(i + 1) * D] = dz.astype(BF16)
            dbs.append(jnp.sum(dz, axis=0, keepdims=True))
        _acc_out(dbg_ref, jnp.concatenate(dbs, axis=1), first)

    bshape = jax.ShapeDtypeStruct((T, D), BF16)
    return pl.pallas_call(
        body, name=name,
        out_shape=(bshape, bshape, bshape, jax.ShapeDtypeStruct((T, D3), BF16), jax.ShapeDtypeStruct((1, D3), F32)),
        grid=(T // ROW_TILE,),
        in_specs=[_row_spec(D), _row_spec(D3), _vec_spec(D3), _row_spec(D), _row_spec(D), _row_spec(D)],
        out_specs=(_row_spec(D), _row_spec(D), _row_spec(D), _row_spec(D3), _vec_spec(D3)),
        compiler_params=_params(("arbitrary",)))(dm, zg, bg.reshape(1, D3), ya, yb, yc)


XQ_TILE = 512


def _xattn_fwd(q, kv, *, B, S, name):
    D = q.shape[1]
    M = kv.shape[0] // B
    E = D // X_HEADS
    scale = E ** -0.5

    def body(q_ref, kv_ref, o_ref):
        outs = []
        for h in range(X_HEADS):
            s = lax.dot_general(q_ref[:, h * E:(h + 1) * E], kv_ref[:, h * E:(h + 1) * E], _NT,
                                preferred_element_type=F32) * scale
            e = jnp.exp(s - jnp.max(s, axis=1, keepdims=True))
            p = e / jnp.sum(e, axis=1, keepdims=True)
            outs.append(jnp.dot(p.astype(BF16), kv_ref[:, D + h * E:D + (h + 1) * E], preferred_element_type=F32))
        o_ref[...] = jnp.concatenate(outs, axis=1).astype(BF16)

    nq = S // XQ_TILE
    return pl.pallas_call(
        body, name=name, out_shape=jax.ShapeDtypeStruct((B * S, D), BF16), grid=(B, nq),
        in_specs=[pl.BlockSpec((XQ_TILE, D), lambda b, i: (b * nq + i, 0)), pl.BlockSpec((M, 2 * D), lambda b, i: (b, 0))],
        out_specs=pl.BlockSpec((XQ_TILE, D), lambda b, i: (b * nq + i, 0)),
        compiler_params=_params(("parallel", "parallel")))(q, kv)


def _xattn_bwd(q, kv, do, *, B, S, name):
    D = q.shape[1]
    M = kv.shape[0] // B
    E = D // X_HEADS
    scale = E ** -0.5

    def body(q_ref, kv_ref, do_ref, dq_ref, dkv_ref):
        first = pl.program_id(1) == 0
        dqs, dks, dvs = [], [], []
        for h in range(X_HEADS):
            qh = q_ref[:, h * E:(h + 1) * E]
            kh = kv_ref[:, h * E:(h + 1) * E]
            vh = kv_ref[:, D + h * E:D + (h + 1) * E]
            doh = do_ref[:, h * E:(h + 1) * E]
            s = lax.dot_general(qh, kh, _NT, preferred_element_type=F32) * scale
            e = jnp.exp(s - jnp.max(s, axis=1, keepdims=True))
            p = e / jnp.sum(e, axis=1, keepdims=True)
            dp = lax.dot_general(doh, vh, _NT, preferred_element_type=F32)
            ds = (p * (dp - jnp.sum(p * dp, axis=1, keepdims=True))).astype(BF16)
            dqs.append(jnp.dot(ds, kh, preferred_element_type=F32) * scale)
            dks.append(lax.dot_general(ds, qh, _TN, preferred_element_type=F32) * scale)
            dvs.append(lax.dot_general(p.astype(BF16), doh, _TN, preferred_element_type=F32))
        dq_ref[...] = jnp.concatenate(dqs, axis=1).astype(BF16)
        _acc_out(dkv_ref, jnp.concatenate(dks + dvs, axis=1), first)

    nq = S // XQ_TILE
    qspec = pl.BlockSpec((XQ_TILE, D), lambda b, i: (b * nq + i, 0))
    kvspec = pl.BlockSpec((M, 2 * D), lambda b, i: (b, 0))
    return pl.pallas_call(
        body, name=name,
        out_shape=(jax.ShapeDtypeStruct((B * S, D), BF16), jax.ShapeDtypeStruct((B * M, 2 * D), F32)),
        grid=(B, nq), in_specs=[qspec, kvspec, qspec], out_specs=(qspec, kvspec),
        compiler_params=_params(("arbitrary", "arbitrary")))(q, kv, do)


FFN_COLS = 256
FFN_PAD = 8
FFN_CH = 256
FFN_K = 3


def _ffn_gate(win, w_ref, cb_ref, n):
    g = jnp.broadcast_to(cb_ref[...], (n, win.shape[1]))
    for j in range(FFN_K):
        off = FFN_PAD - (FFN_K - 1) + j
        g = g + w_ref[j:j + 1, :] * win[off:off + n, :]
    return g


def _ffn_interleave(w):
    lead, n2 = w.shape[:-1], w.shape[-1]
    nc = n2 // (2 * FFN_COLS)
    return jnp.swapaxes(w.reshape(lead + (2, nc, FFN_COLS)), -3, -2).reshape(lead + (n2,))


def _ffn_deinterleave(w):
    lead, n2 = w.shape[:-1], w.shape[-1]
    nc = n2 // (2 * FFN_COLS)
    return jnp.swapaxes(w.reshape(lead + (nc, 2, FFN_COLS)), -3, -2).reshape(lead + (n2,))


def _ffn_fwd(up, conv_w, conv_b, *, B, S, name):
    F = conv_w.shape[1]
    nc = F // FFN_COLS

    def body(gp_ref, val_ref, w_ref, cb_ref, act_ref, pad_ref):
        pad_ref[0:FFN_PAD, :] = jnp.zeros((FFN_PAD, FFN_COLS), F32)
        pad_ref[FFN_PAD:, :] = gp_ref[...].astype(F32)

        def chunk(i, _):
            base = pl.multiple_of(i * FFN_CH, FFN_CH)
            gate = _ffn_gate(pad_ref[pl.ds(base, FFN_CH + FFN_PAD), :], w_ref, cb_ref, FFN_CH)
            act_ref[pl.ds(base, FFN_CH), :] = (_gelu(gate) * val_ref[pl.ds(base, FFN_CH), :].astype(F32)).astype(BF16)
            return 0

        lax.fori_loop(0, S // FFN_CH, chunk, 0)

    return pl.pallas_call(
        body, name=name, out_shape=jax.ShapeDtypeStruct((B * S, F), BF16), grid=(B, nc),
        in_specs=[pl.BlockSpec((S, FFN_COLS), lambda b, j: (b, 2 * j)), pl.BlockSpec((S, FFN_COLS), lambda b, j: (b, 2 * j + 1)),
                  pl.BlockSpec((FFN_K, FFN_COLS), lambda b, j: (0, j)), pl.BlockSpec((1, FFN_COLS), lambda b, j: (0, j))],
        out_specs=pl.BlockSpec((S, FFN_COLS), lambda b, j: (b, j)),
        scratch_shapes=[pltpu.VMEM((S + FFN_PAD, FFN_COLS), F32)],
        compiler_params=_params(("parallel", "parallel")))(up, up, conv_w, conv_b.reshape(1, F))


def _ffn_bwd(dact, up, conv_w, conv_b, *, B, S, name):
    F = conv_w.shape[1]
    nc = F // FFN_COLS
    SUB = 8

    def body(dact_ref, gp_ref, val_ref, w_ref, cb_ref, dup_ref, dw_ref, dcb_ref, pad_ref, pad2_ref, acc_ref):
        first = pl.program_id(1) == 0
        pad_ref[0:FFN_PAD, :] = jnp.zeros((FFN_PAD, FFN_COLS), F32)
        pad_ref[FFN_PAD:, :] = gp_ref[...].astype(F32)
        pad2_ref[S:, :] = jnp.zeros((FFN_PAD, FFN_COLS), F32)
        acc_ref[...] = jnp.zeros_like(acc_ref)

        def chunk1(i, _):
            base = pl.multiple_of(i * FFN_CH, FFN_CH)
            rows = pl.ds(base, FFN_CH)
            gate = _ffn_gate(pad_ref[pl.ds(base, FFN_CH + FFN_PAD), :], w_ref, cb_ref, FFN_CH)
            gl, dgl = _gelu_and_grad(gate)
            da = dact_ref[rows, :].astype(F32)
            dup_ref[rows, FFN_COLS:2 * FFN_COLS] = (da * gl).astype(BF16)
            pad2_ref[rows, :] = da * val_ref[rows, :].astype(F32) * dgl
            return 0

        lax.fori_loop(0, S // FFN_CH, chunk1, 0)

        def chunk2(i, _):
            base = pl.multiple_of(i * FFN_CH, FFN_CH)
            rows = pl.ds(base, FFN_CH)
            win2 = pad2_ref[pl.ds(base, FFN_CH + FFN_PAD), :]
            win1 = pad_ref[pl.ds(base, FFN_CH + FFN_PAD), :]
            dg = win2[0:FFN_CH, :]
            dgp = jnp.zeros((FFN_CH, FFN_COLS), F32)
            for j in range(FFN_K):
                off2 = FFN_K - 1 - j
                dgp = dgp + w_ref[j:j + 1, :] * win2[off2:off2 + FFN_CH, :]
                off1 = FFN_PAD - (FFN_K - 1) + j
                prod = dg * win1[off1:off1 + FFN_CH, :]
                acc_ref[j * SUB:(j + 1) * SUB, :] += jnp.sum(prod.reshape(FFN_CH // SUB, SUB, FFN_COLS), axis=0)
            acc_ref[FFN_K * SUB:(FFN_K + 1) * SUB, :] += jnp.sum(dg.reshape(FFN_CH // SUB, SUB, FFN_COLS), axis=0)
            dup_ref[rows, 0:FFN_COLS] = dgp.astype(BF16)
            return 0

        lax.fori_loop(0, S // FFN_CH, chunk2, 0)
        sums = jnp.sum(acc_ref[...].reshape(FFN_K + 1, SUB, FFN_COLS), axis=1)
        _acc_out(dw_ref, sums[0:FFN_K, :], first)
        _acc_out(dcb_ref, sums[FFN_K:FFN_K + 1, :], first)

    return pl.pallas_call(
        body, name=name,
        out_shape=(jax.ShapeDtypeStruct((B * S, 2 * F), BF16),
                   jax.ShapeDtypeStruct((FFN_K, F), F32), jax.ShapeDtypeStruct((1, F), F32)),
        grid=(nc, B),
        in_specs=[pl.BlockSpec((S, FFN_COLS), lambda j, b: (b, j)), pl.BlockSpec((S, FFN_COLS), lambda j, b: (b, 2 * j)),
                  pl.BlockSpec((S, FFN_COLS), lambda j, b: (b, 2 * j + 1)),
                  pl.BlockSpec((FFN_K, FFN_COLS), lambda j, b: (0, j)), pl.BlockSpec((1, FFN_COLS), lambda j, b: (0, j))],
        out_specs=(pl.BlockSpec((S, 2 * FFN_COLS), lambda j, b: (b, j)),
                   pl.BlockSpec((FFN_K, FFN_COLS), lambda j, b: (0, j)), pl.BlockSpec((1, FFN_COLS), lambda j, b: (0, j))),
        scratch_shapes=[pltpu.VMEM((S + FFN_PAD, FFN_COLS), F32), pltpu.VMEM((S + FFN_PAD, FFN_COLS), F32),
                        pltpu.VMEM(((FFN_K + 1) * SUB, FFN_COLS), F32)],
        compiler_params=_params(("arbitrary", "arbitrary")))(dact, up, up, conv_w, conv_b.reshape(1, F))


def _row_tile(rows, row_bytes, budget):
    tr = rows
    if rows * row_bytes > budget:
        for t in range(16, rows, 16):
            if rows % t == 0 and t * row_bytes <= budget:
                tr = t
    return tr


def _adamw(w, parts, m, v, *, name, own=None, own_slot=None):
    shape = w.shape
    L = len(parts)
    n = parts[0].shape[0]
    cols = shape[-1]
    rows = w.size // (cols * L)
    w3, m3, v3 = (t.reshape(L, rows, cols) for t in (w, m, v))
    tr = _row_tile(rows, cols * 4, 1 << 19)
    c1 = 1.0 - ADAM_B1 ** ADAM_STEP
    c2 = 1.0 - ADAM_B2 ** ADAM_STEP
    has_own = own is not None

    def body(*refs):
        refs = list(refs)
        slot_ref = refs.pop(0) if has_own else None
        w_ref = refs.pop(0)
        p_refs = [refs.pop(0) for _ in range(L)]
        o_refs = [refs.pop(0) for _ in range(L)] if has_own else None
        m_ref, v_ref, g_ref, d_ref, mo_ref, vo_ref = refs
        layer = pl.program_id(0)
        gv = None
        for j in range(L):
            gj = None
            for i in range(n):
                t = p_refs[j][i].astype(F32)
                if has_own:
                    t = jnp.where(slot_ref[0] == i, o_refs[j][...].astype(F32), t)
                gj = t if gj is None else gj + t
            gv = gj if gv is None else jnp.where(layer == j, gj, gv)
        g_ref[...] = gv
        mn = ADAM_B1 * m_ref[...] + (1.0 - ADAM_B1) * gv
        vn = ADAM_B2 * v_ref[...] + (1.0 - ADAM_B2) * (gv * gv)
        d_ref[...] = -ADAM_LR * ((mn / c1) / (jnp.sqrt(vn / c2) + ADAM_EPS) + ADAM_WD * w_ref[...])
        mo_ref[...] = mn
        vo_ref[...] = vn

    spec = pl.BlockSpec((None, tr, cols), lambda l, i, *_: (l, i, 0))

    def pspec(j):
        return pl.BlockSpec((n, tr, cols), lambda l, i, *_: (0, jnp.where(l == j, i, 0), 0))

    def ospec(j):
        return pl.BlockSpec((None, tr, cols), lambda l, i, slot: (slot[0], jnp.where(l == j, i, 0), 0))

    oshape = jax.ShapeDtypeStruct((L, rows, cols), F32)
    in_specs = [spec] + [pspec(j) for j in range(L)] + ([ospec(j) for j in range(L)] if has_own else []) + [spec, spec]
    args = [w3] + [p.reshape(n, rows, cols) for p in parts]
    if has_own:
        args += [o.reshape(n, rows, cols) for o in own]
    args += [m3, v3]
    grid = (L, rows // tr)
    if has_own:
        grid_spec = pltpu.PrefetchScalarGridSpec(num_scalar_prefetch=1, grid=grid, in_specs=in_specs,
                                                 out_specs=(spec,) * 4)
        outs = pl.pallas_call(body, name=name, out_shape=(oshape,) * 4, grid_spec=grid_spec,
                              compiler_params=_params(("parallel", "parallel")))(own_slot, *args)
    else:
        outs = pl.pallas_call(body, name=name, out_shape=(oshape,) * 4, grid=grid, in_specs=in_specs,
                              out_specs=(spec,) * 4, compiler_params=_params(("parallel", "parallel")))(*args)
    return tuple(t.reshape(shape) for t in outs)


def _chip_sum(g, got, core, *, name):
    shard = got.shape[1:]
    cols = shard[-1]
    rows = math.prod(shard) // cols
    tr = _row_tile(rows, cols * 4, 1 << 20)

    def body(c_ref, g_ref, r_ref, o_ref):
        o_ref[...] = (g_ref[...].astype(F32) + r_ref[...].astype(F32)).astype(o_ref.dtype)

    blk = (None, tr, cols)
    grid_spec = pltpu.PrefetchScalarGridSpec(
        num_scalar_prefetch=1, grid=(N_CHIP, rows // tr),
        in_specs=[pl.BlockSpec(blk, lambda s, i, c: (2 * s + c[0], i, 0)), pl.BlockSpec(blk, lambda s, i, c: (s, i, 0))],
        out_specs=pl.BlockSpec(blk, lambda s, i, c: (s, i, 0)))
    out = pl.pallas_call(body, name=name, out_shape=jax.ShapeDtypeStruct((N_CHIP, rows, cols), g.dtype),
                         grid_spec=grid_spec, compiler_params=_params(("parallel", "parallel")),
                         )(core, g.reshape(N_DEV, rows, cols), got.reshape(N_CHIP, rows, cols))
    return out.reshape((N_CHIP,) + shard)


_HBM = pl.BlockSpec(memory_space=pltpu.HBM)
_MESH = pl.DeviceIdType.MESH


def _my_pos():
    return lax.axis_index("x"), lax.axis_index("y"), lax.axis_index("c")


def _flip(pos, k):
    x, y, c = pos
    fx, fy, fc = (k >> 2) & 1, (k >> 1) & 1, k & 1
    return (x ^ fx if fx else x, y ^ fy if fy else y, c ^ fc if fc else c)


def _index_of(pos):
    return 4 * pos[0] + 2 * pos[1] + pos[2]


def _all_gather(xs, *, name):
    n = len(xs)

    def body(*refs):
        x_refs, out_refs = refs[:n], refs[n:2 * n]
        send_sems, recv_sems = refs[2 * n:]
        me = _my_pos()
        sibling = _flip(me, 1)
        chips = [2, 4, 6]

        def copy(i, k, block_pos, to, from_x=False):
            blk = out_refs[i].at[_index_of(block_pos)]
            return pltpu.make_async_remote_copy(src_ref=x_refs[i] if from_x else blk, dst_ref=blk,
                                                send_sem=send_sems.at[k, i], recv_sem=recv_sems.at[k, i],
                                                device_id=to, device_id_type=_MESH)

        first = [copy(i, 1 + j, me, _flip(me, f), from_x=True) for j, f in enumerate(chips) for i in range(n)]
        first += [copy(i, 0, me, sibling, from_x=True) for i in range(n)]
        for cp in first:
            cp.start()
        passed = []
        for j, f in enumerate(chips):
            for i in range(n):
                copy(i, 1 + j, _flip(me, f), me).wait_recv()
                cp = copy(i, 4 + j, _flip(me, f), sibling)
                cp.start()
                passed.append(cp)
        for i in range(n):
            copy(i, 0, sibling, me).wait_recv()
        for j, f in enumerate(chips):
            for i in range(n):
                copy(i, 4 + j, _flip(sibling, f), me).wait_recv()
        for cp in first + passed:
            cp.wait_send()

    return pl.pallas_call(
        body, name=name, out_shape=tuple(jax.ShapeDtypeStruct((N_DEV,) + x.shape, x.dtype) for x in xs),
        in_specs=[_HBM] * n, out_specs=(_HBM,) * n,
        scratch_shapes=[pltpu.SemaphoreType.DMA((7, n)), pltpu.SemaphoreType.DMA((7, n))],
    )(*xs)


N_CHIP = 4


def _scatter_d2d(gs):
    n = len(gs)

    def make(g_refs, recv_refs, send_sems, recv_sems):
        me = _my_pos()
        sibling = _flip(me, 1)
        c = me[2]
        sends = []
        for i in range(n):
            for s in range(N_CHIP):
                sends.append(pltpu.make_async_remote_copy(
                    src_ref=g_refs[i].at[2 * s + 1 - c], dst_ref=recv_refs[i].at[s],
                    send_sem=send_sems.at[N_CHIP * i + s], recv_sem=recv_sems.at[N_CHIP * i + s],
                    device_id=sibling, device_id_type=_MESH))
        return sends, sends

    shapes = [jax.ShapeDtypeStruct((N_CHIP,) + g.shape[1:], g.dtype) for g in gs]
    return _Exchange(list(gs), shapes, N_CHIP * n, make)


def _scatter_ici(ss):
    n = len(ss)

    def make(s_refs, r_refs, send_sems, recv_sems):
        me = _my_pos()
        my_chip = 2 * me[0] + me[1]
        sends, recvs = [], []
        for k in (1, 2, 3):
            peer = _flip(me, 2 * k)
            peer_chip = 2 * peer[0] + peer[1]
            for i in range(n):
                j = 3 * i + k - 1
                sends.append(pltpu.make_async_remote_copy(
                    src_ref=s_refs[i].at[peer_chip], dst_ref=r_refs[i].at[my_chip], send_sem=send_sems.at[j],
                    recv_sem=recv_sems.at[j], device_id=peer, device_id_type=_MESH))
                recvs.append(pltpu.make_async_remote_copy(
                    src_ref=s_refs[i].at[my_chip], dst_ref=r_refs[i].at[peer_chip], send_sem=send_sems.at[j],
                    recv_sem=recv_sems.at[j], device_id=peer, device_id_type=_MESH))
        return sends, recvs

    return _Exchange(list(ss), [jax.ShapeDtypeStruct(s.shape, s.dtype) for s in ss], 3 * n, make)


def _gather_ici(shards, layers):
    n = len(shards)
    FLIPS = (2, 4, 6, 1)

    def make(x_refs, out_refs, send_sems, recv_sems):
        me = _my_pos()
        sends, recvs = [], []
        for k, f in enumerate(FLIPS):
            peer = _flip(me, f)
            for i in range(n):
                src = x_refs[i] if layers[i] is None else x_refs[i].at[layers[i]]
                j = len(FLIPS) * i + k
                sends.append(pltpu.make_async_remote_copy(
                    src_ref=src, dst_ref=out_refs[i].at[_index_of(me)], send_sem=send_sems.at[j],
                    recv_sem=recv_sems.at[j], device_id=peer, device_id_type=_MESH))
                recvs.append(pltpu.make_async_remote_copy(
                    src_ref=src, dst_ref=out_refs[i].at[_index_of(peer)], send_sem=send_sems.at[j],
                    recv_sem=recv_sems.at[j], device_id=peer, device_id_type=_MESH))
        return sends, recvs

    shapes = [jax.ShapeDtypeStruct((N_DEV,) + (x.shape if l is None else x.shape[1:]), x.dtype)
              for x, l in zip(shards, layers)]
    return _Exchange(list(shards), shapes, len(FLIPS) * n, make)


def _gather_d2d(blocks):
    n = len(blocks)
    FLIPS = (2, 4, 6)

    def make(in_refs, out_refs, send_sems, recv_sems):
        me = _my_pos()
        sibling = _flip(me, 1)
        sends, recvs = [], []
        for k, f in enumerate(FLIPS):
            for i in range(n):
                j = len(FLIPS) * i + k
                mine = out_refs[i].at[_index_of(_flip(me, f))]
                sends.append(pltpu.make_async_remote_copy(
                    src_ref=mine, dst_ref=mine, send_sem=send_sems.at[j], recv_sem=recv_sems.at[j],
                    device_id=sibling, device_id_type=_MESH))
                recvs.append(pltpu.make_async_remote_copy(
                    src_ref=mine, dst_ref=out_refs[i].at[_index_of(_flip(sibling, f))], send_sem=send_sems.at[j],
                    recv_sem=recv_sems.at[j], device_id=sibling, device_id_type=_MESH))
        return sends, recvs

    shapes = [jax.ShapeDtypeStruct(b.shape, b.dtype) for b in blocks]
    return _Exchange(list(blocks), shapes, len(FLIPS) * n, make, aliases={i: i for i in range(n)})


class _Exchange:
    def __init__(self, ins, out_shapes, n_copies, make, aliases=None):
        self.ins, self.out_shapes, self.n_copies, self.make, self.aliases = ins, out_shapes, n_copies, make, aliases or {}


def _exchange_call(ex, *, name):
    n_in, n_out = len(ex.ins), len(ex.out_shapes)

    def body(*refs):
        sends, recvs = ex.make(refs[:n_in], refs[n_in:n_in + n_out], refs[n_in + n_out], refs[n_in + n_out + 1])
        for cp in sends:
            cp.start()
        for cp in recvs:
            cp.wait_recv()
        for cp in sends:
            cp.wait_send()

    return pl.pallas_call(
        body, name=name, out_shape=tuple(ex.out_shapes), in_specs=[_HBM] * n_in, out_specs=(_HBM,) * n_out,
        scratch_shapes=[pltpu.SemaphoreType.DMA((ex.n_copies,)), pltpu.SemaphoreType.DMA((ex.n_copies,))],
        input_output_aliases=ex.aliases,
    )(*ex.ins)


class _Hosts:
    def __init__(self):
        self.plan = {}

    def add(self, name, build, done):
        assert name not in self.plan
        self.plan[name] = (build, done)

    def get(self, name):
        return self.plan[name][0]() if name in self.plan else None

    def put(self, name, outs):
        self.plan.pop(name)[1](outs)


def _t5_bucket(dist):
    n = jnp.maximum(dist, 0)
    max_exact = REL_BUCKETS // 2
    nf = jnp.maximum(n, 1).astype(F32)
    large = max_exact + (jnp.log(nf / max_exact) / math.log(REL_MAX_DIST / max_exact)
                         * (REL_BUCKETS - max_exact)).astype(jnp.int32)
    large = jnp.minimum(large, REL_BUCKETS - 1)
    return jnp.where(n < max_exact, n, large)


def _bias_tables(rel_bias):
    qi = jnp.arange(ATT_BLOCK)[:, None]
    ki = jnp.arange(2 * ATT_BLOCK)[None, :]
    rel = qi + ATT_BLOCK - ki
    out = []
    for gi, (window, dil) in enumerate(DIL_GROUPS):
        span = window // dil
        bucket = _t5_bucket(rel * dil)
        valid = (rel >= 0) & (rel <= span)
        rb = rel_bias[:, gi * HEADS_PER_GROUP:(gi + 1) * HEADS_PER_GROUP]
        tab = functools.reduce(jnp.add, [jnp.where((bucket == b)[:, :, None], rb[b], 0.0)
                                         for b in range(REL_BUCKETS)])
        tab = jnp.where(valid[:, :, None], tab, MASK_VALUE).transpose(2, 0, 1)
        out.append((tab.astype(F32), bucket.astype(jnp.int32)))
    return out


def _regroup(t, B, S, d):
    C = t.shape[-1]
    return t.reshape(B, S // d, d, C).swapaxes(1, 2)


def _ungroup(t):
    B, d, L, C = t.shape
    return t.swapaxes(1, 2).reshape(B * L * d, C)


def _group_qkv(qkv, gi):
    n = len(DIL_GROUPS) * GROUP_COLS
    return jnp.concatenate([qkv[:, j * n + gi * GROUP_COLS: j * n + (gi + 1) * GROUP_COLS] for j in range(3)], axis=1)


def _layer_fwd(l, x0, h1, mem2, W, P, tabs, next_pre_g, *, B, S, hosts=None):
    tag = f"l{l}_"
    mm = functools.partial(_mm, hosts=hosts)
    sv = {"x0": x0, "h1": h1}
    za = mm(h1, W["in_a"], out_dtype=ACT, name=tag + "mm_in_a")
    zb = mm(h1, W["in_b"], out_dtype=ACT, name=tag + "mm_in_b")
    qkv = mm(h1, W["in_c"], out_dtype=BF16, name=tag + "mm_in_c")
    zg = mm(h1, W["in_g"], out_dtype=ACT, name=tag + "mm_in_g")
    a4, a2 = _bra_fwd(za, P["conv_a_w"], P["conv_a_b"], P["ln_a_g"], P["ln_a_b"], B=B, S=S, name=tag + "bra_fwd")
    ya = mm(a4, W["a_out"], out_dtype=ACT, name=tag + "mm_a_out")
    b_s_t = P["b_s"].T
    p = _brb_fwd(zb, P["ln_b_g"], P["ln_b_b"], P["w_s"], b_s_t, name=tag + "brb_fwd")
    yb = mm(p, W["b_out"], out_dtype=ACT, name=tag + "mm_b_out")
    os_, lses, qkv_gs = [], [], []
    for gi, (_, dil) in enumerate(DIL_GROUPS):
        qkv_g = _regroup(_group_qkv(qkv, gi), B, S, dil)
        o_g, lse_g = _attn_fwd(qkv_g, tabs[gi][0], name=tag + f"attn_fwd{gi}")
        qkv_gs.append(qkv_g)
        os_.append(_ungroup(o_g))
        lses.append(_ungroup(lse_g))
    oc, lse_tot = _attn_combine(os_, lses, name=tag + "attn_combine")
    yc = mm(oc, W["c_out"], out_dtype=ACT, name=tag + "mm_c_out")
    merged = _merge_fwd(zg, P["b_gate"], ya, yb, yc, name=tag + "merge_fwd")
    y1 = mm(merged, W["mix_out"], out_dtype=ACT, name=tag + "mm_mix")
    x1, h2 = _norm_fwd(x0, y1, P["mix_post_g"], P["x_pre_g"], name=tag + "norm1")
    q = mm(h2, W["xq"], out_dtype=BF16, name=tag + "mm_xq")
    (memn,) = _norm_fwd(mem2, None, None, P["mem_g"], name=tag + "norm_mem")
    kv = mm(memn, W["xkv"], out_dtype=BF16, name=tag + "mm_xkv")
    ox = _xattn_fwd(q, kv, B=B, S=S, name=tag + "xattn_fwd")
    y2 = mm(ox, W["xo"], out_dtype=ACT, name=tag + "mm_xo")
    x2, h3 = _norm_fwd(x1, y2, P["x_post_g"], P["ffn_pre_g"], name=tag + "norm2")
    up = mm(h3, W["up"], out_dtype=ACT, name=tag + "mm_up")
    act = _ffn_fwd(up, P["conv_f_w"], P["conv_f_b"], B=B, S=S, name=tag + "ffn_fwd")
    y3 = mm(act, W["down"], out_dtype=ACT, name=tag + "mm_down")
    outs = _norm_fwd(x2, y3, P["ffn_post_g"], next_pre_g, name=tag + "norm3")
    x3 = outs[0]
    h_next = outs[1] if next_pre_g is not None else None
    sv.update(za=za, zb=zb, zg=zg, a4=a4, a2=a2, ya=ya, p=p, yb=yb, qkv_gs=qkv_gs, oc=oc, lse_tot=lse_tot, yc=yc,
              merged=merged, y1=y1, x1=x1, h2=h2, q=q, memn=memn, kv=kv, ox=ox, y2=y2, x2=x2, h3=h3, up=up, act=act,
              y3=y3, x3=x3, b_s_t=b_s_t)
    return x3, h_next, sv


def _layer_bwd(l, d, dh_next, next_pre_g, sv, mem2, W, WT, P, tabs, G, *, B, S, hosts=None):
    tag = f"l{l}_"
    mm = functools.partial(_mm, hosts=hosts)
    mm_tn = functools.partial(_mm_tn, hosts=hosts)
    if dh_next is not None:
        d, dy3, G["ffn_post_g"], dg_next = _norm_bwd(d, dh_next, sv["x3"], next_pre_g, sv["y3"], P["ffn_post_g"],
                                                     name=tag + "norm3_bwd")
    else:
        dy3, G["ffn_post_g"] = _norm_bwd(d, None, None, None, sv["y3"], P["ffn_post_g"], name=tag + "norm3_bwd")
        dg_next = None
    dact = mm(dy3, WT["down"], out_dtype=ACT, name=tag + "mm_down_dx")
    G["w_down"] = mm_tn(sv["act"], dy3, name=tag + "mm_down_dw")
    dup, G["conv_f_w"], G["conv_f_b"] = _ffn_bwd(dact, sv["up"], P["conv_f_w"], P["conv_f_b"], B=B, S=S,
                                                name=tag + "ffn_bwd")
    dh3 = mm(dup, WT["up"], out_dtype=ACT, name=tag + "mm_up_dx")
    G["w_up"] = _ffn_deinterleave(mm_tn(sv["h3"], dup, name=tag + "mm_up_dw"))
    d, dy2, G["x_post_g"], G["ffn_pre_g"] = _norm_bwd(d, dh3, sv["x2"], P["ffn_pre_g"], sv["y2"], P["x_post_g"],
                                                      name=tag + "norm2_bwd")
    dox = mm(dy2, WT["xo"], out_dtype=BF16, name=tag + "mm_xo_dx")
    G["w_xo"] = mm_tn(sv["ox"], dy2, name=tag + "mm_xo_dw")
    dq, dkv = _xattn_bwd(sv["q"], sv["kv"], dox, B=B, S=S, name=tag + "xattn_bwd")
    dh2 = mm(dq, WT["xq"], out_dtype=ACT, name=tag + "mm_xq_dx")
    G["w_xq"] = mm_tn(sv["h2"], dq, name=tag + "mm_xq_dw")
    G["w_xkv"] = mm_tn(sv["memn"], dkv, name=tag + "mm_xkv_dw")
    dmemn = mm(dkv, WT["xkv"], out_dtype=F32, name=tag + "mm_xkv_dx")
    (G["mem_g"],) = _norm_bwd(None, dmemn, mem2, P["mem_g"], None, None, name=tag + "norm_mem_bwd")
    d, dy1, G["mix_post_g"], G["x_pre_g"] = _norm_bwd(d, dh2, sv["x1"], P["x_pre_g"], sv["y1"], P["mix_post_g"],
                                                      name=tag + "norm1_bwd")
    dm = mm(dy1, WT["mix_out"], out_dtype=ACT, name=tag + "mm_mix_dx")
    G["w_mix_out"] = mm_tn(sv["merged"], dy1, name=tag + "mm_mix_dw")
    dya, dyb, dyc, dzg, dbg = _merge_bwd(dm, sv["zg"], P["b_gate"], sv["ya"], sv["yb"], sv["yc"], name=tag + "merge_bwd")
    G["b_gate"] = dbg.reshape(P["b_gate"].shape)
    da4 = mm(dya, WT["a_out"], out_dtype=ACT, name=tag + "mm_a_out_dx")
    G["w_a_out"] = mm_tn(sv["a4"], dya, name=tag + "mm_a_out_dw")
    da2, G["ln_a_g"], G["ln_a_b"], G["conv_a_b"] = _bra_bwd_ln(da4, sv["a2"], P["ln_a_g"], P["ln_a_b"], name=tag + "bra_bwd_ln")
    dza, G["conv_a_w"] = _bra_bwd_conv(da2, sv["za"], P["conv_a_w"], B=B, S=S, name=tag + "bra_bwd_conv")
    dp = mm(dyb, WT["b_out"], out_dtype=ACT, name=tag + "mm_b_out_dx")
    G["w_b_out"] = mm_tn(sv["p"], dyb, name=tag + "mm_b_out_dw")
    dzb, G["w_s"], dbs_t, G["ln_b_g"], G["ln_b_b"] = _brb_bwd(dp, sv["zb"], P["ln_b_g"], P["ln_b_b"], P["w_s"],
                                                             sv["b_s_t"], name=tag + "brb_bwd")
    G["b_s"] = dbs_t.T
    doc = mm(dyc, WT["c_out"], out_dtype=ACT, name=tag + "mm_c_out_dx")
    G["w_c_out"] = mm_tn(sv["oc"], dyc, name=tag + "mm_c_out_dw")
    dd = _attn_rowdot(doc, sv["oc"], name=tag + "attn_rowdot")
    dq_parts, dk_parts, dv_parts, dbiases = [], [], [], []
    for gi, (_, dil) in enumerate(DIL_GROUPS):
        dqkv_g, dbias = _attn_bwd(sv["qkv_gs"][gi], _regroup(doc, B, S, dil), _regroup(sv["lse_tot"], B, S, dil),
                                  _regroup(dd, B, S, dil), tabs[gi][0], name=tag + f"attn_bwd{gi}")
        t = _ungroup(dqkv_g)
        dq_parts.append(t[:, 0:GROUP_COLS])
        dk_parts.append(t[:, GROUP_COLS:2 * GROUP_COLS])
        dv_parts.append(t[:, 2 * GROUP_COLS:3 * GROUP_COLS])
        dbiases.append(dbias)
    dqkv = jnp.concatenate(dq_parts + dk_parts + dv_parts, axis=1)
    G["dbias"] = jnp.concatenate(dbiases, axis=0)
    h1 = sv["h1"]
    G["w_in"] = jnp.concatenate([mm_tn(h1, dza, name=tag + "mm_in_a_dw"), mm_tn(h1, dzb, name=tag + "mm_in_b_dw"),
                                 mm_tn(h1, dqkv, name=tag + "mm_in_c_dw"), mm_tn(h1, dzg, name=tag + "mm_in_g_dw")],
                                axis=1)
    dh1 = mm(dza, WT["in_a"], out_dtype=F32, name=tag + "mm_in_a_dx")
    dh1 = mm(dzb, WT["in_b"], out_dtype=F32, name=tag + "mm_in_b_dx", add=dh1)
    dh1 = mm(dqkv, WT["in_c"], out_dtype=F32, name=tag + "mm_in_c_dx", add=dh1)
    dh1 = mm(dzg, WT["in_g"], out_dtype=ACT, name=tag + "mm_in_g_dx", add=dh1)
    return d, dh1, dg_next


_COL_SHARDED = ("w_in", "b_gate", "conv_a_w", "w_a_out", "w_b_out", "w_c_out", "w_xkv", "w_up", "conv_f_w")
_ROW_SHARDED = ("w_mix_out", "w_xq", "w_xo", "w_down")
_SHARDED_BIG = ("w_in", "w_a_out", "w_b_out", "w_c_out", "w_mix_out", "w_xq", "w_xkv", "w_xo", "w_up", "w_down")
_SHARDED_SMALL = ("b_gate", "conv_a_w", "conv_f_w")
_REPLICATED = ("mix_pre_g", "mix_post_g", "conv_a_b", "ln_a_g", "ln_a_b", "ln_b_g", "ln_b_b", "w_s", "b_s",
               "x_pre_g", "x_post_g", "mem_g", "ffn_pre_g", "ffn_post_g", "conv_f_b")
_WEIGHTS = ('rel_bias', 'mix_pre_g', 'mix_post_g', 'w_in', 'b_gate', 'conv_a_w', 'conv_a_b', 'ln_a_g', 'ln_a_b',
            'w_a_out', 'ln_b_g', 'ln_b_b', 'w_s', 'b_s', 'w_b_out', 'w_c_out', 'w_mix_out', 'x_pre_g', 'x_post_g',
            'mem_g', 'w_xq', 'w_xkv', 'w_xo', 'ffn_pre_g', 'ffn_post_g', 'w_up', 'conv_f_w', 'conv_f_b', 'w_down')
_PACK_COLS = 1024


def _shard_axis(name):
    return 1 if name in _ROW_SHARDED else 2


def _pack(parts, dtype, row_mult):
    flat = jnp.concatenate([p.astype(dtype).reshape(-1) for p in parts])
    n = flat.shape[0]
    unit = _PACK_COLS * row_mult
    padded = -(-n // unit) * unit
    flat = jnp.pad(flat, (0, padded - n))
    return flat.reshape(padded // _PACK_COLS, _PACK_COLS)


def _join8(blocks, ax):
    t = jnp.moveaxis(blocks, 0, ax)
    shp = t.shape
    return t.reshape(shp[:ax] + (shp[ax] * shp[ax + 1],) + shp[ax + 2:])


def _split8(full, ax):
    shp = full.shape
    t = full.reshape(shp[:ax] + (N_DEV, shp[ax] // N_DEV) + shp[ax + 1:])
    return jnp.moveaxis(t, ax, 0)


_W_KEYS = {"w_a_out": "a_out", "w_b_out": "b_out", "w_c_out": "c_out", "w_mix_out": "mix_out", "w_xq": "xq",
           "w_xkv": "xkv", "w_xo": "xo", "w_up": "up", "w_down": "down"}
_IN_SPLITS = (("in_a", 0, 1024), ("in_b", 1024, 2048), ("in_c", 2048, 4352), ("in_g", 4352, 7424))


def _layer_weights(name, full_l):
    if name == "w_in":
        assert full_l.shape[1] == _IN_SPLITS[-1][2]
        W = {k: full_l[:, a:b] for k, a, b in _IN_SPLITS}
    else:
        W = {_W_KEYS[name]: _ffn_interleave(full_l) if name == "w_up" else full_l}
    return W, {k: v.T for k, v in W.items()}


def _sum_rows(x, *, name):
    n = x.shape[0]

    def body(x_ref, o_ref):
        acc = x_ref[0:1, :]
        for i in range(1, n):
            acc = acc + x_ref[i:i + 1, :]
        o_ref[...] = acc

    return pl.pallas_call(body, name=name, out_shape=jax.ShapeDtypeStruct((1, x.shape[1]), x.dtype))(x)


def kernel(x, mem, rel_bias, mix_pre_g, mix_post_g, w_in, b_gate, conv_a_w, conv_a_b, ln_a_g, ln_a_b, w_a_out, ln_b_g, ln_b_b, w_s, b_s, w_b_out, w_c_out, w_mix_out, x_pre_g, x_post_g, mem_g, w_xq, w_xkv, w_xo, ffn_pre_g, ffn_post_g, w_up, conv_f_w, conv_f_b, w_down, loss_target, m_rel_bias, m_mix_pre_g, m_mix_post_g, m_w_in, m_b_gate, m_conv_a_w, m_conv_a_b, m_ln_a_g, m_ln_a_b, m_w_a_out, m_ln_b_g, m_ln_b_b, m_w_s, m_b_s, m_w_b_out, m_w_c_out, m_w_mix_out, m_x_pre_g, m_x_post_g, m_mem_g, m_w_xq, m_w_xkv, m_w_xo, m_ffn_pre_g, m_ffn_post_g, m_w_up, m_conv_f_w, m_conv_f_b, m_w_down, v_rel_bias, v_mix_pre_g, v_mix_post_g, v_w_in, v_b_gate, v_conv_a_w, v_conv_a_b, v_ln_a_g, v_ln_a_b, v_w_a_out, v_ln_b_g, v_ln_b_b, v_w_s, v_b_s, v_w_b_out, v_w_c_out, v_w_mix_out, v_x_pre_g, v_x_post_g, v_mem_g, v_w_xq, v_w_xkv, v_w_xo, v_ffn_pre_g, v_ffn_post_g, v_w_up, v_conv_f_w, v_conv_f_b, v_w_down):
    args = locals()
    w_loc = {n: args[n] for n in _WEIGHTS}
    m_loc = {n: args["m_" + n] for n in _WEIGHTS}
    v_loc = {n: args["v_" + n] for n in _WEIGHTS}

    depth = w_in.shape[0]
    assert depth == 2
    B, S, D = x.shape
    mine = _index_of(_my_pos())
    core = lax.axis_index("c").astype(jnp.int32).reshape(1)
    my_chip = (2 * lax.axis_index("x") + lax.axis_index("y")).astype(jnp.int32).reshape(1)
    shard = {n: w_loc[n].astype(BF16) for n in _SHARDED_BIG}
    shard.update({n: w_loc[n] for n in _SHARDED_SMALL})
    Ws, WTs = [{} for _ in range(depth)], [{} for _ in range(depth)]
    small_full = {}
    hosts = _Hosts()

    def install(items, blocks):
        for (n, l), blk in zip(items, blocks):
            own = shard[n] if l is None else shard[n][l]
            sel = lax.broadcasted_iota(jnp.int32, (N_DEV,) + (1,) * own.ndim, 0) == mine
            blk = jnp.where(sel, own[None], blk)
            if l is None:
                small_full[n] = _join8(blk, _shard_axis(n))
            else:
                W, WT = _layer_weights(n, _join8(blk, _shard_axis(n) - 1))
                Ws[l].update(W)
                WTs[l].update(WT)

    def carry_gather(name, items, group, then=None):
        def done(outs):
            group.extend(zip(items, outs))
            if then is not None:
                then()
        hosts.add(name, lambda: _gather_ici([shard[n] for n, _ in items], [l for _, l in items]), done)

    def finish_gather(group, tag):
        install([it for it, _ in group], _exchange_call(_gather_d2d([b for _, b in group]), name="gather_d2d_" + tag))

    first = [("w_in", 0)] + [(n, None) for n in _SHARDED_SMALL]
    install(first, _all_gather([shard[n] if l is None else shard[n][l] for n, l in first], name="gather_first"))
    g0, g1 = [], []
    carry_gather("l0_mm_in_a", [("w_xq", 0), ("w_a_out", 0), ("w_b_out", 0), ("w_c_out", 0)], g0)
    carry_gather("l0_mm_in_b", [("w_xo", 0), ("w_mix_out", 0)], g0)
    carry_gather("l0_mm_in_c", [("w_xkv", 0), ("w_down", 0)], g0)
    carry_gather("l0_mm_in_g", [("w_up", 0)], g0, then=lambda: finish_gather(g0, "l0"))
    carry_gather("l0_mm_mix", [("w_xq", 1), ("w_mix_out", 1)], g1)
    carry_gather("l0_mm_xq", [("w_xo", 1), ("w_a_out", 1), ("w_b_out", 1), ("w_c_out", 1)], g1)
    carry_gather("l0_mm_xo", [("w_xkv", 1)], g1)
    carry_gather("l0_mm_up", [("w_in", 1), ("w_down", 1)], g1)
    carry_gather("l0_mm_down", [("w_up", 1)], g1, then=lambda: finish_gather(g1, "l1"))

    Gs = [{} for _ in range(depth)]
    sendbuf, chip_sums, parts = {}, {}, {}

    def to_send(item):
        n, l = item
        if l is None:
            g = jnp.stack([Gs[k][n].reshape(small_full[n].shape[1:]) for k in range(depth)], axis=0)
            sendbuf[item] = _split8(g, _shard_axis(n))
        else:
            sendbuf[item] = _split8(Gs[l][n], _shard_axis(n) - 1).astype(BF16)
        return sendbuf[item]

    def swapped(items, got):
        for item, r in zip(items, got):
            chip_sums[item] = _chip_sum(sendbuf[item], r, core, name=f"chip_sum_{item[0]}_{item[1]}")

    def carry_swap(name, items):
        hosts.add(name, lambda: _scatter_d2d([to_send(it) for it in items]), lambda outs: swapped(items, outs))

    def carry_scatter(name, items):
        hosts.add(name, lambda: _scatter_ici([chip_sums[it] for it in items]), lambda outs: parts.update(zip(items, outs)))

    carry_swap("l0_mm_down_dx", [(n, 1) for n in _SHARDED_BIG])
    carry_scatter("l0_mm_down_dw", [(n, 1) for n in ("w_xq", "w_xkv", "w_xo", "w_mix_out", "w_a_out", "w_b_out", "w_c_out")])
    carry_scatter("l0_mm_up_dx", [("w_in", 1)])
    carry_scatter("l0_mm_up_dw", [("w_up", 1), ("w_down", 1)])
    carry_swap("l0_mm_xo_dx", [("w_down", 0), ("w_up", 0)])
    carry_scatter("l0_mm_xq_dx", [("w_down", 0)])
    carry_swap("l0_mm_mix_dx", [("w_xo", 0), ("w_xq", 0), ("w_xkv", 0)])
    carry_swap("l0_mm_in_a_dw", [("w_mix_out", 0), ("w_a_out", 0), ("w_b_out", 0), ("w_c_out", 0)])
    carry_scatter("l0_mm_in_b_dw", [("w_mix_out", 0), ("w_a_out", 0), ("w_b_out", 0), ("w_c_out", 0)])
    carry_scatter("l0_mm_in_c_dw", [("w_xo", 0), ("w_xq", 0), ("w_xkv", 0)])
    carry_scatter("l0_mm_in_g_dw", [("w_up", 0)])
    carry_swap("l0_mm_in_a_dx", [("w_in", 0)])
    carry_scatter("l0_mm_in_g_dx", [("w_in", 0)])

    rep_w = {n: w_loc[n] for n in ("rel_bias",) + _REPLICATED}
    Ps = lambda: [dict({n: rep_w[n][l] for n in _REPLICATED}, **{n: small_full[n][l] for n in _SHARDED_SMALL})
                  for l in range(depth)]
    loss_vec, grad_x, g_rel = _run_step(x, mem, loss_target, Ws, WTs, Ps(), rep_w["rel_bias"], Gs, hosts)
    assert not hosts.plan, list(hosts.plan)

    last = [(n, None) for n in _SHARDED_SMALL]
    swapped(last, _exchange_call(_scatter_d2d([to_send(it) for it in last]), name="swap_grads_last"))
    parts.update(zip(last, _exchange_call(_scatter_ici([chip_sums[it] for it in last]), name="scatter_grads_last")))

    rep = ("rel_bias",) + _REPLICATED
    rep_g = {n: jnp.stack([Gs[l][n].reshape(w_loc[n].shape[1:]) for l in range(depth)], axis=0) for n in _REPLICATED}
    rep_g["rel_bias"] = g_rel
    packed = _pack([loss_vec] + [rep_g[n] for n in rep], F32, 8)
    (allp,) = _all_gather([packed], name="gather_rep_grads")
    sel = lax.broadcasted_iota(jnp.int32, (N_DEV, 1, 1), 0) == mine
    allp = jnp.where(sel, packed[None], allp).reshape(N_DEV, -1)
    loss = _sum_rows(allp[:, :LANE], name="loss_sum")[0, 0]
    off = LANE
    rep_parts = {}
    for n in rep:
        size = math.prod(w_loc[n].shape)
        rep_parts[n] = allp[:, off:off + size].reshape((N_DEV,) + w_loc[n].shape)
        off += size

    g_loc, deltas, new_m, new_v = {}, {}, {}, {}
    for n in _WEIGHTS:
        if n in rep_parts:
            p, own = [rep_parts[n]], {}
        else:
            its = [(n, None)] if n in _SHARDED_SMALL else [(n, l) for l in range(depth)]
            p, own = [parts[it] for it in its], {"own": [chip_sums[it] for it in its], "own_slot": my_chip}
        g_loc[n], deltas[n], new_m[n], new_v[n] = _adamw(w_loc[n], p, m_loc[n], v_loc[n], name="adamw_" + n, **own)
    return (loss, grad_x, *[g_loc[n] for n in _WEIGHTS], *[deltas[n] for n in _WEIGHTS],
            *[new_m[n] for n in _WEIGHTS], *[new_v[n] for n in _WEIGHTS])


def _local_step(x, mem, loss_target, full):
    depth = full["w_in"].shape[0]
    Ws, WTs, Ps = [{} for _ in range(depth)], [{} for _ in range(depth)], []
    for l in range(depth):
        Ps.append({n: full[n][l] for n in _WEIGHTS if n != "rel_bias"})
        for n in _SHARDED_BIG:
            W, WT = _layer_weights(n, full[n][l].astype(BF16))
            Ws[l].update(W)
            WTs[l].update(WT)
    Gs = [{} for _ in range(depth)]
    loss_vec, grad_x, g_rel = _run_step(x, mem, loss_target, Ws, WTs, Ps, full["rel_bias"], Gs, None)
    grads = {"rel_bias": g_rel}
    for n in _WEIGHTS:
        if n != "rel_bias":
            grads[n] = jnp.stack([Gs[l][n].reshape(full[n].shape[1:]) for l in range(depth)], axis=0)
    return loss_vec[0, 0], grad_x, grads


def _run_step(x, mem, loss_target, Ws, WTs, Ps, rel_bias, Gs, hosts):
    B, S, D = x.shape
    depth = len(Ws)
    x2d = x.reshape(B * S, D)
    mem2 = mem.reshape(-1, D)
    tabs = _bias_tables(rel_bias)
    (h1,) = _norm_fwd(x2d, None, None, Ps[0]["mix_pre_g"], name="norm0")
    xc = x2d
    saved = []
    for l in range(depth):
        nxt = Ps[l + 1]["mix_pre_g"] if l + 1 < depth else None
        xc, h1, sv = _layer_fwd(l, xc, h1, mem2, Ws[l], Ps[l], tabs, nxt, B=B, S=S, hosts=hosts)
        saved.append(sv)
    loss_vec, d = _loss_kernel(xc, loss_target.reshape(B * S, D), name="loss")

    dh_next = None
    for l in reversed(range(depth)):
        nxt = Ps[l + 1]["mix_pre_g"] if l + 1 < depth else None
        d, dh_next, dg_next = _layer_bwd(l, d, dh_next, nxt, saved[l], mem2, Ws[l], WTs[l], Ps[l], tabs, Gs[l], B=B, S=S,
                                         hosts=hosts)
        if dg_next is not None:
            Gs[l + 1]["mix_pre_g"] = dg_next
    grad_x2d, Gs[0]["mix_pre_g"] = _norm_bwd(d, dh_next, x2d, Ps[0]["mix_pre_g"], None, None, name="norm0_bwd")

    dbias = jnp.stack([Gs[l]["dbias"] for l in range(depth)], axis=0)
    buckets = jnp.stack([t[1] for t in tabs], axis=0)
    rb = _bucket_sum(dbias, buckets, name="rel_bias_grad")
    return loss_vec, grad_x2d.reshape(B, S, D), rb[:, :rel_bias.shape[1]]
```

```python
import functools
import math

import jax
import jax.numpy as jnp
from jax import lax
from jax.experimental import pallas as pl
from jax.experimental.pallas import tpu as pltpu

F32 = jnp.float32
BF16 = jnp.bfloat16

N_DEV = 8
NORM_EPS = 1e-6
LN_EPS = 1e-5
CONV_K = 31
SG_CHUNK = 128
ATT_BLOCK = 128
HEAD_DIM = 64
HEADS_PER_GROUP = 4
GROUP_COLS = HEADS_PER_GROUP * HEAD_DIM
DIL_GROUPS = ((128, 1), (512, 4), (2048, 16))
REL_BUCKETS = 32
REL_MAX_DIST = 2048
X_HEADS = 4
ATT_SCALE = HEAD_DIM ** -0.5
MASK_VALUE = -1e30

ADAM_LR = 0.001
ADAM_B1 = 0.9
ADAM_B2 = 0.999
ADAM_EPS = 1e-08
ADAM_WD = 0.01
ADAM_STEP = 10

LANE = 128
ACT = BF16
ROW_TILE = 512
VMEM_LIMIT = 48 << 20

_NT = (((1,), (1,)), ((), ()))
_TN = (((0,), (0,)), ((), ()))


def _params(sem, vmem=VMEM_LIMIT):
    return pltpu.CompilerParams(dimension_semantics=sem, vmem_limit_bytes=vmem)


def _pick(n, cap):
    if n <= cap:
        return n
    best = None
    for t in range(LANE, cap + 1, LANE):
        if n % t == 0:
            best = t
    assert best is not None, (n, cap)
    return best


def _sigmoid(x):
    return 1.0 / (1.0 + jnp.exp(-x))


_GELU_C = math.sqrt(2.0 / math.pi)


def _gelu(x):
    return 0.5 * x * (1.0 + jnp.tanh(_GELU_C * (x + 0.044715 * x * x * x)))


def _gelu_and_grad(x):
    t = jnp.tanh(_GELU_C * (x + 0.044715 * x * x * x))
    g = 0.5 * x * (1.0 + t)
    dg = 0.5 * (1.0 + t) + 0.5 * x * (1.0 - t * t) * _GELU_C * (1.0 + 3 * 0.044715 * x * x)
    return g, dg


MM_VMEM_BUDGET = 36 << 20


def _divisors128(n):
    return [n] + [t for t in range(n - n % LANE, 0, -LANE) if n % t == 0 and t != n]


class _Hosted:
    def __init__(self, hosts, name):
        self.ex = hosts.get(name) if hosts is not None else None
        self.hosts, self.name = hosts, name

    @property
    def on(self):
        return self.ex is not None

    def specs(self):
        if not self.on:
            return [], [], [], []
        n = self.ex.n_copies
        return (list(self.ex.ins), [_HBM] * len(self.ex.ins), list(self.ex.out_shapes),
                [pltpu.SemaphoreType.DMA((n,)), pltpu.SemaphoreType.DMA((n,))])

    def split(self, refs, n_in, n_out, n_scratch):
        if not self.on:
            return list(refs), None
        ni, no = len(self.ex.ins), len(self.ex.out_shapes)
        refs = list(refs)
        ins, rest = refs[:n_in], refs[n_in:]
        ex_in, rest = rest[:ni], rest[ni:]
        outs, rest = rest[:n_out], rest[n_out:]
        ex_out, rest = rest[:no], rest[no:]
        scratch, sems = rest[:n_scratch], rest[n_scratch:]
        return ins + outs + scratch, (ex_in, ex_out, sems[0], sems[1])

    def run(self, ex_refs, grid):
        if not self.on:
            return
        ids = [pl.program_id(ax) for ax in range(len(grid))]
        first = functools.reduce(jnp.logical_and, [i == 0 for i in ids])
        last = functools.reduce(jnp.logical_and, [i == g - 1 for i, g in zip(ids, grid)])

        @pl.when(first)
        def _():
            sends, _ = self.ex.make(*ex_refs)
            for cp in sends:
                cp.start()

        return last

    def finish(self, ex_refs, last):
        if not self.on:
            return

        @pl.when(last)
        def _():
            sends, recvs = self.ex.make(*ex_refs)
            for cp in recvs:
                cp.wait_recv()
            for cp in sends:
                cp.wait_send()

    def done(self, outs):
        if self.on:
            self.hosts.put(self.name, list(outs))


def _mm(a, b, *, out_dtype, name, add=None, hosts=None):
    M, K = a.shape
    K2, N = b.shape
    assert K == K2
    has_add = add is not None
    ob = jnp.dtype(out_dtype).itemsize
    tm = _pick(M, 1024)
    tk = _pick(K, 3072)
    nk = K // tk
    for tn in _divisors128(N):
        use = 2 * (tm * tk * a.dtype.itemsize + tk * tn * b.dtype.itemsize + tm * tn * (ob + (4 if has_add else 0)))
        if use + (tm * tn * 4 if nk > 1 else 0) <= MM_VMEM_BUDGET:
            break
    grid = (M // tm, N // tn, nk)
    hosted = _Hosted(hosts, name)
    n_in = 3 if has_add else 2
    n_scr = 1 if nk > 1 else 0

    def body(*refs):
        own, ex_refs = hosted.split(refs, n_in, 1, n_scr)
        last = hosted.run(ex_refs, grid)
        a_ref, b_ref = own[0], own[1]
        c_ref = own[2] if has_add else None
        o_ref = own[n_in]
        part = jnp.dot(a_ref[...].astype(BF16), b_ref[...].astype(BF16), preferred_element_type=F32)
        if nk == 1:
            if has_add:
                part = part + c_ref[...]
            o_ref[...] = part.astype(o_ref.dtype)
        else:
            acc_ref = own[n_in + 1]
            k = pl.program_id(2)

            @pl.when(k == 0)
            def _():
                acc_ref[...] = part

            @pl.when(k > 0)
            def _():
                acc_ref[...] += part

            @pl.when(k == nk - 1)
            def _():
                r = acc_ref[...]
                if has_add:
                    r = r + c_ref[...]
                o_ref[...] = r.astype(o_ref.dtype)
        hosted.finish(ex_refs, last)

    in_specs = [pl.BlockSpec((tm, tk), lambda i, j, k: (i, k)), pl.BlockSpec((tk, tn), lambda i, j, k: (k, j))]
    args = [a, b]
    if has_add:
        in_specs.append(pl.BlockSpec((tm, tn), lambda i, j, k: (i, j)))
        args.append(add)
    ex_args, ex_in_specs, ex_out_shapes, ex_scratch = hosted.specs()
    outs = pl.pallas_call(
        body, name=name, out_shape=(jax.ShapeDtypeStruct((M, N), out_dtype), *ex_out_shapes),
        grid=grid, in_specs=in_specs + ex_in_specs,
        out_specs=(pl.BlockSpec((tm, tn), lambda i, j, k: (i, j)), *([_HBM] * len(ex_out_shapes))),
        scratch_shapes=([pltpu.VMEM((tm, tn), F32)] if nk > 1 else []) + ex_scratch,
        compiler_params=_params(("arbitrary",) * 3 if hosted.on else ("parallel", "parallel", "arbitrary")),
    )(*args, *ex_args)
    hosted.done(outs[1:])
    return outs[0]


def _mm_tn(a, b, *, name, hosts=None):
    T, M = a.shape
    T2, N = b.shape
    assert T == T2
    tm = _pick(M, 1536)
    tt = _pick(T, 2048)
    nt = T // tt
    for tn in _divisors128(N):
        if 2 * (tt * tm * a.dtype.itemsize + tt * tn * b.dtype.itemsize + tm * tn * 4) <= MM_VMEM_BUDGET:
            break
    grid = (M // tm, N // tn, nt)
    hosted = _Hosted(hosts, name)

    def body(*refs):
        (a_ref, b_ref, o_ref), ex_refs = hosted.split(refs, 2, 1, 0)
        last = hosted.run(ex_refs, grid)
        k = pl.program_id(2)
        part = lax.dot_general(a_ref[...].astype(BF16), b_ref[...].astype(BF16), _TN, preferred_element_type=F32)

        @pl.when(k == 0)
        def _():
            o_ref[...] = part

        @pl.when(k > 0)
        def _():
            o_ref[...] += part

        hosted.finish(ex_refs, last)

    ex_args, ex_in_specs, ex_out_shapes, ex_scratch = hosted.specs()
    outs = pl.pallas_call(
        body, name=name, out_shape=(jax.ShapeDtypeStruct((M, N), F32), *ex_out_shapes),
        grid=grid,
        in_specs=[pl.BlockSpec((tt, tm), lambda i, j, k: (k, i)), pl.BlockSpec((tt, tn), lambda i, j, k: (k, j))]
        + ex_in_specs,
        out_specs=(pl.BlockSpec((tm, tn), lambda i, j, k: (i, j)), *([_HBM] * len(ex_out_shapes))),
        scratch_shapes=ex_scratch,
        compiler_params=_params(("arbitrary",) * 3 if hosted.on else ("parallel", "parallel", "arbitrary")),
    )(a, b, *ex_args)
    hosted.done(outs[1:])
    return outs[0]


def _rms(x):
    r = lax.rsqrt(jnp.mean(x * x, axis=-1, keepdims=True) + NORM_EPS)
    return x * r, r


def _row_spec(cols, tr=ROW_TILE):
    return pl.BlockSpec((tr, cols), lambda i: (i, 0))


def _vec_spec(cols):
    return pl.BlockSpec((1, cols), lambda i: (0, 0))


def _norm_fwd(x, y, g_post, g_pre, *, name):
    T, D = x.shape
    has_post = y is not None
    has_pre = g_pre is not None

    def body(*refs):
        refs = list(refs)
        x_ref = refs.pop(0)
        xn = x_ref[...]
        if has_post:
            y_ref = refs.pop(0)
            gp_ref = refs.pop(0)
        if has_pre:
            gq_ref = refs.pop(0)
        if has_post:
            yh, _ = _rms(y_ref[...].astype(F32))
            xn = xn + yh * gp_ref[...]
            refs.pop(0)[...] = xn
        if has_pre:
            xh, _ = _rms(xn)
            refs.pop(0)[...] = (xh * gq_ref[...]).astype(BF16)

    args, in_specs, out_shape, out_specs = [x], [_row_spec(D)], [], []
    if has_post:
        args += [y, g_post.reshape(1, D)]
        in_specs += [_row_spec(D), _vec_spec(D)]
        out_shape.append(jax.ShapeDtypeStruct((T, D), F32))
        out_specs.append(_row_spec(D))
    if has_pre:
        args.append(g_pre.reshape(1, D))
        in_specs.append(_vec_spec(D))
        out_shape.append(jax.ShapeDtypeStruct((T, D), BF16))
        out_specs.append(_row_spec(D))
    return pl.pallas_call(body, name=name, out_shape=tuple(out_shape), grid=(T // ROW_TILE,), in_specs=in_specs,
                          out_specs=tuple(out_specs), compiler_params=_params(("parallel",)))(*args)


def _norm_bwd(dres, dh, x_new, g_pre, y, g_post, *, name):
    has_res = dres is not None
    has_pre = dh is not None
    has_post = y is not None
    out_d = has_pre and (has_res or has_post)
    T, D = (dres if has_res else dh).shape
    if not has_res:
        assert not has_post

    def body(*refs):
        refs = list(refs)
        i = pl.program_id(0)
        d = refs.pop(0)[...] if has_res else None
        if has_pre:
            dh_v = refs.pop(0)[...].astype(F32)
            xh, r = _rms(refs.pop(0)[...])
            gq = refs.pop(0)[...]
        if has_post:
            yh, ry = _rms(refs.pop(0)[...].astype(F32))
            gp = refs.pop(0)[...]
        if has_pre:
            dxh = dh_v * gq
            dpre = r * (dxh - xh * jnp.mean(dxh * xh, axis=-1, keepdims=True))
            d = dpre if d is None else d + dpre
            dgq = jnp.sum(dh_v * xh, axis=0, keepdims=True)
        if out_d:
            refs.pop(0)[...] = d
        if has_post:
            dyh = d * gp
            refs.pop(0)[...] = (ry * (dyh - yh * jnp.mean(dyh * yh, axis=-1, keepdims=True))).astype(BF16)
            dgp_ref = refs.pop(0)
            dgp = jnp.sum(d * yh, axis=0, keepdims=True)

            @pl.when(i == 0)
            def _():
                dgp_ref[...] = dgp

            @pl.when(i > 0)
            def _():
                dgp_ref[...] += dgp
        if has_pre:
            dgq_ref = refs.pop(0)

            @pl.when(i == 0)
            def _():
                dgq_ref[...] = dgq

            @pl.when(i > 0)
            def _():
                dgq_ref[...] += dgq

    args, in_specs, out_shape, out_specs = [], [], [], []
    if has_res:
        args.append(dres)
        in_specs.append(_row_spec(D))
    if has_pre:
        args += [dh, x_new, g_pre.reshape(1, D)]
        in_specs += [_row_spec(D), _row_spec(D), _vec_spec(D)]
    if has_post:
        args += [y, g_post.reshape(1, D)]
        in_specs += [_row_spec(D), _vec_spec(D)]
    if out_d:
        out_shape.append(jax.ShapeDtypeStruct((T, D), F32))
        out_specs.append(_row_spec(D))
    if has_post:
        out_shape += [jax.ShapeDtypeStruct((T, D), BF16), jax.ShapeDtypeStruct((1, D), F32)]
        out_specs += [_row_spec(D), _vec_spec(D)]
    if has_pre:
        out_shape.append(jax.ShapeDtypeStruct((1, D), F32))
        out_specs.append(_vec_spec(D))
    return pl.pallas_call(body, name=name, out_shape=tuple(out_shape), grid=(T // ROW_TILE,), in_specs=in_specs,
                          out_specs=tuple(out_specs), compiler_params=_params(("arbitrary",)))(*args)


def _loss_kernel(xf, target, *, name):
    T, D = xf.shape

    def body(x_ref, t_ref, loss_ref, d_ref):
        i = pl.program_id(0)
        e = x_ref[...] - t_ref[...]
        d_ref[...] = e * (1.0 / D)
        part = jnp.sum(jnp.sum(e * e, axis=0, keepdims=True), axis=1, keepdims=True) * (0.5 / D)
        part = jnp.broadcast_to(part, (1, LANE))

        @pl.when(i == 0)
        def _():
            loss_ref[...] = part

        @pl.when(i > 0)
        def _():
            loss_ref[...] += part

    return pl.pallas_call(body, name=name,
                          out_shape=(jax.ShapeDtypeStruct((1, LANE), F32), jax.ShapeDtypeStruct((T, D), F32)),
                          grid=(T // ROW_TILE,), in_specs=[_row_spec(D), _row_spec(D)],
                          out_specs=(_vec_spec(LANE), _row_spec(D)), compiler_params=_params(("arbitrary",)))(xf, target)


CONV_PAD = 32
CONV_CH = 64
SUBLANES = 8


def _shift_rows(win, r):
    return win if r == 0 else win[r:r + win.shape[0] - SUBLANES, :]


def _ln_stats(x):
    mu = jnp.mean(x, axis=-1, keepdims=True)
    xc = x - mu
    rstd = lax.rsqrt(jnp.mean(xc * xc, axis=-1, keepdims=True) + LN_EPS)
    return xc * rstd, rstd


def _acc_out(ref, val, first):
    @pl.when(first)
    def _():
        ref[...] = val

    @pl.when(jnp.logical_not(first))
    def _():
        ref[...] += val


def _bra_fwd(za, conv_w, conv_b, ln_g, ln_b, *, B, S, name):
    C = conv_w.shape[1]

    def body(za_ref, w_ref, cb_ref, g_ref, b_ref, a4_ref, a2_ref, pad_ref):
        pad_ref[0:CONV_PAD, :] = jnp.zeros((CONV_PAD, C), F32)
        pad_ref[CONV_PAD:, :] = za_ref[:, 0:C].astype(F32) * _sigmoid(za_ref[:, C:2 * C].astype(F32))

        def chunk(i, _):
            base = pl.multiple_of(i * CONV_CH, CONV_CH)
            win = pad_ref[pl.ds(base, CONV_CH + CONV_PAD), :]
            acc = jnp.broadcast_to(cb_ref[...], (CONV_CH, C))
            for r in range(SUBLANES):
                sh = _shift_rows(win, r)
                for j in range(CONV_K):
                    off = CONV_PAD - (CONV_K - 1) + j
                    if off % SUBLANES == r:
                        acc = acc + w_ref[j:j + 1, :] * sh[off - r:off - r + CONV_CH, :]
            a2_ref[pl.ds(base, CONV_CH), :] = acc
            xh, _ = _ln_stats(acc)
            a3 = xh * g_ref[...] + b_ref[...]
            a4_ref[pl.ds(base, CONV_CH), :] = (a3 * _sigmoid(a3)).astype(BF16)
            return 0

        lax.fori_loop(0, S // CONV_CH, chunk, 0)

    vec = pl.BlockSpec((1, C), lambda b: (0, 0))
    return pl.pallas_call(
        body, name=name,
        out_shape=(jax.ShapeDtypeStruct((B * S, C), BF16), jax.ShapeDtypeStruct((B * S, C), F32)),
        grid=(B,),
        in_specs=[pl.BlockSpec((S, 2 * C), lambda b: (b, 0)), pl.BlockSpec((CONV_K, C), lambda b: (0, 0)), vec, vec, vec],
        out_specs=(pl.BlockSpec((S, C), lambda b: (b, 0)), pl.BlockSpec((S, C), lambda b: (b, 0))),
        scratch_shapes=[pltpu.VMEM((S + CONV_PAD, C), F32)],
        compiler_params=_params(("parallel",)),
    )(za, conv_w, conv_b.reshape(1, C), ln_g.reshape(1, C), ln_b.reshape(1, C))


def _bra_bwd_ln(da4, a2, ln_g, ln_b, *, name):
    T, C = a2.shape

    def body(da4_ref, a2_ref, g_ref, b_ref, da2_ref, dg_ref, db_ref, dcb_ref):
        first = pl.program_id(0) == 0
        xh, rstd = _ln_stats(a2_ref[...])
        a3 = xh * g_ref[...] + b_ref[...]
        sg = _sigmoid(a3)
        da3 = da4_ref[...].astype(F32) * (sg * (1.0 + a3 * (1.0 - sg)))
        dxh = da3 * g_ref[...]
        da2 = rstd * (dxh - jnp.mean(dxh, axis=-1, keepdims=True) - xh * jnp.mean(dxh * xh, axis=-1, keepdims=True))
        da2_ref[...] = da2
        _acc_out(dg_ref, jnp.sum(da3 * xh, axis=0, keepdims=True), first)
        _acc_out(db_ref, jnp.sum(da3, axis=0, keepdims=True), first)
        _acc_out(dcb_ref, jnp.sum(da2, axis=0, keepdims=True), first)

    vec = _vec_spec(C)
    vshape = jax.ShapeDtypeStruct((1, C), F32)
    return pl.pallas_call(body, name=name, out_shape=(jax.ShapeDtypeStruct((T, C), F32), vshape, vshape, vshape),
                          grid=(T // ROW_TILE,), in_specs=[_row_spec(C), _row_spec(C), vec, vec],
                          out_specs=(_row_spec(C), vec, vec, vec),
                          compiler_params=_params(("arbitrary",)))(da4, a2, ln_g.reshape(1, C), ln_b.reshape(1, C))


def _bra_bwd_conv(da2, za, conv_w, *, B, S, name):
    C = conv_w.shape[1]
    SUB = 8

    def body(da2_ref, za_ref, w_ref, dza_ref, dw_ref, pad1_ref, pad2_ref, dwacc_ref):
        first = pl.program_id(0) == 0
        pad1_ref[0:CONV_PAD, :] = jnp.zeros((CONV_PAD, C), F32)
        pad1_ref[CONV_PAD:, :] = za_ref[:, 0:C].astype(F32) * _sigmoid(za_ref[:, C:2 * C].astype(F32))
        pad2_ref[0:S, :] = da2_ref[...]
        pad2_ref[S:, :] = jnp.zeros((CONV_PAD, C), F32)
        dwacc_ref[...] = jnp.zeros_like(dwacc_ref)

        def chunk(i, _):
            base = pl.multiple_of(i * CONV_CH, CONV_CH)
            rows = pl.ds(base, CONV_CH)
            win1 = pad1_ref[pl.ds(base, CONV_CH + CONV_PAD), :]
            win2 = pad2_ref[pl.ds(base, CONV_CH + CONV_PAD), :]
            d = win2[0:CONV_CH, :]
            da1 = jnp.zeros((CONV_CH, C), F32)
            for r in range(SUBLANES):
                sh1 = _shift_rows(win1, r)
                sh2 = _shift_rows(win2, r)
                for j in range(CONV_K):
                    off2 = CONV_K - 1 - j
                    if off2 % SUBLANES == r:
                        da1 = da1 + w_ref[j:j + 1, :] * sh2[off2 - r:off2 - r + CONV_CH, :]
                    off1 = CONV_PAD - (CONV_K - 1) + j
                    if off1 % SUBLANES == r:
                        prod = d * sh1[off1 - r:off1 - r + CONV_CH, :]
                        dwacc_ref[j * SUB:(j + 1) * SUB, :] += jnp.sum(prod.reshape(CONV_CH // SUB, SUB, C), axis=0)
            a_val = za_ref[rows, 0:C].astype(F32)
            sg = _sigmoid(za_ref[rows, C:2 * C].astype(F32))
            dza_ref[rows, 0:C] = (da1 * sg).astype(BF16)
            dza_ref[rows, C:2 * C] = (da1 * a_val * sg * (1.0 - sg)).astype(BF16)
            return 0

        lax.fori_loop(0, S // CONV_CH, chunk, 0)
        _acc_out(dw_ref, jnp.sum(dwacc_ref[...].reshape(CONV_K, SUB, C), axis=1), first)

    return pl.pallas_call(
        body, name=name,
        out_shape=(jax.ShapeDtypeStruct((B * S, 2 * C), BF16), jax.ShapeDtypeStruct((CONV_K, C), F32)),
        grid=(B,),
        in_specs=[pl.BlockSpec((S, C), lambda b: (b, 0)), pl.BlockSpec((S, 2 * C), lambda b: (b, 0)),
                  pl.BlockSpec((CONV_K, C), lambda b: (0, 0))],
        out_specs=(pl.BlockSpec((S, 2 * C), lambda b: (b, 0)), pl.BlockSpec((CONV_K, C), lambda b: (0, 0))),
        scratch_shapes=[pltpu.VMEM((S + CONV_PAD, C), F32), pltpu.VMEM((S + CONV_PAD, C), F32),
                        pltpu.VMEM((CONV_K * SUB, C), F32)],
        compiler_params=_params(("arbitrary",)),
    )(da2, za, conv_w)


SG_STEP = 4


def _tril_mask():
    r = lax.broadcasted_iota(jnp.int32, (SG_CHUNK, SG_CHUNK), 0)
    c = lax.broadcasted_iota(jnp.int32, (SG_CHUNK, SG_CHUNK), 1)
    return c <= r


def _brb_fwd(zb, ln_g, ln_b, w_s, b_s_t, *, name):
    T, C2 = zb.shape
    C = C2 // 2
    G = w_s.shape[0]
    GC = C // G

    def body(zb_ref, g_ref, b_ref, ws_ref, bs_ref, p_ref):
        z = _gelu(zb_ref[...].astype(F32))
        u = z[:, 0:C]
        xh, _ = _ln_stats(z[:, C:2 * C])
        v1 = (xh * g_ref[...] + b_ref[...]).astype(BF16)
        mask = _tril_mask()
        wss = [jnp.where(mask, ws_ref[gi], 0.0).astype(BF16) for gi in range(G)]
        for ci in range(SG_STEP):
            rows = slice(ci * SG_CHUNK, (ci + 1) * SG_CHUNK)
            outs = [jnp.dot(wss[gi], v1[rows, gi * GC:(gi + 1) * GC], preferred_element_type=F32) + bs_ref[:, gi:gi + 1]
                    for gi in range(G)]
            p_ref[rows, :] = (u[rows, :] * jnp.concatenate(outs, axis=1)).astype(BF16)

    tr = SG_STEP * SG_CHUNK
    return pl.pallas_call(
        body, name=name, out_shape=jax.ShapeDtypeStruct((T, C), BF16), grid=(T // tr,),
        in_specs=[_row_spec(C2, tr), _vec_spec(C), _vec_spec(C),
                  pl.BlockSpec((G, SG_CHUNK, SG_CHUNK), lambda i: (0, 0, 0)), pl.BlockSpec((SG_CHUNK, G), lambda i: (0, 0))],
        out_specs=_row_spec(C, tr), compiler_params=_params(("parallel",)),
    )(zb, ln_g.reshape(1, C), ln_b.reshape(1, C), w_s, b_s_t)


def _brb_bwd(dp, zb, ln_g, ln_b, w_s, b_s_t, *, name):
    T, C2 = zb.shape
    C = C2 // 2
    G = w_s.shape[0]
    GC = C // G

    def body(dp_ref, zb_ref, g_ref, b_ref, ws_ref, bs_ref, dzb_ref, dws_ref, dbs_ref, dg_ref, db_ref):
        first = pl.program_id(0) == 0
        z, dz = _gelu_and_grad(zb_ref[...].astype(F32))
        u = z[:, 0:C]
        xh, rstd = _ln_stats(z[:, C:2 * C])
        v1 = (xh * g_ref[...] + b_ref[...]).astype(BF16)
        dp_v = dp_ref[...].astype(F32)
        mask = _tril_mask()
        wss = [jnp.where(mask, ws_ref[gi], 0.0).astype(BF16) for gi in range(G)]
        dv2_all = dp_v * u
        dv2b_all = dv2_all.astype(BF16)
        v2_rows, dv1_rows = [], []
        dwss, dbss = [None] * G, [None] * G
        for ci in range(SG_STEP):
            rows = slice(ci * SG_CHUNK, (ci + 1) * SG_CHUNK)
            v2s, dv1s = [], []
            for gi in range(G):
                cols = slice(gi * GC, (gi + 1) * GC)
                v2s.append(jnp.dot(wss[gi], v1[rows, cols], preferred_element_type=F32) + bs_ref[:, gi:gi + 1])
                dv2b = dv2b_all[rows, cols]
                dbs = jnp.sum(dv2_all[rows, cols], axis=1, keepdims=True)
                dws = lax.dot_general(dv2b, v1[rows, cols], _NT, preferred_element_type=F32)
                dbss[gi] = dbs if dbss[gi] is None else dbss[gi] + dbs
                dwss[gi] = dws if dwss[gi] is None else dwss[gi] + dws
                dv1s.append(lax.dot_general(wss[gi], dv2b, _TN, preferred_element_type=F32))
            v2_rows.append(jnp.concatenate(v2s, axis=1))
            dv1_rows.append(jnp.concatenate(dv1s, axis=1))
        dwss = [jnp.where(mask, t, 0.0) for t in dwss]
        du = dp_v * jnp.concatenate(v2_rows, axis=0)
        dv1 = jnp.concatenate(dv1_rows, axis=0)
        dxh = dv1 * g_ref[...]
        dv0 = rstd * (dxh - jnp.mean(dxh, axis=-1, keepdims=True) - xh * jnp.mean(dxh * xh, axis=-1, keepdims=True))
        dzb_ref[:, 0:C] = (du * dz[:, 0:C]).astype(BF16)
        dzb_ref[:, C:2 * C] = (dv0 * dz[:, C:2 * C]).astype(BF16)
        _acc_out(dws_ref, jnp.stack(dwss, axis=0), first)
        _acc_out(dbs_ref, jnp.concatenate(dbss, axis=1), first)
        _acc_out(dg_ref, jnp.sum(dv1 * xh, axis=0, keepdims=True), first)
        _acc_out(db_ref, jnp.sum(dv1, axis=0, keepdims=True), first)

    wspec = pl.BlockSpec((G, SG_CHUNK, SG_CHUNK), lambda i: (0, 0, 0))
    bspec = pl.BlockSpec((SG_CHUNK, G), lambda i: (0, 0))
    return pl.pallas_call(
        body, name=name,
        out_shape=(jax.ShapeDtypeStruct((T, C2), BF16), jax.ShapeDtypeStruct((G, SG_CHUNK, SG_CHUNK), F32),
                   jax.ShapeDtypeStruct((SG_CHUNK, G), F32), jax.ShapeDtypeStruct((1, C), F32),
                   jax.ShapeDtypeStruct((1, C), F32)),
        grid=(T // (SG_STEP * SG_CHUNK),),
        in_specs=[_row_spec(C, SG_STEP * SG_CHUNK), _row_spec(C2, SG_STEP * SG_CHUNK), _vec_spec(C), _vec_spec(C), wspec,
                  bspec],
        out_specs=(_row_spec(C2, SG_STEP * SG_CHUNK), wspec, bspec, _vec_spec(C), _vec_spec(C)),
        compiler_params=_params(("arbitrary",)),
    )(dp, zb, ln_g.reshape(1, C), ln_b.reshape(1, C), w_s, b_s_t)


ATT_UNROLL = 3


def _att_unroll(trips):
    return 5 if trips % 5 == 0 else ATT_UNROLL


def _each_class(d, fn):
    if d == 1:
        fn(0)
    else:
        def step(c, _):
            fn(c)
            return 0

        lax.fori_loop(0, d, step, 0, unroll=ATT_UNROLL + 1 if d % (ATT_UNROLL + 1) == 0 and d > ATT_UNROLL + 1 else 1)


def _pair_bias(b):
    off = jnp.full((ATT_BLOCK, ATT_BLOCK), MASK_VALUE, F32)
    return jnp.concatenate([jnp.concatenate([b, off], axis=1), jnp.concatenate([off, b], axis=1)], axis=0)


def _attn_fwd(qkv_g, bias, *, name):
    B, d, L, _ = qkv_g.shape
    nb = L // ATT_BLOCK
    GCOL = GROUP_COLS

    def body(qkv_all, bias_ref, o_all, lse_all):
        if nb == 1 and d % 2 == 0:
            two_classes(qkv_all, bias_ref, o_all, lse_all)
        else:
            _each_class(d, lambda c: one_class(qkv_all.at[c], bias_ref, o_all.at[c], lse_all.at[c]))

    def two_classes(qkv_all, bias_ref, o_all, lse_all):
        bias2 = [_pair_bias(bias_ref[h, :, ATT_BLOCK:2 * ATT_BLOCK]) for h in range(HEADS_PER_GROUP)]

        def pair(p, _):
            two = pl.ds(2 * p, 2)
            x = qkv_all[two].reshape(2 * ATT_BLOCK, 3 * GCOL)
            qb, kb, vb = x[:, 0:GCOL], x[:, GCOL:2 * GCOL], x[:, 2 * GCOL:3 * GCOL]
            outs, lses = [], []
            for h in range(HEADS_PER_GROUP):
                sl = slice(h * HEAD_DIM, (h + 1) * HEAD_DIM)
                s = lax.dot_general(qb[:, sl], kb[:, sl], _NT, preferred_element_type=F32) * ATT_SCALE + bias2[h]
                m = jnp.max(s, axis=1, keepdims=True)
                e = jnp.exp(s - m)
                ssum = jnp.sum(e, axis=1, keepdims=True)
                outs.append(jnp.dot(e.astype(BF16), vb[:, sl], preferred_element_type=F32) / ssum)
                lses.append(jnp.broadcast_to(m + jnp.log(ssum), (2 * ATT_BLOCK, HEAD_DIM)))
            o_all[two] = jnp.concatenate(outs, axis=1).reshape(2, ATT_BLOCK, GCOL)
            lse_all[two] = jnp.concatenate(lses, axis=1).reshape(2, ATT_BLOCK, GCOL)
            return 0

        lax.fori_loop(0, d // 2, pair, 0, unroll=2)

    def one_class(qkv_ref, bias_ref, o_ref, lse_ref):
        def blk(r0, first):
            nk = ATT_BLOCK if first else 2 * ATT_BLOCK
            k0 = r0 if first else r0 - ATT_BLOCK
            qb = qkv_ref[pl.ds(r0, ATT_BLOCK), 0:GCOL]
            kb = qkv_ref[pl.ds(k0, nk), GCOL:2 * GCOL]
            vb = qkv_ref[pl.ds(k0, nk), 2 * GCOL:3 * GCOL]
            outs, lses = [], []
            for h in range(HEADS_PER_GROUP):
                sl = slice(h * HEAD_DIM, (h + 1) * HEAD_DIM)
                bh = bias_ref[h, :, ATT_BLOCK:2 * ATT_BLOCK] if first else bias_ref[h]
                s = lax.dot_general(qb[:, sl], kb[:, sl], _NT, preferred_element_type=F32) * ATT_SCALE + bh
                m = jnp.max(s, axis=1, keepdims=True)
                e = jnp.exp(s - m)
                ssum = jnp.sum(e, axis=1, keepdims=True)
                outs.append(jnp.dot(e.astype(BF16), vb[:, sl], preferred_element_type=F32) / ssum)
                lses.append(jnp.broadcast_to(m + jnp.log(ssum), (ATT_BLOCK, HEAD_DIM)))
            o_ref[pl.ds(r0, ATT_BLOCK), :] = jnp.concatenate(outs, axis=1)
            lse_ref[pl.ds(r0, ATT_BLOCK), :] = jnp.concatenate(lses, axis=1)

        blk(0, True)
        if nb > 1:
            def loop(n, _):
                blk(pl.multiple_of(n * ATT_BLOCK, ATT_BLOCK), False)
                return 0

            lax.fori_loop(1, nb, loop, 0, unroll=ATT_UNROLL)

    spec = lambda cols: pl.BlockSpec((None, d, L, cols), lambda b: (b, 0, 0, 0))
    oshape = jax.ShapeDtypeStruct((B, d, L, GCOL), F32)
    return pl.pallas_call(
        body, name=name, out_shape=(oshape, oshape), grid=(B,),
        in_specs=[spec(3 * GCOL), pl.BlockSpec((HEADS_PER_GROUP, ATT_BLOCK, 2 * ATT_BLOCK), lambda b: (0, 0, 0))],
        out_specs=(spec(GCOL), spec(GCOL)), compiler_params=_params(("parallel",)),
    )(qkv_g, bias)


def _attn_bwd(qkv_g, doc_g, lse_g, dd_g, bias, *, name):
    B, d, L, _ = qkv_g.shape
    nb = L // ATT_BLOCK
    GCOL = GROUP_COLS

    def body(qkv_all, doc_all, lse_all, dd_all, bias_ref, dqkv_all, dbias_ref, dk_ref, dv_ref):
        @pl.when(pl.program_id(0) == 0)
        def _():
            dbias_ref[...] = jnp.zeros_like(dbias_ref)

        if nb == 1 and d % 2 == 0:
            two_classes(qkv_all, doc_all, lse_all, dd_all, bias_ref, dqkv_all, dbias_ref)
        else:
            _each_class(d, lambda c: one_class(qkv_all.at[c], doc_all.at[c], lse_all.at[c], dd_all.at[c], bias_ref,
                                               dqkv_all.at[c], dbias_ref, dk_ref, dv_ref))

    def two_classes(qkv_all, doc_all, lse_all, dd_all, bias_ref, dqkv_all, dbias_ref):
        bias2 = [_pair_bias(bias_ref[h, :, ATT_BLOCK:2 * ATT_BLOCK]) for h in range(HEADS_PER_GROUP)]
        A = ATT_BLOCK

        def pair(p, _):
            two = pl.ds(2 * p, 2)
            x = qkv_all[two].reshape(2 * A, 3 * GCOL)
            qb, kb, vb = x[:, 0:GCOL], x[:, GCOL:2 * GCOL], x[:, 2 * GCOL:3 * GCOL]
            dob = doc_all[two].reshape(2 * A, GCOL).astype(BF16)
            lse = lse_all[two].reshape(2 * A, GCOL)
            dd = dd_all[two].reshape(2 * A, GCOL)
            dqs, dks, dvs = [], [], []
            for h in range(HEADS_PER_GROUP):
                sl = slice(h * HEAD_DIM, (h + 1) * HEAD_DIM)
                c0 = h * HEAD_DIM
                s = lax.dot_general(qb[:, sl], kb[:, sl], _NT, preferred_element_type=F32) * ATT_SCALE + bias2[h]
                pr = jnp.exp(s - lse[:, c0:c0 + 1])
                dp = lax.dot_general(dob[:, sl], vb[:, sl], _NT, preferred_element_type=F32)
                ds = pr * (dp - dd[:, c0:c0 + 1])
                dbias_ref[h, :, A:2 * A] += ds[0:A, 0:A] + ds[A:2 * A, A:2 * A]
                dsb = ds.astype(BF16)
                dqs.append(jnp.dot(dsb, kb[:, sl], preferred_element_type=F32) * ATT_SCALE)
                dks.append(lax.dot_general(dsb, qb[:, sl], _TN, preferred_element_type=F32) * ATT_SCALE)
                dvs.append(lax.dot_general(pr.astype(BF16), dob[:, sl], _TN, preferred_element_type=F32))
            dqkv_all[two] = jnp.concatenate(dqs + dks + dvs, axis=1).astype(BF16).reshape(2, A, 3 * GCOL)
            return 0

        lax.fori_loop(0, d // 2, pair, 0, unroll=2)

    def one_class(qkv_ref, doc_ref, lse_ref, dd_ref, bias_ref, dqkv_ref, dbias_ref, dk_ref, dv_ref):
        dk_ref[...] = jnp.zeros_like(dk_ref)
        dv_ref[...] = jnp.zeros_like(dv_ref)

        def blk(r0, first):
            nk = ATT_BLOCK if first else 2 * ATT_BLOCK
            k0 = r0 if first else r0 - ATT_BLOCK
            rows = pl.ds(r0, ATT_BLOCK)
            krows = pl.ds(k0, nk)
            qb = qkv_ref[rows, 0:GCOL]
            kb = qkv_ref[krows, GCOL:2 * GCOL]
            vb = qkv_ref[krows, 2 * GCOL:3 * GCOL]
            dob = doc_ref[rows, :].astype(BF16)
            lse = lse_ref[rows, :]
            dd = dd_ref[rows, :]
            dqs, dks, dvs = [], [], []
            for h in range(HEADS_PER_GROUP):
                sl = slice(h * HEAD_DIM, (h + 1) * HEAD_DIM)
                c0 = h * HEAD_DIM
                bh = bias_ref[h, :, ATT_BLOCK:2 * ATT_BLOCK] if first else bias_ref[h]
                s = lax.dot_general(qb[:, sl], kb[:, sl], _NT, preferred_element_type=F32) * ATT_SCALE + bh
                p = jnp.exp(s - lse[:, c0:c0 + 1])
                dp = lax.dot_general(dob[:, sl], vb[:, sl], _NT, preferred_element_type=F32)
                ds = p * (dp - dd[:, c0:c0 + 1])
                if first:
                    dbias_ref[h, :, ATT_BLOCK:2 * ATT_BLOCK] += ds
                else:
                    dbias_ref[h] += ds
                dsb = ds.astype(BF16)
                dqs.append(jnp.dot(dsb, kb[:, sl], preferred_element_type=F32) * ATT_SCALE)
                dks.append(lax.dot_general(dsb, qb[:, sl], _TN, preferred_element_type=F32) * ATT_SCALE)
                dvs.append(lax.dot_general(p.astype(BF16), dob[:, sl], _TN, preferred_element_type=F32))
            dqkv_ref[rows, 0:GCOL] = jnp.concatenate(dqs, axis=1).astype(BF16)
            dk_ref[krows, :] += jnp.concatenate(dks, axis=1)
            dv_ref[krows, :] += jnp.concatenate(dvs, axis=1)

        blk(0, True)
        if nb > 1:
            def loop(n, _):
                blk(pl.multiple_of(n * ATT_BLOCK, ATT_BLOCK), False)
                return 0

            lax.fori_loop(1, nb, loop, 0, unroll=_att_unroll(nb - 1))
        dqkv_ref[:, GCOL:2 * GCOL] = dk_ref[...].astype(BF16)
        dqkv_ref[:, 2 * GCOL:3 * GCOL] = dv_ref[...].astype(BF16)

    spec = lambda cols: pl.BlockSpec((None, d, L, cols), lambda b: (b, 0, 0, 0))
    bspec = pl.BlockSpec((HEADS_PER_GROUP, ATT_BLOCK, 2 * ATT_BLOCK), lambda b: (0, 0, 0))
    return pl.pallas_call(
        body, name=name,
        out_shape=(jax.ShapeDtypeStruct((B, d, L, 3 * GCOL), BF16),
                   jax.ShapeDtypeStruct((HEADS_PER_GROUP, ATT_BLOCK, 2 * ATT_BLOCK), F32)),
        grid=(B,),
        in_specs=[spec(3 * GCOL), spec(GCOL), spec(GCOL), spec(GCOL), bspec],
        out_specs=(spec(3 * GCOL), bspec),
        scratch_shapes=[pltpu.VMEM((L, GCOL), F32), pltpu.VMEM((L, GCOL), F32)],
        compiler_params=_params(("arbitrary",)),
    )(qkv_g, doc_g, lse_g, dd_g, bias)


def _attn_combine(os_, lses, *, name):
    T, GC = os_[0].shape
    n = len(os_)

    def body(*refs):
        o_refs, l_refs, oc_ref, lt_ref = refs[:n], refs[n:2 * n], refs[2 * n], refs[2 * n + 1]
        ls = [r[...] for r in l_refs]
        m = functools.reduce(jnp.maximum, ls)
        ws = [jnp.exp(l - m) for l in ls]
        tot = functools.reduce(jnp.add, ws)
        acc = functools.reduce(jnp.add, [w * r[...] for w, r in zip(ws, o_refs)])
        oc_ref[...] = acc / tot
        lt_ref[...] = m + jnp.log(tot)

    shp = jax.ShapeDtypeStruct((T, GC), F32)
    return pl.pallas_call(body, name=name, out_shape=(shp, shp), grid=(T // ROW_TILE,),
                          in_specs=[_row_spec(GC)] * (2 * n), out_specs=(_row_spec(GC), _row_spec(GC)),
                          compiler_params=_params(("parallel",)))(*os_, *lses)


def _attn_rowdot(doc, oc, *, name):
    T, GC = oc.shape

    def body(doc_ref, oc_ref, dd_ref):
        prod = doc_ref[...].astype(F32) * oc_ref[...]
        parts = []
        for h in range(GC // HEAD_DIM):
            s = jnp.sum(prod[:, h * HEAD_DIM:(h + 1) * HEAD_DIM], axis=1, keepdims=True)
            parts.append(jnp.broadcast_to(s, (ROW_TILE, HEAD_DIM)))
        dd_ref[...] = jnp.concatenate(parts, axis=1)

    return pl.pallas_call(body, name=name, out_shape=jax.ShapeDtypeStruct((T, GC), F32), grid=(T // ROW_TILE,),
                          in_specs=[_row_spec(GC), _row_spec(GC)], out_specs=_row_spec(GC),
                          compiler_params=_params(("parallel",)))(doc, oc)


def _bucket_sum(dbias, buckets, *, name):
    depth, NH = dbias.shape[:2]

    def body(db_ref, bk_ref, out_ref):
        rows = lax.broadcasted_iota(jnp.int32, (REL_BUCKETS, LANE), 0)
        cols = lax.broadcasted_iota(jnp.int32, (REL_BUCKETS, LANE), 1)

        def per_bucket(b, acc):
            for h in range(NH):
                sel = bk_ref[h // HEADS_PER_GROUP] == b
                tot = functools.reduce(jnp.add, [db_ref[l, h] for l in range(depth)])
                s = jnp.sum(jnp.where(sel, tot, 0.0))
                acc = acc + jnp.where(jnp.logical_and(rows == b, cols == h), s, 0.0)
            return acc

        out_ref[...] = lax.fori_loop(0, REL_BUCKETS, per_bucket, jnp.zeros((REL_BUCKETS, LANE), F32))

    return pl.pallas_call(body, name=name, out_shape=jax.ShapeDtypeStruct((REL_BUCKETS, LANE), F32),
                          compiler_params=pltpu.CompilerParams(vmem_limit_bytes=VMEM_LIMIT))(dbias, buckets)


def _merge_fwd(zg, bg, ya, yb, yc, *, name):
    T, D3 = zg.shape
    D = D3 // 3

    def body(zg_ref, bg_ref, ya_ref, yb_ref, yc_ref, out_ref):
        acc = None
        for i, y_ref in enumerate((ya_ref, yb_ref, yc_ref)):
            g = _sigmoid(zg_ref[:, i * D:(i + 1) * D].astype(F32) + bg_ref[:, i * D:(i + 1) * D])
            t = g * y_ref[...].astype(F32)
            acc = t if acc is None else acc + t
        out_ref[...] = acc.astype(BF16)

    return pl.pallas_call(body, name=name, out_shape=jax.ShapeDtypeStruct((T, D), BF16), grid=(T // ROW_TILE,),
                          in_specs=[_row_spec(D3), _vec_spec(D3), _row_spec(D), _row_spec(D), _row_spec(D)],
                          out_specs=_row_spec(D), compiler_params=_params(("parallel",)))(zg, bg.reshape(1, D3), ya, yb, yc)


def _merge_bwd(dm, zg, bg, ya, yb, yc, *, name):
    T, D3 = zg.shape
    D = D3 // 3

    def body(dm_ref, zg_ref, bg_ref, ya_ref, yb_ref, yc_ref, dya_ref, dyb_ref, dyc_ref, dzg_ref, dbg_ref):
        first = pl.program_id(0) == 0
        dm_v = dm_ref[...].astype(F32)
        dbs = []
        for i, (y_ref, dy_ref) in enumerate(((ya_ref, dya_ref), (yb_ref, dyb_ref), (yc_ref, dyc_ref))):
            g = _sigmoid(zg_ref[:, i * D:(i + 1) * D].astype(F32) + bg_ref[:, i * D:(i + 1) * D])
            dy_ref[...] = (dm_v * g).astype(BF16)
            dz = dm_v * y_ref[...].astype(F32) * g * (1.0 - g)
            dzg_ref[:, i * D:(i + 1) * D] = dz.astype(BF16)
            dbs.append(jnp.sum(dz, axis=0, keepdims=True))
        _acc_out(dbg_ref, jnp.concatenate(dbs, axis=1), first)

    bshape = jax.ShapeDtypeStruct((T, D), BF16)
    return pl.pallas_call(
        body, name=name,
        out_shape=(bshape, bshape, bshape, jax.ShapeDtypeStruct((T, D3), BF16), jax.ShapeDtypeStruct((1, D3), F32)),
        grid=(T // ROW_TILE,),
        in_specs=[_row_spec(D), _row_spec(D3), _vec_spec(D3), _row_spec(D), _row_spec(D), _row_spec(D)],
        out_specs=(_row_spec(D), _row_spec(D), _row_spec(D), _row_spec(D3), _vec_spec(D3)),
        compiler_params=_params(("arbitrary",)))(dm, zg, bg.reshape(1, D3), ya, yb, yc)


XQ_TILE = 1024


def _xattn_fwd(q, kv, *, B, S, name):
    D = q.shape[1]
    M = kv.shape[0] // B
    E = D // X_HEADS
    scale = E ** -0.5

    def body(q_ref, kv_ref, o_ref):
        outs = []
        for h in range(X_HEADS):
            s = lax.dot_general(q_ref[:, h * E:(h + 1) * E], kv_ref[:, h * E:(h + 1) * E], _NT,
                                preferred_element_type=F32) * scale
            e = jnp.exp(s - jnp.max(s, axis=1, keepdims=True))
            p = e / jnp.sum(e, axis=1, keepdims=True)
            outs.append(jnp.dot(p.astype(BF16), kv_ref[:, D + h * E:D + (h + 1) * E], preferred_element_type=F32))
        o_ref[...] = jnp.concatenate(outs, axis=1).astype(BF16)

    nq = S // XQ_TILE
    return pl.pallas_call(
        body, name=name, out_shape=jax.ShapeDtypeStruct((B * S, D), BF16), grid=(B, nq),
        in_specs=[pl.BlockSpec((XQ_TILE, D), lambda b, i: (b * nq + i, 0)), pl.BlockSpec((M, 2 * D), lambda b, i: (b, 0))],
        out_specs=pl.BlockSpec((XQ_TILE, D), lambda b, i: (b * nq + i, 0)),
        compiler_params=_params(("parallel", "parallel")))(q, kv)


def _xattn_bwd(q, kv, do, *, B, S, name):
    D = q.shape[1]
    M = kv.shape[0] // B
    E = D // X_HEADS
    scale = E ** -0.5

    def body(q_ref, kv_ref, do_ref, dq_ref, dkv_ref):
        first = pl.program_id(1) == 0
        dqs, dks, dvs = [], [], []
        for h in range(X_HEADS):
            qh = q_ref[:, h * E:(h + 1) * E]
            kh = kv_ref[:, h * E:(h + 1) * E]
            vh = kv_ref[:, D + h * E:D + (h + 1) * E]
            doh = do_ref[:, h * E:(h + 1) * E]
            s = lax.dot_general(qh, kh, _NT, preferred_element_type=F32) * scale
            e = jnp.exp(s - jnp.max(s, axis=1, keepdims=True))
            p = e / jnp.sum(e, axis=1, keepdims=True)
            dp = lax.dot_general(doh, vh, _NT, preferred_element_type=F32)
            ds = (p * (dp - jnp.sum(p * dp, axis=1, keepdims=True))).astype(BF16)
            dqs.append(jnp.dot(ds, kh, preferred_element_type=F32) * scale)
            dks.append(lax.dot_general(ds, qh, _TN, preferred_element_type=F32) * scale)
            dvs.append(lax.dot_general(p.astype(BF16), doh, _TN, preferred_element_type=F32))
        dq_ref[...] = jnp.concatenate(dqs, axis=1).astype(BF16)
        _acc_out(dkv_ref, jnp.concatenate(dks + dvs, axis=1), first)

    nq = S // XQ_TILE
    qspec = pl.BlockSpec((XQ_TILE, D), lambda b, i: (b * nq + i, 0))
    kvspec = pl.BlockSpec((M, 2 * D), lambda b, i: (b, 0))
    return pl.pallas_call(
        body, name=name,
        out_shape=(jax.ShapeDtypeStruct((B * S, D), BF16), jax.ShapeDtypeStruct((B * M, 2 * D), F32)),
        grid=(B, nq), in_specs=[qspec, kvspec, qspec], out_specs=(qspec, kvspec),
        compiler_params=_params(("arbitrary", "arbitrary")))(q, kv, do)


FFN_COLS = 256
FFN_PAD = 8
FFN_CH = 256
FFN_K = 3


def _ffn_gate(win, w_ref, cb_ref, n):
    g = jnp.broadcast_to(cb_ref[...], (n, win.shape[1]))
    for j in range(FFN_K):
        off = FFN_PAD - (FFN_K - 1) + j
        g = g + w_ref[j:j + 1, :] * win[off:off + n, :]
    return g


def _ffn_interleave(w):
    lead, n2 = w.shape[:-1], w.shape[-1]
    nc = n2 // (2 * FFN_COLS)
    return jnp.swapaxes(w.reshape(lead + (2, nc, FFN_COLS)), -3, -2).reshape(lead + (n2,))


def _ffn_deinterleave(w):
    lead, n2 = w.shape[:-1], w.shape[-1]
    nc = n2 // (2 * FFN_COLS)
    return jnp.swapaxes(w.reshape(lead + (nc, 2, FFN_COLS)), -3, -2).reshape(lead + (n2,))


def _ffn_fwd(up, conv_w, conv_b, *, B, S, name):
    F = conv_w.shape[1]
    nc = F // FFN_COLS

    def body(gp_ref, val_ref, w_ref, cb_ref, act_ref, pad_ref):
        pad_ref[0:FFN_PAD, :] = jnp.zeros((FFN_PAD, FFN_COLS), F32)
        pad_ref[FFN_PAD:, :] = gp_ref[...].astype(F32)

        def chunk(i, _):
            base = pl.multiple_of(i * FFN_CH, FFN_CH)
            gate = _ffn_gate(pad_ref[pl.ds(base, FFN_CH + FFN_PAD), :], w_ref, cb_ref, FFN_CH)
            act_ref[pl.ds(base, FFN_CH), :] = (_gelu(gate) * val_ref[pl.ds(base, FFN_CH), :].astype(F32)).astype(BF16)
            return 0

        lax.fori_loop(0, S // FFN_CH, chunk, 0)

    return pl.pallas_call(
        body, name=name, out_shape=jax.ShapeDtypeStruct((B * S, F), BF16), grid=(B, nc),
        in_specs=[pl.BlockSpec((S, FFN_COLS), lambda b, j: (b, 2 * j)), pl.BlockSpec((S, FFN_COLS), lambda b, j: (b, 2 * j + 1)),
                  pl.BlockSpec((FFN_K, FFN_COLS), lambda b, j: (0, j)), pl.BlockSpec((1, FFN_COLS), lambda b, j: (0, j))],
        out_specs=pl.BlockSpec((S, FFN_COLS), lambda b, j: (b, j)),
        scratch_shapes=[pltpu.VMEM((S + FFN_PAD, FFN_COLS), F32)],
        compiler_params=_params(("parallel", "parallel")))(up, up, conv_w, conv_b.reshape(1, F))


def _ffn_bwd(dact, up, conv_w, conv_b, *, B, S, name):
    F = conv_w.shape[1]
    nc = F // FFN_COLS
    SUB = 8

    def body(dact_ref, gp_ref, val_ref, w_ref, cb_ref, dup_ref, dw_ref, dcb_ref, pad_ref, pad2_ref, acc_ref):
        first = pl.program_id(1) == 0
        pad_ref[0:FFN_PAD, :] = jnp.zeros((FFN_PAD, FFN_COLS), F32)
        pad_ref[FFN_PAD:, :] = gp_ref[...].astype(F32)
        pad2_ref[S:, :] = jnp.zeros((FFN_PAD, FFN_COLS), F32)
        acc_ref[...] = jnp.zeros_like(acc_ref)

        def chunk1(i, _):
            base = pl.multiple_of(i * FFN_CH, FFN_CH)
            rows = pl.ds(base, FFN_CH)
            gate = _ffn_gate(pad_ref[pl.ds(base, FFN_CH + FFN_PAD), :], w_ref, cb_ref, FFN_CH)
            gl, dgl = _gelu_and_grad(gate)
            da = dact_ref[rows, :].astype(F32)
            dup_ref[rows, FFN_COLS:2 * FFN_COLS] = (da * gl).astype(BF16)
            pad2_ref[rows, :] = da * val_ref[rows, :].astype(F32) * dgl
            return 0

        lax.fori_loop(0, S // FFN_CH, chunk1, 0)

        def chunk2(i, _):
            base = pl.multiple_of(i * FFN_CH, FFN_CH)
            rows = pl.ds(base, FFN_CH)
            win2 = pad2_ref[pl.ds(base, FFN_CH + FFN_PAD), :]
            win1 = pad_ref[pl.ds(base, FFN_CH + FFN_PAD), :]
            dg = win2[0:FFN_CH, :]
            dgp = jnp.zeros((FFN_CH, FFN_COLS), F32)
            for j in range(FFN_K):
                off2 = FFN_K - 1 - j
                dgp = dgp + w_ref[j:j + 1, :] * win2[off2:off2 + FFN_CH, :]
                off1 = FFN_PAD - (FFN_K - 1) + j
                prod = dg * win1[off1:off1 + FFN_CH, :]
                acc_ref[j * SUB:(j + 1) * SUB, :] += jnp.sum(prod.reshape(FFN_CH // SUB, SUB, FFN_COLS), axis=0)
            acc_ref[FFN_K * SUB:(FFN_K + 1) * SUB, :] += jnp.sum(dg.reshape(FFN_CH // SUB, SUB, FFN_COLS), axis=0)
            dup_ref[rows, 0:FFN_COLS] = dgp.astype(BF16)
            return 0

        lax.fori_loop(0, S // FFN_CH, chunk2, 0)
        sums = jnp.sum(acc_ref[...].reshape(FFN_K + 1, SUB, FFN_COLS), axis=1)
        _acc_out(dw_ref, sums[0:FFN_K, :], first)
        _acc_out(dcb_ref, sums[FFN_K:FFN_K + 1, :], first)

    return pl.pallas_call(
        body, name=name,
        out_shape=(jax.ShapeDtypeStruct((B * S, 2 * F), BF16),
                   jax.ShapeDtypeStruct((FFN_K, F), F32), jax.ShapeDtypeStruct((1, F), F32)),
        grid=(nc, B),
        in_specs=[pl.BlockSpec((S, FFN_COLS), lambda j, b: (b, j)), pl.BlockSpec((S, FFN_COLS), lambda j, b: (b, 2 * j)),
                  pl.BlockSpec((S, FFN_COLS), lambda j, b: (b, 2 * j + 1)),
                  pl.BlockSpec((FFN_K, FFN_COLS), lambda j, b: (0, j)), pl.BlockSpec((1, FFN_COLS), lambda j, b: (0, j))],
        out_specs=(pl.BlockSpec((S, 2 * FFN_COLS), lambda j, b: (b, j)),
                   pl.BlockSpec((FFN_K, FFN_COLS), lambda j, b: (0, j)), pl.BlockSpec((1, FFN_COLS), lambda j, b: (0, j))),
        scratch_shapes=[pltpu.VMEM((S + FFN_PAD, FFN_COLS), F32), pltpu.VMEM((S + FFN_PAD, FFN_COLS), F32),
                        pltpu.VMEM(((FFN_K + 1) * SUB, FFN_COLS), F32)],
        compiler_params=_params(("arbitrary", "arbitrary")))(dact, up, up, conv_w, conv_b.reshape(1, F))


def _row_tile(rows, row_bytes, budget):
    tr = rows
    if rows * row_bytes > budget:
        for t in range(16, rows, 16):
            if rows % t == 0 and t * row_bytes <= budget:
                tr = t
    return tr


def _adamw(w, parts, m, v, *, name, own=None, own_slot=None):
    shape = w.shape
    L = len(parts)
    n = parts[0].shape[0]
    cols = shape[-1]
    rows = w.size // (cols * L)
    w3, m3, v3 = (t.reshape(L, rows, cols) for t in (w, m, v))
    tr = _row_tile(rows, cols * 4, 1 << 19)
    c1 = 1.0 - ADAM_B1 ** ADAM_STEP
    c2 = 1.0 - ADAM_B2 ** ADAM_STEP
    has_own = own is not None

    def body(*refs):
        refs = list(refs)
        slot_ref = refs.pop(0) if has_own else None
        w_ref = refs.pop(0)
        p_refs = [refs.pop(0) for _ in range(L)]
        o_refs = [refs.pop(0) for _ in range(L)] if has_own else None
        m_ref, v_ref, g_ref, d_ref, mo_ref, vo_ref = refs
        layer = pl.program_id(0)
        gv = None
        for j in range(L):
            gj = None
            for i in range(n):
                t = p_refs[j][i].astype(F32)
                if has_own:
                    t = jnp.where(slot_ref[0] == i, o_refs[j][...].astype(F32), t)
                gj = t if gj is None else gj + t
            gv = gj if gv is None else jnp.where(layer == j, gj, gv)
        g_ref[...] = gv
        mn = ADAM_B1 * m_ref[...] + (1.0 - ADAM_B1) * gv
        vn = ADAM_B2 * v_ref[...] + (1.0 - ADAM_B2) * (gv * gv)
        d_ref[...] = -ADAM_LR * ((mn / c1) / (jnp.sqrt(vn / c2) + ADAM_EPS) + ADAM_WD * w_ref[...])
        mo_ref[...] = mn
        vo_ref[...] = vn

    spec = pl.BlockSpec((None, tr, cols), lambda l, i, *_: (l, i, 0))

    def pspec(j):
        return pl.BlockSpec((n, tr, cols), lambda l, i, *_: (0, jnp.where(l == j, i, 0), 0))

    def ospec(j):
        return pl.BlockSpec((None, tr, cols), lambda l, i, slot: (slot[0], jnp.where(l == j, i, 0), 0))

    oshape = jax.ShapeDtypeStruct((L, rows, cols), F32)
    in_specs = [spec] + [pspec(j) for j in range(L)] + ([ospec(j) for j in range(L)] if has_own else []) + [spec, spec]
    args = [w3] + [p.reshape(n, rows, cols) for p in parts]
    if has_own:
        args += [o.reshape(n, rows, cols) for o in own]
    args += [m3, v3]
    grid = (L, rows // tr)
    if has_own:
        grid_spec = pltpu.PrefetchScalarGridSpec(num_scalar_prefetch=1, grid=grid, in_specs=in_specs,
                                                 out_specs=(spec,) * 4)
        outs = pl.pallas_call(body, name=name, out_shape=(oshape,) * 4, grid_spec=grid_spec,
                              compiler_params=_params(("parallel", "parallel")))(own_slot, *args)
    else:
        outs = pl.pallas_call(body, name=name, out_shape=(oshape,) * 4, grid=grid, in_specs=in_specs,
                              out_specs=(spec,) * 4, compiler_params=_params(("parallel", "parallel")))(*args)
    return tuple(t.reshape(shape) for t in outs)


def _chip_sum(g, got, core, *, name):
    shard = got.shape[1:]
    cols = shard[-1]
    rows = math.prod(shard) // cols
    tr = _row_tile(rows, cols * 4, 1 << 20)

    def body(c_ref, g_ref, r_ref, o_ref):
        o_ref[...] = (g_ref[...].astype(F32) + r_ref[...].astype(F32)).astype(o_ref.dtype)

    blk = (None, tr, cols)
    grid_spec = pltpu.PrefetchScalarGridSpec(
        num_scalar_prefetch=1, grid=(N_CHIP, rows // tr),
        in_specs=[pl.BlockSpec(blk, lambda s, i, c: (2 * s + c[0], i, 0)), pl.BlockSpec(blk, lambda s, i, c: (s, i, 0))],
        out_specs=pl.BlockSpec(blk, lambda s, i, c: (s, i, 0)))
    out = pl.pallas_call(body, name=name, out_shape=jax.ShapeDtypeStruct((N_CHIP, rows, cols), g.dtype),
                         grid_spec=grid_spec, compiler_params=_params(("parallel", "parallel")),
                         )(core, g.reshape(N_DEV, rows, cols), got.reshape(N_CHIP, rows, cols))
    return out.reshape((N_CHIP,) + shard)


_HBM = pl.BlockSpec(memory_space=pltpu.HBM)
_MESH = pl.DeviceIdType.MESH


def _my_pos():
    return lax.axis_index("x"), lax.axis_index("y"), lax.axis_index("c")


def _flip(pos, k):
    x, y, c = pos
    fx, fy, fc = (k >> 2) & 1, (k >> 1) & 1, k & 1
    return (x ^ fx if fx else x, y ^ fy if fy else y, c ^ fc if fc else c)


def _index_of(pos):
    return 4 * pos[0] + 2 * pos[1] + pos[2]


def _all_gather(xs, *, name):
    n = len(xs)

    def body(*refs):
        x_refs, out_refs = refs[:n], refs[n:2 * n]
        send_sems, recv_sems = refs[2 * n:]
        me = _my_pos()
        sibling = _flip(me, 1)
        chips = [2, 4, 6]

        def copy(i, k, block_pos, to, from_x=False):
            blk = out_refs[i].at[_index_of(block_pos)]
            return pltpu.make_async_remote_copy(src_ref=x_refs[i] if from_x else blk, dst_ref=blk,
                                                send_sem=send_sems.at[k, i], recv_sem=recv_sems.at[k, i],
                                                device_id=to, device_id_type=_MESH)

        first = [copy(i, 1 + j, me, _flip(me, f), from_x=True) for j, f in enumerate(chips) for i in range(n)]
        first += [copy(i, 0, me, sibling, from_x=True) for i in range(n)]
        for cp in first:
            cp.start()
        passed = []
        for j, f in enumerate(chips):
            for i in range(n):
                copy(i, 1 + j, _flip(me, f), me).wait_recv()
                cp = copy(i, 4 + j, _flip(me, f), sibling)
                cp.start()
                passed.append(cp)
        for i in range(n):
            copy(i, 0, sibling, me).wait_recv()
        for j, f in enumerate(chips):
            for i in range(n):
                copy(i, 4 + j, _flip(sibling, f), me).wait_recv()
        for cp in first + passed:
            cp.wait_send()

    return pl.pallas_call(
        body, name=name, out_shape=tuple(jax.ShapeDtypeStruct((N_DEV,) + x.shape, x.dtype) for x in xs),
        in_specs=[_HBM] * n, out_specs=(_HBM,) * n,
        scratch_shapes=[pltpu.SemaphoreType.DMA((7, n)), pltpu.SemaphoreType.DMA((7, n))],
    )(*xs)


N_CHIP = 4


def _scatter_d2d(gs):
    n = len(gs)

    def make(g_refs, recv_refs, send_sems, recv_sems):
        me = _my_pos()
        sibling = _flip(me, 1)
        c = me[2]
        sends = []
        for i in range(n):
            for s in range(N_CHIP):
                sends.append(pltpu.make_async_remote_copy(
                    src_ref=g_refs[i].at[2 * s + 1 - c], dst_ref=recv_refs[i].at[s],
                    send_sem=send_sems.at[N_CHIP * i + s], recv_sem=recv_sems.at[N_CHIP * i + s],
                    device_id=sibling, device_id_type=_MESH))
        return sends, sends

    shapes = [jax.ShapeDtypeStruct((N_CHIP,) + g.shape[1:], g.dtype) for g in gs]
    return _Exchange(list(gs), shapes, N_CHIP * n, make)


def _scatter_ici(ss):
    n = len(ss)

    def make(s_refs, r_refs, send_sems, recv_sems):
        me = _my_pos()
        my_chip = 2 * me[0] + me[1]
        sends, recvs = [], []
        for k in (1, 2, 3):
            peer = _flip(me, 2 * k)
            peer_chip = 2 * peer[0] + peer[1]
            for i in range(n):
                j = 3 * i + k - 1
                sends.append(pltpu.make_async_remote_copy(
                    src_ref=s_refs[i].at[peer_chip], dst_ref=r_refs[i].at[my_chip], send_sem=send_sems.at[j],
                    recv_sem=recv_sems.at[j], device_id=peer, device_id_type=_MESH))
                recvs.append(pltpu.make_async_remote_copy(
                    src_ref=s_refs[i].at[my_chip], dst_ref=r_refs[i].at[peer_chip], send_sem=send_sems.at[j],
                    recv_sem=recv_sems.at[j], device_id=peer, device_id_type=_MESH))
        return sends, recvs

    return _Exchange(list(ss), [jax.ShapeDtypeStruct(s.shape, s.dtype) for s in ss], 3 * n, make)


def _gather_ici(shards, layers):
    n = len(shards)
    FLIPS = (2, 4, 6, 1)

    def make(x_refs, out_refs, send_sems, recv_sems):
        me = _my_pos()
        sends, recvs = [], []
        for k, f in enumerate(FLIPS):
            peer = _flip(me, f)
            for i in range(n):
                src = x_refs[i] if layers[i] is None else x_refs[i].at[layers[i]]
                j = len(FLIPS) * i + k
                sends.append(pltpu.make_async_remote_copy(
                    src_ref=src, dst_ref=out_refs[i].at[_index_of(me)], send_sem=send_sems.at[j],
                    recv_sem=recv_sems.at[j], device_id=peer, device_id_type=_MESH))
                recvs.append(pltpu.make_async_remote_copy(
                    src_ref=src, dst_ref=out_refs[i].at[_index_of(peer)], send_sem=send_sems.at[j],
                    recv_sem=recv_sems.at[j], device_id=peer, device_id_type=_MESH))
        return sends, recvs

    shapes = [jax.ShapeDtypeStruct((N_DEV,) + (x.shape if l is None else x.shape[1:]), x.dtype)
              for x, l in zip(shards, layers)]
    return _Exchange(list(shards), shapes, len(FLIPS) * n, make)


def _gather_d2d(blocks):
    n = len(blocks)
    FLIPS = (2, 4, 6)

    def make(in_refs, out_refs, send_sems, recv_sems):
        me = _my_pos()
        sibling = _flip(me, 1)
        sends, recvs = [], []
        for k, f in enumerate(FLIPS):
            for i in range(n):
                j = len(FLIPS) * i + k
                mine = out_refs[i].at[_index_of(_flip(me, f))]
                sends.append(pltpu.make_async_remote_copy(
                    src_ref=mine, dst_ref=mine, send_sem=send_sems.at[j], recv_sem=recv_sems.at[j],
                    device_id=sibling, device_id_type=_MESH))
                recvs.append(pltpu.make_async_remote_copy(
                    src_ref=mine, dst_ref=out_refs[i].at[_index_of(_flip(sibling, f))], send_sem=send_sems.at[j],
                    recv_sem=recv_sems.at[j], device_id=sibling, device_id_type=_MESH))
        return sends, recvs

    shapes = [jax.ShapeDtypeStruct(b.shape, b.dtype) for b in blocks]
    return _Exchange(list(blocks), shapes, len(FLIPS) * n, make, aliases={i: i for i in range(n)})


class _Exchange:
    def __init__(self, ins, out_shapes, n_copies, make, aliases=None):
        self.ins, self.out_shapes, self.n_copies, self.make, self.aliases = ins, out_shapes, n_copies, make, aliases or {}


def _exchange_call(ex, *, name):
    n_in, n_out = len(ex.ins), len(ex.out_shapes)

    def body(*refs):
        sends, recvs = ex.make(refs[:n_in], refs[n_in:n_in + n_out], refs[n_in + n_out], refs[n_in + n_out + 1])
        for cp in sends:
            cp.start()
        for cp in recvs:
            cp.wait_recv()
        for cp in sends:
            cp.wait_send()

    return pl.pallas_call(
        body, name=name, out_shape=tuple(ex.out_shapes), in_specs=[_HBM] * n_in, out_specs=(_HBM,) * n_out,
        scratch_shapes=[pltpu.SemaphoreType.DMA((ex.n_copies,)), pltpu.SemaphoreType.DMA((ex.n_copies,))],
        input_output_aliases=ex.aliases,
    )(*ex.ins)


class _Hosts:
    def __init__(self):
        self.plan = {}

    def add(self, name, build, done):
        assert name not in self.plan
        self.plan[name] = (build, done)

    def get(self, name):
        return self.plan[name][0]() if name in self.plan else None

    def put(self, name, outs):
        self.plan.pop(name)[1](outs)


def _t5_bucket(dist):
    n = jnp.maximum(dist, 0)
    max_exact = REL_BUCKETS // 2
    nf = jnp.maximum(n, 1).astype(F32)
    large = max_exact + (jnp.log(nf / max_exact) / math.log(REL_MAX_DIST / max_exact)
                         * (REL_BUCKETS - max_exact)).astype(jnp.int32)
    large = jnp.minimum(large, REL_BUCKETS - 1)
    return jnp.where(n < max_exact, n, large)


def _bias_tables(rel_bias):
    qi = jnp.arange(ATT_BLOCK)[:, None]
    ki = jnp.arange(2 * ATT_BLOCK)[None, :]
    rel = qi + ATT_BLOCK - ki
    out = []
    for gi, (window, dil) in enumerate(DIL_GROUPS):
        span = window // dil
        bucket = _t5_bucket(rel * dil)
        valid = (rel >= 0) & (rel <= span)
        rb = rel_bias[:, gi * HEADS_PER_GROUP:(gi + 1) * HEADS_PER_GROUP]
        tab = functools.reduce(jnp.add, [jnp.where((bucket == b)[:, :, None], rb[b], 0.0)
                                         for b in range(REL_BUCKETS)])
        tab = jnp.where(valid[:, :, None], tab, MASK_VALUE).transpose(2, 0, 1)
        out.append((tab.astype(F32), bucket.astype(jnp.int32)))
    return out


def _regroup(t, B, S, d):
    C = t.shape[-1]
    return t.reshape(B, S // d, d, C).swapaxes(1, 2)


def _ungroup(t):
    B, d, L, C = t.shape
    return t.swapaxes(1, 2).reshape(B * L * d, C)


def _group_qkv(qkv, gi):
    n = len(DIL_GROUPS) * GROUP_COLS
    return jnp.concatenate([qkv[:, j * n + gi * GROUP_COLS: j * n + (gi + 1) * GROUP_COLS] for j in range(3)], axis=1)


def _layer_fwd(l, x0, h1, mem2, W, P, tabs, next_pre_g, *, B, S, hosts=None):
    tag = f"l{l}_"
    mm = functools.partial(_mm, hosts=hosts)
    sv = {"x0": x0, "h1": h1}
    za = mm(h1, W["in_a"], out_dtype=ACT, name=tag + "mm_in_a")
    zb = mm(h1, W["in_b"], out_dtype=ACT, name=tag + "mm_in_b")
    qkv = mm(h1, W["in_c"], out_dtype=BF16, name=tag + "mm_in_c")
    zg = mm(h1, W["in_g"], out_dtype=ACT, name=tag + "mm_in_g")
    a4, a2 = _bra_fwd(za, P["conv_a_w"], P["conv_a_b"], P["ln_a_g"], P["ln_a_b"], B=B, S=S, name=tag + "bra_fwd")
    ya = mm(a4, W["a_out"], out_dtype=ACT, name=tag + "mm_a_out")
    b_s_t = P["b_s"].T
    p = _brb_fwd(zb, P["ln_b_g"], P["ln_b_b"], P["w_s"], b_s_t, name=tag + "brb_fwd")
    yb = mm(p, W["b_out"], out_dtype=ACT, name=tag + "mm_b_out")
    os_, lses, qkv_gs = [], [], []
    for gi, (_, dil) in enumerate(DIL_GROUPS):
        qkv_g = _regroup(_group_qkv(qkv, gi), B, S, dil)
        o_g, lse_g = _attn_fwd(qkv_g, tabs[gi][0], name=tag + f"attn_fwd{gi}")
        qkv_gs.append(qkv_g)
        os_.append(_ungroup(o_g))
        lses.append(_ungroup(lse_g))
    oc, lse_tot = _attn_combine(os_, lses, name=tag + "attn_combine")
    yc = mm(oc, W["c_out"], out_dtype=ACT, name=tag + "mm_c_out")
    merged = _merge_fwd(zg, P["b_gate"], ya, yb, yc, name=tag + "merge_fwd")
    y1 = mm(merged, W["mix_out"], out_dtype=ACT, name=tag + "mm_mix")
    x1, h2 = _norm_fwd(x0, y1, P["mix_post_g"], P["x_pre_g"], name=tag + "norm1")
    q = mm(h2, W["xq"], out_dtype=BF16, name=tag + "mm_xq")
    (memn,) = _norm_fwd(mem2, None, None, P["mem_g"], name=tag + "norm_mem")
    kv = mm(memn, W["xkv"], out_dtype=BF16, name=tag + "mm_xkv")
    ox = _xattn_fwd(q, kv, B=B, S=S, name=tag + "xattn_fwd")
    y2 = mm(ox, W["xo"], out_dtype=ACT, name=tag + "mm_xo")
    x2, h3 = _norm_fwd(x1, y2, P["x_post_g"], P["ffn_pre_g"], name=tag + "norm2")
    up = mm(h3, W["up"], out_dtype=ACT, name=tag + "mm_up")
    act = _ffn_fwd(up, P["conv_f_w"], P["conv_f_b"], B=B, S=S, name=tag + "ffn_fwd")
    y3 = mm(act, W["down"], out_dtype=ACT, name=tag + "mm_down")
    outs = _norm_fwd(x2, y3, P["ffn_post_g"], next_pre_g, name=tag + "norm3")
    x3 = outs[0]
    h_next = outs[1] if next_pre_g is not None else None
    sv.update(za=za, zb=zb, zg=zg, a4=a4, a2=a2, ya=ya, p=p, yb=yb, qkv_gs=qkv_gs, oc=oc, lse_tot=lse_tot, yc=yc,
              merged=merged, y1=y1, x1=x1, h2=h2, q=q, memn=memn, kv=kv, ox=ox, y2=y2, x2=x2, h3=h3, up=up, act=act,
              y3=y3, x3=x3, b_s_t=b_s_t)
    return x3, h_next, sv


def _layer_bwd(l, d, dh_next, next_pre_g, sv, mem2, W, WT, P, tabs, G, *, B, S, hosts=None):
    tag = f"l{l}_"
    mm = functools.partial(_mm, hosts=hosts)
    mm_tn = functools.partial(_mm_tn, hosts=hosts)
    if dh_next is not None:
        d, dy3, G["ffn_post_g"], dg_next = _norm_bwd(d, dh_next, sv["x3"], next_pre_g, sv["y3"], P["ffn_post_g"],
                                                     name=tag + "norm3_bwd")
    else:
        dy3, G["ffn_post_g"] = _norm_bwd(d, None, None, None, sv["y3"], P["ffn_post_g"], name=tag + "norm3_bwd")
        dg_next = None
    dact = mm(dy3, WT["down"], out_dtype=ACT, name=tag + "mm_down_dx")
    G["w_down"] = mm_tn(sv["act"], dy3, name=tag + "mm_down_dw")
    dup, G["conv_f_w"], G["conv_f_b"] = _ffn_bwd(dact, sv["up"], P["conv_f_w"], P["conv_f_b"], B=B, S=S,
                                                name=tag + "ffn_bwd")
    dh3 = mm(dup, WT["up"], out_dtype=ACT, name=tag + "mm_up_dx")
    G["w_up"] = _ffn_deinterleave(mm_tn(sv["h3"], dup, name=tag + "mm_up_dw"))
    d, dy2, G["x_post_g"], G["ffn_pre_g"] = _norm_bwd(d, dh3, sv["x2"], P["ffn_pre_g"], sv["y2"], P["x_post_g"],
                                                      name=tag + "norm2_bwd")
    dox = mm(dy2, WT["xo"], out_dtype=BF16, name=tag + "mm_xo_dx")
    G["w_xo"] = mm_tn(sv["ox"], dy2, name=tag + "mm_xo_dw")
    dq, dkv = _xattn_bwd(sv["q"], sv["kv"], dox, B=B, S=S, name=tag + "xattn_bwd")
    dh2 = mm(dq, WT["xq"], out_dtype=ACT, name=tag + "mm_xq_dx")
    G["w_xq"] = mm_tn(sv["h2"], dq, name=tag + "mm_xq_dw")
    G["w_xkv"] = mm_tn(sv["memn"], dkv, name=tag + "mm_xkv_dw")
    dmemn = mm(dkv, WT["xkv"], out_dtype=F32, name=tag + "mm_xkv_dx")
    (G["mem_g"],) = _norm_bwd(None, dmemn, mem2, P["mem_g"], None, None, name=tag + "norm_mem_bwd")
    d, dy1, G["mix_post_g"], G["x_pre_g"] = _norm_bwd(d, dh2, sv["x1"], P["x_pre_g"], sv["y1"], P["mix_post_g"],
                                                      name=tag + "norm1_bwd")
    dm = mm(dy1, WT["mix_out"], out_dtype=ACT, name=tag + "mm_mix_dx")
    G["w_mix_out"] = mm_tn(sv["merged"], dy1, name=tag + "mm_mix_dw")
    dya, dyb, dyc, dzg, dbg = _merge_bwd(dm, sv["zg"], P["b_gate"], sv["ya"], sv["yb"], sv["yc"], name=tag + "merge_bwd")
    G["b_gate"] = dbg.reshape(P["b_gate"].shape)
    da4 = mm(dya, WT["a_out"], out_dtype=ACT, name=tag + "mm_a_out_dx")
    G["w_a_out"] = mm_tn(sv["a4"], dya, name=tag + "mm_a_out_dw")
    da2, G["ln_a_g"], G["ln_a_b"], G["conv_a_b"] = _bra_bwd_ln(da4, sv["a2"], P["ln_a_g"], P["ln_a_b"], name=tag + "bra_bwd_ln")
    dza, G["conv_a_w"] = _bra_bwd_conv(da2, sv["za"], P["conv_a_w"], B=B, S=S, name=tag + "bra_bwd_conv")
    dp = mm(dyb, WT["b_out"], out_dtype=ACT, name=tag + "mm_b_out_dx")
    G["w_b_out"] = mm_tn(sv["p"], dyb, name=tag + "mm_b_out_dw")
    dzb, G["w_s"], dbs_t, G["ln_b_g"], G["ln_b_b"] = _brb_bwd(dp, sv["zb"], P["ln_b_g"], P["ln_b_b"], P["w_s"],
                                                             sv["b_s_t"], name=tag + "brb_bwd")
    G["b_s"] = dbs_t.T
    doc = mm(dyc, WT["c_out"], out_dtype=ACT, name=tag + "mm_c_out_dx")
    G["w_c_out"] = mm_tn(sv["oc"], dyc, name=tag + "mm_c_out_dw")
    dd = _attn_rowdot(doc, sv["oc"], name=tag + "attn_rowdot")
    dq_parts, dk_parts, dv_parts, dbiases = [], [], [], []
    for gi, (_, dil) in enumerate(DIL_GROUPS):
        dqkv_g, dbias = _attn_bwd(sv["qkv_gs"][gi], _regroup(doc, B, S, dil), _regroup(sv["lse_tot"], B, S, dil),
                                  _regroup(dd, B, S, dil), tabs[gi][0], name=tag + f"attn_bwd{gi}")
        t = _ungroup(dqkv_g)
        dq_parts.append(t[:, 0:GROUP_COLS])
        dk_parts.append(t[:, GROUP_COLS:2 * GROUP_COLS])
        dv_parts.append(t[:, 2 * GROUP_COLS:3 * GROUP_COLS])
        dbiases.append(dbias)
    dqkv = jnp.concatenate(dq_parts + dk_parts + dv_parts, axis=1)
    G["dbias"] = jnp.concatenate(dbiases, axis=0)
    h1 = sv["h1"]
    G["w_in"] = jnp.concatenate([mm_tn(h1, dza, name=tag + "mm_in_a_dw"), mm_tn(h1, dzb, name=tag + "mm_in_b_dw"),
                                 mm_tn(h1, dqkv, name=tag + "mm_in_c_dw"), mm_tn(h1, dzg, name=tag + "mm_in_g_dw")],
                                axis=1)
    dh1 = mm(dza, WT["in_a"], out_dtype=F32, name=tag + "mm_in_a_dx")
    dh1 = mm(dzb, WT["in_b"], out_dtype=F32, name=tag + "mm_in_b_dx", add=dh1)
    dh1 = mm(dqkv, WT["in_c"], out_dtype=F32, name=tag + "mm_in_c_dx", add=dh1)
    dh1 = mm(dzg, WT["in_g"], out_dtype=ACT, name=tag + "mm_in_g_dx", add=dh1)
    return d, dh1, dg_next


_COL_SHARDED = ("w_in", "b_gate", "conv_a_w", "w_a_out", "w_b_out", "w_c_out", "w_xkv", "w_up", "conv_f_w")
_ROW_SHARDED = ("w_mix_out", "w_xq", "w_xo", "w_down")
_SHARDED_BIG = ("w_in", "w_a_out", "w_b_out", "w_c_out", "w_mix_out", "w_xq", "w_xkv", "w_xo", "w_up", "w_down")
_SHARDED_SMALL = ("b_gate", "conv_a_w", "conv_f_w")
_REPLICATED = ("mix_pre_g", "mix_post_g", "conv_a_b", "ln_a_g", "ln_a_b", "ln_b_g", "ln_b_b", "w_s", "b_s",
               "x_pre_g", "x_post_g", "mem_g", "ffn_pre_g", "ffn_post_g", "conv_f_b")
_WEIGHTS = ('rel_bias', 'mix_pre_g', 'mix_post_g', 'w_in', 'b_gate', 'conv_a_w', 'conv_a_b', 'ln_a_g', 'ln_a_b',
            'w_a_out', 'ln_b_g', 'ln_b_b', 'w_s', 'b_s', 'w_b_out', 'w_c_out', 'w_mix_out', 'x_pre_g', 'x_post_g',
            'mem_g', 'w_xq', 'w_xkv', 'w_xo', 'ffn_pre_g', 'ffn_post_g', 'w_up', 'conv_f_w', 'conv_f_b', 'w_down')
_PACK_COLS = 1024


def _shard_axis(name):
    return 1 if name in _ROW_SHARDED else 2


def _pack(parts, dtype, row_mult):
    flat = jnp.concatenate([p.astype(dtype).reshape(-1) for p in parts])
    n = flat.shape[0]
    unit = _PACK_COLS * row_mult
    padded = -(-n // unit) * unit
    flat = jnp.pad(flat, (0, padded - n))
    return flat.reshape(padded // _PACK_COLS, _PACK_COLS)


def _join8(blocks, ax):
    t = jnp.moveaxis(blocks, 0, ax)
    shp = t.shape
    return t.reshape(shp[:ax] + (shp[ax] * shp[ax + 1],) + shp[ax + 2:])


def _split8(full, ax):
    shp = full.shape
    t = full.reshape(shp[:ax] + (N_DEV, shp[ax] // N_DEV) + shp[ax + 1:])
    return jnp.moveaxis(t, ax, 0)


_W_KEYS = {"w_a_out": "a_out", "w_b_out": "b_out", "w_c_out": "c_out", "w_mix_out": "mix_out", "w_xq": "xq",
           "w_xkv": "xkv", "w_xo": "xo", "w_up": "up", "w_down": "down"}
_IN_SPLITS = (("in_a", 0, 1024), ("in_b", 1024, 2048), ("in_c", 2048, 4352), ("in_g", 4352, 7424))


def _layer_weights(name, full_l):
    if name == "w_in":
        assert full_l.shape[1] == _IN_SPLITS[-1][2]
        W = {k: full_l[:, a:b] for k, a, b in _IN_SPLITS}
    else:
        W = {_W_KEYS[name]: _ffn_interleave(full_l) if name == "w_up" else full_l}
    return W, {k: v.T for k, v in W.items()}


def _sum_rows(x, *, name):
    n = x.shape[0]

    def body(x_ref, o_ref):
        acc = x_ref[0:1, :]
        for i in range(1, n):
            acc = acc + x_ref[i:i + 1, :]
        o_ref[...] = acc

    return pl.pallas_call(body, name=name, out_shape=jax.ShapeDtypeStruct((1, x.shape[1]), x.dtype))(x)


def kernel(x, mem, rel_bias, mix_pre_g, mix_post_g, w_in, b_gate, conv_a_w, conv_a_b, ln_a_g, ln_a_b, w_a_out, ln_b_g, ln_b_b, w_s, b_s, w_b_out, w_c_out, w_mix_out, x_pre_g, x_post_g, mem_g, w_xq, w_xkv, w_xo, ffn_pre_g, ffn_post_g, w_up, conv_f_w, conv_f_b, w_down, loss_target, m_rel_bias, m_mix_pre_g, m_mix_post_g, m_w_in, m_b_gate, m_conv_a_w, m_conv_a_b, m_ln_a_g, m_ln_a_b, m_w_a_out, m_ln_b_g, m_ln_b_b, m_w_s, m_b_s, m_w_b_out, m_w_c_out, m_w_mix_out, m_x_pre_g, m_x_post_g, m_mem_g, m_w_xq, m_w_xkv, m_w_xo, m_ffn_pre_g, m_ffn_post_g, m_w_up, m_conv_f_w, m_conv_f_b, m_w_down, v_rel_bias, v_mix_pre_g, v_mix_post_g, v_w_in, v_b_gate, v_conv_a_w, v_conv_a_b, v_ln_a_g, v_ln_a_b, v_w_a_out, v_ln_b_g, v_ln_b_b, v_w_s, v_b_s, v_w_b_out, v_w_c_out, v_w_mix_out, v_x_pre_g, v_x_post_g, v_mem_g, v_w_xq, v_w_xkv, v_w_xo, v_ffn_pre_g, v_ffn_post_g, v_w_up, v_conv_f_w, v_conv_f_b, v_w_down):
    args = locals()
    w_loc = {n: args[n] for n in _WEIGHTS}
    m_loc = {n: args["m_" + n] for n in _WEIGHTS}
    v_loc = {n: args["v_" + n] for n in _WEIGHTS}

    depth = w_in.shape[0]
    assert depth == 2
    B, S, D = x.shape
    mine = _index_of(_my_pos())
    core = lax.axis_index("c").astype(jnp.int32).reshape(1)
    my_chip = (2 * lax.axis_index("x") + lax.axis_index("y")).astype(jnp.int32).reshape(1)
    shard = {n: w_loc[n].astype(BF16) for n in _SHARDED_BIG}
    shard.update({n: w_loc[n] for n in _SHARDED_SMALL})
    Ws, WTs = [{} for _ in range(depth)], [{} for _ in range(depth)]
    small_full = {}
    hosts = _Hosts()

    def install(items, blocks):
        for (n, l), blk in zip(items, blocks):
            own = shard[n] if l is None else shard[n][l]
            sel = lax.broadcasted_iota(jnp.int32, (N_DEV,) + (1,) * own.ndim, 0) == mine
            blk = jnp.where(sel, own[None], blk)
            if l is None:
                small_full[n] = _join8(blk, _shard_axis(n))
            else:
                W, WT = _layer_weights(n, _join8(blk, _shard_axis(n) - 1))
                Ws[l].update(W)
                WTs[l].update(WT)

    def carry_gather(name, items, group, then=None):
        def done(outs):
            group.extend(zip(items, outs))
            if then is not None:
                then()
        hosts.add(name, lambda: _gather_ici([shard[n] for n, _ in items], [l for _, l in items]), done)

    def finish_gather(group, tag):
        install([it for it, _ in group], _exchange_call(_gather_d2d([b for _, b in group]), name="gather_d2d_" + tag))

    first = [("w_in", 0)] + [(n, None) for n in _SHARDED_SMALL]
    install(first, _all_gather([shard[n] if l is None else shard[n][l] for n, l in first], name="gather_first"))
    g0, g1 = [], []
    carry_gather("l0_mm_in_a", [("w_xq", 0), ("w_a_out", 0), ("w_b_out", 0), ("w_c_out", 0)], g0)
    carry_gather("l0_mm_in_b", [("w_xo", 0), ("w_mix_out", 0)], g0)
    carry_gather("l0_mm_in_c", [("w_xkv", 0), ("w_down", 0)], g0)
    carry_gather("l0_mm_in_g", [("w_up", 0)], g0, then=lambda: finish_gather(g0, "l0"))
    carry_gather("l0_mm_mix", [("w_xq", 1), ("w_mix_out", 1)], g1)
    carry_gather("l0_mm_xq", [("w_xo", 1), ("w_a_out", 1), ("w_b_out", 1), ("w_c_out", 1)], g1)
    carry_gather("l0_mm_xo", [("w_xkv", 1)], g1)
    carry_gather("l0_mm_up", [("w_in", 1), ("w_down", 1)], g1)
    carry_gather("l0_mm_down", [("w_up", 1)], g1, then=lambda: finish_gather(g1, "l1"))

    Gs = [{} for _ in range(depth)]
    sendbuf, chip_sums, parts = {}, {}, {}

    def to_send(item):
        n, l = item
        if l is None:
            g = jnp.stack([Gs[k][n].reshape(small_full[n].shape[1:]) for k in range(depth)], axis=0)
            sendbuf[item] = _split8(g, _shard_axis(n))
        else:
            sendbuf[item] = _split8(Gs[l][n], _shard_axis(n) - 1).astype(BF16)
        return sendbuf[item]

    def swapped(items, got):
        for item, r in zip(items, got):
            chip_sums[item] = _chip_sum(sendbuf[item], r, core, name=f"chip_sum_{item[0]}_{item[1]}")

    def carry_swap(name, items):
        hosts.add(name, lambda: _scatter_d2d([to_send(it) for it in items]), lambda outs: swapped(items, outs))

    def carry_scatter(name, items):
        hosts.add(name, lambda: _scatter_ici([chip_sums[it] for it in items]), lambda outs: parts.update(zip(items, outs)))

    carry_swap("l0_mm_down_dx", [(n, 1) for n in _SHARDED_BIG])
    carry_scatter("l0_mm_down_dw", [(n, 1) for n in ("w_xq", "w_xkv", "w_xo", "w_mix_out", "w_a_out", "w_b_out", "w_c_out")])
    carry_scatter("l0_mm_up_dx", [("w_in", 1)])
    carry_scatter("l0_mm_up_dw", [("w_up", 1), ("w_down", 1)])
    carry_swap("l0_mm_xo_dx", [("w_down", 0), ("w_up", 0)])
    carry_scatter("l0_mm_xq_dx", [("w_down", 0)])
    carry_swap("l0_mm_mix_dx", [("w_xo", 0), ("w_xq", 0), ("w_xkv", 0)])
    carry_swap("l0_mm_in_a_dw", [("w_mix_out", 0), ("w_a_out", 0), ("w_b_out", 0), ("w_c_out", 0)])
    carry_scatter("l0_mm_in_b_dw", [("w_mix_out", 0), ("w_a_out", 0), ("w_b_out", 0), ("w_c_out", 0)])
    carry_scatter("l0_mm_in_c_dw", [("w_xo", 0), ("w_xq", 0), ("w_xkv", 0)])
    carry_scatter("l0_mm_in_g_dw", [("w_up", 0)])
    carry_swap("l0_mm_in_a_dx", [("w_in", 0)])
    halves = {}

    def carry_scatter_half(name, item, half):
        def build():
            cs = chip_sums[item]
            k = cs.shape[1] // 2
            return _scatter_ici([cs[:, half * k:(half + 1) * k]])
        hosts.add(name, build, lambda outs: halves.__setitem__(half, outs[0]))

    carry_scatter_half("l0_mm_in_c_dx", ("w_in", 0), 0)
    carry_scatter_half("l0_mm_in_g_dx", ("w_in", 0), 1)

    rep_w = {n: w_loc[n] for n in ("rel_bias",) + _REPLICATED}
    Ps = lambda: [dict({n: rep_w[n][l] for n in _REPLICATED}, **{n: small_full[n][l] for n in _SHARDED_SMALL})
                  for l in range(depth)]
    loss_vec, grad_x, g_rel = _run_step(x, mem, loss_target, Ws, WTs, Ps(), rep_w["rel_bias"], Gs, hosts)
    assert not hosts.plan, list(hosts.plan)
    parts[("w_in", 0)] = jnp.concatenate([halves[0], halves[1]], axis=1)

    last = [(n, None) for n in _SHARDED_SMALL]
    swapped(last, _exchange_call(_scatter_d2d([to_send(it) for it in last]), name="swap_grads_last"))
    parts.update(zip(last, _exchange_call(_scatter_ici([chip_sums[it] for it in last]), name="scatter_grads_last")))

    rep = ("rel_bias",) + _REPLICATED
    rep_g = {n: jnp.stack([Gs[l][n].reshape(w_loc[n].shape[1:]) for l in range(depth)], axis=0) for n in _REPLICATED}
    rep_g["rel_bias"] = g_rel
    packed = _pack([loss_vec] + [rep_g[n] for n in rep], F32, 8)
    (allp,) = _all_gather([packed], name="gather_rep_grads")
    sel = lax.broadcasted_iota(jnp.int32, (N_DEV, 1, 1), 0) == mine
    allp = jnp.where(sel, packed[None], allp).reshape(N_DEV, -1)
    loss = _sum_rows(allp[:, :LANE], name="loss_sum")[0, 0]
    off = LANE
    rep_parts = {}
    for n in rep:
        size = math.prod(w_loc[n].shape)
        rep_parts[n] = allp[:, off:off + size].reshape((N_DEV,) + w_loc[n].shape)
        off += size

    g_loc, deltas, new_m, new_v = {}, {}, {}, {}
    for n in _WEIGHTS:
        if n in rep_parts:
            p, own = [rep_parts[n]], {}
        else:
            its = [(n, None)] if n in _SHARDED_SMALL else [(n, l) for l in range(depth)]
            p, own = [parts[it] for it in its], {"own": [chip_sums[it] for it in its], "own_slot": my_chip}
        g_loc[n], deltas[n], new_m[n], new_v[n] = _adamw(w_loc[n], p, m_loc[n], v_loc[n], name="adamw_" + n, **own)
    return (loss, grad_x, *[g_loc[n] for n in _WEIGHTS], *[deltas[n] for n in _WEIGHTS],
            *[new_m[n] for n in _WEIGHTS], *[new_v[n] for n in _WEIGHTS])


def _local_step(x, mem, loss_target, full):
    depth = full["w_in"].shape[0]
    Ws, WTs, Ps = [{} for _ in range(depth)], [{} for _ in range(depth)], []
    for l in range(depth):
        Ps.append({n: full[n][l] for n in _WEIGHTS if n != "rel_bias"})
        for n in _SHARDED_BIG:
            W, WT = _layer_weights(n, full[n][l].astype(BF16))
            Ws[l].update(W)
            WTs[l].update(WT)
    Gs = [{} for _ in range(depth)]
    loss_vec, grad_x, g_rel = _run_step(x, mem, loss_target, Ws, WTs, Ps, full["rel_bias"], Gs, None)
    grads = {"rel_bias": g_rel}
    for n in _WEIGHTS:
        if n != "rel_bias":
            grads[n] = jnp.stack([Gs[l][n].reshape(full[n].shape[1:]) for l in range(depth)], axis=0)
    return loss_vec[0, 0], grad_x, grads


def _run_step(x, mem, loss_target, Ws, WTs, Ps, rel_bias, Gs, hosts):
    B, S, D = x.shape
    depth = len(Ws)
    x2d = x.reshape(B * S, D)
    mem2 = mem.reshape(-1, D)
    tabs = _bias_tables(rel_bias)
    (h1,) = _norm_fwd(x2d, None, None, Ps[0]["mix_pre_g"], name="norm0")
    xc = x2d
    saved = []
    for l in range(depth):
        nxt = Ps[l + 1]["mix_pre_g"] if l + 1 < depth else None
        xc, h1, sv = _layer_fwd(l, xc, h1, mem2, Ws[l], Ps[l], tabs, nxt, B=B, S=S, hosts=hosts)
        saved.append(sv)
    loss_vec, d = _loss_kernel(xc, loss_target.reshape(B * S, D), name="loss")

    dh_next = None
    for l in reversed(range(depth)):
        nxt = Ps[l + 1]["mix_pre_g"] if l + 1 < depth else None
        d, dh_next, dg_next = _layer_bwd(l, d, dh_next, nxt, saved[l], mem2, Ws[l], WTs[l], Ps[l], tabs, Gs[l], B=B, S=S,
                                         hosts=hosts)
        if dg_next is not None:
            Gs[l + 1]["mix_pre_g"] = dg_next
    grad_x2d, Gs[0]["mix_pre_g"] = _norm_bwd(d, dh_next, x2d, Ps[0]["mix_pre_g"], None, None, name="norm0_bwd")

    dbias = jnp.stack([Gs[l]["dbias"] for l in range(depth)], axis=0)
    buckets = jnp.stack([t[1] for t in tabs], axis=0)
    rb = _bucket_sum(dbias, buckets, name="rel_bias_grad")
    return loss_vec, grad_x2d.reshape(B, S, D), rb[:, :rel_bias.shape[1]]
```

```python
import functools
import math

import jax
import jax.numpy as jnp
from jax import lax
from jax.experimental import pallas as pl
from jax.experimental.pallas import tpu as pltpu

F32 = jnp.float32
BF16 = jnp.bfloat16

N_DEV = 8
NORM_EPS = 1e-6
LN_EPS = 1e-5
CONV_K = 31
SG_CHUNK = 128
ATT_BLOCK = 128
HEAD_DIM = 64
HEADS_PER_GROUP = 4
GROUP_COLS = HEADS_PER_GROUP * HEAD_DIM
DIL_GROUPS = ((128, 1), (512, 4), (2048, 16))
REL_BUCKETS = 32
REL_MAX_DIST = 2048
X_HEADS = 4
ATT_SCALE = HEAD_DIM ** -0.5
MASK_VALUE = -1e30

ADAM_LR = 0.001
ADAM_B1 = 0.9
ADAM_B2 = 0.999
ADAM_EPS = 1e-08
ADAM_WD = 0.01
ADAM_STEP = 10

LANE = 128
ACT = BF16
ROW_TILE = 512
VMEM_LIMIT = 48 << 20

_NT = (((1,), (1,)), ((), ()))
_TN = (((0,), (0,)), ((), ()))


def _params(sem, vmem=VMEM_LIMIT):
    return pltpu.CompilerParams(dimension_semantics=sem, vmem_limit_bytes=vmem)


def _pick(n, cap):
    if n <= cap:
        return n
    best = None
    for t in range(LANE, cap + 1, LANE):
        if n % t == 0:
            best = t
    assert best is not None, (n, cap)
    return best


def _sigmoid(x):
    return 1.0 / (1.0 + jnp.exp(-x))


_GELU_C = math.sqrt(2.0 / math.pi)


def _gelu(x):
    return 0.5 * x * (1.0 + jnp.tanh(_GELU_C * (x + 0.044715 * x * x * x)))


def _gelu_and_grad(x):
    t = jnp.tanh(_GELU_C * (x + 0.044715 * x * x * x))
    g = 0.5 * x * (1.0 + t)
    dg = 0.5 * (1.0 + t) + 0.5 * x * (1.0 - t * t) * _GELU_C * (1.0 + 3 * 0.044715 * x * x)
    return g, dg


MM_VMEM_BUDGET = 36 << 20


def _divisors128(n):
    return [n] + [t for t in range(n - n % LANE, 0, -LANE) if n % t == 0 and t != n]


class _Hosted:
    def __init__(self, hosts, name):
        self.ex = hosts.get(name) if hosts is not None else None
        self.hosts, self.name = hosts, name

    @property
    def on(self):
        return self.ex is not None

    def specs(self):
        if not self.on:
            return [], [], [], []
        n = self.ex.n_copies
        return (list(self.ex.ins), [_HBM] * len(self.ex.ins), list(self.ex.out_shapes),
                [pltpu.SemaphoreType.DMA((n,)), pltpu.SemaphoreType.DMA((n,))])

    def split(self, refs, n_in, n_out, n_scratch):
        if not self.on:
            return list(refs), None
        ni, no = len(self.ex.ins), len(self.ex.out_shapes)
        refs = list(refs)
        ins, rest = refs[:n_in], refs[n_in:]
        ex_in, rest = rest[:ni], rest[ni:]
        outs, rest = rest[:n_out], rest[n_out:]
        ex_out, rest = rest[:no], rest[no:]
        scratch, sems = rest[:n_scratch], rest[n_scratch:]
        return ins + outs + scratch, (ex_in, ex_out, sems[0], sems[1])

    def run(self, ex_refs, grid):
        if not self.on:
            return
        ids = [pl.program_id(ax) for ax in range(len(grid))]
        first = functools.reduce(jnp.logical_and, [i == 0 for i in ids])
        last = functools.reduce(jnp.logical_and, [i == g - 1 for i, g in zip(ids, grid)])

        @pl.when(first)
        def _():
            sends, _ = self.ex.make(*ex_refs)
            for cp in sends:
                cp.start()

        return last

    def finish(self, ex_refs, last):
        if not self.on:
            return

        @pl.when(last)
        def _():
            sends, recvs = self.ex.make(*ex_refs)
            for cp in recvs:
                cp.wait_recv()
            for cp in sends:
                cp.wait_send()

    def done(self, outs):
        if self.on:
            self.hosts.put(self.name, list(outs))


def _mm(a, b, *, out_dtype, name, add=None, hosts=None, trans_b=False):
    M, K = a.shape
    K2, N = b.shape[::-1] if trans_b else b.shape
    assert K == K2
    has_add = add is not None
    ob = jnp.dtype(out_dtype).itemsize
    tm = _pick(M, 1024)
    tk = _pick(K, 3072)
    nk = K // tk
    for tn in _divisors128(N):
        use = 2 * (tm * tk * a.dtype.itemsize + tk * tn * b.dtype.itemsize + tm * tn * (ob + (4 if has_add else 0)))
        if use + (tm * tn * 4 if nk > 1 else 0) <= MM_VMEM_BUDGET:
            break
    grid = (M // tm, N // tn, nk)
    hosted = _Hosted(hosts, name)
    n_in = 3 if has_add else 2
    n_scr = 1 if nk > 1 else 0

    def body(*refs):
        own, ex_refs = hosted.split(refs, n_in, 1, n_scr)
        last = hosted.run(ex_refs, grid)
        a_ref, b_ref = own[0], own[1]
        c_ref = own[2] if has_add else None
        o_ref = own[n_in]
        if trans_b:
            part = lax.dot_general(a_ref[...].astype(BF16), b_ref[...].astype(BF16), _NT, preferred_element_type=F32)
        else:
            part = jnp.dot(a_ref[...].astype(BF16), b_ref[...].astype(BF16), preferred_element_type=F32)
        if nk == 1:
            if has_add:
                part = part + c_ref[...]
            o_ref[...] = part.astype(o_ref.dtype)
        else:
            acc_ref = own[n_in + 1]
            k = pl.program_id(2)

            @pl.when(k == 0)
            def _():
                acc_ref[...] = part

            @pl.when(k > 0)
            def _():
                acc_ref[...] += part

            @pl.when(k == nk - 1)
            def _():
                r = acc_ref[...]
                if has_add:
                    r = r + c_ref[...]
                o_ref[...] = r.astype(o_ref.dtype)
        hosted.finish(ex_refs, last)

    b_spec = pl.BlockSpec((tn, tk), lambda i, j, k: (j, k)) if trans_b else pl.BlockSpec((tk, tn), lambda i, j, k: (k, j))
    in_specs = [pl.BlockSpec((tm, tk), lambda i, j, k: (i, k)), b_spec]
    args = [a, b]
    if has_add:
        in_specs.append(pl.BlockSpec((tm, tn), lambda i, j, k: (i, j)))
        args.append(add)
    ex_args, ex_in_specs, ex_out_shapes, ex_scratch = hosted.specs()
    outs = pl.pallas_call(
        body, name=name, out_shape=(jax.ShapeDtypeStruct((M, N), out_dtype), *ex_out_shapes),
        grid=grid, in_specs=in_specs + ex_in_specs,
        out_specs=(pl.BlockSpec((tm, tn), lambda i, j, k: (i, j)), *([_HBM] * len(ex_out_shapes))),
        scratch_shapes=([pltpu.VMEM((tm, tn), F32)] if nk > 1 else []) + ex_scratch,
        compiler_params=_params(("arbitrary",) * 3 if hosted.on else ("parallel", "parallel", "arbitrary")),
    )(*args, *ex_args)
    hosted.done(outs[1:])
    return outs[0]


def _mm_tn(a, b, *, name, hosts=None):
    T, M = a.shape
    T2, N = b.shape
    assert T == T2
    tm = _pick(M, 1536)
    tt = _pick(T, 2048)
    nt = T // tt
    for tn in _divisors128(N):
        if 2 * (tt * tm * a.dtype.itemsize + tt * tn * b.dtype.itemsize + tm * tn * 4) <= MM_VMEM_BUDGET:
            break
    grid = (M // tm, N // tn, nt)
    hosted = _Hosted(hosts, name)

    def body(*refs):
        (a_ref, b_ref, o_ref), ex_refs = hosted.split(refs, 2, 1, 0)
        last = hosted.run(ex_refs, grid)
        k = pl.program_id(2)
        part = lax.dot_general(a_ref[...].astype(BF16), b_ref[...].astype(BF16), _TN, preferred_element_type=F32)

        @pl.when(k == 0)
        def _():
            o_ref[...] = part

        @pl.when(k > 0)
        def _():
            o_ref[...] += part

        hosted.finish(ex_refs, last)

    ex_args, ex_in_specs, ex_out_shapes, ex_scratch = hosted.specs()
    outs = pl.pallas_call(
        body, name=name, out_shape=(jax.ShapeDtypeStruct((M, N), F32), *ex_out_shapes),
        grid=grid,
        in_specs=[pl.BlockSpec((tt, tm), lambda i, j, k: (k, i)), pl.BlockSpec((tt, tn), lambda i, j, k: (k, j))]
        + ex_in_specs,
        out_specs=(pl.BlockSpec((tm, tn), lambda i, j, k: (i, j)), *([_HBM] * len(ex_out_shapes))),
        scratch_shapes=ex_scratch,
        compiler_params=_params(("arbitrary",) * 3 if hosted.on else ("parallel", "parallel", "arbitrary")),
    )(a, b, *ex_args)
    hosted.done(outs[1:])
    return outs[0]


def _rms(x):
    r = lax.rsqrt(jnp.mean(x * x, axis=-1, keepdims=True) + NORM_EPS)
    return x * r, r


def _row_spec(cols, tr=ROW_TILE):
    return pl.BlockSpec((tr, cols), lambda i: (i, 0))


def _vec_spec(cols):
    return pl.BlockSpec((1, cols), lambda i: (0, 0))


def _norm_fwd(x, y, g_post, g_pre, *, name):
    T, D = x.shape
    has_post = y is not None
    has_pre = g_pre is not None

    def body(*refs):
        refs = list(refs)
        x_ref = refs.pop(0)
        xn = x_ref[...]
        if has_post:
            y_ref = refs.pop(0)
            gp_ref = refs.pop(0)
        if has_pre:
            gq_ref = refs.pop(0)
        if has_post:
            yh, _ = _rms(y_ref[...].astype(F32))
            xn = xn + yh * gp_ref[...]
            refs.pop(0)[...] = xn
        if has_pre:
            xh, _ = _rms(xn)
            refs.pop(0)[...] = (xh * gq_ref[...]).astype(BF16)

    args, in_specs, out_shape, out_specs = [x], [_row_spec(D)], [], []
    if has_post:
        args += [y, g_post.reshape(1, D)]
        in_specs += [_row_spec(D), _vec_spec(D)]
        out_shape.append(jax.ShapeDtypeStruct((T, D), F32))
        out_specs.append(_row_spec(D))
    if has_pre:
        args.append(g_pre.reshape(1, D))
        in_specs.append(_vec_spec(D))
        out_shape.append(jax.ShapeDtypeStruct((T, D), BF16))
        out_specs.append(_row_spec(D))
    return pl.pallas_call(body, name=name, out_shape=tuple(out_shape), grid=(T // ROW_TILE,), in_specs=in_specs,
                          out_specs=tuple(out_specs), compiler_params=_params(("parallel",)))(*args)


def _norm_bwd(dres, dh, x_new, g_pre, y, g_post, *, name):
    has_res = dres is not None
    has_pre = dh is not None
    has_post = y is not None
    out_d = has_pre and (has_res or has_post)
    T, D = (dres if has_res else dh).shape
    if not has_res:
        assert not has_post

    def body(*refs):
        refs = list(refs)
        i = pl.program_id(0)
        d = refs.pop(0)[...] if has_res else None
        if has_pre:
            dh_v = refs.pop(0)[...].astype(F32)
            xh, r = _rms(refs.pop(0)[...])
            gq = refs.pop(0)[...]
        if has_post:
            yh, ry = _rms(refs.pop(0)[...].astype(F32))
            gp = refs.pop(0)[...]
        if has_pre:
            dxh = dh_v * gq
            dpre = r * (dxh - xh * jnp.mean(dxh * xh, axis=-1, keepdims=True))
            d = dpre if d is None else d + dpre
            dgq = jnp.sum(dh_v * xh, axis=0, keepdims=True)
        if out_d:
            refs.pop(0)[...] = d
        if has_post:
            dyh = d * gp
            refs.pop(0)[...] = (ry * (dyh - yh * jnp.mean(dyh * yh, axis=-1, keepdims=True))).astype(BF16)
            dgp_ref = refs.pop(0)
            dgp = jnp.sum(d * yh, axis=0, keepdims=True)

            @pl.when(i == 0)
            def _():
                dgp_ref[...] = dgp

            @pl.when(i > 0)
            def _():
                dgp_ref[...] += dgp
        if has_pre:
            dgq_ref = refs.pop(0)

            @pl.when(i == 0)
            def _():
                dgq_ref[...] = dgq

            @pl.when(i > 0)
            def _():
                dgq_ref[...] += dgq

    args, in_specs, out_shape, out_specs = [], [], [], []
    if has_res:
        args.append(dres)
        in_specs.append(_row_spec(D))
    if has_pre:
        args += [dh, x_new, g_pre.reshape(1, D)]
        in_specs += [_row_spec(D), _row_spec(D), _vec_spec(D)]
    if has_post:
        args += [y, g_post.reshape(1, D)]
        in_specs += [_row_spec(D), _vec_spec(D)]
    if out_d:
        out_shape.append(jax.ShapeDtypeStruct((T, D), F32))
        out_specs.append(_row_spec(D))
    if has_post:
        out_shape += [jax.ShapeDtypeStruct((T, D), BF16), jax.ShapeDtypeStruct((1, D), F32)]
        out_specs += [_row_spec(D), _vec_spec(D)]
    if has_pre:
        out_shape.append(jax.ShapeDtypeStruct((1, D), F32))
        out_specs.append(_vec_spec(D))
    return pl.pallas_call(body, name=name, out_shape=tuple(out_shape), grid=(T // ROW_TILE,), in_specs=in_specs,
                          out_specs=tuple(out_specs), compiler_params=_params(("arbitrary",)))(*args)


def _loss_kernel(xf, target, *, name):
    T, D = xf.shape

    def body(x_ref, t_ref, loss_ref, d_ref):
        i = pl.program_id(0)
        e = x_ref[...] - t_ref[...]
        d_ref[...] = e * (1.0 / D)
        part = jnp.sum(jnp.sum(e * e, axis=0, keepdims=True), axis=1, keepdims=True) * (0.5 / D)
        part = jnp.broadcast_to(part, (1, LANE))

        @pl.when(i == 0)
        def _():
            loss_ref[...] = part

        @pl.when(i > 0)
        def _():
            loss_ref[...] += part

    return pl.pallas_call(body, name=name,
                          out_shape=(jax.ShapeDtypeStruct((1, LANE), F32), jax.ShapeDtypeStruct((T, D), F32)),
                          grid=(T // ROW_TILE,), in_specs=[_row_spec(D), _row_spec(D)],
                          out_specs=(_vec_spec(LANE), _row_spec(D)), compiler_params=_params(("arbitrary",)))(xf, target)


CONV_PAD = 32
CONV_CH = 64
SUBLANES = 8


def _shift_rows(win, r):
    return win if r == 0 else win[r:r + win.shape[0] - SUBLANES, :]


def _ln_stats(x):
    mu = jnp.mean(x, axis=-1, keepdims=True)
    xc = x - mu
    rstd = lax.rsqrt(jnp.mean(xc * xc, axis=-1, keepdims=True) + LN_EPS)
    return xc * rstd, rstd


def _acc_out(ref, val, first):
    @pl.when(first)
    def _():
        ref[...] = val

    @pl.when(jnp.logical_not(first))
    def _():
        ref[...] += val


def _bra_fwd(za, conv_w, conv_b, ln_g, ln_b, *, B, S, name):
    C = conv_w.shape[1]

    def body(za_ref, w_ref, cb_ref, g_ref, b_ref, a4_ref, a2_ref, pad_ref):
        pad_ref[0:CONV_PAD, :] = jnp.zeros((CONV_PAD, C), F32)
        pad_ref[CONV_PAD:, :] = za_ref[:, 0:C].astype(F32) * _sigmoid(za_ref[:, C:2 * C].astype(F32))

        def chunk(i, _):
            base = pl.multiple_of(i * CONV_CH, CONV_CH)
            win = pad_ref[pl.ds(base, CONV_CH + CONV_PAD), :]
            acc = jnp.broadcast_to(cb_ref[...], (CONV_CH, C))
            for r in range(SUBLANES):
                sh = _shift_rows(win, r)
                for j in range(CONV_K):
                    off = CONV_PAD - (CONV_K - 1) + j
                    if off % SUBLANES == r:
                        acc = acc + w_ref[j:j + 1, :] * sh[off - r:off - r + CONV_CH, :]
            a2_ref[pl.ds(base, CONV_CH), :] = acc
            xh, _ = _ln_stats(acc)
            a3 = xh * g_ref[...] + b_ref[...]
            a4_ref[pl.ds(base, CONV_CH), :] = (a3 * _sigmoid(a3)).astype(BF16)
            return 0

        lax.fori_loop(0, S // CONV_CH, chunk, 0)

    vec = pl.BlockSpec((1, C), lambda b: (0, 0))
    return pl.pallas_call(
        body, name=name,
        out_shape=(jax.ShapeDtypeStruct((B * S, C), BF16), jax.ShapeDtypeStruct((B * S, C), F32)),
        grid=(B,),
        in_specs=[pl.BlockSpec((S, 2 * C), lambda b: (b, 0)), pl.BlockSpec((CONV_K, C), lambda b: (0, 0)), vec, vec, vec],
        out_specs=(pl.BlockSpec((S, C), lambda b: (b, 0)), pl.BlockSpec((S, C), lambda b: (b, 0))),
        scratch_shapes=[pltpu.VMEM((S + CONV_PAD, C), F32)],
        compiler_params=_params(("parallel",)),
    )(za, conv_w, conv_b.reshape(1, C), ln_g.reshape(1, C), ln_b.reshape(1, C))


def _bra_bwd_ln(da4, a2, ln_g, ln_b, *, name):
    T, C = a2.shape

    def body(da4_ref, a2_ref, g_ref, b_ref, da2_ref, dg_ref, db_ref, dcb_ref):
        first = pl.program_id(0) == 0
        xh, rstd = _ln_stats(a2_ref[...])
        a3 = xh * g_ref[...] + b_ref[...]
        sg = _sigmoid(a3)
        da3 = da4_ref[...].astype(F32) * (sg * (1.0 + a3 * (1.0 - sg)))
        dxh = da3 * g_ref[...]
        da2 = rstd * (dxh - jnp.mean(dxh, axis=-1, keepdims=True) - xh * jnp.mean(dxh * xh, axis=-1, keepdims=True))
        da2_ref[...] = da2
        _acc_out(dg_ref, jnp.sum(da3 * xh, axis=0, keepdims=True), first)
        _acc_out(db_ref, jnp.sum(da3, axis=0, keepdims=True), first)
        _acc_out(dcb_ref, jnp.sum(da2, axis=0, keepdims=True), first)

    vec = _vec_spec(C)
    vshape = jax.ShapeDtypeStruct((1, C), F32)
    return pl.pallas_call(body, name=name, out_shape=(jax.ShapeDtypeStruct((T, C), F32), vshape, vshape, vshape),
                          grid=(T // ROW_TILE,), in_specs=[_row_spec(C), _row_spec(C), vec, vec],
                          out_specs=(_row_spec(C), vec, vec, vec),
                          compiler_params=_params(("arbitrary",)))(da4, a2, ln_g.reshape(1, C), ln_b.reshape(1, C))


def _bra_bwd_conv(da2, za, conv_w, *, B, S, name):
    C = conv_w.shape[1]
    SUB = 8

    def body(da2_ref, za_ref, w_ref, dza_ref, dw_ref, pad1_ref, pad2_ref, dwacc_ref):
        first = pl.program_id(0) == 0
        pad1_ref[0:CONV_PAD, :] = jnp.zeros((CONV_PAD, C), F32)
        pad1_ref[CONV_PAD:, :] = za_ref[:, 0:C].astype(F32) * _sigmoid(za_ref[:, C:2 * C].astype(F32))
        pad2_ref[0:S, :] = da2_ref[...]
        pad2_ref[S:, :] = jnp.zeros((CONV_PAD, C), F32)
        dwacc_ref[...] = jnp.zeros_like(dwacc_ref)

        def chunk(i, _):
            base = pl.multiple_of(i * CONV_CH, CONV_CH)
            rows = pl.ds(base, CONV_CH)
            win1 = pad1_ref[pl.ds(base, CONV_CH + CONV_PAD), :]
            win2 = pad2_ref[pl.ds(base, CONV_CH + CONV_PAD), :]
            d = win2[0:CONV_CH, :]
            da1 = jnp.zeros((CONV_CH, C), F32)
            for r in range(SUBLANES):
                sh1 = _shift_rows(win1, r)
                sh2 = _shift_rows(win2, r)
                for j in range(CONV_K):
                    off2 = CONV_K - 1 - j
                    if off2 % SUBLANES == r:
                        da1 = da1 + w_ref[j:j + 1, :] * sh2[off2 - r:off2 - r + CONV_CH, :]
                    off1 = CONV_PAD - (CONV_K - 1) + j
                    if off1 % SUBLANES == r:
                        prod = d * sh1[off1 - r:off1 - r + CONV_CH, :]
                        dwacc_ref[j * SUB:(j + 1) * SUB, :] += jnp.sum(prod.reshape(CONV_CH // SUB, SUB, C), axis=0)
            a_val = za_ref[rows, 0:C].astype(F32)
            sg = _sigmoid(za_ref[rows, C:2 * C].astype(F32))
            dza_ref[rows, 0:C] = (da1 * sg).astype(BF16)
            dza_ref[rows, C:2 * C] = (da1 * a_val * sg * (1.0 - sg)).astype(BF16)
            return 0

        lax.fori_loop(0, S // CONV_CH, chunk, 0)
        _acc_out(dw_ref, jnp.sum(dwacc_ref[...].reshape(CONV_K, SUB, C), axis=1), first)

    return pl.pallas_call(
        body, name=name,
        out_shape=(jax.ShapeDtypeStruct((B * S, 2 * C), BF16), jax.ShapeDtypeStruct((CONV_K, C), F32)),
        grid=(B,),
        in_specs=[pl.BlockSpec((S, C), lambda b: (b, 0)), pl.BlockSpec((S, 2 * C), lambda b: (b, 0)),
                  pl.BlockSpec((CONV_K, C), lambda b: (0, 0))],
        out_specs=(pl.BlockSpec((S, 2 * C), lambda b: (b, 0)), pl.BlockSpec((CONV_K, C), lambda b: (0, 0))),
        scratch_shapes=[pltpu.VMEM((S + CONV_PAD, C), F32), pltpu.VMEM((S + CONV_PAD, C), F32),
                        pltpu.VMEM((CONV_K * SUB, C), F32)],
        compiler_params=_params(("arbitrary",)),
    )(da2, za, conv_w)


SG_STEP = 4


def _tril_mask():
    r = lax.broadcasted_iota(jnp.int32, (SG_CHUNK, SG_CHUNK), 0)
    c = lax.broadcasted_iota(jnp.int32, (SG_CHUNK, SG_CHUNK), 1)
    return c <= r


def _brb_fwd(zb, ln_g, ln_b, w_s, b_s_t, *, name):
    T, C2 = zb.shape
    C = C2 // 2
    G = w_s.shape[0]
    GC = C // G

    def body(zb_ref, g_ref, b_ref, ws_ref, bs_ref, p_ref):
        z = _gelu(zb_ref[...].astype(F32))
        u = z[:, 0:C]
        xh, _ = _ln_stats(z[:, C:2 * C])
        v1 = (xh * g_ref[...] + b_ref[...]).astype(BF16)
        mask = _tril_mask()
        wss = [jnp.where(mask, ws_ref[gi], 0.0).astype(BF16) for gi in range(G)]
        for ci in range(SG_STEP):
            rows = slice(ci * SG_CHUNK, (ci + 1) * SG_CHUNK)
            outs = [jnp.dot(wss[gi], v1[rows, gi * GC:(gi + 1) * GC], preferred_element_type=F32) + bs_ref[:, gi:gi + 1]
                    for gi in range(G)]
            p_ref[rows, :] = (u[rows, :] * jnp.concatenate(outs, axis=1)).astype(BF16)

    tr = SG_STEP * SG_CHUNK
    return pl.pallas_call(
        body, name=name, out_shape=jax.ShapeDtypeStruct((T, C), BF16), grid=(T // tr,),
        in_specs=[_row_spec(C2, tr), _vec_spec(C), _vec_spec(C),
                  pl.BlockSpec((G, SG_CHUNK, SG_CHUNK), lambda i: (0, 0, 0)), pl.BlockSpec((SG_CHUNK, G), lambda i: (0, 0))],
        out_specs=_row_spec(C, tr), compiler_params=_params(("parallel",)),
    )(zb, ln_g.reshape(1, C), ln_b.reshape(1, C), w_s, b_s_t)


def _brb_bwd(dp, zb, ln_g, ln_b, w_s, b_s_t, *, name):
    T, C2 = zb.shape
    C = C2 // 2
    G = w_s.shape[0]
    GC = C // G

    def body(dp_ref, zb_ref, g_ref, b_ref, ws_ref, bs_ref, dzb_ref, dws_ref, dbs_ref, dg_ref, db_ref):
        first = pl.program_id(0) == 0
        z, dz = _gelu_and_grad(zb_ref[...].astype(F32))
        u = z[:, 0:C]
        xh, rstd = _ln_stats(z[:, C:2 * C])
        v1 = (xh * g_ref[...] + b_ref[...]).astype(BF16)
        dp_v = dp_ref[...].astype(F32)
        mask = _tril_mask()
        wss = [jnp.where(mask, ws_ref[gi], 0.0).astype(BF16) for gi in range(G)]
        dv2_all = dp_v * u
        dv2b_all = dv2_all.astype(BF16)
        v2_rows, dv1_rows = [], []
        dwss, dbss = [None] * G, [None] * G
        for ci in range(SG_STEP):
            rows = slice(ci * SG_CHUNK, (ci + 1) * SG_CHUNK)
            v2s, dv1s = [], []
            for gi in range(G):
                cols = slice(gi * GC, (gi + 1) * GC)
                v2s.append(jnp.dot(wss[gi], v1[rows, cols], preferred_element_type=F32) + bs_ref[:, gi:gi + 1])
                dv2b = dv2b_all[rows, cols]
                dbs = jnp.sum(dv2_all[rows, cols], axis=1, keepdims=True)
                dws = lax.dot_general(dv2b, v1[rows, cols], _NT, preferred_element_type=F32)
                dbss[gi] = dbs if dbss[gi] is None else dbss[gi] + dbs
                dwss[gi] = dws if dwss[gi] is None else dwss[gi] + dws
                dv1s.append(lax.dot_general(wss[gi], dv2b, _TN, preferred_element_type=F32))
            v2_rows.append(jnp.concatenate(v2s, axis=1))
            dv1_rows.append(jnp.concatenate(dv1s, axis=1))
        dwss = [jnp.where(mask, t, 0.0) for t in dwss]
        du = dp_v * jnp.concatenate(v2_rows, axis=0)
        dv1 = jnp.concatenate(dv1_rows, axis=0)
        dxh = dv1 * g_ref[...]
        dv0 = rstd * (dxh - jnp.mean(dxh, axis=-1, keepdims=True) - xh * jnp.mean(dxh * xh, axis=-1, keepdims=True))
        dzb_ref[:, 0:C] = (du * dz[:, 0:C]).astype(BF16)
        dzb_ref[:, C:2 * C] = (dv0 * dz[:, C:2 * C]).astype(BF16)
        _acc_out(dws_ref, jnp.stack(dwss, axis=0), first)
        _acc_out(dbs_ref, jnp.concatenate(dbss, axis=1), first)
        _acc_out(dg_ref, jnp.sum(dv1 * xh, axis=0, keepdims=True), first)
        _acc_out(db_ref, jnp.sum(dv1, axis=0, keepdims=True), first)

    wspec = pl.BlockSpec((G, SG_CHUNK, SG_CHUNK), lambda i: (0, 0, 0))
    bspec = pl.BlockSpec((SG_CHUNK, G), lambda i: (0, 0))
    return pl.pallas_call(
        body, name=name,
        out_shape=(jax.ShapeDtypeStruct((T, C2), BF16), jax.ShapeDtypeStruct((G, SG_CHUNK, SG_CHUNK), F32),
                   jax.ShapeDtypeStruct((SG_CHUNK, G), F32), jax.ShapeDtypeStruct((1, C), F32),
                   jax.ShapeDtypeStruct((1, C), F32)),
        grid=(T // (SG_STEP * SG_CHUNK),),
        in_specs=[_row_spec(C, SG_STEP * SG_CHUNK), _row_spec(C2, SG_STEP * SG_CHUNK), _vec_spec(C), _vec_spec(C), wspec,
                  bspec],
        out_specs=(_row_spec(C2, SG_STEP * SG_CHUNK), wspec, bspec, _vec_spec(C), _vec_spec(C)),
        compiler_params=_params(("arbitrary",)),
    )(dp, zb, ln_g.reshape(1, C), ln_b.reshape(1, C), w_s, b_s_t)


ATT_UNROLL = 3


def _att_unroll(trips):
    return 5 if trips % 5 == 0 else ATT_UNROLL


def _each_class(d, fn):
    if d == 1:
        fn(0)
    else:
        def step(c, _):
            fn(c)
            return 0

        lax.fori_loop(0, d, step, 0, unroll=ATT_UNROLL + 1 if d % (ATT_UNROLL + 1) == 0 and d > ATT_UNROLL + 1 else 1)


def _pair_bias(b):
    off = jnp.full((ATT_BLOCK, ATT_BLOCK), MASK_VALUE, F32)
    return jnp.concatenate([jnp.concatenate([b, off], axis=1), jnp.concatenate([off, b], axis=1)], axis=0)


def _attn_fwd(qkv_g, bias, *, name):
    B, d, L, _ = qkv_g.shape
    nb = L // ATT_BLOCK
    GCOL = GROUP_COLS

    def body(qkv_all, bias_ref, o_all, lse_all):
        if nb == 1 and d % 2 == 0:
            two_classes(qkv_all, bias_ref, o_all, lse_all)
        else:
            _each_class(d, lambda c: one_class(qkv_all.at[c], bias_ref, o_all.at[c], lse_all.at[c]))

    def two_classes(qkv_all, bias_ref, o_all, lse_all):
        bias2 = [_pair_bias(bias_ref[h, :, ATT_BLOCK:2 * ATT_BLOCK]) for h in range(HEADS_PER_GROUP)]

        def pair(p, _):
            two = pl.ds(2 * p, 2)
            x = qkv_all[two].reshape(2 * ATT_BLOCK, 3 * GCOL)
            qb, kb, vb = x[:, 0:GCOL], x[:, GCOL:2 * GCOL], x[:, 2 * GCOL:3 * GCOL]
            outs, lses = [], []
            for h in range(HEADS_PER_GROUP):
                sl = slice(h * HEAD_DIM, (h + 1) * HEAD_DIM)
                s = lax.dot_general(qb[:, sl], kb[:, sl], _NT, preferred_element_type=F32) * ATT_SCALE + bias2[h]
                m = jnp.max(s, axis=1, keepdims=True)
                e = jnp.exp(s - m)
                ssum = jnp.sum(e, axis=1, keepdims=True)
                outs.append(jnp.dot(e.astype(BF16), vb[:, sl], preferred_element_type=F32) / ssum)
                lses.append(jnp.broadcast_to(m + jnp.log(ssum), (2 * ATT_BLOCK, HEAD_DIM)))
            o_all[two] = jnp.concatenate(outs, axis=1).reshape(2, ATT_BLOCK, GCOL)
            lse_all[two] = jnp.concatenate(lses, axis=1).reshape(2, ATT_BLOCK, GCOL)
            return 0

        lax.fori_loop(0, d // 2, pair, 0, unroll=2)

    def one_class(qkv_ref, bias_ref, o_ref, lse_ref):
        def blk(r0, first):
            nk = ATT_BLOCK if first else 2 * ATT_BLOCK
            k0 = r0 if first else r0 - ATT_BLOCK
            qb = qkv_ref[pl.ds(r0, ATT_BLOCK), 0:GCOL]
            kb = qkv_ref[pl.ds(k0, nk), GCOL:2 * GCOL]
            vb = qkv_ref[pl.ds(k0, nk), 2 * GCOL:3 * GCOL]
            outs, lses = [], []
            for h in range(HEADS_PER_GROUP):
                sl = slice(h * HEAD_DIM, (h + 1) * HEAD_DIM)
                bh = bias_ref[h, :, ATT_BLOCK:2 * ATT_BLOCK] if first else bias_ref[h]
                s = lax.dot_general(qb[:, sl], kb[:, sl], _NT, preferred_element_type=F32) * ATT_SCALE + bh
                m = jnp.max(s, axis=1, keepdims=True)
                e = jnp.exp(s - m)
                ssum = jnp.sum(e, axis=1, keepdims=True)
                outs.append(jnp.dot(e.astype(BF16), vb[:, sl], preferred_element_type=F32) / ssum)
                lses.append(jnp.broadcast_to(m + jnp.log(ssum), (ATT_BLOCK, HEAD_DIM)))
            o_ref[pl.ds(r0, ATT_BLOCK), :] = jnp.concatenate(outs, axis=1)
            lse_ref[pl.ds(r0, ATT_BLOCK), :] = jnp.concatenate(lses, axis=1)

        blk(0, True)
        if nb > 1:
            def loop(n, _):
                blk(pl.multiple_of(n * ATT_BLOCK, ATT_BLOCK), False)
                return 0

            lax.fori_loop(1, nb, loop, 0, unroll=ATT_UNROLL)

    spec = lambda cols: pl.BlockSpec((None, d, L, cols), lambda b: (b, 0, 0, 0))
    oshape = jax.ShapeDtypeStruct((B, d, L, GCOL), F32)
    return pl.pallas_call(
        body, name=name, out_shape=(oshape, oshape), grid=(B,),
        in_specs=[spec(3 * GCOL), pl.BlockSpec((HEADS_PER_GROUP, ATT_BLOCK, 2 * ATT_BLOCK), lambda b: (0, 0, 0))],
        out_specs=(spec(GCOL), spec(GCOL)), compiler_params=_params(("parallel",)),
    )(qkv_g, bias)


def _attn_bwd(qkv_g, doc_g, lse_g, dd_g, bias, *, name):
    B, d, L, _ = qkv_g.shape
    nb = L // ATT_BLOCK
    GCOL = GROUP_COLS

    def body(qkv_all, doc_all, lse_all, dd_all, bias_ref, dqkv_all, dbias_ref, dk_ref, dv_ref):
        @pl.when(pl.program_id(0) == 0)
        def _():
            dbias_ref[...] = jnp.zeros_like(dbias_ref)

        if nb == 1 and d % 2 == 0:
            two_classes(qkv_all, doc_all, lse_all, dd_all, bias_ref, dqkv_all, dbias_ref)
        else:
            _each_class(d, lambda c: one_class(qkv_all.at[c], doc_all.at[c], lse_all.at[c], dd_all.at[c], bias_ref,
                                               dqkv_all.at[c], dbias_ref, dk_ref, dv_ref))

    def two_classes(qkv_all, doc_all, lse_all, dd_all, bias_ref, dqkv_all, dbias_ref):
        bias2 = [_pair_bias(bias_ref[h, :, ATT_BLOCK:2 * ATT_BLOCK]) for h in range(HEADS_PER_GROUP)]
        A = ATT_BLOCK

        def pair(p, _):
            two = pl.ds(2 * p, 2)
            x = qkv_all[two].reshape(2 * A, 3 * GCOL)
            qb, kb, vb = x[:, 0:GCOL], x[:, GCOL:2 * GCOL], x[:, 2 * GCOL:3 * GCOL]
            dob = doc_all[two].reshape(2 * A, GCOL).astype(BF16)
            lse = lse_all[two].reshape(2 * A, GCOL)
            dd = dd_all[two].reshape(2 * A, GCOL)
            dqs, dks, dvs = [], [], []
            for h in range(HEADS_PER_GROUP):
                sl = slice(h * HEAD_DIM, (h + 1) * HEAD_DIM)
                c0 = h * HEAD_DIM
                s = lax.dot_general(qb[:, sl], kb[:, sl], _NT, preferred_element_type=F32) * ATT_SCALE + bias2[h]
                pr = jnp.exp(s - lse[:, c0:c0 + 1])
                dp = lax.dot_general(dob[:, sl], vb[:, sl], _NT, preferred_element_type=F32)
                ds = pr * (dp - dd[:, c0:c0 + 1])
                dbias_ref[h, :, A:2 * A] += ds[0:A, 0:A] + ds[A:2 * A, A:2 * A]
                dsb = ds.astype(BF16)
                dqs.append(jnp.dot(dsb, kb[:, sl], preferred_element_type=F32) * ATT_SCALE)
                dks.append(lax.dot_general(dsb, qb[:, sl], _TN, preferred_element_type=F32) * ATT_SCALE)
                dvs.append(lax.dot_general(pr.astype(BF16), dob[:, sl], _TN, preferred_element_type=F32))
            dqkv_all[two] = jnp.concatenate(dqs + dks + dvs, axis=1).astype(BF16).reshape(2, A, 3 * GCOL)
            return 0

        lax.fori_loop(0, d // 2, pair, 0, unroll=2)

    def one_class(qkv_ref, doc_ref, lse_ref, dd_ref, bias_ref, dqkv_ref, dbias_ref, dk_ref, dv_ref):
        dk_ref[...] = jnp.zeros_like(dk_ref)
        dv_ref[...] = jnp.zeros_like(dv_ref)

        def blk(r0, first):
            nk = ATT_BLOCK if first else 2 * ATT_BLOCK
            k0 = r0 if first else r0 - ATT_BLOCK
            rows = pl.ds(r0, ATT_BLOCK)
            krows = pl.ds(k0, nk)
            qb = qkv_ref[rows, 0:GCOL]
            kb = qkv_ref[krows, GCOL:2 * GCOL]
            vb = qkv_ref[krows, 2 * GCOL:3 * GCOL]
            dob = doc_ref[rows, :].astype(BF16)
            lse = lse_ref[rows, :]
            dd = dd_ref[rows, :]
            dqs, dks, dvs = [], [], []
            for h in range(HEADS_PER_GROUP):
                sl = slice(h * HEAD_DIM, (h + 1) * HEAD_DIM)
                c0 = h * HEAD_DIM
                bh = bias_ref[h, :, ATT_BLOCK:2 * ATT_BLOCK] if first else bias_ref[h]
                s = lax.dot_general(qb[:, sl], kb[:, sl], _NT, preferred_element_type=F32) * ATT_SCALE + bh
                p = jnp.exp(s - lse[:, c0:c0 + 1])
                dp = lax.dot_general(dob[:, sl], vb[:, sl], _NT, preferred_element_type=F32)
                ds = p * (dp - dd[:, c0:c0 + 1])
                if first:
                    dbias_ref[h, :, ATT_BLOCK:2 * ATT_BLOCK] += ds
                else:
                    dbias_ref[h] += ds
                dsb = ds.astype(BF16)
                dqs.append(jnp.dot(dsb, kb[:, sl], preferred_element_type=F32) * ATT_SCALE)
                dks.append(lax.dot_general(dsb, qb[:, sl], _TN, preferred_element_type=F32) * ATT_SCALE)
                dvs.append(lax.dot_general(p.astype(BF16), dob[:, sl], _TN, preferred_element_type=F32))
            dqkv_ref[rows, 0:GCOL] = jnp.concatenate(dqs, axis=1).astype(BF16)
            dk_ref[krows, :] += jnp.concatenate(dks, axis=1)
            dv_ref[krows, :] += jnp.concatenate(dvs, axis=1)

        blk(0, True)
        if nb > 1:
            def loop(n, _):
                blk(pl.multiple_of(n * ATT_BLOCK, ATT_BLOCK), False)
                return 0

            lax.fori_loop(1, nb, loop, 0, unroll=_att_unroll(nb - 1))
        dqkv_ref[:, GCOL:2 * GCOL] = dk_ref[...].astype(BF16)
        dqkv_ref[:, 2 * GCOL:3 * GCOL] = dv_ref[...].astype(BF16)

    spec = lambda cols: pl.BlockSpec((None, d, L, cols), lambda b: (b, 0, 0, 0))
    bspec = pl.BlockSpec((HEADS_PER_GROUP, ATT_BLOCK, 2 * ATT_BLOCK), lambda b: (0, 0, 0))
    return pl.pallas_call(
        body, name=name,
        out_shape=(jax.ShapeDtypeStruct((B, d, L, 3 * GCOL), BF16),
                   jax.ShapeDtypeStruct((HEADS_PER_GROUP, ATT_BLOCK, 2 * ATT_BLOCK), F32)),
        grid=(B,),
        in_specs=[spec(3 * GCOL), spec(GCOL), spec(GCOL), spec(GCOL), bspec],
        out_specs=(spec(3 * GCOL), bspec),
        scratch_shapes=[pltpu.VMEM((L, GCOL), F32), pltpu.VMEM((L, GCOL), F32)],
        compiler_params=_params(("arbitrary",)),
    )(qkv_g, doc_g, lse_g, dd_g, bias)


def _attn_combine(os_, lses, *, name):
    T, GC = os_[0].shape
    n = len(os_)

    def body(*refs):
        o_refs, l_refs, oc_ref, lt_ref = refs[:n], refs[n:2 * n], refs[2 * n], refs[2 * n + 1]
        ls = [r[...] for r in l_refs]
        m = functools.reduce(jnp.maximum, ls)
        ws = [jnp.exp(l - m) for l in ls]
        tot = functools.reduce(jnp.add, ws)
        acc = functools.reduce(jnp.add, [w * r[...] for w, r in zip(ws, o_refs)])
        oc_ref[...] = acc / tot
        lt_ref[...] = m + jnp.log(tot)

    shp = jax.ShapeDtypeStruct((T, GC), F32)
    return pl.pallas_call(body, name=name, out_shape=(shp, shp), grid=(T // ROW_TILE,),
                          in_specs=[_row_spec(GC)] * (2 * n), out_specs=(_row_spec(GC), _row_spec(GC)),
                          compiler_params=_params(("parallel",)))(*os_, *lses)


def _attn_rowdot(doc, oc, *, name):
    T, GC = oc.shape

    def body(doc_ref, oc_ref, dd_ref):
        prod = doc_ref[...].astype(F32) * oc_ref[...]
        parts = []
        for h in range(GC // HEAD_DIM):
            s = jnp.sum(prod[:, h * HEAD_DIM:(h + 1) * HEAD_DIM], axis=1, keepdims=True)
            parts.append(jnp.broadcast_to(s, (ROW_TILE, HEAD_DIM)))
        dd_ref[...] = jnp.concatenate(parts, axis=1)

    return pl.pallas_call(body, name=name, out_shape=jax.ShapeDtypeStruct((T, GC), F32), grid=(T // ROW_TILE,),
                          in_specs=[_row_spec(GC), _row_spec(GC)], out_specs=_row_spec(GC),
                          compiler_params=_params(("parallel",)))(doc, oc)


def _bucket_sum(dbias, buckets, *, name):
    depth, NH = dbias.shape[:2]

    def body(db_ref, bk_ref, out_ref):
        rows = lax.broadcasted_iota(jnp.int32, (REL_BUCKETS, LANE), 0)
        cols = lax.broadcasted_iota(jnp.int32, (REL_BUCKETS, LANE), 1)

        def per_bucket(b, acc):
            for h in range(NH):
                sel = bk_ref[h // HEADS_PER_GROUP] == b
                tot = functools.reduce(jnp.add, [db_ref[l, h] for l in range(depth)])
                s = jnp.sum(jnp.where(sel, tot, 0.0))
                acc = acc + jnp.where(jnp.logical_and(rows == b, cols == h), s, 0.0)
            return acc

        out_ref[...] = lax.fori_loop(0, REL_BUCKETS, per_bucket, jnp.zeros((REL_BUCKETS, LANE), F32))

    return pl.pallas_call(body, name=name, out_shape=jax.ShapeDtypeStruct((REL_BUCKETS, LANE), F32),
                          compiler_params=pltpu.CompilerParams(vmem_limit_bytes=VMEM_LIMIT))(dbias, buckets)


def _merge_fwd(zg, bg, ya, yb, yc, *, name):
    T, D3 = zg.shape
    D = D3 // 3

    def body(zg_ref, bg_ref, ya_ref, yb_ref, yc_ref, out_ref):
        acc = None
        for i, y_ref in enumerate((ya_ref, yb_ref, yc_ref)):
            g = _sigmoid(zg_ref[:, i * D:(i + 1) * D].astype(F32) + bg_ref[:, i * D:(i + 1) * D])
            t = g * y_ref[...].astype(F32)
            acc = t if acc is None else acc + t
        out_ref[...] = acc.astype(BF16)

    return pl.pallas_call(body, name=name, out_shape=jax.ShapeDtypeStruct((T, D), BF16), grid=(T // ROW_TILE,),
                          in_specs=[_row_spec(D3), _vec_spec(D3), _row_spec(D), _row_spec(D), _row_spec(D)],
                          out_specs=_row_spec(D), compiler_params=_params(("parallel",)))(zg, bg.reshape(1, D3), ya, yb, yc)


def _merge_bwd(dm, zg, bg, ya, yb, yc, *, name):
    T, D3 = zg.shape
    D = D3 // 3

    def body(dm_ref, zg_ref, bg_ref, ya_ref, yb_ref, yc_ref, dya_ref, dyb_ref, dyc_ref, dzg_ref, dbg_ref):
        first = pl.program_id(0) == 0
        dm_v = dm_ref[...].astype(F32)
        dbs = []
        for i, (y_ref, dy_ref) in enumerate(((ya_ref, dya_ref), (yb_ref, dyb_ref), (yc_ref, dyc_ref))):
            g = _sigmoid(zg_ref[:, i * D:(i + 1) * D].astype(F32) + bg_ref[:, i * D:(i + 1) * D])
            dy_ref[...] = (dm_v * g).astype(BF16)
            dz = dm_v * y_ref[...].astype(F32) * g * (1.0 - g)
            dzg_ref[:, i * D:(i + 1) * D] = dz.astype(BF16)
            dbs.append(jnp.sum(dz, axis=0, keepdims=True))
        _acc_out(dbg_ref, jnp.concatenate(dbs, axis=1), first)

    bshape = jax.ShapeDtypeStruct((T, D), BF16)
    return pl.pallas_call(
        body, name=name,
        out_shape=(bshape, bshape, bshape, jax.ShapeDtypeStruct((T, D3), BF16), jax.ShapeDtypeStruct((1, D3), F32)),
        grid=(T // ROW_TILE,),
        in_specs=[_row_spec(D), _row_spec(D3), _vec_spec(D3), _row_spec(D), _row_spec(D), _row_spec(D)],
        out_specs=(_row_spec(D), _row_spec(D), _row_spec(D), _row_spec(D3), _vec_spec(D3)),
        compiler_params=_params(("arbitrary",)))(dm, zg, bg.reshape(1, D3), ya, yb, yc)


XQ_TILE = 1024


def _xattn_fwd(q, kv, *, B, S, name):
    D = q.shape[1]
    M = kv.shape[0] // B
    E = D // X_HEADS
    scale = E ** -0.5

    def body(q_ref, kv_ref, o_ref):
        outs = []
        for h in range(X_HEADS):
            s = lax.dot_general(q_ref[:, h * E:(h + 1) * E], kv_ref[:, h * E:(h + 1) * E], _NT,
                                preferred_element_type=F32) * scale
            e = jnp.exp(s - jnp.max(s, axis=1, keepdims=True))
            p = e / jnp.sum(e, axis=1, keepdims=True)
            outs.append(jnp.dot(p.astype(BF16), kv_ref[:, D + h * E:D + (h + 1) * E], preferred_element_type=F32))
        o_ref[...] = jnp.concatenate(outs, axis=1).astype(BF16)

    nq = S // XQ_TILE
    return pl.pallas_call(
        body, name=name, out_shape=jax.ShapeDtypeStruct((B * S, D), BF16), grid=(B, nq),
        in_specs=[pl.BlockSpec((XQ_TILE, D), lambda b, i: (b * nq + i, 0)), pl.BlockSpec((M, 2 * D), lambda b, i: (b, 0))],
        out_specs=pl.BlockSpec((XQ_TILE, D), lambda b, i: (b * nq + i, 0)),
        compiler_params=_params(("parallel", "parallel")))(q, kv)


def _xattn_bwd(q, kv, do, *, B, S, name):
    D = q.shape[1]
    M = kv.shape[0] // B
    E = D // X_HEADS
    scale = E ** -0.5

    def body(q_ref, kv_ref, do_ref, dq_ref, dkv_ref):
        first = pl.program_id(1) == 0
        dqs, dks, dvs = [], [], []
        for h in range(X_HEADS):
            qh = q_ref[:, h * E:(h + 1) * E]
            kh = kv_ref[:, h * E:(h + 1) * E]
            vh = kv_ref[:, D + h * E:D + (h + 1) * E]
            doh = do_ref[:, h * E:(h + 1) * E]
            s = lax.dot_general(qh, kh, _NT, preferred_element_type=F32) * scale
            e = jnp.exp(s - jnp.max(s, axis=1, keepdims=True))
            p = e / jnp.sum(e, axis=1, keepdims=True)
            dp = lax.dot_general(doh, vh, _NT, preferred_element_type=F32)
            ds = (p * (dp - jnp.sum(p * dp, axis=1, keepdims=True))).astype(BF16)
            dqs.append(jnp.dot(ds, kh, preferred_element_type=F32) * scale)
            dks.append(lax.dot_general(ds, qh, _TN, preferred_element_type=F32) * scale)
            dvs.append(lax.dot_general(p.astype(BF16), doh, _TN, preferred_element_type=F32))
        dq_ref[...] = jnp.concatenate(dqs, axis=1).astype(BF16)
        _acc_out(dkv_ref, jnp.concatenate(dks + dvs, axis=1), first)

    nq = S // XQ_TILE
    qspec = pl.BlockSpec((XQ_TILE, D), lambda b, i: (b * nq + i, 0))
    kvspec = pl.BlockSpec((M, 2 * D), lambda b, i: (b, 0))
    return pl.pallas_call(
        body, name=name,
        out_shape=(jax.ShapeDtypeStruct((B * S, D), BF16), jax.ShapeDtypeStruct((B * M, 2 * D), F32)),
        grid=(B, nq), in_specs=[qspec, kvspec, qspec], out_specs=(qspec, kvspec),
        compiler_params=_params(("arbitrary", "arbitrary")))(q, kv, do)


FFN_COLS = 256
FFN_PAD = 8
FFN_CH = 256
FFN_K = 3


def _ffn_gate(win, w_ref, cb_ref, n):
    g = jnp.broadcast_to(cb_ref[...], (n, win.shape[1]))
    for j in range(FFN_K):
        off = FFN_PAD - (FFN_K - 1) + j
        g = g + w_ref[j:j + 1, :] * win[off:off + n, :]
    return g


def _ffn_interleave(w):
    lead, n2 = w.shape[:-1], w.shape[-1]
    nc = n2 // (2 * FFN_COLS)
    return jnp.swapaxes(w.reshape(lead + (2, nc, FFN_COLS)), -3, -2).reshape(lead + (n2,))


def _ffn_deinterleave(w):
    lead, n2 = w.shape[:-1], w.shape[-1]
    nc = n2 // (2 * FFN_COLS)
    return jnp.swapaxes(w.reshape(lead + (nc, 2, FFN_COLS)), -3, -2).reshape(lead + (n2,))


def _ffn_fwd(up, conv_w, conv_b, *, B, S, name):
    F = conv_w.shape[1]
    nc = F // FFN_COLS

    def body(gp_ref, val_ref, w_ref, cb_ref, act_ref, pad_ref):
        pad_ref[0:FFN_PAD, :] = jnp.zeros((FFN_PAD, FFN_COLS), F32)
        pad_ref[FFN_PAD:, :] = gp_ref[...].astype(F32)

        def chunk(i, _):
            base = pl.multiple_of(i * FFN_CH, FFN_CH)
            gate = _ffn_gate(pad_ref[pl.ds(base, FFN_CH + FFN_PAD), :], w_ref, cb_ref, FFN_CH)
            act_ref[pl.ds(base, FFN_CH), :] = (_gelu(gate) * val_ref[pl.ds(base, FFN_CH), :].astype(F32)).astype(BF16)
            return 0

        lax.fori_loop(0, S // FFN_CH, chunk, 0)

    return pl.pallas_call(
        body, name=name, out_shape=jax.ShapeDtypeStruct((B * S, F), BF16), grid=(B, nc),
        in_specs=[pl.BlockSpec((S, FFN_COLS), lambda b, j: (b, 2 * j)), pl.BlockSpec((S, FFN_COLS), lambda b, j: (b, 2 * j + 1)),
                  pl.BlockSpec((FFN_K, FFN_COLS), lambda b, j: (0, j)), pl.BlockSpec((1, FFN_COLS), lambda b, j: (0, j))],
        out_specs=pl.BlockSpec((S, FFN_COLS), lambda b, j: (b, j)),
        scratch_shapes=[pltpu.VMEM((S + FFN_PAD, FFN_COLS), F32)],
        compiler_params=_params(("parallel", "parallel")))(up, up, conv_w, conv_b.reshape(1, F))


def _ffn_bwd(dact, up, conv_w, conv_b, *, B, S, name):
    F = conv_w.shape[1]
    nc = F // FFN_COLS
    SUB = 8

    def body(dact_ref, gp_ref, val_ref, w_ref, cb_ref, dup_ref, dw_ref, dcb_ref, pad_ref, pad2_ref, acc_ref):
        first = pl.program_id(1) == 0
        pad_ref[0:FFN_PAD, :] = jnp.zeros((FFN_PAD, FFN_COLS), F32)
        pad_ref[FFN_PAD:, :] = gp_ref[...].astype(F32)
        pad2_ref[S:, :] = jnp.zeros((FFN_PAD, FFN_COLS), F32)
        acc_ref[...] = jnp.zeros_like(acc_ref)

        def chunk1(i, _):
            base = pl.multiple_of(i * FFN_CH, FFN_CH)
            rows = pl.ds(base, FFN_CH)
            gate = _ffn_gate(pad_ref[pl.ds(base, FFN_CH + FFN_PAD), :], w_ref, cb_ref, FFN_CH)
            gl, dgl = _gelu_and_grad(gate)
            da = dact_ref[rows, :].astype(F32)
            dup_ref[rows, FFN_COLS:2 * FFN_COLS] = (da * gl).astype(BF16)
            pad2_ref[rows, :] = da * val_ref[rows, :].astype(F32) * dgl
            return 0

        lax.fori_loop(0, S // FFN_CH, chunk1, 0)

        def chunk2(i, _):
            base = pl.multiple_of(i * FFN_CH, FFN_CH)
            rows = pl.ds(base, FFN_CH)
            win2 = pad2_ref[pl.ds(base, FFN_CH + FFN_PAD), :]
            win1 = pad_ref[pl.ds(base, FFN_CH + FFN_PAD), :]
            dg = win2[0:FFN_CH, :]
            dgp = jnp.zeros((FFN_CH, FFN_COLS), F32)
            for j in range(FFN_K):
                off2 = FFN_K - 1 - j
                dgp = dgp + w_ref[j:j + 1, :] * win2[off2:off2 + FFN_CH, :]
                off1 = FFN_PAD - (FFN_K - 1) + j
                prod = dg * win1[off1:off1 + FFN_CH, :]
                acc_ref[j * SUB:(j + 1) * SUB, :] += jnp.sum(prod.reshape(FFN_CH // SUB, SUB, FFN_COLS), axis=0)
            acc_ref[FFN_K * SUB:(FFN_K + 1) * SUB, :] += jnp.sum(dg.reshape(FFN_CH // SUB, SUB, FFN_COLS), axis=0)
            dup_ref[rows, 0:FFN_COLS] = dgp.astype(BF16)
            return 0

        lax.fori_loop(0, S // FFN_CH, chunk2, 0)
        sums = jnp.sum(acc_ref[...].reshape(FFN_K + 1, SUB, FFN_COLS), axis=1)
        _acc_out(dw_ref, sums[0:FFN_K, :], first)
        _acc_out(dcb_ref, sums[FFN_K:FFN_K + 1, :], first)

    return pl.pallas_call(
        body, name=name,
        out_shape=(jax.ShapeDtypeStruct((B * S, 2 * F), BF16),
                   jax.ShapeDtypeStruct((FFN_K, F), F32), jax.ShapeDtypeStruct((1, F), F32)),
        grid=(nc, B),
        in_specs=[pl.BlockSpec((S, FFN_COLS), lambda j, b: (b, j)), pl.BlockSpec((S, FFN_COLS), lambda j, b: (b, 2 * j)),
                  pl.BlockSpec((S, FFN_COLS), lambda j, b: (b, 2 * j + 1)),
                  pl.BlockSpec((FFN_K, FFN_COLS), lambda j, b: (0, j)), pl.BlockSpec((1, FFN_COLS), lambda j, b: (0, j))],
        out_specs=(pl.BlockSpec((S, 2 * FFN_COLS), lambda j, b: (b, j)),
                   pl.BlockSpec((FFN_K, FFN_COLS), lambda j, b: (0, j)), pl.BlockSpec((1, FFN_COLS), lambda j, b: (0, j))),
        scratch_shapes=[pltpu.VMEM((S + FFN_PAD, FFN_COLS), F32), pltpu.VMEM((S + FFN_PAD, FFN_COLS), F32),
                        pltpu.VMEM(((FFN_K + 1) * SUB, FFN_COLS), F32)],
        compiler_params=_params(("arbitrary", "arbitrary")))(dact, up, up, conv_w, conv_b.reshape(1, F))


def _row_tile(rows, row_bytes, budget):
    tr = rows
    if rows * row_bytes > budget:
        for t in range(16, rows, 16):
            if rows % t == 0 and t * row_bytes <= budget:
                tr = t
    return tr


def _adamw(w, parts, m, v, *, name, own=None, own_slot=None):
    shape = w.shape
    L = len(parts)
    n = parts[0].shape[0]
    cols = shape[-1]
    rows = w.size // (cols * L)
    w3, m3, v3 = (t.reshape(L, rows, cols) for t in (w, m, v))
    tr = _row_tile(rows, cols * 4, 1 << 19)
    c1 = 1.0 - ADAM_B1 ** ADAM_STEP
    c2 = 1.0 - ADAM_B2 ** ADAM_STEP
    has_own = own is not None

    def body(*refs):
        refs = list(refs)
        slot_ref = refs.pop(0) if has_own else None
        w_ref = refs.pop(0)
        p_refs = [refs.pop(0) for _ in range(L)]
        o_refs = [refs.pop(0) for _ in range(L)] if has_own else None
        m_ref, v_ref, g_ref, d_ref, mo_ref, vo_ref = refs
        layer = pl.program_id(0)
        gv = None
        for j in range(L):
            gj = None
            for i in range(n):
                t = p_refs[j][i].astype(F32)
                if has_own:
                    t = jnp.where(slot_ref[0] == i, o_refs[j][...].astype(F32), t)
                gj = t if gj is None else gj + t
            gv = gj if gv is None else jnp.where(layer == j, gj, gv)
        g_ref[...] = gv
        mn = ADAM_B1 * m_ref[...] + (1.0 - ADAM_B1) * gv
        vn = ADAM_B2 * v_ref[...] + (1.0 - ADAM_B2) * (gv * gv)
        d_ref[...] = -ADAM_LR * ((mn / c1) / (jnp.sqrt(vn / c2) + ADAM_EPS) + ADAM_WD * w_ref[...])
        mo_ref[...] = mn
        vo_ref[...] = vn

    spec = pl.BlockSpec((None, tr, cols), lambda l, i, *_: (l, i, 0))

    def pspec(j):
        return pl.BlockSpec((n, tr, cols), lambda l, i, *_: (0, jnp.where(l == j, i, 0), 0))

    def ospec(j):
        return pl.BlockSpec((None, tr, cols), lambda l, i, slot: (slot[0], jnp.where(l == j, i, 0), 0))

    oshape = jax.ShapeDtypeStruct((L, rows, cols), F32)
    in_specs = [spec] + [pspec(j) for j in range(L)] + ([ospec(j) for j in range(L)] if has_own else []) + [spec, spec]
    args = [w3] + [p.reshape(n, rows, cols) for p in parts]
    if has_own:
        args += [o.reshape(n, rows, cols) for o in own]
    args += [m3, v3]
    grid = (L, rows // tr)
    if has_own:
        grid_spec = pltpu.PrefetchScalarGridSpec(num_scalar_prefetch=1, grid=grid, in_specs=in_specs,
                                                 out_specs=(spec,) * 4)
        outs = pl.pallas_call(body, name=name, out_shape=(oshape,) * 4, grid_spec=grid_spec,
                              compiler_params=_params(("parallel", "parallel")))(own_slot, *args)
    else:
        outs = pl.pallas_call(body, name=name, out_shape=(oshape,) * 4, grid=grid, in_specs=in_specs,
                              out_specs=(spec,) * 4, compiler_params=_params(("parallel", "parallel")))(*args)
    return tuple(t.reshape(shape) for t in outs)


def _chip_sum(g, got, core, *, name):
    shard = got.shape[1:]
    cols = shard[-1]
    rows = math.prod(shard) // cols
    tr = _row_tile(rows, cols * 4, 1 << 20)

    def body(c_ref, g_ref, r_ref, o_ref):
        o_ref[...] = (g_ref[...].astype(F32) + r_ref[...].astype(F32)).astype(o_ref.dtype)

    blk = (None, tr, cols)
    grid_spec = pltpu.PrefetchScalarGridSpec(
        num_scalar_prefetch=1, grid=(N_CHIP, rows // tr),
        in_specs=[pl.BlockSpec(blk, lambda s, i, c: (2 * s + c[0], i, 0)), pl.BlockSpec(blk, lambda s, i, c: (s, i, 0))],
        out_specs=pl.BlockSpec(blk, lambda s, i, c: (s, i, 0)))
    out = pl.pallas_call(body, name=name, out_shape=jax.ShapeDtypeStruct((N_CHIP, rows, cols), g.dtype),
                         grid_spec=grid_spec, compiler_params=_params(("parallel", "parallel")),
                         )(core, g.reshape(N_DEV, rows, cols), got.reshape(N_CHIP, rows, cols))
    return out.reshape((N_CHIP,) + shard)


_HBM = pl.BlockSpec(memory_space=pltpu.HBM)
_MESH = pl.DeviceIdType.MESH


def _my_pos():
    return lax.axis_index("x"), lax.axis_index("y"), lax.axis_index("c")


def _flip(pos, k):
    x, y, c = pos
    fx, fy, fc = (k >> 2) & 1, (k >> 1) & 1, k & 1
    return (x ^ fx if fx else x, y ^ fy if fy else y, c ^ fc if fc else c)


def _index_of(pos):
    return 4 * pos[0] + 2 * pos[1] + pos[2]


def _all_gather(xs, *, name):
    n = len(xs)

    def body(*refs):
        x_refs, out_refs = refs[:n], refs[n:2 * n]
        send_sems, recv_sems = refs[2 * n:]
        me = _my_pos()
        sibling = _flip(me, 1)
        chips = [2, 4, 6]

        def copy(i, k, block_pos, to, from_x=False):
            blk = out_refs[i].at[_index_of(block_pos)]
            return pltpu.make_async_remote_copy(src_ref=x_refs[i] if from_x else blk, dst_ref=blk,
                                                send_sem=send_sems.at[k, i], recv_sem=recv_sems.at[k, i],
                                                device_id=to, device_id_type=_MESH)

        first = [copy(i, 1 + j, me, _flip(me, f), from_x=True) for j, f in enumerate(chips) for i in range(n)]
        first += [copy(i, 0, me, sibling, from_x=True) for i in range(n)]
        for cp in first:
            cp.start()
        passed = []
        for j, f in enumerate(chips):
            for i in range(n):
                copy(i, 1 + j, _flip(me, f), me).wait_recv()
                cp = copy(i, 4 + j, _flip(me, f), sibling)
                cp.start()
                passed.append(cp)
        for i in range(n):
            copy(i, 0, sibling, me).wait_recv()
        for j, f in enumerate(chips):
            for i in range(n):
                copy(i, 4 + j, _flip(sibling, f), me).wait_recv()
        for cp in first + passed:
            cp.wait_send()

    return pl.pallas_call(
        body, name=name, out_shape=tuple(jax.ShapeDtypeStruct((N_DEV,) + x.shape, x.dtype) for x in xs),
        in_specs=[_HBM] * n, out_specs=(_HBM,) * n,
        scratch_shapes=[pltpu.SemaphoreType.DMA((7, n)), pltpu.SemaphoreType.DMA((7, n))],
    )(*xs)


N_CHIP = 4


def _scatter_d2d(gs):
    n = len(gs)

    def make(g_refs, recv_refs, send_sems, recv_sems):
        me = _my_pos()
        sibling = _flip(me, 1)
        c = me[2]
        sends = []
        for i in range(n):
            for s in range(N_CHIP):
                sends.append(pltpu.make_async_remote_copy(
                    src_ref=g_refs[i].at[2 * s + 1 - c], dst_ref=recv_refs[i].at[s],
                    send_sem=send_sems.at[N_CHIP * i + s], recv_sem=recv_sems.at[N_CHIP * i + s],
                    device_id=sibling, device_id_type=_MESH))
        return sends, sends

    shapes = [jax.ShapeDtypeStruct((N_CHIP,) + g.shape[1:], g.dtype) for g in gs]
    return _Exchange(list(gs), shapes, N_CHIP * n, make)


def _scatter_ici(ss):
    n = len(ss)

    def make(s_refs, r_refs, send_sems, recv_sems):
        me = _my_pos()
        my_chip = 2 * me[0] + me[1]
        sends, recvs = [], []
        for k in (1, 2, 3):
            peer = _flip(me, 2 * k)
            peer_chip = 2 * peer[0] + peer[1]
            for i in range(n):
                j = 3 * i + k - 1
                sends.append(pltpu.make_async_remote_copy(
                    src_ref=s_refs[i].at[peer_chip], dst_ref=r_refs[i].at[my_chip], send_sem=send_sems.at[j],
                    recv_sem=recv_sems.at[j], device_id=peer, device_id_type=_MESH))
                recvs.append(pltpu.make_async_remote_copy(
                    src_ref=s_refs[i].at[my_chip], dst_ref=r_refs[i].at[peer_chip], send_sem=send_sems.at[j],
                    recv_sem=recv_sems.at[j], device_id=peer, device_id_type=_MESH))
        return sends, recvs

    return _Exchange(list(ss), [jax.ShapeDtypeStruct(s.shape, s.dtype) for s in ss], 3 * n, make)


def _gather_ici(shards, layers):
    n = len(shards)
    FLIPS = (2, 4, 6, 1)

    def make(x_refs, out_refs, send_sems, recv_sems):
        me = _my_pos()
        sends, recvs = [], []
        for k, f in enumerate(FLIPS):
            peer = _flip(me, f)
            for i in range(n):
                src = x_refs[i] if layers[i] is None else x_refs[i].at[layers[i]]
                j = len(FLIPS) * i + k
                sends.append(pltpu.make_async_remote_copy(
                    src_ref=src, dst_ref=out_refs[i].at[_index_of(me)], send_sem=send_sems.at[j],
                    recv_sem=recv_sems.at[j], device_id=peer, device_id_type=_MESH))
                recvs.append(pltpu.make_async_remote_copy(
                    src_ref=src, dst_ref=out_refs[i].at[_index_of(peer)], send_sem=send_sems.at[j],
                    recv_sem=recv_sems.at[j], device_id=peer, device_id_type=_MESH))
        return sends, recvs

    shapes = [jax.ShapeDtypeStruct((N_DEV,) + (x.shape if l is None else x.shape[1:]), x.dtype)
              for x, l in zip(shards, layers)]
    return _Exchange(list(shards), shapes, len(FLIPS) * n, make)


def _gather_d2d(blocks):
    n = len(blocks)
    FLIPS = (2, 4, 6)

    def make(in_refs, out_refs, send_sems, recv_sems):
        me = _my_pos()
        sibling = _flip(me, 1)
        sends, recvs = [], []
        for k, f in enumerate(FLIPS):
            for i in range(n):
                j = len(FLIPS) * i + k
                mine = out_refs[i].at[_index_of(_flip(me, f))]
                sends.append(pltpu.make_async_remote_copy(
                    src_ref=mine, dst_ref=mine, send_sem=send_sems.at[j], recv_sem=recv_sems.at[j],
                    device_id=sibling, device_id_type=_MESH))
                recvs.append(pltpu.make_async_remote_copy(
                    src_ref=mine, dst_ref=out_refs[i].at[_index_of(_flip(sibling, f))], send_sem=send_sems.at[j],
                    recv_sem=recv_sems.at[j], device_id=sibling, device_id_type=_MESH))
        return sends, recvs

    shapes = [jax.ShapeDtypeStruct(b.shape, b.dtype) for b in blocks]
    return _Exchange(list(blocks), shapes, len(FLIPS) * n, make, aliases={i: i for i in range(n)})


class _Exchange:
    def __init__(self, ins, out_shapes, n_copies, make, aliases=None):
        self.ins, self.out_shapes, self.n_copies, self.make, self.aliases = ins, out_shapes, n_copies, make, aliases or {}


def _exchange_call(ex, *, name):
    n_in, n_out = len(ex.ins), len(ex.out_shapes)

    def body(*refs):
        sends, recvs = ex.make(refs[:n_in], refs[n_in:n_in + n_out], refs[n_in + n_out], refs[n_in + n_out + 1])
        for cp in sends:
            cp.start()
        for cp in recvs:
            cp.wait_recv()
        for cp in sends:
            cp.wait_send()

    return pl.pallas_call(
        body, name=name, out_shape=tuple(ex.out_shapes), in_specs=[_HBM] * n_in, out_specs=(_HBM,) * n_out,
        scratch_shapes=[pltpu.SemaphoreType.DMA((ex.n_copies,)), pltpu.SemaphoreType.DMA((ex.n_copies,))],
        input_output_aliases=ex.aliases,
    )(*ex.ins)


class _Hosts:
    def __init__(self):
        self.plan = {}

    def add(self, name, build, done):
        assert name not in self.plan
        self.plan[name] = (build, done)

    def get(self, name):
        return self.plan[name][0]() if name in self.plan else None

    def put(self, name, outs):
        self.plan.pop(name)[1](outs)


def _t5_bucket(dist):
    n = jnp.maximum(dist, 0)
    max_exact = REL_BUCKETS // 2
    nf = jnp.maximum(n, 1).astype(F32)
    large = max_exact + (jnp.log(nf / max_exact) / math.log(REL_MAX_DIST / max_exact)
                         * (REL_BUCKETS - max_exact)).astype(jnp.int32)
    large = jnp.minimum(large, REL_BUCKETS - 1)
    return jnp.where(n < max_exact, n, large)


def _bias_tables(rel_bias):
    qi = jnp.arange(ATT_BLOCK)[:, None]
    ki = jnp.arange(2 * ATT_BLOCK)[None, :]
    rel = qi + ATT_BLOCK - ki
    out = []
    for gi, (window, dil) in enumerate(DIL_GROUPS):
        span = window // dil
        bucket = _t5_bucket(rel * dil)
        valid = (rel >= 0) & (rel <= span)
        rb = rel_bias[:, gi * HEADS_PER_GROUP:(gi + 1) * HEADS_PER_GROUP]
        tab = functools.reduce(jnp.add, [jnp.where((bucket == b)[:, :, None], rb[b], 0.0)
                                         for b in range(REL_BUCKETS)])
        tab = jnp.where(valid[:, :, None], tab, MASK_VALUE).transpose(2, 0, 1)
        out.append((tab.astype(F32), bucket.astype(jnp.int32)))
    return out


def _regroup(t, B, S, d):
    C = t.shape[-1]
    return t.reshape(B, S // d, d, C).swapaxes(1, 2)


def _ungroup(t):
    B, d, L, C = t.shape
    return t.swapaxes(1, 2).reshape(B * L * d, C)


def _group_qkv(qkv, gi):
    n = len(DIL_GROUPS) * GROUP_COLS
    return jnp.concatenate([qkv[:, j * n + gi * GROUP_COLS: j * n + (gi + 1) * GROUP_COLS] for j in range(3)], axis=1)


def _layer_fwd(l, x0, h1, mem2, W, P, tabs, next_pre_g, *, B, S, hosts=None):
    tag = f"l{l}_"
    mm = functools.partial(_mm, hosts=hosts)
    sv = {"x0": x0, "h1": h1}
    za = mm(h1, W["in_a"], out_dtype=ACT, name=tag + "mm_in_a")
    zb = mm(h1, W["in_b"], out_dtype=ACT, name=tag + "mm_in_b")
    qkv = mm(h1, W["in_c"], out_dtype=BF16, name=tag + "mm_in_c")
    zg = mm(h1, W["in_g"], out_dtype=ACT, name=tag + "mm_in_g")
    a4, a2 = _bra_fwd(za, P["conv_a_w"], P["conv_a_b"], P["ln_a_g"], P["ln_a_b"], B=B, S=S, name=tag + "bra_fwd")
    ya = mm(a4, W["a_out"], out_dtype=ACT, name=tag + "mm_a_out")
    b_s_t = P["b_s"].T
    p = _brb_fwd(zb, P["ln_b_g"], P["ln_b_b"], P["w_s"], b_s_t, name=tag + "brb_fwd")
    yb = mm(p, W["b_out"], out_dtype=ACT, name=tag + "mm_b_out")
    os_, lses, qkv_gs = [], [], []
    for gi, (_, dil) in enumerate(DIL_GROUPS):
        qkv_g = _regroup(_group_qkv(qkv, gi), B, S, dil)
        o_g, lse_g = _attn_fwd(qkv_g, tabs[gi][0], name=tag + f"attn_fwd{gi}")
        qkv_gs.append(qkv_g)
        os_.append(_ungroup(o_g))
        lses.append(_ungroup(lse_g))
    oc, lse_tot = _attn_combine(os_, lses, name=tag + "attn_combine")
    yc = mm(oc, W["c_out"], out_dtype=ACT, name=tag + "mm_c_out")
    merged = _merge_fwd(zg, P["b_gate"], ya, yb, yc, name=tag + "merge_fwd")
    y1 = mm(merged, W["mix_out"], out_dtype=ACT, name=tag + "mm_mix")
    x1, h2 = _norm_fwd(x0, y1, P["mix_post_g"], P["x_pre_g"], name=tag + "norm1")
    q = mm(h2, W["xq"], out_dtype=BF16, name=tag + "mm_xq")
    (memn,) = _norm_fwd(mem2, None, None, P["mem_g"], name=tag + "norm_mem")
    kv = mm(memn, W["xkv"], out_dtype=BF16, name=tag + "mm_xkv")
    ox = _xattn_fwd(q, kv, B=B, S=S, name=tag + "xattn_fwd")
    y2 = mm(ox, W["xo"], out_dtype=ACT, name=tag + "mm_xo")
    x2, h3 = _norm_fwd(x1, y2, P["x_post_g"], P["ffn_pre_g"], name=tag + "norm2")
    up = mm(h3, W["up"], out_dtype=ACT, name=tag + "mm_up")
    act = _ffn_fwd(up, P["conv_f_w"], P["conv_f_b"], B=B, S=S, name=tag + "ffn_fwd")
    y3 = mm(act, W["down"], out_dtype=ACT, name=tag + "mm_down")
    outs = _norm_fwd(x2, y3, P["ffn_post_g"], next_pre_g, name=tag + "norm3")
    x3 = outs[0]
    h_next = outs[1] if next_pre_g is not None else None
    sv.update(za=za, zb=zb, zg=zg, a4=a4, a2=a2, ya=ya, p=p, yb=yb, qkv_gs=qkv_gs, oc=oc, lse_tot=lse_tot, yc=yc,
              merged=merged, y1=y1, x1=x1, h2=h2, q=q, memn=memn, kv=kv, ox=ox, y2=y2, x2=x2, h3=h3, up=up, act=act,
              y3=y3, x3=x3, b_s_t=b_s_t)
    return x3, h_next, sv


def _layer_bwd(l, d, dh_next, next_pre_g, sv, mem2, W, WT, P, tabs, G, *, B, S, hosts=None):
    tag = f"l{l}_"
    mm = functools.partial(_mm, hosts=hosts, trans_b=True)
    mm_tn = functools.partial(_mm_tn, hosts=hosts)
    if dh_next is not None:
        d, dy3, G["ffn_post_g"], dg_next = _norm_bwd(d, dh_next, sv["x3"], next_pre_g, sv["y3"], P["ffn_post_g"],
                                                     name=tag + "norm3_bwd")
    else:
        dy3, G["ffn_post_g"] = _norm_bwd(d, None, None, None, sv["y3"], P["ffn_post_g"], name=tag + "norm3_bwd")
        dg_next = None
    dact = mm(dy3, W["down"], out_dtype=ACT, name=tag + "mm_down_dx")
    G["w_down"] = mm_tn(sv["act"], dy3, name=tag + "mm_down_dw")
    dup, G["conv_f_w"], G["conv_f_b"] = _ffn_bwd(dact, sv["up"], P["conv_f_w"], P["conv_f_b"], B=B, S=S,
                                                name=tag + "ffn_bwd")
    dh3 = mm(dup, W["up"], out_dtype=ACT, name=tag + "mm_up_dx")
    G["w_up"] = _ffn_deinterleave(mm_tn(sv["h3"], dup, name=tag + "mm_up_dw"))
    d, dy2, G["x_post_g"], G["ffn_pre_g"] = _norm_bwd(d, dh3, sv["x2"], P["ffn_pre_g"], sv["y2"], P["x_post_g"],
                                                      name=tag + "norm2_bwd")
    dox = mm(dy2, W["xo"], out_dtype=BF16, name=tag + "mm_xo_dx")
    G["w_xo"] = mm_tn(sv["ox"], dy2, name=tag + "mm_xo_dw")
    dq, dkv = _xattn_bwd(sv["q"], sv["kv"], dox, B=B, S=S, name=tag + "xattn_bwd")
    dh2 = mm(dq, W["xq"], out_dtype=ACT, name=tag + "mm_xq_dx")
    G["w_xq"] = mm_tn(sv["h2"], dq, name=tag + "mm_xq_dw")
    G["w_xkv"] = mm_tn(sv["memn"], dkv, name=tag + "mm_xkv_dw")
    dmemn = mm(dkv, W["xkv"], out_dtype=F32, name=tag + "mm_xkv_dx")
    (G["mem_g"],) = _norm_bwd(None, dmemn, mem2, P["mem_g"], None, None, name=tag + "norm_mem_bwd")
    d, dy1, G["mix_post_g"], G["x_pre_g"] = _norm_bwd(d, dh2, sv["x1"], P["x_pre_g"], sv["y1"], P["mix_post_g"],
                                                      name=tag + "norm1_bwd")
    dm = mm(dy1, W["mix_out"], out_dtype=ACT, name=tag + "mm_mix_dx")
    G["w_mix_out"] = mm_tn(sv["merged"], dy1, name=tag + "mm_mix_dw")
    dya, dyb, dyc, dzg, dbg = _merge_bwd(dm, sv["zg"], P["b_gate"], sv["ya"], sv["yb"], sv["yc"], name=tag + "merge_bwd")
    G["b_gate"] = dbg.reshape(P["b_gate"].shape)
    da4 = mm(dya, W["a_out"], out_dtype=ACT, name=tag + "mm_a_out_dx")
    G["w_a_out"] = mm_tn(sv["a4"], dya, name=tag + "mm_a_out_dw")
    da2, G["ln_a_g"], G["ln_a_b"], G["conv_a_b"] = _bra_bwd_ln(da4, sv["a2"], P["ln_a_g"], P["ln_a_b"], name=tag + "bra_bwd_ln")
    dza, G["conv_a_w"] = _bra_bwd_conv(da2, sv["za"], P["conv_a_w"], B=B, S=S, name=tag + "bra_bwd_conv")
    dp = mm(dyb, W["b_out"], out_dtype=ACT, name=tag + "mm_b_out_dx")
    G["w_b_out"] = mm_tn(sv["p"], dyb, name=tag + "mm_b_out_dw")
    dzb, G["w_s"], dbs_t, G["ln_b_g"], G["ln_b_b"] = _brb_bwd(dp, sv["zb"], P["ln_b_g"], P["ln_b_b"], P["w_s"],
                                                             sv["b_s_t"], name=tag + "brb_bwd")
    G["b_s"] = dbs_t.T
    doc = mm(dyc, W["c_out"], out_dtype=ACT, name=tag + "mm_c_out_dx")
    G["w_c_out"] = mm_tn(sv["oc"], dyc, name=tag + "mm_c_out_dw")
    dd = _attn_rowdot(doc, sv["oc"], name=tag + "attn_rowdot")
    dq_parts, dk_parts, dv_parts, dbiases = [], [], [], []
    for gi, (_, dil) in enumerate(DIL_GROUPS):
        dqkv_g, dbias = _attn_bwd(sv["qkv_gs"][gi], _regroup(doc, B, S, dil), _regroup(sv["lse_tot"], B, S, dil),
                                  _regroup(dd, B, S, dil), tabs[gi][0], name=tag + f"attn_bwd{gi}")
        t = _ungroup(dqkv_g)
        dq_parts.append(t[:, 0:GROUP_COLS])
        dk_parts.append(t[:, GROUP_COLS:2 * GROUP_COLS])
        dv_parts.append(t[:, 2 * GROUP_COLS:3 * GROUP_COLS])
        dbiases.append(dbias)
    dqkv = jnp.concatenate(dq_parts + dk_parts + dv_parts, axis=1)
    G["dbias"] = jnp.concatenate(dbiases, axis=0)
    h1 = sv["h1"]
    G["w_in"] = jnp.concatenate([mm_tn(h1, dza, name=tag + "mm_in_a_dw"), mm_tn(h1, dzb, name=tag + "mm_in_b_dw"),
                                 mm_tn(h1, dqkv, name=tag + "mm_in_c_dw"), mm_tn(h1, dzg, name=tag + "mm_in_g_dw")],
                                axis=1)
    dh1 = mm(dza, W["in_a"], out_dtype=F32, name=tag + "mm_in_a_dx")
    dh1 = mm(dzb, W["in_b"], out_dtype=F32, name=tag + "mm_in_b_dx", add=dh1)
    dh1 = mm(dqkv, W["in_c"], out_dtype=F32, name=tag + "mm_in_c_dx", add=dh1)
    dh1 = mm(dzg, W["in_g"], out_dtype=ACT, name=tag + "mm_in_g_dx", add=dh1)
    return d, dh1, dg_next


_COL_SHARDED = ("w_in", "b_gate", "conv_a_w", "w_a_out", "w_b_out", "w_c_out", "w_xkv", "w_up", "conv_f_w")
_ROW_SHARDED = ("w_mix_out", "w_xq", "w_xo", "w_down")
_SHARDED_BIG = ("w_in", "w_a_out", "w_b_out", "w_c_out", "w_mix_out", "w_xq", "w_xkv", "w_xo", "w_up", "w_down")
_SHARDED_SMALL = ("b_gate", "conv_a_w", "conv_f_w")
_REPLICATED = ("mix_pre_g", "mix_post_g", "conv_a_b", "ln_a_g", "ln_a_b", "ln_b_g", "ln_b_b", "w_s", "b_s",
               "x_pre_g", "x_post_g", "mem_g", "ffn_pre_g", "ffn_post_g", "conv_f_b")
_WEIGHTS = ('rel_bias', 'mix_pre_g', 'mix_post_g', 'w_in', 'b_gate', 'conv_a_w', 'conv_a_b', 'ln_a_g', 'ln_a_b',
            'w_a_out', 'ln_b_g', 'ln_b_b', 'w_s', 'b_s', 'w_b_out', 'w_c_out', 'w_mix_out', 'x_pre_g', 'x_post_g',
            'mem_g', 'w_xq', 'w_xkv', 'w_xo', 'ffn_pre_g', 'ffn_post_g', 'w_up', 'conv_f_w', 'conv_f_b', 'w_down')
_PACK_COLS = 1024


def _shard_axis(name):
    return 1 if name in _ROW_SHARDED else 2


def _pack(parts, dtype, row_mult):
    flat = jnp.concatenate([p.astype(dtype).reshape(-1) for p in parts])
    n = flat.shape[0]
    unit = _PACK_COLS * row_mult
    padded = -(-n // unit) * unit
    flat = jnp.pad(flat, (0, padded - n))
    return flat.reshape(padded // _PACK_COLS, _PACK_COLS)


def _join8(blocks, ax):
    t = jnp.moveaxis(blocks, 0, ax)
    shp = t.shape
    return t.reshape(shp[:ax] + (shp[ax] * shp[ax + 1],) + shp[ax + 2:])


def _split8(full, ax):
    shp = full.shape
    t = full.reshape(shp[:ax] + (N_DEV, shp[ax] // N_DEV) + shp[ax + 1:])
    return jnp.moveaxis(t, ax, 0)


_W_KEYS = {"w_a_out": "a_out", "w_b_out": "b_out", "w_c_out": "c_out", "w_mix_out": "mix_out", "w_xq": "xq",
           "w_xkv": "xkv", "w_xo": "xo", "w_up": "up", "w_down": "down"}
_IN_SPLITS = (("in_a", 0, 1024), ("in_b", 1024, 2048), ("in_c", 2048, 4352), ("in_g", 4352, 7424))


def _layer_weights(name, full_l):
    if name == "w_in":
        assert full_l.shape[1] == _IN_SPLITS[-1][2]
        W = {k: full_l[:, a:b] for k, a, b in _IN_SPLITS}
    else:
        W = {_W_KEYS[name]: _ffn_interleave(full_l) if name == "w_up" else full_l}
    return W, {k: v.T for k, v in W.items()}


def _sum_rows(x, *, name):
    n = x.shape[0]

    def body(x_ref, o_ref):
        acc = x_ref[0:1, :]
        for i in range(1, n):
            acc = acc + x_ref[i:i + 1, :]
        o_ref[...] = acc

    return pl.pallas_call(body, name=name, out_shape=jax.ShapeDtypeStruct((1, x.shape[1]), x.dtype))(x)


def kernel(x, mem, rel_bias, mix_pre_g, mix_post_g, w_in, b_gate, conv_a_w, conv_a_b, ln_a_g, ln_a_b, w_a_out, ln_b_g, ln_b_b, w_s, b_s, w_b_out, w_c_out, w_mix_out, x_pre_g, x_post_g, mem_g, w_xq, w_xkv, w_xo, ffn_pre_g, ffn_post_g, w_up, conv_f_w, conv_f_b, w_down, loss_target, m_rel_bias, m_mix_pre_g, m_mix_post_g, m_w_in, m_b_gate, m_conv_a_w, m_conv_a_b, m_ln_a_g, m_ln_a_b, m_w_a_out, m_ln_b_g, m_ln_b_b, m_w_s, m_b_s, m_w_b_out, m_w_c_out, m_w_mix_out, m_x_pre_g, m_x_post_g, m_mem_g, m_w_xq, m_w_xkv, m_w_xo, m_ffn_pre_g, m_ffn_post_g, m_w_up, m_conv_f_w, m_conv_f_b, m_w_down, v_rel_bias, v_mix_pre_g, v_mix_post_g, v_w_in, v_b_gate, v_conv_a_w, v_conv_a_b, v_ln_a_g, v_ln_a_b, v_w_a_out, v_ln_b_g, v_ln_b_b, v_w_s, v_b_s, v_w_b_out, v_w_c_out, v_w_mix_out, v_x_pre_g, v_x_post_g, v_mem_g, v_w_xq, v_w_xkv, v_w_xo, v_ffn_pre_g, v_ffn_post_g, v_w_up, v_conv_f_w, v_conv_f_b, v_w_down):
    args = locals()
    w_loc = {n: args[n] for n in _WEIGHTS}
    m_loc = {n: args["m_" + n] for n in _WEIGHTS}
    v_loc = {n: args["v_" + n] for n in _WEIGHTS}

    depth = w_in.shape[0]
    assert depth == 2
    B, S, D = x.shape
    mine = _index_of(_my_pos())
    core = lax.axis_index("c").astype(jnp.int32).reshape(1)
    my_chip = (2 * lax.axis_index("x") + lax.axis_index("y")).astype(jnp.int32).reshape(1)
    shard = {n: w_loc[n].astype(BF16) for n in _SHARDED_BIG}
    shard.update({n: w_loc[n] for n in _SHARDED_SMALL})
    Ws, WTs = [{} for _ in range(depth)], [{} for _ in range(depth)]
    small_full = {}
    hosts = _Hosts()

    def install(items, blocks):
        for (n, l), blk in zip(items, blocks):
            own = shard[n] if l is None else shard[n][l]
            sel = lax.broadcasted_iota(jnp.int32, (N_DEV,) + (1,) * own.ndim, 0) == mine
            blk = jnp.where(sel, own[None], blk)
            if l is None:
                small_full[n] = _join8(blk, _shard_axis(n))
            else:
                W, WT = _layer_weights(n, _join8(blk, _shard_axis(n) - 1))
                Ws[l].update(W)
                WTs[l].update(WT)

    def carry_gather(name, items, group, then=None):
        def done(outs):
            group.extend(zip(items, outs))
            if then is not None:
                then()
        hosts.add(name, lambda: _gather_ici([shard[n] for n, _ in items], [l for _, l in items]), done)

    def finish_gather(group, tag):
        install([it for it, _ in group], _exchange_call(_gather_d2d([b for _, b in group]), name="gather_d2d_" + tag))

    first = [("w_in", 0)] + [(n, None) for n in _SHARDED_SMALL]
    install(first, _all_gather([shard[n] if l is None else shard[n][l] for n, l in first], name="gather_first"))
    g0, g1 = [], []
    carry_gather("l0_mm_in_a", [("w_xq", 0), ("w_a_out", 0), ("w_b_out", 0), ("w_c_out", 0)], g0)
    carry_gather("l0_mm_in_b", [("w_xo", 0), ("w_mix_out", 0)], g0)
    carry_gather("l0_mm_in_c", [("w_xkv", 0), ("w_down", 0)], g0)
    carry_gather("l0_mm_in_g", [("w_up", 0)], g0, then=lambda: finish_gather(g0, "l0"))
    carry_gather("l0_mm_mix", [("w_xq", 1), ("w_mix_out", 1)], g1)
    carry_gather("l0_mm_xq", [("w_xo", 1), ("w_a_out", 1), ("w_b_out", 1), ("w_c_out", 1)], g1)
    carry_gather("l0_mm_xo", [("w_xkv", 1)], g1)
    carry_gather("l0_mm_up", [("w_in", 1), ("w_down", 1)], g1)
    carry_gather("l0_mm_down", [("w_up", 1)], g1, then=lambda: finish_gather(g1, "l1"))

    Gs = [{} for _ in range(depth)]
    sendbuf, chip_sums, parts = {}, {}, {}

    def to_send(item):
        n, l = item
        if l is None:
            g = jnp.stack([Gs[k][n].reshape(small_full[n].shape[1:]) for k in range(depth)], axis=0)
            sendbuf[item] = _split8(g, _shard_axis(n))
        else:
            sendbuf[item] = _split8(Gs[l][n], _shard_axis(n) - 1).astype(BF16)
        return sendbuf[item]

    def swapped(items, got):
        for item, r in zip(items, got):
            chip_sums[item] = _chip_sum(sendbuf[item], r, core, name=f"chip_sum_{item[0]}_{item[1]}")

    def carry_swap(name, items):
        hosts.add(name, lambda: _scatter_d2d([to_send(it) for it in items]), lambda outs: swapped(items, outs))

    def carry_scatter(name, items):
        hosts.add(name, lambda: _scatter_ici([chip_sums[it] for it in items]), lambda outs: parts.update(zip(items, outs)))

    carry_swap("l0_mm_down_dx", [(n, 1) for n in _SHARDED_BIG])
    carry_scatter("l0_mm_down_dw", [(n, 1) for n in ("w_xq", "w_xkv", "w_xo", "w_mix_out", "w_a_out", "w_b_out", "w_c_out")])
    carry_scatter("l0_mm_up_dx", [("w_in", 1)])
    carry_scatter("l0_mm_up_dw", [("w_up", 1), ("w_down", 1)])
    carry_swap("l0_mm_xo_dx", [("w_down", 0), ("w_up", 0)])
    carry_scatter("l0_mm_xq_dx", [("w_down", 0)])
    carry_swap("l0_mm_mix_dx", [("w_xo", 0), ("w_xq", 0), ("w_xkv", 0)])
    carry_swap("l0_mm_in_a_dw", [("w_mix_out", 0), ("w_a_out", 0), ("w_b_out", 0), ("w_c_out", 0)])
    carry_scatter("l0_mm_in_b_dw", [("w_mix_out", 0), ("w_a_out", 0), ("w_b_out", 0), ("w_c_out", 0)])
    carry_scatter("l0_mm_in_c_dw", [("w_xo", 0), ("w_xq", 0), ("w_xkv", 0)])
    carry_scatter("l0_mm_in_g_dw", [("w_up", 0)])
    carry_swap("l0_mm_in_a_dx", [("w_in", 0)])
    halves = {}

    def carry_scatter_half(name, item, half):
        def build():
            cs = chip_sums[item]
            k = cs.shape[1] // 2
            return _scatter_ici([cs[:, half * k:(half + 1) * k]])
        hosts.add(name, build, lambda outs: halves.__setitem__(half, outs[0]))

    carry_scatter_half("l0_mm_in_c_dx", ("w_in", 0), 0)
    carry_scatter_half("l0_mm_in_g_dx", ("w_in", 0), 1)

    rep_w = {n: w_loc[n] for n in ("rel_bias",) + _REPLICATED}
    Ps = lambda: [dict({n: rep_w[n][l] for n in _REPLICATED}, **{n: small_full[n][l] for n in _SHARDED_SMALL})
                  for l in range(depth)]
    loss_vec, grad_x, g_rel = _run_step(x, mem, loss_target, Ws, WTs, Ps(), rep_w["rel_bias"], Gs, hosts)
    assert not hosts.plan, list(hosts.plan)
    parts[("w_in", 0)] = jnp.concatenate([halves[0], halves[1]], axis=1)

    last = [(n, None) for n in _SHARDED_SMALL]
    swapped(last, _exchange_call(_scatter_d2d([to_send(it) for it in last]), name="swap_grads_last"))
    parts.update(zip(last, _exchange_call(_scatter_ici([chip_sums[it] for it in last]), name="scatter_grads_last")))

    rep = ("rel_bias",) + _REPLICATED
    rep_g = {n: jnp.stack([Gs[l][n].reshape(w_loc[n].shape[1:]) for l in range(depth)], axis=0) for n in _REPLICATED}
    rep_g["rel_bias"] = g_rel
    packed = _pack([loss_vec] + [rep_g[n] for n in rep], F32, 8)
    (allp,) = _all_gather([packed], name="gather_rep_grads")
    sel = lax.broadcasted_iota(jnp.int32, (N_DEV, 1, 1), 0) == mine
    allp = jnp.where(sel, packed[None], allp).reshape(N_DEV, -1)
    loss = _sum_rows(allp[:, :LANE], name="loss_sum")[0, 0]
    off = LANE
    rep_parts = {}
    for n in rep:
        size = math.prod(w_loc[n].shape)
        rep_parts[n] = allp[:, off:off + size].reshape((N_DEV,) + w_loc[n].shape)
        off += size

    g_loc, deltas, new_m, new_v = {}, {}, {}, {}
    for n in _WEIGHTS:
        if n in rep_parts:
            p, own = [rep_parts[n]], {}
        else:
            its = [(n, None)] if n in _SHARDED_SMALL else [(n, l) for l in range(depth)]
            p, own = [parts[it] for it in its], {"own": [chip_sums[it] for it in its], "own_slot": my_chip}
        g_loc[n], deltas[n], new_m[n], new_v[n] = _adamw(w_loc[n], p, m_loc[n], v_loc[n], name="adamw_" + n, **own)
    return (loss, grad_x, *[g_loc[n] for n in _WEIGHTS], *[deltas[n] for n in _WEIGHTS],
            *[new_m[n] for n in _WEIGHTS], *[new_v[n] for n in _WEIGHTS])


def _local_step(x, mem, loss_target, full):
    depth = full["w_in"].shape[0]
    Ws, WTs, Ps = [{} for _ in range(depth)], [{} for _ in range(depth)], []
    for l in range(depth):
        Ps.append({n: full[n][l] for n in _WEIGHTS if n != "rel_bias"})
        for n in _SHARDED_BIG:
            W, WT = _layer_weights(n, full[n][l].astype(BF16))
            Ws[l].update(W)
            WTs[l].update(WT)
    Gs = [{} for _ in range(depth)]
    loss_vec, grad_x, g_rel = _run_step(x, mem, loss_target, Ws, WTs, Ps, full["rel_bias"], Gs, None)
    grads = {"rel_bias": g_rel}
    for n in _WEIGHTS:
        if n != "rel_bias":
            grads[n] = jnp.stack([Gs[l][n].reshape(full[n].shape[1:]) for l in range(depth)], axis=0)
    return loss_vec[0, 0], grad_x, grads


def _run_step(x, mem, loss_target, Ws, WTs, Ps, rel_bias, Gs, hosts):
    B, S, D = x.shape
    depth = len(Ws)
    x2d = x.reshape(B * S, D)
    mem2 = mem.reshape(-1, D)
    tabs = _bias_tables(rel_bias)
    (h1,) = _norm_fwd(x2d, None, None, Ps[0]["mix_pre_g"], name="norm0")
    xc = x2d
    saved = []
    for l in range(depth):
        nxt = Ps[l + 1]["mix_pre_g"] if l + 1 < depth else None
        xc, h1, sv = _layer_fwd(l, xc, h1, mem2, Ws[l], Ps[l], tabs, nxt, B=B, S=S, hosts=hosts)
        saved.append(sv)
    loss_vec, d = _loss_kernel(xc, loss_target.reshape(B * S, D), name="loss")

    dh_next = None
    for l in reversed(range(depth)):
        nxt = Ps[l + 1]["mix_pre_g"] if l + 1 < depth else None
        d, dh_next, dg_next = _layer_bwd(l, d, dh_next, nxt, saved[l], mem2, Ws[l], WTs[l], Ps[l], tabs, Gs[l], B=B, S=S,
                                         hosts=hosts)
        if dg_next is not None:
            Gs[l + 1]["mix_pre_g"] = dg_next
    grad_x2d, Gs[0]["mix_pre_g"] = _norm_bwd(d, dh_next, x2d, Ps[0]["mix_pre_g"], None, None, name="norm0_bwd")

    dbias = jnp.stack([Gs[l]["dbias"] for l in range(depth)], axis=0)
    buckets = jnp.stack([t[1] for t in tabs], axis=0)
    rb = _bucket_sum(dbias, buckets, name="rel_bias_grad")
    return loss_vec, grad_x2d.reshape(B, S, D), rb[:, :rel_bias.shape[1]]
```

```python
import functools
import math

import jax
import jax.numpy as jnp
from jax import lax
from jax.experimental import pallas as pl
from jax.experimental.pallas import tpu as pltpu

F32 = jnp.float32
BF16 = jnp.bfloat16

N_DEV = 8
NORM_EPS = 1e-6
LN_EPS = 1e-5
CONV_K = 31
SG_CHUNK = 128
ATT_BLOCK = 128
HEAD_DIM = 64
HEADS_PER_GROUP = 4
GROUP_COLS = HEADS_PER_GROUP * HEAD_DIM
DIL_GROUPS = ((128, 1), (512, 4), (2048, 16))
REL_BUCKETS = 32
REL_MAX_DIST = 2048
X_HEADS = 4
ATT_SCALE = HEAD_DIM ** -0.5
MASK_VALUE = -1e30

ADAM_LR = 0.001
ADAM_B1 = 0.9
ADAM_B2 = 0.999
ADAM_EPS = 1e-08
ADAM_WD = 0.01
ADAM_STEP = 10

LANE = 128
ACT = BF16
ROW_TILE = 512
VMEM_LIMIT = 48 << 20

_NT = (((1,), (1,)), ((), ()))
_TN = (((0,), (0,)), ((), ()))


def _params(sem, vmem=VMEM_LIMIT):
    return pltpu.CompilerParams(dimension_semantics=sem, vmem_limit_bytes=vmem)


def _pick(n, cap):
    if n <= cap:
        return n
    best = None
    for t in range(LANE, cap + 1, LANE):
        if n % t == 0:
            best = t
    assert best is not None, (n, cap)
    return best


def _sigmoid(x):
    return 1.0 / (1.0 + jnp.exp(-x))


_GELU_C = math.sqrt(2.0 / math.pi)


def _gelu(x):
    return 0.5 * x * (1.0 + jnp.tanh(_GELU_C * (x + 0.044715 * x * x * x)))


def _gelu_and_grad(x):
    t = jnp.tanh(_GELU_C * (x + 0.044715 * x * x * x))
    g = 0.5 * x * (1.0 + t)
    dg = 0.5 * (1.0 + t) + 0.5 * x * (1.0 - t * t) * _GELU_C * (1.0 + 3 * 0.044715 * x * x)
    return g, dg


MM_VMEM_BUDGET = 36 << 20


def _divisors128(n):
    return [n] + [t for t in range(n - n % LANE, 0, -LANE) if n % t == 0 and t != n]


class _Hosted:
    def __init__(self, hosts, name):
        self.ex = hosts.get(name) if hosts is not None else None
        self.hosts, self.name = hosts, name

    @property
    def on(self):
        return self.ex is not None

    def specs(self):
        if not self.on:
            return [], [], [], []
        n = self.ex.n_copies
        return (list(self.ex.ins), [_HBM] * len(self.ex.ins), list(self.ex.out_shapes),
                [pltpu.SemaphoreType.DMA((n,)), pltpu.SemaphoreType.DMA((n,))])

    def split(self, refs, n_in, n_out, n_scratch):
        if not self.on:
            return list(refs), None
        ni, no = len(self.ex.ins), len(self.ex.out_shapes)
        refs = list(refs)
        ins, rest = refs[:n_in], refs[n_in:]
        ex_in, rest = rest[:ni], rest[ni:]
        outs, rest = rest[:n_out], rest[n_out:]
        ex_out, rest = rest[:no], rest[no:]
        scratch, sems = rest[:n_scratch], rest[n_scratch:]
        return ins + outs + scratch, (ex_in, ex_out, sems[0], sems[1])

    def run(self, ex_refs, grid):
        if not self.on:
            return
        ids = [pl.program_id(ax) for ax in range(len(grid))]
        first = functools.reduce(jnp.logical_and, [i == 0 for i in ids])
        last = functools.reduce(jnp.logical_and, [i == g - 1 for i, g in zip(ids, grid)])

        @pl.when(first)
        def _():
            sends, _ = self.ex.make(*ex_refs)
            for cp in sends:
                cp.start()

        return last

    def finish(self, ex_refs, last):
        if not self.on:
            return

        @pl.when(last)
        def _():
            sends, recvs = self.ex.make(*ex_refs)
            for cp in recvs:
                cp.wait_recv()
            for cp in sends:
                cp.wait_send()

    def done(self, outs):
        if self.on:
            self.hosts.put(self.name, list(outs))


def _mm(a, b, *, out_dtype, name, add=None, hosts=None):
    M, K = a.shape
    K2, N = b.shape
    assert K == K2
    has_add = add is not None
    ob = jnp.dtype(out_dtype).itemsize
    tm = _pick(M, 1024)
    tk = _pick(K, 3072)
    nk = K // tk
    for tn in _divisors128(N):
        use = 2 * (tm * tk * a.dtype.itemsize + tk * tn * b.dtype.itemsize + tm * tn * (ob + (4 if has_add else 0)))
        if use + (tm * tn * 4 if nk > 1 else 0) <= MM_VMEM_BUDGET:
            break
    grid = (M // tm, N // tn, nk)
    hosted = _Hosted(hosts, name)
    n_in = 3 if has_add else 2
    n_scr = 1 if nk > 1 else 0

    def body(*refs):
        own, ex_refs = hosted.split(refs, n_in, 1, n_scr)
        last = hosted.run(ex_refs, grid)
        a_ref, b_ref = own[0], own[1]
        c_ref = own[2] if has_add else None
        o_ref = own[n_in]
        part = jnp.dot(a_ref[...].astype(BF16), b_ref[...].astype(BF16), preferred_element_type=F32)
        if nk == 1:
            if has_add:
                part = part + c_ref[...]
            o_ref[...] = part.astype(o_ref.dtype)
        else:
            acc_ref = own[n_in + 1]
            k = pl.program_id(2)

            @pl.when(k == 0)
            def _():
                acc_ref[...] = part

            @pl.when(k > 0)
            def _():
                acc_ref[...] += part

            @pl.when(k == nk - 1)
            def _():
                r = acc_ref[...]
                if has_add:
                    r = r + c_ref[...]
                o_ref[...] = r.astype(o_ref.dtype)
        hosted.finish(ex_refs, last)

    in_specs = [pl.BlockSpec((tm, tk), lambda i, j, k: (i, k)), pl.BlockSpec((tk, tn), lambda i, j, k: (k, j))]
    args = [a, b]
    if has_add:
        in_specs.append(pl.BlockSpec((tm, tn), lambda i, j, k: (i, j)))
        args.append(add)
    ex_args, ex_in_specs, ex_out_shapes, ex_scratch = hosted.specs()
    outs = pl.pallas_call(
        body, name=name, out_shape=(jax.ShapeDtypeStruct((M, N), out_dtype), *ex_out_shapes),
        grid=grid, in_specs=in_specs + ex_in_specs,
        out_specs=(pl.BlockSpec((tm, tn), lambda i, j, k: (i, j)), *([_HBM] * len(ex_out_shapes))),
        scratch_shapes=([pltpu.VMEM((tm, tn), F32)] if nk > 1 else []) + ex_scratch,
        compiler_params=_params(("arbitrary",) * 3 if hosted.on else ("parallel", "parallel", "arbitrary")),
    )(*args, *ex_args)
    hosted.done(outs[1:])
    return outs[0]


def _mm_tn(a, b, *, name, hosts=None, out_dtype=F32):
    T, M = a.shape
    T2, N = b.shape
    assert T == T2
    tm = _pick(M, 1536)
    tt = _pick(T, 2048)
    nt = T // tt
    for tn in _divisors128(N):
        if 2 * (tt * tm * a.dtype.itemsize + tt * tn * b.dtype.itemsize + tm * tn * 4) <= MM_VMEM_BUDGET:
            break
    grid = (M // tm, N // tn, nt)
    hosted = _Hosted(hosts, name)
    narrow = jnp.dtype(out_dtype) != jnp.dtype(F32)

    def body(*refs):
        own, ex_refs = hosted.split(refs, 2, 1, 1 if narrow else 0)
        a_ref, b_ref, o_ref = own[:3]
        acc_ref = own[3] if narrow else o_ref
        last = hosted.run(ex_refs, grid)
        k = pl.program_id(2)
        part = lax.dot_general(a_ref[...].astype(BF16), b_ref[...].astype(BF16), _TN, preferred_element_type=F32)

        @pl.when(k == 0)
        def _():
            acc_ref[...] = part

        @pl.when(k > 0)
        def _():
            acc_ref[...] += part

        if narrow:
            @pl.when(k == nt - 1)
            def _():
                o_ref[...] = acc_ref[...].astype(o_ref.dtype)

        hosted.finish(ex_refs, last)

    ex_args, ex_in_specs, ex_out_shapes, ex_scratch = hosted.specs()
    outs = pl.pallas_call(
        body, name=name, out_shape=(jax.ShapeDtypeStruct((M, N), out_dtype), *ex_out_shapes),
        grid=grid,
        in_specs=[pl.BlockSpec((tt, tm), lambda i, j, k: (k, i)), pl.BlockSpec((tt, tn), lambda i, j, k: (k, j))]
        + ex_in_specs,
        out_specs=(pl.BlockSpec((tm, tn), lambda i, j, k: (i, j)), *([_HBM] * len(ex_out_shapes))),
        scratch_shapes=([pltpu.VMEM((tm, tn), F32)] if narrow else []) + ex_scratch,
        compiler_params=_params(("arbitrary",) * 3 if hosted.on else ("parallel", "parallel", "arbitrary")),
    )(a, b, *ex_args)
    hosted.done(outs[1:])
    return outs[0]


def _rms(x):
    r = lax.rsqrt(jnp.mean(x * x, axis=-1, keepdims=True) + NORM_EPS)
    return x * r, r


def _row_spec(cols, tr=ROW_TILE):
    return pl.BlockSpec((tr, cols), lambda i: (i, 0))


def _vec_spec(cols):
    return pl.BlockSpec((1, cols), lambda i: (0, 0))


def _norm_fwd(x, y, g_post, g_pre, *, name):
    T, D = x.shape
    has_post = y is not None
    has_pre = g_pre is not None

    def body(*refs):
        refs = list(refs)
        x_ref = refs.pop(0)
        xn = x_ref[...]
        if has_post:
            y_ref = refs.pop(0)
            gp_ref = refs.pop(0)
        if has_pre:
            gq_ref = refs.pop(0)
        if has_post:
            yh, _ = _rms(y_ref[...].astype(F32))
            xn = xn + yh * gp_ref[...]
            refs.pop(0)[...] = xn
        if has_pre:
            xh, _ = _rms(xn)
            refs.pop(0)[...] = (xh * gq_ref[...]).astype(BF16)

    args, in_specs, out_shape, out_specs = [x], [_row_spec(D)], [], []
    if has_post:
        args += [y, g_post.reshape(1, D)]
        in_specs += [_row_spec(D), _vec_spec(D)]
        out_shape.append(jax.ShapeDtypeStruct((T, D), F32))
        out_specs.append(_row_spec(D))
    if has_pre:
        args.append(g_pre.reshape(1, D))
        in_specs.append(_vec_spec(D))
        out_shape.append(jax.ShapeDtypeStruct((T, D), BF16))
        out_specs.append(_row_spec(D))
    return pl.pallas_call(body, name=name, out_shape=tuple(out_shape), grid=(T // ROW_TILE,), in_specs=in_specs,
                          out_specs=tuple(out_specs), compiler_params=_params(("parallel",)))(*args)


def _norm_bwd(dres, dh, x_new, g_pre, y, g_post, *, name):
    has_res = dres is not None
    has_pre = dh is not None
    has_post = y is not None
    out_d = has_pre and (has_res or has_post)
    T, D = (dres if has_res else dh).shape
    if not has_res:
        assert not has_post

    def body(*refs):
        refs = list(refs)
        i = pl.program_id(0)
        d = refs.pop(0)[...] if has_res else None
        if has_pre:
            dh_v = refs.pop(0)[...].astype(F32)
            xh, r = _rms(refs.pop(0)[...])
            gq = refs.pop(0)[...]
        if has_post:
            yh, ry = _rms(refs.pop(0)[...].astype(F32))
            gp = refs.pop(0)[...]
        if has_pre:
            dxh = dh_v * gq
            dpre = r * (dxh - xh * jnp.mean(dxh * xh, axis=-1, keepdims=True))
            d = dpre if d is None else d + dpre
            dgq = jnp.sum(dh_v * xh, axis=0, keepdims=True)
        if out_d:
            refs.pop(0)[...] = d
        if has_post:
            dyh = d * gp
            refs.pop(0)[...] = (ry * (dyh - yh * jnp.mean(dyh * yh, axis=-1, keepdims=True))).astype(BF16)
            dgp_ref = refs.pop(0)
            dgp = jnp.sum(d * yh, axis=0, keepdims=True)

            @pl.when(i == 0)
            def _():
                dgp_ref[...] = dgp

            @pl.when(i > 0)
            def _():
                dgp_ref[...] += dgp
        if has_pre:
            dgq_ref = refs.pop(0)

            @pl.when(i == 0)
            def _():
                dgq_ref[...] = dgq

            @pl.when(i > 0)
            def _():
                dgq_ref[...] += dgq

    args, in_specs, out_shape, out_specs = [], [], [], []
    if has_res:
        args.append(dres)
        in_specs.append(_row_spec(D))
    if has_pre:
        args += [dh, x_new, g_pre.reshape(1, D)]
        in_specs += [_row_spec(D), _row_spec(D), _vec_spec(D)]
    if has_post:
        args += [y, g_post.reshape(1, D)]
        in_specs += [_row_spec(D), _vec_spec(D)]
    if out_d:
        out_shape.append(jax.ShapeDtypeStruct((T, D), F32))
        out_specs.append(_row_spec(D))
    if has_post:
        out_shape += [jax.ShapeDtypeStruct((T, D), BF16), jax.ShapeDtypeStruct((1, D), F32)]
        out_specs += [_row_spec(D), _vec_spec(D)]
    if has_pre:
        out_shape.append(jax.ShapeDtypeStruct((1, D), F32))
        out_specs.append(_vec_spec(D))
    return pl.pallas_call(body, name=name, out_shape=tuple(out_shape), grid=(T // ROW_TILE,), in_specs=in_specs,
                          out_specs=tuple(out_specs), compiler_params=_params(("arbitrary",)))(*args)


def _loss_kernel(xf, target, *, name):
    T, D = xf.shape

    def body(x_ref, t_ref, loss_ref, d_ref):
        i = pl.program_id(0)
        e = x_ref[...] - t_ref[...]
        d_ref[...] = e * (1.0 / D)
        part = jnp.sum(jnp.sum(e * e, axis=0, keepdims=True), axis=1, keepdims=True) * (0.5 / D)
        part = jnp.broadcast_to(part, (1, LANE))

        @pl.when(i == 0)
        def _():
            loss_ref[...] = part

        @pl.when(i > 0)
        def _():
            loss_ref[...] += part

    return pl.pallas_call(body, name=name,
                          out_shape=(jax.ShapeDtypeStruct((1, LANE), F32), jax.ShapeDtypeStruct((T, D), F32)),
                          grid=(T // ROW_TILE,), in_specs=[_row_spec(D), _row_spec(D)],
                          out_specs=(_vec_spec(LANE), _row_spec(D)), compiler_params=_params(("arbitrary",)))(xf, target)


CONV_PAD = 32
CONV_CH = 64
SUBLANES = 8


def _shift_rows(win, r):
    return win if r == 0 else win[r:r + win.shape[0] - SUBLANES, :]


def _ln_stats(x):
    mu = jnp.mean(x, axis=-1, keepdims=True)
    xc = x - mu
    rstd = lax.rsqrt(jnp.mean(xc * xc, axis=-1, keepdims=True) + LN_EPS)
    return xc * rstd, rstd


def _acc_out(ref, val, first):
    @pl.when(first)
    def _():
        ref[...] = val

    @pl.when(jnp.logical_not(first))
    def _():
        ref[...] += val


def _bra_fwd(za, conv_w, conv_b, ln_g, ln_b, *, B, S, name):
    C = conv_w.shape[1]

    def body(za_ref, w_ref, cb_ref, g_ref, b_ref, a4_ref, a2_ref, pad_ref):
        pad_ref[0:CONV_PAD, :] = jnp.zeros((CONV_PAD, C), F32)
        pad_ref[CONV_PAD:, :] = za_ref[:, 0:C].astype(F32) * _sigmoid(za_ref[:, C:2 * C].astype(F32))

        def chunk(i, _):
            base = pl.multiple_of(i * CONV_CH, CONV_CH)
            win = pad_ref[pl.ds(base, CONV_CH + CONV_PAD), :]
            acc = jnp.broadcast_to(cb_ref[...], (CONV_CH, C))
            for r in range(SUBLANES):
                sh = _shift_rows(win, r)
                for j in range(CONV_K):
                    off = CONV_PAD - (CONV_K - 1) + j
                    if off % SUBLANES == r:
                        acc = acc + w_ref[j:j + 1, :] * sh[off - r:off - r + CONV_CH, :]
            a2_ref[pl.ds(base, CONV_CH), :] = acc
            xh, _ = _ln_stats(acc)
            a3 = xh * g_ref[...] + b_ref[...]
            a4_ref[pl.ds(base, CONV_CH), :] = (a3 * _sigmoid(a3)).astype(BF16)
            return 0

        lax.fori_loop(0, S // CONV_CH, chunk, 0)

    vec = pl.BlockSpec((1, C), lambda b: (0, 0))
    return pl.pallas_call(
        body, name=name,
        out_shape=(jax.ShapeDtypeStruct((B * S, C), BF16), jax.ShapeDtypeStruct((B * S, C), F32)),
        grid=(B,),
        in_specs=[pl.BlockSpec((S, 2 * C), lambda b: (b, 0)), pl.BlockSpec((CONV_K, C), lambda b: (0, 0)), vec, vec, vec],
        out_specs=(pl.BlockSpec((S, C), lambda b: (b, 0)), pl.BlockSpec((S, C), lambda b: (b, 0))),
        scratch_shapes=[pltpu.VMEM((S + CONV_PAD, C), F32)],
        compiler_params=_params(("parallel",)),
    )(za, conv_w, conv_b.reshape(1, C), ln_g.reshape(1, C), ln_b.reshape(1, C))


def _bra_bwd_ln(da4, a2, ln_g, ln_b, *, name):
    T, C = a2.shape

    def body(da4_ref, a2_ref, g_ref, b_ref, da2_ref, dg_ref, db_ref, dcb_ref):
        first = pl.program_id(0) == 0
        xh, rstd = _ln_stats(a2_ref[...])
        a3 = xh * g_ref[...] + b_ref[...]
        sg = _sigmoid(a3)
        da3 = da4_ref[...].astype(F32) * (sg * (1.0 + a3 * (1.0 - sg)))
        dxh = da3 * g_ref[...]
        da2 = rstd * (dxh - jnp.mean(dxh, axis=-1, keepdims=True) - xh * jnp.mean(dxh * xh, axis=-1, keepdims=True))
        da2_ref[...] = da2
        _acc_out(dg_ref, jnp.sum(da3 * xh, axis=0, keepdims=True), first)
        _acc_out(db_ref, jnp.sum(da3, axis=0, keepdims=True), first)
        _acc_out(dcb_ref, jnp.sum(da2, axis=0, keepdims=True), first)

    vec = _vec_spec(C)
    vshape = jax.ShapeDtypeStruct((1, C), F32)
    return pl.pallas_call(body, name=name, out_shape=(jax.ShapeDtypeStruct((T, C), F32), vshape, vshape, vshape),
                          grid=(T // ROW_TILE,), in_specs=[_row_spec(C), _row_spec(C), vec, vec],
                          out_specs=(_row_spec(C), vec, vec, vec),
                          compiler_params=_params(("arbitrary",)))(da4, a2, ln_g.reshape(1, C), ln_b.reshape(1, C))


def _bra_bwd_conv(da2, za, conv_w, *, B, S, name):
    C = conv_w.shape[1]
    SUB = 8

    def body(da2_ref, za_ref, w_ref, dza_ref, dw_ref, pad1_ref, pad2_ref, dwacc_ref):
        first = pl.program_id(0) == 0
        pad1_ref[0:CONV_PAD, :] = jnp.zeros((CONV_PAD, C), F32)
        pad1_ref[CONV_PAD:, :] = za_ref[:, 0:C].astype(F32) * _sigmoid(za_ref[:, C:2 * C].astype(F32))
        pad2_ref[0:S, :] = da2_ref[...]
        pad2_ref[S:, :] = jnp.zeros((CONV_PAD, C), F32)
        dwacc_ref[...] = jnp.zeros_like(dwacc_ref)

        def chunk(i, _):
            base = pl.multiple_of(i * CONV_CH, CONV_CH)
            rows = pl.ds(base, CONV_CH)
            win1 = pad1_ref[pl.ds(base, CONV_CH + CONV_PAD), :]
            win2 = pad2_ref[pl.ds(base, CONV_CH + CONV_PAD), :]
            d = win2[0:CONV_CH, :]
            da1 = jnp.zeros((CONV_CH, C), F32)
            for r in range(SUBLANES):
                sh1 = _shift_rows(win1, r)
                sh2 = _shift_rows(win2, r)
                for j in range(CONV_K):
                    off2 = CONV_K - 1 - j
                    if off2 % SUBLANES == r:
                        da1 = da1 + w_ref[j:j + 1, :] * sh2[off2 - r:off2 - r + CONV_CH, :]
                    off1 = CONV_PAD - (CONV_K - 1) + j
                    if off1 % SUBLANES == r:
                        prod = d * sh1[off1 - r:off1 - r + CONV_CH, :]
                        dwacc_ref[j * SUB:(j + 1) * SUB, :] += jnp.sum(prod.reshape(CONV_CH // SUB, SUB, C), axis=0)
            a_val = za_ref[rows, 0:C].astype(F32)
            sg = _sigmoid(za_ref[rows, C:2 * C].astype(F32))
            dza_ref[rows, 0:C] = (da1 * sg).astype(BF16)
            dza_ref[rows, C:2 * C] = (da1 * a_val * sg * (1.0 - sg)).astype(BF16)
            return 0

        lax.fori_loop(0, S // CONV_CH, chunk, 0)
        _acc_out(dw_ref, jnp.sum(dwacc_ref[...].reshape(CONV_K, SUB, C), axis=1), first)

    return pl.pallas_call(
        body, name=name,
        out_shape=(jax.ShapeDtypeStruct((B * S, 2 * C), BF16), jax.ShapeDtypeStruct((CONV_K, C), F32)),
        grid=(B,),
        in_specs=[pl.BlockSpec((S, C), lambda b: (b, 0)), pl.BlockSpec((S, 2 * C), lambda b: (b, 0)),
                  pl.BlockSpec((CONV_K, C), lambda b: (0, 0))],
        out_specs=(pl.BlockSpec((S, 2 * C), lambda b: (b, 0)), pl.BlockSpec((CONV_K, C), lambda b: (0, 0))),
        scratch_shapes=[pltpu.VMEM((S + CONV_PAD, C), F32), pltpu.VMEM((S + CONV_PAD, C), F32),
                        pltpu.VMEM((CONV_K * SUB, C), F32)],
        compiler_params=_params(("arbitrary",)),
    )(da2, za, conv_w)


SG_STEP = 4


def _tril_mask():
    r = lax.broadcasted_iota(jnp.int32, (SG_CHUNK, SG_CHUNK), 0)
    c = lax.broadcasted_iota(jnp.int32, (SG_CHUNK, SG_CHUNK), 1)
    return c <= r


def _brb_fwd(zb, ln_g, ln_b, w_s, b_s_t, *, name):
    T, C2 = zb.shape
    C = C2 // 2
    G = w_s.shape[0]
    GC = C // G

    def body(zb_ref, g_ref, b_ref, ws_ref, bs_ref, p_ref):
        z = _gelu(zb_ref[...].astype(F32))
        u = z[:, 0:C]
        xh, _ = _ln_stats(z[:, C:2 * C])
        v1 = (xh * g_ref[...] + b_ref[...]).astype(BF16)
        mask = _tril_mask()
        wss = [jnp.where(mask, ws_ref[gi], 0.0).astype(BF16) for gi in range(G)]
        for ci in range(SG_STEP):
            rows = slice(ci * SG_CHUNK, (ci + 1) * SG_CHUNK)
            outs = [jnp.dot(wss[gi], v1[rows, gi * GC:(gi + 1) * GC], preferred_element_type=F32) + bs_ref[:, gi:gi + 1]
                    for gi in range(G)]
            p_ref[rows, :] = (u[rows, :] * jnp.concatenate(outs, axis=1)).astype(BF16)

    tr = SG_STEP * SG_CHUNK
    return pl.pallas_call(
        body, name=name, out_shape=jax.ShapeDtypeStruct((T, C), BF16), grid=(T // tr,),
        in_specs=[_row_spec(C2, tr), _vec_spec(C), _vec_spec(C),
                  pl.BlockSpec((G, SG_CHUNK, SG_CHUNK), lambda i: (0, 0, 0)), pl.BlockSpec((SG_CHUNK, G), lambda i: (0, 0))],
        out_specs=_row_spec(C, tr), compiler_params=_params(("parallel",)),
    )(zb, ln_g.reshape(1, C), ln_b.reshape(1, C), w_s, b_s_t)


def _brb_bwd(dp, zb, ln_g, ln_b, w_s, b_s_t, *, name):
    T, C2 = zb.shape
    C = C2 // 2
    G = w_s.shape[0]
    GC = C // G

    def body(dp_ref, zb_ref, g_ref, b_ref, ws_ref, bs_ref, dzb_ref, dws_ref, dbs_ref, dg_ref, db_ref):
        first = pl.program_id(0) == 0
        z, dz = _gelu_and_grad(zb_ref[...].astype(F32))
        u = z[:, 0:C]
        xh, rstd = _ln_stats(z[:, C:2 * C])
        v1 = (xh * g_ref[...] + b_ref[...]).astype(BF16)
        dp_v = dp_ref[...].astype(F32)
        mask = _tril_mask()
        wss = [jnp.where(mask, ws_ref[gi], 0.0).astype(BF16) for gi in range(G)]
        dv2_all = dp_v * u
        dv2b_all = dv2_all.astype(BF16)
        v2_rows, dv1_rows = [], []
        dwss, dbss = [None] * G, [None] * G
        for ci in range(SG_STEP):
            rows = slice(ci * SG_CHUNK, (ci + 1) * SG_CHUNK)
            v2s, dv1s = [], []
            for gi in range(G):
                cols = slice(gi * GC, (gi + 1) * GC)
                v2s.append(jnp.dot(wss[gi], v1[rows, cols], preferred_element_type=F32) + bs_ref[:, gi:gi + 1])
                dv2b = dv2b_all[rows, cols]
                dbs = jnp.sum(dv2_all[rows, cols], axis=1, keepdims=True)
                dws = lax.dot_general(dv2b, v1[rows, cols], _NT, preferred_element_type=F32)
                dbss[gi] = dbs if dbss[gi] is None else dbss[gi] + dbs
                dwss[gi] = dws if dwss[gi] is None else dwss[gi] + dws
                dv1s.append(lax.dot_general(wss[gi], dv2b, _TN, preferred_element_type=F32))
            v2_rows.append(jnp.concatenate(v2s, axis=1))
            dv1_rows.append(jnp.concatenate(dv1s, axis=1))
        dwss = [jnp.where(mask, t, 0.0) for t in dwss]
        du = dp_v * jnp.concatenate(v2_rows, axis=0)
        dv1 = jnp.concatenate(dv1_rows, axis=0)
        dxh = dv1 * g_ref[...]
        dv0 = rstd * (dxh - jnp.mean(dxh, axis=-1, keepdims=True) - xh * jnp.mean(dxh * xh, axis=-1, keepdims=True))
        dzb_ref[:, 0:C] = (du * dz[:, 0:C]).astype(BF16)
        dzb_ref[:, C:2 * C] = (dv0 * dz[:, C:2 * C]).astype(BF16)
        _acc_out(dws_ref, jnp.stack(dwss, axis=0), first)
        _acc_out(dbs_ref, jnp.concatenate(dbss, axis=1), first)
        _acc_out(dg_ref, jnp.sum(dv1 * xh, axis=0, keepdims=True), first)
        _acc_out(db_ref, jnp.sum(dv1, axis=0, keepdims=True), first)

    wspec = pl.BlockSpec((G, SG_CHUNK, SG_CHUNK), lambda i: (0, 0, 0))
    bspec = pl.BlockSpec((SG_CHUNK, G), lambda i: (0, 0))
    return pl.pallas_call(
        body, name=name,
        out_shape=(jax.ShapeDtypeStruct((T, C2), BF16), jax.ShapeDtypeStruct((G, SG_CHUNK, SG_CHUNK), F32),
                   jax.ShapeDtypeStruct((SG_CHUNK, G), F32), jax.ShapeDtypeStruct((1, C), F32),
                   jax.ShapeDtypeStruct((1, C), F32)),
        grid=(T // (SG_STEP * SG_CHUNK),),
        in_specs=[_row_spec(C, SG_STEP * SG_CHUNK), _row_spec(C2, SG_STEP * SG_CHUNK), _vec_spec(C), _vec_spec(C), wspec,
                  bspec],
        out_specs=(_row_spec(C2, SG_STEP * SG_CHUNK), wspec, bspec, _vec_spec(C), _vec_spec(C)),
        compiler_params=_params(("arbitrary",)),
    )(dp, zb, ln_g.reshape(1, C), ln_b.reshape(1, C), w_s, b_s_t)


ATT_UNROLL = 3


def _att_unroll(trips):
    return 5 if trips % 5 == 0 else ATT_UNROLL


def _each_class(d, fn):
    if d == 1:
        fn(0)
    else:
        def step(c, _):
            fn(c)
            return 0

        lax.fori_loop(0, d, step, 0, unroll=ATT_UNROLL + 1 if d % (ATT_UNROLL + 1) == 0 and d > ATT_UNROLL + 1 else 1)


def _pair_bias(b):
    off = jnp.full((ATT_BLOCK, ATT_BLOCK), MASK_VALUE, F32)
    return jnp.concatenate([jnp.concatenate([b, off], axis=1), jnp.concatenate([off, b], axis=1)], axis=0)


def _attn_fwd(qkv_g, bias, *, name):
    B, d, L, _ = qkv_g.shape
    nb = L // ATT_BLOCK
    GCOL = GROUP_COLS

    def body(qkv_all, bias_ref, o_all, lse_all):
        if nb == 1 and d % 2 == 0:
            two_classes(qkv_all, bias_ref, o_all, lse_all)
        else:
            _each_class(d, lambda c: one_class(qkv_all.at[c], bias_ref, o_all.at[c], lse_all.at[c]))

    def two_classes(qkv_all, bias_ref, o_all, lse_all):
        bias2 = [_pair_bias(bias_ref[h, :, ATT_BLOCK:2 * ATT_BLOCK]) for h in range(HEADS_PER_GROUP)]

        def pair(p, _):
            two = pl.ds(2 * p, 2)
            x = qkv_all[two].reshape(2 * ATT_BLOCK, 3 * GCOL)
            qb, kb, vb = x[:, 0:GCOL], x[:, GCOL:2 * GCOL], x[:, 2 * GCOL:3 * GCOL]
            outs, lses = [], []
            for h in range(HEADS_PER_GROUP):
                sl = slice(h * HEAD_DIM, (h + 1) * HEAD_DIM)
                s = lax.dot_general(qb[:, sl], kb[:, sl], _NT, preferred_element_type=F32) * ATT_SCALE + bias2[h]
                m = jnp.max(s, axis=1, keepdims=True)
                e = jnp.exp(s - m)
                ssum = jnp.sum(e, axis=1, keepdims=True)
                outs.append(jnp.dot(e.astype(BF16), vb[:, sl], preferred_element_type=F32) / ssum)
                lses.append(jnp.broadcast_to(m + jnp.log(ssum), (2 * ATT_BLOCK, HEAD_DIM)))
            o_all[two] = jnp.concatenate(outs, axis=1).reshape(2, ATT_BLOCK, GCOL)
            lse_all[two] = jnp.concatenate(lses, axis=1).reshape(2, ATT_BLOCK, GCOL)
            return 0

        lax.fori_loop(0, d // 2, pair, 0, unroll=2)

    def one_class(qkv_ref, bias_ref, o_ref, lse_ref):
        def blk(r0, first):
            nk = ATT_BLOCK if first else 2 * ATT_BLOCK
            k0 = r0 if first else r0 - ATT_BLOCK
            qb = qkv_ref[pl.ds(r0, ATT_BLOCK), 0:GCOL]
            kb = qkv_ref[pl.ds(k0, nk), GCOL:2 * GCOL]
            vb = qkv_ref[pl.ds(k0, nk), 2 * GCOL:3 * GCOL]
            outs, lses = [], []
            for h in range(HEADS_PER_GROUP):
                sl = slice(h * HEAD_DIM, (h + 1) * HEAD_DIM)
                bh = bias_ref[h, :, ATT_BLOCK:2 * ATT_BLOCK] if first else bias_ref[h]
                s = lax.dot_general(qb[:, sl], kb[:, sl], _NT, preferred_element_type=F32) * ATT_SCALE + bh
                m = jnp.max(s, axis=1, keepdims=True)
                e = jnp.exp(s - m)
                ssum = jnp.sum(e, axis=1, keepdims=True)
                outs.append(jnp.dot(e.astype(BF16), vb[:, sl], preferred_element_type=F32) / ssum)
                lses.append(jnp.broadcast_to(m + jnp.log(ssum), (ATT_BLOCK, HEAD_DIM)))
            o_ref[pl.ds(r0, ATT_BLOCK), :] = jnp.concatenate(outs, axis=1)
            lse_ref[pl.ds(r0, ATT_BLOCK), :] = jnp.concatenate(lses, axis=1)

        blk(0, True)
        if nb > 1:
            def loop(n, _):
                blk(pl.multiple_of(n * ATT_BLOCK, ATT_BLOCK), False)
                return 0

            lax.fori_loop(1, nb, loop, 0, unroll=ATT_UNROLL)

    spec = lambda cols: pl.BlockSpec((None, d, L, cols), lambda b: (b, 0, 0, 0))
    oshape = jax.ShapeDtypeStruct((B, d, L, GCOL), F32)
    return pl.pallas_call(
        body, name=name, out_shape=(oshape, oshape), grid=(B,),
        in_specs=[spec(3 * GCOL), pl.BlockSpec((HEADS_PER_GROUP, ATT_BLOCK, 2 * ATT_BLOCK), lambda b: (0, 0, 0))],
        out_specs=(spec(GCOL), spec(GCOL)), compiler_params=_params(("parallel",)),
    )(qkv_g, bias)


def _attn_bwd(qkv_g, doc_g, lse_g, dd_g, bias, *, name):
    B, d, L, _ = qkv_g.shape
    nb = L // ATT_BLOCK
    GCOL = GROUP_COLS

    def body(qkv_all, doc_all, lse_all, dd_all, bias_ref, dqkv_all, dbias_ref, dk_ref, dv_ref):
        @pl.when(pl.program_id(0) == 0)
        def _():
            dbias_ref[...] = jnp.zeros_like(dbias_ref)

        if nb == 1 and d % 2 == 0:
            two_classes(qkv_all, doc_all, lse_all, dd_all, bias_ref, dqkv_all, dbias_ref)
        else:
            _each_class(d, lambda c: one_class(qkv_all.at[c], doc_all.at[c], lse_all.at[c], dd_all.at[c], bias_ref,
                                               dqkv_all.at[c], dbias_ref, dk_ref, dv_ref))

    def two_classes(qkv_all, doc_all, lse_all, dd_all, bias_ref, dqkv_all, dbias_ref):
        bias2 = [_pair_bias(bias_ref[h, :, ATT_BLOCK:2 * ATT_BLOCK]) for h in range(HEADS_PER_GROUP)]
        A = ATT_BLOCK

        def pair(p, _):
            two = pl.ds(2 * p, 2)
            x = qkv_all[two].reshape(2 * A, 3 * GCOL)
            qb, kb, vb = x[:, 0:GCOL], x[:, GCOL:2 * GCOL], x[:, 2 * GCOL:3 * GCOL]
            dob = doc_all[two].reshape(2 * A, GCOL).astype(BF16)
            lse = lse_all[two].reshape(2 * A, GCOL)
            dd = dd_all[two].reshape(2 * A, GCOL)
            dqs, dks, dvs = [], [], []
            for h in range(HEADS_PER_GROUP):
                sl = slice(h * HEAD_DIM, (h + 1) * HEAD_DIM)
                c0 = h * HEAD_DIM
                s = lax.dot_general(qb[:, sl], kb[:, sl], _NT, preferred_element_type=F32) * ATT_SCALE + bias2[h]
                pr = jnp.exp(s - lse[:, c0:c0 + 1])
                dp = lax.dot_general(dob[:, sl], vb[:, sl], _NT, preferred_element_type=F32)
                ds = pr * (dp - dd[:, c0:c0 + 1])
                dbias_ref[h, :, A:2 * A] += ds[0:A, 0:A] + ds[A:2 * A, A:2 * A]
                dsb = ds.astype(BF16)
                dqs.append(jnp.dot(dsb, kb[:, sl], preferred_element_type=F32) * ATT_SCALE)
                dks.append(lax.dot_general(dsb, qb[:, sl], _TN, preferred_element_type=F32) * ATT_SCALE)
                dvs.append(lax.dot_general(pr.astype(BF16), dob[:, sl], _TN, preferred_element_type=F32))
            dqkv_all[two] = jnp.concatenate(dqs + dks + dvs, axis=1).astype(BF16).reshape(2, A, 3 * GCOL)
            return 0

        lax.fori_loop(0, d // 2, pair, 0, unroll=2)

    def one_class(qkv_ref, doc_ref, lse_ref, dd_ref, bias_ref, dqkv_ref, dbias_ref, dk_ref, dv_ref):
        dk_ref[...] = jnp.zeros_like(dk_ref)
        dv_ref[...] = jnp.zeros_like(dv_ref)

        def blk(r0, first):
            nk = ATT_BLOCK if first else 2 * ATT_BLOCK
            k0 = r0 if first else r0 - ATT_BLOCK
            rows = pl.ds(r0, ATT_BLOCK)
            krows = pl.ds(k0, nk)
            qb = qkv_ref[rows, 0:GCOL]
            kb = qkv_ref[krows, GCOL:2 * GCOL]
            vb = qkv_ref[krows, 2 * GCOL:3 * GCOL]
            dob = doc_ref[rows, :].astype(BF16)
            lse = lse_ref[rows, :]
            dd = dd_ref[rows, :]
            dqs, dks, dvs = [], [], []
            for h in range(HEADS_PER_GROUP):
                sl = slice(h * HEAD_DIM, (h + 1) * HEAD_DIM)
                c0 = h * HEAD_DIM
                bh = bias_ref[h, :, ATT_BLOCK:2 * ATT_BLOCK] if first else bias_ref[h]
                s = lax.dot_general(qb[:, sl], kb[:, sl], _NT, preferred_element_type=F32) * ATT_SCALE + bh
                p = jnp.exp(s - lse[:, c0:c0 + 1])
                dp = lax.dot_general(dob[:, sl], vb[:, sl], _NT, preferred_element_type=F32)
                ds = p * (dp - dd[:, c0:c0 + 1])
                if first:
                    dbias_ref[h, :, ATT_BLOCK:2 * ATT_BLOCK] += ds
                else:
                    dbias_ref[h] += ds
                dsb = ds.astype(BF16)
                dqs.append(jnp.dot(dsb, kb[:, sl], preferred_element_type=F32) * ATT_SCALE)
                dks.append(lax.dot_general(dsb, qb[:, sl], _TN, preferred_element_type=F32) * ATT_SCALE)
                dvs.append(lax.dot_general(p.astype(BF16), dob[:, sl], _TN, preferred_element_type=F32))
            dqkv_ref[rows, 0:GCOL] = jnp.concatenate(dqs, axis=1).astype(BF16)
            dk_ref[krows, :] += jnp.concatenate(dks, axis=1)
            dv_ref[krows, :] += jnp.concatenate(dvs, axis=1)

        blk(0, True)
        if nb > 1:
            def loop(n, _):
                blk(pl.multiple_of(n * ATT_BLOCK, ATT_BLOCK), False)
                return 0

            lax.fori_loop(1, nb, loop, 0, unroll=_att_unroll(nb - 1))
        dqkv_ref[:, GCOL:2 * GCOL] = dk_ref[...].astype(BF16)
        dqkv_ref[:, 2 * GCOL:3 * GCOL] = dv_ref[...].astype(BF16)

    spec = lambda cols: pl.BlockSpec((None, d, L, cols), lambda b: (b, 0, 0, 0))
    bspec = pl.BlockSpec((HEADS_PER_GROUP, ATT_BLOCK, 2 * ATT_BLOCK), lambda b: (0, 0, 0))
    return pl.pallas_call(
        body, name=name,
        out_shape=(jax.ShapeDtypeStruct((B, d, L, 3 * GCOL), BF16),
                   jax.ShapeDtypeStruct((HEADS_PER_GROUP, ATT_BLOCK, 2 * ATT_BLOCK), F32)),
        grid=(B,),
        in_specs=[spec(3 * GCOL), spec(GCOL), spec(GCOL), spec(GCOL), bspec],
        out_specs=(spec(3 * GCOL), bspec),
        scratch_shapes=[pltpu.VMEM((L, GCOL), F32), pltpu.VMEM((L, GCOL), F32)],
        compiler_params=_params(("arbitrary",)),
    )(qkv_g, doc_g, lse_g, dd_g, bias)


def _attn_combine(os_, lses, *, name):
    T, GC = os_[0].shape
    n = len(os_)

    def body(*refs):
        o_refs, l_refs, oc_ref, lt_ref = refs[:n], refs[n:2 * n], refs[2 * n], refs[2 * n + 1]
        ls = [r[...] for r in l_refs]
        m = functools.reduce(jnp.maximum, ls)
        ws = [jnp.exp(l - m) for l in ls]
        tot = functools.reduce(jnp.add, ws)
        acc = functools.reduce(jnp.add, [w * r[...] for w, r in zip(ws, o_refs)])
        oc_ref[...] = acc / tot
        lt_ref[...] = m + jnp.log(tot)

    shp = jax.ShapeDtypeStruct((T, GC), F32)
    return pl.pallas_call(body, name=name, out_shape=(shp, shp), grid=(T // ROW_TILE,),
                          in_specs=[_row_spec(GC)] * (2 * n), out_specs=(_row_spec(GC), _row_spec(GC)),
                          compiler_params=_params(("parallel",)))(*os_, *lses)


def _attn_rowdot(doc, oc, *, name):
    T, GC = oc.shape

    def body(doc_ref, oc_ref, dd_ref):
        prod = doc_ref[...].astype(F32) * oc_ref[...]
        parts = []
        for h in range(GC // HEAD_DIM):
            s = jnp.sum(prod[:, h * HEAD_DIM:(h + 1) * HEAD_DIM], axis=1, keepdims=True)
            parts.append(jnp.broadcast_to(s, (ROW_TILE, HEAD_DIM)))
        dd_ref[...] = jnp.concatenate(parts, axis=1)

    return pl.pallas_call(body, name=name, out_shape=jax.ShapeDtypeStruct((T, GC), F32), grid=(T // ROW_TILE,),
                          in_specs=[_row_spec(GC), _row_spec(GC)], out_specs=_row_spec(GC),
                          compiler_params=_params(("parallel",)))(doc, oc)


def _bucket_sum(dbias, buckets, *, name):
    depth, NH = dbias.shape[:2]

    def body(db_ref, bk_ref, out_ref):
        rows = lax.broadcasted_iota(jnp.int32, (REL_BUCKETS, LANE), 0)
        cols = lax.broadcasted_iota(jnp.int32, (REL_BUCKETS, LANE), 1)

        def per_bucket(b, acc):
            for h in range(NH):
                sel = bk_ref[h // HEADS_PER_GROUP] == b
                tot = functools.reduce(jnp.add, [db_ref[l, h] for l in range(depth)])
                s = jnp.sum(jnp.where(sel, tot, 0.0))
                acc = acc + jnp.where(jnp.logical_and(rows == b, cols == h), s, 0.0)
            return acc

        out_ref[...] = lax.fori_loop(0, REL_BUCKETS, per_bucket, jnp.zeros((REL_BUCKETS, LANE), F32))

    return pl.pallas_call(body, name=name, out_shape=jax.ShapeDtypeStruct((REL_BUCKETS, LANE), F32),
                          compiler_params=pltpu.CompilerParams(vmem_limit_bytes=VMEM_LIMIT))(dbias, buckets)


def _merge_fwd(zg, bg, ya, yb, yc, *, name):
    T, D3 = zg.shape
    D = D3 // 3

    def body(zg_ref, bg_ref, ya_ref, yb_ref, yc_ref, out_ref):
        acc = None
        for i, y_ref in enumerate((ya_ref, yb_ref, yc_ref)):
            g = _sigmoid(zg_ref[:, i * D:(i + 1) * D].astype(F32) + bg_ref[:, i * D:(i + 1) * D])
            t = g * y_ref[...].astype(F32)
            acc = t if acc is None else acc + t
        out_ref[...] = acc.astype(BF16)

    return pl.pallas_call(body, name=name, out_shape=jax.ShapeDtypeStruct((T, D), BF16), grid=(T // ROW_TILE,),
                          in_specs=[_row_spec(D3), _vec_spec(D3), _row_spec(D), _row_spec(D), _row_spec(D)],
                          out_specs=_row_spec(D), compiler_params=_params(("parallel",)))(zg, bg.reshape(1, D3), ya, yb, yc)


def _merge_bwd(dm, zg, bg, ya, yb, yc, *, name):
    T, D3 = zg.shape
    D = D3 // 3

    def body(dm_ref, zg_ref, bg_ref, ya_ref, yb_ref, yc_ref, dya_ref, dyb_ref, dyc_ref, dzg_ref, dbg_ref):
        first = pl.program_id(0) == 0
        dm_v = dm_ref[...].astype(F32)
        dbs = []
        for i, (y_ref, dy_ref) in enumerate(((ya_ref, dya_ref), (yb_ref, dyb_ref), (yc_ref, dyc_ref))):
            g = _sigmoid(zg_ref[:, i * D:(i + 1) * D].astype(F32) + bg_ref[:, i * D:(i + 1) * D])
            dy_ref[...] = (dm_v * g).astype(BF16)
            dz = dm_v * y_ref[...].astype(F32) * g * (1.0 - g)
            dzg_ref[:, i * D:(i + 1) * D] = dz.astype(BF16)
            dbs.append(jnp.sum(dz, axis=0, keepdims=True))
        _acc_out(dbg_ref, jnp.concatenate(dbs, axis=1), first)

    bshape = jax.ShapeDtypeStruct((T, D), BF16)
    return pl.pallas_call(
        body, name=name,
        out_shape=(bshape, bshape, bshape, jax.ShapeDtypeStruct((T, D3), BF16), jax.ShapeDtypeStruct((1, D3), F32)),
        grid=(T // ROW_TILE,),
        in_specs=[_row_spec(D), _row_spec(D3), _vec_spec(D3), _row_spec(D), _row_spec(D), _row_spec(D)],
        out_specs=(_row_spec(D), _row_spec(D), _row_spec(D), _row_spec(D3), _vec_spec(D3)),
        compiler_params=_params(("arbitrary",)))(dm, zg, bg.reshape(1, D3), ya, yb, yc)


XQ_TILE = 1024


def _xattn_fwd(q, kv, *, B, S, name):
    D = q.shape[1]
    M = kv.shape[0] // B
    E = D // X_HEADS
    scale = E ** -0.5

    def body(q_ref, kv_ref, o_ref):
        outs = []
        for h in range(X_HEADS):
            s = lax.dot_general(q_ref[:, h * E:(h + 1) * E], kv_ref[:, h * E:(h + 1) * E], _NT,
                                preferred_element_type=F32) * scale
            e = jnp.exp(s - jnp.max(s, axis=1, keepdims=True))
            p = e / jnp.sum(e, axis=1, keepdims=True)
            outs.append(jnp.dot(p.astype(BF16), kv_ref[:, D + h * E:D + (h + 1) * E], preferred_element_type=F32))
        o_ref[...] = jnp.concatenate(outs, axis=1).astype(BF16)

    nq = S // XQ_TILE
    return pl.pallas_call(
        body, name=name, out_shape=jax.ShapeDtypeStruct((B * S, D), BF16), grid=(B, nq),
        in_specs=[pl.BlockSpec((XQ_TILE, D), lambda b, i: (b * nq + i, 0)), pl.BlockSpec((M, 2 * D), lambda b, i: (b, 0))],
        out_specs=pl.BlockSpec((XQ_TILE, D), lambda b, i: (b * nq + i, 0)),
        compiler_params=_params(("parallel", "parallel")))(q, kv)


def _xattn_bwd(q, kv, do, *, B, S, name):
    D = q.shape[1]
    M = kv.shape[0] // B
    E = D // X_HEADS
    scale = E ** -0.5

    def body(q_ref, kv_ref, do_ref, dq_ref, dkv_ref):
        first = pl.program_id(1) == 0
        dqs, dks, dvs = [], [], []
        for h in range(X_HEADS):
            qh = q_ref[:, h * E:(h + 1) * E]
            kh = kv_ref[:, h * E:(h + 1) * E]
            vh = kv_ref[:, D + h * E:D + (h + 1) * E]
            doh = do_ref[:, h * E:(h + 1) * E]
            s = lax.dot_general(qh, kh, _NT, preferred_element_type=F32) * scale
            e = jnp.exp(s - jnp.max(s, axis=1, keepdims=True))
            p = e / jnp.sum(e, axis=1, keepdims=True)
            dp = lax.dot_general(doh, vh, _NT, preferred_element_type=F32)
            ds = (p * (dp - jnp.sum(p * dp, axis=1, keepdims=True))).astype(BF16)
            dqs.append(jnp.dot(ds, kh, preferred_element_type=F32) * scale)
            dks.append(lax.dot_general(ds, qh, _TN, preferred_element_type=F32) * scale)
            dvs.append(lax.dot_general(p.astype(BF16), doh, _TN, preferred_element_type=F32))
        dq_ref[...] = jnp.concatenate(dqs, axis=1).astype(BF16)
        _acc_out(dkv_ref, jnp.concatenate(dks + dvs, axis=1), first)

    nq = S // XQ_TILE
    qspec = pl.BlockSpec((XQ_TILE, D), lambda b, i: (b * nq + i, 0))
    kvspec = pl.BlockSpec((M, 2 * D), lambda b, i: (b, 0))
    return pl.pallas_call(
        body, name=name,
        out_shape=(jax.ShapeDtypeStruct((B * S, D), BF16), jax.ShapeDtypeStruct((B * M, 2 * D), F32)),
        grid=(B, nq), in_specs=[qspec, kvspec, qspec], out_specs=(qspec, kvspec),
        compiler_params=_params(("arbitrary", "arbitrary")))(q, kv, do)


FFN_COLS = 256
FFN_PAD = 8
FFN_CH = 256
FFN_K = 3


def _ffn_gate(win, w_ref, cb_ref, n):
    g = jnp.broadcast_to(cb_ref[...], (n, win.shape[1]))
    for j in range(FFN_K):
        off = FFN_PAD - (FFN_K - 1) + j
        g = g + w_ref[j:j + 1, :] * win[off:off + n, :]
    return g


def _ffn_interleave(w):
    lead, n2 = w.shape[:-1], w.shape[-1]
    nc = n2 // (2 * FFN_COLS)
    return jnp.swapaxes(w.reshape(lead + (2, nc, FFN_COLS)), -3, -2).reshape(lead + (n2,))


def _ffn_deinterleave(w):
    lead, n2 = w.shape[:-1], w.shape[-1]
    nc = n2 // (2 * FFN_COLS)
    return jnp.swapaxes(w.reshape(lead + (nc, 2, FFN_COLS)), -3, -2).reshape(lead + (n2,))


def _ffn_fwd(up, conv_w, conv_b, *, B, S, name):
    F = conv_w.shape[1]
    nc = F // FFN_COLS

    def body(gp_ref, val_ref, w_ref, cb_ref, act_ref, pad_ref):
        pad_ref[0:FFN_PAD, :] = jnp.zeros((FFN_PAD, FFN_COLS), F32)
        pad_ref[FFN_PAD:, :] = gp_ref[...].astype(F32)

        def chunk(i, _):
            base = pl.multiple_of(i * FFN_CH, FFN_CH)
            gate = _ffn_gate(pad_ref[pl.ds(base, FFN_CH + FFN_PAD), :], w_ref, cb_ref, FFN_CH)
            act_ref[pl.ds(base, FFN_CH), :] = (_gelu(gate) * val_ref[pl.ds(base, FFN_CH), :].astype(F32)).astype(BF16)
            return 0

        lax.fori_loop(0, S // FFN_CH, chunk, 0)

    return pl.pallas_call(
        body, name=name, out_shape=jax.ShapeDtypeStruct((B * S, F), BF16), grid=(B, nc),
        in_specs=[pl.BlockSpec((S, FFN_COLS), lambda b, j: (b, 2 * j)), pl.BlockSpec((S, FFN_COLS), lambda b, j: (b, 2 * j + 1)),
                  pl.BlockSpec((FFN_K, FFN_COLS), lambda b, j: (0, j)), pl.BlockSpec((1, FFN_COLS), lambda b, j: (0, j))],
        out_specs=pl.BlockSpec((S, FFN_COLS), lambda b, j: (b, j)),
        scratch_shapes=[pltpu.VMEM((S + FFN_PAD, FFN_COLS), F32)],
        compiler_params=_params(("parallel", "parallel")))(up, up, conv_w, conv_b.reshape(1, F))


def _ffn_bwd(dact, up, conv_w, conv_b, *, B, S, name):
    F = conv_w.shape[1]
    nc = F // FFN_COLS
    SUB = 8

    def body(dact_ref, gp_ref, val_ref, w_ref, cb_ref, dup_ref, dw_ref, dcb_ref, pad_ref, pad2_ref, acc_ref):
        first = pl.program_id(1) == 0
        pad_ref[0:FFN_PAD, :] = jnp.zeros((FFN_PAD, FFN_COLS), F32)
        pad_ref[FFN_PAD:, :] = gp_ref[...].astype(F32)
        pad2_ref[S:, :] = jnp.zeros((FFN_PAD, FFN_COLS), F32)
        acc_ref[...] = jnp.zeros_like(acc_ref)

        def chunk1(i, _):
            base = pl.multiple_of(i * FFN_CH, FFN_CH)
            rows = pl.ds(base, FFN_CH)
            gate = _ffn_gate(pad_ref[pl.ds(base, FFN_CH + FFN_PAD), :], w_ref, cb_ref, FFN_CH)
            gl, dgl = _gelu_and_grad(gate)
            da = dact_ref[rows, :].astype(F32)
            dup_ref[rows, FFN_COLS:2 * FFN_COLS] = (da * gl).astype(BF16)
            pad2_ref[rows, :] = da * val_ref[rows, :].astype(F32) * dgl
            return 0

        lax.fori_loop(0, S // FFN_CH, chunk1, 0)

        def chunk2(i, _):
            base = pl.multiple_of(i * FFN_CH, FFN_CH)
            rows = pl.ds(base, FFN_CH)
            win2 = pad2_ref[pl.ds(base, FFN_CH + FFN_PAD), :]
            win1 = pad_ref[pl.ds(base, FFN_CH + FFN_PAD), :]
            dg = win2[0:FFN_CH, :]
            dgp = jnp.zeros((FFN_CH, FFN_COLS), F32)
            for j in range(FFN_K):
                off2 = FFN_K - 1 - j
                dgp = dgp + w_ref[j:j + 1, :] * win2[off2:off2 + FFN_CH, :]
                off1 = FFN_PAD - (FFN_K - 1) + j
                prod = dg * win1[off1:off1 + FFN_CH, :]
                acc_ref[j * SUB:(j + 1) * SUB, :] += jnp.sum(prod.reshape(FFN_CH // SUB, SUB, FFN_COLS), axis=0)
            acc_ref[FFN_K * SUB:(FFN_K + 1) * SUB, :] += jnp.sum(dg.reshape(FFN_CH // SUB, SUB, FFN_COLS), axis=0)
            dup_ref[rows, 0:FFN_COLS] = dgp.astype(BF16)
            return 0

        lax.fori_loop(0, S // FFN_CH, chunk2, 0)
        sums = jnp.sum(acc_ref[...].reshape(FFN_K + 1, SUB, FFN_COLS), axis=1)
        _acc_out(dw_ref, sums[0:FFN_K, :], first)
        _acc_out(dcb_ref, sums[FFN_K:FFN_K + 1, :], first)

    return pl.pallas_call(
        body, name=name,
        out_shape=(jax.ShapeDtypeStruct((B * S, 2 * F), BF16),
                   jax.ShapeDtypeStruct((FFN_K, F), F32), jax.ShapeDtypeStruct((1, F), F32)),
        grid=(nc, B),
        in_specs=[pl.BlockSpec((S, FFN_COLS), lambda j, b: (b, j)), pl.BlockSpec((S, FFN_COLS), lambda j, b: (b, 2 * j)),
                  pl.BlockSpec((S, FFN_COLS), lambda j, b: (b, 2 * j + 1)),
                  pl.BlockSpec((FFN_K, FFN_COLS), lambda j, b: (0, j)), pl.BlockSpec((1, FFN_COLS), lambda j, b: (0, j))],
        out_specs=(pl.BlockSpec((S, 2 * FFN_COLS), lambda j, b: (b, j)),
                   pl.BlockSpec((FFN_K, FFN_COLS), lambda j, b: (0, j)), pl.BlockSpec((1, FFN_COLS), lambda j, b: (0, j))),
        scratch_shapes=[pltpu.VMEM((S + FFN_PAD, FFN_COLS), F32), pltpu.VMEM((S + FFN_PAD, FFN_COLS), F32),
                        pltpu.VMEM(((FFN_K + 1) * SUB, FFN_COLS), F32)],
        compiler_params=_params(("arbitrary", "arbitrary")))(dact, up, up, conv_w, conv_b.reshape(1, F))


def _row_tile(rows, row_bytes, budget):
    tr = rows
    if rows * row_bytes > budget:
        for t in range(16, rows, 16):
            if rows % t == 0 and t * row_bytes <= budget:
                tr = t
    return tr


def _adamw(w, parts, m, v, *, name, own=None, own_slot=None):
    shape = w.shape
    L = len(parts)
    n = parts[0].shape[0]
    cols = shape[-1]
    rows = w.size // (cols * L)
    w3, m3, v3 = (t.reshape(L, rows, cols) for t in (w, m, v))
    tr = _row_tile(rows, cols * 4, 1 << 19)
    c1 = 1.0 - ADAM_B1 ** ADAM_STEP
    c2 = 1.0 - ADAM_B2 ** ADAM_STEP
    has_own = own is not None

    def body(*refs):
        refs = list(refs)
        slot_ref = refs.pop(0) if has_own else None
        w_ref = refs.pop(0)
        p_refs = [refs.pop(0) for _ in range(L)]
        o_refs = [refs.pop(0) for _ in range(L)] if has_own else None
        m_ref, v_ref, g_ref, d_ref, mo_ref, vo_ref = refs
        layer = pl.program_id(0)
        gv = None
        for j in range(L):
            gj = None
            for i in range(n):
                t = p_refs[j][i].astype(F32)
                if has_own:
                    t = jnp.where(slot_ref[0] == i, o_refs[j][...].astype(F32), t)
                gj = t if gj is None else gj + t
            gv = gj if gv is None else jnp.where(layer == j, gj, gv)
        g_ref[...] = gv
        mn = ADAM_B1 * m_ref[...] + (1.0 - ADAM_B1) * gv
        vn = ADAM_B2 * v_ref[...] + (1.0 - ADAM_B2) * (gv * gv)
        d_ref[...] = -ADAM_LR * ((mn / c1) / (jnp.sqrt(vn / c2) + ADAM_EPS) + ADAM_WD * w_ref[...])
        mo_ref[...] = mn
        vo_ref[...] = vn

    spec = pl.BlockSpec((None, tr, cols), lambda l, i, *_: (l, i, 0))

    def pspec(j):
        return pl.BlockSpec((n, tr, cols), lambda l, i, *_: (0, jnp.where(l == j, i, 0), 0))

    def ospec(j):
        return pl.BlockSpec((None, tr, cols), lambda l, i, slot: (slot[0], jnp.where(l == j, i, 0), 0))

    oshape = jax.ShapeDtypeStruct((L, rows, cols), F32)
    in_specs = [spec] + [pspec(j) for j in range(L)] + ([ospec(j) for j in range(L)] if has_own else []) + [spec, spec]
    args = [w3] + [p.reshape(n, rows, cols) for p in parts]
    if has_own:
        args += [o.reshape(n, rows, cols) for o in own]
    args += [m3, v3]
    grid = (L, rows // tr)
    if has_own:
        grid_spec = pltpu.PrefetchScalarGridSpec(num_scalar_prefetch=1, grid=grid, in_specs=in_specs,
                                                 out_specs=(spec,) * 4)
        outs = pl.pallas_call(body, name=name, out_shape=(oshape,) * 4, grid_spec=grid_spec,
                              compiler_params=_params(("parallel", "parallel")))(own_slot, *args)
    else:
        outs = pl.pallas_call(body, name=name, out_shape=(oshape,) * 4, grid=grid, in_specs=in_specs,
                              out_specs=(spec,) * 4, compiler_params=_params(("parallel", "parallel")))(*args)
    return tuple(t.reshape(shape) for t in outs)


def _chip_sum(g, got, core, *, name):
    shard = got.shape[1:]
    cols = shard[-1]
    rows = math.prod(shard) // cols
    tr = _row_tile(rows, cols * 4, 1 << 20)

    def body(c_ref, g_ref, r_ref, o_ref):
        o_ref[...] = (g_ref[...].astype(F32) + r_ref[...].astype(F32)).astype(o_ref.dtype)

    blk = (None, tr, cols)
    grid_spec = pltpu.PrefetchScalarGridSpec(
        num_scalar_prefetch=1, grid=(N_CHIP, rows // tr),
        in_specs=[pl.BlockSpec(blk, lambda s, i, c: (2 * s + c[0], i, 0)), pl.BlockSpec(blk, lambda s, i, c: (s, i, 0))],
        out_specs=pl.BlockSpec(blk, lambda s, i, c: (s, i, 0)))
    out = pl.pallas_call(body, name=name, out_shape=jax.ShapeDtypeStruct((N_CHIP, rows, cols), g.dtype),
                         grid_spec=grid_spec, compiler_params=_params(("parallel", "parallel")),
                         )(core, g.reshape(N_DEV, rows, cols), got.reshape(N_CHIP, rows, cols))
    return out.reshape((N_CHIP,) + shard)


_HBM = pl.BlockSpec(memory_space=pltpu.HBM)
_MESH = pl.DeviceIdType.MESH


def _my_pos():
    return lax.axis_index("x"), lax.axis_index("y"), lax.axis_index("c")


def _flip(pos, k):
    x, y, c = pos
    fx, fy, fc = (k >> 2) & 1, (k >> 1) & 1, k & 1
    return (x ^ fx if fx else x, y ^ fy if fy else y, c ^ fc if fc else c)


def _index_of(pos):
    return 4 * pos[0] + 2 * pos[1] + pos[2]


def _all_gather(xs, *, name):
    n = len(xs)

    def body(*refs):
        x_refs, out_refs = refs[:n], refs[n:2 * n]
        send_sems, recv_sems = refs[2 * n:]
        me = _my_pos()
        sibling = _flip(me, 1)
        chips = [2, 4, 6]

        def copy(i, k, block_pos, to, from_x=False):
            blk = out_refs[i].at[_index_of(block_pos)]
            return pltpu.make_async_remote_copy(src_ref=x_refs[i] if from_x else blk, dst_ref=blk,
                                                send_sem=send_sems.at[k, i], recv_sem=recv_sems.at[k, i],
                                                device_id=to, device_id_type=_MESH)

        first = [copy(i, 1 + j, me, _flip(me, f), from_x=True) for j, f in enumerate(chips) for i in range(n)]
        first += [copy(i, 0, me, sibling, from_x=True) for i in range(n)]
        for cp in first:
            cp.start()
        passed = []
        for j, f in enumerate(chips):
            for i in range(n):
                copy(i, 1 + j, _flip(me, f), me).wait_recv()
                cp = copy(i, 4 + j, _flip(me, f), sibling)
                cp.start()
                passed.append(cp)
        for i in range(n):
            copy(i, 0, sibling, me).wait_recv()
        for j, f in enumerate(chips):
            for i in range(n):
                copy(i, 4 + j, _flip(sibling, f), me).wait_recv()
        for cp in first + passed:
            cp.wait_send()

    return pl.pallas_call(
        body, name=name, out_shape=tuple(jax.ShapeDtypeStruct((N_DEV,) + x.shape, x.dtype) for x in xs),
        in_specs=[_HBM] * n, out_specs=(_HBM,) * n,
        scratch_shapes=[pltpu.SemaphoreType.DMA((7, n)), pltpu.SemaphoreType.DMA((7, n))],
    )(*xs)


N_CHIP = 4


def _scatter_d2d(gs):
    n = len(gs)

    def make(g_refs, recv_refs, send_sems, recv_sems):
        me = _my_pos()
        sibling = _flip(me, 1)
        c = me[2]
        sends = []
        for i in range(n):
            for s in range(N_CHIP):
                sends.append(pltpu.make_async_remote_copy(
                    src_ref=g_refs[i].at[2 * s + 1 - c], dst_ref=recv_refs[i].at[s],
                    send_sem=send_sems.at[N_CHIP * i + s], recv_sem=recv_sems.at[N_CHIP * i + s],
                    device_id=sibling, device_id_type=_MESH))
        return sends, sends

    shapes = [jax.ShapeDtypeStruct((N_CHIP,) + g.shape[1:], g.dtype) for g in gs]
    return _Exchange(list(gs), shapes, N_CHIP * n, make)


def _scatter_ici(ss):
    n = len(ss)

    def make(s_refs, r_refs, send_sems, recv_sems):
        me = _my_pos()
        my_chip = 2 * me[0] + me[1]
        sends, recvs = [], []
        for k in (1, 2, 3):
            peer = _flip(me, 2 * k)
            peer_chip = 2 * peer[0] + peer[1]
            for i in range(n):
                j = 3 * i + k - 1
                sends.append(pltpu.make_async_remote_copy(
                    src_ref=s_refs[i].at[peer_chip], dst_ref=r_refs[i].at[my_chip], send_sem=send_sems.at[j],
                    recv_sem=recv_sems.at[j], device_id=peer, device_id_type=_MESH))
                recvs.append(pltpu.make_async_remote_copy(
                    src_ref=s_refs[i].at[my_chip], dst_ref=r_refs[i].at[peer_chip], send_sem=send_sems.at[j],
                    recv_sem=recv_sems.at[j], device_id=peer, device_id_type=_MESH))
        return sends, recvs

    return _Exchange(list(ss), [jax.ShapeDtypeStruct(s.shape, s.dtype) for s in ss], 3 * n, make)


def _gather_ici(shards, layers):
    n = len(shards)
    FLIPS = (2, 4, 6, 1)

    def make(x_refs, out_refs, send_sems, recv_sems):
        me = _my_pos()
        sends, recvs = [], []
        for k, f in enumerate(FLIPS):
            peer = _flip(me, f)
            for i in range(n):
                src = x_refs[i] if layers[i] is None else x_refs[i].at[layers[i]]
                j = len(FLIPS) * i + k
                sends.append(pltpu.make_async_remote_copy(
                    src_ref=src, dst_ref=out_refs[i].at[_index_of(me)], send_sem=send_sems.at[j],
                    recv_sem=recv_sems.at[j], device_id=peer, device_id_type=_MESH))
                recvs.append(pltpu.make_async_remote_copy(
                    src_ref=src, dst_ref=out_refs[i].at[_index_of(peer)], send_sem=send_sems.at[j],
                    recv_sem=recv_sems.at[j], device_id=peer, device_id_type=_MESH))
        return sends, recvs

    shapes = [jax.ShapeDtypeStruct((N_DEV,) + (x.shape if l is None else x.shape[1:]), x.dtype)
              for x, l in zip(shards, layers)]
    return _Exchange(list(shards), shapes, len(FLIPS) * n, make)


def _gather_d2d(blocks):
    n = len(blocks)
    FLIPS = (2, 4, 6)

    def make(in_refs, out_refs, send_sems, recv_sems):
        me = _my_pos()
        sibling = _flip(me, 1)
        sends, recvs = [], []
        for k, f in enumerate(FLIPS):
            for i in range(n):
                j = len(FLIPS) * i + k
                mine = out_refs[i].at[_index_of(_flip(me, f))]
                sends.append(pltpu.make_async_remote_copy(
                    src_ref=mine, dst_ref=mine, send_sem=send_sems.at[j], recv_sem=recv_sems.at[j],
                    device_id=sibling, device_id_type=_MESH))
                recvs.append(pltpu.make_async_remote_copy(
                    src_ref=mine, dst_ref=out_refs[i].at[_index_of(_flip(sibling, f))], send_sem=send_sems.at[j],
                    recv_sem=recv_sems.at[j], device_id=sibling, device_id_type=_MESH))
        return sends, recvs

    shapes = [jax.ShapeDtypeStruct(b.shape, b.dtype) for b in blocks]
    return _Exchange(list(blocks), shapes, len(FLIPS) * n, make, aliases={i: i for i in range(n)})


class _Exchange:
    def __init__(self, ins, out_shapes, n_copies, make, aliases=None):
        self.ins, self.out_shapes, self.n_copies, self.make, self.aliases = ins, out_shapes, n_copies, make, aliases or {}


def _exchange_call(ex, *, name):
    n_in, n_out = len(ex.ins), len(ex.out_shapes)

    def body(*refs):
        sends, recvs = ex.make(refs[:n_in], refs[n_in:n_in + n_out], refs[n_in + n_out], refs[n_in + n_out + 1])
        for cp in sends:
            cp.start()
        for cp in recvs:
            cp.wait_recv()
        for cp in sends:
            cp.wait_send()

    return pl.pallas_call(
        body, name=name, out_shape=tuple(ex.out_shapes), in_specs=[_HBM] * n_in, out_specs=(_HBM,) * n_out,
        scratch_shapes=[pltpu.SemaphoreType.DMA((ex.n_copies,)), pltpu.SemaphoreType.DMA((ex.n_copies,))],
        input_output_aliases=ex.aliases,
    )(*ex.ins)


class _Hosts:
    def __init__(self):
        self.plan = {}

    def add(self, name, build, done):
        assert name not in self.plan
        self.plan[name] = (build, done)

    def get(self, name):
        return self.plan[name][0]() if name in self.plan else None

    def put(self, name, outs):
        self.plan.pop(name)[1](outs)


def _t5_bucket(dist):
    n = jnp.maximum(dist, 0)
    max_exact = REL_BUCKETS // 2
    nf = jnp.maximum(n, 1).astype(F32)
    large = max_exact + (jnp.log(nf / max_exact) / math.log(REL_MAX_DIST / max_exact)
                         * (REL_BUCKETS - max_exact)).astype(jnp.int32)
    large = jnp.minimum(large, REL_BUCKETS - 1)
    return jnp.where(n < max_exact, n, large)


def _bias_tables(rel_bias):
    qi = jnp.arange(ATT_BLOCK)[:, None]
    ki = jnp.arange(2 * ATT_BLOCK)[None, :]
    rel = qi + ATT_BLOCK - ki
    out = []
    for gi, (window, dil) in enumerate(DIL_GROUPS):
        span = window // dil
        bucket = _t5_bucket(rel * dil)
        valid = (rel >= 0) & (rel <= span)
        rb = rel_bias[:, gi * HEADS_PER_GROUP:(gi + 1) * HEADS_PER_GROUP]
        tab = functools.reduce(jnp.add, [jnp.where((bucket == b)[:, :, None], rb[b], 0.0)
                                         for b in range(REL_BUCKETS)])
        tab = jnp.where(valid[:, :, None], tab, MASK_VALUE).transpose(2, 0, 1)
        out.append((tab.astype(F32), bucket.astype(jnp.int32)))
    return out


def _regroup(t, B, S, d):
    C = t.shape[-1]
    return t.reshape(B, S // d, d, C).swapaxes(1, 2)


def _ungroup(t):
    B, d, L, C = t.shape
    return t.swapaxes(1, 2).reshape(B * L * d, C)


def _group_qkv(qkv, gi):
    n = len(DIL_GROUPS) * GROUP_COLS
    return jnp.concatenate([qkv[:, j * n + gi * GROUP_COLS: j * n + (gi + 1) * GROUP_COLS] for j in range(3)], axis=1)


def _layer_fwd(l, x0, h1, mem2, W, P, tabs, next_pre_g, *, B, S, hosts=None):
    tag = f"l{l}_"
    mm = functools.partial(_mm, hosts=hosts)
    sv = {"x0": x0, "h1": h1}
    za = mm(h1, W["in_a"], out_dtype=ACT, name=tag + "mm_in_a")
    zb = mm(h1, W["in_b"], out_dtype=ACT, name=tag + "mm_in_b")
    qkv = mm(h1, W["in_c"], out_dtype=BF16, name=tag + "mm_in_c")
    zg = mm(h1, W["in_g"], out_dtype=ACT, name=tag + "mm_in_g")
    a4, a2 = _bra_fwd(za, P["conv_a_w"], P["conv_a_b"], P["ln_a_g"], P["ln_a_b"], B=B, S=S, name=tag + "bra_fwd")
    ya = mm(a4, W["a_out"], out_dtype=ACT, name=tag + "mm_a_out")
    b_s_t = P["b_s"].T
    p = _brb_fwd(zb, P["ln_b_g"], P["ln_b_b"], P["w_s"], b_s_t, name=tag + "brb_fwd")
    yb = mm(p, W["b_out"], out_dtype=ACT, name=tag + "mm_b_out")
    os_, lses, qkv_gs = [], [], []
    for gi, (_, dil) in enumerate(DIL_GROUPS):
        qkv_g = _regroup(_group_qkv(qkv, gi), B, S, dil)
        o_g, lse_g = _attn_fwd(qkv_g, tabs[gi][0], name=tag + f"attn_fwd{gi}")
        qkv_gs.append(qkv_g)
        os_.append(_ungroup(o_g))
        lses.append(_ungroup(lse_g))
    oc, lse_tot = _attn_combine(os_, lses, name=tag + "attn_combine")
    yc = mm(oc, W["c_out"], out_dtype=ACT, name=tag + "mm_c_out")
    merged = _merge_fwd(zg, P["b_gate"], ya, yb, yc, name=tag + "merge_fwd")
    y1 = mm(merged, W["mix_out"], out_dtype=ACT, name=tag + "mm_mix")
    x1, h2 = _norm_fwd(x0, y1, P["mix_post_g"], P["x_pre_g"], name=tag + "norm1")
    q = mm(h2, W["xq"], out_dtype=BF16, name=tag + "mm_xq")
    (memn,) = _norm_fwd(mem2, None, None, P["mem_g"], name=tag + "norm_mem")
    kv = mm(memn, W["xkv"], out_dtype=BF16, name=tag + "mm_xkv")
    ox = _xattn_fwd(q, kv, B=B, S=S, name=tag + "xattn_fwd")
    y2 = mm(ox, W["xo"], out_dtype=ACT, name=tag + "mm_xo")
    x2, h3 = _norm_fwd(x1, y2, P["x_post_g"], P["ffn_pre_g"], name=tag + "norm2")
    up = mm(h3, W["up"], out_dtype=ACT, name=tag + "mm_up")
    act = _ffn_fwd(up, P["conv_f_w"], P["conv_f_b"], B=B, S=S, name=tag + "ffn_fwd")
    y3 = mm(act, W["down"], out_dtype=ACT, name=tag + "mm_down")
    outs = _norm_fwd(x2, y3, P["ffn_post_g"], next_pre_g, name=tag + "norm3")
    x3 = outs[0]
    h_next = outs[1] if next_pre_g is not None else None
    sv.update(za=za, zb=zb, zg=zg, a4=a4, a2=a2, ya=ya, p=p, yb=yb, qkv_gs=qkv_gs, oc=oc, lse_tot=lse_tot, yc=yc,
              merged=merged, y1=y1, x1=x1, h2=h2, q=q, memn=memn, kv=kv, ox=ox, y2=y2, x2=x2, h3=h3, up=up, act=act,
              y3=y3, x3=x3, b_s_t=b_s_t)
    return x3, h_next, sv


def _layer_bwd(l, d, dh_next, next_pre_g, sv, mem2, W, WT, P, tabs, G, *, B, S, hosts=None):
    tag = f"l{l}_"
    mm = functools.partial(_mm, hosts=hosts)
    mm_tn = functools.partial(_mm_tn, hosts=hosts)
    if dh_next is not None:
        d, dy3, G["ffn_post_g"], dg_next = _norm_bwd(d, dh_next, sv["x3"], next_pre_g, sv["y3"], P["ffn_post_g"],
                                                     name=tag + "norm3_bwd")
    else:
        dy3, G["ffn_post_g"] = _norm_bwd(d, None, None, None, sv["y3"], P["ffn_post_g"], name=tag + "norm3_bwd")
        dg_next = None
    dact = mm(dy3, WT["down"], out_dtype=ACT, name=tag + "mm_down_dx")
    G["w_down"] = mm_tn(sv["act"], dy3, name=tag + "mm_down_dw", out_dtype=BF16)
    dup, G["conv_f_w"], G["conv_f_b"] = _ffn_bwd(dact, sv["up"], P["conv_f_w"], P["conv_f_b"], B=B, S=S,
                                                name=tag + "ffn_bwd")
    dh3 = mm(dup, WT["up"], out_dtype=ACT, name=tag + "mm_up_dx")
    G["w_up"] = _ffn_deinterleave(mm_tn(sv["h3"], dup, name=tag + "mm_up_dw"))
    d, dy2, G["x_post_g"], G["ffn_pre_g"] = _norm_bwd(d, dh3, sv["x2"], P["ffn_pre_g"], sv["y2"], P["x_post_g"],
                                                      name=tag + "norm2_bwd")
    dox = mm(dy2, WT["xo"], out_dtype=BF16, name=tag + "mm_xo_dx")
    G["w_xo"] = mm_tn(sv["ox"], dy2, name=tag + "mm_xo_dw", out_dtype=BF16)
    dq, dkv = _xattn_bwd(sv["q"], sv["kv"], dox, B=B, S=S, name=tag + "xattn_bwd")
    dh2 = mm(dq, WT["xq"], out_dtype=ACT, name=tag + "mm_xq_dx")
    G["w_xq"] = mm_tn(sv["h2"], dq, name=tag + "mm_xq_dw", out_dtype=BF16)
    G["w_xkv"] = mm_tn(sv["memn"], dkv, name=tag + "mm_xkv_dw")
    dmemn = mm(dkv, WT["xkv"], out_dtype=F32, name=tag + "mm_xkv_dx")
    (G["mem_g"],) = _norm_bwd(None, dmemn, mem2, P["mem_g"], None, None, name=tag + "norm_mem_bwd")
    d, dy1, G["mix_post_g"], G["x_pre_g"] = _norm_bwd(d, dh2, sv["x1"], P["x_pre_g"], sv["y1"], P["mix_post_g"],
                                                      name=tag + "norm1_bwd")
    dm = mm(dy1, WT["mix_out"], out_dtype=ACT, name=tag + "mm_mix_dx")
    G["w_mix_out"] = mm_tn(sv["merged"], dy1, name=tag + "mm_mix_dw", out_dtype=BF16)
    dya, dyb, dyc, dzg, dbg = _merge_bwd(dm, sv["zg"], P["b_gate"], sv["ya"], sv["yb"], sv["yc"], name=tag + "merge_bwd")
    G["b_gate"] = dbg.reshape(P["b_gate"].shape)
    da4 = mm(dya, WT["a_out"], out_dtype=ACT, name=tag + "mm_a_out_dx")
    G["w_a_out"] = mm_tn(sv["a4"], dya, name=tag + "mm_a_out_dw")
    da2, G["ln_a_g"], G["ln_a_b"], G["conv_a_b"] = _bra_bwd_ln(da4, sv["a2"], P["ln_a_g"], P["ln_a_b"], name=tag + "bra_bwd_ln")
    dza, G["conv_a_w"] = _bra_bwd_conv(da2, sv["za"], P["conv_a_w"], B=B, S=S, name=tag + "bra_bwd_conv")
    dp = mm(dyb, WT["b_out"], out_dtype=ACT, name=tag + "mm_b_out_dx")
    G["w_b_out"] = mm_tn(sv["p"], dyb, name=tag + "mm_b_out_dw")
    dzb, G["w_s"], dbs_t, G["ln_b_g"], G["ln_b_b"] = _brb_bwd(dp, sv["zb"], P["ln_b_g"], P["ln_b_b"], P["w_s"],
                                                             sv["b_s_t"], name=tag + "brb_bwd")
    G["b_s"] = dbs_t.T
    doc = mm(dyc, WT["c_out"], out_dtype=ACT, name=tag + "mm_c_out_dx")
    G["w_c_out"] = mm_tn(sv["oc"], dyc, name=tag + "mm_c_out_dw")
    dd = _attn_rowdot(doc, sv["oc"], name=tag + "attn_rowdot")
    dq_parts, dk_parts, dv_parts, dbiases = [], [], [], []
    for gi, (_, dil) in enumerate(DIL_GROUPS):
        dqkv_g, dbias = _attn_bwd(sv["qkv_gs"][gi], _regroup(doc, B, S, dil), _regroup(sv["lse_tot"], B, S, dil),
                                  _regroup(dd, B, S, dil), tabs[gi][0], name=tag + f"attn_bwd{gi}")
        t = _ungroup(dqkv_g)
        dq_parts.append(t[:, 0:GROUP_COLS])
        dk_parts.append(t[:, GROUP_COLS:2 * GROUP_COLS])
        dv_parts.append(t[:, 2 * GROUP_COLS:3 * GROUP_COLS])
        dbiases.append(dbias)
    dqkv = jnp.concatenate(dq_parts + dk_parts + dv_parts, axis=1)
    G["dbias"] = jnp.concatenate(dbiases, axis=0)
    h1 = sv["h1"]
    G["w_in"] = jnp.concatenate([mm_tn(h1, dza, name=tag + "mm_in_a_dw"), mm_tn(h1, dzb, name=tag + "mm_in_b_dw"),
                                 mm_tn(h1, dqkv, name=tag + "mm_in_c_dw"), mm_tn(h1, dzg, name=tag + "mm_in_g_dw")],
                                axis=1)
    dh1 = mm(dza, WT["in_a"], out_dtype=F32, name=tag + "mm_in_a_dx")
    dh1 = mm(dzb, WT["in_b"], out_dtype=F32, name=tag + "mm_in_b_dx", add=dh1)
    dh1 = mm(dqkv, WT["in_c"], out_dtype=F32, name=tag + "mm_in_c_dx", add=dh1)
    dh1 = mm(dzg, WT["in_g"], out_dtype=ACT, name=tag + "mm_in_g_dx", add=dh1)
    return d, dh1, dg_next


_COL_SHARDED = ("w_in", "b_gate", "conv_a_w", "w_a_out", "w_b_out", "w_c_out", "w_xkv", "w_up", "conv_f_w")
_ROW_SHARDED = ("w_mix_out", "w_xq", "w_xo", "w_down")
_SHARDED_BIG = ("w_in", "w_a_out", "w_b_out", "w_c_out", "w_mix_out", "w_xq", "w_xkv", "w_xo", "w_up", "w_down")
_SHARDED_SMALL = ("b_gate", "conv_a_w", "conv_f_w")
_REPLICATED = ("mix_pre_g", "mix_post_g", "conv_a_b", "ln_a_g", "ln_a_b", "ln_b_g", "ln_b_b", "w_s", "b_s",
               "x_pre_g", "x_post_g", "mem_g", "ffn_pre_g", "ffn_post_g", "conv_f_b")
_WEIGHTS = ('rel_bias', 'mix_pre_g', 'mix_post_g', 'w_in', 'b_gate', 'conv_a_w', 'conv_a_b', 'ln_a_g', 'ln_a_b',
            'w_a_out', 'ln_b_g', 'ln_b_b', 'w_s', 'b_s', 'w_b_out', 'w_c_out', 'w_mix_out', 'x_pre_g', 'x_post_g',
            'mem_g', 'w_xq', 'w_xkv', 'w_xo', 'ffn_pre_g', 'ffn_post_g', 'w_up', 'conv_f_w', 'conv_f_b', 'w_down')
_PACK_COLS = 1024


def _shard_axis(name):
    return 1 if name in _ROW_SHARDED else 2


def _pack(parts, dtype, row_mult):
    flat = jnp.concatenate([p.astype(dtype).reshape(-1) for p in parts])
    n = flat.shape[0]
    unit = _PACK_COLS * row_mult
    padded = -(-n // unit) * unit
    flat = jnp.pad(flat, (0, padded - n))
    return flat.reshape(padded // _PACK_COLS, _PACK_COLS)


def _join8(blocks, ax):
    t = jnp.moveaxis(blocks, 0, ax)
    shp = t.shape
    return t.reshape(shp[:ax] + (shp[ax] * shp[ax + 1],) + shp[ax + 2:])


def _split8(full, ax):
    shp = full.shape
    t = full.reshape(shp[:ax] + (N_DEV, shp[ax] // N_DEV) + shp[ax + 1:])
    return jnp.moveaxis(t, ax, 0)


_W_KEYS = {"w_a_out": "a_out", "w_b_out": "b_out", "w_c_out": "c_out", "w_mix_out": "mix_out", "w_xq": "xq",
           "w_xkv": "xkv", "w_xo": "xo", "w_up": "up", "w_down": "down"}
_IN_SPLITS = (("in_a", 0, 1024), ("in_b", 1024, 2048), ("in_c", 2048, 4352), ("in_g", 4352, 7424))


def _layer_weights(name, full_l):
    if name == "w_in":
        assert full_l.shape[1] == _IN_SPLITS[-1][2]
        W = {k: full_l[:, a:b] for k, a, b in _IN_SPLITS}
    else:
        W = {_W_KEYS[name]: _ffn_interleave(full_l) if name == "w_up" else full_l}
    return W, {k: v.T for k, v in W.items()}


def _sum_rows(x, *, name):
    n = x.shape[0]

    def body(x_ref, o_ref):
        acc = x_ref[0:1, :]
        for i in range(1, n):
            acc = acc + x_ref[i:i + 1, :]
        o_ref[...] = acc

    return pl.pallas_call(body, name=name, out_shape=jax.ShapeDtypeStruct((1, x.shape[1]), x.dtype))(x)


def kernel(x, mem, rel_bias, mix_pre_g, mix_post_g, w_in, b_gate, conv_a_w, conv_a_b, ln_a_g, ln_a_b, w_a_out, ln_b_g, ln_b_b, w_s, b_s, w_b_out, w_c_out, w_mix_out, x_pre_g, x_post_g, mem_g, w_xq, w_xkv, w_xo, ffn_pre_g, ffn_post_g, w_up, conv_f_w, conv_f_b, w_down, loss_target, m_rel_bias, m_mix_pre_g, m_mix_post_g, m_w_in, m_b_gate, m_conv_a_w, m_conv_a_b, m_ln_a_g, m_ln_a_b, m_w_a_out, m_ln_b_g, m_ln_b_b, m_w_s, m_b_s, m_w_b_out, m_w_c_out, m_w_mix_out, m_x_pre_g, m_x_post_g, m_mem_g, m_w_xq, m_w_xkv, m_w_xo, m_ffn_pre_g, m_ffn_post_g, m_w_up, m_conv_f_w, m_conv_f_b, m_w_down, v_rel_bias, v_mix_pre_g, v_mix_post_g, v_w_in, v_b_gate, v_conv_a_w, v_conv_a_b, v_ln_a_g, v_ln_a_b, v_w_a_out, v_ln_b_g, v_ln_b_b, v_w_s, v_b_s, v_w_b_out, v_w_c_out, v_w_mix_out, v_x_pre_g, v_x_post_g, v_mem_g, v_w_xq, v_w_xkv, v_w_xo, v_ffn_pre_g, v_ffn_post_g, v_w_up, v_conv_f_w, v_conv_f_b, v_w_down):
    args = locals()
    w_loc = {n: args[n] for n in _WEIGHTS}
    m_loc = {n: args["m_" + n] for n in _WEIGHTS}
    v_loc = {n: args["v_" + n] for n in _WEIGHTS}

    depth = w_in.shape[0]
    assert depth == 2
    B, S, D = x.shape
    mine = _index_of(_my_pos())
    core = lax.axis_index("c").astype(jnp.int32).reshape(1)
    my_chip = (2 * lax.axis_index("x") + lax.axis_index("y")).astype(jnp.int32).reshape(1)
    shard = {n: w_loc[n].astype(BF16) for n in _SHARDED_BIG}
    shard.update({n: w_loc[n] for n in _SHARDED_SMALL})
    Ws, WTs = [{} for _ in range(depth)], [{} for _ in range(depth)]
    small_full = {}
    hosts = _Hosts()

    def install(items, blocks):
        for (n, l), blk in zip(items, blocks):
            own = shard[n] if l is None else shard[n][l]
            sel = lax.broadcasted_iota(jnp.int32, (N_DEV,) + (1,) * own.ndim, 0) == mine
            blk = jnp.where(sel, own[None], blk)
            if l is None:
                small_full[n] = _join8(blk, _shard_axis(n))
            else:
                W, WT = _layer_weights(n, _join8(blk, _shard_axis(n) - 1))
                Ws[l].update(W)
                WTs[l].update(WT)

    def carry_gather(name, items, group, then=None):
        def done(outs):
            group.extend(zip(items, outs))
            if then is not None:
                then()
        hosts.add(name, lambda: _gather_ici([shard[n] for n, _ in items], [l for _, l in items]), done)

    def finish_gather(group, tag):
        install([it for it, _ in group], _exchange_call(_gather_d2d([b for _, b in group]), name="gather_d2d_" + tag))

    first = [("w_in", 0)] + [(n, None) for n in _SHARDED_SMALL]
    install(first, _all_gather([shard[n] if l is None else shard[n][l] for n, l in first], name="gather_first"))
    g0, g1 = [], []
    carry_gather("l0_mm_in_a", [("w_xq", 0), ("w_a_out", 0), ("w_b_out", 0), ("w_c_out", 0)], g0)
    carry_gather("l0_mm_in_b", [("w_xo", 0), ("w_mix_out", 0)], g0)
    carry_gather("l0_mm_in_c", [("w_xkv", 0), ("w_down", 0)], g0)
    carry_gather("l0_mm_in_g", [("w_up", 0)], g0, then=lambda: finish_gather(g0, "l0"))
    carry_gather("l0_mm_mix", [("w_xq", 1), ("w_mix_out", 1)], g1)
    carry_gather("l0_mm_xq", [("w_xo", 1), ("w_a_out", 1), ("w_b_out", 1), ("w_c_out", 1)], g1)
    carry_gather("l0_mm_xo", [("w_xkv", 1)], g1)
    carry_gather("l0_mm_up", [("w_in", 1), ("w_down", 1)], g1)
    carry_gather("l0_mm_down", [("w_up", 1)], g1, then=lambda: finish_gather(g1, "l1"))

    Gs = [{} for _ in range(depth)]
    sendbuf, chip_sums, parts = {}, {}, {}

    def to_send(item):
        n, l = item
        if l is None:
            g = jnp.stack([Gs[k][n].reshape(small_full[n].shape[1:]) for k in range(depth)], axis=0)
            sendbuf[item] = _split8(g, _shard_axis(n))
        else:
            sendbuf[item] = _split8(Gs[l][n], _shard_axis(n) - 1).astype(BF16)
        return sendbuf[item]

    def swapped(items, got):
        for item, r in zip(items, got):
            chip_sums[item] = _chip_sum(sendbuf[item], r, core, name=f"chip_sum_{item[0]}_{item[1]}")

    def carry_swap(name, items):
        hosts.add(name, lambda: _scatter_d2d([to_send(it) for it in items]), lambda outs: swapped(items, outs))

    def carry_scatter(name, items):
        hosts.add(name, lambda: _scatter_ici([chip_sums[it] for it in items]), lambda outs: parts.update(zip(items, outs)))

    carry_swap("l0_mm_down_dx", [(n, 1) for n in _SHARDED_BIG])
    carry_scatter("l0_mm_down_dw", [(n, 1) for n in ("w_xq", "w_xkv", "w_xo", "w_mix_out", "w_a_out", "w_b_out", "w_c_out")])
    carry_scatter("l0_mm_up_dx", [("w_in", 1)])
    carry_scatter("l0_mm_up_dw", [("w_up", 1), ("w_down", 1)])
    carry_swap("l0_mm_xo_dx", [("w_down", 0), ("w_up", 0)])
    carry_scatter("l0_mm_xq_dx", [("w_down", 0)])
    carry_swap("l0_mm_mix_dx", [("w_xo", 0), ("w_xq", 0), ("w_xkv", 0)])
    carry_swap("l0_mm_in_a_dw", [("w_mix_out", 0), ("w_a_out", 0), ("w_b_out", 0), ("w_c_out", 0)])
    carry_scatter("l0_mm_in_b_dw", [("w_mix_out", 0), ("w_a_out", 0), ("w_b_out", 0), ("w_c_out", 0)])
    carry_scatter("l0_mm_in_c_dw", [("w_xo", 0), ("w_xq", 0), ("w_xkv", 0)])
    carry_scatter("l0_mm_in_g_dw", [("w_up", 0)])
    carry_swap("l0_mm_in_a_dx", [("w_in", 0)])
    halves = {}

    def carry_scatter_half(name, item, half):
        def build():
            cs = chip_sums[item]
            k = cs.shape[1] // 2
            return _scatter_ici([cs[:, half * k:(half + 1) * k]])
        hosts.add(name, build, lambda outs: halves.__setitem__(half, outs[0]))

    carry_scatter_half("l0_mm_in_c_dx", ("w_in", 0), 0)
    carry_scatter_half("l0_mm_in_g_dx", ("w_in", 0), 1)

    rep_w = {n: w_loc[n] for n in ("rel_bias",) + _REPLICATED}
    Ps = lambda: [dict({n: rep_w[n][l] for n in _REPLICATED}, **{n: small_full[n][l] for n in _SHARDED_SMALL})
                  for l in range(depth)]
    loss_vec, grad_x, g_rel = _run_step(x, mem, loss_target, Ws, WTs, Ps(), rep_w["rel_bias"], Gs, hosts)
    assert not hosts.plan, list(hosts.plan)
    parts[("w_in", 0)] = jnp.concatenate([halves[0], halves[1]], axis=1)

    last = [(n, None) for n in _SHARDED_SMALL]
    swapped(last, _exchange_call(_scatter_d2d([to_send(it) for it in last]), name="swap_grads_last"))
    parts.update(zip(last, _exchange_call(_scatter_ici([chip_sums[it] for it in last]), name="scatter_grads_last")))

    rep = ("rel_bias",) + _REPLICATED
    rep_g = {n: jnp.stack([Gs[l][n].reshape(w_loc[n].shape[1:]) for l in range(depth)], axis=0) for n in _REPLICATED}
    rep_g["rel_bias"] = g_rel
    packed = _pack([loss_vec] + [rep_g[n] for n in rep], F32, 8)
    (allp,) = _all_gather([packed], name="gather_rep_grads")
    sel = lax.broadcasted_iota(jnp.int32, (N_DEV, 1, 1), 0) == mine
    allp = jnp.where(sel, packed[None], allp).reshape(N_DEV, -1)
    loss = _sum_rows(allp[:, :LANE], name="loss_sum")[0, 0]
    off = LANE
    rep_parts = {}
    for n in rep:
        size = math.prod(w_loc[n].shape)
        rep_parts[n] = allp[:, off:off + size].reshape((N_DEV,) + w_loc[n].shape)
        off += size

    g_loc, deltas, new_m, new_v = {}, {}, {}, {}
    for n in _WEIGHTS:
        if n in rep_parts:
            p, own = [rep_parts[n]], {}
        else:
            its = [(n, None)] if n in _SHARDED_SMALL else [(n, l) for l in range(depth)]
            p, own = [parts[it] for it in its], {"own": [chip_sums[it] for it in its], "own_slot": my_chip}
        g_loc[n], deltas[n], new_m[n], new_v[n] = _adamw(w_loc[n], p, m_loc[n], v_loc[n], name="adamw_" + n, **own)
    return (loss, grad_x, *[g_loc[n] for n in _WEIGHTS], *[deltas[n] for n in _WEIGHTS],
            *[new_m[n] for n in _WEIGHTS], *[new_v[n] for n in _WEIGHTS])


def _local_step(x, mem, loss_target, full):
    depth = full["w_in"].shape[0]
    Ws, WTs, Ps = [{} for _ in range(depth)], [{} for _ in range(depth)], []
    for l in range(depth):
        Ps.append({n: full[n][l] for n in _WEIGHTS if n != "rel_bias"})
        for n in _SHARDED_BIG:
            W, WT = _layer_weights(n, full[n][l].astype(BF16))
            Ws[l].update(W)
            WTs[l].update(WT)
    Gs = [{} for _ in range(depth)]
    loss_vec, grad_x, g_rel = _run_step(x, mem, loss_target, Ws, WTs, Ps, full["rel_bias"], Gs, None)
    grads = {"rel_bias": g_rel}
    for n in _WEIGHTS:
        if n != "rel_bias":
            grads[n] = jnp.stack([Gs[l][n].reshape(full[n].shape[1:]) for l in range(depth)], axis=0)
    return loss_vec[0, 0], grad_x, grads


def _run_step(x, mem, loss_target, Ws, WTs, Ps, rel_bias, Gs, hosts):
    B, S, D = x.shape
    depth = len(Ws)
    x2d = x.reshape(B * S, D)
    mem2 = mem.reshape(-1, D)
    tabs = _bias_tables(rel_bias)
    (h1,) = _norm_fwd(x2d, None, None, Ps[0]["mix_pre_g"], name="norm0")
    xc = x2d
    saved = []
    for l in range(depth):
        nxt = Ps[l + 1]["mix_pre_g"] if l + 1 < depth else None
        xc, h1, sv = _layer_fwd(l, xc, h1, mem2, Ws[l], Ps[l], tabs, nxt, B=B, S=S, hosts=hosts)
        saved.append(sv)
    loss_vec, d = _loss_kernel(xc, loss_target.reshape(B * S, D), name="loss")

    dh_next = None
    for l in reversed(range(depth)):
        nxt = Ps[l + 1]["mix_pre_g"] if l + 1 < depth else None
        d, dh_next, dg_next = _layer_bwd(l, d, dh_next, nxt, saved[l], mem2, Ws[l], WTs[l], Ps[l], tabs, Gs[l], B=B, S=S,
                                         hosts=hosts)
        if dg_next is not None:
            Gs[l + 1]["mix_pre_g"] = dg_next
    grad_x2d, Gs[0]["mix_pre_g"] = _norm_bwd(d, dh_next, x2d, Ps[0]["mix_pre_g"], None, None, name="norm0_bwd")

    dbias = jnp.stack([Gs[l]["dbias"] for l in range(depth)], axis=0)
    buckets = jnp.stack([t[1] for t in tabs], axis=0)
    rb = _bucket_sum(dbias, buckets, name="rel_bias_grad")
    return loss_vec, grad_x2d.reshape(B, S, D), rb[:, :rel_bias.shape[1]]
```
